```python
import math
import jax, jax.numpy as jnp
from jax import lax
import numpy as np

D_MODEL = 1024
BATCH = 8
SEQ = 8192
DEPTH = 2

N_ATTN_LAYERS = (DEPTH + 1) // 2
N_REC_LAYERS = DEPTH // 2
HEAD_DIM = 64
BLOCK = 128
A_Q_HEADS = 8
A_KV_HEADS = 2
A_WINDOW = 128
B_HEADS = 8
B_BRANCHES = ((128, 1), (512, 4), (2048, 16))
N_ATTN_HEADS = A_Q_HEADS + B_HEADS
ATTN_SPLITS = [A_Q_HEADS * HEAD_DIM, A_KV_HEADS * HEAD_DIM, A_KV_HEADS * HEAD_DIM,
               B_HEADS * HEAD_DIM, B_HEADS * HEAD_DIM, B_HEADS * HEAD_DIM]
ATTN_IN = sum(ATTN_SPLITS)
ATTN_OUT = N_ATTN_HEADS * HEAD_DIM
S5_GROUP = 16
S5_GROUPS = 16
S5_WIDTH = S5_GROUP * S5_GROUPS
S5_STATE = 64
DN_HEADS = 6
DN_DK = 128
DN_DV = 128
DN_CONV = 4
DN_CHUNK = 64
REC_SPLITS = [S5_WIDTH, DN_HEADS * DN_DK, DN_HEADS * DN_DK, DN_HEADS * DN_DV,
              DN_HEADS * DN_DV, DN_HEADS, DN_HEADS]
REC_IN = sum(REC_SPLITS)
REC_OUT = S5_WIDTH + DN_HEADS * DN_DV
D_FF = 2816
FFN_CONV = 3
EPS = 1e-6

kernel_name = "hybrid_swa_dilated_s5_deltanet_block"


def split_cols(t, sizes):
    offs = np.cumsum(sizes)[:-1]
    return jnp.split(t, [int(o) for o in offs], axis=-1)


def rms_norm(x, w):
    xf = x.astype(jnp.float32)
    y = xf * lax.rsqrt(jnp.mean(xf * xf, axis=-1, keepdims=True) + EPS)
    return (y * w.astype(jnp.float32)).astype(x.dtype)


def causal_dwconv(x, w):
    width, ch = w.shape
    xp = jnp.pad(x, ((0, 0), (width - 1, 0), (0, 0)))
    return lax.conv_general_dilated(xp, w[:, None, :].astype(x.dtype), window_strides=(1,),
                                    padding="VALID", dimension_numbers=("NWC", "WIO", "NWC"),
                                    feature_group_count=ch)


def alibi_slopes(n):
    return jnp.asarray(2.0 ** (-8.0 * np.arange(1, n + 1) / n), dtype=jnp.float32)


def banded_attention(q, k, v, slopes, step, max_dist):
    b, L, K, R, hd = q.shape
    nb = L // BLOCK
    qb = q.reshape(b, nb, BLOCK, K, R, hd)
    pad = ((0, 0), (BLOCK, 0), (0, 0), (0, 0))
    kb = jnp.pad(k, pad).reshape(b, nb + 1, BLOCK, K, hd)
    vb = jnp.pad(v, pad).reshape(b, nb + 1, BLOCK, K, hd)
    kw = jnp.concatenate([kb[:, :-1], kb[:, 1:]], axis=2)
    vw = jnp.concatenate([vb[:, :-1], vb[:, 1:]], axis=2)
    s = jnp.einsum("bnqkrd,bnskd->bnkrqs", qb, kw,
                   preferred_element_type=jnp.float32) * (hd ** -0.5)
    dist = BLOCK + jnp.arange(BLOCK)[:, None] - jnp.arange(2 * BLOCK)[None, :]
    after_start = (jnp.arange(nb)[:, None, None] > 0) | (jnp.arange(2 * BLOCK)[None, None, :] >= BLOCK)
    valid = (dist >= 0) & (dist <= max_dist) & after_start
    bias = -slopes.astype(jnp.float32)[:, :, None, None] * (step * dist).astype(jnp.float32)
    s = jnp.where(valid[None, :, None, None], s + bias, -jnp.inf)
    m = jnp.max(s, axis=-1, keepdims=True)
    p = jnp.exp(s - m)
    l = jnp.sum(p, axis=-1)
    o = jnp.einsum("bnkrqs,bnskd->bnqkrd", p.astype(v.dtype), vw,
                   preferred_element_type=jnp.float32)
    o = o / jnp.moveaxis(l, -1, 2)[..., None]
    lse = jnp.moveaxis(m[..., 0] + jnp.log(l), -1, 2)
    return o.reshape(b, L, K, R, hd).astype(q.dtype), lse.reshape(b, L, K, R)


def dilated_branch(q, k, v, slopes, window, dilation):
    b, L, H, hd = q.shape
    span = dilation * BLOCK
    Lp = -(-L // span) * span
    pad = ((0, 0), (0, Lp - L), (0, 0), (0, 0))

    def strided(t):
        return jnp.pad(t, pad).reshape(b, Lp // dilation, dilation * H, hd)

    o, lse = banded_attention(strided(q)[:, :, :, None, :], strided(k), strided(v),
                              jnp.tile(slopes, dilation)[:, None], dilation, window // dilation)
    o = o[:, :, :, 0].reshape(b, Lp, H, hd)[:, :L]
    lse = lse[..., 0].reshape(b, Lp, H)[:, :L]
    return o, lse


def attention_mixer(h, w_in, q_norm_a, k_norm_a, q_norm_b, k_norm_b, sinks, w_out):
    b, L, _ = h.shape
    rep = A_Q_HEADS // A_KV_HEADS
    qa, ka, va, qb, kb, vb = split_cols(h @ w_in, ATTN_SPLITS)
    slopes = alibi_slopes(N_ATTN_HEADS)
    qa = rms_norm(qa.reshape(b, L, A_KV_HEADS, rep, HEAD_DIM), q_norm_a)
    ka = rms_norm(ka.reshape(b, L, A_KV_HEADS, HEAD_DIM), k_norm_a)
    va = va.reshape(b, L, A_KV_HEADS, HEAD_DIM)
    oa, lse_a = banded_attention(qa, ka, va, slopes[:A_Q_HEADS].reshape(A_KV_HEADS, rep),
                                 1, A_WINDOW - 1)
    keep = jax.nn.sigmoid(lse_a - sinks.astype(jnp.float32).reshape(A_KV_HEADS, rep))
    oa = (oa.astype(jnp.float32) * keep[..., None]).reshape(b, L, A_Q_HEADS * HEAD_DIM)
    qb = rms_norm(qb.reshape(b, L, B_HEADS, HEAD_DIM), q_norm_b)
    kb = rms_norm(kb.reshape(b, L, B_HEADS, HEAD_DIM), k_norm_b)
    vb = vb.reshape(b, L, B_HEADS, HEAD_DIM)
    outs, lses = [], []
    for window, dilation in B_BRANCHES:
        o, l = dilated_branch(qb, kb, vb, slopes[A_Q_HEADS:], window, dilation)
        outs.append(o)
        lses.append(l)
    wts = jax.nn.softmax(jnp.stack(lses), axis=0)
    ob = jnp.einsum("gblh,gblhd->blhd", wts, jnp.stack(outs).astype(jnp.float32))
    ob = ob.reshape(b, L, B_HEADS * HEAD_DIM)
    return jnp.concatenate([oa, ob], axis=-1).astype(h.dtype) @ w_out


def s5_mixer(u, lam_re, lam_im, log_dt, b_re, b_im, c_re, c_im, d_skip, glu_w, glu_b):
    f32 = jnp.float32
    bsz, L, _ = u.shape
    uf = u.astype(f32).reshape(bsz, L, S5_GROUPS, S5_GROUP)
    lr, li = lam_re.astype(f32), lam_im.astype(f32)
    dt = jnp.exp(log_dt.astype(f32))[:, None]
    mag, ang = jnp.exp(lr * dt), li * dt
    ab_re, ab_im = mag * jnp.cos(ang), mag * jnp.sin(ang)
    nr, ni = ab_re - 1.0, ab_im
    den = lr * lr + li * li
    f_re = (nr * lr + ni * li) / den
    f_im = (ni * lr - nr * li) / den
    bu_re = jnp.einsum("blgi,gpi->blgp", uf, b_re.astype(f32))
    bu_im = jnp.einsum("blgi,gpi->blgp", uf, b_im.astype(f32))
    e_re = f_re * bu_re - f_im * bu_im
    e_im = f_re * bu_im + f_im * bu_re
    a_re = jnp.broadcast_to(ab_re, e_re.shape)
    a_im = jnp.broadcast_to(ab_im, e_im.shape)

    def combine(e1, e2):
        a1r, a1i, b1r, b1i = e1
        a2r, a2i, b2r, b2i = e2
        return (a2r * a1r - a2i * a1i, a2r * a1i + a2i * a1r,
                a2r * b1r - a2i * b1i + b2r, a2r * b1i + a2i * b1r + b2i)

    _, _, x_re, x_im = lax.associative_scan(combine, (a_re, a_im, e_re, e_im), axis=1)
    y = (jnp.einsum("blgp,gip->blgi", x_re, c_re.astype(f32))
         - jnp.einsum("blgp,gip->blgi", x_im, c_im.astype(f32))
         + d_skip.astype(f32).reshape(S5_GROUPS, S5_GROUP) * uf)
    g = jax.nn.gelu(y.reshape(bsz, L, S5_WIDTH))
    return (g * jax.nn.sigmoid(g @ glu_w.astype(f32) + glu_b.astype(f32))).astype(u.dtype)


def chunk_gated_delta_rule(q, k, v, g, beta):
    b, L, H, dk = q.shape
    dv = v.shape[-1]
    n, C = L // DN_CHUNK, DN_CHUNK

    def chunks(t):
        return jnp.moveaxis(t.reshape((b, n, C) + t.shape[2:]), 3, 2)

    q, k, v, g, beta = chunks(q), chunks(k), chunks(v), chunks(g), chunks(beta)
    G = jnp.cumsum(g, axis=-1)
    causal = jnp.tril(jnp.ones((C, C), bool))
    strict = jnp.tril(jnp.ones((C, C), bool), -1)
    diff = G[..., :, None] - G[..., None, :]
    gamma = jnp.where(causal, jnp.exp(jnp.where(causal, diff, 0.0)), 0.0)
    kk = jnp.einsum("bnhid,bnhjd->bnhij", k, k)
    n_mat = jnp.where(strict, beta[..., :, None] * kk * gamma, 0.0)
    rhs = jnp.concatenate([v * beta[..., None], k * (beta * jnp.exp(G))[..., None]], axis=-1)
    sol = lax.linalg.triangular_solve(n_mat + jnp.eye(C, dtype=jnp.float32), rhs,
                                      left_side=True, lower=True, unit_diagonal=True)
    u, w = sol[..., :dv], sol[..., dv:]
    qk = jnp.einsum("bnhid,bnhjd->bnhij", q, k) * gamma
    q_dec = q * jnp.exp(G)[..., None]
    k_dec = k * jnp.exp(G[..., -1:] - G)[..., None]
    g_last = jnp.exp(G[..., -1])

    def step(S, xs):
        u_c, w_c, qk_c, qd_c, kd_c, gl_c = xs
        v_new = u_c - jnp.einsum("bhcd,bhde->bhce", w_c, S)
        o = jnp.einsum("bhcd,bhde->bhce", qd_c, S) + jnp.einsum("bhij,bhje->bhie", qk_c, v_new)
        S = S * gl_c[..., None, None] + jnp.einsum("bhcd,bhce->bhde", kd_c, v_new)
        return S, o

    xs = tuple(jnp.moveaxis(t, 1, 0) for t in (u, w, qk, q_dec, k_dec, g_last))
    _, o = lax.scan(step, jnp.zeros((b, H, dk, dv), jnp.float32), xs)
    return jnp.moveaxis(jnp.moveaxis(o, 0, 1), 2, 3).reshape(b, L, H, dv)


def gated_deltanet_mixer(q, k, v, z, a, beta_raw, conv_w, a_log, dt_bias, out_norm):
    f32 = jnp.float32
    bsz, L, _ = q.shape
    qkv = jax.nn.silu(causal_dwconv(jnp.concatenate([q, k, v], axis=-1), conv_w)).astype(f32)
    q, k, v = split_cols(qkv, [DN_HEADS * DN_DK, DN_HEADS * DN_DK, DN_HEADS * DN_DV])
    q = q.reshape(bsz, L, DN_HEADS, DN_DK)
    k = k.reshape(bsz, L, DN_HEADS, DN_DK)
    v = v.reshape(bsz, L, DN_HEADS, DN_DV)
    q = q * lax.rsqrt(jnp.sum(q * q, axis=-1, keepdims=True) + EPS) * (DN_DK ** -0.5)
    k = k * lax.rsqrt(jnp.sum(k * k, axis=-1, keepdims=True) + EPS)
    beta = jax.nn.sigmoid(beta_raw.astype(f32))
    g = -jnp.exp(a_log.astype(f32)) * jax.nn.softplus(a.astype(f32) + dt_bias.astype(f32))
    o = chunk_gated_delta_rule(q, k, v, g, beta)
    o = rms_norm(o, out_norm) * jax.nn.silu(z.astype(f32).reshape(bsz, L, DN_HEADS, DN_DV))
    return o.reshape(bsz, L, DN_HEADS * DN_DV)


def recurrent_mixer(h, w_in, lam_re, lam_im, log_dt, b_re, b_im, c_re, c_im, d_skip, glu_w,
                    glu_b, dn_conv, a_log, dt_bias, out_norm, w_out):
    u, q, k, v, z, a, beta_raw = split_cols(h @ w_in, REC_SPLITS)
    yc = s5_mixer(u, lam_re, lam_im, log_dt, b_re, b_im, c_re, c_im, d_skip, glu_w, glu_b)
    yd = gated_deltanet_mixer(q, k, v, z, a, beta_raw, dn_conv, a_log, dt_bias, out_norm)
    return jnp.concatenate([yc.astype(h.dtype), yd.astype(h.dtype)], axis=-1) @ w_out


def conv_ffn(h, w_up, conv_w, w_down):
    up = causal_dwconv(h @ w_up, conv_w)
    a, b = jnp.split(up, 2, axis=-1)
    return (jax.nn.silu(a) * b) @ w_down


def modulate(x, norm_w, shift, scale):
    return rms_norm(x, norm_w) * (1.0 + scale[:, None, :]) + shift[:, None, :]


def _fwd_setup_inputs(seed: int = 0) -> dict:
    key = jax.random.key(seed)
    ks = iter(jax.random.split(key, 40))
    f32 = jnp.float32
    na, nr = N_ATTN_LAYERS, N_REC_LAYERS

    def nrm(shape, scale):
        return jax.random.normal(next(ks), shape, f32) * scale

    def gain(shape):
        return 1.0 + nrm(shape, 0.02)

    x = nrm((BATCH, SEQ, D_MODEL), 1.0)
    c = nrm((BATCH, D_MODEL), 1.0)
    ada_w = nrm((DEPTH, D_MODEL, 6 * D_MODEL), 0.5 * D_MODEL ** -0.5)
    ada_b = nrm((DEPTH, 6 * D_MODEL), 0.02)
    norm_mix = gain((DEPTH, D_MODEL))
    norm_ffn = gain((DEPTH, D_MODEL))
    attn_w_in = nrm((na, D_MODEL, ATTN_IN), D_MODEL ** -0.5)
    attn_q_norm_a = gain((na, HEAD_DIM))
    attn_k_norm_a = gain((na, HEAD_DIM))
    attn_q_norm_b = gain((na, HEAD_DIM))
    attn_k_norm_b = gain((na, HEAD_DIM))
    attn_sinks = nrm((na, A_Q_HEADS), 1.0)
    attn_w_out = nrm((na, ATTN_OUT, D_MODEL), ATTN_OUT ** -0.5)
    rec_w_in = nrm((nr, D_MODEL, REC_IN), D_MODEL ** -0.5)
    s5_lambda_re = -0.5 + nrm((nr, S5_GROUPS, S5_STATE), 0.01)
    s5_lambda_im = jnp.pi * jnp.arange(S5_STATE, dtype=f32) + nrm((nr, S5_GROUPS, S5_STATE), 0.01)
    s5_log_dt = jax.random.uniform(next(ks), (nr, S5_GROUPS), f32, math.log(1e-3), math.log(1e-1))
    s5_b_re = nrm((nr, S5_GROUPS, S5_STATE, S5_GROUP), (2 * S5_GROUP) ** -0.5)
    s5_b_im = nrm((nr, S5_GROUPS, S5_STATE, S5_GROUP), (2 * S5_GROUP) ** -0.5)
    s5_c_re = nrm((nr, S5_GROUPS, S5_GROUP, S5_STATE), S5_STATE ** -0.5)
    s5_c_im = nrm((nr, S5_GROUPS, S5_GROUP, S5_STATE), S5_STATE ** -0.5)
    s5_d = nrm((nr, S5_WIDTH), 1.0)
    s5_glu_w = nrm((nr, S5_WIDTH, S5_WIDTH), S5_WIDTH ** -0.5)
    s5_glu_b = nrm((nr, S5_WIDTH), 0.02)
    dn_conv = nrm((nr, DN_CONV, DN_HEADS * (2 * DN_DK + DN_DV)), DN_CONV ** -0.5)
    dn_a_log = jnp.log(jax.random.uniform(next(ks), (nr, DN_HEADS), f32, 1.0, 16.0))
    dt0 = jnp.exp(jax.random.uniform(next(ks), (nr, DN_HEADS), f32, math.log(1e-3), math.log(1e-1)))
    dn_dt_bias = dt0 + jnp.log(-jnp.expm1(-dt0))
    dn_out_norm = gain((nr, DN_DV))
    rec_w_out = nrm((nr, REC_OUT, D_MODEL), REC_OUT ** -0.5)
    ffn_w_up = nrm((DEPTH, D_MODEL, 2 * D_FF), D_MODEL ** -0.5)
    ffn_conv = nrm((DEPTH, FFN_CONV, 2 * D_FF), FFN_CONV ** -0.5)
    ffn_w_down = nrm((DEPTH, D_FF, D_MODEL), D_FF ** -0.5)
    return {"x": x, "c": c, "ada_w": ada_w, "ada_b": ada_b, "norm_mix": norm_mix,
            "norm_ffn": norm_ffn, "attn_w_in": attn_w_in, "attn_q_norm_a": attn_q_norm_a,
            "attn_k_norm_a": attn_k_norm_a, "attn_q_norm_b": attn_q_norm_b,
            "attn_k_norm_b": attn_k_norm_b, "attn_sinks": attn_sinks, "attn_w_out": attn_w_out,
            "rec_w_in": rec_w_in, "s5_lambda_re": s5_lambda_re, "s5_lambda_im": s5_lambda_im,
            "s5_log_dt": s5_log_dt, "s5_b_re": s5_b_re, "s5_b_im": s5_b_im, "s5_c_re": s5_c_re,
            "s5_c_im": s5_c_im, "s5_d": s5_d, "s5_glu_w": s5_glu_w, "s5_glu_b": s5_glu_b,
            "dn_conv": dn_conv, "dn_a_log": dn_a_log, "dn_dt_bias": dn_dt_bias,
            "dn_out_norm": dn_out_norm, "rec_w_out": rec_w_out, "ffn_w_up": ffn_w_up,
            "ffn_conv": ffn_conv, "ffn_w_down": ffn_w_down}


def _fwd_reference(x, c, ada_w, ada_b, norm_mix, norm_ffn, attn_w_in, attn_q_norm_a, attn_k_norm_a,
              attn_q_norm_b, attn_k_norm_b, attn_sinks, attn_w_out, rec_w_in, s5_lambda_re,
              s5_lambda_im, s5_log_dt, s5_b_re, s5_b_im, s5_c_re, s5_c_im, s5_d, s5_glu_w,
              s5_glu_b, dn_conv, dn_a_log, dn_dt_bias, dn_out_norm, rec_w_out, ffn_w_up,
              ffn_conv, ffn_w_down):
    cond = jax.nn.silu(c)
    for layer in range(DEPTH):
        mod = cond @ ada_w[layer] + ada_b[layer]
        sh1, sc1, g1, sh2, sc2, g2 = jnp.split(mod, 6, axis=-1)
        h = modulate(x, norm_mix[layer], sh1, sc1)
        i = layer // 2
        if layer % 2 == 0:
            y = attention_mixer(h, attn_w_in[i], attn_q_norm_a[i], attn_k_norm_a[i],
                                attn_q_norm_b[i], attn_k_norm_b[i], attn_sinks[i], attn_w_out[i])
        else:
            y = recurrent_mixer(h, rec_w_in[i], s5_lambda_re[i], s5_lambda_im[i], s5_log_dt[i],
                                s5_b_re[i], s5_b_im[i], s5_c_re[i], s5_c_im[i], s5_d[i],
                                s5_glu_w[i], s5_glu_b[i], dn_conv[i], dn_a_log[i], dn_dt_bias[i],
                                dn_out_norm[i], rec_w_out[i])
        x = x + g1[:, None, :] * y
        h = modulate(x, norm_ffn[layer], sh2, sc2)
        x = x + g2[:, None, :] * conv_ffn(h, ffn_w_up[layer], ffn_conv[layer], ffn_w_down[layer])
    return x


import jax as _jax
import jax.numpy as _jnp

TWIN_FORMAT = 'train_step'
FWD_PARAMS = ['x', 'c', 'ada_w', 'ada_b', 'norm_mix', 'norm_ffn', 'attn_w_in', 'attn_q_norm_a', 'attn_k_norm_a', 'attn_q_norm_b', 'attn_k_norm_b', 'attn_sinks', 'attn_w_out', 'rec_w_in', 's5_lambda_re', 's5_lambda_im', 's5_log_dt', 's5_b_re', 's5_b_im', 's5_c_re', 's5_c_im', 's5_d', 's5_glu_w', 's5_glu_b', 'dn_conv', 'dn_a_log', 'dn_dt_bias', 'dn_out_norm', 'rec_w_out', 'ffn_w_up', 'ffn_conv', 'ffn_w_down']
TWIN_WEIGHTS = ['ada_w', 'ada_b', 'norm_mix', 'norm_ffn', 'attn_w_in', 'attn_q_norm_a', 'attn_k_norm_a', 'attn_q_norm_b', 'attn_k_norm_b', 'attn_sinks', 'attn_w_out', 'rec_w_in', 's5_lambda_re', 's5_lambda_im', 's5_log_dt', 's5_b_re', 's5_b_im', 's5_c_re', 's5_c_im', 's5_d', 's5_glu_w', 's5_glu_b', 'dn_conv', 'dn_a_log', 'dn_dt_bias', 'dn_out_norm', 'rec_w_out', 'ffn_w_up', 'ffn_conv', 'ffn_w_down']
TWIN_DIFF_INPUT = 'x'
TWIN_INPUTS = ['x', 'c', 'ada_w', 'ada_b', 'norm_mix', 'norm_ffn', 'attn_w_in', 'attn_q_norm_a', 'attn_k_norm_a', 'attn_q_norm_b', 'attn_k_norm_b', 'attn_sinks', 'attn_w_out', 'rec_w_in', 's5_lambda_re', 's5_lambda_im', 's5_log_dt', 's5_b_re', 's5_b_im', 's5_c_re', 's5_c_im', 's5_d', 's5_glu_w', 's5_glu_b', 'dn_conv', 'dn_a_log', 'dn_dt_bias', 'dn_out_norm', 'rec_w_out', 'ffn_w_up', 'ffn_conv', 'ffn_w_down', 'loss_target', 'm_ada_w', 'm_ada_b', 'm_norm_mix', 'm_norm_ffn', 'm_attn_w_in', 'm_attn_q_norm_a', 'm_attn_k_norm_a', 'm_attn_q_norm_b', 'm_attn_k_norm_b', 'm_attn_sinks', 'm_attn_w_out', 'm_rec_w_in', 'm_s5_lambda_re', 'm_s5_lambda_im', 'm_s5_log_dt', 'm_s5_b_re', 'm_s5_b_im', 'm_s5_c_re', 'm_s5_c_im', 'm_s5_d', 'm_s5_glu_w', 'm_s5_glu_b', 'm_dn_conv', 'm_dn_a_log', 'm_dn_dt_bias', 'm_dn_out_norm', 'm_rec_w_out', 'm_ffn_w_up', 'm_ffn_conv', 'm_ffn_w_down', 'v_ada_w', 'v_ada_b', 'v_norm_mix', 'v_norm_ffn', 'v_attn_w_in', 'v_attn_q_norm_a', 'v_attn_k_norm_a', 'v_attn_q_norm_b', 'v_attn_k_norm_b', 'v_attn_sinks', 'v_attn_w_out', 'v_rec_w_in', 'v_s5_lambda_re', 'v_s5_lambda_im', 'v_s5_log_dt', 'v_s5_b_re', 'v_s5_b_im', 'v_s5_c_re', 'v_s5_c_im', 'v_s5_d', 'v_s5_glu_w', 'v_s5_glu_b', 'v_dn_conv', 'v_dn_a_log', 'v_dn_dt_bias', 'v_dn_out_norm', 'v_rec_w_out', 'v_ffn_w_up', 'v_ffn_conv', 'v_ffn_w_down']
TWIN_OUTPUTS = ['loss', 'grad_x', 'grad_ada_w', 'grad_ada_b', 'grad_norm_mix', 'grad_norm_ffn', 'grad_attn_w_in', 'grad_attn_q_norm_a', 'grad_attn_k_norm_a', 'grad_attn_q_norm_b', 'grad_attn_k_norm_b', 'grad_attn_sinks', 'grad_attn_w_out', 'grad_rec_w_in', 'grad_s5_lambda_re', 'grad_s5_lambda_im', 'grad_s5_log_dt', 'grad_s5_b_re', 'grad_s5_b_im', 'grad_s5_c_re', 'grad_s5_c_im', 'grad_s5_d', 'grad_s5_glu_w', 'grad_s5_glu_b', 'grad_dn_conv', 'grad_dn_a_log', 'grad_dn_dt_bias', 'grad_dn_out_norm', 'grad_rec_w_out', 'grad_ffn_w_up', 'grad_ffn_conv', 'grad_ffn_w_down', 'delta_ada_w', 'delta_ada_b', 'delta_norm_mix', 'delta_norm_ffn', 'delta_attn_w_in', 'delta_attn_q_norm_a', 'delta_attn_k_norm_a', 'delta_attn_q_norm_b', 'delta_attn_k_norm_b', 'delta_attn_sinks', 'delta_attn_w_out', 'delta_rec_w_in', 'delta_s5_lambda_re', 'delta_s5_lambda_im', 'delta_s5_log_dt', 'delta_s5_b_re', 'delta_s5_b_im', 'delta_s5_c_re', 'delta_s5_c_im', 'delta_s5_d', 'delta_s5_glu_w', 'delta_s5_glu_b', 'delta_dn_conv', 'delta_dn_a_log', 'delta_dn_dt_bias', 'delta_dn_out_norm', 'delta_rec_w_out', 'delta_ffn_w_up', 'delta_ffn_conv', 'delta_ffn_w_down', 'new_m_ada_w', 'new_m_ada_b', 'new_m_norm_mix', 'new_m_norm_ffn', 'new_m_attn_w_in', 'new_m_attn_q_norm_a', 'new_m_attn_k_norm_a', 'new_m_attn_q_norm_b', 'new_m_attn_k_norm_b', 'new_m_attn_sinks', 'new_m_attn_w_out', 'new_m_rec_w_in', 'new_m_s5_lambda_re', 'new_m_s5_lambda_im', 'new_m_s5_log_dt', 'new_m_s5_b_re', 'new_m_s5_b_im', 'new_m_s5_c_re', 'new_m_s5_c_im', 'new_m_s5_d', 'new_m_s5_glu_w', 'new_m_s5_glu_b', 'new_m_dn_conv', 'new_m_dn_a_log', 'new_m_dn_dt_bias', 'new_m_dn_out_norm', 'new_m_rec_w_out', 'new_m_ffn_w_up', 'new_m_ffn_conv', 'new_m_ffn_w_down', 'new_v_ada_w', 'new_v_ada_b', 'new_v_norm_mix', 'new_v_norm_ffn', 'new_v_attn_w_in', 'new_v_attn_q_norm_a', 'new_v_attn_k_norm_a', 'new_v_attn_q_norm_b', 'new_v_attn_k_norm_b', 'new_v_attn_sinks', 'new_v_attn_w_out', 'new_v_rec_w_in', 'new_v_s5_lambda_re', 'new_v_s5_lambda_im', 'new_v_s5_log_dt', 'new_v_s5_b_re', 'new_v_s5_b_im', 'new_v_s5_c_re', 'new_v_s5_c_im', 'new_v_s5_d', 'new_v_s5_glu_w', 'new_v_s5_glu_b', 'new_v_dn_conv', 'new_v_dn_a_log', 'new_v_dn_dt_bias', 'new_v_dn_out_norm', 'new_v_rec_w_out', 'new_v_ffn_w_up', 'new_v_ffn_conv', 'new_v_ffn_w_down']
TWIN_LEAF_KINDS = {'loss': 'loss', 'grad_x': 'grad_x', 'grad_ada_w': 'grad_w', 'grad_ada_b': 'grad_w', 'grad_norm_mix': 'grad_w', 'grad_norm_ffn': 'grad_w', 'grad_attn_w_in': 'grad_w', 'grad_attn_q_norm_a': 'grad_w', 'grad_attn_k_norm_a': 'grad_w', 'grad_attn_q_norm_b': 'grad_w', 'grad_attn_k_norm_b': 'grad_w', 'grad_attn_sinks': 'grad_w', 'grad_attn_w_out': 'grad_w', 'grad_rec_w_in': 'grad_w', 'grad_s5_lambda_re': 'grad_w', 'grad_s5_lambda_im': 'grad_w', 'grad_s5_log_dt': 'grad_w', 'grad_s5_b_re': 'grad_w', 'grad_s5_b_im': 'grad_w', 'grad_s5_c_re': 'grad_w', 'grad_s5_c_im': 'grad_w', 'grad_s5_d': 'grad_w', 'grad_s5_glu_w': 'grad_w', 'grad_s5_glu_b': 'grad_w', 'grad_dn_conv': 'grad_w', 'grad_dn_a_log': 'grad_w', 'grad_dn_dt_bias': 'grad_w', 'grad_dn_out_norm': 'grad_w', 'grad_rec_w_out': 'grad_w', 'grad_ffn_w_up': 'grad_w', 'grad_ffn_conv': 'grad_w', 'grad_ffn_w_down': 'grad_w', 'delta_ada_w': 'delta_w', 'delta_ada_b': 'delta_w', 'delta_norm_mix': 'delta_w', 'delta_norm_ffn': 'delta_w', 'delta_attn_w_in': 'delta_w', 'delta_attn_q_norm_a': 'delta_w', 'delta_attn_k_norm_a': 'delta_w', 'delta_attn_q_norm_b': 'delta_w', 'delta_attn_k_norm_b': 'delta_w', 'delta_attn_sinks': 'delta_w', 'delta_attn_w_out': 'delta_w', 'delta_rec_w_in': 'delta_w', 'delta_s5_lambda_re': 'delta_w', 'delta_s5_lambda_im': 'delta_w', 'delta_s5_log_dt': 'delta_w', 'delta_s5_b_re': 'delta_w', 'delta_s5_b_im': 'delta_w', 'delta_s5_c_re': 'delta_w', 'delta_s5_c_im': 'delta_w', 'delta_s5_d': 'delta_w', 'delta_s5_glu_w': 'delta_w', 'delta_s5_glu_b': 'delta_w', 'delta_dn_conv': 'delta_w', 'delta_dn_a_log': 'delta_w', 'delta_dn_dt_bias': 'delta_w', 'delta_dn_out_norm': 'delta_w', 'delta_rec_w_out': 'delta_w', 'delta_ffn_w_up': 'delta_w', 'delta_ffn_conv': 'delta_w', 'delta_ffn_w_down': 'delta_w', 'new_m_ada_w': 'new_m', 'new_m_ada_b': 'new_m', 'new_m_norm_mix': 'new_m', 'new_m_norm_ffn': 'new_m', 'new_m_attn_w_in': 'new_m', 'new_m_attn_q_norm_a': 'new_m', 'new_m_attn_k_norm_a': 'new_m', 'new_m_attn_q_norm_b': 'new_m', 'new_m_attn_k_norm_b': 'new_m', 'new_m_attn_sinks': 'new_m', 'new_m_attn_w_out': 'new_m', 'new_m_rec_w_in': 'new_m', 'new_m_s5_lambda_re': 'new_m', 'new_m_s5_lambda_im': 'new_m', 'new_m_s5_log_dt': 'new_m', 'new_m_s5_b_re': 'new_m', 'new_m_s5_b_im': 'new_m', 'new_m_s5_c_re': 'new_m', 'new_m_s5_c_im': 'new_m', 'new_m_s5_d': 'new_m', 'new_m_s5_glu_w': 'new_m', 'new_m_s5_glu_b': 'new_m', 'new_m_dn_conv': 'new_m', 'new_m_dn_a_log': 'new_m', 'new_m_dn_dt_bias': 'new_m', 'new_m_dn_out_norm': 'new_m', 'new_m_rec_w_out': 'new_m', 'new_m_ffn_w_up': 'new_m', 'new_m_ffn_conv': 'new_m', 'new_m_ffn_w_down': 'new_m', 'new_v_ada_w': 'new_v', 'new_v_ada_b': 'new_v', 'new_v_norm_mix': 'new_v', 'new_v_norm_ffn': 'new_v', 'new_v_attn_w_in': 'new_v', 'new_v_attn_q_norm_a': 'new_v', 'new_v_attn_k_norm_a': 'new_v', 'new_v_attn_q_norm_b': 'new_v', 'new_v_attn_k_norm_b': 'new_v', 'new_v_attn_sinks': 'new_v', 'new_v_attn_w_out': 'new_v', 'new_v_rec_w_in': 'new_v', 'new_v_s5_lambda_re': 'new_v', 'new_v_s5_lambda_im': 'new_v', 'new_v_s5_log_dt': 'new_v', 'new_v_s5_b_re': 'new_v', 'new_v_s5_b_im': 'new_v', 'new_v_s5_c_re': 'new_v', 'new_v_s5_c_im': 'new_v', 'new_v_s5_d': 'new_v', 'new_v_s5_glu_w': 'new_v', 'new_v_s5_glu_b': 'new_v', 'new_v_dn_conv': 'new_v', 'new_v_dn_a_log': 'new_v', 'new_v_dn_dt_bias': 'new_v', 'new_v_dn_out_norm': 'new_v', 'new_v_rec_w_out': 'new_v', 'new_v_ffn_w_up': 'new_v', 'new_v_ffn_conv': 'new_v', 'new_v_ffn_w_down': 'new_v'}


def _forward(args):
    return _fwd_reference(*[args[k] for k in FWD_PARAMS])


def _output_shape():
    def fwd():
        inp = _fwd_setup_inputs(0)
        return _fwd_reference(*[inp[k] for k in FWD_PARAMS])
    out = _jax.eval_shape(fwd)
    return out.shape, out.dtype

N_MICROBATCH = 1
ADAM_LR = 0.001
ADAM_B1 = 0.9
ADAM_B2 = 0.999
ADAM_EPS = 1e-08
ADAM_WD = 0.01
ADAM_STEP = 10
PER_EXAMPLE_BATCH_AXIS = {'x': 0, 'c': 0, 'loss_target': 0}
SHARED_INPUTS = []
_WEIGHT_DTYPES = {'ada_w': _jnp.float32, 'ada_b': _jnp.float32, 'norm_mix': _jnp.float32, 'norm_ffn': _jnp.float32, 'attn_w_in': _jnp.float32, 'attn_q_norm_a': _jnp.float32, 'attn_k_norm_a': _jnp.float32, 'attn_q_norm_b': _jnp.float32, 'attn_k_norm_b': _jnp.float32, 'attn_sinks': _jnp.float32, 'attn_w_out': _jnp.float32, 'rec_w_in': _jnp.float32, 's5_lambda_re': _jnp.float32, 's5_lambda_im': _jnp.float32, 's5_log_dt': _jnp.float32, 's5_b_re': _jnp.float32, 's5_b_im': _jnp.float32, 's5_c_re': _jnp.float32, 's5_c_im': _jnp.float32, 's5_d': _jnp.float32, 's5_glu_w': _jnp.float32, 's5_glu_b': _jnp.float32, 'dn_conv': _jnp.float32, 'dn_a_log': _jnp.float32, 'dn_dt_bias': _jnp.float32, 'dn_out_norm': _jnp.float32, 'rec_w_out': _jnp.float32, 'ffn_w_up': _jnp.float32, 'ffn_conv': _jnp.float32, 'ffn_w_down': _jnp.float32}
MOMENT_SCALE = {'ada_w': 1.344596e+00, 'ada_b': 3.603827e+00, 'norm_mix': 1.656275e+00, 'norm_ffn': 6.414651e+00, 'attn_w_in': 2.535068e-01, 'attn_q_norm_a': 3.066501e+00, 'attn_k_norm_a': 3.065816e+00, 'attn_q_norm_b': 7.554958e-01, 'attn_k_norm_b': 7.584088e-01, 'attn_sinks': 1.532865e+01, 'attn_w_out': 3.268383e-01, 'rec_w_in': 2.191892e-01, 's5_lambda_re': 1.879821e-02, 's5_lambda_im': 1.973505e-02, 's5_log_dt': 3.757624e+00, 's5_b_re': 1.864540e-02, 's5_b_im': 1.455694e-02, 's5_c_re': 2.302495e-02, 's5_c_im': 2.138687e-02, 's5_d': 1.165437e+00, 's5_glu_w': 3.162489e-01, 's5_glu_b': 6.827419e-01, 'dn_conv': 2.266534e-01, 'dn_a_log': 6.372116e+00, 'dn_dt_bias': 6.023792e+00, 'dn_out_norm': 1.584714e+01, 'rec_w_out': 2.077215e-01, 'ffn_w_up': 1.500118e-01, 'ffn_conv': 9.438273e-01, 'ffn_w_down': 1.263125e-01}


def _to_microbatches(a, axis):
    t = _jnp.moveaxis(a, axis, 0)
    t = t.reshape((N_MICROBATCH, t.shape[0] // N_MICROBATCH) + t.shape[1:])
    return _jnp.moveaxis(t, 1, axis + 1)


def setup_inputs(seed: int = 0) -> dict:
    inp = _fwd_setup_inputs(seed)
    key = _jax.random.fold_in(_jax.random.key(seed), 7919)
    shape, _ = _output_shape()
    out = dict(inp)
    out["loss_target"] = _jax.random.normal(_jax.random.fold_in(key, 0), shape, _jnp.float32)
    for i, name in enumerate(TWIN_WEIGHTS):
        w = inp[name].astype(_jnp.float32)
        if MOMENT_SCALE is None:
            s = _jnp.sqrt(_jnp.mean(_jnp.square(w)) + 1e-30)
        else:
            s = MOMENT_SCALE[name]
        km, kv = _jax.random.split(_jax.random.fold_in(key, i + 1))
        out[name] = w
        out["m_" + name] = s * _jax.random.normal(km, w.shape, _jnp.float32)
        out["v_" + name] = (s * s) * _jax.random.uniform(kv, w.shape, _jnp.float32, 0.5, 1.5)
    if N_MICROBATCH > 1:
        for name, axis in PER_EXAMPLE_BATCH_AXIS.items():
            out[name] = _to_microbatches(out[name], axis)
    return {'x': out['x'], 'c': out['c'], 'ada_w': out['ada_w'], 'ada_b': out['ada_b'], 'norm_mix': out['norm_mix'], 'norm_ffn': out['norm_ffn'], 'attn_w_in': out['attn_w_in'], 'attn_q_norm_a': out['attn_q_norm_a'], 'attn_k_norm_a': out['attn_k_norm_a'], 'attn_q_norm_b': out['attn_q_norm_b'], 'attn_k_norm_b': out['attn_k_norm_b'], 'attn_sinks': out['attn_sinks'], 'attn_w_out': out['attn_w_out'], 'rec_w_in': out['rec_w_in'], 's5_lambda_re': out['s5_lambda_re'], 's5_lambda_im': out['s5_lambda_im'], 's5_log_dt': out['s5_log_dt'], 's5_b_re': out['s5_b_re'], 's5_b_im': out['s5_b_im'], 's5_c_re': out['s5_c_re'], 's5_c_im': out['s5_c_im'], 's5_d': out['s5_d'], 's5_glu_w': out['s5_glu_w'], 's5_glu_b': out['s5_glu_b'], 'dn_conv': out['dn_conv'], 'dn_a_log': out['dn_a_log'], 'dn_dt_bias': out['dn_dt_bias'], 'dn_out_norm': out['dn_out_norm'], 'rec_w_out': out['rec_w_out'], 'ffn_w_up': out['ffn_w_up'], 'ffn_conv': out['ffn_conv'], 'ffn_w_down': out['ffn_w_down'], 'loss_target': out['loss_target'], 'm_ada_w': out['m_ada_w'], 'm_ada_b': out['m_ada_b'], 'm_norm_mix': out['m_norm_mix'], 'm_norm_ffn': out['m_norm_ffn'], 'm_attn_w_in': out['m_attn_w_in'], 'm_attn_q_norm_a': out['m_attn_q_norm_a'], 'm_attn_k_norm_a': out['m_attn_k_norm_a'], 'm_attn_q_norm_b': out['m_attn_q_norm_b'], 'm_attn_k_norm_b': out['m_attn_k_norm_b'], 'm_attn_sinks': out['m_attn_sinks'], 'm_attn_w_out': out['m_attn_w_out'], 'm_rec_w_in': out['m_rec_w_in'], 'm_s5_lambda_re': out['m_s5_lambda_re'], 'm_s5_lambda_im': out['m_s5_lambda_im'], 'm_s5_log_dt': out['m_s5_log_dt'], 'm_s5_b_re': out['m_s5_b_re'], 'm_s5_b_im': out['m_s5_b_im'], 'm_s5_c_re': out['m_s5_c_re'], 'm_s5_c_im': out['m_s5_c_im'], 'm_s5_d': out['m_s5_d'], 'm_s5_glu_w': out['m_s5_glu_w'], 'm_s5_glu_b': out['m_s5_glu_b'], 'm_dn_conv': out['m_dn_conv'], 'm_dn_a_log': out['m_dn_a_log'], 'm_dn_dt_bias': out['m_dn_dt_bias'], 'm_dn_out_norm': out['m_dn_out_norm'], 'm_rec_w_out': out['m_rec_w_out'], 'm_ffn_w_up': out['m_ffn_w_up'], 'm_ffn_conv': out['m_ffn_conv'], 'm_ffn_w_down': out['m_ffn_w_down'], 'v_ada_w': out['v_ada_w'], 'v_ada_b': out['v_ada_b'], 'v_norm_mix': out['v_norm_mix'], 'v_norm_ffn': out['v_norm_ffn'], 'v_attn_w_in': out['v_attn_w_in'], 'v_attn_q_norm_a': out['v_attn_q_norm_a'], 'v_attn_k_norm_a': out['v_attn_k_norm_a'], 'v_attn_q_norm_b': out['v_attn_q_norm_b'], 'v_attn_k_norm_b': out['v_attn_k_norm_b'], 'v_attn_sinks': out['v_attn_sinks'], 'v_attn_w_out': out['v_attn_w_out'], 'v_rec_w_in': out['v_rec_w_in'], 'v_s5_lambda_re': out['v_s5_lambda_re'], 'v_s5_lambda_im': out['v_s5_lambda_im'], 'v_s5_log_dt': out['v_s5_log_dt'], 'v_s5_b_re': out['v_s5_b_re'], 'v_s5_b_im': out['v_s5_b_im'], 'v_s5_c_re': out['v_s5_c_re'], 'v_s5_c_im': out['v_s5_c_im'], 'v_s5_d': out['v_s5_d'], 'v_s5_glu_w': out['v_s5_glu_w'], 'v_s5_glu_b': out['v_s5_glu_b'], 'v_dn_conv': out['v_dn_conv'], 'v_dn_a_log': out['v_dn_a_log'], 'v_dn_dt_bias': out['v_dn_dt_bias'], 'v_dn_out_norm': out['v_dn_out_norm'], 'v_rec_w_out': out['v_rec_w_out'], 'v_ffn_w_up': out['v_ffn_w_up'], 'v_ffn_conv': out['v_ffn_conv'], 'v_ffn_w_down': out['v_ffn_w_down']}


def _loss(weights, diff, rest, loss_target):
    with _jax.named_scope("forward"):
        args = {**rest, TWIN_DIFF_INPUT: diff, **{k: w.astype(_WEIGHT_DTYPES[k]) for k, w in weights.items()}}
        y = _forward(args)
    with _jax.named_scope("loss_head"):
        err = _jnp.square(y.astype(_jnp.float32) - loss_target)
        return 0.5 * _jnp.sum(_jnp.mean(err, axis=-1)) if err.ndim else 0.5 * err


def _adamw(w, g, m, v):
    m = ADAM_B1 * m + (1.0 - ADAM_B1) * g
    v = ADAM_B2 * v + (1.0 - ADAM_B2) * _jnp.square(g)
    m_hat = m / (1.0 - ADAM_B1 ** ADAM_STEP)
    v_hat = v / (1.0 - ADAM_B2 ** ADAM_STEP)
    delta = -ADAM_LR * (m_hat / (_jnp.sqrt(v_hat) + ADAM_EPS) + ADAM_WD * w)
    return delta, m, v


def reference(x, c, ada_w, ada_b, norm_mix, norm_ffn, attn_w_in, attn_q_norm_a, attn_k_norm_a, attn_q_norm_b, attn_k_norm_b, attn_sinks, attn_w_out, rec_w_in, s5_lambda_re, s5_lambda_im, s5_log_dt, s5_b_re, s5_b_im, s5_c_re, s5_c_im, s5_d, s5_glu_w, s5_glu_b, dn_conv, dn_a_log, dn_dt_bias, dn_out_norm, rec_w_out, ffn_w_up, ffn_conv, ffn_w_down, loss_target, m_ada_w, m_ada_b, m_norm_mix, m_norm_ffn, m_attn_w_in, m_attn_q_norm_a, m_attn_k_norm_a, m_attn_q_norm_b, m_attn_k_norm_b, m_attn_sinks, m_attn_w_out, m_rec_w_in, m_s5_lambda_re, m_s5_lambda_im, m_s5_log_dt, m_s5_b_re, m_s5_b_im, m_s5_c_re, m_s5_c_im, m_s5_d, m_s5_glu_w, m_s5_glu_b, m_dn_conv, m_dn_a_log, m_dn_dt_bias, m_dn_out_norm, m_rec_w_out, m_ffn_w_up, m_ffn_conv, m_ffn_w_down, v_ada_w, v_ada_b, v_norm_mix, v_norm_ffn, v_attn_w_in, v_attn_q_norm_a, v_attn_k_norm_a, v_attn_q_norm_b, v_attn_k_norm_b, v_attn_sinks, v_attn_w_out, v_rec_w_in, v_s5_lambda_re, v_s5_lambda_im, v_s5_log_dt, v_s5_b_re, v_s5_b_im, v_s5_c_re, v_s5_c_im, v_s5_d, v_s5_glu_w, v_s5_glu_b, v_dn_conv, v_dn_a_log, v_dn_dt_bias, v_dn_out_norm, v_rec_w_out, v_ffn_w_up, v_ffn_conv, v_ffn_w_down):
    given = dict(x=x, c=c, ada_w=ada_w, ada_b=ada_b, norm_mix=norm_mix, norm_ffn=norm_ffn, attn_w_in=attn_w_in, attn_q_norm_a=attn_q_norm_a, attn_k_norm_a=attn_k_norm_a, attn_q_norm_b=attn_q_norm_b, attn_k_norm_b=attn_k_norm_b, attn_sinks=attn_sinks, attn_w_out=attn_w_out, rec_w_in=rec_w_in, s5_lambda_re=s5_lambda_re, s5_lambda_im=s5_lambda_im, s5_log_dt=s5_log_dt, s5_b_re=s5_b_re, s5_b_im=s5_b_im, s5_c_re=s5_c_re, s5_c_im=s5_c_im, s5_d=s5_d, s5_glu_w=s5_glu_w, s5_glu_b=s5_glu_b, dn_conv=dn_conv, dn_a_log=dn_a_log, dn_dt_bias=dn_dt_bias, dn_out_norm=dn_out_norm, rec_w_out=rec_w_out, ffn_w_up=ffn_w_up, ffn_conv=ffn_conv, ffn_w_down=ffn_w_down, loss_target=loss_target, m_ada_w=m_ada_w, m_ada_b=m_ada_b, m_norm_mix=m_norm_mix, m_norm_ffn=m_norm_ffn, m_attn_w_in=m_attn_w_in, m_attn_q_norm_a=m_attn_q_norm_a, m_attn_k_norm_a=m_attn_k_norm_a, m_attn_q_norm_b=m_attn_q_norm_b, m_attn_k_norm_b=m_attn_k_norm_b, m_attn_sinks=m_attn_sinks, m_attn_w_out=m_attn_w_out, m_rec_w_in=m_rec_w_in, m_s5_lambda_re=m_s5_lambda_re, m_s5_lambda_im=m_s5_lambda_im, m_s5_log_dt=m_s5_log_dt, m_s5_b_re=m_s5_b_re, m_s5_b_im=m_s5_b_im, m_s5_c_re=m_s5_c_re, m_s5_c_im=m_s5_c_im, m_s5_d=m_s5_d, m_s5_glu_w=m_s5_glu_w, m_s5_glu_b=m_s5_glu_b, m_dn_conv=m_dn_conv, m_dn_a_log=m_dn_a_log, m_dn_dt_bias=m_dn_dt_bias, m_dn_out_norm=m_dn_out_norm, m_rec_w_out=m_rec_w_out, m_ffn_w_up=m_ffn_w_up, m_ffn_conv=m_ffn_conv, m_ffn_w_down=m_ffn_w_down, v_ada_w=v_ada_w, v_ada_b=v_ada_b, v_norm_mix=v_norm_mix, v_norm_ffn=v_norm_ffn, v_attn_w_in=v_attn_w_in, v_attn_q_norm_a=v_attn_q_norm_a, v_attn_k_norm_a=v_attn_k_norm_a, v_attn_q_norm_b=v_attn_q_norm_b, v_attn_k_norm_b=v_attn_k_norm_b, v_attn_sinks=v_attn_sinks, v_attn_w_out=v_attn_w_out, v_rec_w_in=v_rec_w_in, v_s5_lambda_re=v_s5_lambda_re, v_s5_lambda_im=v_s5_lambda_im, v_s5_log_dt=v_s5_log_dt, v_s5_b_re=v_s5_b_re, v_s5_b_im=v_s5_b_im, v_s5_c_re=v_s5_c_re, v_s5_c_im=v_s5_c_im, v_s5_d=v_s5_d, v_s5_glu_w=v_s5_glu_w, v_s5_glu_b=v_s5_glu_b, v_dn_conv=v_dn_conv, v_dn_a_log=v_dn_a_log, v_dn_dt_bias=v_dn_dt_bias, v_dn_out_norm=v_dn_out_norm, v_rec_w_out=v_rec_w_out, v_ffn_w_up=v_ffn_w_up, v_ffn_conv=v_ffn_conv, v_ffn_w_down=v_ffn_w_down)
    weights = {n: given[n] for n in TWIN_WEIGHTS}
    shared = {n: given[n] for n in SHARED_INPUTS}
    per_example = {n: given[n] for n in ['x', 'c']}
    grad_fn = _jax.value_and_grad(_loss, argnums=(0, 1))

    def one_microbatch(ex, loss_target):
        ex = dict(ex)
        diff = ex.pop(TWIN_DIFF_INPUT)
        return grad_fn(weights, diff, {**shared, **ex}, loss_target)

    if N_MICROBATCH == 1:
        loss, (grad_w, grad_x) = one_microbatch(per_example, given["loss_target"])
    else:
        def body(carry, xs):
            loss_sum, grad_sum = carry
            l_k, (gw_k, gx_k) = one_microbatch(xs[0], xs[1])
            with _jax.named_scope("update"):
                return (loss_sum + l_k, _jax.tree.map(_jnp.add, grad_sum, gw_k)), gx_k

        init = (_jnp.zeros((), _jnp.float32), _jax.tree.map(_jnp.zeros_like, weights))
        (loss, grad_w), grad_x = _jax.lax.scan(body, init, (per_example, given["loss_target"]))
    with _jax.named_scope("update"):
        delta_w, new_m, new_v = {}, {}, {}
        for n in TWIN_WEIGHTS:
            delta_w[n], new_m[n], new_v[n] = _adamw(weights[n], grad_w[n], given["m_" + n], given["v_" + n])
    return (loss, grad_x, *[grad_w[n] for n in TWIN_WEIGHTS], *[delta_w[n] for n in TWIN_WEIGHTS],
            *[new_m[n] for n in TWIN_WEIGHTS], *[new_v[n] for n in TWIN_WEIGHTS])
```

```python
import functools
import math

import numpy as np
import jax
import jax.numpy as jnp
from jax import lax
from jax.experimental import pallas as pl
from jax.experimental.pallas import tpu as pltpu

f32 = jnp.float32
bf16 = jnp.bfloat16
HI = lax.Precision.HIGHEST
MESH = pl.DeviceIdType.MESH

HEAD_DIM = 64
BLOCK = 128
A_Q_HEADS = 8
A_KV_HEADS = 2
A_WINDOW = 128
B_HEADS = 8
B_BRANCHES = ((128, 1), (512, 4), (2048, 16))
N_ATTN_HEADS = 16
ATTN_IN = 2304
S5_GROUP = 16
S5_GROUPS = 16
S5_WIDTH = 256
S5_STATE = 64
DN_HEADS = 6
DN_DK = 128
DN_CONV = 4
DN_CHUNK = 64
REC_IN = 3340
REC_PAD = 3584
FFN_CONV = 3
EPS = 1e-6
ADAM_LR = 0.001
ADAM_B1 = 0.9
ADAM_B2 = 0.999
ADAM_EPS = 1e-08
ADAM_WD = 0.01
ADAM_STEP = 10

LANE = 128
SUBLANE = 8
VMEM_LIMIT = 52 * 1024 * 1024


def _cp(*sem):
    return pltpu.CompilerParams(dimension_semantics=sem, vmem_limit_bytes=VMEM_LIMIT)


def _pick(dim, cap, unit=LANE):
    for t in (2048, 1024, 768, 512, 384, 256, 128, 64, 32, 16, 8):
        if t <= cap and t % unit == 0 and dim % t == 0:
            return t
    return dim


def _dot(a, b, dims=(((1,), (0,)), ((), ())), precision=None):
    return lax.dot_general(a, b, dims, precision=precision, preferred_element_type=f32)


NT = (((1,), (1,)), ((), ()))
TN = (((0,), (0,)), ((), ()))


def mm(a, b, *, name, ta=False, tb=False, a_win=None, b_win=None, out_dtypes=(f32,),
       epi=None, epi_mn=(), epi_n=(), tm_cap=512, tn_cap=512, tk_cap=1024):
    a0, a1 = a.shape
    b0, b1 = b.shape
    aw = a_win or (0, a1)
    bw = b_win or (0, b1)
    if ta:
        K, M = a0, aw[1]
    else:
        M, K = a0, aw[1]
    if tb:
        N, K2 = b0, bw[1]
    else:
        K2, N = b0, bw[1]
    assert K == K2, (a.shape, b.shape, ta, tb, a_win, b_win)
    tm = _pick(M, tm_cap, SUBLANE if not ta else LANE)
    tn = _pick(N, tn_cap)
    tk = _pick(K, tk_cap, LANE if not ta else SUBLANE)
    if tb:
        tk = _pick(K, tk_cap)
    nk = K // tk
    if ta:
        assert aw[0] % tm == 0
        mo = aw[0] // tm
        a_spec = pl.BlockSpec((tk, tm), lambda i, j, k: (k, i + mo))
    else:
        assert aw[0] % tk == 0
        ko = aw[0] // tk
        a_spec = pl.BlockSpec((tm, tk), lambda i, j, k: (i, k + ko))
    if tb:
        assert bw[0] % tk == 0
        kob = bw[0] // tk
        b_spec = pl.BlockSpec((tn, tk), lambda i, j, k: (j, k + kob))
    else:
        assert bw[0] % tn == 0
        no = bw[0] // tn
        b_spec = pl.BlockSpec((tk, tn), lambda i, j, k: (k, j + no))
    dims = (((0 if ta else 1,), (1 if tb else 0,)), ((), ()))
    n_mn, n_n, n_out = len(epi_mn), len(epi_n), len(out_dtypes)

    def body(a_ref, b_ref, *rest):
        mn_refs = rest[:n_mn]
        n_refs = rest[n_mn:n_mn + n_n]
        out_refs = rest[n_mn + n_n:n_mn + n_n + n_out]
        acc = rest[-1]
        k = pl.program_id(2)

        @pl.when(k == 0)
        def _():
            acc[...] = jnp.zeros_like(acc)

        acc[...] += _dot(a_ref[...].astype(bf16), b_ref[...].astype(bf16), dims)

        @pl.when(k == nk - 1)
        def _():
            r = acc[...]
            if epi is None:
                outs = (r,)
            else:
                outs = epi(r, *[m[...] for m in mn_refs], *[v[...] for v in n_refs])
            for o_ref, o in zip(out_refs, outs):
                o_ref[...] = o.astype(o_ref.dtype)

    mn_spec = pl.BlockSpec((tm, tn), lambda i, j, k: (i, j))
    n_spec = pl.BlockSpec((1, tn), lambda i, j, k: (0, j))
    outs = pl.pallas_call(
        body,
        grid=(M // tm, N // tn, nk),
        in_specs=[a_spec, b_spec] + [mn_spec] * n_mn + [n_spec] * n_n,
        out_specs=[mn_spec] * n_out,
        out_shape=[jax.ShapeDtypeStruct((M, N), d) for d in out_dtypes],
        scratch_shapes=[pltpu.VMEM((tm, tn), f32)],
        compiler_params=_cp("parallel", "parallel", "arbitrary"),
        name=name,
    )(a, b, *epi_mn, *epi_n)
    return outs[0] if n_out == 1 else tuple(outs)


def rowwise(fn, *, name, L, tm, rows=(), consts=(), outs=(), sums=()):
    nb = L // tm
    hb = tm // SUBLANE
    in_specs = []
    arrs = []
    for arr, start, width, kind in rows:
        assert start % width == 0, (name, start, width)
        co = start // width
        if kind == "cur":
            in_specs.append(pl.BlockSpec((tm, width), lambda i, co=co: (i, co)))
        elif kind == "prev":
            in_specs.append(pl.BlockSpec((SUBLANE, width), lambda i, co=co: (jnp.maximum(i * hb - 1, 0), co)))
        else:
            last = L // SUBLANE - 1
            in_specs.append(pl.BlockSpec((SUBLANE, width), lambda i, co=co, last=last: (jnp.minimum((i + 1) * hb, last), co)))
        arrs.append(arr)
    for cst in consts:
        assert cst.ndim == 2
        in_specs.append(pl.BlockSpec(cst.shape, lambda i: (0, 0)))
        arrs.append(cst)
    n_rows, n_c, n_o, n_s = len(rows), len(consts), len(outs), len(sums)
    out_specs = [pl.BlockSpec((tm, w), lambda i: (i, 0)) for w, _ in outs]
    out_specs += [pl.BlockSpec(s, lambda i: (0, 0)) for s in sums]
    out_shape = [jax.ShapeDtypeStruct((L, w), d) for w, d in outs]
    out_shape += [jax.ShapeDtypeStruct(s, f32) for s in sums]

    def body(*refs):
        i = pl.program_id(0)
        vals = [r[...] for r in refs[:n_rows + n_c]]
        res = fn(i, nb, *vals)
        if not isinstance(res, (tuple, list)):
            res = (res,)
        o_refs = refs[n_rows + n_c:n_rows + n_c + n_o]
        s_refs = refs[n_rows + n_c + n_o:]
        for o_ref, o in zip(o_refs, res[:n_o]):
            o_ref[...] = o.astype(o_ref.dtype)
        if n_s:
            @pl.when(i == 0)
            def _():
                for s_ref in s_refs:
                    s_ref[...] = jnp.zeros_like(s_ref)

            for s_ref, s in zip(s_refs, res[n_o:]):
                s_ref[...] += s

    res = pl.pallas_call(
        body,
        grid=(nb,),
        in_specs=in_specs,
        out_specs=out_specs,
        out_shape=out_shape,
        compiler_params=_cp("arbitrary" if n_s else "parallel"),
        name=name,
    )(*arrs)
    return res[0] if len(res) == 1 else tuple(res)


def _shift_down(x, prev8, k):
    cat = jnp.concatenate([prev8, x], axis=0)
    return pltpu.roll(cat, k, 0)[SUBLANE:, :]


def _shift_up(x, next8, k):
    n = x.shape[0]
    cat = jnp.concatenate([x, next8], axis=0)
    return pltpu.roll(cat, n + SUBLANE - k, 0)[:n, :]


def _colsum(x):
    return jnp.sum(x, axis=0, keepdims=True)


def _silu(x):
    return x * jax.nn.sigmoid(x)


def _modulate_fn(x, nw, sc, sh):
    r = lax.rsqrt(jnp.mean(x * x, axis=-1, keepdims=True) + EPS)
    return (x * r * nw) * (1.0 + sc) + sh


def modulate_fwd(x, nw, sc, sh, name):
    L, D = x.shape

    def fn(i, nb, xt, nwv, scv, shv):
        return _modulate_fn(xt, nwv, scv, shv)

    return rowwise(fn, name=name, L=L, tm=_pick(L, 512, SUBLANE), rows=[(x, 0, D, "cur")],
                   consts=[nw, sc, sh], outs=[(D, bf16)])


def modulate_bwd(x, nw, sc, sh, dh, dx_in, name):
    L, D = x.shape

    def fn(i, nb, xt, dht, dxt, nwv, scv, shv):
        _, vjp = jax.vjp(_modulate_fn, xt, nwv, scv, shv)
        dx, dnw, dsc, dsh = vjp(dht)
        return dxt + dx, dnw, dsc, dsh

    return rowwise(fn, name=name, L=L, tm=_pick(L, 256, SUBLANE),
                   rows=[(x, 0, D, "cur"), (dh, 0, D, "cur"), (dx_in, 0, D, "cur")],
                   consts=[nw, sc, sh], outs=[(D, f32)], sums=[(1, D)] * 3)


def resid_bwd(dx, y, g, name):
    L, D = dx.shape

    def fn(i, nb, dxt, yt, gv):
        return dxt * gv, _colsum(dxt * yt)

    return rowwise(fn, name=name, L=L, tm=_pick(L, 512, SUBLANE),
                   rows=[(dx, 0, D, "cur"), (y, 0, D, "cur")], consts=[g],
                   outs=[(D, bf16)], sums=[(1, D)])


def _resid_epi(acc, xt, gv):
    return acc, xt + gv * acc


def _stack_rows(rows, n=SUBLANE):
    c = rows[0].shape[1]
    ridx = lax.broadcasted_iota(jnp.int32, (n, c), 0)
    out = jnp.zeros((n, c), f32)
    for j, r in enumerate(rows):
        out = out + jnp.where(ridx == j, r, 0.0)
    return out


def _conv_causal(x, prev8, w):
    W = w.shape[0]
    y = x * w[W - 1:W, :]
    for j in range(W - 1):
        y = y + _shift_down(x, prev8, W - 1 - j) * w[j:j + 1, :]
    return y


def _conv_causal_bwd_x(dy, next8, w):
    W = w.shape[0]
    dx = dy * w[W - 1:W, :]
    for j in range(W - 1):
        dx = dx + _shift_up(dy, next8, W - 1 - j) * w[j:j + 1, :]
    return dx


def _conv_causal_bwd_w(dy, x, prev8, W):
    rows = [_colsum(dy * _shift_down(x, prev8, W - 1 - j)) for j in range(W - 1)]
    rows.append(_colsum(dy * x))
    return _stack_rows(rows)


def ffn_act_fwd(up, conv_w, name):
    L, F2 = up.shape
    F = F2 // 2

    def fn(i, nb, u, p8, w):
        c = _conv_causal(u, p8 * (i > 0).astype(f32), w)
        return _silu(c[:, :F]) * c[:, F:]

    return rowwise(fn, name=name, L=L, tm=_pick(L, 128, SUBLANE),
                   rows=[(up, 0, F2, "cur"), (up, 0, F2, "prev")], consts=[conv_w], outs=[(F, bf16)])


def ffn_act_bwd1(up, conv_w, dact, name):
    L, F2 = up.shape
    F = F2 // 2

    def fn(i, nb, u, da, p8, w):
        c = _conv_causal(u, p8 * (i > 0).astype(f32), w)
        a, b = c[:, :F], c[:, F:]
        sg = jax.nn.sigmoid(a)
        dadt = sg * (1.0 + a * (1.0 - sg))
        return jnp.concatenate([da * b * dadt, da * a * sg], axis=1)

    return rowwise(fn, name=name, L=L, tm=_pick(L, 128, SUBLANE),
                   rows=[(up, 0, F2, "cur"), (dact, 0, F, "cur"), (up, 0, F2, "prev")],
                   consts=[conv_w], outs=[(F2, f32)])


def conv_bwd(x, w, dc, name, out_dtype=bf16):
    L, C = x.shape
    W = w.shape[0]

    def fn(i, nb, xt, dct, p8, n8, wv):
        dx = _conv_causal_bwd_x(dct, n8 * (i < nb - 1).astype(f32), wv)
        dw = _conv_causal_bwd_w(dct, xt, p8 * (i > 0).astype(f32), W)
        return dx, dw

    return rowwise(fn, name=name, L=L, tm=_pick(L, 128, SUBLANE),
                   rows=[(x, 0, C, "cur"), (dc, 0, C, "cur"), (x, 0, C, "prev"), (dc, 0, C, "next")],
                   consts=[w], outs=[(C, out_dtype)], sums=[(SUBLANE, C)])


ALIBI = [2.0 ** (-8.0 * (i + 1) / N_ATTN_HEADS) for i in range(N_ATTN_HEADS)]
NEG = -1e30


class _Band:
    def __init__(self, dilation, group_a):
        d = dilation
        self.d = d
        self.group_a = group_a
        if group_a:
            self.P, self.qw, self.hps = 1, 512, 8
            self.qcol = lambda p: 0
            self.kcol = lambda p: 4
            self.vcol = lambda p: 5
            self.kv_of = lambda j: j // 4
            self.max_dist = A_WINDOW - 1
            sl = np.repeat(np.asarray(ALIBI[:8], np.float32), HEAD_DIM)[None, None, :]
        else:
            self.P, self.qw, self.hps = 4 * d, 128, 2
            self.qcol = lambda p: (p // 4) * 18 + 6 + p % 4
            self.kcol = lambda p: (p // 4) * 18 + 10 + p % 4
            self.vcol = lambda p: (p // 4) * 18 + 14 + p % 4
            self.kv_of = lambda j: j
            self.max_dist = BLOCK
            per = np.repeat(np.asarray(ALIBI[8:], np.float32), HEAD_DIM).reshape(4, 1, 128)
            sl = np.tile(per, (d, 1, 1))
        self.slopes = jnp.asarray(sl, f32)
        self.kvw = 128


def _band_mask(n, d, max_dist):
    qi = lax.broadcasted_iota(jnp.int32, (BLOCK, 2 * BLOCK), 0)
    kj = lax.broadcasted_iota(jnp.int32, (BLOCK, 2 * BLOCK), 1)
    dist = BLOCK + qi - kj
    valid = (dist >= 0) & (dist <= max_dist) & ((n > 0) | (kj >= BLOCK))
    return valid, -(d * dist).astype(f32)


def _rms64(x, w):
    r = lax.rsqrt(jnp.mean(x * x, axis=-1, keepdims=True) + EPS)
    xh = x * r
    return xh * w, xh, r


def _rms64_bwd(dy, xh, r, w):
    t = dy * w
    return r * (t - xh * jnp.mean(t * xh, axis=-1, keepdims=True)), _colsum(dy * xh)


def attn_fwd(hv, band, wq, wk, name):
    M = hv.shape[0]
    nb = M // BLOCK
    P, qw, hps = band.P, band.qw, band.hps
    d, max_dist, kv_of = band.d, band.max_dist, band.kv_of

    def body(q_ref, kp_ref, kc_ref, vp_ref, vc_ref, sl_ref, wq_ref, wk_ref, o_ref, lse_ref):
        n = pl.program_id(1)
        valid, negd = _band_mask(n, d, max_dist)
        kblk = jnp.concatenate([kp_ref[...], kc_ref[...]], axis=0)
        vblk = jnp.concatenate([vp_ref[...], vc_ref[...]], axis=0)
        wqv, wkv = wq_ref[...], wk_ref[...]
        kn_cache = {}
        for j in range(hps):
            h = kv_of(j)
            if h not in kn_cache:
                kn_cache[h] = _rms64(kblk[:, h * 64:(h + 1) * 64], wkv)[0].astype(bf16)
            kn = kn_cache[h]
            v = vblk[:, h * 64:(h + 1) * 64].astype(bf16)
            qn = _rms64(q_ref[:, j * 64:(j + 1) * 64], wqv)[0].astype(bf16)
            slope = sl_ref[0, :, j * 64:j * 64 + 1]
            s = _dot(qn, kn, NT) * (HEAD_DIM ** -0.5) + slope * negd
            s = jnp.where(valid, s, NEG)
            m = jnp.max(s, axis=-1, keepdims=True)
            p = jnp.exp(s - m)
            l = jnp.sum(p, axis=-1, keepdims=True)
            o = _dot(p.astype(bf16), v) / l
            o_ref[:, j * 64:(j + 1) * 64] = o
            lse_ref[:, j * 64:(j + 1) * 64] = jnp.broadcast_to(m + jnp.log(l), (BLOCK, 64))

    qcol, kcol, vcol = band.qcol, band.kcol, band.vcol
    in_specs = [
        pl.BlockSpec((BLOCK, qw), lambda p, n: (n, qcol(p))),
        pl.BlockSpec((BLOCK, 128), lambda p, n: (jnp.maximum(n - 1, 0), kcol(p))),
        pl.BlockSpec((BLOCK, 128), lambda p, n: (n, kcol(p))),
        pl.BlockSpec((BLOCK, 128), lambda p, n: (jnp.maximum(n - 1, 0), vcol(p))),
        pl.BlockSpec((BLOCK, 128), lambda p, n: (n, vcol(p))),
        pl.BlockSpec((1, 1, qw), lambda p, n: (p, 0, 0)),
        pl.BlockSpec((1, 64), lambda p, n: (0, 0)),
        pl.BlockSpec((1, 64), lambda p, n: (0, 0)),
    ]
    o_spec = pl.BlockSpec((BLOCK, qw), lambda p, n: (n, p))
    return pl.pallas_call(
        body, grid=(P, nb), in_specs=in_specs, out_specs=[o_spec, o_spec],
        out_shape=[jax.ShapeDtypeStruct((M, P * qw), f32)] * 2,
        compiler_params=_cp("parallel", "parallel"), name=name,
    )(hv, hv, hv, hv, hv, band.slopes, wq, wk)


def attn_bwd(hv, band, wq, wk, o, lse, do, dlse, dw0, name):
    M = hv.shape[0]
    nb = M // BLOCK
    P, qw, hps = band.P, band.qw, band.hps
    d, max_dist, kv_of = band.d, band.max_dist, band.kv_of
    kv_heads = sorted({kv_of(j) for j in range(hps)})

    def body(q_ref, kp_ref, kc_ref, vp_ref, vc_ref, sl_ref, wq_ref, wk_ref, o_ref, lse_ref, do_ref, dlse_ref,
             dwq0_ref, dwk0_ref, dq_ref, dk_ref, dv_ref, dwq_ref, dwk_ref, ck, cv):
        pp = pl.program_id(0)
        n = pl.program_id(1)

        @pl.when((pp == 0) & (n == 0))
        def _():
            dwq_ref[...] = dwq0_ref[...]
            dwk_ref[...] = dwk0_ref[...]

        @pl.when(n == 0)
        def _():
            ck[...] = jnp.zeros_like(ck)
            cv[...] = jnp.zeros_like(cv)

        @pl.when(n < nb)
        def _():
            valid, negd = _band_mask(n, d, max_dist)
            kblk = jnp.concatenate([kp_ref[...], kc_ref[...]], axis=0)
            vblk = jnp.concatenate([vp_ref[...], vc_ref[...]], axis=0)
            wqv, wkv = wq_ref[...], wk_ref[...]
            kn_c, dkn, dvv = {}, {}, {}
            dwq_acc = jnp.zeros((1, 64), f32)
            for j in range(hps):
                h = kv_of(j)
                if h not in kn_c:
                    kn_c[h] = _rms64(kblk[:, h * 64:(h + 1) * 64], wkv)
                    dkn[h] = jnp.zeros((2 * BLOCK, 64), f32)
                    dvv[h] = jnp.zeros((2 * BLOCK, 64), f32)
                kn = kn_c[h][0].astype(bf16)
                v = vblk[:, h * 64:(h + 1) * 64].astype(bf16)
                qn_f, qh, rq = _rms64(q_ref[:, j * 64:(j + 1) * 64], wqv)
                qn = qn_f.astype(bf16)
                slope = sl_ref[0, :, j * 64:j * 64 + 1]
                s = _dot(qn, kn, NT) * (HEAD_DIM ** -0.5) + slope * negd
                p = jnp.where(valid, jnp.exp(s - lse_ref[:, j * 64:j * 64 + 1]), 0.0)
                do_j = do_ref[:, j * 64:(j + 1) * 64]
                delta = jnp.sum(do_j * o_ref[:, j * 64:(j + 1) * 64], axis=-1, keepdims=True)
                dp = _dot(do_j.astype(bf16), v, NT)
                ds = (p * (dp - delta + dlse_ref[:, j * 64:j * 64 + 1])).astype(bf16)
                dqn = _dot(ds, kn) * (HEAD_DIM ** -0.5)
                dkn[h] = dkn[h] + _dot(ds, qn, TN) * (HEAD_DIM ** -0.5)
                dvv[h] = dvv[h] + _dot(p.astype(bf16), do_j.astype(bf16), TN)
                dq, dwq_j = _rms64_bwd(dqn, qh, rq, wqv)
                dwq_acc = dwq_acc + dwq_j
                dq_ref[:, j * 64:(j + 1) * 64] = dq
            dwq_ref[...] += dwq_acc
            dks, dwk_acc = [], jnp.zeros((1, 64), f32)
            for h in kv_heads:
                dk_h, dwk_h = _rms64_bwd(dkn[h], kn_c[h][1], kn_c[h][2], wkv)
                dks.append(dk_h)
                dwk_acc = dwk_acc + dwk_h
            dwk_ref[...] += dwk_acc
            dk_all = jnp.concatenate(dks, axis=1)
            dv_all = jnp.concatenate([dvv[h] for h in kv_heads], axis=1)
            dk_ref[...] = ck[...] + dk_all[:BLOCK]
            dv_ref[...] = cv[...] + dv_all[:BLOCK]
            ck[...] = dk_all[BLOCK:]
            cv[...] = dv_all[BLOCK:]

        @pl.when(n == nb)
        def _():
            dk_ref[...] = ck[...]
            dv_ref[...] = cv[...]

    qcol, kcol, vcol = band.qcol, band.kcol, band.vcol
    cl = lambda n: jnp.minimum(n, nb - 1)
    pv = lambda n: jnp.maximum(jnp.minimum(n, nb - 1) - 1, 0)
    o_in = pl.BlockSpec((BLOCK, qw), lambda p, n: (cl(n), p))
    in_specs = [
        pl.BlockSpec((BLOCK, qw), lambda p, n: (cl(n), qcol(p))),
        pl.BlockSpec((BLOCK, 128), lambda p, n: (pv(n), kcol(p))),
        pl.BlockSpec((BLOCK, 128), lambda p, n: (cl(n), kcol(p))),
        pl.BlockSpec((BLOCK, 128), lambda p, n: (pv(n), vcol(p))),
        pl.BlockSpec((BLOCK, 128), lambda p, n: (cl(n), vcol(p))),
        pl.BlockSpec((1, 1, qw), lambda p, n: (p, 0, 0)),
        pl.BlockSpec((1, 64), lambda p, n: (0, 0)),
        pl.BlockSpec((1, 64), lambda p, n: (0, 0)),
        o_in, o_in, o_in, o_in,
        pl.BlockSpec((1, 64), lambda p, n: (0, 0)),
        pl.BlockSpec((1, 64), lambda p, n: (0, 0)),
    ]
    kv_out = pl.BlockSpec((BLOCK, 128), lambda p, n: (jnp.maximum(n - 1, 0), p))
    w_out = pl.BlockSpec((1, 64), lambda p, n: (0, 0))
    return pl.pallas_call(
        body, grid=(P, nb + 1), in_specs=in_specs,
        out_specs=[o_in, kv_out, kv_out, w_out, w_out],
        out_shape=[jax.ShapeDtypeStruct((M, P * qw), f32), jax.ShapeDtypeStruct((M, P * 128), f32),
                   jax.ShapeDtypeStruct((M, P * 128), f32), jax.ShapeDtypeStruct((1, 64), f32),
                   jax.ShapeDtypeStruct((1, 64), f32)],
        scratch_shapes=[pltpu.VMEM((BLOCK, 128), f32), pltpu.VMEM((BLOCK, 128), f32)],
        compiler_params=_cp("arbitrary", "arbitrary"), name=name,
    )(hv, hv, hv, hv, hv, band.slopes, wq, wk, o, lse, do, dlse, *dw0)


def _head_sum(x):
    c = x.shape[1]
    r = lax.broadcasted_iota(jnp.int32, (c, c), 0) // HEAD_DIM
    q = lax.broadcasted_iota(jnp.int32, (c, c), 1) // HEAD_DIM
    return _dot(x, (r == q).astype(f32), precision=HI)


def attn_merge_fwd(oa, la, obs, lbs, sinkb, name):
    L = oa.shape[0]

    def fn(i, nb, oa_t, la_t, o1, o2, o3, l1, l2, l3, sk):
        ya = oa_t * jax.nn.sigmoid(la_t - sk)
        m = jnp.maximum(jnp.maximum(l1, l2), l3)
        e1, e2, e3 = jnp.exp(l1 - m), jnp.exp(l2 - m), jnp.exp(l3 - m)
        yb = (e1 * o1 + e2 * o2 + e3 * o3) / (e1 + e2 + e3)
        return jnp.concatenate([ya, yb], axis=1)

    rows = [(a, 0, 512, "cur") for a in (oa, la, *obs, *lbs)]
    return rowwise(fn, name=name, L=L, tm=_pick(L, 256, SUBLANE), rows=rows, consts=[sinkb], outs=[(1024, bf16)])


def attn_merge_bwd(dcat, oa, la, obs, lbs, sinkb, name):
    L = oa.shape[0]

    def fn(i, nb, da, db, oa_t, la_t, o1, o2, o3, l1, l2, l3, sk):
        keep = jax.nn.sigmoid(la_t - sk)
        dla = _head_sum(da * oa_t) * keep * (1.0 - keep)
        m = jnp.maximum(jnp.maximum(l1, l2), l3)
        e1, e2, e3 = jnp.exp(l1 - m), jnp.exp(l2 - m), jnp.exp(l3 - m)
        z = e1 + e2 + e3
        w1, w2, w3 = e1 / z, e2 / z, e3 / z
        g1, g2, g3 = _head_sum(db * o1), _head_sum(db * o2), _head_sum(db * o3)
        gm = w1 * g1 + w2 * g2 + w3 * g3
        return (da * keep, dla, w1 * db, w2 * db, w3 * db,
                w1 * (g1 - gm), w2 * (g2 - gm), w3 * (g3 - gm), -_colsum(dla))

    rows = [(dcat, 0, 512, "cur"), (dcat, 512, 512, "cur")] + [(a, 0, 512, "cur") for a in (oa, la, *obs, *lbs)]
    return rowwise(fn, name=name, L=L, tm=_pick(L, 256, SUBLANE), rows=rows, consts=[sinkb],
                   outs=[(512, f32)] * 8, sums=[(1, 512)])


def attn_assemble(dqa, dka, dva, dqs, dks, dvs, name):
    L = dqa.shape[0]

    def fn(i, nb, qa, ka, va, q1, q2, q3, k1, k2, k3, v1, v2, v3):
        return jnp.concatenate([qa, ka, va, q1 + q2 + q3, k1 + k2 + k3, v1 + v2 + v3], axis=1)

    rows = [(dqa, 0, 512, "cur"), (dka, 0, 128, "cur"), (dva, 0, 128, "cur")]
    rows += [(a, 0, 512, "cur") for a in (*dqs, *dks, *dvs)]
    return rowwise(fn, name=name, L=L, tm=_pick(L, 256, SUBLANE), rows=rows, outs=[(ATTN_IN, bf16)])


def attention_fwd(hin, wqa, wka, wqb, wkb, sinkb):
    L = hin.shape[0]
    oa, la = attn_fwd(hin, _Band(1, True), wqa, wka, "attn_a_fwd")
    obs, lbs = [], []
    for _, d in B_BRANCHES:
        o, l = attn_fwd(hin.reshape(L // d, d * ATTN_IN), _Band(d, False), wqb, wkb, f"attn_b{d}_fwd")
        obs.append(o.reshape(L, 512))
        lbs.append(l.reshape(L, 512))
    ocat = attn_merge_fwd(oa, la, obs, lbs, sinkb, "attn_merge_fwd")
    return ocat, (oa, la, obs, lbs)


def attention_bwd(hin, wqa, wka, wqb, wkb, sinkb, saved, dcat):
    L = hin.shape[0]
    oa, la, obs, lbs = saved
    res = attn_merge_bwd(dcat, oa, la, obs, lbs, sinkb, "attn_merge_bwd")
    doa, dla, dos, dls, dsink = res[0], res[1], res[2:5], res[5:8], res[8]
    zero = jnp.zeros((1, 64), f32)
    dqa, dka, dva, dwqa, dwka = attn_bwd(hin, _Band(1, True), wqa, wka, oa, la, doa, dla, (zero, zero), "attn_a_bwd")
    dqs, dks, dvs = [], [], []
    dwqb = dwkb = zero
    for g, (_, d) in enumerate(B_BRANCHES):
        M = L // d
        rs = lambda a: a.reshape(M, d * 512)
        dq, dk, dv, dwqb, dwkb = attn_bwd(hin.reshape(M, d * ATTN_IN), _Band(d, False), wqb, wkb, rs(obs[g]),
                                          rs(lbs[g]), rs(dos[g]), rs(dls[g]), (dwqb, dwkb), f"attn_b{d}_bwd")
        dqs.append(dq.reshape(L, 512))
        dks.append(dk.reshape(L, 512))
        dvs.append(dv.reshape(L, 512))
    dhin = attn_assemble(dqa, dka, dva, dqs, dks, dvs, "attn_assemble")
    return dhin, dwqa, dwka, dwqb, dwkb, dsink


NS = S5_GROUPS * S5_STATE


def _s5_param_fn(lr, li, ldt):
    dt = jnp.exp(ldt)
    mag, ang = jnp.exp(lr * dt), li * dt
    ab_re, ab_im = mag * jnp.cos(ang), mag * jnp.sin(ang)
    nr, ni = ab_re - 1.0, ab_im
    den = lr * lr + li * li
    return ab_re, ab_im, (nr * lr + ni * li) / den, (ni * lr - nr * li) / den


def s5_params_fwd(lr, li, ldt):
    def body(lr_ref, li_ref, ldt_ref, *outs):
        for o_ref, o in zip(outs, _s5_param_fn(lr_ref[...], li_ref[...], ldt_ref[...])):
            o_ref[...] = o

    return pl.pallas_call(body, out_shape=[jax.ShapeDtypeStruct(lr.shape, f32)] * 4, name="s5_params_fwd")(lr, li, ldt)


def s5_params_bwd(lr, li, ldt, cts):
    def body(lr_ref, li_ref, ldt_ref, c0, c1, c2, c3, dlr, dli, dldt):
        _, vjp = jax.vjp(_s5_param_fn, lr_ref[...], li_ref[...], ldt_ref[...])
        a, b, c = vjp((c0[...], c1[...], c2[...], c3[...]))
        dlr[...] = a
        dli[...] = b
        dldt[...] = c

    return pl.pallas_call(
        body, out_shape=[jax.ShapeDtypeStruct(lr.shape, f32), jax.ShapeDtypeStruct(li.shape, f32),
                         jax.ShapeDtypeStruct(ldt.shape, f32)], name="s5_params_bwd")(lr, li, ldt, *cts)


def _cmul(ar, ai, br, bi):
    return ar * br - ai * bi, ar * bi + ai * br


def s5_scan(z, ab_re, ab_im, f_re, f_im, *, reverse, name):
    L = z.shape[0]
    tm = _pick(L, 256, SUBLANE)
    nb = L // tm
    ng = tm // SUBLANE
    use_f = f_re is not None
    consts = [ab_re, ab_im] + ([f_re, f_im] if use_f else [])

    def body(*refs):
        z_ref = refs[0]
        c_refs = refs[1:1 + len(consts)]
        x_ref, car = refs[1 + len(consts)], refs[2 + len(consts)]
        i = pl.program_id(0)

        @pl.when(i == 0)
        def _():
            car[...] = jnp.zeros_like(car)

        a1 = (c_refs[0][...], c_refs[1][...])
        a2 = _cmul(*a1, *a1)
        a3 = _cmul(*a2, *a1)
        a4 = _cmul(*a2, *a2)
        pw = [a1, a2, a3, a4, _cmul(*a4, *a1), _cmul(*a4, *a2), _cmul(*a4, *a3), _cmul(*a4, *a4)]
        if reverse:
            pw = pw[::-1]
        pw_re = _stack_rows([p[0] for p in pw])
        pw_im = _stack_rows([p[1] for p in pw])
        ridx = lax.broadcasted_iota(jnp.int32, (SUBLANE, NS), 0)
        if use_f:
            fr, fi = c_refs[2][...], c_refs[3][...]

        def group(s, carry):
            cr, ci = carry
            g = (ng - 1 - s) if reverse else s
            r0 = pl.multiple_of(g * SUBLANE, SUBLANE)
            xr = z_ref[pl.ds(r0, SUBLANE), 0:NS]
            xi = z_ref[pl.ds(r0, SUBLANE), NS:2 * NS]
            if use_f:
                xr, xi = _cmul(fr, fi, xr, xi)
            for sft, (pr, pi) in ((1, a1), (2, a2), (4, a4)):
                if reverse:
                    keep = ridx < SUBLANE - sft
                    sr = jnp.where(keep, pltpu.roll(xr, SUBLANE - sft, 0), 0.0)
                    si = jnp.where(keep, pltpu.roll(xi, SUBLANE - sft, 0), 0.0)
                else:
                    keep = ridx >= sft
                    sr = jnp.where(keep, pltpu.roll(xr, sft, 0), 0.0)
                    si = jnp.where(keep, pltpu.roll(xi, sft, 0), 0.0)
                tr, ti = _cmul(pr, pi, sr, si)
                xr, xi = xr + tr, xi + ti
            tr, ti = _cmul(pw_re, pw_im, cr, ci)
            xr, xi = xr + tr, xi + ti
            x_ref[pl.ds(r0, SUBLANE), 0:NS] = xr
            x_ref[pl.ds(r0, SUBLANE), NS:2 * NS] = xi
            row = 0 if reverse else SUBLANE - 1
            return xr[row:row + 1, :], xi[row:row + 1, :]

        cr, ci = lax.fori_loop(0, ng, group, (car[0:1, 0:NS], car[0:1, NS:2 * NS]))
        car[0:1, 0:NS] = cr
        car[0:1, NS:2 * NS] = ci

    blk = (lambda i: (nb - 1 - i, 0)) if reverse else (lambda i: (i, 0))
    return pl.pallas_call(
        body, grid=(nb,),
        in_specs=[pl.BlockSpec((tm, 2 * NS), blk)] + [pl.BlockSpec((1, NS), lambda i: (0, 0))] * len(consts),
        out_specs=pl.BlockSpec((tm, 2 * NS), blk),
        out_shape=jax.ShapeDtypeStruct((L, 2 * NS), f32),
        scratch_shapes=[pltpu.VMEM((SUBLANE, 2 * NS), f32)],
        compiler_params=_cp("arbitrary"), name=name,
    )(z, *consts)


def _s5_post_fn(ypre, u, dvec, gw, gb):
    y = ypre + dvec * u
    g = jax.nn.gelu(y)
    z = _dot(g.astype(bf16), gw.astype(bf16)) + gb
    return g * jax.nn.sigmoid(z)


def s5_post_fwd(ypre, hin, dvec, gw, gb):
    L = ypre.shape[0]

    def fn(i, nb, yt, ut, dv, gwv, gbv):
        return _s5_post_fn(yt, ut, dv, gwv, gbv)

    return rowwise(fn, name="s5_post_fwd", L=L, tm=_pick(L, 512, SUBLANE),
                   rows=[(ypre, 0, S5_WIDTH, "cur"), (hin, 3072, S5_WIDTH, "cur")],
                   consts=[dvec, gw, gb], outs=[(S5_WIDTH, f32)])


def s5_post_bwd(ypre, hin, dvec, gw, gb, dycat):
    L = ypre.shape[0]

    def fn(i, nb, yt, ut, dyt, dv, gwv, gbv):
        _, vjp = jax.vjp(_s5_post_fn, yt, ut, dv, gwv, gbv)
        return vjp(dyt)

    return rowwise(fn, name="s5_post_bwd", L=L, tm=_pick(L, 512, SUBLANE),
                   rows=[(ypre, 0, S5_WIDTH, "cur"), (hin, 3072, S5_WIDTH, "cur"), (dycat, 0, S5_WIDTH, "cur")],
                   consts=[dvec, gw, gb], outs=[(S5_WIDTH, f32)] * 2,
                   sums=[(1, S5_WIDTH), (S5_WIDTH, S5_WIDTH), (1, S5_WIDTH)])


def s5_acc(G, X, bu, f_re, f_im):
    L = G.shape[0]

    def fn(i, nb, g, x, b, xp8, fr, fi):
        gr, gi = g[:, :NS], g[:, NS:]
        xp = _shift_down(x, xp8 * (i > 0).astype(f32), 1)
        xr, xi = xp[:, :NS], xp[:, NS:]
        br, bi = b[:, :NS], b[:, NS:]
        dbu = jnp.concatenate([fr * gr + fi * gi, fr * gi - fi * gr], axis=1)
        return (dbu, _colsum(xr * gr + xi * gi), _colsum(xr * gi - xi * gr),
                _colsum(br * gr + bi * gi), _colsum(br * gi - bi * gr))

    return rowwise(fn, name="s5_acc", L=L, tm=_pick(L, 256, SUBLANE),
                   rows=[(G, 0, 2 * NS, "cur"), (X, 0, 2 * NS, "cur"), (bu, 0, 2 * NS, "cur"), (X, 0, 2 * NS, "prev")],
                   consts=[f_re, f_im], outs=[(2 * NS, bf16)], sums=[(1, NS)] * 4)


def _s5_blockdiag(b_re, b_im, c_re, c_im):
    eye = jnp.eye(S5_GROUPS, dtype=f32)
    bb = lambda b: jnp.einsum("gpi,gh->gihp", b, eye).reshape(S5_WIDTH, NS)
    cc = lambda c: jnp.einsum("gip,gh->gphi", c, eye).reshape(NS, S5_WIDTH)
    return jnp.concatenate([bb(b_re), bb(b_im)], axis=1), jnp.concatenate([cc(c_re), -cc(c_im)], axis=0)


def _s5_blockdiag_grads(dB, dC):
    gb = lambda m: jnp.einsum("gigp->gpi", m.reshape(S5_GROUPS, S5_GROUP, S5_GROUPS, S5_STATE))
    gc = lambda m: jnp.einsum("gpgi->gip", m.reshape(S5_GROUPS, S5_STATE, S5_GROUPS, S5_GROUP))
    return gb(dB[:, :NS]), gb(dB[:, NS:]), gc(dC[:NS]), -gc(dC[NS:])


def s5_fwd(hin, prm):
    ab_re, ab_im, f_re, f_im = s5_params_fwd(prm["lr"], prm["li"], prm["ldt"])
    flat = lambda a: a.reshape(1, NS)
    ab_re, ab_im, f_re, f_im = flat(ab_re), flat(ab_im), flat(f_re), flat(f_im)
    Bblk, Cblk = _s5_blockdiag(prm["b_re"], prm["b_im"], prm["c_re"], prm["c_im"])
    bu = mm(hin, Bblk, name="s5_bu", a_win=(3072, S5_WIDTH))
    X = s5_scan(bu, ab_re, ab_im, f_re, f_im, reverse=False, name="s5_scan_fwd")
    ypre = mm(X, Cblk, name="s5_y")
    yc = s5_post_fwd(ypre, hin, prm["d"], prm["gw"], prm["gb"])
    return yc, (ab_re, ab_im, f_re, f_im, Bblk, Cblk, bu, X, ypre)


def s5_bwd(hin, prm, saved, dycat):
    ab_re, ab_im, f_re, f_im, Bblk, Cblk, bu, X, ypre = saved
    dypre, du_skip, dd, dgw, dgb = s5_post_bwd(ypre, hin, prm["d"], prm["gw"], prm["gb"], dycat)
    dX = mm(dypre, Cblk, tb=True, name="s5_dx")
    dC = mm(X, dypre, ta=True, name="s5_dc")
    G = s5_scan(dX, ab_re, -ab_im, None, None, reverse=True, name="s5_scan_bwd")
    dbu, dar, dai, dfr, dfi = s5_acc(G, X, bu, f_re, f_im)
    dB = mm(hin, dbu, ta=True, a_win=(3072, S5_WIDTH), name="s5_db")
    du_b = mm(dbu, Bblk, tb=True, name="s5_du")
    sh = prm["lr"].shape
    dlr, dli, dldt = s5_params_bwd(prm["lr"], prm["li"], prm["ldt"],
                                   [a.reshape(sh) for a in (dar, dai, dfr, dfi)])
    db_re, db_im, dc_re, dc_im = _s5_blockdiag_grads(dB, dC)
    grads = dict(lr=dlr, li=dli, ldt=dldt, b_re=db_re, b_im=db_im, c_re=dc_re, c_im=dc_im, d=dd, gw=dgw, gb=dgb)
    return du_skip, du_b, grads


DN_W = DN_HEADS * DN_DK
QKV_W = 3 * DN_W


def _softplus(x):
    return jnp.maximum(x, 0.0) + jnp.log(1.0 + jnp.exp(-jnp.abs(x)))


def _dn_pre(c, ab, alog, dtb):
    s = _silu(c)
    parts = []
    for h in range(2 * DN_HEADS):
        sh = s[:, h * 128:(h + 1) * 128]
        scale = DN_DK ** -0.5 if h < DN_HEADS else 1.0
        parts.append(sh * (lax.rsqrt(jnp.sum(sh * sh, axis=-1, keepdims=True) + EPS) * scale))
    parts.append(s[:, 2 * DN_W:])
    g = -jnp.exp(alog) * _softplus(ab[:, :128] + dtb)
    beta = jax.nn.sigmoid(ab[:, 128:])
    return jnp.concatenate(parts, axis=1), jnp.concatenate([g, beta], axis=1)


def _dn_pre_bwd(c, ab, alog, dtb, dqkv, dgb):
    sg = jax.nn.sigmoid(c)
    s = c * sg
    parts = []
    for h in range(2 * DN_HEADS):
        sh = s[:, h * 128:(h + 1) * 128]
        dy = dqkv[:, h * 128:(h + 1) * 128]
        scale = DN_DK ** -0.5 if h < DN_HEADS else 1.0
        r = lax.rsqrt(jnp.sum(sh * sh, axis=-1, keepdims=True) + EPS)
        parts.append(scale * r * (dy - sh * (r * r) * jnp.sum(dy * sh, axis=-1, keepdims=True)))
    parts.append(dqkv[:, 2 * DN_W:])
    dc = jnp.concatenate(parts, axis=1) * (sg * (1.0 + c * (1.0 - sg)))
    pre = ab[:, :128] + dtb
    ea = jnp.exp(alog)
    dg = dgb[:, :128]
    da = dg * (-ea) * jax.nn.sigmoid(pre)
    dalog = _colsum(dg * (-ea) * _softplus(pre))
    beta = jax.nn.sigmoid(ab[:, 128:])
    db = dgb[:, 128:] * beta * (1.0 - beta)
    return dc, jnp.concatenate([da, db], axis=1), dalog, _colsum(da)


def dn_pre_fwd(hin, conv_w, alog, dtb):
    L = hin.shape[0]

    def fn(i, nb, x, ab, p8, w, al, db):
        c = _conv_causal(x, p8 * (i > 0).astype(f32), w)
        return _dn_pre(c, ab, al, db)

    return rowwise(fn, name="dn_pre_fwd", L=L, tm=_pick(L, 256, SUBLANE),
                   rows=[(hin, 0, QKV_W, "cur"), (hin, 3328, 256, "cur"), (hin, 0, QKV_W, "prev")],
                   consts=[conv_w, alog, dtb], outs=[(QKV_W, f32), (256, f32)])


def dn_pre_bwd(hin, conv_w, alog, dtb, dqkv3, dgH, dbH):
    L = hin.shape[0]

    def fn(i, nb, x, ab, dq, dk, dv, dgh, dbh, p8, w, al, db):
        c = _conv_causal(x, p8 * (i > 0).astype(f32), w)
        lane = lax.broadcasted_iota(jnp.int32, (x.shape[0], 128), 1)
        dg = jnp.zeros((x.shape[0], 128), f32)
        dbt = jnp.zeros((x.shape[0], 128), f32)
        for h in range(DN_HEADS):
            dg = dg + jnp.where(lane == h, dgh[:, h * 128:(h + 1) * 128], 0.0)
            dbt = dbt + jnp.where(lane == h, dbh[:, h * 128:(h + 1) * 128], 0.0)
        return _dn_pre_bwd(c, ab, al, db, jnp.concatenate([dq, dk, dv], axis=1), jnp.concatenate([dg, dbt], axis=1))

    rows = [(hin, 0, QKV_W, "cur"), (hin, 3328, 256, "cur")] + [(a, 0, DN_W, "cur") for a in (*dqkv3, dgH, dbH)]
    rows.append((hin, 0, QKV_W, "prev"))
    return rowwise(fn, name="dn_pre_bwd", L=L, tm=_pick(L, 128, SUBLANE), rows=rows,
                   consts=[conv_w, alog, dtb], outs=[(QKV_W, f32), (256, f32)], sums=[(1, 128), (1, 128)])


def _dn_chunk(q, k, v, gcol, bcol, S):
    C = q.shape[0]
    r = lax.broadcasted_iota(jnp.int32, (C, C), 0)
    c = lax.broadcasted_iota(jnp.int32, (C, C), 1)
    tril = (r >= c).astype(f32)
    strict = (r > c).astype(f32)
    eye = (r == c).astype(f32)
    hd = functools.partial(_dot, precision=HI)
    grow = jnp.sum(eye * gcol, axis=0, keepdims=True)
    Gcol = jnp.sum(tril * grow, axis=1, keepdims=True)
    Grow = jnp.sum(eye * Gcol, axis=0, keepdims=True)
    gamma = jnp.exp((Gcol - Grow) * tril) * tril
    nmat = strict * bcol * hd(k, k, NT) * gamma
    T = eye - nmat
    Pw = hd(nmat, nmat)
    for _ in range(5):
        T = T + hd(T, Pw)
        Pw = hd(Pw, Pw)
    eG = jnp.exp(Gcol)
    u = hd(T, bcol * v)
    w = hd(T, (bcol * eG) * k)
    qk = hd(q, k, NT) * gamma
    vnew = u - hd(w, S)
    o = hd(q * eG, S) + hd(qk, vnew)
    Glast = jnp.sum(gcol, axis=0, keepdims=True)
    S2 = S * jnp.exp(Glast) + hd(k * jnp.exp(Glast - Gcol), vnew, TN)
    return o, S2


def _lane_col(blk, h):
    lane = lax.broadcasted_iota(jnp.int32, blk.shape, 1)
    return jnp.sum(jnp.where(lane == h, blk, 0.0), axis=1, keepdims=True)


def dn_chunks_fwd(qkvn, gb):
    L = qkvn.shape[0]
    C = DN_CHUNK
    nc = L // C

    def body(q_ref, k_ref, v_ref, g_ref, b_ref, o_ref, sin_ref, S):
        h = pl.program_id(0)
        n = pl.program_id(1)

        @pl.when(n == 0)
        def _():
            S[...] = jnp.zeros_like(S)

        s_in = S[...]
        sin_ref[...] = s_in
        o, s2 = _dn_chunk(q_ref[...], k_ref[...], v_ref[...], _lane_col(g_ref[...], h), _lane_col(b_ref[...], h), s_in)
        o_ref[...] = o
        S[...] = s2

    blk = lambda off: pl.BlockSpec((C, 128), lambda h, n, off=off: (n, off + h))
    gblk = lambda j: pl.BlockSpec((C, 128), lambda h, n, j=j: (n, j))
    return pl.pallas_call(
        body, grid=(DN_HEADS, nc),
        in_specs=[blk(0), blk(DN_HEADS), blk(2 * DN_HEADS), gblk(0), gblk(1)],
        out_specs=[pl.BlockSpec((C, 128), lambda h, n: (n, h)),
                   pl.BlockSpec((None, None, 128, 128), lambda h, n: (h, n, 0, 0))],
        out_shape=[jax.ShapeDtypeStruct((L, DN_W), f32), jax.ShapeDtypeStruct((DN_HEADS, nc, 128, 128), f32)],
        scratch_shapes=[pltpu.VMEM((128, 128), f32)],
        compiler_params=_cp("parallel", "arbitrary"), name="dn_chunks_fwd",
    )(qkvn, qkvn, qkvn, gb, gb)


def dn_chunks_bwd(qkvn, gb, s_in, do):
    L = qkvn.shape[0]
    C = DN_CHUNK
    nc = L // C

    def body(q_ref, k_ref, v_ref, g_ref, b_ref, sin_ref, do_ref, dq_ref, dk_ref, dv_ref, dg_ref, db_ref, dS):
        h = pl.program_id(0)
        n = pl.program_id(1)

        @pl.when(n == 0)
        def _():
            dS[...] = jnp.zeros_like(dS)

        args = (q_ref[...], k_ref[...], v_ref[...], _lane_col(g_ref[...], h), _lane_col(b_ref[...], h), sin_ref[...])
        _, vjp = jax.vjp(_dn_chunk, *args)
        dq, dk, dv, dg, db, ds = vjp((do_ref[...], dS[...]))
        dq_ref[...] = dq
        dk_ref[...] = dk
        dv_ref[...] = dv
        dg_ref[...] = jnp.broadcast_to(dg, (C, 128))
        db_ref[...] = jnp.broadcast_to(db, (C, 128))
        dS[...] = ds

    rv = lambda n: nc - 1 - n
    blk = lambda off: pl.BlockSpec((C, 128), lambda h, n, off=off: (rv(n), off + h))
    gblk = lambda j: pl.BlockSpec((C, 128), lambda h, n, j=j: (rv(n), j))
    oblk = pl.BlockSpec((C, 128), lambda h, n: (rv(n), h))
    return pl.pallas_call(
        body, grid=(DN_HEADS, nc),
        in_specs=[blk(0), blk(DN_HEADS), blk(2 * DN_HEADS), gblk(0), gblk(1),
                  pl.BlockSpec((None, None, 128, 128), lambda h, n: (h, rv(n), 0, 0)), oblk],
        out_specs=[oblk] * 5,
        out_shape=[jax.ShapeDtypeStruct((L, DN_W), f32)] * 5,
        scratch_shapes=[pltpu.VMEM((128, 128), f32)],
        compiler_params=_cp("parallel", "arbitrary"), name="dn_chunks_bwd",
    )(qkvn, qkvn, qkvn, gb, gb, s_in, do)


def _dn_post(o, z, w):
    parts = []
    for h in range(DN_HEADS):
        oh = o[:, h * 128:(h + 1) * 128]
        r = lax.rsqrt(jnp.mean(oh * oh, axis=-1, keepdims=True) + EPS)
        parts.append(oh * r * w)
    return jnp.concatenate(parts, axis=1) * _silu(z)


def dn_post_fwd(o, hin, yc, onorm):
    L = o.shape[0]

    def fn(i, nb, ot, zt, yct, w):
        return jnp.concatenate([yct, _dn_post(ot, zt, w)], axis=1)

    return rowwise(fn, name="dn_post_fwd", L=L, tm=_pick(L, 256, SUBLANE),
                   rows=[(o, 0, DN_W, "cur"), (hin, 2304, DN_W, "cur"), (yc, 0, S5_WIDTH, "cur")],
                   consts=[onorm], outs=[(1024, bf16)])


def dn_post_bwd(o, hin, onorm, dycat):
    L = o.shape[0]

    def fn(i, nb, ot, zt, d0, d1, d2, w):
        dy = jnp.concatenate([d0, d1, d2], axis=1)
        sg = jax.nn.sigmoid(zt)
        sz = zt * sg
        dos, dw = [], jnp.zeros((1, 128), f32)
        nrm = []
        for h in range(DN_HEADS):
            sl = slice(h * 128, (h + 1) * 128)
            oh = ot[:, sl]
            r = lax.rsqrt(jnp.mean(oh * oh, axis=-1, keepdims=True) + EPS)
            ohat = oh * r
            t = dy[:, sl] * sz[:, sl]
            dw = dw + _colsum(t * ohat)
            t = t * w
            dos.append(r * (t - ohat * jnp.mean(t * ohat, axis=-1, keepdims=True)))
            nrm.append(ohat * w)
        dz = dy * jnp.concatenate(nrm, axis=1) * (sg * (1.0 + zt * (1.0 - sg)))
        return jnp.concatenate(dos, axis=1), dz, dw

    rows = [(o, 0, DN_W, "cur"), (hin, 2304, DN_W, "cur")] + [(dycat, 256 * (1 + j), 256, "cur") for j in range(3)]
    return rowwise(fn, name="dn_post_bwd", L=L, tm=_pick(L, 256, SUBLANE), rows=rows,
                   consts=[onorm], outs=[(DN_W, f32), (DN_W, f32)], sums=[(1, 128)])


def conv_bwd_win(xarr, start, C, w, dc, name):
    L = xarr.shape[0]
    W = w.shape[0]

    def fn(i, nb, xt, dct, p8, n8, wv):
        dx = _conv_causal_bwd_x(dct, n8 * (i < nb - 1).astype(f32), wv)
        dw = _conv_causal_bwd_w(dct, xt, p8 * (i > 0).astype(f32), W)
        return dx, dw

    return rowwise(fn, name=name, L=L, tm=_pick(L, 128, SUBLANE),
                   rows=[(xarr, start, C, "cur"), (dc, 0, C, "cur"), (xarr, start, C, "prev"), (dc, 0, C, "next")],
                   consts=[w], outs=[(C, bf16)], sums=[(SUBLANE, C)])


def rec_assemble(dx_qkv, dz, du1, du2, dab):
    L = dz.shape[0]

    def fn(i, nb, a, b, c, d, e):
        return jnp.concatenate([a.astype(f32), b, c + d, e], axis=1)

    return rowwise(fn, name="rec_assemble", L=L, tm=_pick(L, 256, SUBLANE),
                   rows=[(dx_qkv, 0, QKV_W, "cur"), (dz, 0, DN_W, "cur"), (du1, 0, 256, "cur"),
                         (du2, 0, 256, "cur"), (dab, 0, 256, "cur")], outs=[(REC_PAD, bf16)])


def deltanet_fwd(hin, prm, yc):
    qkvn, gb = dn_pre_fwd(hin, prm["conv"], prm["alog"], prm["dtb"])
    o, s_in = dn_chunks_fwd(qkvn, gb)
    ycat = dn_post_fwd(o, hin, yc, prm["onorm"])
    return ycat, (qkvn, gb, o, s_in)


def deltanet_bwd(hin, prm, saved, dycat):
    qkvn, gb, o, s_in = saved
    do, dz, donorm = dn_post_bwd(o, hin, prm["onorm"], dycat)
    dq, dk, dv, dgH, dbH = dn_chunks_bwd(qkvn, gb, s_in, do)
    dc, dab, dalog, ddtb = dn_pre_bwd(hin, prm["conv"], prm["alog"], prm["dtb"], (dq, dk, dv), dgH, dbH)
    dx_qkv, dconv = conv_bwd_win(hin, 0, QKV_W, prm["conv"], dc, "dn_conv_bwd")
    return dx_qkv, dz, dab, dict(conv=dconv[:DN_CONV], alog=dalog, dtb=ddtb, onorm=donorm)


AXES = ("x", "y", "c")


def _collective(x, axes, mode, name):
    k = len(axes)
    P = 2 ** k
    shape = x.shape if mode == "gather" else x.shape[1:]

    def body(x_ref, out_ref, send_sems, recv_sems, local_sem):
        co = {a: lax.axis_index(a) for a in AXES}
        me = 0
        for a in axes:
            me = me * 2 + co[a]

        def src(j):
            return x_ref if mode == "gather" else x_ref.at[j]

        local = pltpu.make_async_copy(src(me), out_ref.at[me], local_sem)
        local.start()
        sends = []
        for m in range(1, P):
            tco = dict(co)
            t = 0
            for i, a in enumerate(axes):
                if (m >> (k - 1 - i)) & 1:
                    tco[a] = 1 - co[a]
                t = t * 2 + tco[a]
            dev = tuple(tco[a] for a in AXES)
            cp = pltpu.make_async_remote_copy(src_ref=src(t), dst_ref=out_ref.at[me], send_sem=send_sems.at[m - 1],
                                              recv_sem=recv_sems.at[m - 1], device_id=dev, device_id_type=MESH)
            cp.start()
            sends.append((cp, t, dev))
        for m, (cp, t, dev) in enumerate(sends):
            pltpu.make_async_remote_copy(src_ref=src(t), dst_ref=out_ref.at[t], send_sem=send_sems.at[m],
                                         recv_sem=recv_sems.at[m], device_id=dev, device_id_type=MESH).wait_recv()
        for cp, _, _ in sends:
            cp.wait_send()
        local.wait()

    return pl.pallas_call(
        body,
        in_specs=[pl.BlockSpec(memory_space=pl.ANY)],
        out_specs=pl.BlockSpec(memory_space=pl.ANY),
        out_shape=jax.ShapeDtypeStruct((P,) + tuple(shape), x.dtype),
        scratch_shapes=[pltpu.SemaphoreType.DMA((P - 1,)), pltpu.SemaphoreType.DMA((P - 1,)), pltpu.SemaphoreType.DMA],
        name=name,
    )(x)


def all_gather(x, axes, name):
    return _collective(x, axes, "gather", name)


def exchange(x, axes, name):
    return _collective(x, axes, "exchange", name)


def sum_slots(x, name):
    P, R, C = x.shape
    tr = _pick(R, 256, SUBLANE)

    def body(x_ref, o_ref):
        acc = x_ref[0]
        for j in range(1, P):
            acc = acc + x_ref[j]
        o_ref[...] = acc

    return pl.pallas_call(
        body, grid=(R // tr,), in_specs=[pl.BlockSpec((P, tr, C), lambda i: (0, i, 0))],
        out_specs=pl.BlockSpec((tr, C), lambda i: (i, 0)), out_shape=jax.ShapeDtypeStruct((R, C), x.dtype),
        compiler_params=_cp("parallel"), name=name,
    )(x)


def _pack(arrs, width, row_mult, dtype):
    flat = jnp.concatenate([a.astype(dtype).reshape(-1) for a in arrs])
    unit = width * row_mult
    n = -(-flat.shape[0] // unit) * unit
    return jnp.pad(flat, (0, n - flat.shape[0])).reshape(n // width, width)


def _unpack(flat, shapes):
    flat = flat.reshape(-1)
    out, off = [], 0
    for s in shapes:
        n = int(np.prod(s))
        out.append(flat[off:off + n].reshape(s))
        off += n
    return out


def ada_fwd(c_all, ada_w):
    def body(c_ref, w_ref, o_ref):
        cond = _silu(c_ref[...])
        for l in range(ada_w.shape[0]):
            o_ref[l] = _dot(cond, w_ref[l], precision=HI)

    return pl.pallas_call(body, out_shape=jax.ShapeDtypeStruct((ada_w.shape[0], c_all.shape[0], ada_w.shape[2]), f32),
                          compiler_params=pltpu.CompilerParams(vmem_limit_bytes=VMEM_LIMIT), name="ada_fwd")(c_all, ada_w)


def ada_bwd(c_all, dmod):
    def body(c_ref, d_ref, o_ref):
        cond = _silu(c_ref[...])
        for l in range(dmod.shape[0]):
            o_ref[l] = _dot(cond, d_ref[l], TN, precision=HI)

    return pl.pallas_call(body, out_shape=jax.ShapeDtypeStruct((dmod.shape[0], c_all.shape[1], dmod.shape[2]), f32),
                          compiler_params=pltpu.CompilerParams(vmem_limit_bytes=VMEM_LIMIT), name="ada_bwd")(c_all, dmod)


def loss_fwd_bwd(y, target):
    L, D = y.shape

    def fn(i, nb, yt, tt):
        e = yt - tt
        return e * (1.0 / D), jnp.sum(jnp.sum(e * e, axis=1, keepdims=True), axis=0, keepdims=True)

    return rowwise(fn, name="loss", L=L, tm=_pick(L, 512, SUBLANE), rows=[(y, 0, D, "cur"), (target, 0, D, "cur")],
                   outs=[(D, f32)], sums=[(1, 1)])


def adamw(w, g, m, v, name):
    R, C = w.shape

    def fn(i, nb, wt, gt, mt, vt):
        m2 = ADAM_B1 * mt + (1.0 - ADAM_B1) * gt
        v2 = ADAM_B2 * vt + (1.0 - ADAM_B2) * (gt * gt)
        m_hat = m2 / (1.0 - ADAM_B1 ** ADAM_STEP)
        v_hat = v2 / (1.0 - ADAM_B2 ** ADAM_STEP)
        delta = -ADAM_LR * (m_hat / (jnp.sqrt(v_hat) + ADAM_EPS) + ADAM_WD * wt)
        return delta, m2, v2

    return rowwise(fn, name=name, L=R, tm=_pick(R, 256, SUBLANE), rows=[(a, 0, C, "cur") for a in (w, g, m, v)],
                   outs=[(C, f32)] * 3)


W_NAMES = ["ada_w", "ada_b", "norm_mix", "norm_ffn", "attn_w_in", "attn_q_norm_a", "attn_k_norm_a", "attn_q_norm_b",
           "attn_k_norm_b", "attn_sinks", "attn_w_out", "rec_w_in", "s5_lambda_re", "s5_lambda_im", "s5_log_dt",
           "s5_b_re", "s5_b_im", "s5_c_re", "s5_c_im", "s5_d", "s5_glu_w", "s5_glu_b", "dn_conv", "dn_a_log",
           "dn_dt_bias", "dn_out_norm", "rec_w_out", "ffn_w_up", "ffn_conv", "ffn_w_down"]
BIG = ["attn_w_in", "attn_w_out", "rec_w_in", "rec_w_out", "ffn_w_up", "ffn_w_down"]
SMALL_SHARDED = ["s5_d", "s5_glu_w", "s5_glu_b", "dn_conv", "ffn_conv"]
SMALL_REPL = [n for n in W_NAMES if n not in BIG and n not in SMALL_SHARDED and n != "ada_w"]
NSH = 4


def _unshard(g, name):
    ax = {"attn_w_in": 2, "attn_w_out": 1, "rec_w_in": 2, "rec_w_out": 1, "ffn_w_up": 2, "ffn_w_down": 1,
          "s5_d": 1, "s5_glu_w": 1, "s5_glu_b": 1, "dn_conv": 2, "ffn_conv": 2}[name]
    g = jnp.moveaxis(g, 0, ax)
    s = g.shape
    return g.reshape(s[:ax] + (s[ax] * s[ax + 1],) + s[ax + 2:])


def _to_shards(full, name):
    ax = {"attn_w_in": 2, "attn_w_out": 1, "rec_w_in": 2, "rec_w_out": 1, "ffn_w_up": 2, "ffn_w_down": 1,
          "s5_d": 1, "s5_glu_w": 1, "s5_glu_b": 1, "dn_conv": 2, "ffn_conv": 2}[name]
    s = full.shape
    g = full.reshape(s[:ax] + (NSH, s[ax] // NSH) + s[ax + 1:])
    return jnp.moveaxis(g, ax, 0)


def _rec_pad_cols(w):
    z6 = jnp.zeros(w.shape[:-1] + (122,), w.dtype)
    return jnp.concatenate([w[..., 256:3328], w[..., 0:256], w[..., 3328:3334], z6, w[..., 3334:3340], z6], axis=-1)


def _rec_unpad_cols(g):
    return jnp.concatenate([g[..., 3072:3328], g[..., 0:3072], g[..., 3328:3334], g[..., 3456:3462]], axis=-1)


def _ffn_fwd(x1, nf, sc, sh, gate, w_up, conv, w_dn, tag):
    h2 = modulate_fwd(x1, nf, sc, sh, f"{tag}_mod2_fwd")
    up = mm(h2, w_up, name=f"{tag}_ffn_up")
    act = ffn_act_fwd(up, conv, f"{tag}_ffn_act_fwd")
    f, x2 = mm(act, w_dn, name=f"{tag}_ffn_down", out_dtypes=(f32, f32), epi=_resid_epi, epi_mn=[x1], epi_n=[gate])
    return x2, (h2, up, act, f)


def _ffn_bwd(dx, x1, nf, sc, sh, gate, w_up, conv, w_dn, saved, tag):
    h2, up, act, f = saved
    df, dgate = resid_bwd(dx, f, gate, f"{tag}_res2_bwd")
    dact = mm(df, w_dn, tb=True, name=f"{tag}_ffn_dact")
    dw_dn = mm(act, df, ta=True, name=f"{tag}_ffn_dwdown")
    dc = ffn_act_bwd1(up, conv, dact, f"{tag}_ffn_act_bwd")
    dup, dconv = conv_bwd(up, conv, dc, f"{tag}_ffn_conv_bwd")
    dw_up = mm(h2, dup, ta=True, name=f"{tag}_ffn_dwup")
    dh2 = mm(dup, w_up, tb=True, name=f"{tag}_ffn_dh")
    dx, dnf, dsc, dsh = modulate_bwd(x1, nf, sc, sh, dh2, dx, f"{tag}_mod2_bwd")
    return dx, dict(nf=dnf, sc=dsc, sh=dsh, gate=dgate, w_up=dw_up, conv=dconv[:FFN_CONV], w_dn=dw_dn)


def kernel(x, c, ada_w, ada_b, norm_mix, norm_ffn, attn_w_in, attn_q_norm_a, attn_k_norm_a, attn_q_norm_b, attn_k_norm_b, attn_sinks, attn_w_out, rec_w_in, s5_lambda_re, s5_lambda_im, s5_log_dt, s5_b_re, s5_b_im, s5_c_re, s5_c_im, s5_d, s5_glu_w, s5_glu_b, dn_conv, dn_a_log, dn_dt_bias, dn_out_norm, rec_w_out, ffn_w_up, ffn_conv, ffn_w_down, loss_target, m_ada_w, m_ada_b, m_norm_mix, m_norm_ffn, m_attn_w_in, m_attn_q_norm_a, m_attn_k_norm_a, m_attn_q_norm_b, m_attn_k_norm_b, m_attn_sinks, m_attn_w_out, m_rec_w_in, m_s5_lambda_re, m_s5_lambda_im, m_s5_log_dt, m_s5_b_re, m_s5_b_im, m_s5_c_re, m_s5_c_im, m_s5_d, m_s5_glu_w, m_s5_glu_b, m_dn_conv, m_dn_a_log, m_dn_dt_bias, m_dn_out_norm, m_rec_w_out, m_ffn_w_up, m_ffn_conv, m_ffn_w_down, v_ada_w, v_ada_b, v_norm_mix, v_norm_ffn, v_attn_w_in, v_attn_q_norm_a, v_attn_k_norm_a, v_attn_q_norm_b, v_attn_k_norm_b, v_attn_sinks, v_attn_w_out, v_rec_w_in, v_s5_lambda_re, v_s5_lambda_im, v_s5_log_dt, v_s5_b_re, v_s5_b_im, v_s5_c_re, v_s5_c_im, v_s5_d, v_s5_glu_w, v_s5_glu_b, v_dn_conv, v_dn_a_log, v_dn_dt_bias, v_dn_out_norm, v_rec_w_out, v_ffn_w_up, v_ffn_conv, v_ffn_w_down):
    args = (ada_w, ada_b, norm_mix, norm_ffn, attn_w_in, attn_q_norm_a, attn_k_norm_a, attn_q_norm_b, attn_k_norm_b, attn_sinks, attn_w_out, rec_w_in, s5_lambda_re, s5_lambda_im, s5_log_dt, s5_b_re, s5_b_im, s5_c_re, s5_c_im, s5_d, s5_glu_w, s5_glu_b, dn_conv, dn_a_log, dn_dt_bias, dn_out_norm, rec_w_out, ffn_w_up, ffn_conv, ffn_w_down)
    ms = (m_ada_w, m_ada_b, m_norm_mix, m_norm_ffn, m_attn_w_in, m_attn_q_norm_a, m_attn_k_norm_a, m_attn_q_norm_b, m_attn_k_norm_b, m_attn_sinks, m_attn_w_out, m_rec_w_in, m_s5_lambda_re, m_s5_lambda_im, m_s5_log_dt, m_s5_b_re, m_s5_b_im, m_s5_c_re, m_s5_c_im, m_s5_d, m_s5_glu_w, m_s5_glu_b, m_dn_conv, m_dn_a_log, m_dn_dt_bias, m_dn_out_norm, m_rec_w_out, m_ffn_w_up, m_ffn_conv, m_ffn_w_down)
    vs = (v_ada_w, v_ada_b, v_norm_mix, v_norm_ffn, v_attn_w_in, v_attn_q_norm_a, v_attn_k_norm_a, v_attn_q_norm_b, v_attn_k_norm_b, v_attn_sinks, v_attn_w_out, v_rec_w_in, v_s5_lambda_re, v_s5_lambda_im, v_s5_log_dt, v_s5_b_re, v_s5_b_im, v_s5_c_re, v_s5_c_im, v_s5_d, v_s5_glu_w, v_s5_glu_b, v_dn_conv, v_dn_a_log, v_dn_dt_bias, v_dn_out_norm, v_rec_w_out, v_ffn_w_up, v_ffn_conv, v_ffn_w_down)
    W = dict(zip(W_NAMES, args))
    Mo = dict(zip(W_NAMES, ms))
    Vo = dict(zip(W_NAMES, vs))
    xi, yi, ci = lax.axis_index("x"), lax.axis_index("y"), lax.axis_index("c")
    shard = 2 * xi + yi
    me8 = 4 * xi + 2 * yi + ci
    xs = x[0]
    target = loss_target[0]
    L, D = xs.shape

    wflat = _pack([W[n] for n in BIG], 1024, 32, bf16)
    R = wflat.shape[0]
    half = lax.dynamic_slice(wflat, (ci * (R // 2), 0), (R // 2, 1024))
    g4 = all_gather(half, ("x", "y"), "gather_w_xy")
    g2 = all_gather(g4, ("c",), "gather_w_c")
    wfull = jnp.concatenate([g2[0], g2[1]], axis=1).reshape(NSH, -1)
    Wf = {}
    off = 0
    for n in BIG:
        sz = int(np.prod(W[n].shape))
        Wf[n] = _unshard(wfull[:, off:off + sz].reshape((NSH,) + W[n].shape), n)
        off += sz
    rec_w_in_p = _rec_pad_cols(Wf["rec_w_in"][0])

    sflat = _pack([c] + [W[n] for n in SMALL_SHARDED], 1024, 8, f32)
    s8 = all_gather(sflat, AXES, "gather_small")
    s8f = s8.reshape(8, -1)
    c_all = s8f[:, :D]
    Ws = {}
    off = D
    for n in SMALL_SHARDED:
        sz = int(np.prod(W[n].shape))
        Ws[n] = _unshard(s8f[0::2, off:off + sz].reshape((NSH,) + W[n].shape), n)
        off += sz

    modp = ada_fwd(c_all, ada_w)
    modg = all_gather(modp, ("x", "y"), "gather_mod")
    mod_all = jnp.moveaxis(modg, 0, 2).reshape(2, 8, -1) + ada_b[:, None, :]
    mod = lax.dynamic_slice(mod_all, (0, me8, 0), (2, 1, mod_all.shape[2]))[:, 0, :]
    mods = [[mod[l:l + 1, j * D:(j + 1) * D] for j in range(6)] for l in range(2)]

    sh1, sc1, g1, sh2, sc2, g2_ = mods[0]
    nm0, nf0 = norm_mix[0:1], norm_ffn[0:1]
    sinkb = jnp.repeat(attn_sinks[0], HEAD_DIM)[None]
    h0 = modulate_fwd(xs, nm0, sc1, sh1, "l0_mod1_fwd")
    hin0 = mm(h0, Wf["attn_w_in"][0], name="l0_in_proj")
    ocat, att_saved = attention_fwd(hin0, attn_q_norm_a, attn_k_norm_a, attn_q_norm_b, attn_k_norm_b, sinkb)
    y0, x1 = mm(ocat, Wf["attn_w_out"][0], name="l0_out_proj", out_dtypes=(f32, f32), epi=_resid_epi,
                epi_mn=[xs], epi_n=[g1])
    x2, ffn0_saved = _ffn_fwd(x1, nf0, sc2, sh2, g2_, Wf["ffn_w_up"][0], Ws["ffn_conv"][0], Wf["ffn_w_down"][0], "l0")

    th1, tc1, t1, th2, tc2, t2 = mods[1]
    nm1, nf1 = norm_mix[1:2], norm_ffn[1:2]
    pad128 = lambda a: jnp.pad(a, ((0, 0), (0, 128 - a.shape[1])))
    s5p = dict(lr=s5_lambda_re[0], li=s5_lambda_im[0], ldt=s5_log_dt[0][:, None], b_re=s5_b_re[0], b_im=s5_b_im[0],
               c_re=s5_c_re[0], c_im=s5_c_im[0], d=Ws["s5_d"], gw=Ws["s5_glu_w"][0], gb=Ws["s5_glu_b"])
    dnp = dict(conv=Ws["dn_conv"][0], alog=pad128(dn_a_log), dtb=pad128(dn_dt_bias), onorm=dn_out_norm)
    h1 = modulate_fwd(x2, nm1, tc1, th1, "l1_mod1_fwd")
    hin1 = mm(h1, rec_w_in_p, name="l1_in_proj")
    yc, s5_saved = s5_fwd(hin1, s5p)
    ycat, dn_saved = deltanet_fwd(hin1, dnp, yc)
    y1, x3 = mm(ycat, Wf["rec_w_out"][0], name="l1_out_proj", out_dtypes=(f32, f32), epi=_resid_epi,
                epi_mn=[x2], epi_n=[t1])
    x4, ffn1_saved = _ffn_fwd(x3, nf1, tc2, th2, t2, Wf["ffn_w_up"][1], Ws["ffn_conv"][1], Wf["ffn_w_down"][1], "l1")

    dx, sse = loss_fwd_bwd(x4, target)
    loss = lax.psum(0.5 * sse[0, 0] / D, AXES)

    dx, gf1 = _ffn_bwd(dx, x3, nf1, tc2, th2, t2, Wf["ffn_w_up"][1], Ws["ffn_conv"][1], Wf["ffn_w_down"][1], ffn1_saved, "l1")
    dy1, dt1 = resid_bwd(dx, y1, t1, "l1_res1_bwd")
    dycat = mm(dy1, Wf["rec_w_out"][0], tb=True, name="l1_dycat")
    dw_rec_out = mm(ycat, dy1, ta=True, name="l1_dwout")
    du_skip, du_b, s5g = s5_bwd(hin1, s5p, s5_saved, dycat)
    dx_qkv, dz, dab, dng = deltanet_bwd(hin1, dnp, dn_saved, dycat)
    dhin1 = rec_assemble(dx_qkv, dz, du_skip, du_b, dab)
    dw_rec_in = _rec_unpad_cols(mm(h1, dhin1, ta=True, name="l1_dwin"))
    dh1 = mm(dhin1, rec_w_in_p, tb=True, name="l1_dh")
    dx, dnm1, dtc1, dth1 = modulate_bwd(x2, nm1, tc1, th1, dh1, dx, "l1_mod1_bwd")

    dx, gf0 = _ffn_bwd(dx, x1, nf0, sc2, sh2, g2_, Wf["ffn_w_up"][0], Ws["ffn_conv"][0], Wf["ffn_w_down"][0], ffn0_saved, "l0")
    dy0, dg1 = resid_bwd(dx, y0, g1, "l0_res1_bwd")
    dcat = mm(dy0, Wf["attn_w_out"][0], tb=True, name="l0_dcat")
    dw_attn_out = mm(ocat, dy0, ta=True, name="l0_dwout")
    dhin0, dwqa, dwka, dwqb, dwkb, dsinkb = attention_bwd(hin0, attn_q_norm_a, attn_k_norm_a, attn_q_norm_b,
                                                          attn_k_norm_b, sinkb, att_saved, dcat)
    dw_attn_in = mm(h0, dhin0, ta=True, name="l0_dwin")
    dh0 = mm(dhin0, Wf["attn_w_in"][0], tb=True, name="l0_dh")
    grad_x, dnm0, dsc1, dsh1 = modulate_bwd(xs, nm0, sc1, sh1, dh0, dx, "l0_mod1_bwd")

    dmod = jnp.concatenate([
        jnp.concatenate([dsh1, dsc1, dg1, gf0["sh"], gf0["sc"], gf0["gate"]], axis=1),
        jnp.concatenate([dth1, dtc1, dt1, gf1["sh"], gf1["sc"], gf1["gate"]], axis=1)], axis=0)
    gl = {
        "ada_b": dmod,
        "norm_mix": jnp.concatenate([dnm0, dnm1], axis=0),
        "norm_ffn": jnp.concatenate([gf0["nf"], gf1["nf"]], axis=0),
        "attn_q_norm_a": dwqa, "attn_k_norm_a": dwka, "attn_q_norm_b": dwqb, "attn_k_norm_b": dwkb,
        "attn_sinks": dsinkb[:, ::HEAD_DIM],
        "s5_lambda_re": s5g["lr"][None], "s5_lambda_im": s5g["li"][None], "s5_log_dt": s5g["ldt"][:, 0][None],
        "s5_b_re": s5g["b_re"][None], "s5_b_im": s5g["b_im"][None], "s5_c_re": s5g["c_re"][None],
        "s5_c_im": s5g["c_im"][None],
        "dn_a_log": dng["alog"][:, :DN_HEADS], "dn_dt_bias": dng["dtb"][:, :DN_HEADS], "dn_out_norm": dng["onorm"],
        "s5_d": s5g["d"], "s5_glu_w": s5g["gw"][None], "s5_glu_b": s5g["gb"], "dn_conv": dng["conv"][None],
        "ffn_conv": jnp.stack([gf0["conv"], gf1["conv"]]),
        "attn_w_in": dw_attn_in[None], "attn_w_out": dw_attn_out[None], "rec_w_in": dw_rec_in[None],
        "rec_w_out": dw_rec_out[None], "ffn_w_up": jnp.stack([gf0["w_up"], gf1["w_up"]]),
        "ffn_w_down": jnp.stack([gf0["w_dn"], gf1["w_dn"]]),
    }

    small_names = SMALL_REPL + SMALL_SHARDED
    gs = _pack([gl[n] for n in small_names], 128, 8, f32)
    gs8 = all_gather(gs, AXES, "gather_small_grads")
    gsum = sum_slots(gs8, "sum_small_grads")
    full_shapes = [gl[n].shape for n in small_names]
    gfull = dict(zip(small_names, _unpack(gsum, full_shapes)))
    dmod_all = gs8.reshape(8, -1)[:, :2 * 6 * D].reshape(8, 2, 6 * D)
    ncol = ada_w.shape[2]
    dmod_sh = jnp.moveaxis(lax.dynamic_slice(dmod_all, (0, 0, shard * ncol), (8, 2, ncol)), 0, 1)
    grads = {"ada_w": ada_bwd(c_all, dmod_sh)}
    for n in SMALL_REPL:
        grads[n] = gfull[n]
    for n in SMALL_SHARDED:
        sh_all = _to_shards(gfull[n], n)
        grads[n] = lax.dynamic_slice(sh_all, (shard,) + (0,) * (sh_all.ndim - 1), (1,) + sh_all.shape[1:])[0]

    gflat = jnp.concatenate([_to_shards(gl[n], n).reshape(NSH, -1) for n in BIG], axis=1)
    nel = gflat.shape[1]
    unit = 2 * 16 * 1024
    npad = -(-nel // unit) * unit
    gflat = jnp.pad(gflat, ((0, 0), (0, npad - nel))).reshape(NSH, 2, npad // 2048, 1024)
    gx = jnp.moveaxis(gflat, 1, 0).reshape(2, NSH * (npad // 2048), 1024)
    gc = exchange(gx, ("c",), "reduce_c")
    gpair = sum_slots(gc, "sum_pair")
    gq = exchange(gpair.reshape(NSH, npad // 2048, 1024), ("x", "y"), "reduce_xy")
    ghalf = sum_slots(gq, "sum_chips")
    gsh = all_gather(ghalf, ("c",), "gather_grad_c").reshape(-1)
    off = 0
    for n in BIG:
        sz = int(np.prod(W[n].shape))
        grads[n] = gsh[off:off + sz].reshape(W[n].shape)
        off += sz

    delta, new_m, new_v = {}, {}, {}

    def as2d(a):
        return a.reshape(-1, a.shape[-1])

    for n in ["ada_w"] + BIG:
        d_, m_, v_ = adamw(as2d(W[n]), as2d(grads[n]), as2d(Mo[n]), as2d(Vo[n]), f"adamw_{n}")
        delta[n], new_m[n], new_v[n] = d_.reshape(W[n].shape), m_.reshape(W[n].shape), v_.reshape(W[n].shape)
    pk = lambda dd: _pack([dd[n] for n in small_names], 128, 8, f32)
    d_, m_, v_ = adamw(pk(W), pk(grads), pk(Mo), pk(Vo), "adamw_small")
    shp = [W[n].shape for n in small_names]
    for dst, src in ((delta, d_), (new_m, m_), (new_v, v_)):
        dst.update(zip(small_names, _unpack(src, shp)))

    return (loss, grad_x[None], *[grads[n] for n in W_NAMES], *[delta[n] for n in W_NAMES],
            *[new_m[n] for n in W_NAMES], *[new_v[n] for n in W_NAMES])
```

```python
import functools
import math

import numpy as np
import jax
import jax.numpy as jnp
from jax import lax
from jax.experimental import pallas as pl
from jax.experimental.pallas import tpu as pltpu

f32 = jnp.float32
bf16 = jnp.bfloat16
HI = lax.Precision.HIGHEST
MESH = pl.DeviceIdType.MESH

HEAD_DIM = 64
BLOCK = 128
A_Q_HEADS = 8
A_KV_HEADS = 2
A_WINDOW = 128
B_HEADS = 8
B_BRANCHES = ((128, 1), (512, 4), (2048, 16))
N_ATTN_HEADS = 16
ATTN_IN = 2304
S5_GROUP = 16
S5_GROUPS = 16
S5_WIDTH = 256
S5_STATE = 64
DN_HEADS = 6
DN_DK = 128
DN_CONV = 4
DN_CHUNK = 64
REC_IN = 3340
REC_PAD = 3584
FFN_CONV = 3
EPS = 1e-6
ADAM_LR = 0.001
ADAM_B1 = 0.9
ADAM_B2 = 0.999
ADAM_EPS = 1e-08
ADAM_WD = 0.01
ADAM_STEP = 10

LANE = 128
SUBLANE = 8
VMEM_LIMIT = 52 * 1024 * 1024
MM_FULL_K = 5632


def _cp(*sem):
    return pltpu.CompilerParams(dimension_semantics=sem, vmem_limit_bytes=VMEM_LIMIT)


def _pick(dim, cap, unit=LANE):
    for t in (2048, 1024, 768, 512, 384, 256, 128, 64, 32, 16, 8):
        if t <= cap and t % unit == 0 and dim % t == 0:
            return t
    return dim


def _dot(a, b, dims=(((1,), (0,)), ((), ())), precision=None):
    return lax.dot_general(a, b, dims, precision=precision, preferred_element_type=f32)


NN = (((1,), (0,)), ((), ()))
NT = (((1,), (1,)), ((), ()))
TN = (((0,), (0,)), ((), ()))


def mm(a, b, *, name, ta=False, tb=False, a_win=None, b_win=None, out_dtypes=(f32,),
       epi=None, epi_mn=(), epi_n=(), tm_cap=512, tn_cap=512, tk_cap=None):
    a0, a1 = a.shape
    b0, b1 = b.shape
    aw = a_win or (0, a1)
    bw = b_win or (0, b1)
    if ta:
        K, M = a0, aw[1]
    else:
        M, K = a0, aw[1]
    if tb:
        N, K2 = b0, bw[1]
    else:
        K2, N = b0, bw[1]
    assert K == K2, (a.shape, b.shape, ta, tb, a_win, b_win)
    if tk_cap is None:
        tk_cap = K if K <= MM_FULL_K else 2048
    tm = _pick(M, tm_cap, SUBLANE if not ta else LANE)
    tn = _pick(N, tn_cap)
    tk = _pick(K, tk_cap, LANE if not ta else SUBLANE)
    if tb:
        tk = _pick(K, tk_cap)
    nk = K // tk
    if ta:
        assert aw[0] % tm == 0
        mo = aw[0] // tm
        a_spec = pl.BlockSpec((tk, tm), lambda i, j, k: (k, i + mo))
    else:
        assert aw[0] % tk == 0
        ko = aw[0] // tk
        a_spec = pl.BlockSpec((tm, tk), lambda i, j, k: (i, k + ko))
    if tb:
        assert bw[0] % tk == 0
        kob = bw[0] // tk
        b_spec = pl.BlockSpec((tn, tk), lambda i, j, k: (j, k + kob))
    else:
        assert bw[0] % tn == 0
        no = bw[0] // tn
        b_spec = pl.BlockSpec((tk, tn), lambda i, j, k: (k, j + no))
    dims = (((0 if ta else 1,), (1 if tb else 0,)), ((), ()))
    n_mn, n_n, n_out = len(epi_mn), len(epi_n), len(out_dtypes)

    def body(a_ref, b_ref, *rest):
        mn_refs = rest[:n_mn]
        n_refs = rest[n_mn:n_mn + n_n]
        out_refs = rest[n_mn + n_n:n_mn + n_n + n_out]
        acc = rest[-1]
        k = pl.program_id(2)

        @pl.when(k == 0)
        def _():
            acc[...] = jnp.zeros_like(acc)

        acc[...] += _dot(a_ref[...].astype(bf16), b_ref[...].astype(bf16), dims)

        @pl.when(k == nk - 1)
        def _():
            r = acc[...]
            if epi is None:
                outs = (r,)
            else:
                outs = epi(r, *[m[...] for m in mn_refs], *[v[...] for v in n_refs])
            for o_ref, o in zip(out_refs, outs):
                o_ref[...] = o.astype(o_ref.dtype)

    mn_spec = pl.BlockSpec((tm, tn), lambda i, j, k: (i, j))
    n_spec = pl.BlockSpec((1, tn), lambda i, j, k: (0, j))
    outs = pl.pallas_call(
        body,
        grid=(M // tm, N // tn, nk),
        in_specs=[a_spec, b_spec] + [mn_spec] * n_mn + [n_spec] * n_n,
        out_specs=[mn_spec] * n_out,
        out_shape=[jax.ShapeDtypeStruct((M, N), d) for d in out_dtypes],
        scratch_shapes=[pltpu.VMEM((tm, tn), f32)],
        compiler_params=_cp("parallel", "parallel", "arbitrary"),
        name=name,
    )(a, b, *epi_mn, *epi_n)
    return outs[0] if n_out == 1 else tuple(outs)


def rowwise(fn, *, name, L, tm, rows=(), consts=(), outs=(), sums=()):
    nb = L // tm
    hb = tm // SUBLANE
    in_specs = []
    arrs = []
    for arr, start, width, kind in rows:
        assert start % width == 0, (name, start, width)
        co = start // width
        if kind == "cur":
            in_specs.append(pl.BlockSpec((tm, width), lambda i, co=co: (i, co)))
        elif kind == "prev":
            in_specs.append(pl.BlockSpec((SUBLANE, width), lambda i, co=co: (jnp.maximum(i * hb - 1, 0), co)))
        else:
            last = L // SUBLANE - 1
            in_specs.append(pl.BlockSpec((SUBLANE, width), lambda i, co=co, last=last: (jnp.minimum((i + 1) * hb, last), co)))
        arrs.append(arr)
    for cst in consts:
        assert cst.ndim == 2
        in_specs.append(pl.BlockSpec(cst.shape, lambda i: (0, 0)))
        arrs.append(cst)
    n_rows, n_c, n_o, n_s = len(rows), len(consts), len(outs), len(sums)
    out_specs = [pl.BlockSpec((tm, w), lambda i: (i, 0)) for w, _ in outs]
    out_specs += [pl.BlockSpec(s, lambda i: (0, 0)) for s in sums]
    out_shape = [jax.ShapeDtypeStruct((L, w), d) for w, d in outs]
    out_shape += [jax.ShapeDtypeStruct(s, f32) for s in sums]

    def body(*refs):
        i = pl.program_id(0)
        vals = [r[...] for r in refs[:n_rows + n_c]]
        res = fn(i, nb, *vals)
        if not isinstance(res, (tuple, list)):
            res = (res,)
        o_refs = refs[n_rows + n_c:n_rows + n_c + n_o]
        s_refs = refs[n_rows + n_c + n_o:]
        for o_ref, o in zip(o_refs, res[:n_o]):
            o_ref[...] = o.astype(o_ref.dtype)
        if n_s:
            @pl.when(i == 0)
            def _():
                for s_ref in s_refs:
                    s_ref[...] = jnp.zeros_like(s_ref)

            for s_ref, s in zip(s_refs, res[n_o:]):
                s_ref[...] += s

    res = pl.pallas_call(
        body,
        grid=(nb,),
        in_specs=in_specs,
        out_specs=out_specs,
        out_shape=out_shape,
        compiler_params=_cp("arbitrary" if n_s else "parallel"),
        name=name,
    )(*arrs)
    return res[0] if len(res) == 1 else tuple(res)


def _shift_down(x, prev8, k):
    cat = jnp.concatenate([prev8, x], axis=0)
    return pltpu.roll(cat, k, 0)[SUBLANE:, :]


def _shift_up(x, next8, k):
    n = x.shape[0]
    cat = jnp.concatenate([x, next8], axis=0)
    return pltpu.roll(cat, n + SUBLANE - k, 0)[:n, :]


def _colsum(x):
    return jnp.sum(x, axis=0, keepdims=True)


def _silu(x):
    return x * jax.nn.sigmoid(x)


def _modulate_fn(x, nw, sc, sh):
    r = lax.rsqrt(jnp.mean(x * x, axis=-1, keepdims=True) + EPS)
    return (x * r * nw) * (1.0 + sc) + sh


def modulate_fwd(x, nw, sc, sh, name):
    L, D = x.shape

    def fn(i, nb, xt, nwv, scv, shv):
        return _modulate_fn(xt, nwv, scv, shv)

    return rowwise(fn, name=name, L=L, tm=_pick(L, 512, SUBLANE), rows=[(x, 0, D, "cur")],
                   consts=[nw, sc, sh], outs=[(D, bf16)])


def modulate_bwd(x, nw, sc, sh, dh, dx_in, name):
    L, D = x.shape

    def fn(i, nb, xt, dht, dxt, nwv, scv, shv):
        _, vjp = jax.vjp(_modulate_fn, xt, nwv, scv, shv)
        dx, dnw, dsc, dsh = vjp(dht)
        return dxt + dx, dnw, dsc, dsh

    return rowwise(fn, name=name, L=L, tm=_pick(L, 256, SUBLANE),
                   rows=[(x, 0, D, "cur"), (dh, 0, D, "cur"), (dx_in, 0, D, "cur")],
                   consts=[nw, sc, sh], outs=[(D, f32)], sums=[(1, D)] * 3)


def resid_bwd(dx, y, g, name):
    L, D = dx.shape

    def fn(i, nb, dxt, yt, gv):
        return dxt * gv, _colsum(dxt * yt)

    return rowwise(fn, name=name, L=L, tm=_pick(L, 512, SUBLANE),
                   rows=[(dx, 0, D, "cur"), (y, 0, D, "cur")], consts=[g],
                   outs=[(D, bf16)], sums=[(1, D)])


def _resid_epi(acc, xt, gv):
    return acc, xt + gv * acc


def _stack_rows(rows, n=SUBLANE):
    c = rows[0].shape[1]
    ridx = lax.broadcasted_iota(jnp.int32, (n, c), 0)
    out = jnp.zeros((n, c), f32)
    for j, r in enumerate(rows):
        out = out + jnp.where(ridx == j, r, 0.0)
    return out


def _conv_causal(x, prev8, w):
    W = w.shape[0]
    y = x * w[W - 1:W, :]
    for j in range(W - 1):
        y = y + _shift_down(x, prev8, W - 1 - j) * w[j:j + 1, :]
    return y


def _conv_causal_bwd_x(dy, next8, w):
    W = w.shape[0]
    dx = dy * w[W - 1:W, :]
    for j in range(W - 1):
        dx = dx + _shift_up(dy, next8, W - 1 - j) * w[j:j + 1, :]
    return dx


def _conv_causal_bwd_w(dy, x, prev8, W):
    rows = [_colsum(dy * _shift_down(x, prev8, W - 1 - j)) for j in range(W - 1)]
    rows.append(_colsum(dy * x))
    return _stack_rows(rows)


def ffn_act_fwd(up, conv_w, name):
    L, F2 = up.shape
    F = F2 // 2

    def fn(i, nb, u, p8, w):
        c = _conv_causal(u, p8 * (i > 0).astype(f32), w)
        return _silu(c[:, :F]) * c[:, F:]

    return rowwise(fn, name=name, L=L, tm=_pick(L, 128, SUBLANE),
                   rows=[(up, 0, F2, "cur"), (up, 0, F2, "prev")], consts=[conv_w], outs=[(F, bf16)])


def ffn_act_bwd1(up, conv_w, dact, name):
    L, F2 = up.shape
    F = F2 // 2

    def fn(i, nb, u, da, p8, w):
        c = _conv_causal(u, p8 * (i > 0).astype(f32), w)
        a, b = c[:, :F], c[:, F:]
        sg = jax.nn.sigmoid(a)
        dadt = sg * (1.0 + a * (1.0 - sg))
        return jnp.concatenate([da * b * dadt, da * a * sg], axis=1)

    return rowwise(fn, name=name, L=L, tm=_pick(L, 128, SUBLANE),
                   rows=[(up, 0, F2, "cur"), (dact, 0, F, "cur"), (up, 0, F2, "prev")],
                   consts=[conv_w], outs=[(F2, f32)])


def conv_bwd(x, w, dc, name, out_dtype=bf16):
    L, C = x.shape
    W = w.shape[0]

    def fn(i, nb, xt, dct, p8, n8, wv):
        dx = _conv_causal_bwd_x(dct, n8 * (i < nb - 1).astype(f32), wv)
        dw = _conv_causal_bwd_w(dct, xt, p8 * (i > 0).astype(f32), W)
        return dx, dw

    return rowwise(fn, name=name, L=L, tm=_pick(L, 128, SUBLANE),
                   rows=[(x, 0, C, "cur"), (dc, 0, C, "cur"), (x, 0, C, "prev"), (dc, 0, C, "next")],
                   consts=[w], outs=[(C, out_dtype)], sums=[(SUBLANE, C)])


ALIBI = [2.0 ** (-8.0 * (i + 1) / N_ATTN_HEADS) for i in range(N_ATTN_HEADS)]
NEG = -1e30


class _Band:
    def __init__(self, dilation, group_a):
        d = dilation
        self.d = d
        self.group_a = group_a
        if group_a:
            self.P, self.qw, self.hps = 1, 512, 8
            self.qcol = lambda p: 0
            self.kcol = lambda p: 4
            self.vcol = lambda p: 5
            self.kv_of = lambda j: j // 4
            self.max_dist = A_WINDOW - 1
            sl = np.repeat(np.asarray(ALIBI[:8], np.float32), HEAD_DIM)[None, None, :]
        else:
            self.P, self.qw, self.hps = 4 * d, 128, 2
            self.qcol = lambda p: (p // 4) * 18 + 6 + p % 4
            self.kcol = lambda p: (p // 4) * 18 + 10 + p % 4
            self.vcol = lambda p: (p // 4) * 18 + 14 + p % 4
            self.kv_of = lambda j: j
            self.max_dist = BLOCK
            per = np.repeat(np.asarray(ALIBI[8:], np.float32), HEAD_DIM).reshape(4, 1, 128)
            sl = np.tile(per, (d, 1, 1))
        self.slopes = jnp.asarray(sl, f32)
        self.kvw = 128


def _band_mask(n, d, max_dist):
    qi = lax.broadcasted_iota(jnp.int32, (BLOCK, 2 * BLOCK), 0)
    kj = lax.broadcasted_iota(jnp.int32, (BLOCK, 2 * BLOCK), 1)
    dist = BLOCK + qi - kj
    valid = (dist >= 0) & (dist <= max_dist) & ((n > 0) | (kj >= BLOCK))
    return valid, -(d * dist).astype(f32)


def _rms64(x, w):
    r = lax.rsqrt(jnp.mean(x * x, axis=-1, keepdims=True) + EPS)
    xh = x * r
    return xh * w, xh, r


def _rms64_bwd(dy, xh, r, w):
    t = dy * w
    return r * (t - xh * jnp.mean(t * xh, axis=-1, keepdims=True)), _colsum(dy * xh)


def attn_fwd(hv, band, wq, wk, name):
    M = hv.shape[0]
    nb = M // BLOCK
    P, qw, hps = band.P, band.qw, band.hps
    d, max_dist, kv_of = band.d, band.max_dist, band.kv_of

    def body(q_ref, kp_ref, kc_ref, vp_ref, vc_ref, sl_ref, wq_ref, wk_ref, o_ref, lse_ref):
        n = pl.program_id(1)
        valid, negd = _band_mask(n, d, max_dist)
        kblk = jnp.concatenate([kp_ref[...], kc_ref[...]], axis=0)
        vblk = jnp.concatenate([vp_ref[...], vc_ref[...]], axis=0)
        wqv, wkv = wq_ref[...], wk_ref[...]
        kn_cache = {}
        for j in range(hps):
            h = kv_of(j)
            if h not in kn_cache:
                kn_cache[h] = _rms64(kblk[:, h * 64:(h + 1) * 64], wkv)[0].astype(bf16)
            kn = kn_cache[h]
            v = vblk[:, h * 64:(h + 1) * 64].astype(bf16)
            qn = _rms64(q_ref[:, j * 64:(j + 1) * 64], wqv)[0].astype(bf16)
            slope = sl_ref[0, :, j * 64:j * 64 + 1]
            s = _dot(qn, kn, NT) * (HEAD_DIM ** -0.5) + slope * negd
            s = jnp.where(valid, s, NEG)
            m = jnp.max(s, axis=-1, keepdims=True)
            p = jnp.exp(s - m)
            l = jnp.sum(p, axis=-1, keepdims=True)
            o = _dot(p.astype(bf16), v) / l
            o_ref[:, j * 64:(j + 1) * 64] = o
            lse_ref[:, j * 64:(j + 1) * 64] = jnp.broadcast_to(m + jnp.log(l), (BLOCK, 64))

    qcol, kcol, vcol = band.qcol, band.kcol, band.vcol
    in_specs = [
        pl.BlockSpec((BLOCK, qw), lambda p, n: (n, qcol(p))),
        pl.BlockSpec((BLOCK, 128), lambda p, n: (jnp.maximum(n - 1, 0), kcol(p))),
        pl.BlockSpec((BLOCK, 128), lambda p, n: (n, kcol(p))),
        pl.BlockSpec((BLOCK, 128), lambda p, n: (jnp.maximum(n - 1, 0), vcol(p))),
        pl.BlockSpec((BLOCK, 128), lambda p, n: (n, vcol(p))),
        pl.BlockSpec((1, 1, qw), lambda p, n: (p, 0, 0)),
        pl.BlockSpec((1, 64), lambda p, n: (0, 0)),
        pl.BlockSpec((1, 64), lambda p, n: (0, 0)),
    ]
    o_spec = pl.BlockSpec((BLOCK, qw), lambda p, n: (n, p))
    return pl.pallas_call(
        body, grid=(P, nb), in_specs=in_specs, out_specs=[o_spec, o_spec],
        out_shape=[jax.ShapeDtypeStruct((M, P * qw), f32)] * 2,
        compiler_params=_cp("parallel", "parallel"), name=name,
    )(hv, hv, hv, hv, hv, band.slopes, wq, wk)


def attn_bwd(hv, band, wq, wk, o, lse, do, dlse, dw0, name):
    M = hv.shape[0]
    nb = M // BLOCK
    P, qw, hps = band.P, band.qw, band.hps
    d, max_dist, kv_of = band.d, band.max_dist, band.kv_of
    kv_heads = sorted({kv_of(j) for j in range(hps)})

    def body(q_ref, kp_ref, kc_ref, vp_ref, vc_ref, sl_ref, wq_ref, wk_ref, o_ref, lse_ref, do_ref, dlse_ref,
             dwq0_ref, dwk0_ref, dq_ref, dk_ref, dv_ref, dwq_ref, dwk_ref, ck, cv):
        pp = pl.program_id(0)
        n = pl.program_id(1)

        @pl.when((pp == 0) & (n == 0))
        def _():
            dwq_ref[...] = dwq0_ref[...]
            dwk_ref[...] = dwk0_ref[...]

        @pl.when(n == 0)
        def _():
            ck[...] = jnp.zeros_like(ck)
            cv[...] = jnp.zeros_like(cv)

        @pl.when(n < nb)
        def _():
            valid, negd = _band_mask(n, d, max_dist)
            kblk = jnp.concatenate([kp_ref[...], kc_ref[...]], axis=0)
            vblk = jnp.concatenate([vp_ref[...], vc_ref[...]], axis=0)
            wqv, wkv = wq_ref[...], wk_ref[...]
            kn_c, dkn, dvv = {}, {}, {}
            dwq_acc = jnp.zeros((1, 64), f32)
            for j in range(hps):
                h = kv_of(j)
                if h not in kn_c:
                    kn_c[h] = _rms64(kblk[:, h * 64:(h + 1) * 64], wkv)
                    dkn[h] = jnp.zeros((2 * BLOCK, 64), f32)
                    dvv[h] = jnp.zeros((2 * BLOCK, 64), f32)
                kn = kn_c[h][0].astype(bf16)
                v = vblk[:, h * 64:(h + 1) * 64].astype(bf16)
                qn_f, qh, rq = _rms64(q_ref[:, j * 64:(j + 1) * 64], wqv)
                qn = qn_f.astype(bf16)
                slope = sl_ref[0, :, j * 64:j * 64 + 1]
                s = _dot(qn, kn, NT) * (HEAD_DIM ** -0.5) + slope * negd
                p = jnp.where(valid, jnp.exp(s - lse_ref[:, j * 64:j * 64 + 1]), 0.0)
                do_j = do_ref[:, j * 64:(j + 1) * 64]
                delta = jnp.sum(do_j * o_ref[:, j * 64:(j + 1) * 64], axis=-1, keepdims=True)
                dp = _dot(do_j.astype(bf16), v, NT)
                ds = (p * (dp - delta + dlse_ref[:, j * 64:j * 64 + 1])).astype(bf16)
                dqn = _dot(ds, kn) * (HEAD_DIM ** -0.5)
                dkn[h] = dkn[h] + _dot(ds, qn, TN) * (HEAD_DIM ** -0.5)
                dvv[h] = dvv[h] + _dot(p.astype(bf16), do_j.astype(bf16), TN)
                dq, dwq_j = _rms64_bwd(dqn, qh, rq, wqv)
                dwq_acc = dwq_acc + dwq_j
                dq_ref[:, j * 64:(j + 1) * 64] = dq
            dwq_ref[...] += dwq_acc
            dks, dwk_acc = [], jnp.zeros((1, 64), f32)
            for h in kv_heads:
                dk_h, dwk_h = _rms64_bwd(dkn[h], kn_c[h][1], kn_c[h][2], wkv)
                dks.append(dk_h)
                dwk_acc = dwk_acc + dwk_h
            dwk_ref[...] += dwk_acc
            dk_all = jnp.concatenate(dks, axis=1)
            dv_all = jnp.concatenate([dvv[h] for h in kv_heads], axis=1)
            dk_ref[...] = ck[...] + dk_all[:BLOCK]
            dv_ref[...] = cv[...] + dv_all[:BLOCK]
            ck[...] = dk_all[BLOCK:]
            cv[...] = dv_all[BLOCK:]

        @pl.when(n == nb)
        def _():
            dk_ref[...] = ck[...]
            dv_ref[...] = cv[...]

    qcol, kcol, vcol = band.qcol, band.kcol, band.vcol
    cl = lambda n: jnp.minimum(n, nb - 1)
    pv = lambda n: jnp.maximum(jnp.minimum(n, nb - 1) - 1, 0)
    o_in = pl.BlockSpec((BLOCK, qw), lambda p, n: (cl(n), p))
    in_specs = [
        pl.BlockSpec((BLOCK, qw), lambda p, n: (cl(n), qcol(p))),
        pl.BlockSpec((BLOCK, 128), lambda p, n: (pv(n), kcol(p))),
        pl.BlockSpec((BLOCK, 128), lambda p, n: (cl(n), kcol(p))),
        pl.BlockSpec((BLOCK, 128), lambda p, n: (pv(n), vcol(p))),
        pl.BlockSpec((BLOCK, 128), lambda p, n: (cl(n), vcol(p))),
        pl.BlockSpec((1, 1, qw), lambda p, n: (p, 0, 0)),
        pl.BlockSpec((1, 64), lambda p, n: (0, 0)),
        pl.BlockSpec((1, 64), lambda p, n: (0, 0)),
        o_in, o_in, o_in, o_in,
        pl.BlockSpec((1, 64), lambda p, n: (0, 0)),
        pl.BlockSpec((1, 64), lambda p, n: (0, 0)),
    ]
    kv_out = pl.BlockSpec((BLOCK, 128), lambda p, n: (jnp.maximum(n - 1, 0), p))
    w_out = pl.BlockSpec((1, 64), lambda p, n: (0, 0))
    return pl.pallas_call(
        body, grid=(P, nb + 1), in_specs=in_specs,
        out_specs=[o_in, kv_out, kv_out, w_out, w_out],
        out_shape=[jax.ShapeDtypeStruct((M, P * qw), f32), jax.ShapeDtypeStruct((M, P * 128), f32),
                   jax.ShapeDtypeStruct((M, P * 128), f32), jax.ShapeDtypeStruct((1, 64), f32),
                   jax.ShapeDtypeStruct((1, 64), f32)],
        scratch_shapes=[pltpu.VMEM((BLOCK, 128), f32), pltpu.VMEM((BLOCK, 128), f32)],
        compiler_params=_cp("arbitrary", "arbitrary"), name=name,
    )(hv, hv, hv, hv, hv, band.slopes, wq, wk, o, lse, do, dlse, *dw0)


def _head_sum(x):
    c = x.shape[1]
    r = lax.broadcasted_iota(jnp.int32, (c, c), 0) // HEAD_DIM
    q = lax.broadcasted_iota(jnp.int32, (c, c), 1) // HEAD_DIM
    return _dot(x, (r == q).astype(f32), precision=HI)


def attn_merge_fwd(oa, la, obs, lbs, sinkb, name):
    L = oa.shape[0]

    def fn(i, nb, oa_t, la_t, o1, o2, o3, l1, l2, l3, sk):
        ya = oa_t * jax.nn.sigmoid(la_t - sk)
        m = jnp.maximum(jnp.maximum(l1, l2), l3)
        e1, e2, e3 = jnp.exp(l1 - m), jnp.exp(l2 - m), jnp.exp(l3 - m)
        yb = (e1 * o1 + e2 * o2 + e3 * o3) / (e1 + e2 + e3)
        return jnp.concatenate([ya, yb], axis=1)

    rows = [(a, 0, 512, "cur") for a in (oa, la, *obs, *lbs)]
    return rowwise(fn, name=name, L=L, tm=_pick(L, 256, SUBLANE), rows=rows, consts=[sinkb], outs=[(1024, bf16)])


def attn_merge_bwd(dcat, oa, la, obs, lbs, sinkb, name):
    L = oa.shape[0]

    def fn(i, nb, da, db, oa_t, la_t, o1, o2, o3, l1, l2, l3, sk):
        keep = jax.nn.sigmoid(la_t - sk)
        dla = _head_sum(da * oa_t) * keep * (1.0 - keep)
        m = jnp.maximum(jnp.maximum(l1, l2), l3)
        e1, e2, e3 = jnp.exp(l1 - m), jnp.exp(l2 - m), jnp.exp(l3 - m)
        z = e1 + e2 + e3
        w1, w2, w3 = e1 / z, e2 / z, e3 / z
        g1, g2, g3 = _head_sum(db * o1), _head_sum(db * o2), _head_sum(db * o3)
        gm = w1 * g1 + w2 * g2 + w3 * g3
        return (da * keep, dla, w1 * db, w2 * db, w3 * db,
                w1 * (g1 - gm), w2 * (g2 - gm), w3 * (g3 - gm), -_colsum(dla))

    rows = [(dcat, 0, 512, "cur"), (dcat, 512, 512, "cur")] + [(a, 0, 512, "cur") for a in (oa, la, *obs, *lbs)]
    return rowwise(fn, name=name, L=L, tm=_pick(L, 256, SUBLANE), rows=rows, consts=[sinkb],
                   outs=[(512, f32)] * 8, sums=[(1, 512)])


def attn_assemble(dqa, dka, dva, dqs, dks, dvs, name):
    L = dqa.shape[0]

    def fn(i, nb, qa, ka, va, q1, q2, q3, k1, k2, k3, v1, v2, v3):
        return jnp.concatenate([qa, ka, va, q1 + q2 + q3, k1 + k2 + k3, v1 + v2 + v3], axis=1)

    rows = [(dqa, 0, 512, "cur"), (dka, 0, 128, "cur"), (dva, 0, 128, "cur")]
    rows += [(a, 0, 512, "cur") for a in (*dqs, *dks, *dvs)]
    return rowwise(fn, name=name, L=L, tm=_pick(L, 256, SUBLANE), rows=rows, outs=[(ATTN_IN, bf16)])


def attention_fwd(hin, wqa, wka, wqb, wkb, sinkb):
    L = hin.shape[0]
    oa, la = attn_fwd(hin, _Band(1, True), wqa, wka, "attn_a_fwd")
    obs, lbs = [], []
    for _, d in B_BRANCHES:
        o, l = attn_fwd(hin.reshape(L // d, d * ATTN_IN), _Band(d, False), wqb, wkb, f"attn_b{d}_fwd")
        obs.append(o.reshape(L, 512))
        lbs.append(l.reshape(L, 512))
    ocat = attn_merge_fwd(oa, la, obs, lbs, sinkb, "attn_merge_fwd")
    return ocat, (oa, la, obs, lbs)


def attention_bwd(hin, wqa, wka, wqb, wkb, sinkb, saved, dcat):
    L = hin.shape[0]
    oa, la, obs, lbs = saved
    res = attn_merge_bwd(dcat, oa, la, obs, lbs, sinkb, "attn_merge_bwd")
    doa, dla, dos, dls, dsink = res[0], res[1], res[2:5], res[5:8], res[8]
    zero = jnp.zeros((1, 64), f32)
    dqa, dka, dva, dwqa, dwka = attn_bwd(hin, _Band(1, True), wqa, wka, oa, la, doa, dla, (zero, zero), "attn_a_bwd")
    dqs, dks, dvs = [], [], []
    dwqb = dwkb = zero
    for g, (_, d) in enumerate(B_BRANCHES):
        M = L // d
        rs = lambda a: a.reshape(M, d * 512)
        dq, dk, dv, dwqb, dwkb = attn_bwd(hin.reshape(M, d * ATTN_IN), _Band(d, False), wqb, wkb, rs(obs[g]),
                                          rs(lbs[g]), rs(dos[g]), rs(dls[g]), (dwqb, dwkb), f"attn_b{d}_bwd")
        dqs.append(dq.reshape(L, 512))
        dks.append(dk.reshape(L, 512))
        dvs.append(dv.reshape(L, 512))
    dhin = attn_assemble(dqa, dka, dva, dqs, dks, dvs, "attn_assemble")
    return dhin, dwqa, dwka, dwqb, dwkb, dsink


NS = S5_GROUPS * S5_STATE


def _s5_param_fn(lr, li, ldt):
    dt = jnp.exp(ldt)
    mag, ang = jnp.exp(lr * dt), li * dt
    ab_re, ab_im = mag * jnp.cos(ang), mag * jnp.sin(ang)
    nr, ni = ab_re - 1.0, ab_im
    den = lr * lr + li * li
    return ab_re, ab_im, (nr * lr + ni * li) / den, (ni * lr - nr * li) / den


def s5_params_fwd(lr, li, ldt):
    def body(lr_ref, li_ref, ldt_ref, *outs):
        for o_ref, o in zip(outs, _s5_param_fn(lr_ref[...], li_ref[...], ldt_ref[...])):
            o_ref[...] = o

    return pl.pallas_call(body, out_shape=[jax.ShapeDtypeStruct(lr.shape, f32)] * 4, name="s5_params_fwd")(lr, li, ldt)


def s5_params_bwd(lr, li, ldt, cts):
    def body(lr_ref, li_ref, ldt_ref, c0, c1, c2, c3, dlr, dli, dldt):
        _, vjp = jax.vjp(_s5_param_fn, lr_ref[...], li_ref[...], ldt_ref[...])
        a, b, c = vjp((c0[...], c1[...], c2[...], c3[...]))
        dlr[...] = a
        dli[...] = b
        dldt[...] = c

    return pl.pallas_call(
        body, out_shape=[jax.ShapeDtypeStruct(lr.shape, f32), jax.ShapeDtypeStruct(li.shape, f32),
                         jax.ShapeDtypeStruct(ldt.shape, f32)], name="s5_params_bwd")(lr, li, ldt, *cts)


def _cmul(ar, ai, br, bi):
    return ar * br - ai * bi, ar * bi + ai * br


def s5_scan(z, ab_re, ab_im, f_re, f_im, *, reverse, name):
    L = z.shape[0]
    tm = _pick(L, 256, SUBLANE)
    nb = L // tm
    ng = tm // SUBLANE
    use_f = f_re is not None
    consts = [ab_re, ab_im] + ([f_re, f_im] if use_f else [])

    def body(*refs):
        z_ref = refs[0]
        c_refs = refs[1:1 + len(consts)]
        x_ref, car = refs[1 + len(consts)], refs[2 + len(consts)]
        i = pl.program_id(0)

        @pl.when(i == 0)
        def _():
            car[...] = jnp.zeros_like(car)

        a1 = (c_refs[0][...], c_refs[1][...])
        a2 = _cmul(*a1, *a1)
        a3 = _cmul(*a2, *a1)
        a4 = _cmul(*a2, *a2)
        pw = [a1, a2, a3, a4, _cmul(*a4, *a1), _cmul(*a4, *a2), _cmul(*a4, *a3), _cmul(*a4, *a4)]
        if reverse:
            pw = pw[::-1]
        pw_re = _stack_rows([p[0] for p in pw])
        pw_im = _stack_rows([p[1] for p in pw])
        ridx = lax.broadcasted_iota(jnp.int32, (SUBLANE, NS), 0)
        if use_f:
            fr, fi = c_refs[2][...], c_refs[3][...]

        def group(s, carry):
            cr, ci = carry
            g = (ng - 1 - s) if reverse else s
            r0 = pl.multiple_of(g * SUBLANE, SUBLANE)
            xr = z_ref[pl.ds(r0, SUBLANE), 0:NS]
            xi = z_ref[pl.ds(r0, SUBLANE), NS:2 * NS]
            if use_f:
                xr, xi = _cmul(fr, fi, xr, xi)
            for sft, (pr, pi) in ((1, a1), (2, a2), (4, a4)):
                if reverse:
                    keep = ridx < SUBLANE - sft
                    sr = jnp.where(keep, pltpu.roll(xr, SUBLANE - sft, 0), 0.0)
                    si = jnp.where(keep, pltpu.roll(xi, SUBLANE - sft, 0), 0.0)
                else:
                    keep = ridx >= sft
                    sr = jnp.where(keep, pltpu.roll(xr, sft, 0), 0.0)
                    si = jnp.where(keep, pltpu.roll(xi, sft, 0), 0.0)
                tr, ti = _cmul(pr, pi, sr, si)
                xr, xi = xr + tr, xi + ti
            tr, ti = _cmul(pw_re, pw_im, cr, ci)
            xr, xi = xr + tr, xi + ti
            x_ref[pl.ds(r0, SUBLANE), 0:NS] = xr
            x_ref[pl.ds(r0, SUBLANE), NS:2 * NS] = xi
            row = 0 if reverse else SUBLANE - 1
            return xr[row:row + 1, :], xi[row:row + 1, :]

        cr, ci = lax.fori_loop(0, ng, group, (car[0:1, 0:NS], car[0:1, NS:2 * NS]))
        car[0:1, 0:NS] = cr
        car[0:1, NS:2 * NS] = ci

    blk = (lambda i: (nb - 1 - i, 0)) if reverse else (lambda i: (i, 0))
    return pl.pallas_call(
        body, grid=(nb,),
        in_specs=[pl.BlockSpec((tm, 2 * NS), blk)] + [pl.BlockSpec((1, NS), lambda i: (0, 0))] * len(consts),
        out_specs=pl.BlockSpec((tm, 2 * NS), blk),
        out_shape=jax.ShapeDtypeStruct((L, 2 * NS), f32),
        scratch_shapes=[pltpu.VMEM((SUBLANE, 2 * NS), f32)],
        compiler_params=_cp("arbitrary"), name=name,
    )(z, *consts)


def _s5_post_fn(ypre, u, dvec, gw, gb):
    y = ypre + dvec * u
    g = jax.nn.gelu(y)
    z = _dot(g.astype(bf16), gw.astype(bf16)) + gb
    return g * jax.nn.sigmoid(z)


def s5_post_fwd(ypre, hin, dvec, gw, gb):
    L = ypre.shape[0]

    def fn(i, nb, yt, ut, dv, gwv, gbv):
        return _s5_post_fn(yt, ut, dv, gwv, gbv)

    return rowwise(fn, name="s5_post_fwd", L=L, tm=_pick(L, 512, SUBLANE),
                   rows=[(ypre, 0, S5_WIDTH, "cur"), (hin, 3072, S5_WIDTH, "cur")],
                   consts=[dvec, gw, gb], outs=[(S5_WIDTH, f32)])


def s5_post_bwd(ypre, hin, dvec, gw, gb, dycat):
    L = ypre.shape[0]

    def fn(i, nb, yt, ut, dyt, dv, gwv, gbv):
        _, vjp = jax.vjp(_s5_post_fn, yt, ut, dv, gwv, gbv)
        return vjp(dyt)

    return rowwise(fn, name="s5_post_bwd", L=L, tm=_pick(L, 512, SUBLANE),
                   rows=[(ypre, 0, S5_WIDTH, "cur"), (hin, 3072, S5_WIDTH, "cur"), (dycat, 0, S5_WIDTH, "cur")],
                   consts=[dvec, gw, gb], outs=[(S5_WIDTH, f32)] * 2,
                   sums=[(1, S5_WIDTH), (S5_WIDTH, S5_WIDTH), (1, S5_WIDTH)])


def s5_acc(G, X, bu, f_re, f_im):
    L = G.shape[0]

    def fn(i, nb, g, x, b, xp8, fr, fi):
        gr, gi = g[:, :NS], g[:, NS:]
        xp = _shift_down(x, xp8 * (i > 0).astype(f32), 1)
        xr, xi = xp[:, :NS], xp[:, NS:]
        br, bi = b[:, :NS], b[:, NS:]
        dbu = jnp.concatenate([fr * gr + fi * gi, fr * gi - fi * gr], axis=1)
        return (dbu, _colsum(xr * gr + xi * gi), _colsum(xr * gi - xi * gr),
                _colsum(br * gr + bi * gi), _colsum(br * gi - bi * gr))

    return rowwise(fn, name="s5_acc", L=L, tm=_pick(L, 256, SUBLANE),
                   rows=[(G, 0, 2 * NS, "cur"), (X, 0, 2 * NS, "cur"), (bu, 0, 2 * NS, "cur"), (X, 0, 2 * NS, "prev")],
                   consts=[f_re, f_im], outs=[(2 * NS, bf16)], sums=[(1, NS)] * 4)


def _s5_blockdiag(b_re, b_im, c_re, c_im):
    eye = jnp.eye(S5_GROUPS, dtype=f32)
    bb = lambda b: jnp.einsum("gpi,gh->gihp", b, eye).reshape(S5_WIDTH, NS)
    cc = lambda c: jnp.einsum("gip,gh->gphi", c, eye).reshape(NS, S5_WIDTH)
    return jnp.concatenate([bb(b_re), bb(b_im)], axis=1), jnp.concatenate([cc(c_re), -cc(c_im)], axis=0)


def _s5_blockdiag_grads(dB, dC):
    gb = lambda m: jnp.einsum("gigp->gpi", m.reshape(S5_GROUPS, S5_GROUP, S5_GROUPS, S5_STATE))
    gc = lambda m: jnp.einsum("gpgi->gip", m.reshape(S5_GROUPS, S5_STATE, S5_GROUPS, S5_GROUP))
    return gb(dB[:, :NS]), gb(dB[:, NS:]), gc(dC[:NS]), -gc(dC[NS:])


def s5_fwd(hin, prm):
    ab_re, ab_im, f_re, f_im = s5_params_fwd(prm["lr"], prm["li"], prm["ldt"])
    flat = lambda a: a.reshape(1, NS)
    ab_re, ab_im, f_re, f_im = flat(ab_re), flat(ab_im), flat(f_re), flat(f_im)
    Bblk, Cblk = _s5_blockdiag(prm["b_re"], prm["b_im"], prm["c_re"], prm["c_im"])
    bu = mm(hin, Bblk, name="s5_bu", a_win=(3072, S5_WIDTH))
    X = s5_scan(bu, ab_re, ab_im, f_re, f_im, reverse=False, name="s5_scan_fwd")
    ypre = mm(X, Cblk, name="s5_y")
    yc = s5_post_fwd(ypre, hin, prm["d"], prm["gw"], prm["gb"])
    return yc, (ab_re, ab_im, f_re, f_im, Bblk, Cblk, bu, X, ypre)


def s5_bwd(hin, prm, saved, dycat):
    ab_re, ab_im, f_re, f_im, Bblk, Cblk, bu, X, ypre = saved
    dypre, du_skip, dd, dgw, dgb = s5_post_bwd(ypre, hin, prm["d"], prm["gw"], prm["gb"], dycat)
    dX = mm(dypre, Cblk, tb=True, name="s5_dx")
    dC = mm(X, dypre, ta=True, name="s5_dc")
    G = s5_scan(dX, ab_re, -ab_im, None, None, reverse=True, name="s5_scan_bwd")
    dbu, dar, dai, dfr, dfi = s5_acc(G, X, bu, f_re, f_im)
    dB = mm(hin, dbu, ta=True, a_win=(3072, S5_WIDTH), name="s5_db")
    du_b = mm(dbu, Bblk, tb=True, name="s5_du")
    sh = prm["lr"].shape
    dlr, dli, dldt = s5_params_bwd(prm["lr"], prm["li"], prm["ldt"],
                                   [a.reshape(sh) for a in (dar, dai, dfr, dfi)])
    db_re, db_im, dc_re, dc_im = _s5_blockdiag_grads(dB, dC)
    grads = dict(lr=dlr, li=dli, ldt=dldt, b_re=db_re, b_im=db_im, c_re=dc_re, c_im=dc_im, d=dd, gw=dgw, gb=dgb)
    return du_skip, du_b, grads


DN_W = DN_HEADS * DN_DK
QKV_W = 3 * DN_W


def _softplus(x):
    return jnp.maximum(x, 0.0) + jnp.log(1.0 + jnp.exp(-jnp.abs(x)))


def _dn_pre(c, ab, alog, dtb):
    s = _silu(c)
    parts = []
    for h in range(2 * DN_HEADS):
        sh = s[:, h * 128:(h + 1) * 128]
        scale = DN_DK ** -0.5 if h < DN_HEADS else 1.0
        parts.append(sh * (lax.rsqrt(jnp.sum(sh * sh, axis=-1, keepdims=True) + EPS) * scale))
    parts.append(s[:, 2 * DN_W:])
    g = -jnp.exp(alog) * _softplus(ab[:, :128] + dtb)
    beta = jax.nn.sigmoid(ab[:, 128:])
    return jnp.concatenate(parts, axis=1), jnp.concatenate([g, beta], axis=1)


def _dn_pre_bwd(c, ab, alog, dtb, dqkv, dgb):
    sg = jax.nn.sigmoid(c)
    s = c * sg
    parts = []
    for h in range(2 * DN_HEADS):
        sh = s[:, h * 128:(h + 1) * 128]
        dy = dqkv[:, h * 128:(h + 1) * 128]
        scale = DN_DK ** -0.5 if h < DN_HEADS else 1.0
        r = lax.rsqrt(jnp.sum(sh * sh, axis=-1, keepdims=True) + EPS)
        parts.append(scale * r * (dy - sh * (r * r) * jnp.sum(dy * sh, axis=-1, keepdims=True)))
    parts.append(dqkv[:, 2 * DN_W:])
    dc = jnp.concatenate(parts, axis=1) * (sg * (1.0 + c * (1.0 - sg)))
    pre = ab[:, :128] + dtb
    ea = jnp.exp(alog)
    dg = dgb[:, :128]
    da = dg * (-ea) * jax.nn.sigmoid(pre)
    dalog = _colsum(dg * (-ea) * _softplus(pre))
    beta = jax.nn.sigmoid(ab[:, 128:])
    db = dgb[:, 128:] * beta * (1.0 - beta)
    return dc, jnp.concatenate([da, db], axis=1), dalog, _colsum(da)


def dn_pre_fwd(hin, conv_w, alog, dtb):
    L = hin.shape[0]

    def fn(i, nb, x, ab, p8, w, al, db):
        c = _conv_causal(x, p8 * (i > 0).astype(f32), w)
        return _dn_pre(c, ab, al, db)

    return rowwise(fn, name="dn_pre_fwd", L=L, tm=_pick(L, 256, SUBLANE),
                   rows=[(hin, 0, QKV_W, "cur"), (hin, 3328, 256, "cur"), (hin, 0, QKV_W, "prev")],
                   consts=[conv_w, alog, dtb], outs=[(QKV_W, f32), (256, f32)])


def dn_pre_bwd(hin, conv_w, alog, dtb, dqkv3, dg, dbeta):
    L = hin.shape[0]

    def fn(i, nb, x, ab, dq, dk, dv, dgt, dbt, p8, w, al, db):
        c = _conv_causal(x, p8 * (i > 0).astype(f32), w)
        return _dn_pre_bwd(c, ab, al, db, jnp.concatenate([dq, dk, dv], axis=1), jnp.concatenate([dgt, dbt], axis=1))

    rows = [(hin, 0, QKV_W, "cur"), (hin, 3328, 256, "cur")] + [(a, 0, DN_W, "cur") for a in dqkv3]
    rows += [(dg, 0, 128, "cur"), (dbeta, 0, 128, "cur"), (hin, 0, QKV_W, "prev")]
    return rowwise(fn, name="dn_pre_bwd", L=L, tm=_pick(L, 128, SUBLANE), rows=rows,
                   consts=[conv_w, alog, dtb], outs=[(QKV_W, f32), (256, f32)], sums=[(1, 128), (1, 128)])


def _split(a):
    hi = a.astype(bf16)
    return hi, (a - hi.astype(f32)).astype(bf16)


def _dot3_raw(a, b, dims):
    ah, al = _split(a)
    bh, bl = _split(b)
    return _dot(ah, bh, dims) + (_dot(ah, bl, dims) + _dot(al, bh, dims))


@functools.partial(jax.custom_vjp, nondiff_argnums=(2,))
def _dot3(a, b, dims=NN):
    return _dot3_raw(a, b, dims)


def _dot3_fwd(a, b, dims):
    return _dot3_raw(a, b, dims), (a, b)


def _dot3_bwd(dims, res, g):
    a, b = res
    if dims == NN:
        return _dot3_raw(g, b, NT), _dot3_raw(a, g, TN)
    if dims == NT:
        return _dot3_raw(g, b, NN), _dot3_raw(g, a, TN)
    assert dims == TN
    return _dot3_raw(b, g, NT), _dot3_raw(a, g, NN)


_dot3.defvjp(_dot3_fwd, _dot3_bwd)


def _dn_chunk(q, k, v, gcol, bcol, S):
    C = q.shape[0]
    r = lax.broadcasted_iota(jnp.int32, (C, C), 0)
    c = lax.broadcasted_iota(jnp.int32, (C, C), 1)
    tril = (r >= c).astype(f32)
    strict = (r > c).astype(f32)
    eye = (r == c).astype(f32)
    hd = _dot3
    grow = jnp.sum(eye * gcol, axis=0, keepdims=True)
    Gcol = jnp.sum(tril * grow, axis=1, keepdims=True)
    Grow = jnp.sum(eye * Gcol, axis=0, keepdims=True)
    gamma = jnp.exp((Gcol - Grow) * tril) * tril
    nmat = strict * bcol * hd(k, k, NT) * gamma
    T = eye - nmat
    Pw = hd(nmat, nmat)
    for _ in range(5):
        T = T + hd(T, Pw)
        Pw = hd(Pw, Pw)
    eG = jnp.exp(Gcol)
    u = hd(T, bcol * v)
    w = hd(T, (bcol * eG) * k)
    qk = hd(q, k, NT) * gamma
    vnew = u - hd(w, S)
    o = hd(q * eG, S) + hd(qk, vnew)
    Glast = jnp.sum(gcol, axis=0, keepdims=True)
    S2 = S * jnp.exp(Glast) + hd(k * jnp.exp(Glast - Gcol), vnew, TN)
    return o, S2


def dn_chunks_fwd(qkvn, gb):
    L = qkvn.shape[0]
    C = DN_CHUNK
    nc = L // C

    def body(q_ref, k_ref, v_ref, g_ref, b_ref, o_ref, sin_ref, S):
        n = pl.program_id(0)

        @pl.when(n == 0)
        def _():
            S[...] = jnp.zeros_like(S)

        for h in range(DN_HEADS):
            sl = slice(h * 128, (h + 1) * 128)
            s_in = S[h]
            sin_ref[h] = s_in
            o, s2 = _dn_chunk(q_ref[:, sl], k_ref[:, sl], v_ref[:, sl], g_ref[:, h:h + 1], b_ref[:, h:h + 1], s_in)
            o_ref[:, sl] = o
            S[h] = s2

    blk = lambda j: pl.BlockSpec((C, DN_W), lambda n, j=j: (n, j))
    gblk = lambda j: pl.BlockSpec((C, 128), lambda n, j=j: (n, j))
    return pl.pallas_call(
        body, grid=(nc,),
        in_specs=[blk(0), blk(1), blk(2), gblk(0), gblk(1)],
        out_specs=[pl.BlockSpec((C, DN_W), lambda n: (n, 0)),
                   pl.BlockSpec((DN_HEADS, None, 128, 128), lambda n: (0, n, 0, 0))],
        out_shape=[jax.ShapeDtypeStruct((L, DN_W), f32), jax.ShapeDtypeStruct((DN_HEADS, nc, 128, 128), f32)],
        scratch_shapes=[pltpu.VMEM((DN_HEADS, 128, 128), f32)],
        compiler_params=_cp("arbitrary"), name="dn_chunks_fwd",
    )(qkvn, qkvn, qkvn, gb, gb)


def dn_chunks_bwd(qkvn, gb, s_in, do):
    L = qkvn.shape[0]
    C = DN_CHUNK
    nc = L // C

    def body(q_ref, k_ref, v_ref, g_ref, b_ref, sin_ref, do_ref, dq_ref, dk_ref, dv_ref, dg_ref, db_ref, dS):
        n = pl.program_id(0)

        @pl.when(n == 0)
        def _():
            dS[...] = jnp.zeros_like(dS)

        lane = lax.broadcasted_iota(jnp.int32, (C, 128), 1)
        dg_all = jnp.zeros((C, 128), f32)
        db_all = jnp.zeros((C, 128), f32)
        for h in range(DN_HEADS):
            sl = slice(h * 128, (h + 1) * 128)
            args = (q_ref[:, sl], k_ref[:, sl], v_ref[:, sl], g_ref[:, h:h + 1], b_ref[:, h:h + 1], sin_ref[h])
            _, vjp = jax.vjp(_dn_chunk, *args)
            dq, dk, dv, dg, db, ds = vjp((do_ref[:, sl], dS[h]))
            dq_ref[:, sl] = dq
            dk_ref[:, sl] = dk
            dv_ref[:, sl] = dv
            dg_all = dg_all + jnp.where(lane == h, dg, 0.0)
            db_all = db_all + jnp.where(lane == h, db, 0.0)
            dS[h] = ds
        dg_ref[...] = dg_all
        db_ref[...] = db_all

    rv = lambda n: nc - 1 - n
    blk = lambda j: pl.BlockSpec((C, DN_W), lambda n, j=j: (rv(n), j))
    gblk = lambda j: pl.BlockSpec((C, 128), lambda n, j=j: (rv(n), j))
    oblk = pl.BlockSpec((C, DN_W), lambda n: (rv(n), 0))
    gout = pl.BlockSpec((C, 128), lambda n: (rv(n), 0))
    return pl.pallas_call(
        body, grid=(nc,),
        in_specs=[blk(0), blk(1), blk(2), gblk(0), gblk(1),
                  pl.BlockSpec((DN_HEADS, None, 128, 128), lambda n: (0, rv(n), 0, 0)), oblk],
        out_specs=[oblk] * 3 + [gout] * 2,
        out_shape=[jax.ShapeDtypeStruct((L, DN_W), f32)] * 3 + [jax.ShapeDtypeStruct((L, 128), f32)] * 2,
        scratch_shapes=[pltpu.VMEM((DN_HEADS, 128, 128), f32)],
        compiler_params=_cp("arbitrary"), name="dn_chunks_bwd",
    )(qkvn, qkvn, qkvn, gb, gb, s_in, do)


def _dn_post(o, z, w):
    parts = []
    for h in range(DN_HEADS):
        oh = o[:, h * 128:(h + 1) * 128]
        r = lax.rsqrt(jnp.mean(oh * oh, axis=-1, keepdims=True) + EPS)
        parts.append(oh * r * w)
    return jnp.concatenate(parts, axis=1) * _silu(z)


def dn_post_fwd(o, hin, yc, onorm):
    L = o.shape[0]

    def fn(i, nb, ot, zt, yct, w):
        return jnp.concatenate([yct, _dn_post(ot, zt, w)], axis=1)

    return rowwise(fn, name="dn_post_fwd", L=L, tm=_pick(L, 256, SUBLANE),
                   rows=[(o, 0, DN_W, "cur"), (hin, 2304, DN_W, "cur"), (yc, 0, S5_WIDTH, "cur")],
                   consts=[onorm], outs=[(1024, bf16)])


def dn_post_bwd(o, hin, onorm, dycat):
    L = o.shape[0]

    def fn(i, nb, ot, zt, d0, d1, d2, w):
        dy = jnp.concatenate([d0, d1, d2], axis=1)
        sg = jax.nn.sigmoid(zt)
        sz = zt * sg
        dos, dw = [], jnp.zeros((1, 128), f32)
        nrm = []
        for h in range(DN_HEADS):
            sl = slice(h * 128, (h + 1) * 128)
            oh = ot[:, sl]
            r = lax.rsqrt(jnp.mean(oh * oh, axis=-1, keepdims=True) + EPS)
            ohat = oh * r
            t = dy[:, sl] * sz[:, sl]
            dw = dw + _colsum(t * ohat)
            t = t * w
            dos.append(r * (t - ohat * jnp.mean(t * ohat, axis=-1, keepdims=True)))
            nrm.append(ohat * w)
        dz = dy * jnp.concatenate(nrm, axis=1) * (sg * (1.0 + zt * (1.0 - sg)))
        return jnp.concatenate(dos, axis=1), dz, dw

    rows = [(o, 0, DN_W, "cur"), (hin, 2304, DN_W, "cur")] + [(dycat, 256 * (1 + j), 256, "cur") for j in range(3)]
    return rowwise(fn, name="dn_post_bwd", L=L, tm=_pick(L, 256, SUBLANE), rows=rows,
                   consts=[onorm], outs=[(DN_W, f32), (DN_W, f32)], sums=[(1, 128)])


def conv_bwd_win(xarr, start, C, w, dc, name):
    L = xarr.shape[0]
    W = w.shape[0]

    def fn(i, nb, xt, dct, p8, n8, wv):
        dx = _conv_causal_bwd_x(dct, n8 * (i < nb - 1).astype(f32), wv)
        dw = _conv_causal_bwd_w(dct, xt, p8 * (i > 0).astype(f32), W)
        return dx, dw

    return rowwise(fn, name=name, L=L, tm=_pick(L, 128, SUBLANE),
                   rows=[(xarr, start, C, "cur"), (dc, 0, C, "cur"), (xarr, start, C, "prev"), (dc, 0, C, "next")],
                   consts=[w], outs=[(C, bf16)], sums=[(SUBLANE, C)])


def rec_assemble(dx_qkv, dz, du1, du2, dab):
    L = dz.shape[0]

    def fn(i, nb, a, b, c, d, e):
        return jnp.concatenate([a.astype(f32), b, c + d, e], axis=1)

    return rowwise(fn, name="rec_assemble", L=L, tm=_pick(L, 256, SUBLANE),
                   rows=[(dx_qkv, 0, QKV_W, "cur"), (dz, 0, DN_W, "cur"), (du1, 0, 256, "cur"),
                         (du2, 0, 256, "cur"), (dab, 0, 256, "cur")], outs=[(REC_PAD, bf16)])


def deltanet_fwd(hin, prm, yc):
    qkvn, gb = dn_pre_fwd(hin, prm["conv"], prm["alog"], prm["dtb"])
    o, s_in = dn_chunks_fwd(qkvn, gb)
    ycat = dn_post_fwd(o, hin, yc, prm["onorm"])
    return ycat, (qkvn, gb, o, s_in)


def deltanet_bwd(hin, prm, saved, dycat):
    qkvn, gb, o, s_in = saved
    do, dz, donorm = dn_post_bwd(o, hin, prm["onorm"], dycat)
    dq, dk, dv, dgH, dbH = dn_chunks_bwd(qkvn, gb, s_in, do)
    dc, dab, dalog, ddtb = dn_pre_bwd(hin, prm["conv"], prm["alog"], prm["dtb"], (dq, dk, dv), dgH, dbH)
    dx_qkv, dconv = conv_bwd_win(hin, 0, QKV_W, prm["conv"], dc, "dn_conv_bwd")
    return dx_qkv, dz, dab, dict(conv=dconv[:DN_CONV], alog=dalog, dtb=ddtb, onorm=donorm)


AXES = ("x", "y", "c")
D2D_CHUNKS = 16


def _collective(x, axes, mode, name, nchunk=1):
    k = len(axes)
    P = 2 ** k
    shape = x.shape if mode == "gather" else x.shape[1:]
    rows = shape[0] // nchunk
    assert rows * nchunk == shape[0]

    def body(x_ref, out_ref, send_sems, recv_sems, local_sems):
        co = {a: lax.axis_index(a) for a in AXES}
        me = 0
        for a in axes:
            me = me * 2 + co[a]

        def src(j, q):
            s = x_ref if mode == "gather" else x_ref.at[j]
            return s.at[pl.ds(q * rows, rows)]

        def dst(j, q):
            return out_ref.at[j].at[pl.ds(q * rows, rows)]

        locals_ = [pltpu.make_async_copy(src(me, q), dst(me, q), local_sems.at[q]) for q in range(nchunk)]
        for cp in locals_:
            cp.start()
        sends = []
        for m in range(1, P):
            tco = dict(co)
            t = 0
            for i, a in enumerate(axes):
                if (m >> (k - 1 - i)) & 1:
                    tco[a] = 1 - co[a]
                t = t * 2 + tco[a]
            dev = tuple(tco[a] for a in AXES)
            for q in range(nchunk):
                s = (m - 1) * nchunk + q
                cp = pltpu.make_async_remote_copy(src_ref=src(t, q), dst_ref=dst(me, q), send_sem=send_sems.at[s],
                                                  recv_sem=recv_sems.at[s], device_id=dev, device_id_type=MESH)
                cp.start()
                sends.append((cp, t, q, s, dev))
        for cp, t, q, s, dev in sends:
            pltpu.make_async_remote_copy(src_ref=src(t, q), dst_ref=dst(t, q), send_sem=send_sems.at[s],
                                         recv_sem=recv_sems.at[s], device_id=dev, device_id_type=MESH).wait_recv()
        for cp, *_ in sends:
            cp.wait_send()
        for cp in locals_:
            cp.wait()

    ns = (P - 1) * nchunk
    return pl.pallas_call(
        body,
        in_specs=[pl.BlockSpec(memory_space=pl.ANY)],
        out_specs=pl.BlockSpec(memory_space=pl.ANY),
        out_shape=jax.ShapeDtypeStruct((P,) + tuple(shape), x.dtype),
        scratch_shapes=[pltpu.SemaphoreType.DMA((ns,)), pltpu.SemaphoreType.DMA((ns,)),
                        pltpu.SemaphoreType.DMA((nchunk,))],
        name=name,
    )(x)


def all_gather(x, axes, name, nchunk=1):
    return _collective(x, axes, "gather", name, nchunk)


def exchange(x, axes, name, nchunk=1):
    return _collective(x, axes, "exchange", name, nchunk)


def sum_slots(x, name, out_dtype=f32):
    P, R, C = x.shape
    tr = _pick(R, 256, 2 * SUBLANE)

    def body(x_ref, o_ref):
        acc = x_ref[0].astype(f32)
        for j in range(1, P):
            acc = acc + x_ref[j].astype(f32)
        o_ref[...] = acc.astype(o_ref.dtype)

    return pl.pallas_call(
        body, grid=(R // tr,), in_specs=[pl.BlockSpec((P, tr, C), lambda i: (0, i, 0))],
        out_specs=pl.BlockSpec((tr, C), lambda i: (i, 0)), out_shape=jax.ShapeDtypeStruct((R, C), out_dtype),
        compiler_params=_cp("parallel"), name=name,
    )(x)


def _pack(arrs, width, row_mult, dtype):
    flat = jnp.concatenate([a.astype(dtype).reshape(-1) for a in arrs])
    unit = width * row_mult
    n = -(-flat.shape[0] // unit) * unit
    return jnp.pad(flat, (0, n - flat.shape[0])).reshape(n // width, width)


def _unpack(flat, shapes):
    flat = flat.reshape(-1)
    out, off = [], 0
    for s in shapes:
        n = int(np.prod(s))
        out.append(flat[off:off + n].reshape(s))
        off += n
    return out


def ada_fwd(c_all, ada_w):
    def body(c_ref, w_ref, o_ref):
        cond = _silu(c_ref[...])
        for l in range(ada_w.shape[0]):
            o_ref[l] = _dot(cond, w_ref[l], precision=HI)

    return pl.pallas_call(body, out_shape=jax.ShapeDtypeStruct((ada_w.shape[0], c_all.shape[0], ada_w.shape[2]), f32),
                          compiler_params=pltpu.CompilerParams(vmem_limit_bytes=VMEM_LIMIT), name="ada_fwd")(c_all, ada_w)


def ada_bwd(c_all, dmod):
    def body(c_ref, d_ref, o_ref):
        cond = _silu(c_ref[...])
        for l in range(dmod.shape[0]):
            o_ref[l] = _dot(cond, d_ref[l], TN, precision=HI)

    return pl.pallas_call(body, out_shape=jax.ShapeDtypeStruct((dmod.shape[0], c_all.shape[1], dmod.shape[2]), f32),
                          compiler_params=pltpu.CompilerParams(vmem_limit_bytes=VMEM_LIMIT), name="ada_bwd")(c_all, dmod)


def loss_fwd_bwd(y, target):
    L, D = y.shape

    def fn(i, nb, yt, tt):
        e = yt - tt
        return e * (1.0 / D), jnp.sum(jnp.sum(e * e, axis=1, keepdims=True), axis=0, keepdims=True)

    return rowwise(fn, name="loss", L=L, tm=_pick(L, 512, SUBLANE), rows=[(y, 0, D, "cur"), (target, 0, D, "cur")],
                   outs=[(D, f32)], sums=[(1, 1)])


def adamw(w, g, m, v, name):
    R, C = w.shape

    def fn(i, nb, wt, gt, mt, vt):
        m2 = ADAM_B1 * mt + (1.0 - ADAM_B1) * gt
        v2 = ADAM_B2 * vt + (1.0 - ADAM_B2) * (gt * gt)
        m_hat = m2 / (1.0 - ADAM_B1 ** ADAM_STEP)
        v_hat = v2 / (1.0 - ADAM_B2 ** ADAM_STEP)
        delta = -ADAM_LR * (m_hat / (jnp.sqrt(v_hat) + ADAM_EPS) + ADAM_WD * wt)
        return delta, m2, v2

    return rowwise(fn, name=name, L=R, tm=_pick(R, 256, SUBLANE), rows=[(a, 0, C, "cur") for a in (w, g, m, v)],
                   outs=[(C, f32)] * 3)


W_NAMES = ["ada_w", "ada_b", "norm_mix", "norm_ffn", "attn_w_in", "attn_q_norm_a", "attn_k_norm_a", "attn_q_norm_b",
           "attn_k_norm_b", "attn_sinks", "attn_w_out", "rec_w_in", "s5_lambda_re", "s5_lambda_im", "s5_log_dt",
           "s5_b_re", "s5_b_im", "s5_c_re", "s5_c_im", "s5_d", "s5_glu_w", "s5_glu_b", "dn_conv", "dn_a_log",
           "dn_dt_bias", "dn_out_norm", "rec_w_out", "ffn_w_up", "ffn_conv", "ffn_w_down"]
BIG = ["attn_w_in", "attn_w_out", "rec_w_in", "rec_w_out", "ffn_w_up", "ffn_w_down"]
SMALL_SHARDED = ["s5_d", "s5_glu_w", "s5_glu_b", "dn_conv", "ffn_conv"]
SMALL_REPL = [n for n in W_NAMES if n not in BIG and n not in SMALL_SHARDED and n != "ada_w"]
NSH = 4


def _unshard(g, name):
    ax = {"attn_w_in": 2, "attn_w_out": 1, "rec_w_in": 2, "rec_w_out": 1, "ffn_w_up": 2, "ffn_w_down": 1,
          "s5_d": 1, "s5_glu_w": 1, "s5_glu_b": 1, "dn_conv": 2, "ffn_conv": 2}[name]
    g = jnp.moveaxis(g, 0, ax)
    s = g.shape
    return g.reshape(s[:ax] + (s[ax] * s[ax + 1],) + s[ax + 2:])


def _to_shards(full, name):
    ax = {"attn_w_in": 2, "attn_w_out": 1, "rec_w_in": 2, "rec_w_out": 1, "ffn_w_up": 2, "ffn_w_down": 1,
          "s5_d": 1, "s5_glu_w": 1, "s5_glu_b": 1, "dn_conv": 2, "ffn_conv": 2}[name]
    s = full.shape
    g = full.reshape(s[:ax] + (NSH, s[ax] // NSH) + s[ax + 1:])
    return jnp.moveaxis(g, ax, 0)


def _rec_pad_cols(w):
    z6 = jnp.zeros(w.shape[:-1] + (122,), w.dtype)
    return jnp.concatenate([w[..., 256:3328], w[..., 0:256], w[..., 3328:3334], z6, w[..., 3334:3340], z6], axis=-1)


def _rec_unpad_cols(g):
    return jnp.concatenate([g[..., 3072:3328], g[..., 0:3072], g[..., 3328:3334], g[..., 3456:3462]], axis=-1)


def _ffn_fwd(x1, nf, sc, sh, gate, w_up, conv, w_dn, tag):
    h2 = modulate_fwd(x1, nf, sc, sh, f"{tag}_mod2_fwd")
    up = mm(h2, w_up, name=f"{tag}_ffn_up")
    act = ffn_act_fwd(up, conv, f"{tag}_ffn_act_fwd")
    f, x2 = mm(act, w_dn, name=f"{tag}_ffn_down", out_dtypes=(f32, f32), epi=_resid_epi, epi_mn=[x1], epi_n=[gate])
    return x2, (h2, up, act, f)


def _ffn_bwd(dx, x1, nf, sc, sh, gate, w_up, conv, w_dn, saved, tag):
    h2, up, act, f = saved
    df, dgate = resid_bwd(dx, f, gate, f"{tag}_res2_bwd")
    dact = mm(df, w_dn, tb=True, name=f"{tag}_ffn_dact")
    dw_dn = mm(act, df, ta=True, name=f"{tag}_ffn_dwdown")
    dc = ffn_act_bwd1(up, conv, dact, f"{tag}_ffn_act_bwd")
    dup, dconv = conv_bwd(up, conv, dc, f"{tag}_ffn_conv_bwd")
    dw_up = mm(h2, dup, ta=True, name=f"{tag}_ffn_dwup")
    dh2 = mm(dup, w_up, tb=True, name=f"{tag}_ffn_dh")
    dx, dnf, dsc, dsh = modulate_bwd(x1, nf, sc, sh, dh2, dx, f"{tag}_mod2_bwd")
    return dx, dict(nf=dnf, sc=dsc, sh=dsh, gate=dgate, w_up=dw_up, conv=dconv[:FFN_CONV], w_dn=dw_dn)


def kernel(x, c, ada_w, ada_b, norm_mix, norm_ffn, attn_w_in, attn_q_norm_a, attn_k_norm_a, attn_q_norm_b, attn_k_norm_b, attn_sinks, attn_w_out, rec_w_in, s5_lambda_re, s5_lambda_im, s5_log_dt, s5_b_re, s5_b_im, s5_c_re, s5_c_im, s5_d, s5_glu_w, s5_glu_b, dn_conv, dn_a_log, dn_dt_bias, dn_out_norm, rec_w_out, ffn_w_up, ffn_conv, ffn_w_down, loss_target, m_ada_w, m_ada_b, m_norm_mix, m_norm_ffn, m_attn_w_in, m_attn_q_norm_a, m_attn_k_norm_a, m_attn_q_norm_b, m_attn_k_norm_b, m_attn_sinks, m_attn_w_out, m_rec_w_in, m_s5_lambda_re, m_s5_lambda_im, m_s5_log_dt, m_s5_b_re, m_s5_b_im, m_s5_c_re, m_s5_c_im, m_s5_d, m_s5_glu_w, m_s5_glu_b, m_dn_conv, m_dn_a_log, m_dn_dt_bias, m_dn_out_norm, m_rec_w_out, m_ffn_w_up, m_ffn_conv, m_ffn_w_down, v_ada_w, v_ada_b, v_norm_mix, v_norm_ffn, v_attn_w_in, v_attn_q_norm_a, v_attn_k_norm_a, v_attn_q_norm_b, v_attn_k_norm_b, v_attn_sinks, v_attn_w_out, v_rec_w_in, v_s5_lambda_re, v_s5_lambda_im, v_s5_log_dt, v_s5_b_re, v_s5_b_im, v_s5_c_re, v_s5_c_im, v_s5_d, v_s5_glu_w, v_s5_glu_b, v_dn_conv, v_dn_a_log, v_dn_dt_bias, v_dn_out_norm, v_rec_w_out, v_ffn_w_up, v_ffn_conv, v_ffn_w_down):
    args = (ada_w, ada_b, norm_mix, norm_ffn, attn_w_in, attn_q_norm_a, attn_k_norm_a, attn_q_norm_b, attn_k_norm_b, attn_sinks, attn_w_out, rec_w_in, s5_lambda_re, s5_lambda_im, s5_log_dt, s5_b_re, s5_b_im, s5_c_re, s5_c_im, s5_d, s5_glu_w, s5_glu_b, dn_conv, dn_a_log, dn_dt_bias, dn_out_norm, rec_w_out, ffn_w_up, ffn_conv, ffn_w_down)
    ms = (m_ada_w, m_ada_b, m_norm_mix, m_norm_ffn, m_attn_w_in, m_attn_q_norm_a, m_attn_k_norm_a, m_attn_q_norm_b, m_attn_k_norm_b, m_attn_sinks, m_attn_w_out, m_rec_w_in, m_s5_lambda_re, m_s5_lambda_im, m_s5_log_dt, m_s5_b_re, m_s5_b_im, m_s5_c_re, m_s5_c_im, m_s5_d, m_s5_glu_w, m_s5_glu_b, m_dn_conv, m_dn_a_log, m_dn_dt_bias, m_dn_out_norm, m_rec_w_out, m_ffn_w_up, m_ffn_conv, m_ffn_w_down)
    vs = (v_ada_w, v_ada_b, v_norm_mix, v_norm_ffn, v_attn_w_in, v_attn_q_norm_a, v_attn_k_norm_a, v_attn_q_norm_b, v_attn_k_norm_b, v_attn_sinks, v_attn_w_out, v_rec_w_in, v_s5_lambda_re, v_s5_lambda_im, v_s5_log_dt, v_s5_b_re, v_s5_b_im, v_s5_c_re, v_s5_c_im, v_s5_d, v_s5_glu_w, v_s5_glu_b, v_dn_conv, v_dn_a_log, v_dn_dt_bias, v_dn_out_norm, v_rec_w_out, v_ffn_w_up, v_ffn_conv, v_ffn_w_down)
    W = dict(zip(W_NAMES, args))
    Mo = dict(zip(W_NAMES, ms))
    Vo = dict(zip(W_NAMES, vs))
    xi, yi, ci = lax.axis_index("x"), lax.axis_index("y"), lax.axis_index("c")
    shard = 2 * xi + yi
    me8 = 4 * xi + 2 * yi + ci
    xs = x[0]
    target = loss_target[0]
    L, D = xs.shape

    wflat = _pack([W[n] for n in BIG], 1024, 2 * D2D_CHUNKS * 16, bf16)
    R = wflat.shape[0]
    half = lax.dynamic_slice(wflat, (ci * (R // 2), 0), (R // 2, 1024))
    g4 = all_gather(half, ("x", "y"), "gather_w_xy")
    g2 = all_gather(g4.reshape(NSH * (R // 2), 1024), ("c",), "gather_w_c", D2D_CHUNKS).reshape(2, NSH, R // 2, 1024)
    wfull = jnp.concatenate([g2[0], g2[1]], axis=1).reshape(NSH, -1)
    Wf = {}
    off = 0
    for n in BIG:
        sz = int(np.prod(W[n].shape))
        Wf[n] = _unshard(wfull[:, off:off + sz].reshape((NSH,) + W[n].shape), n)
        off += sz
    rec_w_in_p = _rec_pad_cols(Wf["rec_w_in"][0])

    sflat = _pack([c] + [W[n] for n in SMALL_SHARDED], 1024, 8, f32)
    s8 = all_gather(sflat, AXES, "gather_small")
    s8f = s8.reshape(8, -1)
    c_all = s8f[:, :D]
    Ws = {}
    off = D
    for n in SMALL_SHARDED:
        sz = int(np.prod(W[n].shape))
        Ws[n] = _unshard(s8f[0::2, off:off + sz].reshape((NSH,) + W[n].shape), n)
        off += sz

    modp = ada_fwd(c_all, ada_w)
    modg = all_gather(modp, ("x", "y"), "gather_mod")
    mod_all = jnp.moveaxis(modg, 0, 2).reshape(2, 8, -1) + ada_b[:, None, :]
    mod = lax.dynamic_slice(mod_all, (0, me8, 0), (2, 1, mod_all.shape[2]))[:, 0, :]
    mods = [[mod[l:l + 1, j * D:(j + 1) * D] for j in range(6)] for l in range(2)]

    sh1, sc1, g1, sh2, sc2, g2_ = mods[0]
    nm0, nf0 = norm_mix[0:1], norm_ffn[0:1]
    sinkb = jnp.repeat(attn_sinks[0], HEAD_DIM)[None]
    h0 = modulate_fwd(xs, nm0, sc1, sh1, "l0_mod1_fwd")
    hin0 = mm(h0, Wf["attn_w_in"][0], name="l0_in_proj")
    ocat, att_saved = attention_fwd(hin0, attn_q_norm_a, attn_k_norm_a, attn_q_norm_b, attn_k_norm_b, sinkb)
    y0, x1 = mm(ocat, Wf["attn_w_out"][0], name="l0_out_proj", out_dtypes=(f32, f32), epi=_resid_epi,
                epi_mn=[xs], epi_n=[g1])
    x2, ffn0_saved = _ffn_fwd(x1, nf0, sc2, sh2, g2_, Wf["ffn_w_up"][0], Ws["ffn_conv"][0], Wf["ffn_w_down"][0], "l0")

    th1, tc1, t1, th2, tc2, t2 = mods[1]
    nm1, nf1 = norm_mix[1:2], norm_ffn[1:2]
    pad128 = lambda a: jnp.pad(a, ((0, 0), (0, 128 - a.shape[1])))
    s5p = dict(lr=s5_lambda_re[0], li=s5_lambda_im[0], ldt=s5_log_dt[0][:, None], b_re=s5_b_re[0], b_im=s5_b_im[0],
               c_re=s5_c_re[0], c_im=s5_c_im[0], d=Ws["s5_d"], gw=Ws["s5_glu_w"][0], gb=Ws["s5_glu_b"])
    dnp = dict(conv=Ws["dn_conv"][0], alog=pad128(dn_a_log), dtb=pad128(dn_dt_bias), onorm=dn_out_norm)
    h1 = modulate_fwd(x2, nm1, tc1, th1, "l1_mod1_fwd")
    hin1 = mm(h1, rec_w_in_p, name="l1_in_proj")
    yc, s5_saved = s5_fwd(hin1, s5p)
    ycat, dn_saved = deltanet_fwd(hin1, dnp, yc)
    y1, x3 = mm(ycat, Wf["rec_w_out"][0], name="l1_out_proj", out_dtypes=(f32, f32), epi=_resid_epi,
                epi_mn=[x2], epi_n=[t1])
    x4, ffn1_saved = _ffn_fwd(x3, nf1, tc2, th2, t2, Wf["ffn_w_up"][1], Ws["ffn_conv"][1], Wf["ffn_w_down"][1], "l1")

    dx, sse = loss_fwd_bwd(x4, target)
    loss = lax.psum(0.5 * sse[0, 0] / D, AXES)

    dx, gf1 = _ffn_bwd(dx, x3, nf1, tc2, th2, t2, Wf["ffn_w_up"][1], Ws["ffn_conv"][1], Wf["ffn_w_down"][1], ffn1_saved, "l1")
    dy1, dt1 = resid_bwd(dx, y1, t1, "l1_res1_bwd")
    dycat = mm(dy1, Wf["rec_w_out"][0], tb=True, name="l1_dycat")
    dw_rec_out = mm(ycat, dy1, ta=True, name="l1_dwout")
    du_skip, du_b, s5g = s5_bwd(hin1, s5p, s5_saved, dycat)
    dx_qkv, dz, dab, dng = deltanet_bwd(hin1, dnp, dn_saved, dycat)
    dhin1 = rec_assemble(dx_qkv, dz, du_skip, du_b, dab)
    dw_rec_in = _rec_unpad_cols(mm(h1, dhin1, ta=True, name="l1_dwin"))
    dh1 = mm(dhin1, rec_w_in_p, tb=True, name="l1_dh")
    dx, dnm1, dtc1, dth1 = modulate_bwd(x2, nm1, tc1, th1, dh1, dx, "l1_mod1_bwd")

    dx, gf0 = _ffn_bwd(dx, x1, nf0, sc2, sh2, g2_, Wf["ffn_w_up"][0], Ws["ffn_conv"][0], Wf["ffn_w_down"][0], ffn0_saved, "l0")
    dy0, dg1 = resid_bwd(dx, y0, g1, "l0_res1_bwd")
    dcat = mm(dy0, Wf["attn_w_out"][0], tb=True, name="l0_dcat")
    dw_attn_out = mm(ocat, dy0, ta=True, name="l0_dwout")
    dhin0, dwqa, dwka, dwqb, dwkb, dsinkb = attention_bwd(hin0, attn_q_norm_a, attn_k_norm_a, attn_q_norm_b,
                                                          attn_k_norm_b, sinkb, att_saved, dcat)
    dw_attn_in = mm(h0, dhin0, ta=True, name="l0_dwin")
    dh0 = mm(dhin0, Wf["attn_w_in"][0], tb=True, name="l0_dh")
    grad_x, dnm0, dsc1, dsh1 = modulate_bwd(xs, nm0, sc1, sh1, dh0, dx, "l0_mod1_bwd")

    dmod = jnp.concatenate([
        jnp.concatenate([dsh1, dsc1, dg1, gf0["sh"], gf0["sc"], gf0["gate"]], axis=1),
        jnp.concatenate([dth1, dtc1, dt1, gf1["sh"], gf1["sc"], gf1["gate"]], axis=1)], axis=0)
    gl = {
        "ada_b": dmod,
        "norm_mix": jnp.concatenate([dnm0, dnm1], axis=0),
        "norm_ffn": jnp.concatenate([gf0["nf"], gf1["nf"]], axis=0),
        "attn_q_norm_a": dwqa, "attn_k_norm_a": dwka, "attn_q_norm_b": dwqb, "attn_k_norm_b": dwkb,
        "attn_sinks": dsinkb[:, ::HEAD_DIM],
        "s5_lambda_re": s5g["lr"][None], "s5_lambda_im": s5g["li"][None], "s5_log_dt": s5g["ldt"][:, 0][None],
        "s5_b_re": s5g["b_re"][None], "s5_b_im": s5g["b_im"][None], "s5_c_re": s5g["c_re"][None],
        "s5_c_im": s5g["c_im"][None],
        "dn_a_log": dng["alog"][:, :DN_HEADS], "dn_dt_bias": dng["dtb"][:, :DN_HEADS], "dn_out_norm": dng["onorm"],
        "s5_d": s5g["d"], "s5_glu_w": s5g["gw"][None], "s5_glu_b": s5g["gb"], "dn_conv": dng["conv"][None],
        "ffn_conv": jnp.stack([gf0["conv"], gf1["conv"]]),
        "attn_w_in": dw_attn_in[None], "attn_w_out": dw_attn_out[None], "rec_w_in": dw_rec_in[None],
        "rec_w_out": dw_rec_out[None], "ffn_w_up": jnp.stack([gf0["w_up"], gf1["w_up"]]),
        "ffn_w_down": jnp.stack([gf0["w_dn"], gf1["w_dn"]]),
    }

    small_names = SMALL_REPL + SMALL_SHARDED
    gs = _pack([gl[n] for n in small_names], 128, 8, f32)
    gs8 = all_gather(gs, AXES, "gather_small_grads")
    gsum = sum_slots(gs8, "sum_small_grads")
    full_shapes = [gl[n].shape for n in small_names]
    gfull = dict(zip(small_names, _unpack(gsum, full_shapes)))
    dmod_all = gs8.reshape(8, -1)[:, :2 * 6 * D].reshape(8, 2, 6 * D)
    ncol = ada_w.shape[2]
    dmod_sh = jnp.moveaxis(lax.dynamic_slice(dmod_all, (0, 0, shard * ncol), (8, 2, ncol)), 0, 1)
    grads = {"ada_w": ada_bwd(c_all, dmod_sh)}
    for n in SMALL_REPL:
        grads[n] = gfull[n]
    for n in SMALL_SHARDED:
        sh_all = _to_shards(gfull[n], n)
        grads[n] = lax.dynamic_slice(sh_all, (shard,) + (0,) * (sh_all.ndim - 1), (1,) + sh_all.shape[1:])[0]

    gflat = jnp.concatenate([_to_shards(gl[n], n).reshape(NSH, -1) for n in BIG], axis=1)
    nel = gflat.shape[1]
    unit = 2 * D2D_CHUNKS * 8 * 1024
    npad = -(-nel // unit) * unit
    gflat = jnp.pad(gflat, ((0, 0), (0, npad - nel))).reshape(NSH, 2, npad // 2048, 1024)
    gx = jnp.moveaxis(gflat, 1, 0).reshape(2, NSH * (npad // 2048), 1024)
    gc = exchange(gx, ("c",), "reduce_c", D2D_CHUNKS)
    gpair = sum_slots(gc, "sum_pair", bf16)
    gq = exchange(gpair.reshape(NSH, npad // 2048, 1024), ("x", "y"), "reduce_xy")
    ghalf = sum_slots(gq, "sum_chips")
    gsh = all_gather(ghalf, ("c",), "gather_grad_c", D2D_CHUNKS).reshape(-1)
    off = 0
    for n in BIG:
        sz = int(np.prod(W[n].shape))
        grads[n] = gsh[off:off + sz].reshape(W[n].shape)
        off += sz

    delta, new_m, new_v = {}, {}, {}

    def as2d(a):
        return a.reshape(-1, a.shape[-1])

    for n in ["ada_w"] + BIG:
        d_, m_, v_ = adamw(as2d(W[n]), as2d(grads[n]), as2d(Mo[n]), as2d(Vo[n]), f"adamw_{n}")
        delta[n], new_m[n], new_v[n] = d_.reshape(W[n].shape), m_.reshape(W[n].shape), v_.reshape(W[n].shape)
    pk = lambda dd: _pack([dd[n] for n in small_names], 128, 8, f32)
    d_, m_, v_ = adamw(pk(W), pk(grads), pk(Mo), pk(Vo), "adamw_small")
    shp = [W[n].shape for n in small_names]
    for dst, src in ((delta, d_), (new_m, m_), (new_v, v_)):
        dst.update(zip(small_names, _unpack(src, shp)))

    return (loss, grad_x[None], *[grads[n] for n in W_NAMES], *[delta[n] for n in W_NAMES],
            *[new_m[n] for n in W_NAMES], *[new_v[n] for n in W_NAMES])
```

```python
import functools
import math

import numpy as np
import jax
import jax.numpy as jnp
from jax import lax
from jax.experimental import pallas as pl
from jax.experimental.pallas import tpu as pltpu

f32 = jnp.float32
bf16 = jnp.bfloat16
HI = lax.Precision.HIGHEST
MESH = pl.DeviceIdType.MESH

HEAD_DIM = 64
BLOCK = 128
A_Q_HEADS = 8
A_KV_HEADS = 2
A_WINDOW = 128
B_HEADS = 8
B_BRANCHES = ((128, 1), (512, 4), (2048, 16))
N_ATTN_HEADS = 16
ATTN_IN = 2304
S5_GROUP = 16
S5_GROUPS = 16
S5_WIDTH = 256
S5_STATE = 64
DN_HEADS = 6
DN_DK = 128
DN_CONV = 4
DN_CHUNK = 64
REC_IN = 3340
REC_PAD = 3584
FFN_CONV = 3
EPS = 1e-6
ADAM_LR = 0.001
ADAM_B1 = 0.9
ADAM_B2 = 0.999
ADAM_EPS = 1e-08
ADAM_WD = 0.01
ADAM_STEP = 10

LANE = 128
SUBLANE = 8
VMEM_LIMIT = 52 * 1024 * 1024
MM_FULL_K = 5632
MM_VMEM_BUDGET = 40 * 1024 * 1024


def _cp(*sem):
    return pltpu.CompilerParams(dimension_semantics=sem, vmem_limit_bytes=VMEM_LIMIT)


def _pick(dim, cap, unit=LANE):
    for t in (2048, 1024, 768, 512, 384, 256, 128, 64, 32, 16, 8):
        if t <= cap and t % unit == 0 and dim % t == 0:
            return t
    return dim


def _dot(a, b, dims=(((1,), (0,)), ((), ())), precision=None):
    return lax.dot_general(a, b, dims, precision=precision, preferred_element_type=f32)


NN = (((1,), (0,)), ((), ()))
NT = (((1,), (1,)), ((), ()))
TN = (((0,), (0,)), ((), ()))


def mm(a, b, *, name, ta=False, tb=False, a_win=None, b_win=None, out_dtypes=(f32,),
       epi=None, epi_mn=(), epi_n=(), tm_cap=1024, tn_cap=8192, tk_cap=None):
    a0, a1 = a.shape
    b0, b1 = b.shape
    aw = a_win or (0, a1)
    bw = b_win or (0, b1)
    if ta:
        K, M = a0, aw[1]
    else:
        M, K = a0, aw[1]
    if tb:
        N, K2 = b0, bw[1]
    else:
        K2, N = b0, bw[1]
    assert K == K2, (a.shape, b.shape, ta, tb, a_win, b_win)
    if tk_cap is None:
        tk_cap = K if K <= MM_FULL_K else 2048
    tk = _pick(K, tk_cap, SUBLANE if (ta and not tb) else LANE)
    nk = K // tk
    sa, sb = a.dtype.itemsize, b.dtype.itemsize
    so = sum(jnp.dtype(d).itemsize for d in out_dtypes)
    n_mn, n_n, n_out = len(epi_mn), len(epi_n), len(out_dtypes)

    def vmem(tm_, tn_):
        return 2 * (tm_ * tk * sa + tk * tn_ * sb + tm_ * tn_ * (so + 4 * n_mn)) + 2 * tm_ * tn_ * 4

    best = None
    for tm_ in (t for t in (1024, 512, 256, 128) if M % t == 0 and (not ta or aw[0] % t == 0)):
        for tn_ in (t for t in (N, N // 2, 1024, 768, 512, 384, 256, 128)
                    if t % LANE == 0 and N % t == 0 and (tb or bw[0] % t == 0)):
            if tm_ <= tm_cap and tn_ <= max(tn_cap, 0) and vmem(tm_, tn_) <= MM_VMEM_BUDGET:
                if best is None or (tm_ * tn_, tn_) > (best[0] * best[1], best[1]):
                    best = (tm_, tn_)
    assert best is not None, (name, M, N, K)
    tm, tn = best
    b_outer = tk * tn * sb > tm * tk * sa

    def ix(f):
        if b_outer:
            return lambda j, i, k: f(i, j, k)
        return f

    if ta:
        mo = aw[0] // tm
        a_spec = pl.BlockSpec((tk, tm), ix(lambda i, j, k: (k, i + mo)))
    else:
        assert aw[0] % tk == 0
        ko = aw[0] // tk
        a_spec = pl.BlockSpec((tm, tk), ix(lambda i, j, k: (i, k + ko)))
    if tb:
        assert bw[0] % tk == 0
        kob = bw[0] // tk
        b_spec = pl.BlockSpec((tn, tk), ix(lambda i, j, k: (j, k + kob)))
    else:
        no = bw[0] // tn
        b_spec = pl.BlockSpec((tk, tn), ix(lambda i, j, k: (k, j + no)))
    dims = (((0 if ta else 1,), (1 if tb else 0,)), ((), ()))

    def body(a_ref, b_ref, *rest):
        mn_refs = rest[:n_mn]
        n_refs = rest[n_mn:n_mn + n_n]
        out_refs = rest[n_mn + n_n:n_mn + n_n + n_out]

        def finish(r):
            if epi is None:
                outs = (r,)
            else:
                outs = epi(r, *[m[...] for m in mn_refs], *[v[...] for v in n_refs])
            for o_ref, o in zip(out_refs, outs):
                o_ref[...] = o.astype(o_ref.dtype)

        part = _dot(a_ref[...].astype(bf16), b_ref[...].astype(bf16), dims)
        if nk == 1:
            finish(part)
            return
        acc = rest[-1]
        k = pl.program_id(2)

        @pl.when(k == 0)
        def _():
            acc[...] = part

        @pl.when(k > 0)
        def _():
            acc[...] += part

        @pl.when(k == nk - 1)
        def _():
            finish(acc[...])

    mn_spec = pl.BlockSpec((tm, tn), ix(lambda i, j, k: (i, j)))
    n_spec = pl.BlockSpec((1, tn), ix(lambda i, j, k: (0, j)))
    gi, gj = M // tm, N // tn
    outs = pl.pallas_call(
        body,
        grid=(gj, gi, nk) if b_outer else (gi, gj, nk),
        in_specs=[a_spec, b_spec] + [mn_spec] * n_mn + [n_spec] * n_n,
        out_specs=[mn_spec] * n_out,
        out_shape=[jax.ShapeDtypeStruct((M, N), d) for d in out_dtypes],
        scratch_shapes=[pltpu.VMEM((tm, tn), f32)] if nk > 1 else [],
        compiler_params=_cp("parallel", "parallel", "arbitrary"),
        name=name,
    )(a, b, *epi_mn, *epi_n)
    return outs[0] if n_out == 1 else tuple(outs)


def rowwise(fn, *, name, L, tm, rows=(), consts=(), outs=(), sums=()):
    nb = L // tm
    hb = tm // SUBLANE
    in_specs = []
    arrs = []
    for arr, start, width, kind in rows:
        assert start % width == 0, (name, start, width)
        co = start // width
        if kind == "cur":
            in_specs.append(pl.BlockSpec((tm, width), lambda i, co=co: (i, co)))
        elif kind == "prev":
            in_specs.append(pl.BlockSpec((SUBLANE, width), lambda i, co=co: (jnp.maximum(i * hb - 1, 0), co)))
        else:
            last = L // SUBLANE - 1
            in_specs.append(pl.BlockSpec((SUBLANE, width), lambda i, co=co, last=last: (jnp.minimum((i + 1) * hb, last), co)))
        arrs.append(arr)
    for cst in consts:
        assert cst.ndim == 2
        in_specs.append(pl.BlockSpec(cst.shape, lambda i: (0, 0)))
        arrs.append(cst)
    n_rows, n_c, n_o, n_s = len(rows), len(consts), len(outs), len(sums)
    out_specs = [pl.BlockSpec((tm, w), lambda i: (i, 0)) for w, _ in outs]
    out_specs += [pl.BlockSpec(s, lambda i: (0, 0)) for s in sums]
    out_shape = [jax.ShapeDtypeStruct((L, w), d) for w, d in outs]
    out_shape += [jax.ShapeDtypeStruct(s, f32) for s in sums]

    def body(*refs):
        i = pl.program_id(0)
        vals = [r[...] for r in refs[:n_rows + n_c]]
        res = fn(i, nb, *vals)
        if not isinstance(res, (tuple, list)):
            res = (res,)
        o_refs = refs[n_rows + n_c:n_rows + n_c + n_o]
        s_refs = refs[n_rows + n_c + n_o:]
        for o_ref, o in zip(o_refs, res[:n_o]):
            o_ref[...] = o.astype(o_ref.dtype)
        if n_s:
            @pl.when(i == 0)
            def _():
                for s_ref in s_refs:
                    s_ref[...] = jnp.zeros_like(s_ref)

            for s_ref, s in zip(s_refs, res[n_o:]):
                s_ref[...] += s

    res = pl.pallas_call(
        body,
        grid=(nb,),
        in_specs=in_specs,
        out_specs=out_specs,
        out_shape=out_shape,
        compiler_params=_cp("arbitrary" if n_s else "parallel"),
        name=name,
    )(*arrs)
    return res[0] if len(res) == 1 else tuple(res)


def _shift_down(x, prev8, k):
    cat = jnp.concatenate([prev8, x], axis=0)
    return pltpu.roll(cat, k, 0)[SUBLANE:, :]


def _shift_up(x, next8, k):
    n = x.shape[0]
    cat = jnp.concatenate([x, next8], axis=0)
    return pltpu.roll(cat, n + SUBLANE - k, 0)[:n, :]


def _colsum(x):
    return jnp.sum(x, axis=0, keepdims=True)


def _silu(x):
    return x * jax.nn.sigmoid(x)


def _modulate_fn(x, nw, sc, sh):
    r = lax.rsqrt(jnp.mean(x * x, axis=-1, keepdims=True) + EPS)
    return (x * r * nw) * (1.0 + sc) + sh


def modulate_fwd(x, nw, sc, sh, name):
    L, D = x.shape

    def fn(i, nb, xt, nwv, scv, shv):
        return _modulate_fn(xt, nwv, scv, shv)

    return rowwise(fn, name=name, L=L, tm=_pick(L, 512, SUBLANE), rows=[(x, 0, D, "cur")],
                   consts=[nw, sc, sh], outs=[(D, bf16)])


def modulate_bwd(x, nw, sc, sh, dh, dx_in, name):
    L, D = x.shape

    def fn(i, nb, xt, dht, dxt, nwv, scv, shv):
        _, vjp = jax.vjp(_modulate_fn, xt, nwv, scv, shv)
        dx, dnw, dsc, dsh = vjp(dht)
        return dxt + dx, dnw, dsc, dsh

    return rowwise(fn, name=name, L=L, tm=_pick(L, 256, SUBLANE),
                   rows=[(x, 0, D, "cur"), (dh, 0, D, "cur"), (dx_in, 0, D, "cur")],
                   consts=[nw, sc, sh], outs=[(D, f32)], sums=[(1, D)] * 3)


def resid_bwd(dx, y, g, name):
    L, D = dx.shape

    def fn(i, nb, dxt, yt, gv):
        return dxt * gv, _colsum(dxt * yt)

    return rowwise(fn, name=name, L=L, tm=_pick(L, 512, SUBLANE),
                   rows=[(dx, 0, D, "cur"), (y, 0, D, "cur")], consts=[g],
                   outs=[(D, bf16)], sums=[(1, D)])


def _resid_epi(acc, xt, gv):
    return acc, xt + gv * acc


def _stack_rows(rows, n=SUBLANE):
    c = rows[0].shape[1]
    ridx = lax.broadcasted_iota(jnp.int32, (n, c), 0)
    out = jnp.zeros((n, c), f32)
    for j, r in enumerate(rows):
        out = out + jnp.where(ridx == j, r, 0.0)
    return out


def _conv_causal(x, prev8, w):
    W = w.shape[0]
    y = x * w[W - 1:W, :]
    for j in range(W - 1):
        y = y + _shift_down(x, prev8, W - 1 - j) * w[j:j + 1, :]
    return y


def _conv_causal_bwd_x(dy, next8, w):
    W = w.shape[0]
    dx = dy * w[W - 1:W, :]
    for j in range(W - 1):
        dx = dx + _shift_up(dy, next8, W - 1 - j) * w[j:j + 1, :]
    return dx


def _conv_causal_bwd_w(dy, x, prev8, W):
    rows = [_colsum(dy * _shift_down(x, prev8, W - 1 - j)) for j in range(W - 1)]
    rows.append(_colsum(dy * x))
    return _stack_rows(rows)


def ffn_act_fwd(up, conv_w, name):
    L, F2 = up.shape
    F = F2 // 2

    def fn(i, nb, u, p8, w):
        c = _conv_causal(u, p8 * (i > 0).astype(f32), w)
        return _silu(c[:, :F]) * c[:, F:]

    return rowwise(fn, name=name, L=L, tm=_pick(L, 128, SUBLANE),
                   rows=[(up, 0, F2, "cur"), (up, 0, F2, "prev")], consts=[conv_w], outs=[(F, bf16)])


def ffn_act_bwd1(up, conv_w, dact, name):
    L, F2 = up.shape
    F = F2 // 2

    def fn(i, nb, u, da, p8, w):
        c = _conv_causal(u, p8 * (i > 0).astype(f32), w)
        a, b = c[:, :F], c[:, F:]
        sg = jax.nn.sigmoid(a)
        dadt = sg * (1.0 + a * (1.0 - sg))
        return jnp.concatenate([da * b * dadt, da * a * sg], axis=1)

    return rowwise(fn, name=name, L=L, tm=_pick(L, 128, SUBLANE),
                   rows=[(up, 0, F2, "cur"), (dact, 0, F, "cur"), (up, 0, F2, "prev")],
                   consts=[conv_w], outs=[(F2, f32)])


def conv_bwd(x, w, dc, name, out_dtype=bf16):
    L, C = x.shape
    W = w.shape[0]

    def fn(i, nb, xt, dct, p8, n8, wv):
        dx = _conv_causal_bwd_x(dct, n8 * (i < nb - 1).astype(f32), wv)
        dw = _conv_causal_bwd_w(dct, xt, p8 * (i > 0).astype(f32), W)
        return dx, dw

    return rowwise(fn, name=name, L=L, tm=_pick(L, 128, SUBLANE),
                   rows=[(x, 0, C, "cur"), (dc, 0, C, "cur"), (x, 0, C, "prev"), (dc, 0, C, "next")],
                   consts=[w], outs=[(C, out_dtype)], sums=[(SUBLANE, C)])


ALIBI = [2.0 ** (-8.0 * (i + 1) / N_ATTN_HEADS) for i in range(N_ATTN_HEADS)]
NEG = -1e30


class _Band:
    def __init__(self, dilation, group_a):
        d = dilation
        self.d = d
        self.group_a = group_a
        if group_a:
            self.P, self.qw, self.hps, self.kvw = 1, 512, 8, 128
            self.qcol = lambda p: 0
            self.kcol = lambda p: 4
            self.vcol = lambda p: 5
            self.kv_of = lambda j: j // 4
            self.max_dist = A_WINDOW - 1
            sl = np.repeat(np.asarray(ALIBI[:8], np.float32), HEAD_DIM)[None, None, :]
        else:
            self.P, self.qw, self.hps, self.kvw = 2 * d, 256, 4, 256
            self.qcol = lambda p: lax.div(p, 2) * 9 + 3 + lax.rem(p, 2)
            self.kcol = lambda p: lax.div(p, 2) * 9 + 5 + lax.rem(p, 2)
            self.vcol = lambda p: lax.div(p, 2) * 9 + 7 + lax.rem(p, 2)
            self.kv_of = lambda j: j
            self.max_dist = BLOCK
            per = np.repeat(np.asarray(ALIBI[8:], np.float32), HEAD_DIM).reshape(2, 1, 256)
            sl = np.tile(per, (d, 1, 1))
        self.slopes = jnp.asarray(sl, f32)


def _band_mask(n, d, max_dist):
    qi = lax.broadcasted_iota(jnp.int32, (BLOCK, 2 * BLOCK), 0)
    kj = lax.broadcasted_iota(jnp.int32, (BLOCK, 2 * BLOCK), 1)
    dist = BLOCK + qi - kj
    valid = (dist >= 0) & (dist <= max_dist) & ((n > 0) | (kj >= BLOCK))
    return valid, -(d * dist).astype(f32)


def _rms64(x, w):
    r = lax.rsqrt(jnp.mean(x * x, axis=-1, keepdims=True) + EPS)
    xh = x * r
    return xh * w, xh, r


def _rms64_bwd(dy, xh, r, w):
    t = dy * w
    return r * (t - xh * jnp.mean(t * xh, axis=-1, keepdims=True)), _colsum(dy * xh)


def attn_fwd(hv, band, wq, wk, name):
    M = hv.shape[0]
    nb = M // BLOCK
    P, qw, hps = band.P, band.qw, band.hps
    d, max_dist, kv_of = band.d, band.max_dist, band.kv_of

    def body(q_ref, kp_ref, kc_ref, vp_ref, vc_ref, sl_ref, wq_ref, wk_ref, o_ref, lse_ref):
        n = pl.program_id(1)
        valid, negd = _band_mask(n, d, max_dist)
        kblk = jnp.concatenate([kp_ref[...], kc_ref[...]], axis=0)
        vblk = jnp.concatenate([vp_ref[...], vc_ref[...]], axis=0)
        wqv, wkv = wq_ref[...], wk_ref[...]
        kn_cache = {}
        for j in range(hps):
            h = kv_of(j)
            if h not in kn_cache:
                kn_cache[h] = _rms64(kblk[:, h * 64:(h + 1) * 64], wkv)[0].astype(bf16)
            kn = kn_cache[h]
            v = vblk[:, h * 64:(h + 1) * 64].astype(bf16)
            qn = _rms64(q_ref[:, j * 64:(j + 1) * 64], wqv)[0].astype(bf16)
            slope = sl_ref[0, :, j * 64:j * 64 + 1]
            s = _dot(qn, kn, NT) * (HEAD_DIM ** -0.5) + slope * negd
            s = jnp.where(valid, s, NEG)
            m = jnp.max(s, axis=-1, keepdims=True)
            p = jnp.exp(s - m)
            l = jnp.sum(p, axis=-1, keepdims=True)
            o = _dot(p.astype(bf16), v) / l
            o_ref[:, j * 64:(j + 1) * 64] = o
            lse_ref[:, j * 64:(j + 1) * 64] = jnp.broadcast_to(m + jnp.log(l), (BLOCK, 64))

    qcol, kcol, vcol, kvw = band.qcol, band.kcol, band.vcol, band.kvw
    in_specs = [
        pl.BlockSpec((BLOCK, qw), lambda p, n: (n, qcol(p))),
        pl.BlockSpec((BLOCK, kvw), lambda p, n: (jnp.maximum(n - 1, 0), kcol(p))),
        pl.BlockSpec((BLOCK, kvw), lambda p, n: (n, kcol(p))),
        pl.BlockSpec((BLOCK, kvw), lambda p, n: (jnp.maximum(n - 1, 0), vcol(p))),
        pl.BlockSpec((BLOCK, kvw), lambda p, n: (n, vcol(p))),
        pl.BlockSpec((1, 1, qw), lambda p, n: (p, 0, 0)),
        pl.BlockSpec((1, 64), lambda p, n: (0, 0)),
        pl.BlockSpec((1, 64), lambda p, n: (0, 0)),
    ]
    o_spec = pl.BlockSpec((BLOCK, qw), lambda p, n: (n, p))
    return pl.pallas_call(
        body, grid=(P, nb), in_specs=in_specs, out_specs=[o_spec, o_spec],
        out_shape=[jax.ShapeDtypeStruct((M, P * qw), f32)] * 2,
        compiler_params=_cp("parallel", "parallel"), name=name,
    )(hv, hv, hv, hv, hv, band.slopes, wq, wk)


def attn_bwd(hv, band, wq, wk, o, lse, do, dlse, dw0, name):
    M = hv.shape[0]
    nb = M // BLOCK
    P, qw, hps = band.P, band.qw, band.hps
    d, max_dist, kv_of = band.d, band.max_dist, band.kv_of
    kv_heads = sorted({kv_of(j) for j in range(hps)})

    def body(q_ref, kp_ref, kc_ref, vp_ref, vc_ref, sl_ref, wq_ref, wk_ref, o_ref, lse_ref, do_ref, dlse_ref,
             dwq0_ref, dwk0_ref, dq_ref, dk_ref, dv_ref, dwq_ref, dwk_ref, ck, cv):
        pp = pl.program_id(0)
        n = pl.program_id(1)

        @pl.when((pp == 0) & (n == 0))
        def _():
            dwq_ref[...] = dwq0_ref[...]
            dwk_ref[...] = dwk0_ref[...]

        @pl.when(n == 0)
        def _():
            ck[...] = jnp.zeros_like(ck)
            cv[...] = jnp.zeros_like(cv)

        @pl.when(n < nb)
        def _():
            valid, negd = _band_mask(n, d, max_dist)
            kblk = jnp.concatenate([kp_ref[...], kc_ref[...]], axis=0)
            vblk = jnp.concatenate([vp_ref[...], vc_ref[...]], axis=0)
            wqv, wkv = wq_ref[...], wk_ref[...]
            kn_c, dkn, dvv = {}, {}, {}
            dwq_acc = jnp.zeros((1, 64), f32)
            for j in range(hps):
                h = kv_of(j)
                if h not in kn_c:
                    kn_c[h] = _rms64(kblk[:, h * 64:(h + 1) * 64], wkv)
                    dkn[h] = jnp.zeros((2 * BLOCK, 64), f32)
                    dvv[h] = jnp.zeros((2 * BLOCK, 64), f32)
                kn = kn_c[h][0].astype(bf16)
                v = vblk[:, h * 64:(h + 1) * 64].astype(bf16)
                qn_f, qh, rq = _rms64(q_ref[:, j * 64:(j + 1) * 64], wqv)
                qn = qn_f.astype(bf16)
                slope = sl_ref[0, :, j * 64:j * 64 + 1]
                s = _dot(qn, kn, NT) * (HEAD_DIM ** -0.5) + slope * negd
                p = jnp.where(valid, jnp.exp(s - lse_ref[:, j * 64:j * 64 + 1]), 0.0)
                do_j = do_ref[:, j * 64:(j + 1) * 64]
                delta = jnp.sum(do_j * o_ref[:, j * 64:(j + 1) * 64], axis=-1, keepdims=True)
                dp = _dot(do_j.astype(bf16), v, NT)
                ds = (p * (dp - delta + dlse_ref[:, j * 64:j * 64 + 1])).astype(bf16)
                dqn = _dot(ds, kn) * (HEAD_DIM ** -0.5)
                dkn[h] = dkn[h] + _dot(ds, qn, TN) * (HEAD_DIM ** -0.5)
                dvv[h] = dvv[h] + _dot(p.astype(bf16), do_j.astype(bf16), TN)
                dq, dwq_j = _rms64_bwd(dqn, qh, rq, wqv)
                dwq_acc = dwq_acc + dwq_j
                dq_ref[:, j * 64:(j + 1) * 64] = dq
            dwq_ref[...] += dwq_acc
            dks, dwk_acc = [], jnp.zeros((1, 64), f32)
            for h in kv_heads:
                dk_h, dwk_h = _rms64_bwd(dkn[h], kn_c[h][1], kn_c[h][2], wkv)
                dks.append(dk_h)
                dwk_acc = dwk_acc + dwk_h
            dwk_ref[...] += dwk_acc
            dk_all = jnp.concatenate(dks, axis=1)
            dv_all = jnp.concatenate([dvv[h] for h in kv_heads], axis=1)
            dk_ref[...] = ck[...] + dk_all[:BLOCK]
            dv_ref[...] = cv[...] + dv_all[:BLOCK]
            ck[...] = dk_all[BLOCK:]
            cv[...] = dv_all[BLOCK:]

        @pl.when(n == nb)
        def _():
            dk_ref[...] = ck[...]
            dv_ref[...] = cv[...]

    qcol, kcol, vcol, kvw = band.qcol, band.kcol, band.vcol, band.kvw
    cl = lambda n: jnp.minimum(n, nb - 1)
    pv = lambda n: jnp.maximum(jnp.minimum(n, nb - 1) - 1, 0)
    o_in = pl.BlockSpec((BLOCK, qw), lambda p, n: (cl(n), p))
    in_specs = [
        pl.BlockSpec((BLOCK, qw), lambda p, n: (cl(n), qcol(p))),
        pl.BlockSpec((BLOCK, kvw), lambda p, n: (pv(n), kcol(p))),
        pl.BlockSpec((BLOCK, kvw), lambda p, n: (cl(n), kcol(p))),
        pl.BlockSpec((BLOCK, kvw), lambda p, n: (pv(n), vcol(p))),
        pl.BlockSpec((BLOCK, kvw), lambda p, n: (cl(n), vcol(p))),
        pl.BlockSpec((1, 1, qw), lambda p, n: (p, 0, 0)),
        pl.BlockSpec((1, 64), lambda p, n: (0, 0)),
        pl.BlockSpec((1, 64), lambda p, n: (0, 0)),
        o_in, o_in, o_in, o_in,
        pl.BlockSpec((1, 64), lambda p, n: (0, 0)),
        pl.BlockSpec((1, 64), lambda p, n: (0, 0)),
    ]
    kv_out = pl.BlockSpec((BLOCK, kvw), lambda p, n: (jnp.maximum(n - 1, 0), p))
    w_out = pl.BlockSpec((1, 64), lambda p, n: (0, 0))
    return pl.pallas_call(
        body, grid=(P, nb + 1), in_specs=in_specs,
        out_specs=[o_in, kv_out, kv_out, w_out, w_out],
        out_shape=[jax.ShapeDtypeStruct((M, P * qw), f32), jax.ShapeDtypeStruct((M, P * kvw), f32),
                   jax.ShapeDtypeStruct((M, P * kvw), f32), jax.ShapeDtypeStruct((1, 64), f32),
                   jax.ShapeDtypeStruct((1, 64), f32)],
        scratch_shapes=[pltpu.VMEM((BLOCK, kvw), f32), pltpu.VMEM((BLOCK, kvw), f32)],
        compiler_params=_cp("arbitrary", "arbitrary"), name=name,
    )(hv, hv, hv, hv, hv, band.slopes, wq, wk, o, lse, do, dlse, *dw0)


def _head_sum(x):
    c = x.shape[1]
    r = lax.broadcasted_iota(jnp.int32, (c, c), 0) // HEAD_DIM
    q = lax.broadcasted_iota(jnp.int32, (c, c), 1) // HEAD_DIM
    return _dot(x, (r == q).astype(f32), precision=HI)


def attn_merge_fwd(oa, la, obs, lbs, sinkb, name):
    L = oa.shape[0]

    def fn(i, nb, oa_t, la_t, o1, o2, o3, l1, l2, l3, sk):
        ya = oa_t * jax.nn.sigmoid(la_t - sk)
        m = jnp.maximum(jnp.maximum(l1, l2), l3)
        e1, e2, e3 = jnp.exp(l1 - m), jnp.exp(l2 - m), jnp.exp(l3 - m)
        yb = (e1 * o1 + e2 * o2 + e3 * o3) / (e1 + e2 + e3)
        return jnp.concatenate([ya, yb], axis=1)

    rows = [(a, 0, 512, "cur") for a in (oa, la, *obs, *lbs)]
    return rowwise(fn, name=name, L=L, tm=_pick(L, 256, SUBLANE), rows=rows, consts=[sinkb], outs=[(1024, bf16)])


def attn_merge_bwd(dcat, oa, la, obs, lbs, sinkb, name):
    L = oa.shape[0]

    def fn(i, nb, da, db, oa_t, la_t, o1, o2, o3, l1, l2, l3, sk):
        keep = jax.nn.sigmoid(la_t - sk)
        dla = _head_sum(da * oa_t) * keep * (1.0 - keep)
        m = jnp.maximum(jnp.maximum(l1, l2), l3)
        e1, e2, e3 = jnp.exp(l1 - m), jnp.exp(l2 - m), jnp.exp(l3 - m)
        z = e1 + e2 + e3
        w1, w2, w3 = e1 / z, e2 / z, e3 / z
        g1, g2, g3 = _head_sum(db * o1), _head_sum(db * o2), _head_sum(db * o3)
        gm = w1 * g1 + w2 * g2 + w3 * g3
        return (da * keep, dla, w1 * db, w2 * db, w3 * db,
                w1 * (g1 - gm), w2 * (g2 - gm), w3 * (g3 - gm), -_colsum(dla))

    rows = [(dcat, 0, 512, "cur"), (dcat, 512, 512, "cur")] + [(a, 0, 512, "cur") for a in (oa, la, *obs, *lbs)]
    return rowwise(fn, name=name, L=L, tm=_pick(L, 256, SUBLANE), rows=rows, consts=[sinkb],
                   outs=[(512, f32)] * 8, sums=[(1, 512)])


def attn_assemble(dqa, dka, dva, dqs, dks, dvs, name):
    L = dqa.shape[0]

    def fn(i, nb, qa, ka, va, q1, q2, q3, k1, k2, k3, v1, v2, v3):
        return jnp.concatenate([qa, ka, va, q1 + q2 + q3, k1 + k2 + k3, v1 + v2 + v3], axis=1)

    rows = [(dqa, 0, 512, "cur"), (dka, 0, 128, "cur"), (dva, 0, 128, "cur")]
    rows += [(a, 0, 512, "cur") for a in (*dqs, *dks, *dvs)]
    return rowwise(fn, name=name, L=L, tm=_pick(L, 256, SUBLANE), rows=rows, outs=[(ATTN_IN, bf16)])


def attention_fwd(hin, wqa, wka, wqb, wkb, sinkb):
    L = hin.shape[0]
    oa, la = attn_fwd(hin, _Band(1, True), wqa, wka, "attn_a_fwd")
    obs, lbs = [], []
    for _, d in B_BRANCHES:
        o, l = attn_fwd(hin.reshape(L // d, d * ATTN_IN), _Band(d, False), wqb, wkb, f"attn_b{d}_fwd")
        obs.append(o.reshape(L, 512))
        lbs.append(l.reshape(L, 512))
    ocat = attn_merge_fwd(oa, la, obs, lbs, sinkb, "attn_merge_fwd")
    return ocat, (oa, la, obs, lbs)


def attention_bwd(hin, wqa, wka, wqb, wkb, sinkb, saved, dcat):
    L = hin.shape[0]
    oa, la, obs, lbs = saved
    res = attn_merge_bwd(dcat, oa, la, obs, lbs, sinkb, "attn_merge_bwd")
    doa, dla, dos, dls, dsink = res[0], res[1], res[2:5], res[5:8], res[8]
    zero = jnp.zeros((1, 64), f32)
    dqa, dka, dva, dwqa, dwka = attn_bwd(hin, _Band(1, True), wqa, wka, oa, la, doa, dla, (zero, zero), "attn_a_bwd")
    dqs, dks, dvs = [], [], []
    dwqb = dwkb = zero
    for g, (_, d) in enumerate(B_BRANCHES):
        M = L // d
        rs = lambda a: a.reshape(M, d * 512)
        dq, dk, dv, dwqb, dwkb = attn_bwd(hin.reshape(M, d * ATTN_IN), _Band(d, False), wqb, wkb, rs(obs[g]),
                                          rs(lbs[g]), rs(dos[g]), rs(dls[g]), (dwqb, dwkb), f"attn_b{d}_bwd")
        dqs.append(dq.reshape(L, 512))
        dks.append(dk.reshape(L, 512))
        dvs.append(dv.reshape(L, 512))
    dhin = attn_assemble(dqa, dka, dva, dqs, dks, dvs, "attn_assemble")
    return dhin, dwqa, dwka, dwqb, dwkb, dsink


NS = S5_GROUPS * S5_STATE


def _s5_param_fn(lr, li, ldt):
    dt = jnp.exp(ldt)
    mag, ang = jnp.exp(lr * dt), li * dt
    ab_re, ab_im = mag * jnp.cos(ang), mag * jnp.sin(ang)
    nr, ni = ab_re - 1.0, ab_im
    den = lr * lr + li * li
    return ab_re, ab_im, (nr * lr + ni * li) / den, (ni * lr - nr * li) / den


def s5_params_fwd(lr, li, ldt):
    def body(lr_ref, li_ref, ldt_ref, *outs):
        for o_ref, o in zip(outs, _s5_param_fn(lr_ref[...], li_ref[...], ldt_ref[...])):
            o_ref[...] = o

    return pl.pallas_call(body, out_shape=[jax.ShapeDtypeStruct(lr.shape, f32)] * 4, name="s5_params_fwd")(lr, li, ldt)


def s5_params_bwd(lr, li, ldt, cts):
    def body(lr_ref, li_ref, ldt_ref, c0, c1, c2, c3, dlr, dli, dldt):
        _, vjp = jax.vjp(_s5_param_fn, lr_ref[...], li_ref[...], ldt_ref[...])
        a, b, c = vjp((c0[...], c1[...], c2[...], c3[...]))
        dlr[...] = a
        dli[...] = b
        dldt[...] = c

    return pl.pallas_call(
        body, out_shape=[jax.ShapeDtypeStruct(lr.shape, f32), jax.ShapeDtypeStruct(li.shape, f32),
                         jax.ShapeDtypeStruct(ldt.shape, f32)], name="s5_params_bwd")(lr, li, ldt, *cts)


def _cmul(ar, ai, br, bi):
    return ar * br - ai * bi, ar * bi + ai * br


def s5_scan(z, ab_re, ab_im, f_re, f_im, *, reverse, name):
    L = z.shape[0]
    tm = _pick(L, 256, SUBLANE)
    nb = L // tm
    ng = tm // SUBLANE
    use_f = f_re is not None
    consts = [ab_re, ab_im] + ([f_re, f_im] if use_f else [])

    def body(*refs):
        z_ref = refs[0]
        c_refs = refs[1:1 + len(consts)]
        x_ref, car = refs[1 + len(consts)], refs[2 + len(consts)]
        i = pl.program_id(0)

        @pl.when(i == 0)
        def _():
            car[...] = jnp.zeros_like(car)

        a1 = (c_refs[0][...], c_refs[1][...])
        a2 = _cmul(*a1, *a1)
        a3 = _cmul(*a2, *a1)
        a4 = _cmul(*a2, *a2)
        pw = [a1, a2, a3, a4, _cmul(*a4, *a1), _cmul(*a4, *a2), _cmul(*a4, *a3), _cmul(*a4, *a4)]
        if reverse:
            pw = pw[::-1]
        pw_re = _stack_rows([p[0] for p in pw])
        pw_im = _stack_rows([p[1] for p in pw])
        ridx = lax.broadcasted_iota(jnp.int32, (SUBLANE, NS), 0)
        if use_f:
            fr, fi = c_refs[2][...], c_refs[3][...]

        def group(s, carry):
            cr, ci = carry
            g = (ng - 1 - s) if reverse else s
            r0 = pl.multiple_of(g * SUBLANE, SUBLANE)
            xr = z_ref[pl.ds(r0, SUBLANE), 0:NS]
            xi = z_ref[pl.ds(r0, SUBLANE), NS:2 * NS]
            if use_f:
                xr, xi = _cmul(fr, fi, xr, xi)
            for sft, (pr, pi) in ((1, a1), (2, a2), (4, a4)):
                if reverse:
                    keep = ridx < SUBLANE - sft
                    sr = jnp.where(keep, pltpu.roll(xr, SUBLANE - sft, 0), 0.0)
                    si = jnp.where(keep, pltpu.roll(xi, SUBLANE - sft, 0), 0.0)
                else:
                    keep = ridx >= sft
                    sr = jnp.where(keep, pltpu.roll(xr, sft, 0), 0.0)
                    si = jnp.where(keep, pltpu.roll(xi, sft, 0), 0.0)
                tr, ti = _cmul(pr, pi, sr, si)
                xr, xi = xr + tr, xi + ti
            tr, ti = _cmul(pw_re, pw_im, cr, ci)
            xr, xi = xr + tr, xi + ti
            x_ref[pl.ds(r0, SUBLANE), 0:NS] = xr
            x_ref[pl.ds(r0, SUBLANE), NS:2 * NS] = xi
            row = 0 if reverse else SUBLANE - 1
            return xr[row:row + 1, :], xi[row:row + 1, :]

        cr, ci = lax.fori_loop(0, ng, group, (car[0:1, 0:NS], car[0:1, NS:2 * NS]))
        car[0:1, 0:NS] = cr
        car[0:1, NS:2 * NS] = ci

    blk = (lambda i: (nb - 1 - i, 0)) if reverse else (lambda i: (i, 0))
    return pl.pallas_call(
        body, grid=(nb,),
        in_specs=[pl.BlockSpec((tm, 2 * NS), blk)] + [pl.BlockSpec((1, NS), lambda i: (0, 0))] * len(consts),
        out_specs=pl.BlockSpec((tm, 2 * NS), blk),
        out_shape=jax.ShapeDtypeStruct((L, 2 * NS), f32),
        scratch_shapes=[pltpu.VMEM((SUBLANE, 2 * NS), f32)],
        compiler_params=_cp("arbitrary"), name=name,
    )(z, *consts)


def _s5_post_fn(ypre, u, dvec, gw, gb):
    y = ypre + dvec * u
    g = jax.nn.gelu(y)
    z = _dot(g.astype(bf16), gw.astype(bf16)) + gb
    return g * jax.nn.sigmoid(z)


def s5_post_fwd(ypre, hin, dvec, gw, gb):
    L = ypre.shape[0]

    def fn(i, nb, yt, ut, dv, gwv, gbv):
        return _s5_post_fn(yt, ut, dv, gwv, gbv)

    return rowwise(fn, name="s5_post_fwd", L=L, tm=_pick(L, 512, SUBLANE),
                   rows=[(ypre, 0, S5_WIDTH, "cur"), (hin, 3072, S5_WIDTH, "cur")],
                   consts=[dvec, gw, gb], outs=[(S5_WIDTH, f32)])


def s5_post_bwd(ypre, hin, dvec, gw, gb, dycat):
    L = ypre.shape[0]

    def fn(i, nb, yt, ut, dyt, dv, gwv, gbv):
        _, vjp = jax.vjp(_s5_post_fn, yt, ut, dv, gwv, gbv)
        return vjp(dyt)

    return rowwise(fn, name="s5_post_bwd", L=L, tm=_pick(L, 512, SUBLANE),
                   rows=[(ypre, 0, S5_WIDTH, "cur"), (hin, 3072, S5_WIDTH, "cur"), (dycat, 0, S5_WIDTH, "cur")],
                   consts=[dvec, gw, gb], outs=[(S5_WIDTH, f32)] * 2,
                   sums=[(1, S5_WIDTH), (S5_WIDTH, S5_WIDTH), (1, S5_WIDTH)])


def s5_acc(G, X, bu, f_re, f_im):
    L = G.shape[0]

    def fn(i, nb, g, x, b, xp8, fr, fi):
        gr, gi = g[:, :NS], g[:, NS:]
        xp = _shift_down(x, xp8 * (i > 0).astype(f32), 1)
        xr, xi = xp[:, :NS], xp[:, NS:]
        br, bi = b[:, :NS], b[:, NS:]
        dbu = jnp.concatenate([fr * gr + fi * gi, fr * gi - fi * gr], axis=1)
        return (dbu, _colsum(xr * gr + xi * gi), _colsum(xr * gi - xi * gr),
                _colsum(br * gr + bi * gi), _colsum(br * gi - bi * gr))

    return rowwise(fn, name="s5_acc", L=L, tm=_pick(L, 256, SUBLANE),
                   rows=[(G, 0, 2 * NS, "cur"), (X, 0, 2 * NS, "cur"), (bu, 0, 2 * NS, "cur"), (X, 0, 2 * NS, "prev")],
                   consts=[f_re, f_im], outs=[(2 * NS, bf16)], sums=[(1, NS)] * 4)


def _s5_blockdiag(b_re, b_im, c_re, c_im):
    eye = jnp.eye(S5_GROUPS, dtype=f32)
    bb = lambda b: jnp.einsum("gpi,gh->gihp", b, eye).reshape(S5_WIDTH, NS)
    cc = lambda c: jnp.einsum("gip,gh->gphi", c, eye).reshape(NS, S5_WIDTH)
    return jnp.concatenate([bb(b_re), bb(b_im)], axis=1), jnp.concatenate([cc(c_re), -cc(c_im)], axis=0)


def _s5_blockdiag_grads(dB, dC):
    gb = lambda m: jnp.einsum("gigp->gpi", m.reshape(S5_GROUPS, S5_GROUP, S5_GROUPS, S5_STATE))
    gc = lambda m: jnp.einsum("gpgi->gip", m.reshape(S5_GROUPS, S5_STATE, S5_GROUPS, S5_GROUP))
    return gb(dB[:, :NS]), gb(dB[:, NS:]), gc(dC[:NS]), -gc(dC[NS:])


def s5_fwd(hin, prm):
    ab_re, ab_im, f_re, f_im = s5_params_fwd(prm["lr"], prm["li"], prm["ldt"])
    flat = lambda a: a.reshape(1, NS)
    ab_re, ab_im, f_re, f_im = flat(ab_re), flat(ab_im), flat(f_re), flat(f_im)
    Bblk, Cblk = _s5_blockdiag(prm["b_re"], prm["b_im"], prm["c_re"], prm["c_im"])
    bu = mm(hin, Bblk, name="s5_bu", a_win=(3072, S5_WIDTH))
    X = s5_scan(bu, ab_re, ab_im, f_re, f_im, reverse=False, name="s5_scan_fwd")
    ypre = mm(X, Cblk, name="s5_y")
    yc = s5_post_fwd(ypre, hin, prm["d"], prm["gw"], prm["gb"])
    return yc, (ab_re, ab_im, f_re, f_im, Bblk, Cblk, bu, X, ypre)


def s5_bwd(hin, prm, saved, dycat):
    ab_re, ab_im, f_re, f_im, Bblk, Cblk, bu, X, ypre = saved
    dypre, du_skip, dd, dgw, dgb = s5_post_bwd(ypre, hin, prm["d"], prm["gw"], prm["gb"], dycat)
    dX = mm(dypre, Cblk, tb=True, name="s5_dx")
    dC = mm(X, dypre, ta=True, name="s5_dc")
    G = s5_scan(dX, ab_re, -ab_im, None, None, reverse=True, name="s5_scan_bwd")
    dbu, dar, dai, dfr, dfi = s5_acc(G, X, bu, f_re, f_im)
    dB = mm(hin, dbu, ta=True, a_win=(3072, S5_WIDTH), name="s5_db")
    du_b = mm(dbu, Bblk, tb=True, name="s5_du")
    sh = prm["lr"].shape
    dlr, dli, dldt = s5_params_bwd(prm["lr"], prm["li"], prm["ldt"],
                                   [a.reshape(sh) for a in (dar, dai, dfr, dfi)])
    db_re, db_im, dc_re, dc_im = _s5_blockdiag_grads(dB, dC)
    grads = dict(lr=dlr, li=dli, ldt=dldt, b_re=db_re, b_im=db_im, c_re=dc_re, c_im=dc_im, d=dd, gw=dgw, gb=dgb)
    return du_skip, du_b, grads


DN_W = DN_HEADS * DN_DK
QKV_W = 3 * DN_W


def _softplus(x):
    return jnp.maximum(x, 0.0) + jnp.log(1.0 + jnp.exp(-jnp.abs(x)))


def _dn_pre(c, ab, alog, dtb):
    s = _silu(c)
    parts = []
    for h in range(2 * DN_HEADS):
        sh = s[:, h * 128:(h + 1) * 128]
        scale = DN_DK ** -0.5 if h < DN_HEADS else 1.0
        parts.append(sh * (lax.rsqrt(jnp.sum(sh * sh, axis=-1, keepdims=True) + EPS) * scale))
    parts.append(s[:, 2 * DN_W:])
    g = -jnp.exp(alog) * _softplus(ab[:, :128] + dtb)
    beta = jax.nn.sigmoid(ab[:, 128:])
    return jnp.concatenate(parts, axis=1), jnp.concatenate([g, beta], axis=1)


def _dn_pre_bwd(c, ab, alog, dtb, dqkv, dgb):
    sg = jax.nn.sigmoid(c)
    s = c * sg
    parts = []
    for h in range(2 * DN_HEADS):
        sh = s[:, h * 128:(h + 1) * 128]
        dy = dqkv[:, h * 128:(h + 1) * 128]
        scale = DN_DK ** -0.5 if h < DN_HEADS else 1.0
        r = lax.rsqrt(jnp.sum(sh * sh, axis=-1, keepdims=True) + EPS)
        parts.append(scale * r * (dy - sh * (r * r) * jnp.sum(dy * sh, axis=-1, keepdims=True)))
    parts.append(dqkv[:, 2 * DN_W:])
    dc = jnp.concatenate(parts, axis=1) * (sg * (1.0 + c * (1.0 - sg)))
    pre = ab[:, :128] + dtb
    ea = jnp.exp(alog)
    dg = dgb[:, :128]
    da = dg * (-ea) * jax.nn.sigmoid(pre)
    dalog = _colsum(dg * (-ea) * _softplus(pre))
    beta = jax.nn.sigmoid(ab[:, 128:])
    db = dgb[:, 128:] * beta * (1.0 - beta)
    return dc, jnp.concatenate([da, db], axis=1), dalog, _colsum(da)


def dn_pre_fwd(hin, conv_w, alog, dtb):
    L = hin.shape[0]

    def fn(i, nb, x, ab, p8, w, al, db):
        c = _conv_causal(x, p8 * (i > 0).astype(f32), w)
        return _dn_pre(c, ab, al, db)

    return rowwise(fn, name="dn_pre_fwd", L=L, tm=_pick(L, 256, SUBLANE),
                   rows=[(hin, 0, QKV_W, "cur"), (hin, 3328, 256, "cur"), (hin, 0, QKV_W, "prev")],
                   consts=[conv_w, alog, dtb], outs=[(QKV_W, f32), (256, f32)])


def dn_pre_bwd(hin, conv_w, alog, dtb, dqkv3, dg, dbeta):
    L = hin.shape[0]

    def fn(i, nb, x, ab, dq, dk, dv, dgt, dbt, p8, w, al, db):
        c = _conv_causal(x, p8 * (i > 0).astype(f32), w)
        return _dn_pre_bwd(c, ab, al, db, jnp.concatenate([dq, dk, dv], axis=1), jnp.concatenate([dgt, dbt], axis=1))

    rows = [(hin, 0, QKV_W, "cur"), (hin, 3328, 256, "cur")] + [(a, 0, DN_W, "cur") for a in dqkv3]
    rows += [(dg, 0, 128, "cur"), (dbeta, 0, 128, "cur"), (hin, 0, QKV_W, "prev")]
    return rowwise(fn, name="dn_pre_bwd", L=L, tm=_pick(L, 128, SUBLANE), rows=rows,
                   consts=[conv_w, alog, dtb], outs=[(QKV_W, f32), (256, f32)], sums=[(1, 128), (1, 128)])


def _split(a):
    hi = a.astype(bf16)
    return hi, (a - hi.astype(f32)).astype(bf16)


def _dot3_raw(a, b, dims):
    ah, al = _split(a)
    bh, bl = _split(b)
    return _dot(ah, bh, dims) + (_dot(ah, bl, dims) + _dot(al, bh, dims))


@functools.partial(jax.custom_vjp, nondiff_argnums=(2,))
def _dot3(a, b, dims=NN):
    return _dot3_raw(a, b, dims)


def _dot3_fwd(a, b, dims):
    return _dot3_raw(a, b, dims), (a, b)


BNN = (((2,), (1,)), ((0,), (0,)))
BNT = (((2,), (2,)), ((0,), (0,)))
BTN = (((1,), (1,)), ((0,), (0,)))


def _dot3_bwd(dims, res, g):
    a, b = res
    nn, nt, tn = (BNN, BNT, BTN) if dims[1][0] else (NN, NT, TN)
    if dims == nn:
        return _dot3_raw(g, b, nt), _dot3_raw(a, g, tn)
    if dims == nt:
        return _dot3_raw(g, b, nn), _dot3_raw(g, a, tn)
    assert dims == tn
    return _dot3_raw(b, g, nt), _dot3_raw(a, g, nn)


_dot3.defvjp(_dot3_fwd, _dot3_bwd)


def _dn_chunk(q, k, v, gcol, bcol, S):
    C = q.shape[1]
    r = lax.broadcasted_iota(jnp.int32, (C, C), 0)
    c = lax.broadcasted_iota(jnp.int32, (C, C), 1)
    tril = (r >= c).astype(f32)
    strict = (r > c).astype(f32)
    eye = (r == c).astype(f32)
    hd = _dot3
    grow = jnp.sum(eye * gcol, axis=1, keepdims=True)
    Gcol = jnp.sum(tril * grow, axis=2, keepdims=True)
    Grow = jnp.sum(eye * Gcol, axis=1, keepdims=True)
    gamma = jnp.exp((Gcol - Grow) * tril) * tril
    nmat = strict * bcol * hd(k, k, BNT) * gamma
    T = eye - nmat
    Pw = hd(nmat, nmat, BNN)
    for step in range(5):
        T = T + hd(T, Pw, BNN)
        if step < 4:
            Pw = hd(Pw, Pw, BNN)
    eG = jnp.exp(Gcol)
    u = hd(T, bcol * v, BNN)
    w = hd(T, (bcol * eG) * k, BNN)
    qk = hd(q, k, BNT) * gamma
    vnew = u - hd(w, S, BNN)
    o = hd(q * eG, S, BNN) + hd(qk, vnew, BNN)
    Glast = jnp.sum(gcol, axis=1, keepdims=True)
    S2 = S * jnp.exp(Glast) + hd(k * jnp.exp(Glast - Gcol), vnew, BTN)
    return o, S2


def _heads(x_ref):
    return jnp.stack([x_ref[:, h * 128:(h + 1) * 128] for h in range(DN_HEADS)])


def _head_cols(g_ref):
    return jnp.stack([g_ref[:, h:h + 1] for h in range(DN_HEADS)])


def dn_chunks_fwd(qkvn, gb):
    L = qkvn.shape[0]
    C = DN_CHUNK
    nc = L // C

    def body(q_ref, k_ref, v_ref, g_ref, b_ref, o_ref, sin_ref, S):
        n = pl.program_id(0)

        @pl.when(n == 0)
        def _():
            S[...] = jnp.zeros_like(S)

        s_in = S[...]
        sin_ref[...] = s_in
        o, s2 = _dn_chunk(_heads(q_ref), _heads(k_ref), _heads(v_ref), _head_cols(g_ref), _head_cols(b_ref), s_in)
        for h in range(DN_HEADS):
            o_ref[:, h * 128:(h + 1) * 128] = o[h]
        S[...] = s2

    blk = lambda j: pl.BlockSpec((C, DN_W), lambda n, j=j: (n, j))
    gblk = lambda j: pl.BlockSpec((C, 128), lambda n, j=j: (n, j))
    return pl.pallas_call(
        body, grid=(nc,),
        in_specs=[blk(0), blk(1), blk(2), gblk(0), gblk(1)],
        out_specs=[pl.BlockSpec((C, DN_W), lambda n: (n, 0)),
                   pl.BlockSpec((DN_HEADS, None, 128, 128), lambda n: (0, n, 0, 0))],
        out_shape=[jax.ShapeDtypeStruct((L, DN_W), f32), jax.ShapeDtypeStruct((DN_HEADS, nc, 128, 128), f32)],
        scratch_shapes=[pltpu.VMEM((DN_HEADS, 128, 128), f32)],
        compiler_params=_cp("arbitrary"), name="dn_chunks_fwd",
    )(qkvn, qkvn, qkvn, gb, gb)


def dn_chunks_bwd(qkvn, gb, s_in, do):
    L = qkvn.shape[0]
    C = DN_CHUNK
    nc = L // C

    def body(q_ref, k_ref, v_ref, g_ref, b_ref, sin_ref, do_ref, dq_ref, dk_ref, dv_ref, dg_ref, db_ref, dS):
        n = pl.program_id(0)

        @pl.when(n == 0)
        def _():
            dS[...] = jnp.zeros_like(dS)

        args = (_heads(q_ref), _heads(k_ref), _heads(v_ref), _head_cols(g_ref), _head_cols(b_ref), sin_ref[...])
        _, vjp = jax.vjp(_dn_chunk, *args)
        dq, dk, dv, dg, db, ds = vjp((_heads(do_ref), dS[...]))
        lane = lax.broadcasted_iota(jnp.int32, (C, 128), 1)
        dg_all = jnp.zeros((C, 128), f32)
        db_all = jnp.zeros((C, 128), f32)
        for h in range(DN_HEADS):
            sl = slice(h * 128, (h + 1) * 128)
            dq_ref[:, sl] = dq[h]
            dk_ref[:, sl] = dk[h]
            dv_ref[:, sl] = dv[h]
            dg_all = dg_all + jnp.where(lane == h, dg[h], 0.0)
            db_all = db_all + jnp.where(lane == h, db[h], 0.0)
        dS[...] = ds
        dg_ref[...] = dg_all
        db_ref[...] = db_all

    rv = lambda n: nc - 1 - n
    blk = lambda j: pl.BlockSpec((C, DN_W), lambda n, j=j: (rv(n), j))
    gblk = lambda j: pl.BlockSpec((C, 128), lambda n, j=j: (rv(n), j))
    oblk = pl.BlockSpec((C, DN_W), lambda n: (rv(n), 0))
    gout = pl.BlockSpec((C, 128), lambda n: (rv(n), 0))
    return pl.pallas_call(
        body, grid=(nc,),
        in_specs=[blk(0), blk(1), blk(2), gblk(0), gblk(1),
                  pl.BlockSpec((DN_HEADS, None, 128, 128), lambda n: (0, rv(n), 0, 0)), oblk],
        out_specs=[oblk] * 3 + [gout] * 2,
        out_shape=[jax.ShapeDtypeStruct((L, DN_W), f32)] * 3 + [jax.ShapeDtypeStruct((L, 128), f32)] * 2,
        scratch_shapes=[pltpu.VMEM((DN_HEADS, 128, 128), f32)],
        compiler_params=_cp("arbitrary"), name="dn_chunks_bwd",
    )(qkvn, qkvn, qkvn, gb, gb, s_in, do)


def _dn_post(o, z, w):
    parts = []
    for h in range(DN_HEADS):
        oh = o[:, h * 128:(h + 1) * 128]
        r = lax.rsqrt(jnp.mean(oh * oh, axis=-1, keepdims=True) + EPS)
        parts.append(oh * r * w)
    return jnp.concatenate(parts, axis=1) * _silu(z)


def dn_post_fwd(o, hin, yc, onorm):
    L = o.shape[0]

    def fn(i, nb, ot, zt, yct, w):
        return jnp.concatenate([yct, _dn_post(ot, zt, w)], axis=1)

    return rowwise(fn, name="dn_post_fwd", L=L, tm=_pick(L, 256, SUBLANE),
                   rows=[(o, 0, DN_W, "cur"), (hin, 2304, DN_W, "cur"), (yc, 0, S5_WIDTH, "cur")],
                   consts=[onorm], outs=[(1024, bf16)])


def dn_post_bwd(o, hin, onorm, dycat):
    L = o.shape[0]

    def fn(i, nb, ot, zt, d0, d1, d2, w):
        dy = jnp.concatenate([d0, d1, d2], axis=1)
        sg = jax.nn.sigmoid(zt)
        sz = zt * sg
        dos, dw = [], jnp.zeros((1, 128), f32)
        nrm = []
        for h in range(DN_HEADS):
            sl = slice(h * 128, (h + 1) * 128)
            oh = ot[:, sl]
            r = lax.rsqrt(jnp.mean(oh * oh, axis=-1, keepdims=True) + EPS)
            ohat = oh * r
            t = dy[:, sl] * sz[:, sl]
            dw = dw + _colsum(t * ohat)
            t = t * w
            dos.append(r * (t - ohat * jnp.mean(t * ohat, axis=-1, keepdims=True)))
            nrm.append(ohat * w)
        dz = dy * jnp.concatenate(nrm, axis=1) * (sg * (1.0 + zt * (1.0 - sg)))
        return jnp.concatenate(dos, axis=1), dz, dw

    rows = [(o, 0, DN_W, "cur"), (hin, 2304, DN_W, "cur")] + [(dycat, 256 * (1 + j), 256, "cur") for j in range(3)]
    return rowwise(fn, name="dn_post_bwd", L=L, tm=_pick(L, 256, SUBLANE), rows=rows,
                   consts=[onorm], outs=[(DN_W, f32), (DN_W, f32)], sums=[(1, 128)])


def conv_bwd_win(xarr, start, C, w, dc, name):
    L = xarr.shape[0]
    W = w.shape[0]

    def fn(i, nb, xt, dct, p8, n8, wv):
        dx = _conv_causal_bwd_x(dct, n8 * (i < nb - 1).astype(f32), wv)
        dw = _conv_causal_bwd_w(dct, xt, p8 * (i > 0).astype(f32), W)
        return dx, dw

    return rowwise(fn, name=name, L=L, tm=_pick(L, 128, SUBLANE),
                   rows=[(xarr, start, C, "cur"), (dc, 0, C, "cur"), (xarr, start, C, "prev"), (dc, 0, C, "next")],
                   consts=[w], outs=[(C, bf16)], sums=[(SUBLANE, C)])


def rec_assemble(dx_qkv, dz, du1, du2, dab):
    L = dz.shape[0]

    def fn(i, nb, a, b, c, d, e):
        return jnp.concatenate([a.astype(f32), b, c + d, e], axis=1)

    return rowwise(fn, name="rec_assemble", L=L, tm=_pick(L, 256, SUBLANE),
                   rows=[(dx_qkv, 0, QKV_W, "cur"), (dz, 0, DN_W, "cur"), (du1, 0, 256, "cur"),
                         (du2, 0, 256, "cur"), (dab, 0, 256, "cur")], outs=[(REC_PAD, bf16)])


def deltanet_fwd(hin, prm, yc):
    qkvn, gb = dn_pre_fwd(hin, prm["conv"], prm["alog"], prm["dtb"])
    o, s_in = dn_chunks_fwd(qkvn, gb)
    ycat = dn_post_fwd(o, hin, yc, prm["onorm"])
    return ycat, (qkvn, gb, o, s_in)


def deltanet_bwd(hin, prm, saved, dycat):
    qkvn, gb, o, s_in = saved
    do, dz, donorm = dn_post_bwd(o, hin, prm["onorm"], dycat)
    dq, dk, dv, dgH, dbH = dn_chunks_bwd(qkvn, gb, s_in, do)
    dc, dab, dalog, ddtb = dn_pre_bwd(hin, prm["conv"], prm["alog"], prm["dtb"], (dq, dk, dv), dgH, dbH)
    dx_qkv, dconv = conv_bwd_win(hin, 0, QKV_W, prm["conv"], dc, "dn_conv_bwd")
    return dx_qkv, dz, dab, dict(conv=dconv[:DN_CONV], alog=dalog, dtb=ddtb, onorm=donorm)


AXES = ("x", "y", "c")


def _collective(x, axes, mode, name, nchunk=1):
    k = len(axes)
    P = 2 ** k
    shape = x.shape if mode == "gather" else x.shape[1:]
    rows = shape[0] // nchunk
    assert rows * nchunk == shape[0]

    def body(x_ref, out_ref, send_sems, recv_sems, local_sems):
        co = {a: lax.axis_index(a) for a in AXES}
        me = 0
        for a in axes:
            me = me * 2 + co[a]

        def src(j, q):
            s = x_ref if mode == "gather" else x_ref.at[j]
            return s.at[pl.ds(q * rows, rows)]

        def dst(j, q):
            return out_ref.at[j].at[pl.ds(q * rows, rows)]

        locals_ = [pltpu.make_async_copy(src(me, q), dst(me, q), local_sems.at[q]) for q in range(nchunk)]
        for cp in locals_:
            cp.start()
        sends = []
        for m in range(1, P):
            tco = dict(co)
            t = 0
            for i, a in enumerate(axes):
                if (m >> (k - 1 - i)) & 1:
                    tco[a] = 1 - co[a]
                t = t * 2 + tco[a]
            dev = tuple(tco[a] for a in AXES)
            for q in range(nchunk):
                s = (m - 1) * nchunk + q
                cp = pltpu.make_async_remote_copy(src_ref=src(t, q), dst_ref=dst(me, q), send_sem=send_sems.at[s],
                                                  recv_sem=recv_sems.at[s], device_id=dev, device_id_type=MESH)
                cp.start()
                sends.append((cp, t, q, s, dev))
        for cp, t, q, s, dev in sends:
            pltpu.make_async_remote_copy(src_ref=src(t, q), dst_ref=dst(t, q), send_sem=send_sems.at[s],
                                         recv_sem=recv_sems.at[s], device_id=dev, device_id_type=MESH).wait_recv()
        for cp, *_ in sends:
            cp.wait_send()
        for cp in locals_:
            cp.wait()

    ns = (P - 1) * nchunk
    return pl.pallas_call(
        body,
        in_specs=[pl.BlockSpec(memory_space=pl.ANY)],
        out_specs=pl.BlockSpec(memory_space=pl.ANY),
        out_shape=jax.ShapeDtypeStruct((P,) + tuple(shape), x.dtype),
        scratch_shapes=[pltpu.SemaphoreType.DMA((ns,)), pltpu.SemaphoreType.DMA((ns,)),
                        pltpu.SemaphoreType.DMA((nchunk,))],
        name=name,
    )(x)


def all_gather(x, axes, name, nchunk=1):
    return _collective(x, axes, "gather", name, nchunk)


def exchange(x, axes, name, nchunk=1):
    return _collective(x, axes, "exchange", name, nchunk)


def sum_slots(x, name, out_dtype=f32):
    P, R, C = x.shape
    tr = _pick(R, 256, 2 * SUBLANE)

    def body(x_ref, o_ref):
        acc = x_ref[0].astype(f32)
        for j in range(1, P):
            acc = acc + x_ref[j].astype(f32)
        o_ref[...] = acc.astype(o_ref.dtype)

    return pl.pallas_call(
        body, grid=(R // tr,), in_specs=[pl.BlockSpec((P, tr, C), lambda i: (0, i, 0))],
        out_specs=pl.BlockSpec((tr, C), lambda i: (i, 0)), out_shape=jax.ShapeDtypeStruct((R, C), out_dtype),
        compiler_params=_cp("parallel"), name=name,
    )(x)


def _pack(arrs, width, row_mult, dtype):
    flat = jnp.concatenate([a.astype(dtype).reshape(-1) for a in arrs])
    unit = width * row_mult
    n = -(-flat.shape[0] // unit) * unit
    return jnp.pad(flat, (0, n - flat.shape[0])).reshape(n // width, width)


def _unpack(flat, shapes):
    flat = flat.reshape(-1)
    out, off = [], 0
    for s in shapes:
        n = int(np.prod(s))
        out.append(flat[off:off + n].reshape(s))
        off += n
    return out


def ada_fwd(c_all, ada_w):
    def body(c_ref, w_ref, o_ref):
        cond = _silu(c_ref[...])
        for l in range(ada_w.shape[0]):
            o_ref[l] = _dot(cond, w_ref[l], precision=HI)

    return pl.pallas_call(body, out_shape=jax.ShapeDtypeStruct((ada_w.shape[0], c_all.shape[0], ada_w.shape[2]), f32),
                          compiler_params=pltpu.CompilerParams(vmem_limit_bytes=VMEM_LIMIT), name="ada_fwd")(c_all, ada_w)


def ada_bwd(c_all, dmod):
    def body(c_ref, d_ref, o_ref):
        cond = _silu(c_ref[...])
        for l in range(dmod.shape[0]):
            o_ref[l] = _dot(cond, d_ref[l], TN, precision=HI)

    return pl.pallas_call(body, out_shape=jax.ShapeDtypeStruct((dmod.shape[0], c_all.shape[1], dmod.shape[2]), f32),
                          compiler_params=pltpu.CompilerParams(vmem_limit_bytes=VMEM_LIMIT), name="ada_bwd")(c_all, dmod)


def loss_fwd_bwd(y, target):
    L, D = y.shape

    def fn(i, nb, yt, tt):
        e = yt - tt
        return e * (1.0 / D), jnp.sum(jnp.sum(e * e, axis=1, keepdims=True), axis=0, keepdims=True)

    return rowwise(fn, name="loss", L=L, tm=_pick(L, 512, SUBLANE), rows=[(y, 0, D, "cur"), (target, 0, D, "cur")],
                   outs=[(D, f32)], sums=[(1, 1)])


def adamw(w, g, m, v, name):
    R, C = w.shape

    def fn(i, nb, wt, gt, mt, vt):
        m2 = ADAM_B1 * mt + (1.0 - ADAM_B1) * gt
        v2 = ADAM_B2 * vt + (1.0 - ADAM_B2) * (gt * gt)
        m_hat = m2 / (1.0 - ADAM_B1 ** ADAM_STEP)
        v_hat = v2 / (1.0 - ADAM_B2 ** ADAM_STEP)
        delta = -ADAM_LR * (m_hat / (jnp.sqrt(v_hat) + ADAM_EPS) + ADAM_WD * wt)
        return delta, m2, v2

    return rowwise(fn, name=name, L=R, tm=_pick(R, 256, SUBLANE), rows=[(a, 0, C, "cur") for a in (w, g, m, v)],
                   outs=[(C, f32)] * 3)


W_NAMES = ["ada_w", "ada_b", "norm_mix", "norm_ffn", "attn_w_in", "attn_q_norm_a", "attn_k_norm_a", "attn_q_norm_b",
           "attn_k_norm_b", "attn_sinks", "attn_w_out", "rec_w_in", "s5_lambda_re", "s5_lambda_im", "s5_log_dt",
           "s5_b_re", "s5_b_im", "s5_c_re", "s5_c_im", "s5_d", "s5_glu_w", "s5_glu_b", "dn_conv", "dn_a_log",
           "dn_dt_bias", "dn_out_norm", "rec_w_out", "ffn_w_up", "ffn_conv", "ffn_w_down"]
BIG = ["attn_w_in", "attn_w_out", "rec_w_in", "rec_w_out", "ffn_w_up", "ffn_w_down"]
SMALL_SHARDED = ["s5_d", "s5_glu_w", "s5_glu_b", "dn_conv", "ffn_conv"]
SMALL_REPL = [n for n in W_NAMES if n not in BIG and n not in SMALL_SHARDED and n != "ada_w"]
NSH = 4


def _unshard(g, name):
    ax = {"attn_w_in": 2, "attn_w_out": 1, "rec_w_in": 2, "rec_w_out": 1, "ffn_w_up": 2, "ffn_w_down": 1,
          "s5_d": 1, "s5_glu_w": 1, "s5_glu_b": 1, "dn_conv": 2, "ffn_conv": 2}[name]
    g = jnp.moveaxis(g, 0, ax)
    s = g.shape
    return g.reshape(s[:ax] + (s[ax] * s[ax + 1],) + s[ax + 2:])


def _to_shards(full, name):
    ax = {"attn_w_in": 2, "attn_w_out": 1, "rec_w_in": 2, "rec_w_out": 1, "ffn_w_up": 2, "ffn_w_down": 1,
          "s5_d": 1, "s5_glu_w": 1, "s5_glu_b": 1, "dn_conv": 2, "ffn_conv": 2}[name]
    s = full.shape
    g = full.reshape(s[:ax] + (NSH, s[ax] // NSH) + s[ax + 1:])
    return jnp.moveaxis(g, ax, 0)


def _rec_pad_cols(w):
    z6 = jnp.zeros(w.shape[:-1] + (122,), w.dtype)
    return jnp.concatenate([w[..., 256:3328], w[..., 0:256], w[..., 3328:3334], z6, w[..., 3334:3340], z6], axis=-1)


def _rec_unpad_cols(g):
    return jnp.concatenate([g[..., 3072:3328], g[..., 0:3072], g[..., 3328:3334], g[..., 3456:3462]], axis=-1)


def _ffn_fwd(x1, nf, sc, sh, gate, w_up, conv, w_dn, tag):
    h2 = modulate_fwd(x1, nf, sc, sh, f"{tag}_mod2_fwd")
    up = mm(h2, w_up, name=f"{tag}_ffn_up")
    act = ffn_act_fwd(up, conv, f"{tag}_ffn_act_fwd")
    f, x2 = mm(act, w_dn, name=f"{tag}_ffn_down", out_dtypes=(f32, f32), epi=_resid_epi, epi_mn=[x1], epi_n=[gate])
    return x2, (h2, up, act, f)


def _ffn_bwd(dx, x1, nf, sc, sh, gate, w_up, conv, w_dn, saved, tag):
    h2, up, act, f = saved
    df, dgate = resid_bwd(dx, f, gate, f"{tag}_res2_bwd")
    dact = mm(df, w_dn, tb=True, name=f"{tag}_ffn_dact")
    dw_dn = mm(act, df, ta=True, name=f"{tag}_ffn_dwdown")
    dc = ffn_act_bwd1(up, conv, dact, f"{tag}_ffn_act_bwd")
    dup, dconv = conv_bwd(up, conv, dc, f"{tag}_ffn_conv_bwd")
    dw_up = mm(h2, dup, ta=True, name=f"{tag}_ffn_dwup")
    dh2 = mm(dup, w_up, tb=True, name=f"{tag}_ffn_dh")
    dx, dnf, dsc, dsh = modulate_bwd(x1, nf, sc, sh, dh2, dx, f"{tag}_mod2_bwd")
    return dx, dict(nf=dnf, sc=dsc, sh=dsh, gate=dgate, w_up=dw_up, conv=dconv[:FFN_CONV], w_dn=dw_dn)


def kernel(x, c, ada_w, ada_b, norm_mix, norm_ffn, attn_w_in, attn_q_norm_a, attn_k_norm_a, attn_q_norm_b, attn_k_norm_b, attn_sinks, attn_w_out, rec_w_in, s5_lambda_re, s5_lambda_im, s5_log_dt, s5_b_re, s5_b_im, s5_c_re, s5_c_im, s5_d, s5_glu_w, s5_glu_b, dn_conv, dn_a_log, dn_dt_bias, dn_out_norm, rec_w_out, ffn_w_up, ffn_conv, ffn_w_down, loss_target, m_ada_w, m_ada_b, m_norm_mix, m_norm_ffn, m_attn_w_in, m_attn_q_norm_a, m_attn_k_norm_a, m_attn_q_norm_b, m_attn_k_norm_b, m_attn_sinks, m_attn_w_out, m_rec_w_in, m_s5_lambda_re, m_s5_lambda_im, m_s5_log_dt, m_s5_b_re, m_s5_b_im, m_s5_c_re, m_s5_c_im, m_s5_d, m_s5_glu_w, m_s5_glu_b, m_dn_conv, m_dn_a_log, m_dn_dt_bias, m_dn_out_norm, m_rec_w_out, m_ffn_w_up, m_ffn_conv, m_ffn_w_down, v_ada_w, v_ada_b, v_norm_mix, v_norm_ffn, v_attn_w_in, v_attn_q_norm_a, v_attn_k_norm_a, v_attn_q_norm_b, v_attn_k_norm_b, v_attn_sinks, v_attn_w_out, v_rec_w_in, v_s5_lambda_re, v_s5_lambda_im, v_s5_log_dt, v_s5_b_re, v_s5_b_im, v_s5_c_re, v_s5_c_im, v_s5_d, v_s5_glu_w, v_s5_glu_b, v_dn_conv, v_dn_a_log, v_dn_dt_bias, v_dn_out_norm, v_rec_w_out, v_ffn_w_up, v_ffn_conv, v_ffn_w_down):
    args = (ada_w, ada_b, norm_mix, norm_ffn, attn_w_in, attn_q_norm_a, attn_k_norm_a, attn_q_norm_b, attn_k_norm_b, attn_sinks, attn_w_out, rec_w_in, s5_lambda_re, s5_lambda_im, s5_log_dt, s5_b_re, s5_b_im, s5_c_re, s5_c_im, s5_d, s5_glu_w, s5_glu_b, dn_conv, dn_a_log, dn_dt_bias, dn_out_norm, rec_w_out, ffn_w_up, ffn_conv, ffn_w_down)
    ms = (m_ada_w, m_ada_b, m_norm_mix, m_norm_ffn, m_attn_w_in, m_attn_q_norm_a, m_attn_k_norm_a, m_attn_q_norm_b, m_attn_k_norm_b, m_attn_sinks, m_attn_w_out, m_rec_w_in, m_s5_lambda_re, m_s5_lambda_im, m_s5_log_dt, m_s5_b_re, m_s5_b_im, m_s5_c_re, m_s5_c_im, m_s5_d, m_s5_glu_w, m_s5_glu_b, m_dn_conv, m_dn_a_log, m_dn_dt_bias, m_dn_out_norm, m_rec_w_out, m_ffn_w_up, m_ffn_conv, m_ffn_w_down)
    vs = (v_ada_w, v_ada_b, v_norm_mix, v_norm_ffn, v_attn_w_in, v_attn_q_norm_a, v_attn_k_norm_a, v_attn_q_norm_b, v_attn_k_norm_b, v_attn_sinks, v_attn_w_out, v_rec_w_in, v_s5_lambda_re, v_s5_lambda_im, v_s5_log_dt, v_s5_b_re, v_s5_b_im, v_s5_c_re, v_s5_c_im, v_s5_d, v_s5_glu_w, v_s5_glu_b, v_dn_conv, v_dn_a_log, v_dn_dt_bias, v_dn_out_norm, v_rec_w_out, v_ffn_w_up, v_ffn_conv, v_ffn_w_down)
    W = dict(zip(W_NAMES, args))
    Mo = dict(zip(W_NAMES, ms))
    Vo = dict(zip(W_NAMES, vs))
    xi, yi, ci = lax.axis_index("x"), lax.axis_index("y"), lax.axis_index("c")
    shard = 2 * xi + yi
    me8 = 4 * xi + 2 * yi + ci
    xs = x[0]
    target = loss_target[0]
    L, D = xs.shape

    wflat = _pack([W[n] for n in BIG], 1024, 16, bf16)
    wfull = all_gather(wflat, ("x", "y"), "gather_w").reshape(NSH, -1)
    Wf = {}
    off = 0
    for n in BIG:
        sz = int(np.prod(W[n].shape))
        Wf[n] = _unshard(wfull[:, off:off + sz].reshape((NSH,) + W[n].shape), n)
        off += sz
    rec_w_in_p = _rec_pad_cols(Wf["rec_w_in"][0])

    sflat = _pack([c] + [W[n] for n in SMALL_SHARDED], 1024, 8, f32)
    s8 = all_gather(sflat, AXES, "gather_small")
    s8f = s8.reshape(8, -1)
    c_all = s8f[:, :D]
    Ws = {}
    off = D
    for n in SMALL_SHARDED:
        sz = int(np.prod(W[n].shape))
        Ws[n] = _unshard(s8f[0::2, off:off + sz].reshape((NSH,) + W[n].shape), n)
        off += sz

    modp = ada_fwd(c_all, ada_w)
    modg = all_gather(modp, ("x", "y"), "gather_mod")
    mod_all = jnp.moveaxis(modg, 0, 2).reshape(2, 8, -1) + ada_b[:, None, :]
    mod = lax.dynamic_slice(mod_all, (0, me8, 0), (2, 1, mod_all.shape[2]))[:, 0, :]
    mods = [[mod[l:l + 1, j * D:(j + 1) * D] for j in range(6)] for l in range(2)]

    sh1, sc1, g1, sh2, sc2, g2_ = mods[0]
    nm0, nf0 = norm_mix[0:1], norm_ffn[0:1]
    sinkb = jnp.repeat(attn_sinks[0], HEAD_DIM)[None]
    h0 = modulate_fwd(xs, nm0, sc1, sh1, "l0_mod1_fwd")
    hin0 = mm(h0, Wf["attn_w_in"][0], name="l0_in_proj")
    ocat, att_saved = attention_fwd(hin0, attn_q_norm_a, attn_k_norm_a, attn_q_norm_b, attn_k_norm_b, sinkb)
    y0, x1 = mm(ocat, Wf["attn_w_out"][0], name="l0_out_proj", out_dtypes=(f32, f32), epi=_resid_epi,
                epi_mn=[xs], epi_n=[g1])
    x2, ffn0_saved = _ffn_fwd(x1, nf0, sc2, sh2, g2_, Wf["ffn_w_up"][0], Ws["ffn_conv"][0], Wf["ffn_w_down"][0], "l0")

    th1, tc1, t1, th2, tc2, t2 = mods[1]
    nm1, nf1 = norm_mix[1:2], norm_ffn[1:2]
    pad128 = lambda a: jnp.pad(a, ((0, 0), (0, 128 - a.shape[1])))
    s5p = dict(lr=s5_lambda_re[0], li=s5_lambda_im[0], ldt=s5_log_dt[0][:, None], b_re=s5_b_re[0], b_im=s5_b_im[0],
               c_re=s5_c_re[0], c_im=s5_c_im[0], d=Ws["s5_d"], gw=Ws["s5_glu_w"][0], gb=Ws["s5_glu_b"])
    dnp = dict(conv=Ws["dn_conv"][0], alog=pad128(dn_a_log), dtb=pad128(dn_dt_bias), onorm=dn_out_norm)
    h1 = modulate_fwd(x2, nm1, tc1, th1, "l1_mod1_fwd")
    hin1 = mm(h1, rec_w_in_p, name="l1_in_proj")
    yc, s5_saved = s5_fwd(hin1, s5p)
    ycat, dn_saved = deltanet_fwd(hin1, dnp, yc)
    y1, x3 = mm(ycat, Wf["rec_w_out"][0], name="l1_out_proj", out_dtypes=(f32, f32), epi=_resid_epi,
                epi_mn=[x2], epi_n=[t1])
    x4, ffn1_saved = _ffn_fwd(x3, nf1, tc2, th2, t2, Wf["ffn_w_up"][1], Ws["ffn_conv"][1], Wf["ffn_w_down"][1], "l1")

    dx, sse = loss_fwd_bwd(x4, target)
    loss = lax.psum(0.5 * sse[0, 0] / D, AXES)

    dx, gf1 = _ffn_bwd(dx, x3, nf1, tc2, th2, t2, Wf["ffn_w_up"][1], Ws["ffn_conv"][1], Wf["ffn_w_down"][1], ffn1_saved, "l1")
    dy1, dt1 = resid_bwd(dx, y1, t1, "l1_res1_bwd")
    dycat = mm(dy1, Wf["rec_w_out"][0], tb=True, name="l1_dycat")
    dw_rec_out = mm(ycat, dy1, ta=True, name="l1_dwout")
    du_skip, du_b, s5g = s5_bwd(hin1, s5p, s5_saved, dycat)
    dx_qkv, dz, dab, dng = deltanet_bwd(hin1, dnp, dn_saved, dycat)
    dhin1 = rec_assemble(dx_qkv, dz, du_skip, du_b, dab)
    dw_rec_in = _rec_unpad_cols(mm(h1, dhin1, ta=True, name="l1_dwin"))
    dh1 = mm(dhin1, rec_w_in_p, tb=True, name="l1_dh")
    dx, dnm1, dtc1, dth1 = modulate_bwd(x2, nm1, tc1, th1, dh1, dx, "l1_mod1_bwd")

    dx, gf0 = _ffn_bwd(dx, x1, nf0, sc2, sh2, g2_, Wf["ffn_w_up"][0], Ws["ffn_conv"][0], Wf["ffn_w_down"][0], ffn0_saved, "l0")
    dy0, dg1 = resid_bwd(dx, y0, g1, "l0_res1_bwd")
    dcat = mm(dy0, Wf["attn_w_out"][0], tb=True, name="l0_dcat")
    dw_attn_out = mm(ocat, dy0, ta=True, name="l0_dwout")
    dhin0, dwqa, dwka, dwqb, dwkb, dsinkb = attention_bwd(hin0, attn_q_norm_a, attn_k_norm_a, attn_q_norm_b,
                                                          attn_k_norm_b, sinkb, att_saved, dcat)
    dw_attn_in = mm(h0, dhin0, ta=True, name="l0_dwin")
    dh0 = mm(dhin0, Wf["attn_w_in"][0], tb=True, name="l0_dh")
    grad_x, dnm0, dsc1, dsh1 = modulate_bwd(xs, nm0, sc1, sh1, dh0, dx, "l0_mod1_bwd")

    dmod = jnp.concatenate([
        jnp.concatenate([dsh1, dsc1, dg1, gf0["sh"], gf0["sc"], gf0["gate"]], axis=1),
        jnp.concatenate([dth1, dtc1, dt1, gf1["sh"], gf1["sc"], gf1["gate"]], axis=1)], axis=0)
    gl = {
        "ada_b": dmod,
        "norm_mix": jnp.concatenate([dnm0, dnm1], axis=0),
        "norm_ffn": jnp.concatenate([gf0["nf"], gf1["nf"]], axis=0),
        "attn_q_norm_a": dwqa, "attn_k_norm_a": dwka, "attn_q_norm_b": dwqb, "attn_k_norm_b": dwkb,
        "attn_sinks": dsinkb[:, ::HEAD_DIM],
        "s5_lambda_re": s5g["lr"][None], "s5_lambda_im": s5g["li"][None], "s5_log_dt": s5g["ldt"][:, 0][None],
        "s5_b_re": s5g["b_re"][None], "s5_b_im": s5g["b_im"][None], "s5_c_re": s5g["c_re"][None],
        "s5_c_im": s5g["c_im"][None],
        "dn_a_log": dng["alog"][:, :DN_HEADS], "dn_dt_bias": dng["dtb"][:, :DN_HEADS], "dn_out_norm": dng["onorm"],
        "s5_d": s5g["d"], "s5_glu_w": s5g["gw"][None], "s5_glu_b": s5g["gb"], "dn_conv": dng["conv"][None],
        "ffn_conv": jnp.stack([gf0["conv"], gf1["conv"]]),
        "attn_w_in": dw_attn_in[None], "attn_w_out": dw_attn_out[None], "rec_w_in": dw_rec_in[None],
        "rec_w_out": dw_rec_out[None], "ffn_w_up": jnp.stack([gf0["w_up"], gf1["w_up"]]),
        "ffn_w_down": jnp.stack([gf0["w_dn"], gf1["w_dn"]]),
    }

    small_names = SMALL_REPL + SMALL_SHARDED
    gs = _pack([gl[n] for n in small_names], 128, 8, f32)
    gs8 = all_gather(gs, AXES, "gather_small_grads")
    gsum = sum_slots(gs8, "sum_small_grads")
    full_shapes = [gl[n].shape for n in small_names]
    gfull = dict(zip(small_names, _unpack(gsum, full_shapes)))
    dmod_all = gs8.reshape(8, -1)[:, :2 * 6 * D].reshape(8, 2, 6 * D)
    ncol = ada_w.shape[2]
    dmod_sh = jnp.moveaxis(lax.dynamic_slice(dmod_all, (0, 0, shard * ncol), (8, 2, ncol)), 0, 1)
    grads = {"ada_w": ada_bwd(c_all, dmod_sh)}
    for n in SMALL_REPL:
        grads[n] = gfull[n]
    for n in SMALL_SHARDED:
        sh_all = _to_shards(gfull[n], n)
        grads[n] = lax.dynamic_slice(sh_all, (shard,) + (0,) * (sh_all.ndim - 1), (1,) + sh_all.shape[1:])[0]

    gflat = jnp.concatenate([_to_shards(gl[n], n).reshape(NSH, -1) for n in BIG], axis=1)
    nel = gflat.shape[1]
    unit = 16 * 1024
    npad = -(-nel // unit) * unit
    gflat = jnp.pad(gflat, ((0, 0), (0, npad - nel))).astype(bf16).reshape(NSH, npad // 1024, 1024)
    gq = exchange(gflat, ("x", "y"), "reduce_xy")
    gpart = sum_slots(gq, "sum_chips", bf16)
    gc = all_gather(gpart, ("c",), "gather_grad_c")
    gsh = sum_slots(gc, "sum_pair").reshape(-1)
    off = 0
    for n in BIG:
        sz = int(np.prod(W[n].shape))
        grads[n] = gsh[off:off + sz].reshape(W[n].shape)
        off += sz

    delta, new_m, new_v = {}, {}, {}

    def as2d(a):
        return a.reshape(-1, a.shape[-1])

    for n in ["ada_w"] + BIG:
        d_, m_, v_ = adamw(as2d(W[n]), as2d(grads[n]), as2d(Mo[n]), as2d(Vo[n]), f"adamw_{n}")
        delta[n], new_m[n], new_v[n] = d_.reshape(W[n].shape), m_.reshape(W[n].shape), v_.reshape(W[n].shape)
    pk = lambda dd: _pack([dd[n] for n in small_names], 128, 8, f32)
    d_, m_, v_ = adamw(pk(W), pk(grads), pk(Mo), pk(Vo), "adamw_small")
    shp = [W[n].shape for n in small_names]
    for dst, src in ((delta, d_), (new_m, m_), (new_v, v_)):
        dst.update(zip(small_names, _unpack(src, shp)))

    return (loss, grad_x[None], *[grads[n] for n in W_NAMES], *[delta[n] for n in W_NAMES],
            *[new_m[n] for n in W_NAMES], *[new_v[n] for n in W_NAMES])
```

```python
import functools
import math

import numpy as np
import jax
import jax.numpy as jnp
from jax import lax
from jax.experimental import pallas as pl
from jax.experimental.pallas import tpu as pltpu

f32 = jnp.float32
bf16 = jnp.bfloat16
HI = lax.Precision.HIGHEST
MESH = pl.DeviceIdType.MESH

HEAD_DIM = 64
BLOCK = 128
A_Q_HEADS = 8
A_KV_HEADS = 2
A_WINDOW = 128
B_HEADS = 8
B_BRANCHES = ((128, 1), (512, 4), (2048, 16))
N_ATTN_HEADS = 16
ATTN_IN = 2304
S5_GROUP = 16
S5_GROUPS = 16
S5_WIDTH = 256
S5_STATE = 64
DN_HEADS = 6
DN_DK = 128
DN_CONV = 4
DN_CHUNK = 64
REC_IN = 3340
REC_PAD = 3584
FFN_CONV = 3
EPS = 1e-6
ADAM_LR = 0.001
ADAM_B1 = 0.9
ADAM_B2 = 0.999
ADAM_EPS = 1e-08
ADAM_WD = 0.01
ADAM_STEP = 10

LANE = 128
SUBLANE = 8
VMEM_LIMIT = 52 * 1024 * 1024
MM_FULL_K = 5632
MM_VMEM_BUDGET = 40 * 1024 * 1024


def _cp(*sem):
    return pltpu.CompilerParams(dimension_semantics=sem, vmem_limit_bytes=VMEM_LIMIT)


def _pick(dim, cap, unit=LANE):
    for t in (2048, 1024, 768, 512, 384, 256, 128, 64, 32, 16, 8):
        if t <= cap and t % unit == 0 and dim % t == 0:
            return t
    return dim


def _dot(a, b, dims=(((1,), (0,)), ((), ())), precision=None):
    return lax.dot_general(a, b, dims, precision=precision, preferred_element_type=f32)


NN = (((1,), (0,)), ((), ()))
NT = (((1,), (1,)), ((), ()))
TN = (((0,), (0,)), ((), ()))


def mm(a, b, *, name, ta=False, tb=False, a_win=None, b_win=None, out_dtypes=(f32,),
       epi=None, epi_mn=(), epi_n=(), tm_cap=1024, tn_cap=8192, tk_cap=None):
    a0, a1 = a.shape
    b0, b1 = b.shape
    aw = a_win or (0, a1)
    bw = b_win or (0, b1)
    if ta:
        K, M = a0, aw[1]
    else:
        M, K = a0, aw[1]
    if tb:
        N, K2 = b0, bw[1]
    else:
        K2, N = b0, bw[1]
    assert K == K2, (a.shape, b.shape, ta, tb, a_win, b_win)
    if tk_cap is None:
        tk_cap = K if K <= MM_FULL_K else 2048
    tk = _pick(K, tk_cap, SUBLANE if (ta and not tb) else LANE)
    nk = K // tk
    sa, sb = a.dtype.itemsize, b.dtype.itemsize
    so = sum(jnp.dtype(d).itemsize for d in out_dtypes)
    n_mn, n_n, n_out = len(epi_mn), len(epi_n), len(out_dtypes)

    def vmem(tm_, tn_):
        return 2 * (tm_ * tk * sa + tk * tn_ * sb + tm_ * tn_ * (so + 4 * n_mn)) + 2 * tm_ * tn_ * 4

    best = None
    for tm_ in (t for t in (1024, 512, 256, 128) if M % t == 0 and (not ta or aw[0] % t == 0)):
        for tn_ in (t for t in (N, N // 2, 1024, 768, 512, 384, 256, 128)
                    if t % LANE == 0 and N % t == 0 and (tb or bw[0] % t == 0)):
            if tm_ <= tm_cap and tn_ <= max(tn_cap, 0) and vmem(tm_, tn_) <= MM_VMEM_BUDGET:
                if best is None or (tm_ * tn_, tn_) > (best[0] * best[1], best[1]):
                    best = (tm_, tn_)
    assert best is not None, (name, M, N, K)
    tm, tn = best
    b_outer = tk * tn * sb > tm * tk * sa

    def ix(f):
        if b_outer:
            return lambda j, i, k: f(i, j, k)
        return f

    if ta:
        mo = aw[0] // tm
        a_spec = pl.BlockSpec((tk, tm), ix(lambda i, j, k: (k, i + mo)))
    else:
        assert aw[0] % tk == 0
        ko = aw[0] // tk
        a_spec = pl.BlockSpec((tm, tk), ix(lambda i, j, k: (i, k + ko)))
    if tb:
        assert bw[0] % tk == 0
        kob = bw[0] // tk
        b_spec = pl.BlockSpec((tn, tk), ix(lambda i, j, k: (j, k + kob)))
    else:
        no = bw[0] // tn
        b_spec = pl.BlockSpec((tk, tn), ix(lambda i, j, k: (k, j + no)))
    dims = (((0 if ta else 1,), (1 if tb else 0,)), ((), ()))

    def body(a_ref, b_ref, *rest):
        mn_refs = rest[:n_mn]
        n_refs = rest[n_mn:n_mn + n_n]
        out_refs = rest[n_mn + n_n:n_mn + n_n + n_out]

        def finish(r):
            if epi is None:
                outs = (r,)
            else:
                outs = epi(r, *[m[...] for m in mn_refs], *[v[...] for v in n_refs])
            for o_ref, o in zip(out_refs, outs):
                o_ref[...] = o.astype(o_ref.dtype)

        part = _dot(a_ref[...].astype(bf16), b_ref[...].astype(bf16), dims)
        if nk == 1:
            finish(part)
            return
        acc = rest[-1]
        k = pl.program_id(2)

        @pl.when(k == 0)
        def _():
            acc[...] = part

        @pl.when(k > 0)
        def _():
            acc[...] += part

        @pl.when(k == nk - 1)
        def _():
            finish(acc[...])

    mn_spec = pl.BlockSpec((tm, tn), ix(lambda i, j, k: (i, j)))
    n_spec = pl.BlockSpec((1, tn), ix(lambda i, j, k: (0, j)))
    gi, gj = M // tm, N // tn
    outs = pl.pallas_call(
        body,
        grid=(gj, gi, nk) if b_outer else (gi, gj, nk),
        in_specs=[a_spec, b_spec] + [mn_spec] * n_mn + [n_spec] * n_n,
        out_specs=[mn_spec] * n_out,
        out_shape=[jax.ShapeDtypeStruct((M, N), d) for d in out_dtypes],
        scratch_shapes=[pltpu.VMEM((tm, tn), f32)] if nk > 1 else [],
        compiler_params=_cp("parallel", "parallel", "arbitrary"),
        name=name,
    )(a, b, *epi_mn, *epi_n)
    return outs[0] if n_out == 1 else tuple(outs)


def rowwise(fn, *, name, L, tm, rows=(), consts=(), outs=(), sums=()):
    nb = L // tm
    hb = tm // SUBLANE
    in_specs = []
    arrs = []
    for arr, start, width, kind in rows:
        assert start % width == 0, (name, start, width)
        co = start // width
        if kind == "cur":
            in_specs.append(pl.BlockSpec((tm, width), lambda i, co=co: (i, co)))
        elif kind == "prev":
            in_specs.append(pl.BlockSpec((SUBLANE, width), lambda i, co=co: (jnp.maximum(i * hb - 1, 0), co)))
        else:
            last = L // SUBLANE - 1
            in_specs.append(pl.BlockSpec((SUBLANE, width), lambda i, co=co, last=last: (jnp.minimum((i + 1) * hb, last), co)))
        arrs.append(arr)
    for cst in consts:
        assert cst.ndim == 2
        in_specs.append(pl.BlockSpec(cst.shape, lambda i: (0, 0)))
        arrs.append(cst)
    n_rows, n_c, n_o, n_s = len(rows), len(consts), len(outs), len(sums)
    out_specs = [pl.BlockSpec((tm, w), lambda i: (i, 0)) for w, _ in outs]
    out_specs += [pl.BlockSpec(s, lambda i: (0, 0)) for s in sums]
    out_shape = [jax.ShapeDtypeStruct((L, w), d) for w, d in outs]
    out_shape += [jax.ShapeDtypeStruct(s, f32) for s in sums]

    def body(*refs):
        i = pl.program_id(0)
        vals = [r[...] for r in refs[:n_rows + n_c]]
        res = fn(i, nb, *vals)
        if not isinstance(res, (tuple, list)):
            res = (res,)
        o_refs = refs[n_rows + n_c:n_rows + n_c + n_o]
        s_refs = refs[n_rows + n_c + n_o:]
        for o_ref, o in zip(o_refs, res[:n_o]):
            o_ref[...] = o.astype(o_ref.dtype)
        if n_s:
            @pl.when(i == 0)
            def _():
                for s_ref in s_refs:
                    s_ref[...] = jnp.zeros_like(s_ref)

            for s_ref, s in zip(s_refs, res[n_o:]):
                s_ref[...] += s

    res = pl.pallas_call(
        body,
        grid=(nb,),
        in_specs=in_specs,
        out_specs=out_specs,
        out_shape=out_shape,
        compiler_params=_cp("arbitrary" if n_s else "parallel"),
        name=name,
    )(*arrs)
    return res[0] if len(res) == 1 else tuple(res)


def _shift_down(x, prev8, k):
    cat = jnp.concatenate([prev8, x], axis=0)
    return pltpu.roll(cat, k, 0)[SUBLANE:, :]


def _shift_up(x, next8, k):
    n = x.shape[0]
    cat = jnp.concatenate([x, next8], axis=0)
    return pltpu.roll(cat, n + SUBLANE - k, 0)[:n, :]


def _colsum(x):
    return jnp.sum(x, axis=0, keepdims=True)


def _silu(x):
    return x * jax.nn.sigmoid(x)


def _modulate_fn(x, nw, sc, sh):
    r = lax.rsqrt(jnp.mean(x * x, axis=-1, keepdims=True) + EPS)
    return (x * r * nw) * (1.0 + sc) + sh


def modulate_fwd(x, nw, sc, sh, name):
    L, D = x.shape

    def fn(i, nb, xt, nwv, scv, shv):
        return _modulate_fn(xt, nwv, scv, shv)

    return rowwise(fn, name=name, L=L, tm=_pick(L, 512, SUBLANE), rows=[(x, 0, D, "cur")],
                   consts=[nw, sc, sh], outs=[(D, bf16)])


def modulate_bwd(x, nw, sc, sh, dh, dx_in, name):
    L, D = x.shape

    def fn(i, nb, xt, dht, dxt, nwv, scv, shv):
        _, vjp = jax.vjp(_modulate_fn, xt, nwv, scv, shv)
        dx, dnw, dsc, dsh = vjp(dht)
        return dxt + dx, dnw, dsc, dsh

    return rowwise(fn, name=name, L=L, tm=_pick(L, 256, SUBLANE),
                   rows=[(x, 0, D, "cur"), (dh, 0, D, "cur"), (dx_in, 0, D, "cur")],
                   consts=[nw, sc, sh], outs=[(D, f32)], sums=[(1, D)] * 3)


def resid_bwd(dx, y, g, name):
    L, D = dx.shape

    def fn(i, nb, dxt, yt, gv):
        return dxt * gv, _colsum(dxt * yt)

    return rowwise(fn, name=name, L=L, tm=_pick(L, 512, SUBLANE),
                   rows=[(dx, 0, D, "cur"), (y, 0, D, "cur")], consts=[g],
                   outs=[(D, bf16)], sums=[(1, D)])


def _resid_epi(acc, xt, gv):
    return acc, xt + gv * acc


def _stack_rows(rows, n=SUBLANE):
    c = rows[0].shape[1]
    ridx = lax.broadcasted_iota(jnp.int32, (n, c), 0)
    out = jnp.zeros((n, c), f32)
    for j, r in enumerate(rows):
        out = out + jnp.where(ridx == j, r, 0.0)
    return out


def _conv_causal(x, prev8, w):
    W = w.shape[0]
    y = x * w[W - 1:W, :]
    for j in range(W - 1):
        y = y + _shift_down(x, prev8, W - 1 - j) * w[j:j + 1, :]
    return y


def _conv_causal_bwd_x(dy, next8, w):
    W = w.shape[0]
    dx = dy * w[W - 1:W, :]
    for j in range(W - 1):
        dx = dx + _shift_up(dy, next8, W - 1 - j) * w[j:j + 1, :]
    return dx


def _conv_causal_bwd_w(dy, x, prev8, W):
    rows = [_colsum(dy * _shift_down(x, prev8, W - 1 - j)) for j in range(W - 1)]
    rows.append(_colsum(dy * x))
    return _stack_rows(rows)


def ffn_act_fwd(up, conv_w, name):
    L, F2 = up.shape
    F = F2 // 2

    def fn(i, nb, u, p8, w):
        c = _conv_causal(u, p8 * (i > 0).astype(f32), w)
        return _silu(c[:, :F]) * c[:, F:]

    return rowwise(fn, name=name, L=L, tm=_pick(L, 128, SUBLANE),
                   rows=[(up, 0, F2, "cur"), (up, 0, F2, "prev")], consts=[conv_w], outs=[(F, bf16)])


def ffn_act_conv_bwd(up, conv_w, dact, name):
    L, F2 = up.shape
    F = F2 // 2
    W = conv_w.shape[0]

    def fn(i, nb, u, da, p8, un8, dan8, w):
        tm = u.shape[0]
        more = (i < nb - 1).astype(f32)
        p8 = p8 * (i > 0).astype(f32)
        c = _conv_causal(jnp.concatenate([u, un8 * more], axis=0), p8, w)
        dae = jnp.concatenate([da, dan8 * more], axis=0)
        a, b = c[:, :F], c[:, F:]
        sg = jax.nn.sigmoid(a)
        dc = jnp.concatenate([dae * b * (sg * (1.0 + a * (1.0 - sg))), dae * a * sg], axis=1)
        dx = dc[:tm] * w[W - 1:W, :]
        for j in range(W - 1):
            dx = dx + pltpu.roll(dc, tm + SUBLANE - (W - 1 - j), 0)[:tm] * w[j:j + 1, :]
        return dx, _conv_causal_bwd_w(dc[:tm], u, p8, W)

    return rowwise(fn, name=name, L=L, tm=_pick(L, 128, SUBLANE),
                   rows=[(up, 0, F2, "cur"), (dact, 0, F, "cur"), (up, 0, F2, "prev"), (up, 0, F2, "next"),
                         (dact, 0, F, "next")],
                   consts=[conv_w], outs=[(F2, bf16)], sums=[(SUBLANE, F2)])


ALIBI = [2.0 ** (-8.0 * (i + 1) / N_ATTN_HEADS) for i in range(N_ATTN_HEADS)]
NEG = -1e30


class _Band:
    def __init__(self, dilation, group_a):
        d = dilation
        self.d = d
        self.group_a = group_a
        if group_a:
            self.P, self.qw, self.hps, self.kvw = 1, 512, 8, 128
            self.qcol = lambda p: 0
            self.kcol = lambda p: 4
            self.vcol = lambda p: 5
            self.kv_of = lambda j: j // 4
            self.max_dist = A_WINDOW - 1
            sl = np.repeat(np.asarray(ALIBI[:8], np.float32), HEAD_DIM)[None, None, :]
        else:
            self.P, self.qw, self.hps, self.kvw = 2 * d, 256, 4, 256
            self.qcol = lambda p: lax.div(p, 2) * 9 + 3 + lax.rem(p, 2)
            self.kcol = lambda p: lax.div(p, 2) * 9 + 5 + lax.rem(p, 2)
            self.vcol = lambda p: lax.div(p, 2) * 9 + 7 + lax.rem(p, 2)
            self.kv_of = lambda j: j
            self.max_dist = BLOCK
            per = np.repeat(np.asarray(ALIBI[8:], np.float32), HEAD_DIM).reshape(2, 1, 256)
            sl = np.tile(per, (d, 1, 1))
        self.slopes = jnp.asarray(sl, f32)


def _band_mask(n, d, max_dist):
    qi = lax.broadcasted_iota(jnp.int32, (BLOCK, 2 * BLOCK), 0)
    kj = lax.broadcasted_iota(jnp.int32, (BLOCK, 2 * BLOCK), 1)
    dist = BLOCK + qi - kj
    valid = (dist >= 0) & (dist <= max_dist) & ((n > 0) | (kj >= BLOCK))
    return valid, -(d * dist).astype(f32)


def _rms64(x, w):
    r = lax.rsqrt(jnp.mean(x * x, axis=-1, keepdims=True) + EPS)
    xh = x * r
    return xh * w, xh, r


def _rms64_bwd(dy, xh, r, w):
    t = dy * w
    dw = jnp.sum(jnp.sum(dy * xh, axis=0), axis=0, keepdims=True)
    return r * (t - xh * jnp.mean(t * xh, axis=-1, keepdims=True)), dw


def _heads64(x, heads):
    return jnp.stack([x[:, h * 64:(h + 1) * 64] for h in heads])


def attn_fwd(hv, band, wq, wk, name):
    M = hv.shape[0]
    nb = M // BLOCK
    P, qw, hps = band.P, band.qw, band.hps
    d, max_dist, kv_of = band.d, band.max_dist, band.kv_of
    kv_heads = sorted({kv_of(j) for j in range(hps)})
    kv_pos = {h: i for i, h in enumerate(kv_heads)}
    gqa = len(kv_heads) != hps

    def body(q_ref, kp_ref, kc_ref, vp_ref, vc_ref, sl_ref, wq_ref, wk_ref, o_ref, lse_ref):
        n = pl.program_id(1)
        valid, negd = _band_mask(n, d, max_dist)
        kblk = jnp.concatenate([kp_ref[...], kc_ref[...]], axis=0)
        vblk = jnp.concatenate([vp_ref[...], vc_ref[...]], axis=0)
        q = _heads64(q_ref, range(hps))
        kn = _rms64(_heads64(kblk, kv_heads), wk_ref[...])[0].astype(bf16)
        v = _heads64(vblk, kv_heads).astype(bf16)
        kn_q = jnp.stack([kn[kv_pos[kv_of(j)]] for j in range(hps)]) if gqa else kn
        v_q = jnp.stack([v[kv_pos[kv_of(j)]] for j in range(hps)]) if gqa else v
        qn = _rms64(q, wq_ref[...])[0].astype(bf16)
        slope = jnp.stack([sl_ref[0, :, j * 64:j * 64 + 1] for j in range(hps)])
        s = _dot(qn, kn_q, BNT) * (HEAD_DIM ** -0.5) + slope * negd
        s = jnp.where(valid, s, NEG)
        m = jnp.max(s, axis=-1, keepdims=True)
        p = jnp.exp(s - m)
        l = jnp.sum(p, axis=-1, keepdims=True)
        o = _dot(p.astype(bf16), v_q, BNN) / l
        lse = m + jnp.log(l)
        for j in range(hps):
            o_ref[:, j * 64:(j + 1) * 64] = o[j]
            lse_ref[:, j * 64:(j + 1) * 64] = jnp.broadcast_to(lse[j], (BLOCK, 64))

    qcol, kcol, vcol, kvw = band.qcol, band.kcol, band.vcol, band.kvw
    in_specs = [
        pl.BlockSpec((BLOCK, qw), lambda p, n: (n, qcol(p))),
        pl.BlockSpec((BLOCK, kvw), lambda p, n: (jnp.maximum(n - 1, 0), kcol(p))),
        pl.BlockSpec((BLOCK, kvw), lambda p, n: (n, kcol(p))),
        pl.BlockSpec((BLOCK, kvw), lambda p, n: (jnp.maximum(n - 1, 0), vcol(p))),
        pl.BlockSpec((BLOCK, kvw), lambda p, n: (n, vcol(p))),
        pl.BlockSpec((1, 1, qw), lambda p, n: (p, 0, 0)),
        pl.BlockSpec((1, 64), lambda p, n: (0, 0)),
        pl.BlockSpec((1, 64), lambda p, n: (0, 0)),
    ]
    o_spec = pl.BlockSpec((BLOCK, qw), lambda p, n: (n, p))
    return pl.pallas_call(
        body, grid=(P, nb), in_specs=in_specs, out_specs=[o_spec, o_spec],
        out_shape=[jax.ShapeDtypeStruct((M, P * qw), f32)] * 2,
        compiler_params=_cp("parallel", "parallel"), name=name,
    )(hv, hv, hv, hv, hv, band.slopes, wq, wk)


def attn_bwd(hv, band, wq, wk, o, lse, do, dlse, dw0, name):
    M = hv.shape[0]
    nb = M // BLOCK
    P, qw, hps = band.P, band.qw, band.hps
    d, max_dist, kv_of = band.d, band.max_dist, band.kv_of
    kv_heads = sorted({kv_of(j) for j in range(hps)})
    kv_pos = {h: i for i, h in enumerate(kv_heads)}
    gqa = len(kv_heads) != hps

    def body(q_ref, kp_ref, kc_ref, vp_ref, vc_ref, sl_ref, wq_ref, wk_ref, o_ref, lse_ref, do_ref, dlse_ref,
             dwq0_ref, dwk0_ref, dq_ref, dk_ref, dv_ref, dwq_ref, dwk_ref, ck, cv):
        pp = pl.program_id(0)
        n = pl.program_id(1)

        @pl.when((pp == 0) & (n == 0))
        def _():
            dwq_ref[...] = dwq0_ref[...]
            dwk_ref[...] = dwk0_ref[...]

        @pl.when(n == 0)
        def _():
            ck[...] = jnp.zeros_like(ck)
            cv[...] = jnp.zeros_like(cv)

        @pl.when(n < nb)
        def _():
            valid, negd = _band_mask(n, d, max_dist)
            kblk = jnp.concatenate([kp_ref[...], kc_ref[...]], axis=0)
            vblk = jnp.concatenate([vp_ref[...], vc_ref[...]], axis=0)
            wqv, wkv = wq_ref[...], wk_ref[...]
            hs = range(hps)
            kn_f, kh, rk = _rms64(_heads64(kblk, kv_heads), wkv)
            kn = kn_f.astype(bf16)
            v = _heads64(vblk, kv_heads).astype(bf16)
            kn_q = jnp.stack([kn[kv_pos[kv_of(j)]] for j in hs]) if gqa else kn
            v_q = jnp.stack([v[kv_pos[kv_of(j)]] for j in hs]) if gqa else v
            qn_f, qh, rq = _rms64(_heads64(q_ref, hs), wqv)
            qn = qn_f.astype(bf16)
            col = lambda ref: jnp.stack([ref[:, j * 64:j * 64 + 1] for j in hs])
            slope = jnp.stack([sl_ref[0, :, j * 64:j * 64 + 1] for j in hs])
            s = _dot(qn, kn_q, BNT) * (HEAD_DIM ** -0.5) + slope * negd
            p = jnp.where(valid, jnp.exp(s - col(lse_ref)), 0.0)
            do_h = _heads64(do_ref, hs)
            delta = jnp.sum(do_h * _heads64(o_ref, hs), axis=-1, keepdims=True)
            do_b = do_h.astype(bf16)
            dp = _dot(do_b, v_q, BNT)
            ds = (p * (dp - delta + col(dlse_ref))).astype(bf16)
            dqn = _dot(ds, kn_q, BNN) * (HEAD_DIM ** -0.5)
            dkn_q = _dot(ds, qn, BTN) * (HEAD_DIM ** -0.5)
            dv_q = _dot(p.astype(bf16), do_b, BTN)
            if gqa:
                grp = lambda t: jnp.stack([sum(t[j] for j in hs if kv_of(j) == h) for h in kv_heads])
                dkn_q, dv_q = grp(dkn_q), grp(dv_q)
            dq, dwq_acc = _rms64_bwd(dqn, qh, rq, wqv)
            for j in hs:
                dq_ref[:, j * 64:(j + 1) * 64] = dq[j]
            dwq_ref[...] += dwq_acc
            dk_h, dwk_acc = _rms64_bwd(dkn_q, kh, rk, wkv)
            dwk_ref[...] += dwk_acc
            dk_all = jnp.concatenate([dk_h[i] for i in range(len(kv_heads))], axis=1)
            dv_all = jnp.concatenate([dv_q[i] for i in range(len(kv_heads))], axis=1)
            dk_ref[...] = ck[...] + dk_all[:BLOCK]
            dv_ref[...] = cv[...] + dv_all[:BLOCK]
            ck[...] = dk_all[BLOCK:]
            cv[...] = dv_all[BLOCK:]

        @pl.when(n == nb)
        def _():
            dk_ref[...] = ck[...]
            dv_ref[...] = cv[...]

    qcol, kcol, vcol, kvw = band.qcol, band.kcol, band.vcol, band.kvw
    cl = lambda n: jnp.minimum(n, nb - 1)
    pv = lambda n: jnp.maximum(jnp.minimum(n, nb - 1) - 1, 0)
    o_in = pl.BlockSpec((BLOCK, qw), lambda p, n: (cl(n), p))
    in_specs = [
        pl.BlockSpec((BLOCK, qw), lambda p, n: (cl(n), qcol(p))),
        pl.BlockSpec((BLOCK, kvw), lambda p, n: (pv(n), kcol(p))),
        pl.BlockSpec((BLOCK, kvw), lambda p, n: (cl(n), kcol(p))),
        pl.BlockSpec((BLOCK, kvw), lambda p, n: (pv(n), vcol(p))),
        pl.BlockSpec((BLOCK, kvw), lambda p, n: (cl(n), vcol(p))),
        pl.BlockSpec((1, 1, qw), lambda p, n: (p, 0, 0)),
        pl.BlockSpec((1, 64), lambda p, n: (0, 0)),
        pl.BlockSpec((1, 64), lambda p, n: (0, 0)),
        o_in, o_in, o_in, o_in,
        pl.BlockSpec((1, 64), lambda p, n: (0, 0)),
        pl.BlockSpec((1, 64), lambda p, n: (0, 0)),
    ]
    kv_out = pl.BlockSpec((BLOCK, kvw), lambda p, n: (jnp.maximum(n - 1, 0), p))
    w_out = pl.BlockSpec((1, 64), lambda p, n: (0, 0))
    return pl.pallas_call(
        body, grid=(P, nb + 1), in_specs=in_specs,
        out_specs=[o_in, kv_out, kv_out, w_out, w_out],
        out_shape=[jax.ShapeDtypeStruct((M, P * qw), f32), jax.ShapeDtypeStruct((M, P * kvw), f32),
                   jax.ShapeDtypeStruct((M, P * kvw), f32), jax.ShapeDtypeStruct((1, 64), f32),
                   jax.ShapeDtypeStruct((1, 64), f32)],
        scratch_shapes=[pltpu.VMEM((BLOCK, kvw), f32), pltpu.VMEM((BLOCK, kvw), f32)],
        compiler_params=_cp("arbitrary", "arbitrary"), name=name,
    )(hv, hv, hv, hv, hv, band.slopes, wq, wk, o, lse, do, dlse, *dw0)


def _head_sum(x):
    c = x.shape[1]
    r = lax.broadcasted_iota(jnp.int32, (c, c), 0) // HEAD_DIM
    q = lax.broadcasted_iota(jnp.int32, (c, c), 1) // HEAD_DIM
    return _dot(x, (r == q).astype(f32), precision=HI)


def attn_merge_fwd(oa, la, obs, lbs, sinkb, name):
    L = oa.shape[0]

    def fn(i, nb, oa_t, la_t, o1, o2, o3, l1, l2, l3, sk):
        ya = oa_t * jax.nn.sigmoid(la_t - sk)
        m = jnp.maximum(jnp.maximum(l1, l2), l3)
        e1, e2, e3 = jnp.exp(l1 - m), jnp.exp(l2 - m), jnp.exp(l3 - m)
        yb = (e1 * o1 + e2 * o2 + e3 * o3) / (e1 + e2 + e3)
        return jnp.concatenate([ya, yb], axis=1)

    rows = [(a, 0, 512, "cur") for a in (oa, la, *obs, *lbs)]
    return rowwise(fn, name=name, L=L, tm=_pick(L, 256, SUBLANE), rows=rows, consts=[sinkb], outs=[(1024, bf16)])


def attn_merge_bwd(dcat, oa, la, obs, lbs, sinkb, name):
    L = oa.shape[0]

    def fn(i, nb, da, db, oa_t, la_t, o1, o2, o3, l1, l2, l3, sk):
        keep = jax.nn.sigmoid(la_t - sk)
        dla = _head_sum(da * oa_t) * keep * (1.0 - keep)
        m = jnp.maximum(jnp.maximum(l1, l2), l3)
        e1, e2, e3 = jnp.exp(l1 - m), jnp.exp(l2 - m), jnp.exp(l3 - m)
        z = e1 + e2 + e3
        w1, w2, w3 = e1 / z, e2 / z, e3 / z
        g1, g2, g3 = _head_sum(db * o1), _head_sum(db * o2), _head_sum(db * o3)
        gm = w1 * g1 + w2 * g2 + w3 * g3
        return (da * keep, dla, w1 * db, w2 * db, w3 * db,
                w1 * (g1 - gm), w2 * (g2 - gm), w3 * (g3 - gm), -_colsum(dla))

    rows = [(dcat, 0, 512, "cur"), (dcat, 512, 512, "cur")] + [(a, 0, 512, "cur") for a in (oa, la, *obs, *lbs)]
    return rowwise(fn, name=name, L=L, tm=_pick(L, 256, SUBLANE), rows=rows, consts=[sinkb],
                   outs=[(512, f32)] * 8, sums=[(1, 512)])


def attn_assemble(dqa, dka, dva, dqs, dks, dvs, name):
    L = dqa.shape[0]

    def fn(i, nb, qa, ka, va, q1, q2, q3, k1, k2, k3, v1, v2, v3):
        return jnp.concatenate([qa, ka, va, q1 + q2 + q3, k1 + k2 + k3, v1 + v2 + v3], axis=1)

    rows = [(dqa, 0, 512, "cur"), (dka, 0, 128, "cur"), (dva, 0, 128, "cur")]
    rows += [(a, 0, 512, "cur") for a in (*dqs, *dks, *dvs)]
    return rowwise(fn, name=name, L=L, tm=_pick(L, 256, SUBLANE), rows=rows, outs=[(ATTN_IN, bf16)])


def attention_fwd(hin, wqa, wka, wqb, wkb, sinkb):
    L = hin.shape[0]
    oa, la = attn_fwd(hin, _Band(1, True), wqa, wka, "attn_a_fwd")
    obs, lbs = [], []
    for _, d in B_BRANCHES:
        o, l = attn_fwd(hin.reshape(L // d, d * ATTN_IN), _Band(d, False), wqb, wkb, f"attn_b{d}_fwd")
        obs.append(o.reshape(L, 512))
        lbs.append(l.reshape(L, 512))
    ocat = attn_merge_fwd(oa, la, obs, lbs, sinkb, "attn_merge_fwd")
    return ocat, (oa, la, obs, lbs)


def attention_bwd(hin, wqa, wka, wqb, wkb, sinkb, saved, dcat):
    L = hin.shape[0]
    oa, la, obs, lbs = saved
    res = attn_merge_bwd(dcat, oa, la, obs, lbs, sinkb, "attn_merge_bwd")
    doa, dla, dos, dls, dsink = res[0], res[1], res[2:5], res[5:8], res[8]
    zero = jnp.zeros((1, 64), f32)
    dqa, dka, dva, dwqa, dwka = attn_bwd(hin, _Band(1, True), wqa, wka, oa, la, doa, dla, (zero, zero), "attn_a_bwd")
    dqs, dks, dvs = [], [], []
    dwqb = dwkb = zero
    for g, (_, d) in enumerate(B_BRANCHES):
        M = L // d
        rs = lambda a: a.reshape(M, d * 512)
        dq, dk, dv, dwqb, dwkb = attn_bwd(hin.reshape(M, d * ATTN_IN), _Band(d, False), wqb, wkb, rs(obs[g]),
                                          rs(lbs[g]), rs(dos[g]), rs(dls[g]), (dwqb, dwkb), f"attn_b{d}_bwd")
        dqs.append(dq.reshape(L, 512))
        dks.append(dk.reshape(L, 512))
        dvs.append(dv.reshape(L, 512))
    dhin = attn_assemble(dqa, dka, dva, dqs, dks, dvs, "attn_assemble")
    return dhin, dwqa, dwka, dwqb, dwkb, dsink


NS = S5_GROUPS * S5_STATE


def _s5_param_fn(lr, li, ldt):
    dt = jnp.exp(ldt)
    mag, ang = jnp.exp(lr * dt), li * dt
    ab_re, ab_im = mag * jnp.cos(ang), mag * jnp.sin(ang)
    nr, ni = ab_re - 1.0, ab_im
    den = lr * lr + li * li
    return ab_re, ab_im, (nr * lr + ni * li) / den, (ni * lr - nr * li) / den


def s5_params_fwd(lr, li, ldt):
    def body(lr_ref, li_ref, ldt_ref, *outs):
        for o_ref, o in zip(outs, _s5_param_fn(lr_ref[...], li_ref[...], ldt_ref[...])):
            o_ref[...] = o

    return pl.pallas_call(body, out_shape=[jax.ShapeDtypeStruct(lr.shape, f32)] * 4, name="s5_params_fwd")(lr, li, ldt)


def s5_params_bwd(lr, li, ldt, cts):
    def body(lr_ref, li_ref, ldt_ref, c0, c1, c2, c3, dlr, dli, dldt):
        _, vjp = jax.vjp(_s5_param_fn, lr_ref[...], li_ref[...], ldt_ref[...])
        a, b, c = vjp((c0[...], c1[...], c2[...], c3[...]))
        dlr[...] = a
        dli[...] = b
        dldt[...] = c

    return pl.pallas_call(
        body, out_shape=[jax.ShapeDtypeStruct(lr.shape, f32), jax.ShapeDtypeStruct(li.shape, f32),
                         jax.ShapeDtypeStruct(ldt.shape, f32)], name="s5_params_bwd")(lr, li, ldt, *cts)


def _cmul(ar, ai, br, bi):
    return ar * br - ai * bi, ar * bi + ai * br


def s5_scan(z, ab_re, ab_im, f_re, f_im, *, reverse, name):
    L = z.shape[0]
    tm = _pick(L, 256, SUBLANE)
    nb = L // tm
    ng = tm // SUBLANE
    use_f = f_re is not None
    consts = [ab_re, ab_im] + ([f_re, f_im] if use_f else [])

    def body(*refs):
        z_ref = refs[0]
        c_refs = refs[1:1 + len(consts)]
        x_ref, car = refs[1 + len(consts)], refs[2 + len(consts)]
        i = pl.program_id(0)

        @pl.when(i == 0)
        def _():
            car[...] = jnp.zeros_like(car)

        a1 = (c_refs[0][...], c_refs[1][...])
        a2 = _cmul(*a1, *a1)
        a3 = _cmul(*a2, *a1)
        a4 = _cmul(*a2, *a2)
        pw = [a1, a2, a3, a4, _cmul(*a4, *a1), _cmul(*a4, *a2), _cmul(*a4, *a3), _cmul(*a4, *a4)]
        if reverse:
            pw = pw[::-1]
        pw_re = _stack_rows([p[0] for p in pw])
        pw_im = _stack_rows([p[1] for p in pw])
        ridx = lax.broadcasted_iota(jnp.int32, (SUBLANE, NS), 0)
        if use_f:
            fr, fi = c_refs[2][...], c_refs[3][...]

        def group(s, carry):
            cr, ci = carry
            g = (ng - 1 - s) if reverse else s
            r0 = pl.multiple_of(g * SUBLANE, SUBLANE)
            xr = z_ref[pl.ds(r0, SUBLANE), 0:NS]
            xi = z_ref[pl.ds(r0, SUBLANE), NS:2 * NS]
            if use_f:
                xr, xi = _cmul(fr, fi, xr, xi)
            for sft, (pr, pi) in ((1, a1), (2, a2), (4, a4)):
                if reverse:
                    keep = ridx < SUBLANE - sft
                    sr = jnp.where(keep, pltpu.roll(xr, SUBLANE - sft, 0), 0.0)
                    si = jnp.where(keep, pltpu.roll(xi, SUBLANE - sft, 0), 0.0)
                else:
                    keep = ridx >= sft
                    sr = jnp.where(keep, pltpu.roll(xr, sft, 0), 0.0)
                    si = jnp.where(keep, pltpu.roll(xi, sft, 0), 0.0)
                tr, ti = _cmul(pr, pi, sr, si)
                xr, xi = xr + tr, xi + ti
            tr, ti = _cmul(pw_re, pw_im, cr, ci)
            xr, xi = xr + tr, xi + ti
            x_ref[pl.ds(r0, SUBLANE), 0:NS] = xr
            x_ref[pl.ds(r0, SUBLANE), NS:2 * NS] = xi
            row = 0 if reverse else SUBLANE - 1
            return xr[row:row + 1, :], xi[row:row + 1, :]

        cr, ci = lax.fori_loop(0, ng, group, (car[0:1, 0:NS], car[0:1, NS:2 * NS]))
        car[0:1, 0:NS] = cr
        car[0:1, NS:2 * NS] = ci

    blk = (lambda i: (nb - 1 - i, 0)) if reverse else (lambda i: (i, 0))
    return pl.pallas_call(
        body, grid=(nb,),
        in_specs=[pl.BlockSpec((tm, 2 * NS), blk)] + [pl.BlockSpec((1, NS), lambda i: (0, 0))] * len(consts),
        out_specs=pl.BlockSpec((tm, 2 * NS), blk),
        out_shape=jax.ShapeDtypeStruct((L, 2 * NS), f32),
        scratch_shapes=[pltpu.VMEM((SUBLANE, 2 * NS), f32)],
        compiler_params=_cp("arbitrary"), name=name,
    )(z, *consts)


def _s5_post_fn(ypre, u, dvec, gw, gb):
    y = ypre + dvec * u
    g = jax.nn.gelu(y)
    z = _dot(g.astype(bf16), gw.astype(bf16)) + gb
    return g * jax.nn.sigmoid(z)


def s5_post_fwd(ypre, hin, dvec, gw, gb):
    L = ypre.shape[0]

    def fn(i, nb, yt, ut, dv, gwv, gbv):
        return _s5_post_fn(yt, ut, dv, gwv, gbv)

    return rowwise(fn, name="s5_post_fwd", L=L, tm=_pick(L, 512, SUBLANE),
                   rows=[(ypre, 0, S5_WIDTH, "cur"), (hin, 3072, S5_WIDTH, "cur")],
                   consts=[dvec, gw, gb], outs=[(S5_WIDTH, f32)])


def s5_post_bwd(ypre, hin, dvec, gw, gb, dycat):
    L = ypre.shape[0]

    def fn(i, nb, yt, ut, dyt, dv, gwv, gbv):
        _, vjp = jax.vjp(_s5_post_fn, yt, ut, dv, gwv, gbv)
        return vjp(dyt)

    return rowwise(fn, name="s5_post_bwd", L=L, tm=_pick(L, 512, SUBLANE),
                   rows=[(ypre, 0, S5_WIDTH, "cur"), (hin, 3072, S5_WIDTH, "cur"), (dycat, 0, S5_WIDTH, "cur")],
                   consts=[dvec, gw, gb], outs=[(S5_WIDTH, f32)] * 2,
                   sums=[(1, S5_WIDTH), (S5_WIDTH, S5_WIDTH), (1, S5_WIDTH)])


def s5_acc(G, X, bu, f_re, f_im):
    L = G.shape[0]

    def fn(i, nb, g, x, b, xp8, fr, fi):
        gr, gi = g[:, :NS], g[:, NS:]
        xp = _shift_down(x, xp8 * (i > 0).astype(f32), 1)
        xr, xi = xp[:, :NS], xp[:, NS:]
        br, bi = b[:, :NS], b[:, NS:]
        dbu = jnp.concatenate([fr * gr + fi * gi, fr * gi - fi * gr], axis=1)
        return (dbu, _colsum(xr * gr + xi * gi), _colsum(xr * gi - xi * gr),
                _colsum(br * gr + bi * gi), _colsum(br * gi - bi * gr))

    return rowwise(fn, name="s5_acc", L=L, tm=_pick(L, 256, SUBLANE),
                   rows=[(G, 0, 2 * NS, "cur"), (X, 0, 2 * NS, "cur"), (bu, 0, 2 * NS, "cur"), (X, 0, 2 * NS, "prev")],
                   consts=[f_re, f_im], outs=[(2 * NS, bf16)], sums=[(1, NS)] * 4)


def _s5_blockdiag(b_re, b_im, c_re, c_im):
    eye = jnp.eye(S5_GROUPS, dtype=f32)
    bb = lambda b: jnp.einsum("gpi,gh->gihp", b, eye).reshape(S5_WIDTH, NS)
    cc = lambda c: jnp.einsum("gip,gh->gphi", c, eye).reshape(NS, S5_WIDTH)
    return jnp.concatenate([bb(b_re), bb(b_im)], axis=1), jnp.concatenate([cc(c_re), -cc(c_im)], axis=0)


def _s5_blockdiag_grads(dB, dC):
    gb = lambda m: jnp.einsum("gigp->gpi", m.reshape(S5_GROUPS, S5_GROUP, S5_GROUPS, S5_STATE))
    gc = lambda m: jnp.einsum("gpgi->gip", m.reshape(S5_GROUPS, S5_STATE, S5_GROUPS, S5_GROUP))
    return gb(dB[:, :NS]), gb(dB[:, NS:]), gc(dC[:NS]), -gc(dC[NS:])


def s5_fwd(hin, prm):
    ab_re, ab_im, f_re, f_im = s5_params_fwd(prm["lr"], prm["li"], prm["ldt"])
    flat = lambda a: a.reshape(1, NS)
    ab_re, ab_im, f_re, f_im = flat(ab_re), flat(ab_im), flat(f_re), flat(f_im)
    Bblk, Cblk = _s5_blockdiag(prm["b_re"], prm["b_im"], prm["c_re"], prm["c_im"])
    bu = mm(hin, Bblk, name="s5_bu", a_win=(3072, S5_WIDTH))
    X = s5_scan(bu, ab_re, ab_im, f_re, f_im, reverse=False, name="s5_scan_fwd")
    ypre = mm(X, Cblk, name="s5_y")
    yc = s5_post_fwd(ypre, hin, prm["d"], prm["gw"], prm["gb"])
    return yc, (ab_re, ab_im, f_re, f_im, Bblk, Cblk, bu, X, ypre)


def s5_bwd(hin, prm, saved, dycat):
    ab_re, ab_im, f_re, f_im, Bblk, Cblk, bu, X, ypre = saved
    dypre, du_skip, dd, dgw, dgb = s5_post_bwd(ypre, hin, prm["d"], prm["gw"], prm["gb"], dycat)
    dX = mm(dypre, Cblk, tb=True, name="s5_dx")
    dC = mm(X, dypre, ta=True, name="s5_dc")
    G = s5_scan(dX, ab_re, -ab_im, None, None, reverse=True, name="s5_scan_bwd")
    dbu, dar, dai, dfr, dfi = s5_acc(G, X, bu, f_re, f_im)
    dB = mm(hin, dbu, ta=True, a_win=(3072, S5_WIDTH), name="s5_db")
    du_b = mm(dbu, Bblk, tb=True, name="s5_du")
    sh = prm["lr"].shape
    dlr, dli, dldt = s5_params_bwd(prm["lr"], prm["li"], prm["ldt"],
                                   [a.reshape(sh) for a in (dar, dai, dfr, dfi)])
    db_re, db_im, dc_re, dc_im = _s5_blockdiag_grads(dB, dC)
    grads = dict(lr=dlr, li=dli, ldt=dldt, b_re=db_re, b_im=db_im, c_re=dc_re, c_im=dc_im, d=dd, gw=dgw, gb=dgb)
    return du_skip, du_b, grads


DN_W = DN_HEADS * DN_DK
QKV_W = 3 * DN_W


def _softplus(x):
    return jnp.maximum(x, 0.0) + jnp.log(1.0 + jnp.exp(-jnp.abs(x)))


def _dn_pre(c, ab, alog, dtb):
    s = _silu(c)
    parts = []
    for h in range(2 * DN_HEADS):
        sh = s[:, h * 128:(h + 1) * 128]
        scale = DN_DK ** -0.5 if h < DN_HEADS else 1.0
        parts.append(sh * (lax.rsqrt(jnp.sum(sh * sh, axis=-1, keepdims=True) + EPS) * scale))
    parts.append(s[:, 2 * DN_W:])
    g = -jnp.exp(alog) * _softplus(ab[:, :128] + dtb)
    beta = jax.nn.sigmoid(ab[:, 128:])
    return jnp.concatenate(parts, axis=1), jnp.concatenate([g, beta], axis=1)


def _dn_pre_bwd(c, ab, alog, dtb, dqkv, dgb):
    sg = jax.nn.sigmoid(c)
    s = c * sg
    parts = []
    for h in range(2 * DN_HEADS):
        sh = s[:, h * 128:(h + 1) * 128]
        dy = dqkv[:, h * 128:(h + 1) * 128]
        scale = DN_DK ** -0.5 if h < DN_HEADS else 1.0
        r = lax.rsqrt(jnp.sum(sh * sh, axis=-1, keepdims=True) + EPS)
        parts.append(scale * r * (dy - sh * (r * r) * jnp.sum(dy * sh, axis=-1, keepdims=True)))
    parts.append(dqkv[:, 2 * DN_W:])
    dc = jnp.concatenate(parts, axis=1) * (sg * (1.0 + c * (1.0 - sg)))
    pre = ab[:, :128] + dtb
    ea = jnp.exp(alog)
    dg = dgb[:, :128]
    da = dg * (-ea) * jax.nn.sigmoid(pre)
    dalog = _colsum(dg * (-ea) * _softplus(pre))
    beta = jax.nn.sigmoid(ab[:, 128:])
    db = dgb[:, 128:] * beta * (1.0 - beta)
    return dc, jnp.concatenate([da, db], axis=1), dalog, _colsum(da)


def dn_pre_fwd(hin, conv_w, alog, dtb):
    L = hin.shape[0]

    def fn(i, nb, x, ab, p8, w, al, db):
        c = _conv_causal(x, p8 * (i > 0).astype(f32), w)
        return _dn_pre(c, ab, al, db)

    return rowwise(fn, name="dn_pre_fwd", L=L, tm=_pick(L, 256, SUBLANE),
                   rows=[(hin, 0, QKV_W, "cur"), (hin, 3328, 256, "cur"), (hin, 0, QKV_W, "prev")],
                   consts=[conv_w, alog, dtb], outs=[(QKV_W, f32), (256, f32)])


def dn_pre_bwd(hin, conv_w, alog, dtb, dqkv3, dg, dbeta):
    L = hin.shape[0]

    def fn(i, nb, x, ab, dq, dk, dv, dgt, dbt, p8, w, al, db):
        c = _conv_causal(x, p8 * (i > 0).astype(f32), w)
        return _dn_pre_bwd(c, ab, al, db, jnp.concatenate([dq, dk, dv], axis=1), jnp.concatenate([dgt, dbt], axis=1))

    rows = [(hin, 0, QKV_W, "cur"), (hin, 3328, 256, "cur")] + [(a, 0, DN_W, "cur") for a in dqkv3]
    rows += [(dg, 0, 128, "cur"), (dbeta, 0, 128, "cur"), (hin, 0, QKV_W, "prev")]
    return rowwise(fn, name="dn_pre_bwd", L=L, tm=_pick(L, 128, SUBLANE), rows=rows,
                   consts=[conv_w, alog, dtb], outs=[(QKV_W, f32), (256, f32)], sums=[(1, 128), (1, 128)])


def _split(a):
    hi = a.astype(bf16)
    return hi, (a - hi.astype(f32)).astype(bf16)


def _dot3_raw(a, b, dims):
    ah, al = _split(a)
    bh, bl = _split(b)
    return _dot(ah, bh, dims) + (_dot(ah, bl, dims) + _dot(al, bh, dims))


@functools.partial(jax.custom_vjp, nondiff_argnums=(2,))
def _dot3(a, b, dims=NN):
    return _dot3_raw(a, b, dims)


def _dot3_fwd(a, b, dims):
    return _dot3_raw(a, b, dims), (a, b)


BNN = (((2,), (1,)), ((0,), (0,)))
BNT = (((2,), (2,)), ((0,), (0,)))
BTN = (((1,), (1,)), ((0,), (0,)))


def _dot3_bwd(dims, res, g):
    a, b = res
    nn, nt, tn = (BNN, BNT, BTN) if dims[1][0] else (NN, NT, TN)
    if dims == nn:
        return _dot3_raw(g, b, nt), _dot3_raw(a, g, tn)
    if dims == nt:
        return _dot3_raw(g, b, nn), _dot3_raw(g, a, tn)
    assert dims == tn
    return _dot3_raw(b, g, nt), _dot3_raw(a, g, nn)


_dot3.defvjp(_dot3_fwd, _dot3_bwd)


def _dn_chunk(q, k, v, gcol, bcol, S):
    C = q.shape[1]
    r = lax.broadcasted_iota(jnp.int32, (C, C), 0)
    c = lax.broadcasted_iota(jnp.int32, (C, C), 1)
    tril = (r >= c).astype(f32)
    strict = (r > c).astype(f32)
    eye = (r == c).astype(f32)
    hd = _dot3
    grow = jnp.sum(eye * gcol, axis=1, keepdims=True)
    Gcol = jnp.sum(tril * grow, axis=2, keepdims=True)
    Grow = jnp.sum(eye * Gcol, axis=1, keepdims=True)
    gamma = jnp.exp((Gcol - Grow) * tril) * tril
    nmat = strict * bcol * hd(k, k, BNT) * gamma
    T = eye - nmat
    Pw = hd(nmat, nmat, BNN)
    for step in range(5):
        T = T + hd(T, Pw, BNN)
        if step < 4:
            Pw = hd(Pw, Pw, BNN)
    eG = jnp.exp(Gcol)
    u = hd(T, bcol * v, BNN)
    w = hd(T, (bcol * eG) * k, BNN)
    qk = hd(q, k, BNT) * gamma
    vnew = u - hd(w, S, BNN)
    o = hd(q * eG, S, BNN) + hd(qk, vnew, BNN)
    Glast = jnp.sum(gcol, axis=1, keepdims=True)
    S2 = S * jnp.exp(Glast) + hd(k * jnp.exp(Glast - Gcol), vnew, BTN)
    return o, S2


def _heads(x_ref):
    return jnp.stack([x_ref[:, h * 128:(h + 1) * 128] for h in range(DN_HEADS)])


def _head_cols(g_ref):
    return jnp.stack([g_ref[:, h:h + 1] for h in range(DN_HEADS)])


def dn_chunks_fwd(qkvn, gb):
    L = qkvn.shape[0]
    C = DN_CHUNK
    nc = L // C

    def body(q_ref, k_ref, v_ref, g_ref, b_ref, o_ref, sin_ref, S):
        n = pl.program_id(0)

        @pl.when(n == 0)
        def _():
            S[...] = jnp.zeros_like(S)

        s_in = S[...]
        sin_ref[...] = s_in
        o, s2 = _dn_chunk(_heads(q_ref), _heads(k_ref), _heads(v_ref), _head_cols(g_ref), _head_cols(b_ref), s_in)
        for h in range(DN_HEADS):
            o_ref[:, h * 128:(h + 1) * 128] = o[h]
        S[...] = s2

    blk = lambda j: pl.BlockSpec((C, DN_W), lambda n, j=j: (n, j))
    gblk = lambda j: pl.BlockSpec((C, 128), lambda n, j=j: (n, j))
    return pl.pallas_call(
        body, grid=(nc,),
        in_specs=[blk(0), blk(1), blk(2), gblk(0), gblk(1)],
        out_specs=[pl.BlockSpec((C, DN_W), lambda n: (n, 0)),
                   pl.BlockSpec((DN_HEADS, None, 128, 128), lambda n: (0, n, 0, 0))],
        out_shape=[jax.ShapeDtypeStruct((L, DN_W), f32), jax.ShapeDtypeStruct((DN_HEADS, nc, 128, 128), f32)],
        scratch_shapes=[pltpu.VMEM((DN_HEADS, 128, 128), f32)],
        compiler_params=_cp("arbitrary"), name="dn_chunks_fwd",
    )(qkvn, qkvn, qkvn, gb, gb)


def dn_chunks_bwd(qkvn, gb, s_in, do):
    L = qkvn.shape[0]
    C = DN_CHUNK
    nc = L // C

    def body(q_ref, k_ref, v_ref, g_ref, b_ref, sin_ref, do_ref, dq_ref, dk_ref, dv_ref, dg_ref, db_ref, dS):
        n = pl.program_id(0)

        @pl.when(n == 0)
        def _():
            dS[...] = jnp.zeros_like(dS)

        args = (_heads(q_ref), _heads(k_ref), _heads(v_ref), _head_cols(g_ref), _head_cols(b_ref), sin_ref[...])
        _, vjp = jax.vjp(_dn_chunk, *args)
        dq, dk, dv, dg, db, ds = vjp((_heads(do_ref), dS[...]))
        lane = lax.broadcasted_iota(jnp.int32, (C, 128), 1)
        dg_all = jnp.zeros((C, 128), f32)
        db_all = jnp.zeros((C, 128), f32)
        for h in range(DN_HEADS):
            sl = slice(h * 128, (h + 1) * 128)
            dq_ref[:, sl] = dq[h]
            dk_ref[:, sl] = dk[h]
            dv_ref[:, sl] = dv[h]
            dg_all = dg_all + jnp.where(lane == h, dg[h], 0.0)
            db_all = db_all + jnp.where(lane == h, db[h], 0.0)
        dS[...] = ds
        dg_ref[...] = dg_all
        db_ref[...] = db_all

    rv = lambda n: nc - 1 - n
    blk = lambda j: pl.BlockSpec((C, DN_W), lambda n, j=j: (rv(n), j))
    gblk = lambda j: pl.BlockSpec((C, 128), lambda n, j=j: (rv(n), j))
    oblk = pl.BlockSpec((C, DN_W), lambda n: (rv(n), 0))
    gout = pl.BlockSpec((C, 128), lambda n: (rv(n), 0))
    return pl.pallas_call(
        body, grid=(nc,),
        in_specs=[blk(0), blk(1), blk(2), gblk(0), gblk(1),
                  pl.BlockSpec((DN_HEADS, None, 128, 128), lambda n: (0, rv(n), 0, 0)), oblk],
        out_specs=[oblk] * 3 + [gout] * 2,
        out_shape=[jax.ShapeDtypeStruct((L, DN_W), f32)] * 3 + [jax.ShapeDtypeStruct((L, 128), f32)] * 2,
        scratch_shapes=[pltpu.VMEM((DN_HEADS, 128, 128), f32)],
        compiler_params=_cp("arbitrary"), name="dn_chunks_bwd",
    )(qkvn, qkvn, qkvn, gb, gb, s_in, do)


def _dn_post(o, z, w):
    parts = []
    for h in range(DN_HEADS):
        oh = o[:, h * 128:(h + 1) * 128]
        r = lax.rsqrt(jnp.mean(oh * oh, axis=-1, keepdims=True) + EPS)
        parts.append(oh * r * w)
    return jnp.concatenate(parts, axis=1) * _silu(z)


def dn_post_fwd(o, hin, yc, onorm):
    L = o.shape[0]

    def fn(i, nb, ot, zt, yct, w):
        return jnp.concatenate([yct, _dn_post(ot, zt, w)], axis=1)

    return rowwise(fn, name="dn_post_fwd", L=L, tm=_pick(L, 256, SUBLANE),
                   rows=[(o, 0, DN_W, "cur"), (hin, 2304, DN_W, "cur"), (yc, 0, S5_WIDTH, "cur")],
                   consts=[onorm], outs=[(1024, bf16)])


def dn_post_bwd(o, hin, onorm, dycat):
    L = o.shape[0]

    def fn(i, nb, ot, zt, d0, d1, d2, w):
        dy = jnp.concatenate([d0, d1, d2], axis=1)
        sg = jax.nn.sigmoid(zt)
        sz = zt * sg
        dos, dw = [], jnp.zeros((1, 128), f32)
        nrm = []
        for h in range(DN_HEADS):
            sl = slice(h * 128, (h + 1) * 128)
            oh = ot[:, sl]
            r = lax.rsqrt(jnp.mean(oh * oh, axis=-1, keepdims=True) + EPS)
            ohat = oh * r
            t = dy[:, sl] * sz[:, sl]
            dw = dw + _colsum(t * ohat)
            t = t * w
            dos.append(r * (t - ohat * jnp.mean(t * ohat, axis=-1, keepdims=True)))
            nrm.append(ohat * w)
        dz = dy * jnp.concatenate(nrm, axis=1) * (sg * (1.0 + zt * (1.0 - sg)))
        return jnp.concatenate(dos, axis=1), dz, dw

    rows = [(o, 0, DN_W, "cur"), (hin, 2304, DN_W, "cur")] + [(dycat, 256 * (1 + j), 256, "cur") for j in range(3)]
    return rowwise(fn, name="dn_post_bwd", L=L, tm=_pick(L, 256, SUBLANE), rows=rows,
                   consts=[onorm], outs=[(DN_W, f32), (DN_W, f32)], sums=[(1, 128)])


def conv_bwd_win(xarr, start, C, w, dc, name):
    L = xarr.shape[0]
    W = w.shape[0]

    def fn(i, nb, xt, dct, p8, n8, wv):
        dx = _conv_causal_bwd_x(dct, n8 * (i < nb - 1).astype(f32), wv)
        dw = _conv_causal_bwd_w(dct, xt, p8 * (i > 0).astype(f32), W)
        return dx, dw

    return rowwise(fn, name=name, L=L, tm=_pick(L, 128, SUBLANE),
                   rows=[(xarr, start, C, "cur"), (dc, 0, C, "cur"), (xarr, start, C, "prev"), (dc, 0, C, "next")],
                   consts=[w], outs=[(C, bf16)], sums=[(SUBLANE, C)])


def rec_assemble(dx_qkv, dz, du1, du2, dab):
    L = dz.shape[0]

    def fn(i, nb, a, b, c, d, e):
        return jnp.concatenate([a.astype(f32), b, c + d, e], axis=1)

    return rowwise(fn, name="rec_assemble", L=L, tm=_pick(L, 256, SUBLANE),
                   rows=[(dx_qkv, 0, QKV_W, "cur"), (dz, 0, DN_W, "cur"), (du1, 0, 256, "cur"),
                         (du2, 0, 256, "cur"), (dab, 0, 256, "cur")], outs=[(REC_PAD, bf16)])


def deltanet_fwd(hin, prm, yc):
    qkvn, gb = dn_pre_fwd(hin, prm["conv"], prm["alog"], prm["dtb"])
    o, s_in = dn_chunks_fwd(qkvn, gb)
    ycat = dn_post_fwd(o, hin, yc, prm["onorm"])
    return ycat, (qkvn, gb, o, s_in)


def deltanet_bwd(hin, prm, saved, dycat):
    qkvn, gb, o, s_in = saved
    do, dz, donorm = dn_post_bwd(o, hin, prm["onorm"], dycat)
    dq, dk, dv, dgH, dbH = dn_chunks_bwd(qkvn, gb, s_in, do)
    dc, dab, dalog, ddtb = dn_pre_bwd(hin, prm["conv"], prm["alog"], prm["dtb"], (dq, dk, dv), dgH, dbH)
    dx_qkv, dconv = conv_bwd_win(hin, 0, QKV_W, prm["conv"], dc, "dn_conv_bwd")
    return dx_qkv, dz, dab, dict(conv=dconv[:DN_CONV], alog=dalog, dtb=ddtb, onorm=donorm)


AXES = ("x", "y", "c")


def _collective(x, axes, mode, name, nchunk=1):
    k = len(axes)
    P = 2 ** k
    shape = x.shape if mode == "gather" else x.shape[1:]
    rows = shape[0] // nchunk
    assert rows * nchunk == shape[0]

    def body(x_ref, out_ref, send_sems, recv_sems, local_sems):
        co = {a: lax.axis_index(a) for a in AXES}
        me = 0
        for a in axes:
            me = me * 2 + co[a]

        def src(j, q):
            s = x_ref if mode == "gather" else x_ref.at[j]
            return s.at[pl.ds(q * rows, rows)]

        def dst(j, q):
            return out_ref.at[j].at[pl.ds(q * rows, rows)]

        locals_ = [pltpu.make_async_copy(src(me, q), dst(me, q), local_sems.at[q]) for q in range(nchunk)]
        for cp in locals_:
            cp.start()
        sends = []
        for m in range(1, P):
            tco = dict(co)
            t = 0
            for i, a in enumerate(axes):
                if (m >> (k - 1 - i)) & 1:
                    tco[a] = 1 - co[a]
                t = t * 2 + tco[a]
            dev = tuple(tco[a] for a in AXES)
            for q in range(nchunk):
                s = (m - 1) * nchunk + q
                cp = pltpu.make_async_remote_copy(src_ref=src(t, q), dst_ref=dst(me, q), send_sem=send_sems.at[s],
                                                  recv_sem=recv_sems.at[s], device_id=dev, device_id_type=MESH)
                cp.start()
                sends.append((cp, t, q, s, dev))
        for cp, t, q, s, dev in sends:
            pltpu.make_async_remote_copy(src_ref=src(t, q), dst_ref=dst(t, q), send_sem=send_sems.at[s],
                                         recv_sem=recv_sems.at[s], device_id=dev, device_id_type=MESH).wait_recv()
        for cp, *_ in sends:
            cp.wait_send()
        for cp in locals_:
            cp.wait()

    ns = (P - 1) * nchunk
    return pl.pallas_call(
        body,
        in_specs=[pl.BlockSpec(memory_space=pl.ANY)],
        out_specs=pl.BlockSpec(memory_space=pl.ANY),
        out_shape=jax.ShapeDtypeStruct((P,) + tuple(shape), x.dtype),
        scratch_shapes=[pltpu.SemaphoreType.DMA((ns,)), pltpu.SemaphoreType.DMA((ns,)),
                        pltpu.SemaphoreType.DMA((nchunk,))],
        name=name,
    )(x)


def all_gather(x, axes, name, nchunk=1):
    return _collective(x, axes, "gather", name, nchunk)


def exchange(x, axes, name, nchunk=1):
    return _collective(x, axes, "exchange", name, nchunk)


def sum_slots(x, name, out_dtype=f32):
    P, R, C = x.shape
    tr = _pick(R, 256, 2 * SUBLANE)

    def body(x_ref, o_ref):
        acc = x_ref[0].astype(f32)
        for j in range(1, P):
            acc = acc + x_ref[j].astype(f32)
        o_ref[...] = acc.astype(o_ref.dtype)

    return pl.pallas_call(
        body, grid=(R // tr,), in_specs=[pl.BlockSpec((P, tr, C), lambda i: (0, i, 0))],
        out_specs=pl.BlockSpec((tr, C), lambda i: (i, 0)), out_shape=jax.ShapeDtypeStruct((R, C), out_dtype),
        compiler_params=_cp("parallel"), name=name,
    )(x)


def _pack(arrs, width, row_mult, dtype):
    flat = jnp.concatenate([a.astype(dtype).reshape(-1) for a in arrs])
    unit = width * row_mult
    n = -(-flat.shape[0] // unit) * unit
    return jnp.pad(flat, (0, n - flat.shape[0])).reshape(n // width, width)


def _unpack(flat, shapes):
    flat = flat.reshape(-1)
    out, off = [], 0
    for s in shapes:
        n = int(np.prod(s))
        out.append(flat[off:off + n].reshape(s))
        off += n
    return out


def ada_fwd(c_all, ada_w):
    def body(c_ref, w_ref, o_ref):
        cond = _silu(c_ref[...])
        for l in range(ada_w.shape[0]):
            o_ref[l] = _dot(cond, w_ref[l], precision=HI)

    return pl.pallas_call(body, out_shape=jax.ShapeDtypeStruct((ada_w.shape[0], c_all.shape[0], ada_w.shape[2]), f32),
                          compiler_params=pltpu.CompilerParams(vmem_limit_bytes=VMEM_LIMIT), name="ada_fwd")(c_all, ada_w)


def ada_bwd(c_all, dmod):
    def body(c_ref, d_ref, o_ref):
        cond = _silu(c_ref[...])
        for l in range(dmod.shape[0]):
            o_ref[l] = _dot(cond, d_ref[l], TN, precision=HI)

    return pl.pallas_call(body, out_shape=jax.ShapeDtypeStruct((dmod.shape[0], c_all.shape[1], dmod.shape[2]), f32),
                          compiler_params=pltpu.CompilerParams(vmem_limit_bytes=VMEM_LIMIT), name="ada_bwd")(c_all, dmod)


def loss_fwd_bwd(y, target):
    L, D = y.shape

    def fn(i, nb, yt, tt):
        e = yt - tt
        return e * (1.0 / D), jnp.sum(jnp.sum(e * e, axis=1, keepdims=True), axis=0, keepdims=True)

    return rowwise(fn, name="loss", L=L, tm=_pick(L, 512, SUBLANE), rows=[(y, 0, D, "cur"), (target, 0, D, "cur")],
                   outs=[(D, f32)], sums=[(1, 1)])


def adamw(w, g, m, v, name):
    R, C = w.shape

    def fn(i, nb, wt, gt, mt, vt):
        m2 = ADAM_B1 * mt + (1.0 - ADAM_B1) * gt
        v2 = ADAM_B2 * vt + (1.0 - ADAM_B2) * (gt * gt)
        m_hat = m2 / (1.0 - ADAM_B1 ** ADAM_STEP)
        v_hat = v2 / (1.0 - ADAM_B2 ** ADAM_STEP)
        delta = -ADAM_LR * (m_hat / (jnp.sqrt(v_hat) + ADAM_EPS) + ADAM_WD * wt)
        return delta, m2, v2

    return rowwise(fn, name=name, L=R, tm=_pick(R, 256, SUBLANE), rows=[(a, 0, C, "cur") for a in (w, g, m, v)],
                   outs=[(C, f32)] * 3)


W_NAMES = ["ada_w", "ada_b", "norm_mix", "norm_ffn", "attn_w_in", "attn_q_norm_a", "attn_k_norm_a", "attn_q_norm_b",
           "attn_k_norm_b", "attn_sinks", "attn_w_out", "rec_w_in", "s5_lambda_re", "s5_lambda_im", "s5_log_dt",
           "s5_b_re", "s5_b_im", "s5_c_re", "s5_c_im", "s5_d", "s5_glu_w", "s5_glu_b", "dn_conv", "dn_a_log",
           "dn_dt_bias", "dn_out_norm", "rec_w_out", "ffn_w_up", "ffn_conv", "ffn_w_down"]
BIG = ["attn_w_in", "attn_w_out", "rec_w_in", "rec_w_out", "ffn_w_up", "ffn_w_down"]
SMALL_SHARDED = ["s5_d", "s5_glu_w", "s5_glu_b", "dn_conv", "ffn_conv"]
SMALL_REPL = [n for n in W_NAMES if n not in BIG and n not in SMALL_SHARDED and n != "ada_w"]
NSH = 4


def _unshard(g, name):
    ax = {"attn_w_in": 2, "attn_w_out": 1, "rec_w_in": 2, "rec_w_out": 1, "ffn_w_up": 2, "ffn_w_down": 1,
          "s5_d": 1, "s5_glu_w": 1, "s5_glu_b": 1, "dn_conv": 2, "ffn_conv": 2}[name]
    g = jnp.moveaxis(g, 0, ax)
    s = g.shape
    return g.reshape(s[:ax] + (s[ax] * s[ax + 1],) + s[ax + 2:])


def _to_shards(full, name):
    ax = {"attn_w_in": 2, "attn_w_out": 1, "rec_w_in": 2, "rec_w_out": 1, "ffn_w_up": 2, "ffn_w_down": 1,
          "s5_d": 1, "s5_glu_w": 1, "s5_glu_b": 1, "dn_conv": 2, "ffn_conv": 2}[name]
    s = full.shape
    g = full.reshape(s[:ax] + (NSH, s[ax] // NSH) + s[ax + 1:])
    return jnp.moveaxis(g, ax, 0)


def _rec_pad_cols(w):
    z6 = jnp.zeros(w.shape[:-1] + (122,), w.dtype)
    return jnp.concatenate([w[..., 256:3328], w[..., 0:256], w[..., 3328:3334], z6, w[..., 3334:3340], z6], axis=-1)


def _rec_unpad_cols(g):
    return jnp.concatenate([g[..., 3072:3328], g[..., 0:3072], g[..., 3328:3334], g[..., 3456:3462]], axis=-1)


def _ffn_fwd(x1, nf, sc, sh, gate, w_up, conv, w_dn, tag):
    h2 = modulate_fwd(x1, nf, sc, sh, f"{tag}_mod2_fwd")
    up = mm(h2, w_up, name=f"{tag}_ffn_up")
    act = ffn_act_fwd(up, conv, f"{tag}_ffn_act_fwd")
    f, x2 = mm(act, w_dn, name=f"{tag}_ffn_down", out_dtypes=(f32, f32), epi=_resid_epi, epi_mn=[x1], epi_n=[gate])
    return x2, (h2, up, act, f)


def _ffn_bwd(dx, x1, nf, sc, sh, gate, w_up, conv, w_dn, saved, tag):
    h2, up, act, f = saved
    df, dgate = resid_bwd(dx, f, gate, f"{tag}_res2_bwd")
    dact = mm(df, w_dn, tb=True, name=f"{tag}_ffn_dact")
    dw_dn = mm(act, df, ta=True, name=f"{tag}_ffn_dwdown")
    dup, dconv = ffn_act_conv_bwd(up, conv, dact, f"{tag}_ffn_act_conv_bwd")
    dw_up = mm(h2, dup, ta=True, name=f"{tag}_ffn_dwup")
    dh2 = mm(dup, w_up, tb=True, name=f"{tag}_ffn_dh")
    dx, dnf, dsc, dsh = modulate_bwd(x1, nf, sc, sh, dh2, dx, f"{tag}_mod2_bwd")
    return dx, dict(nf=dnf, sc=dsc, sh=dsh, gate=dgate, w_up=dw_up, conv=dconv[:FFN_CONV], w_dn=dw_dn)


def kernel(x, c, ada_w, ada_b, norm_mix, norm_ffn, attn_w_in, attn_q_norm_a, attn_k_norm_a, attn_q_norm_b, attn_k_norm_b, attn_sinks, attn_w_out, rec_w_in, s5_lambda_re, s5_lambda_im, s5_log_dt, s5_b_re, s5_b_im, s5_c_re, s5_c_im, s5_d, s5_glu_w, s5_glu_b, dn_conv, dn_a_log, dn_dt_bias, dn_out_norm, rec_w_out, ffn_w_up, ffn_conv, ffn_w_down, loss_target, m_ada_w, m_ada_b, m_norm_mix, m_norm_ffn, m_attn_w_in, m_attn_q_norm_a, m_attn_k_norm_a, m_attn_q_norm_b, m_attn_k_norm_b, m_attn_sinks, m_attn_w_out, m_rec_w_in, m_s5_lambda_re, m_s5_lambda_im, m_s5_log_dt, m_s5_b_re, m_s5_b_im, m_s5_c_re, m_s5_c_im, m_s5_d, m_s5_glu_w, m_s5_glu_b, m_dn_conv, m_dn_a_log, m_dn_dt_bias, m_dn_out_norm, m_rec_w_out, m_ffn_w_up, m_ffn_conv, m_ffn_w_down, v_ada_w, v_ada_b, v_norm_mix, v_norm_ffn, v_attn_w_in, v_attn_q_norm_a, v_attn_k_norm_a, v_attn_q_norm_b, v_attn_k_norm_b, v_attn_sinks, v_attn_w_out, v_rec_w_in, v_s5_lambda_re, v_s5_lambda_im, v_s5_log_dt, v_s5_b_re, v_s5_b_im, v_s5_c_re, v_s5_c_im, v_s5_d, v_s5_glu_w, v_s5_glu_b, v_dn_conv, v_dn_a_log, v_dn_dt_bias, v_dn_out_norm, v_rec_w_out, v_ffn_w_up, v_ffn_conv, v_ffn_w_down):
    args = (ada_w, ada_b, norm_mix, norm_ffn, attn_w_in, attn_q_norm_a, attn_k_norm_a, attn_q_norm_b, attn_k_norm_b, attn_sinks, attn_w_out, rec_w_in, s5_lambda_re, s5_lambda_im, s5_log_dt, s5_b_re, s5_b_im, s5_c_re, s5_c_im, s5_d, s5_glu_w, s5_glu_b, dn_conv, dn_a_log, dn_dt_bias, dn_out_norm, rec_w_out, ffn_w_up, ffn_conv, ffn_w_down)
    ms = (m_ada_w, m_ada_b, m_norm_mix, m_norm_ffn, m_attn_w_in, m_attn_q_norm_a, m_attn_k_norm_a, m_attn_q_norm_b, m_attn_k_norm_b, m_attn_sinks, m_attn_w_out, m_rec_w_in, m_s5_lambda_re, m_s5_lambda_im, m_s5_log_dt, m_s5_b_re, m_s5_b_im, m_s5_c_re, m_s5_c_im, m_s5_d, m_s5_glu_w, m_s5_glu_b, m_dn_conv, m_dn_a_log, m_dn_dt_bias, m_dn_out_norm, m_rec_w_out, m_ffn_w_up, m_ffn_conv, m_ffn_w_down)
    vs = (v_ada_w, v_ada_b, v_norm_mix, v_norm_ffn, v_attn_w_in, v_attn_q_norm_a, v_attn_k_norm_a, v_attn_q_norm_b, v_attn_k_norm_b, v_attn_sinks, v_attn_w_out, v_rec_w_in, v_s5_lambda_re, v_s5_lambda_im, v_s5_log_dt, v_s5_b_re, v_s5_b_im, v_s5_c_re, v_s5_c_im, v_s5_d, v_s5_glu_w, v_s5_glu_b, v_dn_conv, v_dn_a_log, v_dn_dt_bias, v_dn_out_norm, v_rec_w_out, v_ffn_w_up, v_ffn_conv, v_ffn_w_down)
    W = dict(zip(W_NAMES, args))
    Mo = dict(zip(W_NAMES, ms))
    Vo = dict(zip(W_NAMES, vs))
    xi, yi, ci = lax.axis_index("x"), lax.axis_index("y"), lax.axis_index("c")
    shard = 2 * xi + yi
    me8 = 4 * xi + 2 * yi + ci
    xs = x[0]
    target = loss_target[0]
    L, D = xs.shape

    wflat = _pack([W[n] for n in BIG], 1024, 16, bf16)
    wfull = all_gather(wflat, ("x", "y"), "gather_w").reshape(NSH, -1)
    Wf = {}
    off = 0
    for n in BIG:
        sz = int(np.prod(W[n].shape))
        Wf[n] = _unshard(wfull[:, off:off + sz].reshape((NSH,) + W[n].shape), n)
        off += sz
    rec_w_in_p = _rec_pad_cols(Wf["rec_w_in"][0])

    sflat = _pack([c] + [W[n] for n in SMALL_SHARDED], 1024, 8, f32)
    s8 = all_gather(sflat, AXES, "gather_small")
    s8f = s8.reshape(8, -1)
    c_all = s8f[:, :D]
    Ws = {}
    off = D
    for n in SMALL_SHARDED:
        sz = int(np.prod(W[n].shape))
        Ws[n] = _unshard(s8f[0::2, off:off + sz].reshape((NSH,) + W[n].shape), n)
        off += sz

    modp = ada_fwd(c_all, ada_w)
    modg = all_gather(modp, ("x", "y"), "gather_mod")
    mod_all = jnp.moveaxis(modg, 0, 2).reshape(2, 8, -1) + ada_b[:, None, :]
    mod = lax.dynamic_slice(mod_all, (0, me8, 0), (2, 1, mod_all.shape[2]))[:, 0, :]
    mods = [[mod[l:l + 1, j * D:(j + 1) * D] for j in range(6)] for l in range(2)]

    sh1, sc1, g1, sh2, sc2, g2_ = mods[0]
    nm0, nf0 = norm_mix[0:1], norm_ffn[0:1]
    sinkb = jnp.repeat(attn_sinks[0], HEAD_DIM)[None]
    h0 = modulate_fwd(xs, nm0, sc1, sh1, "l0_mod1_fwd")
    hin0 = mm(h0, Wf["attn_w_in"][0], name="l0_in_proj")
    ocat, att_saved = attention_fwd(hin0, attn_q_norm_a, attn_k_norm_a, attn_q_norm_b, attn_k_norm_b, sinkb)
    y0, x1 = mm(ocat, Wf["attn_w_out"][0], name="l0_out_proj", out_dtypes=(f32, f32), epi=_resid_epi,
                epi_mn=[xs], epi_n=[g1])
    x2, ffn0_saved = _ffn_fwd(x1, nf0, sc2, sh2, g2_, Wf["ffn_w_up"][0], Ws["ffn_conv"][0], Wf["ffn_w_down"][0], "l0")

    th1, tc1, t1, th2, tc2, t2 = mods[1]
    nm1, nf1 = norm_mix[1:2], norm_ffn[1:2]
    pad128 = lambda a: jnp.pad(a, ((0, 0), (0, 128 - a.shape[1])))
    s5p = dict(lr=s5_lambda_re[0], li=s5_lambda_im[0], ldt=s5_log_dt[0][:, None], b_re=s5_b_re[0], b_im=s5_b_im[0],
               c_re=s5_c_re[0], c_im=s5_c_im[0], d=Ws["s5_d"], gw=Ws["s5_glu_w"][0], gb=Ws["s5_glu_b"])
    dnp = dict(conv=Ws["dn_conv"][0], alog=pad128(dn_a_log), dtb=pad128(dn_dt_bias), onorm=dn_out_norm)
    h1 = modulate_fwd(x2, nm1, tc1, th1, "l1_mod1_fwd")
    hin1 = mm(h1, rec_w_in_p, name="l1_in_proj")
    yc, s5_saved = s5_fwd(hin1, s5p)
    ycat, dn_saved = deltanet_fwd(hin1, dnp, yc)
    y1, x3 = mm(ycat, Wf["rec_w_out"][0], name="l1_out_proj", out_dtypes=(f32, f32), epi=_resid_epi,
                epi_mn=[x2], epi_n=[t1])
    x4, ffn1_saved = _ffn_fwd(x3, nf1, tc2, th2, t2, Wf["ffn_w_up"][1], Ws["ffn_conv"][1], Wf["ffn_w_down"][1], "l1")

    dx, sse = loss_fwd_bwd(x4, target)
    loss = lax.psum(0.5 * sse[0, 0] / D, AXES)

    dx, gf1 = _ffn_bwd(dx, x3, nf1, tc2, th2, t2, Wf["ffn_w_up"][1], Ws["ffn_conv"][1], Wf["ffn_w_down"][1], ffn1_saved, "l1")
    dy1, dt1 = resid_bwd(dx, y1, t1, "l1_res1_bwd")
    dycat = mm(dy1, Wf["rec_w_out"][0], tb=True, name="l1_dycat")
    dw_rec_out = mm(ycat, dy1, ta=True, name="l1_dwout")
    du_skip, du_b, s5g = s5_bwd(hin1, s5p, s5_saved, dycat)
    dx_qkv, dz, dab, dng = deltanet_bwd(hin1, dnp, dn_saved, dycat)
    dhin1 = rec_assemble(dx_qkv, dz, du_skip, du_b, dab)
    dw_rec_in = _rec_unpad_cols(mm(h1, dhin1, ta=True, name="l1_dwin"))
    dh1 = mm(dhin1, rec_w_in_p, tb=True, name="l1_dh")
    dx, dnm1, dtc1, dth1 = modulate_bwd(x2, nm1, tc1, th1, dh1, dx, "l1_mod1_bwd")

    dx, gf0 = _ffn_bwd(dx, x1, nf0, sc2, sh2, g2_, Wf["ffn_w_up"][0], Ws["ffn_conv"][0], Wf["ffn_w_down"][0], ffn0_saved, "l0")
    dy0, dg1 = resid_bwd(dx, y0, g1, "l0_res1_bwd")
    dcat = mm(dy0, Wf["attn_w_out"][0], tb=True, name="l0_dcat")
    dw_attn_out = mm(ocat, dy0, ta=True, name="l0_dwout")
    dhin0, dwqa, dwka, dwqb, dwkb, dsinkb = attention_bwd(hin0, attn_q_norm_a, attn_k_norm_a, attn_q_norm_b,
                                                          attn_k_norm_b, sinkb, att_saved, dcat)
    dw_attn_in = mm(h0, dhin0, ta=True, name="l0_dwin")
    dh0 = mm(dhin0, Wf["attn_w_in"][0], tb=True, name="l0_dh")
    grad_x, dnm0, dsc1, dsh1 = modulate_bwd(xs, nm0, sc1, sh1, dh0, dx, "l0_mod1_bwd")

    dmod = jnp.concatenate([
        jnp.concatenate([dsh1, dsc1, dg1, gf0["sh"], gf0["sc"], gf0["gate"]], axis=1),
        jnp.concatenate([dth1, dtc1, dt1, gf1["sh"], gf1["sc"], gf1["gate"]], axis=1)], axis=0)
    gl = {
        "ada_b": dmod,
        "norm_mix": jnp.concatenate([dnm0, dnm1], axis=0),
        "norm_ffn": jnp.concatenate([gf0["nf"], gf1["nf"]], axis=0),
        "attn_q_norm_a": dwqa, "attn_k_norm_a": dwka, "attn_q_norm_b": dwqb, "attn_k_norm_b": dwkb,
        "attn_sinks": dsinkb[:, ::HEAD_DIM],
        "s5_lambda_re": s5g["lr"][None], "s5_lambda_im": s5g["li"][None], "s5_log_dt": s5g["ldt"][:, 0][None],
        "s5_b_re": s5g["b_re"][None], "s5_b_im": s5g["b_im"][None], "s5_c_re": s5g["c_re"][None],
        "s5_c_im": s5g["c_im"][None],
        "dn_a_log": dng["alog"][:, :DN_HEADS], "dn_dt_bias": dng["dtb"][:, :DN_HEADS], "dn_out_norm": dng["onorm"],
        "s5_d": s5g["d"], "s5_glu_w": s5g["gw"][None], "s5_glu_b": s5g["gb"], "dn_conv": dng["conv"][None],
        "ffn_conv": jnp.stack([gf0["conv"], gf1["conv"]]),
        "attn_w_in": dw_attn_in[None], "attn_w_out": dw_attn_out[None], "rec_w_in": dw_rec_in[None],
        "rec_w_out": dw_rec_out[None], "ffn_w_up": jnp.stack([gf0["w_up"], gf1["w_up"]]),
        "ffn_w_down": jnp.stack([gf0["w_dn"], gf1["w_dn"]]),
    }

    small_names = SMALL_REPL + SMALL_SHARDED
    gs = _pack([gl[n] for n in small_names], 128, 256, f32)
    gs8 = all_gather(gs, AXES, "gather_small_grads")
    gsum = sum_slots(gs8, "sum_small_grads")
    full_shapes = [gl[n].shape for n in small_names]
    gfull = dict(zip(small_names, _unpack(gsum, full_shapes)))
    dmod_all = gs8.reshape(8, -1)[:, :2 * 6 * D].reshape(8, 2, 6 * D)
    ncol = ada_w.shape[2]
    dmod_sh = jnp.moveaxis(lax.dynamic_slice(dmod_all, (0, 0, shard * ncol), (8, 2, ncol)), 0, 1)
    grads = {"ada_w": ada_bwd(c_all, dmod_sh)}
    for n in SMALL_REPL:
        grads[n] = gfull[n]
    for n in SMALL_SHARDED:
        sh_all = _to_shards(gfull[n], n)
        grads[n] = lax.dynamic_slice(sh_all, (shard,) + (0,) * (sh_all.ndim - 1), (1,) + sh_all.shape[1:])[0]

    gflat = jnp.concatenate([_to_shards(gl[n], n).reshape(NSH, -1) for n in BIG], axis=1)
    nel = gflat.shape[1]
    unit = 256 * 1024
    npad = -(-nel // unit) * unit
    gflat = jnp.pad(gflat, ((0, 0), (0, npad - nel))).astype(bf16).reshape(NSH, npad // 1024, 1024)
    gq = exchange(gflat, ("x", "y"), "reduce_xy")
    gpart = sum_slots(gq, "sum_chips", bf16)
    gc = all_gather(gpart, ("c",), "gather_grad_c")
    gsh = sum_slots(gc, "sum_pair").reshape(-1)
    off = 0
    for n in BIG:
        sz = int(np.prod(W[n].shape))
        grads[n] = gsh[off:off + sz].reshape(W[n].shape)
        off += sz

    delta, new_m, new_v = {}, {}, {}

    def as2d(a):
        return a.reshape(-1, a.shape[-1])

    for n in ["ada_w"] + BIG:
        d_, m_, v_ = adamw(as2d(W[n]), as2d(grads[n]), as2d(Mo[n]), as2d(Vo[n]), f"adamw_{n}")
        delta[n], new_m[n], new_v[n] = d_.reshape(W[n].shape), m_.reshape(W[n].shape), v_.reshape(W[n].shape)
    pk = lambda dd: _pack([dd[n] for n in small_names], 128, 256, f32)
    d_, m_, v_ = adamw(pk(W), pk(grads), pk(Mo), pk(Vo), "adamw_small")
    shp = [W[n].shape for n in small_names]
    for dst, src in ((delta, d_), (new_m, m_), (new_v, v_)):
        dst.update(zip(small_names, _unpack(src, shp)))

    return (loss, grad_x[None], *[grads[n] for n in W_NAMES], *[delta[n] for n in W_NAMES],
            *[new_m[n] for n in W_NAMES], *[new_v[n] for n in W_NAMES])
```

```python
import functools
import math

import numpy as np
import jax
import jax.numpy as jnp
from jax import lax
from jax.experimental import pallas as pl
from jax.experimental.pallas import tpu as pltpu

f32 = jnp.float32
bf16 = jnp.bfloat16
HI = lax.Precision.HIGHEST
MESH = pl.DeviceIdType.MESH

HEAD_DIM = 64
BLOCK = 128
A_Q_HEADS = 8
A_KV_HEADS = 2
A_WINDOW = 128
B_HEADS = 8
B_BRANCHES = ((128, 1), (512, 4), (2048, 16))
N_ATTN_HEADS = 16
ATTN_IN = 2304
S5_GROUP = 16
S5_GROUPS = 16
S5_WIDTH = 256
S5_STATE = 64
DN_HEADS = 6
DN_DK = 128
DN_CONV = 4
DN_CHUNK = 64
REC_IN = 3340
REC_PAD = 3584
FFN_CONV = 3
EPS = 1e-6
ADAM_LR = 0.001
ADAM_B1 = 0.9
ADAM_B2 = 0.999
ADAM_EPS = 1e-08
ADAM_WD = 0.01
ADAM_STEP = 10

LANE = 128
SUBLANE = 8
VMEM_LIMIT = 52 * 1024 * 1024
MM_FULL_K = 5632
MM_VMEM_BUDGET = 40 * 1024 * 1024


def _cp(*sem):
    return pltpu.CompilerParams(dimension_semantics=sem, vmem_limit_bytes=VMEM_LIMIT)


def _pick(dim, cap, unit=LANE):
    for t in (2048, 1024, 768, 512, 384, 256, 128, 64, 32, 16, 8):
        if t <= cap and t % unit == 0 and dim % t == 0:
            return t
    return dim


def _dot(a, b, dims=(((1,), (0,)), ((), ())), precision=None):
    return lax.dot_general(a, b, dims, precision=precision, preferred_element_type=f32)


NN = (((1,), (0,)), ((), ()))
NT = (((1,), (1,)), ((), ()))
TN = (((0,), (0,)), ((), ()))


def mm(a, b, *, name, ta=False, tb=False, a_win=None, b_win=None, out_dtypes=(f32,),
       epi=None, epi_mn=(), epi_n=(), tm_cap=1024, tn_cap=8192, tk_cap=None):
    a0, a1 = a.shape
    b0, b1 = b.shape
    aw = a_win or (0, a1)
    bw = b_win or (0, b1)
    if ta:
        K, M = a0, aw[1]
    else:
        M, K = a0, aw[1]
    if tb:
        N, K2 = b0, bw[1]
    else:
        K2, N = b0, bw[1]
    assert K == K2, (a.shape, b.shape, ta, tb, a_win, b_win)
    if tk_cap is None:
        tk_cap = K if K <= MM_FULL_K else 2048
    tk = _pick(K, tk_cap, SUBLANE if (ta and not tb) else LANE)
    nk = K // tk
    sa, sb = a.dtype.itemsize, b.dtype.itemsize
    so = sum(jnp.dtype(d).itemsize for d in out_dtypes)
    n_mn, n_n, n_out = len(epi_mn), len(epi_n), len(out_dtypes)

    def vmem(tm_, tn_):
        return 2 * (tm_ * tk * sa + tk * tn_ * sb + tm_ * tn_ * (so + 4 * n_mn)) + 2 * tm_ * tn_ * 4

    best = None
    for tm_ in (t for t in (1024, 512, 256, 128) if M % t == 0 and (not ta or aw[0] % t == 0)):
        for tn_ in (t for t in (N, N // 2, 1024, 768, 512, 384, 256, 128)
                    if t % LANE == 0 and N % t == 0 and (tb or bw[0] % t == 0)):
            if tm_ <= tm_cap and tn_ <= max(tn_cap, 0) and vmem(tm_, tn_) <= MM_VMEM_BUDGET:
                if best is None or (tm_ * tn_, tn_) > (best[0] * best[1], best[1]):
                    best = (tm_, tn_)
    assert best is not None, (name, M, N, K)
    tm, tn = best
    b_outer = tk * tn * sb > tm * tk * sa

    def ix(f):
        if b_outer:
            return lambda j, i, k: f(i, j, k)
        return f

    if ta:
        mo = aw[0] // tm
        a_spec = pl.BlockSpec((tk, tm), ix(lambda i, j, k: (k, i + mo)))
    else:
        assert aw[0] % tk == 0
        ko = aw[0] // tk
        a_spec = pl.BlockSpec((tm, tk), ix(lambda i, j, k: (i, k + ko)))
    if tb:
        assert bw[0] % tk == 0
        kob = bw[0] // tk
        b_spec = pl.BlockSpec((tn, tk), ix(lambda i, j, k: (j, k + kob)))
    else:
        no = bw[0] // tn
        b_spec = pl.BlockSpec((tk, tn), ix(lambda i, j, k: (k, j + no)))
    dims = (((0 if ta else 1,), (1 if tb else 0,)), ((), ()))

    def body(a_ref, b_ref, *rest):
        mn_refs = rest[:n_mn]
        n_refs = rest[n_mn:n_mn + n_n]
        out_refs = rest[n_mn + n_n:n_mn + n_n + n_out]

        def finish(r):
            if epi is None:
                outs = (r,)
            else:
                outs = epi(r, *[m[...] for m in mn_refs], *[v[...] for v in n_refs])
            for o_ref, o in zip(out_refs, outs):
                o_ref[...] = o.astype(o_ref.dtype)

        part = _dot(a_ref[...].astype(bf16), b_ref[...].astype(bf16), dims)
        if nk == 1:
            finish(part)
            return
        acc = rest[-1]
        k = pl.program_id(2)

        @pl.when(k == 0)
        def _():
            acc[...] = part

        @pl.when(k > 0)
        def _():
            acc[...] += part

        @pl.when(k == nk - 1)
        def _():
            finish(acc[...])

    mn_spec = pl.BlockSpec((tm, tn), ix(lambda i, j, k: (i, j)))
    n_spec = pl.BlockSpec((1, tn), ix(lambda i, j, k: (0, j)))
    gi, gj = M // tm, N // tn
    outs = pl.pallas_call(
        body,
        grid=(gj, gi, nk) if b_outer else (gi, gj, nk),
        in_specs=[a_spec, b_spec] + [mn_spec] * n_mn + [n_spec] * n_n,
        out_specs=[mn_spec] * n_out,
        out_shape=[jax.ShapeDtypeStruct((M, N), d) for d in out_dtypes],
        scratch_shapes=[pltpu.VMEM((tm, tn), f32)] if nk > 1 else [],
        compiler_params=_cp("parallel", "parallel", "arbitrary"),
        name=name,
    )(a, b, *epi_mn, *epi_n)
    return outs[0] if n_out == 1 else tuple(outs)


def rowwise(fn, *, name, L, tm, rows=(), consts=(), outs=(), sums=()):
    nb = L // tm
    hb = tm // SUBLANE
    in_specs = []
    arrs = []
    for arr, start, width, kind in rows:
        assert start % width == 0, (name, start, width)
        co = start // width
        if kind == "cur":
            in_specs.append(pl.BlockSpec((tm, width), lambda i, co=co: (i, co)))
        elif kind == "prev":
            in_specs.append(pl.BlockSpec((SUBLANE, width), lambda i, co=co: (jnp.maximum(i * hb - 1, 0), co)))
        else:
            last = L // SUBLANE - 1
            in_specs.append(pl.BlockSpec((SUBLANE, width), lambda i, co=co, last=last: (jnp.minimum((i + 1) * hb, last), co)))
        arrs.append(arr)
    for cst in consts:
        assert cst.ndim == 2
        in_specs.append(pl.BlockSpec(cst.shape, lambda i: (0, 0)))
        arrs.append(cst)
    n_rows, n_c, n_o, n_s = len(rows), len(consts), len(outs), len(sums)
    out_specs = [pl.BlockSpec((tm, w), lambda i: (i, 0)) for w, _ in outs]
    out_specs += [pl.BlockSpec(s, lambda i: (0, 0)) for s in sums]
    out_shape = [jax.ShapeDtypeStruct((L, w), d) for w, d in outs]
    out_shape += [jax.ShapeDtypeStruct(s, f32) for s in sums]

    def body(*refs):
        i = pl.program_id(0)
        vals = [r[...] for r in refs[:n_rows + n_c]]
        res = fn(i, nb, *vals)
        if not isinstance(res, (tuple, list)):
            res = (res,)
        o_refs = refs[n_rows + n_c:n_rows + n_c + n_o]
        s_refs = refs[n_rows + n_c + n_o:]
        for o_ref, o in zip(o_refs, res[:n_o]):
            o_ref[...] = o.astype(o_ref.dtype)
        if n_s:
            @pl.when(i == 0)
            def _():
                for s_ref in s_refs:
                    s_ref[...] = jnp.zeros_like(s_ref)

            for s_ref, s in zip(s_refs, res[n_o:]):
                s_ref[...] += s

    res = pl.pallas_call(
        body,
        grid=(nb,),
        in_specs=in_specs,
        out_specs=out_specs,
        out_shape=out_shape,
        compiler_params=_cp("arbitrary" if n_s else "parallel"),
        name=name,
    )(*arrs)
    return res[0] if len(res) == 1 else tuple(res)


def _shift_down(x, prev8, k):
    cat = jnp.concatenate([prev8, x], axis=0)
    return pltpu.roll(cat, k, 0)[SUBLANE:, :]


def _shift_up(x, next8, k):
    n = x.shape[0]
    cat = jnp.concatenate([x, next8], axis=0)
    return pltpu.roll(cat, n + SUBLANE - k, 0)[:n, :]


def _colsum(x):
    return jnp.sum(x, axis=0, keepdims=True)


def _silu(x):
    return x * jax.nn.sigmoid(x)


def _modulate_fn(x, nw, sc, sh):
    r = lax.rsqrt(jnp.mean(x * x, axis=-1, keepdims=True) + EPS)
    return (x * r * nw) * (1.0 + sc) + sh


def modulate_fwd(x, nw, sc, sh, name):
    L, D = x.shape

    def fn(i, nb, xt, nwv, scv, shv):
        return _modulate_fn(xt, nwv, scv, shv)

    return rowwise(fn, name=name, L=L, tm=_pick(L, 512, SUBLANE), rows=[(x, 0, D, "cur")],
                   consts=[nw, sc, sh], outs=[(D, bf16)])


def modulate_bwd(x, nw, sc, sh, dh, dx_in, name):
    L, D = x.shape

    def fn(i, nb, xt, dht, dxt, nwv, scv, shv):
        _, vjp = jax.vjp(_modulate_fn, xt, nwv, scv, shv)
        dx, dnw, dsc, dsh = vjp(dht)
        return dxt + dx, dnw, dsc, dsh

    return rowwise(fn, name=name, L=L, tm=_pick(L, 256, SUBLANE),
                   rows=[(x, 0, D, "cur"), (dh, 0, D, "cur"), (dx_in, 0, D, "cur")],
                   consts=[nw, sc, sh], outs=[(D, f32)], sums=[(1, D)] * 3)


def resid_bwd(dx, y, g, name):
    L, D = dx.shape

    def fn(i, nb, dxt, yt, gv):
        return dxt * gv, _colsum(dxt * yt)

    return rowwise(fn, name=name, L=L, tm=_pick(L, 512, SUBLANE),
                   rows=[(dx, 0, D, "cur"), (y, 0, D, "cur")], consts=[g],
                   outs=[(D, bf16)], sums=[(1, D)])


def _resid_epi(acc, xt, gv):
    return acc, xt + gv * acc


def _stack_rows(rows, n=SUBLANE):
    c = rows[0].shape[1]
    ridx = lax.broadcasted_iota(jnp.int32, (n, c), 0)
    out = jnp.zeros((n, c), f32)
    for j, r in enumerate(rows):
        out = out + jnp.where(ridx == j, r, 0.0)
    return out


def _conv_causal(x, prev8, w):
    W = w.shape[0]
    y = x * w[W - 1:W, :]
    for j in range(W - 1):
        y = y + _shift_down(x, prev8, W - 1 - j) * w[j:j + 1, :]
    return y


def _conv_causal_bwd_x(dy, next8, w):
    W = w.shape[0]
    dx = dy * w[W - 1:W, :]
    for j in range(W - 1):
        dx = dx + _shift_up(dy, next8, W - 1 - j) * w[j:j + 1, :]
    return dx


def _conv_causal_bwd_w(dy, x, prev8, W):
    rows = [_colsum(dy * _shift_down(x, prev8, W - 1 - j)) for j in range(W - 1)]
    rows.append(_colsum(dy * x))
    return _stack_rows(rows)


def ffn_act_fwd(up, conv_w, name):
    L, F2 = up.shape
    F = F2 // 2

    def fn(i, nb, u, p8, w):
        c = _conv_causal(u, p8 * (i > 0).astype(f32), w)
        return _silu(c[:, :F]) * c[:, F:]

    return rowwise(fn, name=name, L=L, tm=_pick(L, 128, SUBLANE),
                   rows=[(up, 0, F2, "cur"), (up, 0, F2, "prev")], consts=[conv_w], outs=[(F, bf16)])


def ffn_act_conv_bwd(up, conv_w, dact, name):
    L, F2 = up.shape
    F = F2 // 2
    W = conv_w.shape[0]

    def fn(i, nb, u, da, p8, un8, dan8, w):
        tm = u.shape[0]
        more = (i < nb - 1).astype(f32)
        p8 = p8 * (i > 0).astype(f32)
        c = _conv_causal(jnp.concatenate([u, un8 * more], axis=0), p8, w)
        dae = jnp.concatenate([da, dan8 * more], axis=0)
        a, b = c[:, :F], c[:, F:]
        sg = jax.nn.sigmoid(a)
        dc = jnp.concatenate([dae * b * (sg * (1.0 + a * (1.0 - sg))), dae * a * sg], axis=1)
        dx = dc[:tm] * w[W - 1:W, :]
        for j in range(W - 1):
            dx = dx + pltpu.roll(dc, tm + SUBLANE - (W - 1 - j), 0)[:tm] * w[j:j + 1, :]
        return dx, _conv_causal_bwd_w(dc[:tm], u, p8, W)

    return rowwise(fn, name=name, L=L, tm=_pick(L, 128, SUBLANE),
                   rows=[(up, 0, F2, "cur"), (dact, 0, F, "cur"), (up, 0, F2, "prev"), (up, 0, F2, "next"),
                         (dact, 0, F, "next")],
                   consts=[conv_w], outs=[(F2, bf16)], sums=[(SUBLANE, F2)])


ALIBI = [2.0 ** (-8.0 * (i + 1) / N_ATTN_HEADS) for i in range(N_ATTN_HEADS)]
NEG = -1e30


class _Band:
    def __init__(self, dilation, group_a):
        d = dilation
        self.d = d
        self.group_a = group_a
        if group_a:
            self.P, self.qw, self.hps, self.kvw = 1, 512, 8, 128
            self.qcol = lambda p: 0
            self.kcol = lambda p: 4
            self.vcol = lambda p: 5
            self.kv_of = lambda j: j // 4
            self.max_dist = A_WINDOW - 1
            sl = np.repeat(np.asarray(ALIBI[:8], np.float32), HEAD_DIM)[None, None, :]
        else:
            self.P, self.qw, self.hps, self.kvw = 2 * d, 256, 4, 256
            self.qcol = lambda p: lax.div(p, 2) * 9 + 3 + lax.rem(p, 2)
            self.kcol = lambda p: lax.div(p, 2) * 9 + 5 + lax.rem(p, 2)
            self.vcol = lambda p: lax.div(p, 2) * 9 + 7 + lax.rem(p, 2)
            self.kv_of = lambda j: j
            self.max_dist = BLOCK
            per = np.repeat(np.asarray(ALIBI[8:], np.float32), HEAD_DIM).reshape(2, 1, 256)
            sl = np.tile(per, (d, 1, 1))
        self.slopes = jnp.asarray(sl, f32)


def _band_mask(n, d, max_dist):
    qi = lax.broadcasted_iota(jnp.int32, (BLOCK, 2 * BLOCK), 0)
    kj = lax.broadcasted_iota(jnp.int32, (BLOCK, 2 * BLOCK), 1)
    dist = BLOCK + qi - kj
    valid = (dist >= 0) & (dist <= max_dist) & ((n > 0) | (kj >= BLOCK))
    return valid, -(d * dist).astype(f32)


def _rms64(x, w):
    r = lax.rsqrt(jnp.mean(x * x, axis=-1, keepdims=True) + EPS)
    xh = x * r
    return xh * w, xh, r


def _rms64_bwd(dy, xh, r, w):
    t = dy * w
    dw = jnp.sum(jnp.sum(dy * xh, axis=0), axis=0, keepdims=True)
    return r * (t - xh * jnp.mean(t * xh, axis=-1, keepdims=True)), dw


def _heads64(x, heads):
    return jnp.stack([x[:, h * 64:(h + 1) * 64] for h in heads])


def attn_fwd(hv, band, wq, wk, name):
    M = hv.shape[0]
    nb = M // BLOCK
    P, qw, hps = band.P, band.qw, band.hps
    d, max_dist, kv_of = band.d, band.max_dist, band.kv_of
    kv_heads = sorted({kv_of(j) for j in range(hps)})
    kv_pos = {h: i for i, h in enumerate(kv_heads)}
    gqa = len(kv_heads) != hps

    def body(q_ref, kp_ref, kc_ref, vp_ref, vc_ref, sl_ref, wq_ref, wk_ref, o_ref, lse_ref):
        n = pl.program_id(1)
        valid, negd = _band_mask(n, d, max_dist)
        kblk = jnp.concatenate([kp_ref[...], kc_ref[...]], axis=0)
        vblk = jnp.concatenate([vp_ref[...], vc_ref[...]], axis=0)
        q = _heads64(q_ref, range(hps))
        kn = _rms64(_heads64(kblk, kv_heads), wk_ref[...])[0].astype(bf16)
        v = _heads64(vblk, kv_heads).astype(bf16)
        kn_q = jnp.stack([kn[kv_pos[kv_of(j)]] for j in range(hps)]) if gqa else kn
        v_q = jnp.stack([v[kv_pos[kv_of(j)]] for j in range(hps)]) if gqa else v
        qn = _rms64(q, wq_ref[...])[0].astype(bf16)
        slope = jnp.stack([sl_ref[0, :, j * 64:j * 64 + 1] for j in range(hps)])
        s = _dot(qn, kn_q, BNT) * (HEAD_DIM ** -0.5) + slope * negd
        s = jnp.where(valid, s, NEG)
        m = jnp.max(s, axis=-1, keepdims=True)
        p = jnp.exp(s - m)
        l = jnp.sum(p, axis=-1, keepdims=True)
        o = _dot(p.astype(bf16), v_q, BNN) / l
        lse = m + jnp.log(l)
        for j in range(hps):
            o_ref[:, j * 64:(j + 1) * 64] = o[j]
            lse_ref[:, j * 64:(j + 1) * 64] = jnp.broadcast_to(lse[j], (BLOCK, 64))

    qcol, kcol, vcol, kvw = band.qcol, band.kcol, band.vcol, band.kvw
    in_specs = [
        pl.BlockSpec((BLOCK, qw), lambda p, n: (n, qcol(p))),
        pl.BlockSpec((BLOCK, kvw), lambda p, n: (jnp.maximum(n - 1, 0), kcol(p))),
        pl.BlockSpec((BLOCK, kvw), lambda p, n: (n, kcol(p))),
        pl.BlockSpec((BLOCK, kvw), lambda p, n: (jnp.maximum(n - 1, 0), vcol(p))),
        pl.BlockSpec((BLOCK, kvw), lambda p, n: (n, vcol(p))),
        pl.BlockSpec((1, 1, qw), lambda p, n: (p, 0, 0)),
        pl.BlockSpec((1, 64), lambda p, n: (0, 0)),
        pl.BlockSpec((1, 64), lambda p, n: (0, 0)),
    ]
    o_spec = pl.BlockSpec((BLOCK, qw), lambda p, n: (n, p))
    return pl.pallas_call(
        body, grid=(P, nb), in_specs=in_specs, out_specs=[o_spec, o_spec],
        out_shape=[jax.ShapeDtypeStruct((M, P * qw), f32)] * 2,
        compiler_params=_cp("parallel", "parallel"), name=name,
    )(hv, hv, hv, hv, hv, band.slopes, wq, wk)


def attn_bwd(hv, band, wq, wk, o, lse, do, dlse, dw0, name):
    M = hv.shape[0]
    nb = M // BLOCK
    P, qw, hps = band.P, band.qw, band.hps
    d, max_dist, kv_of = band.d, band.max_dist, band.kv_of
    kv_heads = sorted({kv_of(j) for j in range(hps)})
    kv_pos = {h: i for i, h in enumerate(kv_heads)}
    gqa = len(kv_heads) != hps

    def body(q_ref, kp_ref, kc_ref, vp_ref, vc_ref, sl_ref, wq_ref, wk_ref, o_ref, lse_ref, do_ref, dlse_ref,
             dwq0_ref, dwk0_ref, dq_ref, dk_ref, dv_ref, dwq_ref, dwk_ref, ck, cv):
        pp = pl.program_id(0)
        n = pl.program_id(1)

        @pl.when((pp == 0) & (n == 0))
        def _():
            dwq_ref[...] = dwq0_ref[...]
            dwk_ref[...] = dwk0_ref[...]

        @pl.when(n == 0)
        def _():
            ck[...] = jnp.zeros_like(ck)
            cv[...] = jnp.zeros_like(cv)

        @pl.when(n < nb)
        def _():
            valid, negd = _band_mask(n, d, max_dist)
            kblk = jnp.concatenate([kp_ref[...], kc_ref[...]], axis=0)
            vblk = jnp.concatenate([vp_ref[...], vc_ref[...]], axis=0)
            wqv, wkv = wq_ref[...], wk_ref[...]
            hs = range(hps)
            kn_f, kh, rk = _rms64(_heads64(kblk, kv_heads), wkv)
            kn = kn_f.astype(bf16)
            v = _heads64(vblk, kv_heads).astype(bf16)
            kn_q = jnp.stack([kn[kv_pos[kv_of(j)]] for j in hs]) if gqa else kn
            v_q = jnp.stack([v[kv_pos[kv_of(j)]] for j in hs]) if gqa else v
            qn_f, qh, rq = _rms64(_heads64(q_ref, hs), wqv)
            qn = qn_f.astype(bf16)
            col = lambda ref: jnp.stack([ref[:, j * 64:j * 64 + 1] for j in hs])
            slope = jnp.stack([sl_ref[0, :, j * 64:j * 64 + 1] for j in hs])
            s = _dot(qn, kn_q, BNT) * (HEAD_DIM ** -0.5) + slope * negd
            p = jnp.where(valid, jnp.exp(s - col(lse_ref)), 0.0)
            do_h = _heads64(do_ref, hs)
            delta = jnp.sum(do_h * _heads64(o_ref, hs), axis=-1, keepdims=True)
            do_b = do_h.astype(bf16)
            dp = _dot(do_b, v_q, BNT)
            ds = (p * (dp - delta + col(dlse_ref))).astype(bf16)
            dqn = _dot(ds, kn_q, BNN) * (HEAD_DIM ** -0.5)
            dkn_q = _dot(ds, qn, BTN) * (HEAD_DIM ** -0.5)
            dv_q = _dot(p.astype(bf16), do_b, BTN)
            if gqa:
                grp = lambda t: jnp.stack([sum(t[j] for j in hs if kv_of(j) == h) for h in kv_heads])
                dkn_q, dv_q = grp(dkn_q), grp(dv_q)
            dq, dwq_acc = _rms64_bwd(dqn, qh, rq, wqv)
            for j in hs:
                dq_ref[:, j * 64:(j + 1) * 64] = dq[j]
            dwq_ref[...] += dwq_acc
            dk_h, dwk_acc = _rms64_bwd(dkn_q, kh, rk, wkv)
            dwk_ref[...] += dwk_acc
            dk_all = jnp.concatenate([dk_h[i] for i in range(len(kv_heads))], axis=1)
            dv_all = jnp.concatenate([dv_q[i] for i in range(len(kv_heads))], axis=1)
            dk_ref[...] = ck[...] + dk_all[:BLOCK]
            dv_ref[...] = cv[...] + dv_all[:BLOCK]
            ck[...] = dk_all[BLOCK:]
            cv[...] = dv_all[BLOCK:]

        @pl.when(n == nb)
        def _():
            dk_ref[...] = ck[...]
            dv_ref[...] = cv[...]

    qcol, kcol, vcol, kvw = band.qcol, band.kcol, band.vcol, band.kvw
    cl = lambda n: jnp.minimum(n, nb - 1)
    pv = lambda n: jnp.maximum(jnp.minimum(n, nb - 1) - 1, 0)
    o_in = pl.BlockSpec((BLOCK, qw), lambda p, n: (cl(n), p))
    in_specs = [
        pl.BlockSpec((BLOCK, qw), lambda p, n: (cl(n), qcol(p))),
        pl.BlockSpec((BLOCK, kvw), lambda p, n: (pv(n), kcol(p))),
        pl.BlockSpec((BLOCK, kvw), lambda p, n: (cl(n), kcol(p))),
        pl.BlockSpec((BLOCK, kvw), lambda p, n: (pv(n), vcol(p))),
        pl.BlockSpec((BLOCK, kvw), lambda p, n: (cl(n), vcol(p))),
        pl.BlockSpec((1, 1, qw), lambda p, n: (p, 0, 0)),
        pl.BlockSpec((1, 64), lambda p, n: (0, 0)),
        pl.BlockSpec((1, 64), lambda p, n: (0, 0)),
        o_in, o_in, o_in, o_in,
        pl.BlockSpec((1, 64), lambda p, n: (0, 0)),
        pl.BlockSpec((1, 64), lambda p, n: (0, 0)),
    ]
    kv_out = pl.BlockSpec((BLOCK, kvw), lambda p, n: (jnp.maximum(n - 1, 0), p))
    w_out = pl.BlockSpec((1, 64), lambda p, n: (0, 0))
    return pl.pallas_call(
        body, grid=(P, nb + 1), in_specs=in_specs,
        out_specs=[o_in, kv_out, kv_out, w_out, w_out],
        out_shape=[jax.ShapeDtypeStruct((M, P * qw), f32), jax.ShapeDtypeStruct((M, P * kvw), f32),
                   jax.ShapeDtypeStruct((M, P * kvw), f32), jax.ShapeDtypeStruct((1, 64), f32),
                   jax.ShapeDtypeStruct((1, 64), f32)],
        scratch_shapes=[pltpu.VMEM((BLOCK, kvw), f32), pltpu.VMEM((BLOCK, kvw), f32)],
        compiler_params=_cp("arbitrary", "arbitrary"), name=name,
    )(hv, hv, hv, hv, hv, band.slopes, wq, wk, o, lse, do, dlse, *dw0)


class _Plan:
    def __init__(self, dilation, group_a, nq):
        self.d, self.nq = dilation, nq
        if group_a:
            self.P, self.nkv = 1, 1
            self.q0, self.k0, self.v0 = 0, 4, 5
            self.kv_of = lambda j: j // 4
            self.max_dist = A_WINDOW - 1
            slopes = ALIBI[:8]
        else:
            self.P, self.nkv = 4 // nq, nq
            self.q0, self.k0, self.v0 = 6, 10, 14
            self.kv_of = lambda j: j
            self.max_dist = BLOCK
            slopes = ALIBI[8:]
        self.hps = 2 * nq
        sl = np.repeat(np.asarray(slopes, np.float32), HEAD_DIM).reshape(self.P, 1, self.hps * HEAD_DIM)
        self.slopes = jnp.asarray(sl, f32)


def _rows(r, d):
    return pl.ds(r, BLOCK, stride=d) if d > 1 else pl.ds(0, BLOCK)


def _pairs(refs, rows):
    parts = []
    for ref in refs:
        blk = ref[rows, :]
        parts += [blk[:, :HEAD_DIM], blk[:, HEAD_DIM:]]
    return jnp.stack(parts)


def _pairs2(prev_refs, cur_refs, rows):
    parts = []
    for pr, cr in zip(prev_refs, cur_refs):
        blk = jnp.concatenate([pr[rows, :], cr[rows, :]], axis=0)
        parts += [blk[:, :HEAD_DIM], blk[:, HEAD_DIM:]]
    return jnp.stack(parts)


def _lane_pair(t, i):
    return jnp.concatenate([t[2 * i], t[2 * i + 1]], axis=1)


def attn2_fwd(hin, plan, wq, wk, name):
    L = hin.shape[0]
    d, nq, nkv, hps, P = plan.d, plan.nq, plan.nkv, plan.hps, plan.P
    R = BLOCK * d
    nb = L // R
    kv_of, max_dist = plan.kv_of, plan.max_dist
    gqa = 2 * nkv != hps

    def body(*refs):
        q_refs = refs[:nq]
        kp, kc = refs[nq:nq + nkv], refs[nq + nkv:nq + 2 * nkv]
        vp, vc = refs[nq + 2 * nkv:nq + 3 * nkv], refs[nq + 3 * nkv:nq + 4 * nkv]
        sl_ref, wq_ref, wk_ref, o_ref, lse_ref = refs[nq + 4 * nkv:nq + 4 * nkv + 5]
        o_refs = refs[nq + 4 * nkv + 5:2 * nq + 4 * nkv + 5]
        lse_refs = refs[2 * nq + 4 * nkv + 5:]
        n = pl.program_id(1)
        valid, negd = _band_mask(n, d, max_dist)
        slope = jnp.stack([sl_ref[0, :, j * 64:j * 64 + 1] for j in range(hps)])
        wqv, wkv = wq_ref[...], wk_ref[...]

        def residue(r, carry):
            rows = _rows(r, d)
            kn = _rms64(_pairs2(kp, kc, rows), wkv)[0].astype(bf16)
            v = _pairs2(vp, vc, rows).astype(bf16)
            if gqa:
                kn = jnp.stack([kn[kv_of(j)] for j in range(hps)])
                v = jnp.stack([v[kv_of(j)] for j in range(hps)])
            qn = _rms64(_pairs(q_refs, rows), wqv)[0].astype(bf16)
            s = _dot(qn, kn, BNT) * (HEAD_DIM ** -0.5) + slope * negd
            s = jnp.where(valid, s, NEG)
            m = jnp.max(s, axis=-1, keepdims=True)
            p = jnp.exp(s - m)
            l = jnp.sum(p, axis=-1, keepdims=True)
            o = _dot(p.astype(bf16), v, BNN) / l
            lse = jnp.broadcast_to(m + jnp.log(l), (hps, BLOCK, HEAD_DIM))
            for i in range(nq):
                o_refs[i][rows, :] = _lane_pair(o, i)
                lse_refs[i][rows, :] = _lane_pair(lse, i)
            return carry

        lax.fori_loop(0, d, residue, 0)
        for i in range(nq):
            o_ref[:, i * 128:(i + 1) * 128] = o_refs[i][...]
            lse_ref[:, i * 128:(i + 1) * 128] = lse_refs[i][...]

    col = lambda c0, i: (lambda p, n: (n, c0 + p * nq + i))
    prv = lambda c0, i: (lambda p, n: (jnp.maximum(n - 1, 0), c0 + p * nq + i))
    blk = lambda f: pl.BlockSpec((R, 128), f)
    in_specs = [blk(col(plan.q0, i)) for i in range(nq)]
    in_specs += [blk(prv(plan.k0, i)) for i in range(nkv)] + [blk(col(plan.k0, i)) for i in range(nkv)]
    in_specs += [blk(prv(plan.v0, i)) for i in range(nkv)] + [blk(col(plan.v0, i)) for i in range(nkv)]
    in_specs += [pl.BlockSpec((1, 1, hps * 64), lambda p, n: (p, 0, 0)),
                 pl.BlockSpec((1, 64), lambda p, n: (0, 0)), pl.BlockSpec((1, 64), lambda p, n: (0, 0))]
    wide = pl.BlockSpec((R, 128 * nq), lambda p, n: (n, p))
    return pl.pallas_call(
        body, grid=(P, nb), in_specs=in_specs, out_specs=[wide, wide],
        out_shape=[jax.ShapeDtypeStruct((L, 512), f32)] * 2,
        scratch_shapes=[pltpu.VMEM((R, 128), f32)] * (2 * nq),
        compiler_params=_cp("parallel", "parallel"), name=name,
    )(*([hin] * (nq + 4 * nkv)), plan.slopes, wq, wk)


def attn2_bwd(hin, plan, wq, wk, o, lse, do, dlse, dw0, name):
    L = hin.shape[0]
    d, nq, nkv, hps, P = plan.d, plan.nq, plan.nkv, plan.hps, plan.P
    R = BLOCK * d
    nb = L // R
    kv_of, max_dist = plan.kv_of, plan.max_dist
    nkh = 2 * nkv
    gqa = nkh != hps
    n_in = nq + 4 * nkv + 3 + 4 * nq + 2

    def body(*refs):
        q_refs = refs[:nq]
        kp, kc = refs[nq:nq + nkv], refs[nq + nkv:nq + 2 * nkv]
        vp, vc = refs[nq + 2 * nkv:nq + 3 * nkv], refs[nq + 3 * nkv:nq + 4 * nkv]
        b = nq + 4 * nkv
        sl_ref, wq_ref, wk_ref = refs[b:b + 3]
        b += 3
        o_refs, lse_refs = refs[b:b + nq], refs[b + nq:b + 2 * nq]
        do_refs, dlse_refs = refs[b + 2 * nq:b + 3 * nq], refs[b + 3 * nq:b + 4 * nq]
        dwq0_ref, dwk0_ref = refs[b + 4 * nq:b + 4 * nq + 2]
        dq_ref, dk_ref, dv_ref, dwq_ref, dwk_ref = refs[n_in:n_in + 5]
        sc = refs[n_in + 5:]
        dq_s, dk_s, dv_s = sc[:nq], sc[nq:nq + nkv], sc[nq + nkv:nq + 2 * nkv]
        ck, cv = sc[nq + 2 * nkv:nq + 3 * nkv], sc[nq + 3 * nkv:]
        pp = pl.program_id(0)
        n = pl.program_id(1)

        @pl.when((pp == 0) & (n == 0))
        def _():
            dwq_ref[...] = dwq0_ref[...]
            dwk_ref[...] = dwk0_ref[...]

        @pl.when(n == 0)
        def _():
            for c in (*ck, *cv):
                c[...] = jnp.zeros_like(c)

        @pl.when(n < nb)
        def _():
            valid, negd = _band_mask(n, d, max_dist)
            slope = jnp.stack([sl_ref[0, :, j * 64:j * 64 + 1] for j in range(hps)])
            wqv, wkv = wq_ref[...], wk_ref[...]
            hs = range(hps)

            def residue(r, carry):
                rows = _rows(r, d)
                kn_f, kh, rk = _rms64(_pairs2(kp, kc, rows), wkv)
                kn = kn_f.astype(bf16)
                v = _pairs2(vp, vc, rows).astype(bf16)
                if gqa:
                    kn = jnp.stack([kn[kv_of(j)] for j in hs])
                    v = jnp.stack([v[kv_of(j)] for j in hs])
                qn_f, qh, rq = _rms64(_pairs(q_refs, rows), wqv)
                qn = qn_f.astype(bf16)
                s = _dot(qn, kn, BNT) * (HEAD_DIM ** -0.5) + slope * negd
                p = jnp.where(valid, jnp.exp(s - _pairs(lse_refs, rows)[:, :, :1]), 0.0)
                do_h = _pairs(do_refs, rows)
                delta = jnp.sum(do_h * _pairs(o_refs, rows), axis=-1, keepdims=True)
                do_b = do_h.astype(bf16)
                dp = _dot(do_b, v, BNT)
                ds = (p * (dp - delta + _pairs(dlse_refs, rows)[:, :, :1])).astype(bf16)
                dqn = _dot(ds, kn, BNN) * (HEAD_DIM ** -0.5)
                dkn = _dot(ds, qn, BTN) * (HEAD_DIM ** -0.5)
                dvv = _dot(p.astype(bf16), do_b, BTN)
                if gqa:
                    grp = lambda t: jnp.stack([sum(t[j] for j in hs if kv_of(j) == h) for h in range(nkh)])
                    dkn, dvv = grp(dkn), grp(dvv)
                dq, dwq = _rms64_bwd(dqn, qh, rq, wqv)
                dk, dwk = _rms64_bwd(dkn, kh, rk, wkv)
                for i in range(nq):
                    dq_s[i][rows, :] = _lane_pair(dq, i)
                for i in range(nkv):
                    dk_s[i][rows, :] = ck[i][rows, :] + _lane_pair(dk[:, :BLOCK], i)
                    dv_s[i][rows, :] = cv[i][rows, :] + _lane_pair(dvv[:, :BLOCK], i)
                    ck[i][rows, :] = _lane_pair(dk[:, BLOCK:], i)
                    cv[i][rows, :] = _lane_pair(dvv[:, BLOCK:], i)
                return carry[0] + dwq, carry[1] + dwk

            zero = jnp.zeros((1, HEAD_DIM), f32)
            dwq_a, dwk_a = lax.fori_loop(0, d, residue, (zero, zero))
            dwq_ref[...] += dwq_a
            dwk_ref[...] += dwk_a
            for i in range(nq):
                dq_ref[:, i * 128:(i + 1) * 128] = dq_s[i][...]
            for i in range(nkv):
                dk_ref[:, i * 128:(i + 1) * 128] = dk_s[i][...]
                dv_ref[:, i * 128:(i + 1) * 128] = dv_s[i][...]

        @pl.when(n == nb)
        def _():
            for i in range(nkv):
                dk_ref[:, i * 128:(i + 1) * 128] = ck[i][...]
                dv_ref[:, i * 128:(i + 1) * 128] = cv[i][...]

    cl = lambda n: jnp.minimum(n, nb - 1)
    pv = lambda n: jnp.maximum(jnp.minimum(n, nb - 1) - 1, 0)
    col = lambda c0, i: (lambda p, n: (cl(n), c0 + p * nq + i))
    prv = lambda c0, i: (lambda p, n: (pv(n), c0 + p * nq + i))
    blk = lambda f: pl.BlockSpec((R, 128), f)
    w64 = pl.BlockSpec((1, 64), lambda p, n: (0, 0))
    in_specs = [blk(col(plan.q0, i)) for i in range(nq)]
    in_specs += [blk(prv(plan.k0, i)) for i in range(nkv)] + [blk(col(plan.k0, i)) for i in range(nkv)]
    in_specs += [blk(prv(plan.v0, i)) for i in range(nkv)] + [blk(col(plan.v0, i)) for i in range(nkv)]
    in_specs += [pl.BlockSpec((1, 1, hps * 64), lambda p, n: (p, 0, 0)), w64, w64]
    in_specs += [blk(col(0, i)) for i in range(nq)] * 4 + [w64, w64]
    kvw = 128 * nkv
    out_specs = [pl.BlockSpec((R, 128 * nq), lambda p, n: (cl(n), p)),
                 pl.BlockSpec((R, kvw), lambda p, n: (jnp.maximum(n - 1, 0), p)),
                 pl.BlockSpec((R, kvw), lambda p, n: (jnp.maximum(n - 1, 0), p)), w64, w64]
    same = lambda a: [a] * nq
    return pl.pallas_call(
        body, grid=(P, nb + 1), in_specs=in_specs, out_specs=out_specs,
        out_shape=[jax.ShapeDtypeStruct((L, 512), f32), jax.ShapeDtypeStruct((L, kvw * P), f32),
                   jax.ShapeDtypeStruct((L, kvw * P), f32), jax.ShapeDtypeStruct((1, 64), f32),
                   jax.ShapeDtypeStruct((1, 64), f32)],
        scratch_shapes=[pltpu.VMEM((R, 128), f32)] * (nq + 4 * nkv),
        compiler_params=_cp("arbitrary", "arbitrary"), name=name,
    )(*([hin] * (nq + 4 * nkv)), plan.slopes, wq, wk, *same(o), *same(lse), *same(do), *same(dlse), *dw0)


def _head_sum(x):
    c = x.shape[1]
    r = lax.broadcasted_iota(jnp.int32, (c, c), 0) // HEAD_DIM
    q = lax.broadcasted_iota(jnp.int32, (c, c), 1) // HEAD_DIM
    return _dot(x, (r == q).astype(f32), precision=HI)


def attn_merge_fwd(oa, la, obs, lbs, sinkb, name):
    L = oa.shape[0]

    def fn(i, nb, oa_t, la_t, o1, o2, o3, l1, l2, l3, sk):
        ya = oa_t * jax.nn.sigmoid(la_t - sk)
        m = jnp.maximum(jnp.maximum(l1, l2), l3)
        e1, e2, e3 = jnp.exp(l1 - m), jnp.exp(l2 - m), jnp.exp(l3 - m)
        yb = (e1 * o1 + e2 * o2 + e3 * o3) / (e1 + e2 + e3)
        return jnp.concatenate([ya, yb], axis=1)

    rows = [(a, 0, 512, "cur") for a in (oa, la, *obs, *lbs)]
    return rowwise(fn, name=name, L=L, tm=_pick(L, 256, SUBLANE), rows=rows, consts=[sinkb], outs=[(1024, bf16)])


def attn_merge_bwd(dcat, oa, la, obs, lbs, sinkb, name):
    L = oa.shape[0]

    def fn(i, nb, da, db, oa_t, la_t, o1, o2, o3, l1, l2, l3, sk):
        keep = jax.nn.sigmoid(la_t - sk)
        dla = _head_sum(da * oa_t) * keep * (1.0 - keep)
        m = jnp.maximum(jnp.maximum(l1, l2), l3)
        e1, e2, e3 = jnp.exp(l1 - m), jnp.exp(l2 - m), jnp.exp(l3 - m)
        z = e1 + e2 + e3
        w1, w2, w3 = e1 / z, e2 / z, e3 / z
        g1, g2, g3 = _head_sum(db * o1), _head_sum(db * o2), _head_sum(db * o3)
        gm = w1 * g1 + w2 * g2 + w3 * g3
        return (da * keep, dla, w1 * db, w2 * db, w3 * db,
                w1 * (g1 - gm), w2 * (g2 - gm), w3 * (g3 - gm), -_colsum(dla))

    rows = [(dcat, 0, 512, "cur"), (dcat, 512, 512, "cur")] + [(a, 0, 512, "cur") for a in (oa, la, *obs, *lbs)]
    return rowwise(fn, name=name, L=L, tm=_pick(L, 256, SUBLANE), rows=rows, consts=[sinkb],
                   outs=[(512, f32)] * 8, sums=[(1, 512)])


def attn_assemble(dqa, dka, dva, dqs, dks, dvs, name):
    L = dqa.shape[0]

    def fn(i, nb, qa, ka, va, q1, q2, q3, k1, k2, k3, v1, v2, v3):
        return jnp.concatenate([qa, ka, va, q1 + q2 + q3, k1 + k2 + k3, v1 + v2 + v3], axis=1)

    rows = [(dqa, 0, 512, "cur"), (dka, 0, 128, "cur"), (dva, 0, 128, "cur")]
    rows += [(a, 0, 512, "cur") for a in (*dqs, *dks, *dvs)]
    return rowwise(fn, name=name, L=L, tm=_pick(L, 256, SUBLANE), rows=rows, outs=[(ATTN_IN, bf16)])


def attention_fwd(hin, wqa, wka, wqb, wkb, sinkb):
    oa, la = attn2_fwd(hin, _Plan(1, True, 4), wqa, wka, "attn_a_fwd")
    obs, lbs = [], []
    for _, d in B_BRANCHES:
        o, l = attn2_fwd(hin, _Plan(d, False, 2), wqb, wkb, f"attn_b{d}_fwd")
        obs.append(o)
        lbs.append(l)
    ocat = attn_merge_fwd(oa, la, obs, lbs, sinkb, "attn_merge_fwd")
    return ocat, (oa, la, obs, lbs)


def attention_bwd(hin, wqa, wka, wqb, wkb, sinkb, saved, dcat):
    oa, la, obs, lbs = saved
    res = attn_merge_bwd(dcat, oa, la, obs, lbs, sinkb, "attn_merge_bwd")
    doa, dla, dos, dls, dsink = res[0], res[1], res[2:5], res[5:8], res[8]
    zero = jnp.zeros((1, 64), f32)
    dqa, dka, dva, dwqa, dwka = attn2_bwd(hin, _Plan(1, True, 4), wqa, wka, oa, la, doa, dla, (zero, zero), "attn_a_bwd")
    dqs, dks, dvs = [], [], []
    dwqb = dwkb = zero
    for g, (_, d) in enumerate(B_BRANCHES):
        dq, dk, dv, dwqb, dwkb = attn2_bwd(hin, _Plan(d, False, 2 if d < 16 else 1), wqb, wkb, obs[g], lbs[g],
                                           dos[g], dls[g], (dwqb, dwkb), f"attn_b{d}_bwd")
        dqs.append(dq)
        dks.append(dk)
        dvs.append(dv)
    dhin = attn_assemble(dqa, dka, dva, dqs, dks, dvs, "attn_assemble")
    return dhin, dwqa, dwka, dwqb, dwkb, dsink


NS = S5_GROUPS * S5_STATE


def _s5_param_fn(lr, li, ldt):
    dt = jnp.exp(ldt)
    mag, ang = jnp.exp(lr * dt), li * dt
    ab_re, ab_im = mag * jnp.cos(ang), mag * jnp.sin(ang)
    nr, ni = ab_re - 1.0, ab_im
    den = lr * lr + li * li
    return ab_re, ab_im, (nr * lr + ni * li) / den, (ni * lr - nr * li) / den


def s5_params_fwd(lr, li, ldt):
    def body(lr_ref, li_ref, ldt_ref, *outs):
        for o_ref, o in zip(outs, _s5_param_fn(lr_ref[...], li_ref[...], ldt_ref[...])):
            o_ref[...] = o

    return pl.pallas_call(body, out_shape=[jax.ShapeDtypeStruct(lr.shape, f32)] * 4, name="s5_params_fwd")(lr, li, ldt)


def s5_params_bwd(lr, li, ldt, cts):
    def body(lr_ref, li_ref, ldt_ref, c0, c1, c2, c3, dlr, dli, dldt):
        _, vjp = jax.vjp(_s5_param_fn, lr_ref[...], li_ref[...], ldt_ref[...])
        a, b, c = vjp((c0[...], c1[...], c2[...], c3[...]))
        dlr[...] = a
        dli[...] = b
        dldt[...] = c

    return pl.pallas_call(
        body, out_shape=[jax.ShapeDtypeStruct(lr.shape, f32), jax.ShapeDtypeStruct(li.shape, f32),
                         jax.ShapeDtypeStruct(ldt.shape, f32)], name="s5_params_bwd")(lr, li, ldt, *cts)


def _cmul(ar, ai, br, bi):
    return ar * br - ai * bi, ar * bi + ai * br


def s5_scan(z, ab_re, ab_im, f_re, f_im, *, reverse, name):
    L = z.shape[0]
    tm = _pick(L, 256, SUBLANE)
    nb = L // tm
    ng = tm // SUBLANE
    use_f = f_re is not None
    consts = [ab_re, ab_im] + ([f_re, f_im] if use_f else [])

    def body(*refs):
        z_ref = refs[0]
        c_refs = refs[1:1 + len(consts)]
        x_ref, car = refs[1 + len(consts)], refs[2 + len(consts)]
        i = pl.program_id(0)

        @pl.when(i == 0)
        def _():
            car[...] = jnp.zeros_like(car)

        a1 = (c_refs[0][...], c_refs[1][...])
        a2 = _cmul(*a1, *a1)
        a3 = _cmul(*a2, *a1)
        a4 = _cmul(*a2, *a2)
        pw = [a1, a2, a3, a4, _cmul(*a4, *a1), _cmul(*a4, *a2), _cmul(*a4, *a3), _cmul(*a4, *a4)]
        if reverse:
            pw = pw[::-1]
        pw_re = _stack_rows([p[0] for p in pw])
        pw_im = _stack_rows([p[1] for p in pw])
        ridx = lax.broadcasted_iota(jnp.int32, (SUBLANE, NS), 0)
        if use_f:
            fr, fi = c_refs[2][...], c_refs[3][...]

        def group(s, carry):
            cr, ci = carry
            g = (ng - 1 - s) if reverse else s
            r0 = pl.multiple_of(g * SUBLANE, SUBLANE)
            xr = z_ref[pl.ds(r0, SUBLANE), 0:NS]
            xi = z_ref[pl.ds(r0, SUBLANE), NS:2 * NS]
            if use_f:
                xr, xi = _cmul(fr, fi, xr, xi)
            for sft, (pr, pi) in ((1, a1), (2, a2), (4, a4)):
                if reverse:
                    keep = ridx < SUBLANE - sft
                    sr = jnp.where(keep, pltpu.roll(xr, SUBLANE - sft, 0), 0.0)
                    si = jnp.where(keep, pltpu.roll(xi, SUBLANE - sft, 0), 0.0)
                else:
                    keep = ridx >= sft
                    sr = jnp.where(keep, pltpu.roll(xr, sft, 0), 0.0)
                    si = jnp.where(keep, pltpu.roll(xi, sft, 0), 0.0)
                tr, ti = _cmul(pr, pi, sr, si)
                xr, xi = xr + tr, xi + ti
            tr, ti = _cmul(pw_re, pw_im, cr, ci)
            xr, xi = xr + tr, xi + ti
            x_ref[pl.ds(r0, SUBLANE), 0:NS] = xr
            x_ref[pl.ds(r0, SUBLANE), NS:2 * NS] = xi
            row = 0 if reverse else SUBLANE - 1
            return xr[row:row + 1, :], xi[row:row + 1, :]

        cr, ci = lax.fori_loop(0, ng, group, (car[0:1, 0:NS], car[0:1, NS:2 * NS]))
        car[0:1, 0:NS] = cr
        car[0:1, NS:2 * NS] = ci

    blk = (lambda i: (nb - 1 - i, 0)) if reverse else (lambda i: (i, 0))
    return pl.pallas_call(
        body, grid=(nb,),
        in_specs=[pl.BlockSpec((tm, 2 * NS), blk)] + [pl.BlockSpec((1, NS), lambda i: (0, 0))] * len(consts),
        out_specs=pl.BlockSpec((tm, 2 * NS), blk),
        out_shape=jax.ShapeDtypeStruct((L, 2 * NS), f32),
        scratch_shapes=[pltpu.VMEM((SUBLANE, 2 * NS), f32)],
        compiler_params=_cp("arbitrary"), name=name,
    )(z, *consts)


def _s5_post_fn(ypre, u, dvec, gw, gb):
    y = ypre + dvec * u
    g = jax.nn.gelu(y)
    z = _dot(g.astype(bf16), gw.astype(bf16)) + gb
    return g * jax.nn.sigmoid(z)


def s5_post_fwd(ypre, hin, dvec, gw, gb):
    L = ypre.shape[0]

    def fn(i, nb, yt, ut, dv, gwv, gbv):
        return _s5_post_fn(yt, ut, dv, gwv, gbv)

    return rowwise(fn, name="s5_post_fwd", L=L, tm=_pick(L, 512, SUBLANE),
                   rows=[(ypre, 0, S5_WIDTH, "cur"), (hin, 3072, S5_WIDTH, "cur")],
                   consts=[dvec, gw, gb], outs=[(S5_WIDTH, f32)])


def s5_post_bwd(ypre, hin, dvec, gw, gb, dycat):
    L = ypre.shape[0]

    def fn(i, nb, yt, ut, dyt, dv, gwv, gbv):
        _, vjp = jax.vjp(_s5_post_fn, yt, ut, dv, gwv, gbv)
        return vjp(dyt)

    return rowwise(fn, name="s5_post_bwd", L=L, tm=_pick(L, 512, SUBLANE),
                   rows=[(ypre, 0, S5_WIDTH, "cur"), (hin, 3072, S5_WIDTH, "cur"), (dycat, 0, S5_WIDTH, "cur")],
                   consts=[dvec, gw, gb], outs=[(S5_WIDTH, f32)] * 2,
                   sums=[(1, S5_WIDTH), (S5_WIDTH, S5_WIDTH), (1, S5_WIDTH)])


def s5_acc(G, X, bu, f_re, f_im):
    L = G.shape[0]

    def fn(i, nb, g, x, b, xp8, fr, fi):
        gr, gi = g[:, :NS], g[:, NS:]
        xp = _shift_down(x, xp8 * (i > 0).astype(f32), 1)
        xr, xi = xp[:, :NS], xp[:, NS:]
        br, bi = b[:, :NS], b[:, NS:]
        dbu = jnp.concatenate([fr * gr + fi * gi, fr * gi - fi * gr], axis=1)
        return (dbu, _colsum(xr * gr + xi * gi), _colsum(xr * gi - xi * gr),
                _colsum(br * gr + bi * gi), _colsum(br * gi - bi * gr))

    return rowwise(fn, name="s5_acc", L=L, tm=_pick(L, 256, SUBLANE),
                   rows=[(G, 0, 2 * NS, "cur"), (X, 0, 2 * NS, "cur"), (bu, 0, 2 * NS, "cur"), (X, 0, 2 * NS, "prev")],
                   consts=[f_re, f_im], outs=[(2 * NS, bf16)], sums=[(1, NS)] * 4)


def _s5_blockdiag(b_re, b_im, c_re, c_im):
    eye = jnp.eye(S5_GROUPS, dtype=f32)
    bb = lambda b: jnp.einsum("gpi,gh->gihp", b, eye).reshape(S5_WIDTH, NS)
    cc = lambda c: jnp.einsum("gip,gh->gphi", c, eye).reshape(NS, S5_WIDTH)
    return jnp.concatenate([bb(b_re), bb(b_im)], axis=1), jnp.concatenate([cc(c_re), -cc(c_im)], axis=0)


def _s5_blockdiag_grads(dB, dC):
    gb = lambda m: jnp.einsum("gigp->gpi", m.reshape(S5_GROUPS, S5_GROUP, S5_GROUPS, S5_STATE))
    gc = lambda m: jnp.einsum("gpgi->gip", m.reshape(S5_GROUPS, S5_STATE, S5_GROUPS, S5_GROUP))
    return gb(dB[:, :NS]), gb(dB[:, NS:]), gc(dC[:NS]), -gc(dC[NS:])


def s5_fwd(hin, prm):
    ab_re, ab_im, f_re, f_im = s5_params_fwd(prm["lr"], prm["li"], prm["ldt"])
    flat = lambda a: a.reshape(1, NS)
    ab_re, ab_im, f_re, f_im = flat(ab_re), flat(ab_im), flat(f_re), flat(f_im)
    Bblk, Cblk = _s5_blockdiag(prm["b_re"], prm["b_im"], prm["c_re"], prm["c_im"])
    bu = mm(hin, Bblk, name="s5_bu", a_win=(3072, S5_WIDTH))
    X = s5_scan(bu, ab_re, ab_im, f_re, f_im, reverse=False, name="s5_scan_fwd")
    ypre = mm(X, Cblk, name="s5_y")
    yc = s5_post_fwd(ypre, hin, prm["d"], prm["gw"], prm["gb"])
    return yc, (ab_re, ab_im, f_re, f_im, Bblk, Cblk, bu, X, ypre)


def s5_bwd(hin, prm, saved, dycat):
    ab_re, ab_im, f_re, f_im, Bblk, Cblk, bu, X, ypre = saved
    dypre, du_skip, dd, dgw, dgb = s5_post_bwd(ypre, hin, prm["d"], prm["gw"], prm["gb"], dycat)
    dX = mm(dypre, Cblk, tb=True, name="s5_dx")
    dC = mm(X, dypre, ta=True, name="s5_dc")
    G = s5_scan(dX, ab_re, -ab_im, None, None, reverse=True, name="s5_scan_bwd")
    dbu, dar, dai, dfr, dfi = s5_acc(G, X, bu, f_re, f_im)
    dB = mm(hin, dbu, ta=True, a_win=(3072, S5_WIDTH), name="s5_db")
    du_b = mm(dbu, Bblk, tb=True, name="s5_du")
    sh = prm["lr"].shape
    dlr, dli, dldt = s5_params_bwd(prm["lr"], prm["li"], prm["ldt"],
                                   [a.reshape(sh) for a in (dar, dai, dfr, dfi)])
    db_re, db_im, dc_re, dc_im = _s5_blockdiag_grads(dB, dC)
    grads = dict(lr=dlr, li=dli, ldt=dldt, b_re=db_re, b_im=db_im, c_re=dc_re, c_im=dc_im, d=dd, gw=dgw, gb=dgb)
    return du_skip, du_b, grads


DN_W = DN_HEADS * DN_DK
QKV_W = 3 * DN_W


def _softplus(x):
    return jnp.maximum(x, 0.0) + jnp.log(1.0 + jnp.exp(-jnp.abs(x)))


def _dn_pre(c, ab, alog, dtb):
    s = _silu(c)
    parts = []
    for h in range(2 * DN_HEADS):
        sh = s[:, h * 128:(h + 1) * 128]
        scale = DN_DK ** -0.5 if h < DN_HEADS else 1.0
        parts.append(sh * (lax.rsqrt(jnp.sum(sh * sh, axis=-1, keepdims=True) + EPS) * scale))
    parts.append(s[:, 2 * DN_W:])
    g = -jnp.exp(alog) * _softplus(ab[:, :128] + dtb)
    beta = jax.nn.sigmoid(ab[:, 128:])
    return jnp.concatenate(parts, axis=1), jnp.concatenate([g, beta], axis=1)


def _dn_pre_bwd(c, ab, alog, dtb, dqkv, dgb):
    sg = jax.nn.sigmoid(c)
    s = c * sg
    parts = []
    for h in range(2 * DN_HEADS):
        sh = s[:, h * 128:(h + 1) * 128]
        dy = dqkv[:, h * 128:(h + 1) * 128]
        scale = DN_DK ** -0.5 if h < DN_HEADS else 1.0
        r = lax.rsqrt(jnp.sum(sh * sh, axis=-1, keepdims=True) + EPS)
        parts.append(scale * r * (dy - sh * (r * r) * jnp.sum(dy * sh, axis=-1, keepdims=True)))
    parts.append(dqkv[:, 2 * DN_W:])
    dc = jnp.concatenate(parts, axis=1) * (sg * (1.0 + c * (1.0 - sg)))
    pre = ab[:, :128] + dtb
    ea = jnp.exp(alog)
    dg = dgb[:, :128]
    da = dg * (-ea) * jax.nn.sigmoid(pre)
    dalog = _colsum(dg * (-ea) * _softplus(pre))
    beta = jax.nn.sigmoid(ab[:, 128:])
    db = dgb[:, 128:] * beta * (1.0 - beta)
    return dc, jnp.concatenate([da, db], axis=1), dalog, _colsum(da)


def dn_pre_fwd(hin, conv_w, alog, dtb):
    L = hin.shape[0]

    def fn(i, nb, x, ab, p8, w, al, db):
        c = _conv_causal(x, p8 * (i > 0).astype(f32), w)
        return _dn_pre(c, ab, al, db)

    return rowwise(fn, name="dn_pre_fwd", L=L, tm=_pick(L, 256, SUBLANE),
                   rows=[(hin, 0, QKV_W, "cur"), (hin, 3328, 256, "cur"), (hin, 0, QKV_W, "prev")],
                   consts=[conv_w, alog, dtb], outs=[(QKV_W, f32), (256, f32)])


def dn_pre_bwd(hin, conv_w, alog, dtb, dqkv3, dg, dbeta):
    L = hin.shape[0]

    def fn(i, nb, x, ab, dq, dk, dv, dgt, dbt, p8, w, al, db):
        c = _conv_causal(x, p8 * (i > 0).astype(f32), w)
        return _dn_pre_bwd(c, ab, al, db, jnp.concatenate([dq, dk, dv], axis=1), jnp.concatenate([dgt, dbt], axis=1))

    rows = [(hin, 0, QKV_W, "cur"), (hin, 3328, 256, "cur")] + [(a, 0, DN_W, "cur") for a in dqkv3]
    rows += [(dg, 0, 128, "cur"), (dbeta, 0, 128, "cur"), (hin, 0, QKV_W, "prev")]
    return rowwise(fn, name="dn_pre_bwd", L=L, tm=_pick(L, 128, SUBLANE), rows=rows,
                   consts=[conv_w, alog, dtb], outs=[(QKV_W, f32), (256, f32)], sums=[(1, 128), (1, 128)])


def _split(a):
    hi = a.astype(bf16)
    return hi, (a - hi.astype(f32)).astype(bf16)


def _dot3_raw(a, b, dims):
    ah, al = _split(a)
    bh, bl = _split(b)
    return _dot(ah, bh, dims) + (_dot(ah, bl, dims) + _dot(al, bh, dims))


@functools.partial(jax.custom_vjp, nondiff_argnums=(2,))
def _dot3(a, b, dims=NN):
    return _dot3_raw(a, b, dims)


def _dot3_fwd(a, b, dims):
    return _dot3_raw(a, b, dims), (a, b)


BNN = (((2,), (1,)), ((0,), (0,)))
BNT = (((2,), (2,)), ((0,), (0,)))
BTN = (((1,), (1,)), ((0,), (0,)))


def _dot3_bwd(dims, res, g):
    a, b = res
    nn, nt, tn = (BNN, BNT, BTN) if dims[1][0] else (NN, NT, TN)
    if dims == nn:
        return _dot3_raw(g, b, nt), _dot3_raw(a, g, tn)
    if dims == nt:
        return _dot3_raw(g, b, nn), _dot3_raw(g, a, tn)
    assert dims == tn
    return _dot3_raw(b, g, nt), _dot3_raw(a, g, nn)


_dot3.defvjp(_dot3_fwd, _dot3_bwd)


def _dn_chunk(q, k, v, gcol, bcol, S):
    C = q.shape[1]
    r = lax.broadcasted_iota(jnp.int32, (C, C), 0)
    c = lax.broadcasted_iota(jnp.int32, (C, C), 1)
    tril = (r >= c).astype(f32)
    strict = (r > c).astype(f32)
    eye = (r == c).astype(f32)
    hd = _dot3
    grow = jnp.sum(eye * gcol, axis=1, keepdims=True)
    Gcol = jnp.sum(tril * grow, axis=2, keepdims=True)
    Grow = jnp.sum(eye * Gcol, axis=1, keepdims=True)
    gamma = jnp.exp((Gcol - Grow) * tril) * tril
    nmat = strict * bcol * hd(k, k, BNT) * gamma
    T = eye - nmat
    Pw = hd(nmat, nmat, BNN)
    for step in range(5):
        T = T + hd(T, Pw, BNN)
        if step < 4:
            Pw = hd(Pw, Pw, BNN)
    eG = jnp.exp(Gcol)
    u = hd(T, bcol * v, BNN)
    w = hd(T, (bcol * eG) * k, BNN)
    qk = hd(q, k, BNT) * gamma
    vnew = u - hd(w, S, BNN)
    o = hd(q * eG, S, BNN) + hd(qk, vnew, BNN)
    Glast = jnp.sum(gcol, axis=1, keepdims=True)
    S2 = S * jnp.exp(Glast) + hd(k * jnp.exp(Glast - Gcol), vnew, BTN)
    return o, S2


def _heads(x_ref):
    return jnp.stack([x_ref[:, h * 128:(h + 1) * 128] for h in range(DN_HEADS)])


def _head_cols(g_ref):
    return jnp.stack([g_ref[:, h:h + 1] for h in range(DN_HEADS)])


def dn_chunks_fwd(qkvn, gb):
    L = qkvn.shape[0]
    C = DN_CHUNK
    nc = L // C

    def body(q_ref, k_ref, v_ref, g_ref, b_ref, o_ref, sin_ref, S):
        n = pl.program_id(0)

        @pl.when(n == 0)
        def _():
            S[...] = jnp.zeros_like(S)

        s_in = S[...]
        sin_ref[...] = s_in
        o, s2 = _dn_chunk(_heads(q_ref), _heads(k_ref), _heads(v_ref), _head_cols(g_ref), _head_cols(b_ref), s_in)
        for h in range(DN_HEADS):
            o_ref[:, h * 128:(h + 1) * 128] = o[h]
        S[...] = s2

    blk = lambda j: pl.BlockSpec((C, DN_W), lambda n, j=j: (n, j))
    gblk = lambda j: pl.BlockSpec((C, 128), lambda n, j=j: (n, j))
    return pl.pallas_call(
        body, grid=(nc,),
        in_specs=[blk(0), blk(1), blk(2), gblk(0), gblk(1)],
        out_specs=[pl.BlockSpec((C, DN_W), lambda n: (n, 0)),
                   pl.BlockSpec((DN_HEADS, None, 128, 128), lambda n: (0, n, 0, 0))],
        out_shape=[jax.ShapeDtypeStruct((L, DN_W), f32), jax.ShapeDtypeStruct((DN_HEADS, nc, 128, 128), f32)],
        scratch_shapes=[pltpu.VMEM((DN_HEADS, 128, 128), f32)],
        compiler_params=_cp("arbitrary"), name="dn_chunks_fwd",
    )(qkvn, qkvn, qkvn, gb, gb)


def dn_chunks_bwd(qkvn, gb, s_in, do):
    L = qkvn.shape[0]
    C = DN_CHUNK
    nc = L // C

    def body(q_ref, k_ref, v_ref, g_ref, b_ref, sin_ref, do_ref, dq_ref, dk_ref, dv_ref, dg_ref, db_ref, dS):
        n = pl.program_id(0)

        @pl.when(n == 0)
        def _():
            dS[...] = jnp.zeros_like(dS)

        args = (_heads(q_ref), _heads(k_ref), _heads(v_ref), _head_cols(g_ref), _head_cols(b_ref), sin_ref[...])
        _, vjp = jax.vjp(_dn_chunk, *args)
        dq, dk, dv, dg, db, ds = vjp((_heads(do_ref), dS[...]))
        lane = lax.broadcasted_iota(jnp.int32, (C, 128), 1)
        dg_all = jnp.zeros((C, 128), f32)
        db_all = jnp.zeros((C, 128), f32)
        for h in range(DN_HEADS):
            sl = slice(h * 128, (h + 1) * 128)
            dq_ref[:, sl] = dq[h]
            dk_ref[:, sl] = dk[h]
            dv_ref[:, sl] = dv[h]
            dg_all = dg_all + jnp.where(lane == h, dg[h], 0.0)
            db_all = db_all + jnp.where(lane == h, db[h], 0.0)
        dS[...] = ds
        dg_ref[...] = dg_all
        db_ref[...] = db_all

    rv = lambda n: nc - 1 - n
    blk = lambda j: pl.BlockSpec((C, DN_W), lambda n, j=j: (rv(n), j))
    gblk = lambda j: pl.BlockSpec((C, 128), lambda n, j=j: (rv(n), j))
    oblk = pl.BlockSpec((C, DN_W), lambda n: (rv(n), 0))
    gout = pl.BlockSpec((C, 128), lambda n: (rv(n), 0))
    return pl.pallas_call(
        body, grid=(nc,),
        in_specs=[blk(0), blk(1), blk(2), gblk(0), gblk(1),
                  pl.BlockSpec((DN_HEADS, None, 128, 128), lambda n: (0, rv(n), 0, 0)), oblk],
        out_specs=[oblk] * 3 + [gout] * 2,
        out_shape=[jax.ShapeDtypeStruct((L, DN_W), f32)] * 3 + [jax.ShapeDtypeStruct((L, 128), f32)] * 2,
        scratch_shapes=[pltpu.VMEM((DN_HEADS, 128, 128), f32)],
        compiler_params=_cp("arbitrary"), name="dn_chunks_bwd",
    )(qkvn, qkvn, qkvn, gb, gb, s_in, do)


def _dn_post(o, z, w):
    parts = []
    for h in range(DN_HEADS):
        oh = o[:, h * 128:(h + 1) * 128]
        r = lax.rsqrt(jnp.mean(oh * oh, axis=-1, keepdims=True) + EPS)
        parts.append(oh * r * w)
    return jnp.concatenate(parts, axis=1) * _silu(z)


def dn_post_fwd(o, hin, yc, onorm):
    L = o.shape[0]

    def fn(i, nb, ot, zt, yct, w):
        return jnp.concatenate([yct, _dn_post(ot, zt, w)], axis=1)

    return rowwise(fn, name="dn_post_fwd", L=L, tm=_pick(L, 256, SUBLANE),
                   rows=[(o, 0, DN_W, "cur"), (hin, 2304, DN_W, "cur"), (yc, 0, S5_WIDTH, "cur")],
                   consts=[onorm], outs=[(1024, bf16)])


def dn_post_bwd(o, hin, onorm, dycat):
    L = o.shape[0]

    def fn(i, nb, ot, zt, d0, d1, d2, w):
        dy = jnp.concatenate([d0, d1, d2], axis=1)
        sg = jax.nn.sigmoid(zt)
        sz = zt * sg
        dos, dw = [], jnp.zeros((1, 128), f32)
        nrm = []
        for h in range(DN_HEADS):
            sl = slice(h * 128, (h + 1) * 128)
            oh = ot[:, sl]
            r = lax.rsqrt(jnp.mean(oh * oh, axis=-1, keepdims=True) + EPS)
            ohat = oh * r
            t = dy[:, sl] * sz[:, sl]
            dw = dw + _colsum(t * ohat)
            t = t * w
            dos.append(r * (t - ohat * jnp.mean(t * ohat, axis=-1, keepdims=True)))
            nrm.append(ohat * w)
        dz = dy * jnp.concatenate(nrm, axis=1) * (sg * (1.0 + zt * (1.0 - sg)))
        return jnp.concatenate(dos, axis=1), dz, dw

    rows = [(o, 0, DN_W, "cur"), (hin, 2304, DN_W, "cur")] + [(dycat, 256 * (1 + j), 256, "cur") for j in range(3)]
    return rowwise(fn, name="dn_post_bwd", L=L, tm=_pick(L, 256, SUBLANE), rows=rows,
                   consts=[onorm], outs=[(DN_W, f32), (DN_W, f32)], sums=[(1, 128)])


def conv_bwd_win(xarr, start, C, w, dc, name):
    L = xarr.shape[0]
    W = w.shape[0]

    def fn(i, nb, xt, dct, p8, n8, wv):
        dx = _conv_causal_bwd_x(dct, n8 * (i < nb - 1).astype(f32), wv)
        dw = _conv_causal_bwd_w(dct, xt, p8 * (i > 0).astype(f32), W)
        return dx, dw

    return rowwise(fn, name=name, L=L, tm=_pick(L, 128, SUBLANE),
                   rows=[(xarr, start, C, "cur"), (dc, 0, C, "cur"), (xarr, start, C, "prev"), (dc, 0, C, "next")],
                   consts=[w], outs=[(C, bf16)], sums=[(SUBLANE, C)])


def rec_assemble(dx_qkv, dz, du1, du2, dab):
    L = dz.shape[0]

    def fn(i, nb, a, b, c, d, e):
        return jnp.concatenate([a.astype(f32), b, c + d, e], axis=1)

    return rowwise(fn, name="rec_assemble", L=L, tm=_pick(L, 256, SUBLANE),
                   rows=[(dx_qkv, 0, QKV_W, "cur"), (dz, 0, DN_W, "cur"), (du1, 0, 256, "cur"),
                         (du2, 0, 256, "cur"), (dab, 0, 256, "cur")], outs=[(REC_PAD, bf16)])


def deltanet_fwd(hin, prm, yc):
    qkvn, gb = dn_pre_fwd(hin, prm["conv"], prm["alog"], prm["dtb"])
    o, s_in = dn_chunks_fwd(qkvn, gb)
    ycat = dn_post_fwd(o, hin, yc, prm["onorm"])
    return ycat, (qkvn, gb, o, s_in)


def deltanet_bwd(hin, prm, saved, dycat):
    qkvn, gb, o, s_in = saved
    do, dz, donorm = dn_post_bwd(o, hin, prm["onorm"], dycat)
    dq, dk, dv, dgH, dbH = dn_chunks_bwd(qkvn, gb, s_in, do)
    dc, dab, dalog, ddtb = dn_pre_bwd(hin, prm["conv"], prm["alog"], prm["dtb"], (dq, dk, dv), dgH, dbH)
    dx_qkv, dconv = conv_bwd_win(hin, 0, QKV_W, prm["conv"], dc, "dn_conv_bwd")
    return dx_qkv, dz, dab, dict(conv=dconv[:DN_CONV], alog=dalog, dtb=ddtb, onorm=donorm)


AXES = ("x", "y", "c")


def _collective(x, axes, mode, name, nchunk=1):
    k = len(axes)
    P = 2 ** k
    shape = x.shape if mode == "gather" else x.shape[1:]
    rows = shape[0] // nchunk
    assert rows * nchunk == shape[0]

    def body(x_ref, out_ref, send_sems, recv_sems, local_sems):
        co = {a: lax.axis_index(a) for a in AXES}
        me = 0
        for a in axes:
            me = me * 2 + co[a]

        def src(j, q):
            s = x_ref if mode == "gather" else x_ref.at[j]
            return s.at[pl.ds(q * rows, rows)]

        def dst(j, q):
            return out_ref.at[j].at[pl.ds(q * rows, rows)]

        locals_ = [pltpu.make_async_copy(src(me, q), dst(me, q), local_sems.at[q]) for q in range(nchunk)]
        for cp in locals_:
            cp.start()
        sends = []
        for m in range(1, P):
            tco = dict(co)
            t = 0
            for i, a in enumerate(axes):
                if (m >> (k - 1 - i)) & 1:
                    tco[a] = 1 - co[a]
                t = t * 2 + tco[a]
            dev = tuple(tco[a] for a in AXES)
            for q in range(nchunk):
                s = (m - 1) * nchunk + q
                cp = pltpu.make_async_remote_copy(src_ref=src(t, q), dst_ref=dst(me, q), send_sem=send_sems.at[s],
                                                  recv_sem=recv_sems.at[s], device_id=dev, device_id_type=MESH)
                cp.start()
                sends.append((cp, t, q, s, dev))
        for cp, t, q, s, dev in sends:
            pltpu.make_async_remote_copy(src_ref=src(t, q), dst_ref=dst(t, q), send_sem=send_sems.at[s],
                                         recv_sem=recv_sems.at[s], device_id=dev, device_id_type=MESH).wait_recv()
        for cp, *_ in sends:
            cp.wait_send()
        for cp in locals_:
            cp.wait()

    ns = (P - 1) * nchunk
    return pl.pallas_call(
        body,
        in_specs=[pl.BlockSpec(memory_space=pl.ANY)],
        out_specs=pl.BlockSpec(memory_space=pl.ANY),
        out_shape=jax.ShapeDtypeStruct((P,) + tuple(shape), x.dtype),
        scratch_shapes=[pltpu.SemaphoreType.DMA((ns,)), pltpu.SemaphoreType.DMA((ns,)),
                        pltpu.SemaphoreType.DMA((nchunk,))],
        name=name,
    )(x)


def all_gather(x, axes, name, nchunk=1):
    return _collective(x, axes, "gather", name, nchunk)


def exchange(x, axes, name, nchunk=1):
    return _collective(x, axes, "exchange", name, nchunk)


def sum_slots(x, name, out_dtype=f32):
    P, R, C = x.shape
    tr = _pick(R, 256, 2 * SUBLANE)

    def body(x_ref, o_ref):
        acc = x_ref[0].astype(f32)
        for j in range(1, P):
            acc = acc + x_ref[j].astype(f32)
        o_ref[...] = acc.astype(o_ref.dtype)

    return pl.pallas_call(
        body, grid=(R // tr,), in_specs=[pl.BlockSpec((P, tr, C), lambda i: (0, i, 0))],
        out_specs=pl.BlockSpec((tr, C), lambda i: (i, 0)), out_shape=jax.ShapeDtypeStruct((R, C), out_dtype),
        compiler_params=_cp("parallel"), name=name,
    )(x)


def _pack(arrs, width, row_mult, dtype):
    flat = jnp.concatenate([a.astype(dtype).reshape(-1) for a in arrs])
    unit = width * row_mult
    n = -(-flat.shape[0] // unit) * unit
    return jnp.pad(flat, (0, n - flat.shape[0])).reshape(n // width, width)


def _unpack(flat, shapes):
    flat = flat.reshape(-1)
    out, off = [], 0
    for s in shapes:
        n = int(np.prod(s))
        out.append(flat[off:off + n].reshape(s))
        off += n
    return out


def ada_fwd(c_all, ada_w):
    def body(c_ref, w_ref, o_ref):
        cond = _silu(c_ref[...])
        for l in range(ada_w.shape[0]):
            o_ref[l] = _dot(cond, w_ref[l], precision=HI)

    return pl.pallas_call(body, out_shape=jax.ShapeDtypeStruct((ada_w.shape[0], c_all.shape[0], ada_w.shape[2]), f32),
                          compiler_params=pltpu.CompilerParams(vmem_limit_bytes=VMEM_LIMIT), name="ada_fwd")(c_all, ada_w)


def ada_bwd(c_all, dmod):
    def body(c_ref, d_ref, o_ref):
        cond = _silu(c_ref[...])
        for l in range(dmod.shape[0]):
            o_ref[l] = _dot(cond, d_ref[l], TN, precision=HI)

    return pl.pallas_call(body, out_shape=jax.ShapeDtypeStruct((dmod.shape[0], c_all.shape[1], dmod.shape[2]), f32),
                          compiler_params=pltpu.CompilerParams(vmem_limit_bytes=VMEM_LIMIT), name="ada_bwd")(c_all, dmod)


def loss_fwd_bwd(y, target):
    L, D = y.shape

    def fn(i, nb, yt, tt):
        e = yt - tt
        return e * (1.0 / D), jnp.sum(jnp.sum(e * e, axis=1, keepdims=True), axis=0, keepdims=True)

    return rowwise(fn, name="loss", L=L, tm=_pick(L, 512, SUBLANE), rows=[(y, 0, D, "cur"), (target, 0, D, "cur")],
                   outs=[(D, f32)], sums=[(1, 1)])


def adamw(w, g, m, v, name):
    R, C = w.shape

    def fn(i, nb, wt, gt, mt, vt):
        m2 = ADAM_B1 * mt + (1.0 - ADAM_B1) * gt
        v2 = ADAM_B2 * vt + (1.0 - ADAM_B2) * (gt * gt)
        m_hat = m2 / (1.0 - ADAM_B1 ** ADAM_STEP)
        v_hat = v2 / (1.0 - ADAM_B2 ** ADAM_STEP)
        delta = -ADAM_LR * (m_hat / (jnp.sqrt(v_hat) + ADAM_EPS) + ADAM_WD * wt)
        return delta, m2, v2

    return rowwise(fn, name=name, L=R, tm=_pick(R, 256, SUBLANE), rows=[(a, 0, C, "cur") for a in (w, g, m, v)],
                   outs=[(C, f32)] * 3)


W_NAMES = ["ada_w", "ada_b", "norm_mix", "norm_ffn", "attn_w_in", "attn_q_norm_a", "attn_k_norm_a", "attn_q_norm_b",
           "attn_k_norm_b", "attn_sinks", "attn_w_out", "rec_w_in", "s5_lambda_re", "s5_lambda_im", "s5_log_dt",
           "s5_b_re", "s5_b_im", "s5_c_re", "s5_c_im", "s5_d", "s5_glu_w", "s5_glu_b", "dn_conv", "dn_a_log",
           "dn_dt_bias", "dn_out_norm", "rec_w_out", "ffn_w_up", "ffn_conv", "ffn_w_down"]
BIG = ["attn_w_in", "attn_w_out", "rec_w_in", "rec_w_out", "ffn_w_up", "ffn_w_down"]
SMALL_SHARDED = ["s5_d", "s5_glu_w", "s5_glu_b", "dn_conv", "ffn_conv"]
SMALL_REPL = [n for n in W_NAMES if n not in BIG and n not in SMALL_SHARDED and n != "ada_w"]
NSH = 4


def _unshard(g, name):
    ax = {"attn_w_in": 2, "attn_w_out": 1, "rec_w_in": 2, "rec_w_out": 1, "ffn_w_up": 2, "ffn_w_down": 1,
          "s5_d": 1, "s5_glu_w": 1, "s5_glu_b": 1, "dn_conv": 2, "ffn_conv": 2}[name]
    g = jnp.moveaxis(g, 0, ax)
    s = g.shape
    return g.reshape(s[:ax] + (s[ax] * s[ax + 1],) + s[ax + 2:])


def _to_shards(full, name):
    ax = {"attn_w_in": 2, "attn_w_out": 1, "rec_w_in": 2, "rec_w_out": 1, "ffn_w_up": 2, "ffn_w_down": 1,
          "s5_d": 1, "s5_glu_w": 1, "s5_glu_b": 1, "dn_conv": 2, "ffn_conv": 2}[name]
    s = full.shape
    g = full.reshape(s[:ax] + (NSH, s[ax] // NSH) + s[ax + 1:])
    return jnp.moveaxis(g, ax, 0)


def _rec_pad_cols(w):
    z6 = jnp.zeros(w.shape[:-1] + (122,), w.dtype)
    return jnp.concatenate([w[..., 256:3328], w[..., 0:256], w[..., 3328:3334], z6, w[..., 3334:3340], z6], axis=-1)


def _rec_unpad_cols(g):
    return jnp.concatenate([g[..., 3072:3328], g[..., 0:3072], g[..., 3328:3334], g[..., 3456:3462]], axis=-1)


def _ffn_fwd(x1, nf, sc, sh, gate, w_up, conv, w_dn, tag):
    h2 = modulate_fwd(x1, nf, sc, sh, f"{tag}_mod2_fwd")
    up = mm(h2, w_up, name=f"{tag}_ffn_up")
    act = ffn_act_fwd(up, conv, f"{tag}_ffn_act_fwd")
    f, x2 = mm(act, w_dn, name=f"{tag}_ffn_down", out_dtypes=(f32, f32), epi=_resid_epi, epi_mn=[x1], epi_n=[gate])
    return x2, (h2, up, act, f)


def _ffn_bwd(dx, x1, nf, sc, sh, gate, w_up, conv, w_dn, saved, tag):
    h2, up, act, f = saved
    df, dgate = resid_bwd(dx, f, gate, f"{tag}_res2_bwd")
    dact = mm(df, w_dn, tb=True, name=f"{tag}_ffn_dact")
    dw_dn = mm(act, df, ta=True, name=f"{tag}_ffn_dwdown")
    dup, dconv = ffn_act_conv_bwd(up, conv, dact, f"{tag}_ffn_act_conv_bwd")
    dw_up = mm(h2, dup, ta=True, name=f"{tag}_ffn_dwup")
    dh2 = mm(dup, w_up, tb=True, name=f"{tag}_ffn_dh")
    dx, dnf, dsc, dsh = modulate_bwd(x1, nf, sc, sh, dh2, dx, f"{tag}_mod2_bwd")
    return dx, dict(nf=dnf, sc=dsc, sh=dsh, gate=dgate, w_up=dw_up, conv=dconv[:FFN_CONV], w_dn=dw_dn)


def kernel(x, c, ada_w, ada_b, norm_mix, norm_ffn, attn_w_in, attn_q_norm_a, attn_k_norm_a, attn_q_norm_b, attn_k_norm_b, attn_sinks, attn_w_out, rec_w_in, s5_lambda_re, s5_lambda_im, s5_log_dt, s5_b_re, s5_b_im, s5_c_re, s5_c_im, s5_d, s5_glu_w, s5_glu_b, dn_conv, dn_a_log, dn_dt_bias, dn_out_norm, rec_w_out, ffn_w_up, ffn_conv, ffn_w_down, loss_target, m_ada_w, m_ada_b, m_norm_mix, m_norm_ffn, m_attn_w_in, m_attn_q_norm_a, m_attn_k_norm_a, m_attn_q_norm_b, m_attn_k_norm_b, m_attn_sinks, m_attn_w_out, m_rec_w_in, m_s5_lambda_re, m_s5_lambda_im, m_s5_log_dt, m_s5_b_re, m_s5_b_im, m_s5_c_re, m_s5_c_im, m_s5_d, m_s5_glu_w, m_s5_glu_b, m_dn_conv, m_dn_a_log, m_dn_dt_bias, m_dn_out_norm, m_rec_w_out, m_ffn_w_up, m_ffn_conv, m_ffn_w_down, v_ada_w, v_ada_b, v_norm_mix, v_norm_ffn, v_attn_w_in, v_attn_q_norm_a, v_attn_k_norm_a, v_attn_q_norm_b, v_attn_k_norm_b, v_attn_sinks, v_attn_w_out, v_rec_w_in, v_s5_lambda_re, v_s5_lambda_im, v_s5_log_dt, v_s5_b_re, v_s5_b_im, v_s5_c_re, v_s5_c_im, v_s5_d, v_s5_glu_w, v_s5_glu_b, v_dn_conv, v_dn_a_log, v_dn_dt_bias, v_dn_out_norm, v_rec_w_out, v_ffn_w_up, v_ffn_conv, v_ffn_w_down):
    args = (ada_w, ada_b, norm_mix, norm_ffn, attn_w_in, attn_q_norm_a, attn_k_norm_a, attn_q_norm_b, attn_k_norm_b, attn_sinks, attn_w_out, rec_w_in, s5_lambda_re, s5_lambda_im, s5_log_dt, s5_b_re, s5_b_im, s5_c_re, s5_c_im, s5_d, s5_glu_w, s5_glu_b, dn_conv, dn_a_log, dn_dt_bias, dn_out_norm, rec_w_out, ffn_w_up, ffn_conv, ffn_w_down)
    ms = (m_ada_w, m_ada_b, m_norm_mix, m_norm_ffn, m_attn_w_in, m_attn_q_norm_a, m_attn_k_norm_a, m_attn_q_norm_b, m_attn_k_norm_b, m_attn_sinks, m_attn_w_out, m_rec_w_in, m_s5_lambda_re, m_s5_lambda_im, m_s5_log_dt, m_s5_b_re, m_s5_b_im, m_s5_c_re, m_s5_c_im, m_s5_d, m_s5_glu_w, m_s5_glu_b, m_dn_conv, m_dn_a_log, m_dn_dt_bias, m_dn_out_norm, m_rec_w_out, m_ffn_w_up, m_ffn_conv, m_ffn_w_down)
    vs = (v_ada_w, v_ada_b, v_norm_mix, v_norm_ffn, v_attn_w_in, v_attn_q_norm_a, v_attn_k_norm_a, v_attn_q_norm_b, v_attn_k_norm_b, v_attn_sinks, v_attn_w_out, v_rec_w_in, v_s5_lambda_re, v_s5_lambda_im, v_s5_log_dt, v_s5_b_re, v_s5_b_im, v_s5_c_re, v_s5_c_im, v_s5_d, v_s5_glu_w, v_s5_glu_b, v_dn_conv, v_dn_a_log, v_dn_dt_bias, v_dn_out_norm, v_rec_w_out, v_ffn_w_up, v_ffn_conv, v_ffn_w_down)
    W = dict(zip(W_NAMES, args))
    Mo = dict(zip(W_NAMES, ms))
    Vo = dict(zip(W_NAMES, vs))
    xi, yi, ci = lax.axis_index("x"), lax.axis_index("y"), lax.axis_index("c")
    shard = 2 * xi + yi
    me8 = 4 * xi + 2 * yi + ci
    xs = x[0]
    target = loss_target[0]
    L, D = xs.shape

    wflat = _pack([W[n] for n in BIG], 1024, 16, bf16)
    wfull = all_gather(wflat, ("x", "y"), "gather_w").reshape(NSH, -1)
    Wf = {}
    off = 0
    for n in BIG:
        sz = int(np.prod(W[n].shape))
        Wf[n] = _unshard(wfull[:, off:off + sz].reshape((NSH,) + W[n].shape), n)
        off += sz
    rec_w_in_p = _rec_pad_cols(Wf["rec_w_in"][0])

    sflat = _pack([c] + [W[n] for n in SMALL_SHARDED], 1024, 8, f32)
    s8 = all_gather(sflat, AXES, "gather_small")
    s8f = s8.reshape(8, -1)
    c_all = s8f[:, :D]
    Ws = {}
    off = D
    for n in SMALL_SHARDED:
        sz = int(np.prod(W[n].shape))
        Ws[n] = _unshard(s8f[0::2, off:off + sz].reshape((NSH,) + W[n].shape), n)
        off += sz

    modp = ada_fwd(c_all, ada_w)
    modg = all_gather(modp, ("x", "y"), "gather_mod")
    mod_all = jnp.moveaxis(modg, 0, 2).reshape(2, 8, -1) + ada_b[:, None, :]
    mod = lax.dynamic_slice(mod_all, (0, me8, 0), (2, 1, mod_all.shape[2]))[:, 0, :]
    mods = [[mod[l:l + 1, j * D:(j + 1) * D] for j in range(6)] for l in range(2)]

    sh1, sc1, g1, sh2, sc2, g2_ = mods[0]
    nm0, nf0 = norm_mix[0:1], norm_ffn[0:1]
    sinkb = jnp.repeat(attn_sinks[0], HEAD_DIM)[None]
    h0 = modulate_fwd(xs, nm0, sc1, sh1, "l0_mod1_fwd")
    hin0 = mm(h0, Wf["attn_w_in"][0], name="l0_in_proj")
    ocat, att_saved = attention_fwd(hin0, attn_q_norm_a, attn_k_norm_a, attn_q_norm_b, attn_k_norm_b, sinkb)
    y0, x1 = mm(ocat, Wf["attn_w_out"][0], name="l0_out_proj", out_dtypes=(f32, f32), epi=_resid_epi,
                epi_mn=[xs], epi_n=[g1])
    x2, ffn0_saved = _ffn_fwd(x1, nf0, sc2, sh2, g2_, Wf["ffn_w_up"][0], Ws["ffn_conv"][0], Wf["ffn_w_down"][0], "l0")

    th1, tc1, t1, th2, tc2, t2 = mods[1]
    nm1, nf1 = norm_mix[1:2], norm_ffn[1:2]
    pad128 = lambda a: jnp.pad(a, ((0, 0), (0, 128 - a.shape[1])))
    s5p = dict(lr=s5_lambda_re[0], li=s5_lambda_im[0], ldt=s5_log_dt[0][:, None], b_re=s5_b_re[0], b_im=s5_b_im[0],
               c_re=s5_c_re[0], c_im=s5_c_im[0], d=Ws["s5_d"], gw=Ws["s5_glu_w"][0], gb=Ws["s5_glu_b"])
    dnp = dict(conv=Ws["dn_conv"][0], alog=pad128(dn_a_log), dtb=pad128(dn_dt_bias), onorm=dn_out_norm)
    h1 = modulate_fwd(x2, nm1, tc1, th1, "l1_mod1_fwd")
    hin1 = mm(h1, rec_w_in_p, name="l1_in_proj")
    yc, s5_saved = s5_fwd(hin1, s5p)
    ycat, dn_saved = deltanet_fwd(hin1, dnp, yc)
    y1, x3 = mm(ycat, Wf["rec_w_out"][0], name="l1_out_proj", out_dtypes=(f32, f32), epi=_resid_epi,
                epi_mn=[x2], epi_n=[t1])
    x4, ffn1_saved = _ffn_fwd(x3, nf1, tc2, th2, t2, Wf["ffn_w_up"][1], Ws["ffn_conv"][1], Wf["ffn_w_down"][1], "l1")

    dx, sse = loss_fwd_bwd(x4, target)
    loss = lax.psum(0.5 * sse[0, 0] / D, AXES)

    dx, gf1 = _ffn_bwd(dx, x3, nf1, tc2, th2, t2, Wf["ffn_w_up"][1], Ws["ffn_conv"][1], Wf["ffn_w_down"][1], ffn1_saved, "l1")
    dy1, dt1 = resid_bwd(dx, y1, t1, "l1_res1_bwd")
    dycat = mm(dy1, Wf["rec_w_out"][0], tb=True, name="l1_dycat")
    dw_rec_out = mm(ycat, dy1, ta=True, name="l1_dwout")
    du_skip, du_b, s5g = s5_bwd(hin1, s5p, s5_saved, dycat)
    dx_qkv, dz, dab, dng = deltanet_bwd(hin1, dnp, dn_saved, dycat)
    dhin1 = rec_assemble(dx_qkv, dz, du_skip, du_b, dab)
    dw_rec_in = _rec_unpad_cols(mm(h1, dhin1, ta=True, name="l1_dwin"))
    dh1 = mm(dhin1, rec_w_in_p, tb=True, name="l1_dh")
    dx, dnm1, dtc1, dth1 = modulate_bwd(x2, nm1, tc1, th1, dh1, dx, "l1_mod1_bwd")

    dx, gf0 = _ffn_bwd(dx, x1, nf0, sc2, sh2, g2_, Wf["ffn_w_up"][0], Ws["ffn_conv"][0], Wf["ffn_w_down"][0], ffn0_saved, "l0")
    dy0, dg1 = resid_bwd(dx, y0, g1, "l0_res1_bwd")
    dcat = mm(dy0, Wf["attn_w_out"][0], tb=True, name="l0_dcat")
    dw_attn_out = mm(ocat, dy0, ta=True, name="l0_dwout")
    dhin0, dwqa, dwka, dwqb, dwkb, dsinkb = attention_bwd(hin0, attn_q_norm_a, attn_k_norm_a, attn_q_norm_b,
                                                          attn_k_norm_b, sinkb, att_saved, dcat)
    dw_attn_in = mm(h0, dhin0, ta=True, name="l0_dwin")
    dh0 = mm(dhin0, Wf["attn_w_in"][0], tb=True, name="l0_dh")
    grad_x, dnm0, dsc1, dsh1 = modulate_bwd(xs, nm0, sc1, sh1, dh0, dx, "l0_mod1_bwd")

    dmod = jnp.concatenate([
        jnp.concatenate([dsh1, dsc1, dg1, gf0["sh"], gf0["sc"], gf0["gate"]], axis=1),
        jnp.concatenate([dth1, dtc1, dt1, gf1["sh"], gf1["sc"], gf1["gate"]], axis=1)], axis=0)
    gl = {
        "ada_b": dmod,
        "norm_mix": jnp.concatenate([dnm0, dnm1], axis=0),
        "norm_ffn": jnp.concatenate([gf0["nf"], gf1["nf"]], axis=0),
        "attn_q_norm_a": dwqa, "attn_k_norm_a": dwka, "attn_q_norm_b": dwqb, "attn_k_norm_b": dwkb,
        "attn_sinks": dsinkb[:, ::HEAD_DIM],
        "s5_lambda_re": s5g["lr"][None], "s5_lambda_im": s5g["li"][None], "s5_log_dt": s5g["ldt"][:, 0][None],
        "s5_b_re": s5g["b_re"][None], "s5_b_im": s5g["b_im"][None], "s5_c_re": s5g["c_re"][None],
        "s5_c_im": s5g["c_im"][None],
        "dn_a_log": dng["alog"][:, :DN_HEADS], "dn_dt_bias": dng["dtb"][:, :DN_HEADS], "dn_out_norm": dng["onorm"],
        "s5_d": s5g["d"], "s5_glu_w": s5g["gw"][None], "s5_glu_b": s5g["gb"], "dn_conv": dng["conv"][None],
        "ffn_conv": jnp.stack([gf0["conv"], gf1["conv"]]),
        "attn_w_in": dw_attn_in[None], "attn_w_out": dw_attn_out[None], "rec_w_in": dw_rec_in[None],
        "rec_w_out": dw_rec_out[None], "ffn_w_up": jnp.stack([gf0["w_up"], gf1["w_up"]]),
        "ffn_w_down": jnp.stack([gf0["w_dn"], gf1["w_dn"]]),
    }

    small_names = SMALL_REPL + SMALL_SHARDED
    gs = _pack([gl[n] for n in small_names], 128, 256, f32)
    gs8 = all_gather(gs, AXES, "gather_small_grads")
    gsum = sum_slots(gs8, "sum_small_grads")
    full_shapes = [gl[n].shape for n in small_names]
    gfull = dict(zip(small_names, _unpack(gsum, full_shapes)))
    dmod_all = gs8.reshape(8, -1)[:, :2 * 6 * D].reshape(8, 2, 6 * D)
    ncol = ada_w.shape[2]
    dmod_sh = jnp.moveaxis(lax.dynamic_slice(dmod_all, (0, 0, shard * ncol), (8, 2, ncol)), 0, 1)
    grads = {"ada_w": ada_bwd(c_all, dmod_sh)}
    for n in SMALL_REPL:
        grads[n] = gfull[n]
    for n in SMALL_SHARDED:
        sh_all = _to_shards(gfull[n], n)
        grads[n] = lax.dynamic_slice(sh_all, (shard,) + (0,) * (sh_all.ndim - 1), (1,) + sh_all.shape[1:])[0]

    gflat = jnp.concatenate([_to_shards(gl[n], n).reshape(NSH, -1) for n in BIG], axis=1)
    nel = gflat.shape[1]
    unit = 256 * 1024
    npad = -(-nel // unit) * unit
    gflat = jnp.pad(gflat, ((0, 0), (0, npad - nel))).astype(bf16).reshape(NSH, npad // 1024, 1024)
    gq = exchange(gflat, ("x", "y"), "reduce_xy")
    gpart = sum_slots(gq, "sum_chips", bf16)
    gc = all_gather(gpart, ("c",), "gather_grad_c")
    gsh = sum_slots(gc, "sum_pair").reshape(-1)
    off = 0
    for n in BIG:
        sz = int(np.prod(W[n].shape))
        grads[n] = gsh[off:off + sz].reshape(W[n].shape)
        off += sz

    delta, new_m, new_v = {}, {}, {}

    def as2d(a):
        return a.reshape(-1, a.shape[-1])

    for n in ["ada_w"] + BIG:
        d_, m_, v_ = adamw(as2d(W[n]), as2d(grads[n]), as2d(Mo[n]), as2d(Vo[n]), f"adamw_{n}")
        delta[n], new_m[n], new_v[n] = d_.reshape(W[n].shape), m_.reshape(W[n].shape), v_.reshape(W[n].shape)
    pk = lambda dd: _pack([dd[n] for n in small_names], 128, 256, f32)
    d_, m_, v_ = adamw(pk(W), pk(grads), pk(Mo), pk(Vo), "adamw_small")
    shp = [W[n].shape for n in small_names]
    for dst, src in ((delta, d_), (new_m, m_), (new_v, v_)):
        dst.update(zip(small_names, _unpack(src, shp)))

    return (loss, grad_x[None], *[grads[n] for n in W_NAMES], *[delta[n] for n in W_NAMES],
            *[new_m[n] for n in W_NAMES], *[new_v[n] for n in W_NAMES])
```

```python
import functools
import math

import numpy as np
import jax
import jax.numpy as jnp
from jax import lax
from jax.experimental import pallas as pl
from jax.experimental.pallas import tpu as pltpu

f32 = jnp.float32
bf16 = jnp.bfloat16
HI = lax.Precision.HIGHEST
MESH = pl.DeviceIdType.MESH

HEAD_DIM = 64
BLOCK = 128
A_Q_HEADS = 8
A_KV_HEADS = 2
A_WINDOW = 128
B_HEADS = 8
B_BRANCHES = ((128, 1), (512, 4), (2048, 16))
N_ATTN_HEADS = 16
ATTN_IN = 2304
S5_GROUP = 16
S5_GROUPS = 16
S5_WIDTH = 256
S5_STATE = 64
DN_HEADS = 6
DN_DK = 128
DN_CONV = 4
DN_CHUNK = 64
REC_IN = 3340
REC_PAD = 3584
FFN_CONV = 3
EPS = 1e-6
ADAM_LR = 0.001
ADAM_B1 = 0.9
ADAM_B2 = 0.999
ADAM_EPS = 1e-08
ADAM_WD = 0.01
ADAM_STEP = 10

LANE = 128
SUBLANE = 8
VMEM_LIMIT = 52 * 1024 * 1024
MM_FULL_K = 5632
MM_VMEM_BUDGET = 40 * 1024 * 1024


def _cp(*sem):
    return pltpu.CompilerParams(dimension_semantics=sem, vmem_limit_bytes=VMEM_LIMIT)


def _pick(dim, cap, unit=LANE):
    for t in (2048, 1024, 768, 512, 384, 256, 128, 64, 32, 16, 8):
        if t <= cap and t % unit == 0 and dim % t == 0:
            return t
    return dim


def _dot(a, b, dims=(((1,), (0,)), ((), ())), precision=None):
    return lax.dot_general(a, b, dims, precision=precision, preferred_element_type=f32)


NN = (((1,), (0,)), ((), ()))
NT = (((1,), (1,)), ((), ()))
TN = (((0,), (0,)), ((), ()))


def mm(a, b, *, name, ta=False, tb=False, a_win=None, b_win=None, out_dtypes=(f32,),
       epi=None, epi_mn=(), epi_n=(), tm_cap=1024, tn_cap=8192, tk_cap=None):
    a0, a1 = a.shape
    b0, b1 = b.shape
    aw = a_win or (0, a1)
    bw = b_win or (0, b1)
    if ta:
        K, M = a0, aw[1]
    else:
        M, K = a0, aw[1]
    if tb:
        N, K2 = b0, bw[1]
    else:
        K2, N = b0, bw[1]
    assert K == K2, (a.shape, b.shape, ta, tb, a_win, b_win)
    if tk_cap is None:
        tk_cap = K if K <= MM_FULL_K else 2048
    tk = _pick(K, tk_cap, SUBLANE if (ta and not tb) else LANE)
    nk = K // tk
    sa, sb = a.dtype.itemsize, b.dtype.itemsize
    so = sum(jnp.dtype(d).itemsize for d in out_dtypes)
    n_mn, n_n, n_out = len(epi_mn), len(epi_n), len(out_dtypes)

    def vmem(tm_, tn_):
        return 2 * (tm_ * tk * sa + tk * tn_ * sb + tm_ * tn_ * (so + 4 * n_mn)) + 2 * tm_ * tn_ * 4

    best = None
    for tm_ in (t for t in (1024, 512, 256, 128) if M % t == 0 and (not ta or aw[0] % t == 0)):
        for tn_ in (t for t in (N, N // 2, 1024, 768, 512, 384, 256, 128)
                    if t % LANE == 0 and N % t == 0 and (tb or bw[0] % t == 0)):
            if tm_ <= tm_cap and tn_ <= max(tn_cap, 0) and vmem(tm_, tn_) <= MM_VMEM_BUDGET:
                if best is None or (tm_ * tn_, tn_) > (best[0] * best[1], best[1]):
                    best = (tm_, tn_)
    assert best is not None, (name, M, N, K)
    tm, tn = best
    b_outer = tk * tn * sb > tm * tk * sa

    def ix(f):
        if b_outer:
            return lambda j, i, k: f(i, j, k)
        return f

    if ta:
        mo = aw[0] // tm
        a_spec = pl.BlockSpec((tk, tm), ix(lambda i, j, k: (k, i + mo)))
    else:
        assert aw[0] % tk == 0
        ko = aw[0] // tk
        a_spec = pl.BlockSpec((tm, tk), ix(lambda i, j, k: (i, k + ko)))
    if tb:
        assert bw[0] % tk == 0
        kob = bw[0] // tk
        b_spec = pl.BlockSpec((tn, tk), ix(lambda i, j, k: (j, k + kob)))
    else:
        no = bw[0] // tn
        b_spec = pl.BlockSpec((tk, tn), ix(lambda i, j, k: (k, j + no)))
    dims = (((0 if ta else 1,), (1 if tb else 0,)), ((), ()))

    def body(a_ref, b_ref, *rest):
        mn_refs = rest[:n_mn]
        n_refs = rest[n_mn:n_mn + n_n]
        out_refs = rest[n_mn + n_n:n_mn + n_n + n_out]

        def finish(r):
            if epi is None:
                outs = (r,)
            else:
                outs = epi(r, *[m[...] for m in mn_refs], *[v[...] for v in n_refs])
            for o_ref, o in zip(out_refs, outs):
                o_ref[...] = o.astype(o_ref.dtype)

        part = _dot(a_ref[...].astype(bf16), b_ref[...].astype(bf16), dims)
        if nk == 1:
            finish(part)
            return
        acc = rest[-1]
        k = pl.program_id(2)

        @pl.when(k == 0)
        def _():
            acc[...] = part

        @pl.when(k > 0)
        def _():
            acc[...] += part

        @pl.when(k == nk - 1)
        def _():
            finish(acc[...])

    mn_spec = pl.BlockSpec((tm, tn), ix(lambda i, j, k: (i, j)))
    n_spec = pl.BlockSpec((1, tn), ix(lambda i, j, k: (0, j)))
    gi, gj = M // tm, N // tn
    outs = pl.pallas_call(
        body,
        grid=(gj, gi, nk) if b_outer else (gi, gj, nk),
        in_specs=[a_spec, b_spec] + [mn_spec] * n_mn + [n_spec] * n_n,
        out_specs=[mn_spec] * n_out,
        out_shape=[jax.ShapeDtypeStruct((M, N), d) for d in out_dtypes],
        scratch_shapes=[pltpu.VMEM((tm, tn), f32)] if nk > 1 else [],
        compiler_params=_cp("parallel", "parallel", "arbitrary"),
        name=name,
    )(a, b, *epi_mn, *epi_n)
    return outs[0] if n_out == 1 else tuple(outs)


def rowwise(fn, *, name, L, tm, rows=(), consts=(), outs=(), sums=()):
    nb = L // tm
    in_specs = []
    arrs = []
    for arr, start, width, kind in rows:
        assert start % width == 0, (name, start, width)
        co = start // width
        hr = SUBLANE * (4 // arr.dtype.itemsize)
        hb = tm // hr
        if kind == "cur":
            in_specs.append(pl.BlockSpec((tm, width), lambda i, co=co: (i, co)))
        elif kind == "prev":
            in_specs.append(pl.BlockSpec((hr, width), lambda i, co=co, hb=hb: (jnp.maximum(i * hb - 1, 0), co)))
        else:
            last = L // hr - 1
            in_specs.append(pl.BlockSpec((hr, width), lambda i, co=co, hb=hb, last=last:
                                         (jnp.minimum((i + 1) * hb, last), co)))
        arrs.append(arr)
    for cst in consts:
        assert cst.ndim == 2
        in_specs.append(pl.BlockSpec(cst.shape, lambda i: (0, 0)))
        arrs.append(cst)
    n_rows, n_c, n_o, n_s = len(rows), len(consts), len(outs), len(sums)
    out_specs = [pl.BlockSpec((tm, w), lambda i: (i, 0)) for w, _ in outs]
    out_specs += [pl.BlockSpec(s, lambda i: (0, 0)) for s in sums]
    out_shape = [jax.ShapeDtypeStruct((L, w), d) for w, d in outs]
    out_shape += [jax.ShapeDtypeStruct(s, f32) for s in sums]

    def body(*refs):
        i = pl.program_id(0)
        vals = [r[...] for r in refs[:n_rows + n_c]]
        res = fn(i, nb, *vals)
        if not isinstance(res, (tuple, list)):
            res = (res,)
        o_refs = refs[n_rows + n_c:n_rows + n_c + n_o]
        s_refs = refs[n_rows + n_c + n_o:]
        for o_ref, o in zip(o_refs, res[:n_o]):
            o_ref[...] = o.astype(o_ref.dtype)
        if n_s:
            @pl.when(i == 0)
            def _():
                for s_ref in s_refs:
                    s_ref[...] = jnp.zeros_like(s_ref)

            for s_ref, s in zip(s_refs, res[n_o:]):
                s_ref[...] += s

    res = pl.pallas_call(
        body,
        grid=(nb,),
        in_specs=in_specs,
        out_specs=out_specs,
        out_shape=out_shape,
        compiler_params=_cp("arbitrary" if n_s else "parallel"),
        name=name,
    )(*arrs)
    return res[0] if len(res) == 1 else tuple(res)


def _shift_down(x, prev8, k):
    cat = jnp.concatenate([prev8, x], axis=0)
    return pltpu.roll(cat, k, 0)[prev8.shape[0]:, :]


def _shift_up(x, next8, k):
    n = x.shape[0]
    cat = jnp.concatenate([x, next8], axis=0)
    return pltpu.roll(cat, n + next8.shape[0] - k, 0)[:n, :]


def _colsum(x):
    return jnp.sum(x, axis=0, keepdims=True)


def _silu(x):
    return x * jax.nn.sigmoid(x)


def _modulate_fn(x, nw, sc, sh):
    r = lax.rsqrt(jnp.mean(x * x, axis=-1, keepdims=True) + EPS)
    return (x * r * nw) * (1.0 + sc) + sh


def modulate_fwd(x, nw, sc, sh, name):
    L, D = x.shape

    def fn(i, nb, xt, nwv, scv, shv):
        return _modulate_fn(xt, nwv, scv, shv)

    return rowwise(fn, name=name, L=L, tm=_pick(L, 512, SUBLANE), rows=[(x, 0, D, "cur")],
                   consts=[nw, sc, sh], outs=[(D, bf16)])


def modulate_bwd(x, nw, sc, sh, dh, dx_in, name):
    L, D = x.shape

    def fn(i, nb, xt, dht, dxt, nwv, scv, shv):
        _, vjp = jax.vjp(_modulate_fn, xt, nwv, scv, shv)
        dx, dnw, dsc, dsh = vjp(dht)
        return dxt + dx, dnw, dsc, dsh

    return rowwise(fn, name=name, L=L, tm=_pick(L, 256, SUBLANE),
                   rows=[(x, 0, D, "cur"), (dh, 0, D, "cur"), (dx_in, 0, D, "cur")],
                   consts=[nw, sc, sh], outs=[(D, f32)], sums=[(1, D)] * 3)


def resid_bwd(dx, y, g, name):
    L, D = dx.shape

    def fn(i, nb, dxt, yt, gv):
        return dxt * gv, _colsum(dxt * yt)

    return rowwise(fn, name=name, L=L, tm=_pick(L, 512, SUBLANE),
                   rows=[(dx, 0, D, "cur"), (y, 0, D, "cur")], consts=[g],
                   outs=[(D, bf16)], sums=[(1, D)])


def _resid_epi(acc, xt, gv):
    return acc, xt + gv * acc


def _stack_rows(rows, n=SUBLANE):
    c = rows[0].shape[1]
    ridx = lax.broadcasted_iota(jnp.int32, (n, c), 0)
    out = jnp.zeros((n, c), f32)
    for j, r in enumerate(rows):
        out = out + jnp.where(ridx == j, r, 0.0)
    return out


def _conv_causal(x, prev8, w):
    W = w.shape[0]
    y = x * w[W - 1:W, :]
    for j in range(W - 1):
        y = y + _shift_down(x, prev8, W - 1 - j) * w[j:j + 1, :]
    return y


def _conv_causal_bwd_x(dy, next8, w):
    W = w.shape[0]
    dx = dy * w[W - 1:W, :]
    for j in range(W - 1):
        dx = dx + _shift_up(dy, next8, W - 1 - j) * w[j:j + 1, :]
    return dx


def _conv_causal_bwd_w(dy, x, prev8, W):
    rows = [_colsum(dy * _shift_down(x, prev8, W - 1 - j)) for j in range(W - 1)]
    rows.append(_colsum(dy * x))
    return _stack_rows(rows)


def ffn_act_fwd(up, conv_w, name):
    L, F2 = up.shape
    F = F2 // 2

    def fn(i, nb, u, p8, w):
        c = _conv_causal(u.astype(f32), p8.astype(f32) * (i > 0).astype(f32), w)
        return _silu(c[:, :F]) * c[:, F:]

    return rowwise(fn, name=name, L=L, tm=_pick(L, 128, SUBLANE),
                   rows=[(up, 0, F2, "cur"), (up, 0, F2, "prev")], consts=[conv_w], outs=[(F, bf16)])


def ffn_act_conv_bwd(up, conv_w, dact, name):
    L, F2 = up.shape
    F = F2 // 2
    W = conv_w.shape[0]

    def fn(i, nb, u, da, p8, un8, dan8, w):
        tm, ext = u.shape[0], un8.shape[0]
        more = (i < nb - 1).astype(f32)
        u, da = u.astype(f32), da.astype(f32)
        p8 = p8.astype(f32) * (i > 0).astype(f32)
        c = _conv_causal(jnp.concatenate([u, un8.astype(f32) * more], axis=0), p8, w)
        dae = jnp.concatenate([da, dan8.astype(f32) * more], axis=0)
        a, b = c[:, :F], c[:, F:]
        sg = jax.nn.sigmoid(a)
        dc = jnp.concatenate([dae * b * (sg * (1.0 + a * (1.0 - sg))), dae * a * sg], axis=1)
        dx = dc[:tm] * w[W - 1:W, :]
        for j in range(W - 1):
            dx = dx + pltpu.roll(dc, tm + ext - (W - 1 - j), 0)[:tm] * w[j:j + 1, :]
        return dx, _conv_causal_bwd_w(dc[:tm], u, p8, W)

    return rowwise(fn, name=name, L=L, tm=_pick(L, 128, SUBLANE),
                   rows=[(up, 0, F2, "cur"), (dact, 0, F, "cur"), (up, 0, F2, "prev"), (up, 0, F2, "next"),
                         (dact, 0, F, "next")],
                   consts=[conv_w], outs=[(F2, bf16)], sums=[(SUBLANE, F2)])


ALIBI = [2.0 ** (-8.0 * (i + 1) / N_ATTN_HEADS) for i in range(N_ATTN_HEADS)]
NEG = -1e30


class _Band:
    def __init__(self, dilation, group_a):
        d = dilation
        self.d = d
        self.group_a = group_a
        if group_a:
            self.P, self.qw, self.hps, self.kvw = 1, 512, 8, 128
            self.qcol = lambda p: 0
            self.kcol = lambda p: 4
            self.vcol = lambda p: 5
            self.kv_of = lambda j: j // 4
            self.max_dist = A_WINDOW - 1
            sl = np.repeat(np.asarray(ALIBI[:8], np.float32), HEAD_DIM)[None, None, :]
        else:
            self.P, self.qw, self.hps, self.kvw = 2 * d, 256, 4, 256
            self.qcol = lambda p: lax.div(p, 2) * 9 + 3 + lax.rem(p, 2)
            self.kcol = lambda p: lax.div(p, 2) * 9 + 5 + lax.rem(p, 2)
            self.vcol = lambda p: lax.div(p, 2) * 9 + 7 + lax.rem(p, 2)
            self.kv_of = lambda j: j
            self.max_dist = BLOCK
            per = np.repeat(np.asarray(ALIBI[8:], np.float32), HEAD_DIM).reshape(2, 1, 256)
            sl = np.tile(per, (d, 1, 1))
        self.slopes = jnp.asarray(sl, f32)


def _band_mask(n, d, max_dist):
    qi = lax.broadcasted_iota(jnp.int32, (BLOCK, 2 * BLOCK), 0)
    kj = lax.broadcasted_iota(jnp.int32, (BLOCK, 2 * BLOCK), 1)
    dist = BLOCK + qi - kj
    valid = (dist >= 0) & (dist <= max_dist) & ((n > 0) | (kj >= BLOCK))
    return valid, -(d * dist).astype(f32)


def _rms64(x, w):
    r = lax.rsqrt(jnp.mean(x * x, axis=-1, keepdims=True) + EPS)
    xh = x * r
    return xh * w, xh, r


def _rms64_bwd(dy, xh, r, w):
    t = dy * w
    dw = jnp.sum(jnp.sum(dy * xh, axis=0), axis=0, keepdims=True)
    return r * (t - xh * jnp.mean(t * xh, axis=-1, keepdims=True)), dw


def _heads64(x, heads):
    return jnp.stack([x[:, h * 64:(h + 1) * 64] for h in heads])


def attn_fwd(hv, band, wq, wk, name):
    M = hv.shape[0]
    nb = M // BLOCK
    P, qw, hps = band.P, band.qw, band.hps
    d, max_dist, kv_of = band.d, band.max_dist, band.kv_of
    kv_heads = sorted({kv_of(j) for j in range(hps)})
    kv_pos = {h: i for i, h in enumerate(kv_heads)}
    gqa = len(kv_heads) != hps

    def body(q_ref, kp_ref, kc_ref, vp_ref, vc_ref, sl_ref, wq_ref, wk_ref, o_ref, lse_ref):
        n = pl.program_id(1)
        valid, negd = _band_mask(n, d, max_dist)
        kblk = jnp.concatenate([kp_ref[...], kc_ref[...]], axis=0)
        vblk = jnp.concatenate([vp_ref[...], vc_ref[...]], axis=0)
        q = _heads64(q_ref, range(hps))
        kn = _rms64(_heads64(kblk, kv_heads), wk_ref[...])[0].astype(bf16)
        v = _heads64(vblk, kv_heads).astype(bf16)
        kn_q = jnp.stack([kn[kv_pos[kv_of(j)]] for j in range(hps)]) if gqa else kn
        v_q = jnp.stack([v[kv_pos[kv_of(j)]] for j in range(hps)]) if gqa else v
        qn = _rms64(q, wq_ref[...])[0].astype(bf16)
        slope = jnp.stack([sl_ref[0, :, j * 64:j * 64 + 1] for j in range(hps)])
        s = _dot(qn, kn_q, BNT) * (HEAD_DIM ** -0.5) + slope * negd
        s = jnp.where(valid, s, NEG)
        m = jnp.max(s, axis=-1, keepdims=True)
        p = jnp.exp(s - m)
        l = jnp.sum(p, axis=-1, keepdims=True)
        o = _dot(p.astype(bf16), v_q, BNN) / l
        lse = m + jnp.log(l)
        for j in range(hps):
            o_ref[:, j * 64:(j + 1) * 64] = o[j]
            lse_ref[:, j * 64:(j + 1) * 64] = jnp.broadcast_to(lse[j], (BLOCK, 64))

    qcol, kcol, vcol, kvw = band.qcol, band.kcol, band.vcol, band.kvw
    in_specs = [
        pl.BlockSpec((BLOCK, qw), lambda p, n: (n, qcol(p))),
        pl.BlockSpec((BLOCK, kvw), lambda p, n: (jnp.maximum(n - 1, 0), kcol(p))),
        pl.BlockSpec((BLOCK, kvw), lambda p, n: (n, kcol(p))),
        pl.BlockSpec((BLOCK, kvw), lambda p, n: (jnp.maximum(n - 1, 0), vcol(p))),
        pl.BlockSpec((BLOCK, kvw), lambda p, n: (n, vcol(p))),
        pl.BlockSpec((1, 1, qw), lambda p, n: (p, 0, 0)),
        pl.BlockSpec((1, 64), lambda p, n: (0, 0)),
        pl.BlockSpec((1, 64), lambda p, n: (0, 0)),
    ]
    o_spec = pl.BlockSpec((BLOCK, qw), lambda p, n: (n, p))
    return pl.pallas_call(
        body, grid=(P, nb), in_specs=in_specs, out_specs=[o_spec, o_spec],
        out_shape=[jax.ShapeDtypeStruct((M, P * qw), f32)] * 2,
        compiler_params=_cp("parallel", "parallel"), name=name,
    )(hv, hv, hv, hv, hv, band.slopes, wq, wk)


def attn_bwd(hv, band, wq, wk, o, lse, do, dlse, dw0, name):
    M = hv.shape[0]
    nb = M // BLOCK
    P, qw, hps = band.P, band.qw, band.hps
    d, max_dist, kv_of = band.d, band.max_dist, band.kv_of
    kv_heads = sorted({kv_of(j) for j in range(hps)})
    kv_pos = {h: i for i, h in enumerate(kv_heads)}
    gqa = len(kv_heads) != hps

    def body(q_ref, kp_ref, kc_ref, vp_ref, vc_ref, sl_ref, wq_ref, wk_ref, o_ref, lse_ref, do_ref, dlse_ref,
             dwq0_ref, dwk0_ref, dq_ref, dk_ref, dv_ref, dwq_ref, dwk_ref, ck, cv):
        pp = pl.program_id(0)
        n = pl.program_id(1)

        @pl.when((pp == 0) & (n == 0))
        def _():
            dwq_ref[...] = dwq0_ref[...]
            dwk_ref[...] = dwk0_ref[...]

        @pl.when(n == 0)
        def _():
            ck[...] = jnp.zeros_like(ck)
            cv[...] = jnp.zeros_like(cv)

        @pl.when(n < nb)
        def _():
            valid, negd = _band_mask(n, d, max_dist)
            kblk = jnp.concatenate([kp_ref[...], kc_ref[...]], axis=0)
            vblk = jnp.concatenate([vp_ref[...], vc_ref[...]], axis=0)
            wqv, wkv = wq_ref[...], wk_ref[...]
            hs = range(hps)
            kn_f, kh, rk = _rms64(_heads64(kblk, kv_heads), wkv)
            kn = kn_f.astype(bf16)
            v = _heads64(vblk, kv_heads).astype(bf16)
            kn_q = jnp.stack([kn[kv_pos[kv_of(j)]] for j in hs]) if gqa else kn
            v_q = jnp.stack([v[kv_pos[kv_of(j)]] for j in hs]) if gqa else v
            qn_f, qh, rq = _rms64(_heads64(q_ref, hs), wqv)
            qn = qn_f.astype(bf16)
            col = lambda ref: jnp.stack([ref[:, j * 64:j * 64 + 1] for j in hs])
            slope = jnp.stack([sl_ref[0, :, j * 64:j * 64 + 1] for j in hs])
            s = _dot(qn, kn_q, BNT) * (HEAD_DIM ** -0.5) + slope * negd
            p = jnp.where(valid, jnp.exp(s - col(lse_ref)), 0.0)
            do_h = _heads64(do_ref, hs)
            delta = jnp.sum(do_h * _heads64(o_ref, hs), axis=-1, keepdims=True)
            do_b = do_h.astype(bf16)
            dp = _dot(do_b, v_q, BNT)
            ds = (p * (dp - delta + col(dlse_ref))).astype(bf16)
            dqn = _dot(ds, kn_q, BNN) * (HEAD_DIM ** -0.5)
            dkn_q = _dot(ds, qn, BTN) * (HEAD_DIM ** -0.5)
            dv_q = _dot(p.astype(bf16), do_b, BTN)
            if gqa:
                grp = lambda t: jnp.stack([sum(t[j] for j in hs if kv_of(j) == h) for h in kv_heads])
                dkn_q, dv_q = grp(dkn_q), grp(dv_q)
            dq, dwq_acc = _rms64_bwd(dqn, qh, rq, wqv)
            for j in hs:
                dq_ref[:, j * 64:(j + 1) * 64] = dq[j]
            dwq_ref[...] += dwq_acc
            dk_h, dwk_acc = _rms64_bwd(dkn_q, kh, rk, wkv)
            dwk_ref[...] += dwk_acc
            dk_all = jnp.concatenate([dk_h[i] for i in range(len(kv_heads))], axis=1)
            dv_all = jnp.concatenate([dv_q[i] for i in range(len(kv_heads))], axis=1)
            dk_ref[...] = ck[...] + dk_all[:BLOCK]
            dv_ref[...] = cv[...] + dv_all[:BLOCK]
            ck[...] = dk_all[BLOCK:]
            cv[...] = dv_all[BLOCK:]

        @pl.when(n == nb)
        def _():
            dk_ref[...] = ck[...]
            dv_ref[...] = cv[...]

    qcol, kcol, vcol, kvw = band.qcol, band.kcol, band.vcol, band.kvw
    cl = lambda n: jnp.minimum(n, nb - 1)
    pv = lambda n: jnp.maximum(jnp.minimum(n, nb - 1) - 1, 0)
    o_in = pl.BlockSpec((BLOCK, qw), lambda p, n: (cl(n), p))
    in_specs = [
        pl.BlockSpec((BLOCK, qw), lambda p, n: (cl(n), qcol(p))),
        pl.BlockSpec((BLOCK, kvw), lambda p, n: (pv(n), kcol(p))),
        pl.BlockSpec((BLOCK, kvw), lambda p, n: (cl(n), kcol(p))),
        pl.BlockSpec((BLOCK, kvw), lambda p, n: (pv(n), vcol(p))),
        pl.BlockSpec((BLOCK, kvw), lambda p, n: (cl(n), vcol(p))),
        pl.BlockSpec((1, 1, qw), lambda p, n: (p, 0, 0)),
        pl.BlockSpec((1, 64), lambda p, n: (0, 0)),
        pl.BlockSpec((1, 64), lambda p, n: (0, 0)),
        o_in, o_in, o_in, o_in,
        pl.BlockSpec((1, 64), lambda p, n: (0, 0)),
        pl.BlockSpec((1, 64), lambda p, n: (0, 0)),
    ]
    kv_out = pl.BlockSpec((BLOCK, kvw), lambda p, n: (jnp.maximum(n - 1, 0), p))
    w_out = pl.BlockSpec((1, 64), lambda p, n: (0, 0))
    return pl.pallas_call(
        body, grid=(P, nb + 1), in_specs=in_specs,
        out_specs=[o_in, kv_out, kv_out, w_out, w_out],
        out_shape=[jax.ShapeDtypeStruct((M, P * qw), f32), jax.ShapeDtypeStruct((M, P * kvw), f32),
                   jax.ShapeDtypeStruct((M, P * kvw), f32), jax.ShapeDtypeStruct((1, 64), f32),
                   jax.ShapeDtypeStruct((1, 64), f32)],
        scratch_shapes=[pltpu.VMEM((BLOCK, kvw), f32), pltpu.VMEM((BLOCK, kvw), f32)],
        compiler_params=_cp("arbitrary", "arbitrary"), name=name,
    )(hv, hv, hv, hv, hv, band.slopes, wq, wk, o, lse, do, dlse, *dw0)


class _Plan:
    def __init__(self, dilation, group_a, nq):
        self.d, self.nq = dilation, nq
        if group_a:
            self.P, self.nkv = 1, 1
            self.q0, self.k0, self.v0 = 0, 4, 5
            self.kv_of = lambda j: j // 4
            self.max_dist = A_WINDOW - 1
            slopes = ALIBI[:8]
        else:
            self.P, self.nkv = 4 // nq, nq
            self.q0, self.k0, self.v0 = 6, 10, 14
            self.kv_of = lambda j: j
            self.max_dist = BLOCK
            slopes = ALIBI[8:]
        self.hps = 2 * nq
        sl = np.repeat(np.asarray(slopes, np.float32), HEAD_DIM).reshape(self.P, 1, self.hps * HEAD_DIM)
        self.slopes = jnp.asarray(sl, f32)


def _rows(r, d):
    return pl.ds(r, BLOCK, stride=d) if d > 1 else pl.ds(0, BLOCK)


def _pairs(refs, rows):
    parts = []
    for ref in refs:
        blk = ref[rows, :]
        parts += [blk[:, :HEAD_DIM], blk[:, HEAD_DIM:]]
    return jnp.stack(parts)


def _pairs2(prev_refs, cur_refs, rows):
    parts = []
    for pr, cr in zip(prev_refs, cur_refs):
        blk = jnp.concatenate([pr[rows, :], cr[rows, :]], axis=0)
        parts += [blk[:, :HEAD_DIM], blk[:, HEAD_DIM:]]
    return jnp.stack(parts)


def _lane_pair(t, i):
    return jnp.concatenate([t[2 * i], t[2 * i + 1]], axis=1)


def attn2_fwd(hin, plan, wq, wk, name):
    L = hin.shape[0]
    d, nq, nkv, hps, P = plan.d, plan.nq, plan.nkv, plan.hps, plan.P
    R = BLOCK * d
    nb = L // R
    kv_of, max_dist = plan.kv_of, plan.max_dist
    gqa = 2 * nkv != hps

    def body(*refs):
        q_refs = refs[:nq]
        kp, kc = refs[nq:nq + nkv], refs[nq + nkv:nq + 2 * nkv]
        vp, vc = refs[nq + 2 * nkv:nq + 3 * nkv], refs[nq + 3 * nkv:nq + 4 * nkv]
        sl_ref, wq_ref, wk_ref, o_ref, lse_ref = refs[nq + 4 * nkv:nq + 4 * nkv + 5]
        o_refs = refs[nq + 4 * nkv + 5:2 * nq + 4 * nkv + 5]
        lse_refs = refs[2 * nq + 4 * nkv + 5:]
        n = pl.program_id(1)
        valid, negd = _band_mask(n, d, max_dist)
        slope = jnp.stack([sl_ref[0, :, j * 64:j * 64 + 1] for j in range(hps)])
        wqv, wkv = wq_ref[...], wk_ref[...]

        def residue(r, carry):
            rows = _rows(r, d)
            kn = _rms64(_pairs2(kp, kc, rows), wkv)[0].astype(bf16)
            v = _pairs2(vp, vc, rows).astype(bf16)
            if gqa:
                kn = jnp.stack([kn[kv_of(j)] for j in range(hps)])
                v = jnp.stack([v[kv_of(j)] for j in range(hps)])
            qn = _rms64(_pairs(q_refs, rows), wqv)[0].astype(bf16)
            s = _dot(qn, kn, BNT) * (HEAD_DIM ** -0.5) + slope * negd
            s = jnp.where(valid, s, NEG)
            m = jnp.max(s, axis=-1, keepdims=True)
            p = jnp.exp(s - m)
            l = jnp.sum(p, axis=-1, keepdims=True)
            o = _dot(p.astype(bf16), v, BNN) / l
            lse = jnp.broadcast_to(m + jnp.log(l), (hps, BLOCK, HEAD_DIM))
            for i in range(nq):
                o_refs[i][rows, :] = _lane_pair(o, i)
                lse_refs[i][rows, :] = _lane_pair(lse, i)
            return carry

        lax.fori_loop(0, d, residue, 0)
        for i in range(nq):
            o_ref[:, i * 128:(i + 1) * 128] = o_refs[i][...]
            lse_ref[:, i * 128:(i + 1) * 128] = lse_refs[i][...]

    col = lambda c0, i: (lambda p, n: (n, c0 + p * nq + i))
    prv = lambda c0, i: (lambda p, n: (jnp.maximum(n - 1, 0), c0 + p * nq + i))
    blk = lambda f: pl.BlockSpec((R, 128), f)
    in_specs = [blk(col(plan.q0, i)) for i in range(nq)]
    in_specs += [blk(prv(plan.k0, i)) for i in range(nkv)] + [blk(col(plan.k0, i)) for i in range(nkv)]
    in_specs += [blk(prv(plan.v0, i)) for i in range(nkv)] + [blk(col(plan.v0, i)) for i in range(nkv)]
    in_specs += [pl.BlockSpec((1, 1, hps * 64), lambda p, n: (p, 0, 0)),
                 pl.BlockSpec((1, 64), lambda p, n: (0, 0)), pl.BlockSpec((1, 64), lambda p, n: (0, 0))]
    wide = pl.BlockSpec((R, 128 * nq), lambda p, n: (n, p))
    return pl.pallas_call(
        body, grid=(P, nb), in_specs=in_specs, out_specs=[wide, wide],
        out_shape=[jax.ShapeDtypeStruct((L, 512), f32)] * 2,
        scratch_shapes=[pltpu.VMEM((R, 128), f32)] * (2 * nq),
        compiler_params=_cp("parallel", "parallel"), name=name,
    )(*([hin] * (nq + 4 * nkv)), plan.slopes, wq, wk)


def attn2_bwd(hin, plan, wq, wk, o, lse, do, dlse, dw0, name):
    L = hin.shape[0]
    d, nq, nkv, hps, P = plan.d, plan.nq, plan.nkv, plan.hps, plan.P
    R = BLOCK * d
    nb = L // R
    kv_of, max_dist = plan.kv_of, plan.max_dist
    nkh = 2 * nkv
    gqa = nkh != hps
    n_in = nq + 4 * nkv + 3 + 4 * nq + 2

    def body(*refs):
        q_refs = refs[:nq]
        kp, kc = refs[nq:nq + nkv], refs[nq + nkv:nq + 2 * nkv]
        vp, vc = refs[nq + 2 * nkv:nq + 3 * nkv], refs[nq + 3 * nkv:nq + 4 * nkv]
        b = nq + 4 * nkv
        sl_ref, wq_ref, wk_ref = refs[b:b + 3]
        b += 3
        o_refs, lse_refs = refs[b:b + nq], refs[b + nq:b + 2 * nq]
        do_refs, dlse_refs = refs[b + 2 * nq:b + 3 * nq], refs[b + 3 * nq:b + 4 * nq]
        dwq0_ref, dwk0_ref = refs[b + 4 * nq:b + 4 * nq + 2]
        dq_ref, dk_ref, dv_ref, dwq_ref, dwk_ref = refs[n_in:n_in + 5]
        sc = refs[n_in + 5:]
        dq_s, dk_s, dv_s = sc[:nq], sc[nq:nq + nkv], sc[nq + nkv:nq + 2 * nkv]
        ck, cv = sc[nq + 2 * nkv:nq + 3 * nkv], sc[nq + 3 * nkv:]
        pp = pl.program_id(0)
        n = pl.program_id(1)

        @pl.when((pp == 0) & (n == 0))
        def _():
            dwq_ref[...] = dwq0_ref[...]
            dwk_ref[...] = dwk0_ref[...]

        @pl.when(n == 0)
        def _():
            for c in (*ck, *cv):
                c[...] = jnp.zeros_like(c)

        @pl.when(n < nb)
        def _():
            valid, negd = _band_mask(n, d, max_dist)
            slope = jnp.stack([sl_ref[0, :, j * 64:j * 64 + 1] for j in range(hps)])
            wqv, wkv = wq_ref[...], wk_ref[...]
            hs = range(hps)

            def residue(r, carry):
                rows = _rows(r, d)
                kn_f, kh, rk = _rms64(_pairs2(kp, kc, rows), wkv)
                kn = kn_f.astype(bf16)
                v = _pairs2(vp, vc, rows).astype(bf16)
                if gqa:
                    kn = jnp.stack([kn[kv_of(j)] for j in hs])
                    v = jnp.stack([v[kv_of(j)] for j in hs])
                qn_f, qh, rq = _rms64(_pairs(q_refs, rows), wqv)
                qn = qn_f.astype(bf16)
                s = _dot(qn, kn, BNT) * (HEAD_DIM ** -0.5) + slope * negd
                p = jnp.where(valid, jnp.exp(s - _pairs(lse_refs, rows)[:, :, :1]), 0.0)
                do_h = _pairs(do_refs, rows)
                delta = jnp.sum(do_h * _pairs(o_refs, rows), axis=-1, keepdims=True)
                do_b = do_h.astype(bf16)
                dp = _dot(do_b, v, BNT)
                ds = (p * (dp - delta + _pairs(dlse_refs, rows)[:, :, :1])).astype(bf16)
                dqn = _dot(ds, kn, BNN) * (HEAD_DIM ** -0.5)
                dkn = _dot(ds, qn, BTN) * (HEAD_DIM ** -0.5)
                dvv = _dot(p.astype(bf16), do_b, BTN)
                if gqa:
                    grp = lambda t: jnp.stack([sum(t[j] for j in hs if kv_of(j) == h) for h in range(nkh)])
                    dkn, dvv = grp(dkn), grp(dvv)
                dq, dwq = _rms64_bwd(dqn, qh, rq, wqv)
                dk, dwk = _rms64_bwd(dkn, kh, rk, wkv)
                for i in range(nq):
                    dq_s[i][rows, :] = _lane_pair(dq, i)
                for i in range(nkv):
                    dk_s[i][rows, :] = ck[i][rows, :] + _lane_pair(dk[:, :BLOCK], i)
                    dv_s[i][rows, :] = cv[i][rows, :] + _lane_pair(dvv[:, :BLOCK], i)
                    ck[i][rows, :] = _lane_pair(dk[:, BLOCK:], i)
                    cv[i][rows, :] = _lane_pair(dvv[:, BLOCK:], i)
                return carry[0] + dwq, carry[1] + dwk

            zero = jnp.zeros((1, HEAD_DIM), f32)
            dwq_a, dwk_a = lax.fori_loop(0, d, residue, (zero, zero))
            dwq_ref[...] += dwq_a
            dwk_ref[...] += dwk_a
            for i in range(nq):
                dq_ref[:, i * 128:(i + 1) * 128] = dq_s[i][...]
            for i in range(nkv):
                dk_ref[:, i * 128:(i + 1) * 128] = dk_s[i][...]
                dv_ref[:, i * 128:(i + 1) * 128] = dv_s[i][...]

        @pl.when(n == nb)
        def _():
            for i in range(nkv):
                dk_ref[:, i * 128:(i + 1) * 128] = ck[i][...]
                dv_ref[:, i * 128:(i + 1) * 128] = cv[i][...]

    cl = lambda n: jnp.minimum(n, nb - 1)
    pv = lambda n: jnp.maximum(jnp.minimum(n, nb - 1) - 1, 0)
    col = lambda c0, i: (lambda p, n: (cl(n), c0 + p * nq + i))
    prv = lambda c0, i: (lambda p, n: (pv(n), c0 + p * nq + i))
    blk = lambda f: pl.BlockSpec((R, 128), f)
    w64 = pl.BlockSpec((1, 64), lambda p, n: (0, 0))
    in_specs = [blk(col(plan.q0, i)) for i in range(nq)]
    in_specs += [blk(prv(plan.k0, i)) for i in range(nkv)] + [blk(col(plan.k0, i)) for i in range(nkv)]
    in_specs += [blk(prv(plan.v0, i)) for i in range(nkv)] + [blk(col(plan.v0, i)) for i in range(nkv)]
    in_specs += [pl.BlockSpec((1, 1, hps * 64), lambda p, n: (p, 0, 0)), w64, w64]
    in_specs += [blk(col(0, i)) for i in range(nq)] * 4 + [w64, w64]
    kvw = 128 * nkv
    out_specs = [pl.BlockSpec((R, 128 * nq), lambda p, n: (cl(n), p)),
                 pl.BlockSpec((R, kvw), lambda p, n: (jnp.maximum(n - 1, 0), p)),
                 pl.BlockSpec((R, kvw), lambda p, n: (jnp.maximum(n - 1, 0), p)), w64, w64]
    same = lambda a: [a] * nq
    return pl.pallas_call(
        body, grid=(P, nb + 1), in_specs=in_specs, out_specs=out_specs,
        out_shape=[jax.ShapeDtypeStruct((L, 512), f32), jax.ShapeDtypeStruct((L, kvw * P), f32),
                   jax.ShapeDtypeStruct((L, kvw * P), f32), jax.ShapeDtypeStruct((1, 64), f32),
                   jax.ShapeDtypeStruct((1, 64), f32)],
        scratch_shapes=[pltpu.VMEM((R, 128), f32)] * (nq + 4 * nkv),
        compiler_params=_cp("arbitrary", "arbitrary"), name=name,
    )(*([hin] * (nq + 4 * nkv)), plan.slopes, wq, wk, *same(o), *same(lse), *same(do), *same(dlse), *dw0)


def _head_sum(x):
    c = x.shape[1]
    r = lax.broadcasted_iota(jnp.int32, (c, c), 0) // HEAD_DIM
    q = lax.broadcasted_iota(jnp.int32, (c, c), 1) // HEAD_DIM
    return _dot(x, (r == q).astype(f32), precision=HI)


def attn_merge_fwd(oa, la, obs, lbs, sinkb, name):
    L = oa.shape[0]

    def fn(i, nb, oa_t, la_t, o1, o2, o3, l1, l2, l3, sk):
        ya = oa_t * jax.nn.sigmoid(la_t - sk)
        m = jnp.maximum(jnp.maximum(l1, l2), l3)
        e1, e2, e3 = jnp.exp(l1 - m), jnp.exp(l2 - m), jnp.exp(l3 - m)
        yb = (e1 * o1 + e2 * o2 + e3 * o3) / (e1 + e2 + e3)
        return jnp.concatenate([ya, yb], axis=1)

    rows = [(a, 0, 512, "cur") for a in (oa, la, *obs, *lbs)]
    return rowwise(fn, name=name, L=L, tm=_pick(L, 256, SUBLANE), rows=rows, consts=[sinkb], outs=[(1024, bf16)])


def attn_merge_bwd(dcat, oa, la, obs, lbs, sinkb, name):
    L = oa.shape[0]

    def fn(i, nb, da, db, oa_t, la_t, o1, o2, o3, l1, l2, l3, sk):
        keep = jax.nn.sigmoid(la_t - sk)
        dla = _head_sum(da * oa_t) * keep * (1.0 - keep)
        m = jnp.maximum(jnp.maximum(l1, l2), l3)
        e1, e2, e3 = jnp.exp(l1 - m), jnp.exp(l2 - m), jnp.exp(l3 - m)
        z = e1 + e2 + e3
        w1, w2, w3 = e1 / z, e2 / z, e3 / z
        g1, g2, g3 = _head_sum(db * o1), _head_sum(db * o2), _head_sum(db * o3)
        gm = w1 * g1 + w2 * g2 + w3 * g3
        return (da * keep, dla, w1 * db, w2 * db, w3 * db,
                w1 * (g1 - gm), w2 * (g2 - gm), w3 * (g3 - gm), -_colsum(dla))

    rows = [(dcat, 0, 512, "cur"), (dcat, 512, 512, "cur")] + [(a, 0, 512, "cur") for a in (oa, la, *obs, *lbs)]
    return rowwise(fn, name=name, L=L, tm=_pick(L, 256, SUBLANE), rows=rows, consts=[sinkb],
                   outs=[(512, f32)] * 8, sums=[(1, 512)])


def attn_assemble(dqa, dka, dva, dqs, dks, dvs, name):
    L = dqa.shape[0]

    def fn(i, nb, qa, ka, va, q1, q2, q3, k1, k2, k3, v1, v2, v3):
        return jnp.concatenate([qa, ka, va, q1 + q2 + q3, k1 + k2 + k3, v1 + v2 + v3], axis=1)

    rows = [(dqa, 0, 512, "cur"), (dka, 0, 128, "cur"), (dva, 0, 128, "cur")]
    rows += [(a, 0, 512, "cur") for a in (*dqs, *dks, *dvs)]
    return rowwise(fn, name=name, L=L, tm=_pick(L, 256, SUBLANE), rows=rows, outs=[(ATTN_IN, bf16)])


def attention_fwd(hin, wqa, wka, wqb, wkb, sinkb):
    oa, la = attn2_fwd(hin, _Plan(1, True, 4), wqa, wka, "attn_a_fwd")
    obs, lbs = [], []
    for _, d in B_BRANCHES:
        o, l = attn2_fwd(hin, _Plan(d, False, 2), wqb, wkb, f"attn_b{d}_fwd")
        obs.append(o)
        lbs.append(l)
    ocat = attn_merge_fwd(oa, la, obs, lbs, sinkb, "attn_merge_fwd")
    return ocat, (oa, la, obs, lbs)


def attention_bwd(hin, wqa, wka, wqb, wkb, sinkb, saved, dcat):
    oa, la, obs, lbs = saved
    res = attn_merge_bwd(dcat, oa, la, obs, lbs, sinkb, "attn_merge_bwd")
    doa, dla, dos, dls, dsink = res[0], res[1], res[2:5], res[5:8], res[8]
    zero = jnp.zeros((1, 64), f32)
    dqa, dka, dva, dwqa, dwka = attn2_bwd(hin, _Plan(1, True, 4), wqa, wka, oa, la, doa, dla, (zero, zero), "attn_a_bwd")
    dqs, dks, dvs = [], [], []
    dwqb = dwkb = zero
    for g, (_, d) in enumerate(B_BRANCHES):
        dq, dk, dv, dwqb, dwkb = attn2_bwd(hin, _Plan(d, False, 2 if d < 16 else 1), wqb, wkb, obs[g], lbs[g],
                                           dos[g], dls[g], (dwqb, dwkb), f"attn_b{d}_bwd")
        dqs.append(dq)
        dks.append(dk)
        dvs.append(dv)
    dhin = attn_assemble(dqa, dka, dva, dqs, dks, dvs, "attn_assemble")
    return dhin, dwqa, dwka, dwqb, dwkb, dsink


NS = S5_GROUPS * S5_STATE


def _s5_param_fn(lr, li, ldt):
    dt = jnp.exp(ldt)
    mag, ang = jnp.exp(lr * dt), li * dt
    ab_re, ab_im = mag * jnp.cos(ang), mag * jnp.sin(ang)
    nr, ni = ab_re - 1.0, ab_im
    den = lr * lr + li * li
    return ab_re, ab_im, (nr * lr + ni * li) / den, (ni * lr - nr * li) / den


def s5_params_fwd(lr, li, ldt):
    def body(lr_ref, li_ref, ldt_ref, *outs):
        for o_ref, o in zip(outs, _s5_param_fn(lr_ref[...], li_ref[...], ldt_ref[...])):
            o_ref[...] = o

    return pl.pallas_call(body, out_shape=[jax.ShapeDtypeStruct(lr.shape, f32)] * 4, name="s5_params_fwd")(lr, li, ldt)


def s5_params_bwd(lr, li, ldt, cts):
    def body(lr_ref, li_ref, ldt_ref, c0, c1, c2, c3, dlr, dli, dldt):
        _, vjp = jax.vjp(_s5_param_fn, lr_ref[...], li_ref[...], ldt_ref[...])
        a, b, c = vjp((c0[...], c1[...], c2[...], c3[...]))
        dlr[...] = a
        dli[...] = b
        dldt[...] = c

    return pl.pallas_call(
        body, out_shape=[jax.ShapeDtypeStruct(lr.shape, f32), jax.ShapeDtypeStruct(li.shape, f32),
                         jax.ShapeDtypeStruct(ldt.shape, f32)], name="s5_params_bwd")(lr, li, ldt, *cts)


def _cmul(ar, ai, br, bi):
    return ar * br - ai * bi, ar * bi + ai * br


def s5_scan(z, ab_re, ab_im, f_re, f_im, *, reverse, name):
    L = z.shape[0]
    tm = _pick(L, 256, SUBLANE)
    nb = L // tm
    ng = tm // SUBLANE
    use_f = f_re is not None
    consts = [ab_re, ab_im] + ([f_re, f_im] if use_f else [])

    def body(*refs):
        z_ref = refs[0]
        c_refs = refs[1:1 + len(consts)]
        x_ref, car = refs[1 + len(consts)], refs[2 + len(consts)]
        i = pl.program_id(0)

        @pl.when(i == 0)
        def _():
            car[...] = jnp.zeros_like(car)

        a1 = (c_refs[0][...], c_refs[1][...])
        a2 = _cmul(*a1, *a1)
        a3 = _cmul(*a2, *a1)
        a4 = _cmul(*a2, *a2)
        pw = [a1, a2, a3, a4, _cmul(*a4, *a1), _cmul(*a4, *a2), _cmul(*a4, *a3), _cmul(*a4, *a4)]
        if reverse:
            pw = pw[::-1]
        pw_re = _stack_rows([p[0] for p in pw])
        pw_im = _stack_rows([p[1] for p in pw])
        ridx = lax.broadcasted_iota(jnp.int32, (SUBLANE, NS), 0)
        if use_f:
            fr, fi = c_refs[2][...], c_refs[3][...]

        def group(s, carry):
            cr, ci = carry
            g = (ng - 1 - s) if reverse else s
            r0 = pl.multiple_of(g * SUBLANE, SUBLANE)
            xr = z_ref[pl.ds(r0, SUBLANE), 0:NS]
            xi = z_ref[pl.ds(r0, SUBLANE), NS:2 * NS]
            if use_f:
                xr, xi = _cmul(fr, fi, xr, xi)
            for sft, (pr, pi) in ((1, a1), (2, a2), (4, a4)):
                if reverse:
                    keep = ridx < SUBLANE - sft
                    sr = jnp.where(keep, pltpu.roll(xr, SUBLANE - sft, 0), 0.0)
                    si = jnp.where(keep, pltpu.roll(xi, SUBLANE - sft, 0), 0.0)
                else:
                    keep = ridx >= sft
                    sr = jnp.where(keep, pltpu.roll(xr, sft, 0), 0.0)
                    si = jnp.where(keep, pltpu.roll(xi, sft, 0), 0.0)
                tr, ti = _cmul(pr, pi, sr, si)
                xr, xi = xr + tr, xi + ti
            tr, ti = _cmul(pw_re, pw_im, cr, ci)
            xr, xi = xr + tr, xi + ti
            x_ref[pl.ds(r0, SUBLANE), 0:NS] = xr
            x_ref[pl.ds(r0, SUBLANE), NS:2 * NS] = xi
            row = 0 if reverse else SUBLANE - 1
            return xr[row:row + 1, :], xi[row:row + 1, :]

        cr, ci = lax.fori_loop(0, ng, group, (car[0:1, 0:NS], car[0:1, NS:2 * NS]))
        car[0:1, 0:NS] = cr
        car[0:1, NS:2 * NS] = ci

    blk = (lambda i: (nb - 1 - i, 0)) if reverse else (lambda i: (i, 0))
    return pl.pallas_call(
        body, grid=(nb,),
        in_specs=[pl.BlockSpec((tm, 2 * NS), blk)] + [pl.BlockSpec((1, NS), lambda i: (0, 0))] * len(consts),
        out_specs=pl.BlockSpec((tm, 2 * NS), blk),
        out_shape=jax.ShapeDtypeStruct((L, 2 * NS), f32),
        scratch_shapes=[pltpu.VMEM((SUBLANE, 2 * NS), f32)],
        compiler_params=_cp("arbitrary"), name=name,
    )(z, *consts)


def _s5_post_fn(ypre, u, dvec, gw, gb):
    y = ypre + dvec * u
    g = jax.nn.gelu(y)
    z = _dot(g.astype(bf16), gw.astype(bf16)) + gb
    return g * jax.nn.sigmoid(z)


def s5_post_fwd(ypre, hin, dvec, gw, gb):
    L = ypre.shape[0]

    def fn(i, nb, yt, ut, dv, gwv, gbv):
        return _s5_post_fn(yt, ut, dv, gwv, gbv)

    return rowwise(fn, name="s5_post_fwd", L=L, tm=_pick(L, 512, SUBLANE),
                   rows=[(ypre, 0, S5_WIDTH, "cur"), (hin, 3072, S5_WIDTH, "cur")],
                   consts=[dvec, gw, gb], outs=[(S5_WIDTH, f32)])


def s5_post_bwd(ypre, hin, dvec, gw, gb, dycat):
    L = ypre.shape[0]

    def fn(i, nb, yt, ut, dyt, dv, gwv, gbv):
        _, vjp = jax.vjp(_s5_post_fn, yt, ut, dv, gwv, gbv)
        return vjp(dyt)

    return rowwise(fn, name="s5_post_bwd", L=L, tm=_pick(L, 512, SUBLANE),
                   rows=[(ypre, 0, S5_WIDTH, "cur"), (hin, 3072, S5_WIDTH, "cur"), (dycat, 0, S5_WIDTH, "cur")],
                   consts=[dvec, gw, gb], outs=[(S5_WIDTH, f32)] * 2,
                   sums=[(1, S5_WIDTH), (S5_WIDTH, S5_WIDTH), (1, S5_WIDTH)])


def s5_acc(G, X, bu, f_re, f_im):
    L = G.shape[0]

    def fn(i, nb, g, x, b, xp8, fr, fi):
        gr, gi = g[:, :NS], g[:, NS:]
        xp = _shift_down(x, xp8 * (i > 0).astype(f32), 1)
        xr, xi = xp[:, :NS], xp[:, NS:]
        br, bi = b[:, :NS], b[:, NS:]
        dbu = jnp.concatenate([fr * gr + fi * gi, fr * gi - fi * gr], axis=1)
        return (dbu, _colsum(xr * gr + xi * gi), _colsum(xr * gi - xi * gr),
                _colsum(br * gr + bi * gi), _colsum(br * gi - bi * gr))

    return rowwise(fn, name="s5_acc", L=L, tm=_pick(L, 256, SUBLANE),
                   rows=[(G, 0, 2 * NS, "cur"), (X, 0, 2 * NS, "cur"), (bu, 0, 2 * NS, "cur"), (X, 0, 2 * NS, "prev")],
                   consts=[f_re, f_im], outs=[(2 * NS, bf16)], sums=[(1, NS)] * 4)


def _s5_blockdiag(b_re, b_im, c_re, c_im):
    eye = jnp.eye(S5_GROUPS, dtype=f32)
    bb = lambda b: jnp.einsum("gpi,gh->gihp", b, eye).reshape(S5_WIDTH, NS)
    cc = lambda c: jnp.einsum("gip,gh->gphi", c, eye).reshape(NS, S5_WIDTH)
    return jnp.concatenate([bb(b_re), bb(b_im)], axis=1), jnp.concatenate([cc(c_re), -cc(c_im)], axis=0)


def _s5_blockdiag_grads(dB, dC):
    gb = lambda m: jnp.einsum("gigp->gpi", m.reshape(S5_GROUPS, S5_GROUP, S5_GROUPS, S5_STATE))
    gc = lambda m: jnp.einsum("gpgi->gip", m.reshape(S5_GROUPS, S5_STATE, S5_GROUPS, S5_GROUP))
    return gb(dB[:, :NS]), gb(dB[:, NS:]), gc(dC[:NS]), -gc(dC[NS:])


def s5_fwd(hin, prm):
    ab_re, ab_im, f_re, f_im = s5_params_fwd(prm["lr"], prm["li"], prm["ldt"])
    flat = lambda a: a.reshape(1, NS)
    ab_re, ab_im, f_re, f_im = flat(ab_re), flat(ab_im), flat(f_re), flat(f_im)
    Bblk, Cblk = _s5_blockdiag(prm["b_re"], prm["b_im"], prm["c_re"], prm["c_im"])
    bu = mm(hin, Bblk, name="s5_bu", a_win=(3072, S5_WIDTH))
    X = s5_scan(bu, ab_re, ab_im, f_re, f_im, reverse=False, name="s5_scan_fwd")
    ypre = mm(X, Cblk, name="s5_y")
    yc = s5_post_fwd(ypre, hin, prm["d"], prm["gw"], prm["gb"])
    return yc, (ab_re, ab_im, f_re, f_im, Bblk, Cblk, bu, X, ypre)


def s5_bwd(hin, prm, saved, dycat):
    ab_re, ab_im, f_re, f_im, Bblk, Cblk, bu, X, ypre = saved
    dypre, du_skip, dd, dgw, dgb = s5_post_bwd(ypre, hin, prm["d"], prm["gw"], prm["gb"], dycat)
    dX = mm(dypre, Cblk, tb=True, name="s5_dx")
    dC = mm(X, dypre, ta=True, name="s5_dc")
    G = s5_scan(dX, ab_re, -ab_im, None, None, reverse=True, name="s5_scan_bwd")
    dbu, dar, dai, dfr, dfi = s5_acc(G, X, bu, f_re, f_im)
    dB = mm(hin, dbu, ta=True, a_win=(3072, S5_WIDTH), name="s5_db")
    du_b = mm(dbu, Bblk, tb=True, name="s5_du")
    sh = prm["lr"].shape
    dlr, dli, dldt = s5_params_bwd(prm["lr"], prm["li"], prm["ldt"],
                                   [a.reshape(sh) for a in (dar, dai, dfr, dfi)])
    db_re, db_im, dc_re, dc_im = _s5_blockdiag_grads(dB, dC)
    grads = dict(lr=dlr, li=dli, ldt=dldt, b_re=db_re, b_im=db_im, c_re=dc_re, c_im=dc_im, d=dd, gw=dgw, gb=dgb)
    return du_skip, du_b, grads


DN_W = DN_HEADS * DN_DK
QKV_W = 3 * DN_W


def _softplus(x):
    return jnp.maximum(x, 0.0) + jnp.log(1.0 + jnp.exp(-jnp.abs(x)))


def _dn_pre(c, ab, alog, dtb):
    s = _silu(c)
    parts = []
    for h in range(2 * DN_HEADS):
        sh = s[:, h * 128:(h + 1) * 128]
        scale = DN_DK ** -0.5 if h < DN_HEADS else 1.0
        parts.append(sh * (lax.rsqrt(jnp.sum(sh * sh, axis=-1, keepdims=True) + EPS) * scale))
    parts.append(s[:, 2 * DN_W:])
    g = -jnp.exp(alog) * _softplus(ab[:, :128] + dtb)
    beta = jax.nn.sigmoid(ab[:, 128:])
    return jnp.concatenate(parts, axis=1), jnp.concatenate([g, beta], axis=1)


def _dn_pre_bwd(c, ab, alog, dtb, dqkv, dgb):
    sg = jax.nn.sigmoid(c)
    s = c * sg
    parts = []
    for h in range(2 * DN_HEADS):
        sh = s[:, h * 128:(h + 1) * 128]
        dy = dqkv[:, h * 128:(h + 1) * 128]
        scale = DN_DK ** -0.5 if h < DN_HEADS else 1.0
        r = lax.rsqrt(jnp.sum(sh * sh, axis=-1, keepdims=True) + EPS)
        parts.append(scale * r * (dy - sh * (r * r) * jnp.sum(dy * sh, axis=-1, keepdims=True)))
    parts.append(dqkv[:, 2 * DN_W:])
    dc = jnp.concatenate(parts, axis=1) * (sg * (1.0 + c * (1.0 - sg)))
    pre = ab[:, :128] + dtb
    ea = jnp.exp(alog)
    dg = dgb[:, :128]
    da = dg * (-ea) * jax.nn.sigmoid(pre)
    dalog = _colsum(dg * (-ea) * _softplus(pre))
    beta = jax.nn.sigmoid(ab[:, 128:])
    db = dgb[:, 128:] * beta * (1.0 - beta)
    return dc, jnp.concatenate([da, db], axis=1), dalog, _colsum(da)


def dn_pre_fwd(hin, conv_w, alog, dtb):
    L = hin.shape[0]

    def fn(i, nb, x, ab, p8, w, al, db):
        c = _conv_causal(x, p8 * (i > 0).astype(f32), w)
        return _dn_pre(c, ab, al, db)

    return rowwise(fn, name="dn_pre_fwd", L=L, tm=_pick(L, 256, SUBLANE),
                   rows=[(hin, 0, QKV_W, "cur"), (hin, 3328, 256, "cur"), (hin, 0, QKV_W, "prev")],
                   consts=[conv_w, alog, dtb], outs=[(QKV_W, f32), (256, f32)])


def dn_pre_bwd(hin, conv_w, alog, dtb, dqkv3, dg, dbeta):
    L = hin.shape[0]

    def fn(i, nb, x, ab, dq, dk, dv, dgt, dbt, p8, w, al, db):
        c = _conv_causal(x, p8 * (i > 0).astype(f32), w)
        return _dn_pre_bwd(c, ab, al, db, jnp.concatenate([dq, dk, dv], axis=1), jnp.concatenate([dgt, dbt], axis=1))

    rows = [(hin, 0, QKV_W, "cur"), (hin, 3328, 256, "cur")] + [(a, 0, DN_W, "cur") for a in dqkv3]
    rows += [(dg, 0, 128, "cur"), (dbeta, 0, 128, "cur"), (hin, 0, QKV_W, "prev")]
    return rowwise(fn, name="dn_pre_bwd", L=L, tm=_pick(L, 128, SUBLANE), rows=rows,
                   consts=[conv_w, alog, dtb], outs=[(QKV_W, f32), (256, f32)], sums=[(1, 128), (1, 128)])


def _split(a):
    hi = a.astype(bf16)
    return hi, (a - hi.astype(f32)).astype(bf16)


def _dot3_raw(a, b, dims):
    ah, al = _split(a)
    bh, bl = _split(b)
    return _dot(ah, bh, dims) + (_dot(ah, bl, dims) + _dot(al, bh, dims))


@functools.partial(jax.custom_vjp, nondiff_argnums=(2,))
def _dot3(a, b, dims=NN):
    return _dot3_raw(a, b, dims)


def _dot3_fwd(a, b, dims):
    return _dot3_raw(a, b, dims), (a, b)


BNN = (((2,), (1,)), ((0,), (0,)))
BNT = (((2,), (2,)), ((0,), (0,)))
BTN = (((1,), (1,)), ((0,), (0,)))


def _dot_bwd(raw, dims, res, g):
    a, b = res
    nn, nt, tn = (BNN, BNT, BTN) if dims[1][0] else (NN, NT, TN)
    if dims == nn:
        return raw(g, b, nt), raw(a, g, tn)
    if dims == nt:
        return raw(g, b, nn), raw(g, a, tn)
    assert dims == tn
    return raw(b, g, nt), raw(a, g, nn)


_dot3.defvjp(_dot3_fwd, functools.partial(_dot_bwd, _dot3_raw))


def _dot1_raw(a, b, dims):
    return _dot(a.astype(bf16), b.astype(bf16), dims)


@functools.partial(jax.custom_vjp, nondiff_argnums=(2,))
def _dot1(a, b, dims=NN):
    return _dot1_raw(a, b, dims)


_dot1.defvjp(lambda a, b, dims: (_dot1_raw(a, b, dims), (a, b)), functools.partial(_dot_bwd, _dot1_raw))


def _dn_chunk(q, k, v, gcol, bcol, S):
    C = q.shape[1]
    r = lax.broadcasted_iota(jnp.int32, (C, C), 0)
    c = lax.broadcasted_iota(jnp.int32, (C, C), 1)
    tril = (r >= c).astype(f32)
    strict = (r > c).astype(f32)
    eye = (r == c).astype(f32)
    hd = _dot3
    grow = jnp.sum(eye * gcol, axis=1, keepdims=True)
    Gcol = jnp.sum(tril * grow, axis=2, keepdims=True)
    Grow = jnp.sum(eye * Gcol, axis=1, keepdims=True)
    gamma = jnp.exp((Gcol - Grow) * tril) * tril
    ld = _dot1
    nmat = strict * bcol * ld(k, k, BNT) * gamma
    T = eye - nmat
    Pw = hd(nmat, nmat, BNN)
    for step in range(5):
        T = T + hd(T, Pw, BNN)
        if step < 4:
            Pw = hd(Pw, Pw, BNN)
    eG = jnp.exp(Gcol)
    u = hd(T, bcol * v, BNN)
    w = hd(T, (bcol * eG) * k, BNN)
    qk = ld(q, k, BNT) * gamma
    vnew = u - ld(w, S, BNN)
    o = ld(q * eG, S, BNN) + ld(qk, vnew, BNN)
    Glast = jnp.sum(gcol, axis=1, keepdims=True)
    S2 = S * jnp.exp(Glast) + ld(k * jnp.exp(Glast - Gcol), vnew, BTN)
    return o, S2


def _heads(x_ref):
    return jnp.stack([x_ref[:, h * 128:(h + 1) * 128] for h in range(DN_HEADS)])


def _head_cols(g_ref):
    return jnp.stack([g_ref[:, h:h + 1] for h in range(DN_HEADS)])


def dn_chunks_fwd(qkvn, gb):
    L = qkvn.shape[0]
    C = DN_CHUNK
    nc = L // C

    def body(q_ref, k_ref, v_ref, g_ref, b_ref, o_ref, sin_ref, S):
        n = pl.program_id(0)

        @pl.when(n == 0)
        def _():
            S[...] = jnp.zeros_like(S)

        s_in = S[...]
        sin_ref[...] = s_in
        o, s2 = _dn_chunk(_heads(q_ref), _heads(k_ref), _heads(v_ref), _head_cols(g_ref), _head_cols(b_ref), s_in)
        for h in range(DN_HEADS):
            o_ref[:, h * 128:(h + 1) * 128] = o[h]
        S[...] = s2

    blk = lambda j: pl.BlockSpec((C, DN_W), lambda n, j=j: (n, j))
    gblk = lambda j: pl.BlockSpec((C, 128), lambda n, j=j: (n, j))
    return pl.pallas_call(
        body, grid=(nc,),
        in_specs=[blk(0), blk(1), blk(2), gblk(0), gblk(1)],
        out_specs=[pl.BlockSpec((C, DN_W), lambda n: (n, 0)),
                   pl.BlockSpec((DN_HEADS, None, 128, 128), lambda n: (0, n, 0, 0))],
        out_shape=[jax.ShapeDtypeStruct((L, DN_W), f32), jax.ShapeDtypeStruct((DN_HEADS, nc, 128, 128), f32)],
        scratch_shapes=[pltpu.VMEM((DN_HEADS, 128, 128), f32)],
        compiler_params=_cp("arbitrary"), name="dn_chunks_fwd",
    )(qkvn, qkvn, qkvn, gb, gb)


def dn_chunks_bwd(qkvn, gb, s_in, do):
    L = qkvn.shape[0]
    C = DN_CHUNK
    nc = L // C

    def body(q_ref, k_ref, v_ref, g_ref, b_ref, sin_ref, do_ref, dq_ref, dk_ref, dv_ref, dg_ref, db_ref, dS):
        n = pl.program_id(0)

        @pl.when(n == 0)
        def _():
            dS[...] = jnp.zeros_like(dS)

        args = (_heads(q_ref), _heads(k_ref), _heads(v_ref), _head_cols(g_ref), _head_cols(b_ref), sin_ref[...])
        _, vjp = jax.vjp(_dn_chunk, *args)
        dq, dk, dv, dg, db, ds = vjp((_heads(do_ref), dS[...]))
        lane = lax.broadcasted_iota(jnp.int32, (C, 128), 1)
        dg_all = jnp.zeros((C, 128), f32)
        db_all = jnp.zeros((C, 128), f32)
        for h in range(DN_HEADS):
            sl = slice(h * 128, (h + 1) * 128)
            dq_ref[:, sl] = dq[h]
            dk_ref[:, sl] = dk[h]
            dv_ref[:, sl] = dv[h]
            dg_all = dg_all + jnp.where(lane == h, dg[h], 0.0)
            db_all = db_all + jnp.where(lane == h, db[h], 0.0)
        dS[...] = ds
        dg_ref[...] = dg_all
        db_ref[...] = db_all

    rv = lambda n: nc - 1 - n
    blk = lambda j: pl.BlockSpec((C, DN_W), lambda n, j=j: (rv(n), j))
    gblk = lambda j: pl.BlockSpec((C, 128), lambda n, j=j: (rv(n), j))
    oblk = pl.BlockSpec((C, DN_W), lambda n: (rv(n), 0))
    gout = pl.BlockSpec((C, 128), lambda n: (rv(n), 0))
    return pl.pallas_call(
        body, grid=(nc,),
        in_specs=[blk(0), blk(1), blk(2), gblk(0), gblk(1),
                  pl.BlockSpec((DN_HEADS, None, 128, 128), lambda n: (0, rv(n), 0, 0)), oblk],
        out_specs=[oblk] * 3 + [gout] * 2,
        out_shape=[jax.ShapeDtypeStruct((L, DN_W), f32)] * 3 + [jax.ShapeDtypeStruct((L, 128), f32)] * 2,
        scratch_shapes=[pltpu.VMEM((DN_HEADS, 128, 128), f32)],
        compiler_params=_cp("arbitrary"), name="dn_chunks_bwd",
    )(qkvn, qkvn, qkvn, gb, gb, s_in, do)


def _dn_post(o, z, w):
    parts = []
    for h in range(DN_HEADS):
        oh = o[:, h * 128:(h + 1) * 128]
        r = lax.rsqrt(jnp.mean(oh * oh, axis=-1, keepdims=True) + EPS)
        parts.append(oh * r * w)
    return jnp.concatenate(parts, axis=1) * _silu(z)


def dn_post_fwd(o, hin, yc, onorm):
    L = o.shape[0]

    def fn(i, nb, ot, zt, yct, w):
        return jnp.concatenate([yct, _dn_post(ot, zt, w)], axis=1)

    return rowwise(fn, name="dn_post_fwd", L=L, tm=_pick(L, 256, SUBLANE),
                   rows=[(o, 0, DN_W, "cur"), (hin, 2304, DN_W, "cur"), (yc, 0, S5_WIDTH, "cur")],
                   consts=[onorm], outs=[(1024, bf16)])


def dn_post_bwd(o, hin, onorm, dycat):
    L = o.shape[0]

    def fn(i, nb, ot, zt, d0, d1, d2, w):
        dy = jnp.concatenate([d0, d1, d2], axis=1)
        sg = jax.nn.sigmoid(zt)
        sz = zt * sg
        dos, dw = [], jnp.zeros((1, 128), f32)
        nrm = []
        for h in range(DN_HEADS):
            sl = slice(h * 128, (h + 1) * 128)
            oh = ot[:, sl]
            r = lax.rsqrt(jnp.mean(oh * oh, axis=-1, keepdims=True) + EPS)
            ohat = oh * r
            t = dy[:, sl] * sz[:, sl]
            dw = dw + _colsum(t * ohat)
            t = t * w
            dos.append(r * (t - ohat * jnp.mean(t * ohat, axis=-1, keepdims=True)))
            nrm.append(ohat * w)
        dz = dy * jnp.concatenate(nrm, axis=1) * (sg * (1.0 + zt * (1.0 - sg)))
        return jnp.concatenate(dos, axis=1), dz, dw

    rows = [(o, 0, DN_W, "cur"), (hin, 2304, DN_W, "cur")] + [(dycat, 256 * (1 + j), 256, "cur") for j in range(3)]
    return rowwise(fn, name="dn_post_bwd", L=L, tm=_pick(L, 256, SUBLANE), rows=rows,
                   consts=[onorm], outs=[(DN_W, f32), (DN_W, f32)], sums=[(1, 128)])


def conv_bwd_win(xarr, start, C, w, dc, name):
    L = xarr.shape[0]
    W = w.shape[0]

    def fn(i, nb, xt, dct, p8, n8, wv):
        dx = _conv_causal_bwd_x(dct, n8 * (i < nb - 1).astype(f32), wv)
        dw = _conv_causal_bwd_w(dct, xt, p8 * (i > 0).astype(f32), W)
        return dx, dw

    return rowwise(fn, name=name, L=L, tm=_pick(L, 128, SUBLANE),
                   rows=[(xarr, start, C, "cur"), (dc, 0, C, "cur"), (xarr, start, C, "prev"), (dc, 0, C, "next")],
                   consts=[w], outs=[(C, bf16)], sums=[(SUBLANE, C)])


def rec_assemble(dx_qkv, dz, du1, du2, dab):
    L = dz.shape[0]

    def fn(i, nb, a, b, c, d, e):
        return jnp.concatenate([a.astype(f32), b, c + d, e], axis=1)

    return rowwise(fn, name="rec_assemble", L=L, tm=_pick(L, 256, SUBLANE),
                   rows=[(dx_qkv, 0, QKV_W, "cur"), (dz, 0, DN_W, "cur"), (du1, 0, 256, "cur"),
                         (du2, 0, 256, "cur"), (dab, 0, 256, "cur")], outs=[(REC_PAD, bf16)])


def deltanet_fwd(hin, prm, yc):
    qkvn, gb = dn_pre_fwd(hin, prm["conv"], prm["alog"], prm["dtb"])
    o, s_in = dn_chunks_fwd(qkvn, gb)
    ycat = dn_post_fwd(o, hin, yc, prm["onorm"])
    return ycat, (qkvn, gb, o, s_in)


def deltanet_bwd(hin, prm, saved, dycat):
    qkvn, gb, o, s_in = saved
    do, dz, donorm = dn_post_bwd(o, hin, prm["onorm"], dycat)
    dq, dk, dv, dgH, dbH = dn_chunks_bwd(qkvn, gb, s_in, do)
    dc, dab, dalog, ddtb = dn_pre_bwd(hin, prm["conv"], prm["alog"], prm["dtb"], (dq, dk, dv), dgH, dbH)
    dx_qkv, dconv = conv_bwd_win(hin, 0, QKV_W, prm["conv"], dc, "dn_conv_bwd")
    return dx_qkv, dz, dab, dict(conv=dconv[:DN_CONV], alog=dalog, dtb=ddtb, onorm=donorm)


AXES = ("x", "y", "c")


def _collective(x, axes, mode, name, nchunk=1):
    k = len(axes)
    P = 2 ** k
    shape = x.shape if mode == "gather" else x.shape[1:]
    rows = shape[0] // nchunk
    assert rows * nchunk == shape[0]

    def body(x_ref, out_ref, send_sems, recv_sems, local_sems):
        co = {a: lax.axis_index(a) for a in AXES}
        me = 0
        for a in axes:
            me = me * 2 + co[a]

        def src(j, q):
            s = x_ref if mode == "gather" else x_ref.at[j]
            return s.at[pl.ds(q * rows, rows)]

        def dst(j, q):
            return out_ref.at[j].at[pl.ds(q * rows, rows)]

        locals_ = [pltpu.make_async_copy(src(me, q), dst(me, q), local_sems.at[q]) for q in range(nchunk)]
        for cp in locals_:
            cp.start()
        sends = []
        for m in range(1, P):
            tco = dict(co)
            t = 0
            for i, a in enumerate(axes):
                if (m >> (k - 1 - i)) & 1:
                    tco[a] = 1 - co[a]
                t = t * 2 + tco[a]
            dev = tuple(tco[a] for a in AXES)
            for q in range(nchunk):
                s = (m - 1) * nchunk + q
                cp = pltpu.make_async_remote_copy(src_ref=src(t, q), dst_ref=dst(me, q), send_sem=send_sems.at[s],
                                                  recv_sem=recv_sems.at[s], device_id=dev, device_id_type=MESH)
                cp.start()
                sends.append((cp, t, q, s, dev))
        for cp, t, q, s, dev in sends:
            pltpu.make_async_remote_copy(src_ref=src(t, q), dst_ref=dst(t, q), send_sem=send_sems.at[s],
                                         recv_sem=recv_sems.at[s], device_id=dev, device_id_type=MESH).wait_recv()
        for cp, *_ in sends:
            cp.wait_send()
        for cp in locals_:
            cp.wait()

    ns = (P - 1) * nchunk
    return pl.pallas_call(
        body,
        in_specs=[pl.BlockSpec(memory_space=pl.ANY)],
        out_specs=pl.BlockSpec(memory_space=pl.ANY),
        out_shape=jax.ShapeDtypeStruct((P,) + tuple(shape), x.dtype),
        scratch_shapes=[pltpu.SemaphoreType.DMA((ns,)), pltpu.SemaphoreType.DMA((ns,)),
                        pltpu.SemaphoreType.DMA((nchunk,))],
        name=name,
    )(x)


def all_gather(x, axes, name, nchunk=1):
    return _collective(x, axes, "gather", name, nchunk)


def exchange(x, axes, name, nchunk=1):
    return _collective(x, axes, "exchange", name, nchunk)


def sum_slots(x, name, out_dtype=f32):
    P, R, C = x.shape
    tr = _pick(R, 256, 2 * SUBLANE)

    def body(x_ref, o_ref):
        acc = x_ref[0].astype(f32)
        for j in range(1, P):
            acc = acc + x_ref[j].astype(f32)
        o_ref[...] = acc.astype(o_ref.dtype)

    return pl.pallas_call(
        body, grid=(R // tr,), in_specs=[pl.BlockSpec((P, tr, C), lambda i: (0, i, 0))],
        out_specs=pl.BlockSpec((tr, C), lambda i: (i, 0)), out_shape=jax.ShapeDtypeStruct((R, C), out_dtype),
        compiler_params=_cp("parallel"), name=name,
    )(x)


def _pack(arrs, width, row_mult, dtype):
    flat = jnp.concatenate([a.astype(dtype).reshape(-1) for a in arrs])
    unit = width * row_mult
    n = -(-flat.shape[0] // unit) * unit
    return jnp.pad(flat, (0, n - flat.shape[0])).reshape(n // width, width)


def _unpack(flat, shapes):
    flat = flat.reshape(-1)
    out, off = [], 0
    for s in shapes:
        n = int(np.prod(s))
        out.append(flat[off:off + n].reshape(s))
        off += n
    return out


def ada_fwd(c_all, ada_w):
    def body(c_ref, w_ref, o_ref):
        cond = _silu(c_ref[...])
        for l in range(ada_w.shape[0]):
            o_ref[l] = _dot(cond, w_ref[l], precision=HI)

    return pl.pallas_call(body, out_shape=jax.ShapeDtypeStruct((ada_w.shape[0], c_all.shape[0], ada_w.shape[2]), f32),
                          compiler_params=pltpu.CompilerParams(vmem_limit_bytes=VMEM_LIMIT), name="ada_fwd")(c_all, ada_w)


def ada_bwd(c_all, dmod):
    def body(c_ref, d_ref, o_ref):
        cond = _silu(c_ref[...])
        for l in range(dmod.shape[0]):
            o_ref[l] = _dot(cond, d_ref[l], TN, precision=HI)

    return pl.pallas_call(body, out_shape=jax.ShapeDtypeStruct((dmod.shape[0], c_all.shape[1], dmod.shape[2]), f32),
                          compiler_params=pltpu.CompilerParams(vmem_limit_bytes=VMEM_LIMIT), name="ada_bwd")(c_all, dmod)


def loss_fwd_bwd(y, target):
    L, D = y.shape

    def fn(i, nb, yt, tt):
        e = yt - tt
        return e * (1.0 / D), jnp.sum(jnp.sum(e * e, axis=1, keepdims=True), axis=0, keepdims=True)

    return rowwise(fn, name="loss", L=L, tm=_pick(L, 512, SUBLANE), rows=[(y, 0, D, "cur"), (target, 0, D, "cur")],
                   outs=[(D, f32)], sums=[(1, 1)])


def adamw(w, g, m, v, name):
    R, C = w.shape

    def fn(i, nb, wt, gt, mt, vt):
        m2 = ADAM_B1 * mt + (1.0 - ADAM_B1) * gt
        v2 = ADAM_B2 * vt + (1.0 - ADAM_B2) * (gt * gt)
        m_hat = m2 / (1.0 - ADAM_B1 ** ADAM_STEP)
        v_hat = v2 / (1.0 - ADAM_B2 ** ADAM_STEP)
        delta = -ADAM_LR * (m_hat / (jnp.sqrt(v_hat) + ADAM_EPS) + ADAM_WD * wt)
        return delta, m2, v2

    return rowwise(fn, name=name, L=R, tm=_pick(R, 256, SUBLANE), rows=[(a, 0, C, "cur") for a in (w, g, m, v)],
                   outs=[(C, f32)] * 3)


W_NAMES = ["ada_w", "ada_b", "norm_mix", "norm_ffn", "attn_w_in", "attn_q_norm_a", "attn_k_norm_a", "attn_q_norm_b",
           "attn_k_norm_b", "attn_sinks", "attn_w_out", "rec_w_in", "s5_lambda_re", "s5_lambda_im", "s5_log_dt",
           "s5_b_re", "s5_b_im", "s5_c_re", "s5_c_im", "s5_d", "s5_glu_w", "s5_glu_b", "dn_conv", "dn_a_log",
           "dn_dt_bias", "dn_out_norm", "rec_w_out", "ffn_w_up", "ffn_conv", "ffn_w_down"]
BIG = ["attn_w_in", "attn_w_out", "rec_w_in", "rec_w_out", "ffn_w_up", "ffn_w_down"]
SMALL_SHARDED = ["s5_d", "s5_glu_w", "s5_glu_b", "dn_conv", "ffn_conv"]
SMALL_REPL = [n for n in W_NAMES if n not in BIG and n not in SMALL_SHARDED and n != "ada_w"]
NSH = 4
GRAD_WIRE = (bf16,)


def _unshard(g, name):
    ax = {"attn_w_in": 2, "attn_w_out": 1, "rec_w_in": 2, "rec_w_out": 1, "ffn_w_up": 2, "ffn_w_down": 1,
          "s5_d": 1, "s5_glu_w": 1, "s5_glu_b": 1, "dn_conv": 2, "ffn_conv": 2}[name]
    g = jnp.moveaxis(g, 0, ax)
    s = g.shape
    return g.reshape(s[:ax] + (s[ax] * s[ax + 1],) + s[ax + 2:])


def _to_shards(full, name):
    ax = {"attn_w_in": 2, "attn_w_out": 1, "rec_w_in": 2, "rec_w_out": 1, "ffn_w_up": 2, "ffn_w_down": 1,
          "s5_d": 1, "s5_glu_w": 1, "s5_glu_b": 1, "dn_conv": 2, "ffn_conv": 2}[name]
    s = full.shape
    g = full.reshape(s[:ax] + (NSH, s[ax] // NSH) + s[ax + 1:])
    return jnp.moveaxis(g, ax, 0)


def _rec_pad_cols(w):
    z6 = jnp.zeros(w.shape[:-1] + (122,), w.dtype)
    return jnp.concatenate([w[..., 256:3328], w[..., 0:256], w[..., 3328:3334], z6, w[..., 3334:3340], z6], axis=-1)


def _rec_unpad_cols(g):
    return jnp.concatenate([g[..., 3072:3328], g[..., 0:3072], g[..., 3328:3334], g[..., 3456:3462]], axis=-1)


def _ffn_fwd(x1, nf, sc, sh, gate, w_up, conv, w_dn, tag):
    h2 = modulate_fwd(x1, nf, sc, sh, f"{tag}_mod2_fwd")
    up = mm(h2, w_up, name=f"{tag}_ffn_up", out_dtypes=(bf16,))
    act = ffn_act_fwd(up, conv, f"{tag}_ffn_act_fwd")
    f, x2 = mm(act, w_dn, name=f"{tag}_ffn_down", out_dtypes=(f32, f32), epi=_resid_epi, epi_mn=[x1], epi_n=[gate])
    return x2, (h2, up, act, f)


def _ffn_bwd(dx, x1, nf, sc, sh, gate, w_up, conv, w_dn, saved, tag):
    h2, up, act, f = saved
    df, dgate = resid_bwd(dx, f, gate, f"{tag}_res2_bwd")
    dact = mm(df, w_dn, tb=True, name=f"{tag}_ffn_dact", out_dtypes=(bf16,))
    dw_dn = mm(act, df, ta=True, name=f"{tag}_ffn_dwdown", out_dtypes=GRAD_WIRE)
    dup, dconv = ffn_act_conv_bwd(up, conv, dact, f"{tag}_ffn_act_conv_bwd")
    dw_up = mm(h2, dup, ta=True, name=f"{tag}_ffn_dwup", out_dtypes=GRAD_WIRE)
    dh2 = mm(dup, w_up, tb=True, name=f"{tag}_ffn_dh")
    dx, dnf, dsc, dsh = modulate_bwd(x1, nf, sc, sh, dh2, dx, f"{tag}_mod2_bwd")
    return dx, dict(nf=dnf, sc=dsc, sh=dsh, gate=dgate, w_up=dw_up, conv=dconv[:FFN_CONV], w_dn=dw_dn)


def kernel(x, c, ada_w, ada_b, norm_mix, norm_ffn, attn_w_in, attn_q_norm_a, attn_k_norm_a, attn_q_norm_b, attn_k_norm_b, attn_sinks, attn_w_out, rec_w_in, s5_lambda_re, s5_lambda_im, s5_log_dt, s5_b_re, s5_b_im, s5_c_re, s5_c_im, s5_d, s5_glu_w, s5_glu_b, dn_conv, dn_a_log, dn_dt_bias, dn_out_norm, rec_w_out, ffn_w_up, ffn_conv, ffn_w_down, loss_target, m_ada_w, m_ada_b, m_norm_mix, m_norm_ffn, m_attn_w_in, m_attn_q_norm_a, m_attn_k_norm_a, m_attn_q_norm_b, m_attn_k_norm_b, m_attn_sinks, m_attn_w_out, m_rec_w_in, m_s5_lambda_re, m_s5_lambda_im, m_s5_log_dt, m_s5_b_re, m_s5_b_im, m_s5_c_re, m_s5_c_im, m_s5_d, m_s5_glu_w, m_s5_glu_b, m_dn_conv, m_dn_a_log, m_dn_dt_bias, m_dn_out_norm, m_rec_w_out, m_ffn_w_up, m_ffn_conv, m_ffn_w_down, v_ada_w, v_ada_b, v_norm_mix, v_norm_ffn, v_attn_w_in, v_attn_q_norm_a, v_attn_k_norm_a, v_attn_q_norm_b, v_attn_k_norm_b, v_attn_sinks, v_attn_w_out, v_rec_w_in, v_s5_lambda_re, v_s5_lambda_im, v_s5_log_dt, v_s5_b_re, v_s5_b_im, v_s5_c_re, v_s5_c_im, v_s5_d, v_s5_glu_w, v_s5_glu_b, v_dn_conv, v_dn_a_log, v_dn_dt_bias, v_dn_out_norm, v_rec_w_out, v_ffn_w_up, v_ffn_conv, v_ffn_w_down):
    args = (ada_w, ada_b, norm_mix, norm_ffn, attn_w_in, attn_q_norm_a, attn_k_norm_a, attn_q_norm_b, attn_k_norm_b, attn_sinks, attn_w_out, rec_w_in, s5_lambda_re, s5_lambda_im, s5_log_dt, s5_b_re, s5_b_im, s5_c_re, s5_c_im, s5_d, s5_glu_w, s5_glu_b, dn_conv, dn_a_log, dn_dt_bias, dn_out_norm, rec_w_out, ffn_w_up, ffn_conv, ffn_w_down)
    ms = (m_ada_w, m_ada_b, m_norm_mix, m_norm_ffn, m_attn_w_in, m_attn_q_norm_a, m_attn_k_norm_a, m_attn_q_norm_b, m_attn_k_norm_b, m_attn_sinks, m_attn_w_out, m_rec_w_in, m_s5_lambda_re, m_s5_lambda_im, m_s5_log_dt, m_s5_b_re, m_s5_b_im, m_s5_c_re, m_s5_c_im, m_s5_d, m_s5_glu_w, m_s5_glu_b, m_dn_conv, m_dn_a_log, m_dn_dt_bias, m_dn_out_norm, m_rec_w_out, m_ffn_w_up, m_ffn_conv, m_ffn_w_down)
    vs = (v_ada_w, v_ada_b, v_norm_mix, v_norm_ffn, v_attn_w_in, v_attn_q_norm_a, v_attn_k_norm_a, v_attn_q_norm_b, v_attn_k_norm_b, v_attn_sinks, v_attn_w_out, v_rec_w_in, v_s5_lambda_re, v_s5_lambda_im, v_s5_log_dt, v_s5_b_re, v_s5_b_im, v_s5_c_re, v_s5_c_im, v_s5_d, v_s5_glu_w, v_s5_glu_b, v_dn_conv, v_dn_a_log, v_dn_dt_bias, v_dn_out_norm, v_rec_w_out, v_ffn_w_up, v_ffn_conv, v_ffn_w_down)
    W = dict(zip(W_NAMES, args))
    Mo = dict(zip(W_NAMES, ms))
    Vo = dict(zip(W_NAMES, vs))
    xi, yi, ci = lax.axis_index("x"), lax.axis_index("y"), lax.axis_index("c")
    shard = 2 * xi + yi
    me8 = 4 * xi + 2 * yi + ci
    xs = x[0]
    target = loss_target[0]
    L, D = xs.shape

    wflat = _pack([W[n] for n in BIG], 1024, 16, bf16)
    wfull = all_gather(wflat, ("x", "y"), "gather_w").reshape(NSH, -1)
    Wf = {}
    off = 0
    for n in BIG:
        sz = int(np.prod(W[n].shape))
        Wf[n] = _unshard(wfull[:, off:off + sz].reshape((NSH,) + W[n].shape), n)
        off += sz
    rec_w_in_p = _rec_pad_cols(Wf["rec_w_in"][0])

    sflat = _pack([c] + [W[n] for n in SMALL_SHARDED], 1024, 8, f32)
    s8 = all_gather(sflat, AXES, "gather_small")
    s8f = s8.reshape(8, -1)
    c_all = s8f[:, :D]
    Ws = {}
    off = D
    for n in SMALL_SHARDED:
        sz = int(np.prod(W[n].shape))
        Ws[n] = _unshard(s8f[0::2, off:off + sz].reshape((NSH,) + W[n].shape), n)
        off += sz

    modp = ada_fwd(c_all, ada_w)
    modg = all_gather(modp, ("x", "y"), "gather_mod")
    mod_all = jnp.moveaxis(modg, 0, 2).reshape(2, 8, -1) + ada_b[:, None, :]
    mod = lax.dynamic_slice(mod_all, (0, me8, 0), (2, 1, mod_all.shape[2]))[:, 0, :]
    mods = [[mod[l:l + 1, j * D:(j + 1) * D] for j in range(6)] for l in range(2)]

    sh1, sc1, g1, sh2, sc2, g2_ = mods[0]
    nm0, nf0 = norm_mix[0:1], norm_ffn[0:1]
    sinkb = jnp.repeat(attn_sinks[0], HEAD_DIM)[None]
    h0 = modulate_fwd(xs, nm0, sc1, sh1, "l0_mod1_fwd")
    hin0 = mm(h0, Wf["attn_w_in"][0], name="l0_in_proj")
    ocat, att_saved = attention_fwd(hin0, attn_q_norm_a, attn_k_norm_a, attn_q_norm_b, attn_k_norm_b, sinkb)
    y0, x1 = mm(ocat, Wf["attn_w_out"][0], name="l0_out_proj", out_dtypes=(f32, f32), epi=_resid_epi,
                epi_mn=[xs], epi_n=[g1])
    x2, ffn0_saved = _ffn_fwd(x1, nf0, sc2, sh2, g2_, Wf["ffn_w_up"][0], Ws["ffn_conv"][0], Wf["ffn_w_down"][0], "l0")

    th1, tc1, t1, th2, tc2, t2 = mods[1]
    nm1, nf1 = norm_mix[1:2], norm_ffn[1:2]
    pad128 = lambda a: jnp.pad(a, ((0, 0), (0, 128 - a.shape[1])))
    s5p = dict(lr=s5_lambda_re[0], li=s5_lambda_im[0], ldt=s5_log_dt[0][:, None], b_re=s5_b_re[0], b_im=s5_b_im[0],
               c_re=s5_c_re[0], c_im=s5_c_im[0], d=Ws["s5_d"], gw=Ws["s5_glu_w"][0], gb=Ws["s5_glu_b"])
    dnp = dict(conv=Ws["dn_conv"][0], alog=pad128(dn_a_log), dtb=pad128(dn_dt_bias), onorm=dn_out_norm)
    h1 = modulate_fwd(x2, nm1, tc1, th1, "l1_mod1_fwd")
    hin1 = mm(h1, rec_w_in_p, name="l1_in_proj")
    yc, s5_saved = s5_fwd(hin1, s5p)
    ycat, dn_saved = deltanet_fwd(hin1, dnp, yc)
    y1, x3 = mm(ycat, Wf["rec_w_out"][0], name="l1_out_proj", out_dtypes=(f32, f32), epi=_resid_epi,
                epi_mn=[x2], epi_n=[t1])
    x4, ffn1_saved = _ffn_fwd(x3, nf1, tc2, th2, t2, Wf["ffn_w_up"][1], Ws["ffn_conv"][1], Wf["ffn_w_down"][1], "l1")

    dx, sse = loss_fwd_bwd(x4, target)
    loss = lax.psum(0.5 * sse[0, 0] / D, AXES)

    dx, gf1 = _ffn_bwd(dx, x3, nf1, tc2, th2, t2, Wf["ffn_w_up"][1], Ws["ffn_conv"][1], Wf["ffn_w_down"][1], ffn1_saved, "l1")
    dy1, dt1 = resid_bwd(dx, y1, t1, "l1_res1_bwd")
    dycat = mm(dy1, Wf["rec_w_out"][0], tb=True, name="l1_dycat")
    dw_rec_out = mm(ycat, dy1, ta=True, name="l1_dwout", out_dtypes=GRAD_WIRE)
    du_skip, du_b, s5g = s5_bwd(hin1, s5p, s5_saved, dycat)
    dx_qkv, dz, dab, dng = deltanet_bwd(hin1, dnp, dn_saved, dycat)
    dhin1 = rec_assemble(dx_qkv, dz, du_skip, du_b, dab)
    dw_rec_in = _rec_unpad_cols(mm(h1, dhin1, ta=True, name="l1_dwin", out_dtypes=GRAD_WIRE))
    dh1 = mm(dhin1, rec_w_in_p, tb=True, name="l1_dh")
    dx, dnm1, dtc1, dth1 = modulate_bwd(x2, nm1, tc1, th1, dh1, dx, "l1_mod1_bwd")

    dx, gf0 = _ffn_bwd(dx, x1, nf0, sc2, sh2, g2_, Wf["ffn_w_up"][0], Ws["ffn_conv"][0], Wf["ffn_w_down"][0], ffn0_saved, "l0")
    dy0, dg1 = resid_bwd(dx, y0, g1, "l0_res1_bwd")
    dcat = mm(dy0, Wf["attn_w_out"][0], tb=True, name="l0_dcat")
    dw_attn_out = mm(ocat, dy0, ta=True, name="l0_dwout", out_dtypes=GRAD_WIRE)
    dhin0, dwqa, dwka, dwqb, dwkb, dsinkb = attention_bwd(hin0, attn_q_norm_a, attn_k_norm_a, attn_q_norm_b,
                                                          attn_k_norm_b, sinkb, att_saved, dcat)
    dw_attn_in = mm(h0, dhin0, ta=True, name="l0_dwin", out_dtypes=GRAD_WIRE)
    dh0 = mm(dhin0, Wf["attn_w_in"][0], tb=True, name="l0_dh")
    grad_x, dnm0, dsc1, dsh1 = modulate_bwd(xs, nm0, sc1, sh1, dh0, dx, "l0_mod1_bwd")

    dmod = jnp.concatenate([
        jnp.concatenate([dsh1, dsc1, dg1, gf0["sh"], gf0["sc"], gf0["gate"]], axis=1),
        jnp.concatenate([dth1, dtc1, dt1, gf1["sh"], gf1["sc"], gf1["gate"]], axis=1)], axis=0)
    gl = {
        "ada_b": dmod,
        "norm_mix": jnp.concatenate([dnm0, dnm1], axis=0),
        "norm_ffn": jnp.concatenate([gf0["nf"], gf1["nf"]], axis=0),
        "attn_q_norm_a": dwqa, "attn_k_norm_a": dwka, "attn_q_norm_b": dwqb, "attn_k_norm_b": dwkb,
        "attn_sinks": dsinkb[:, ::HEAD_DIM],
        "s5_lambda_re": s5g["lr"][None], "s5_lambda_im": s5g["li"][None], "s5_log_dt": s5g["ldt"][:, 0][None],
        "s5_b_re": s5g["b_re"][None], "s5_b_im": s5g["b_im"][None], "s5_c_re": s5g["c_re"][None],
        "s5_c_im": s5g["c_im"][None],
        "dn_a_log": dng["alog"][:, :DN_HEADS], "dn_dt_bias": dng["dtb"][:, :DN_HEADS], "dn_out_norm": dng["onorm"],
        "s5_d": s5g["d"], "s5_glu_w": s5g["gw"][None], "s5_glu_b": s5g["gb"], "dn_conv": dng["conv"][None],
        "ffn_conv": jnp.stack([gf0["conv"], gf1["conv"]]),
        "attn_w_in": dw_attn_in[None], "attn_w_out": dw_attn_out[None], "rec_w_in": dw_rec_in[None],
        "rec_w_out": dw_rec_out[None], "ffn_w_up": jnp.stack([gf0["w_up"], gf1["w_up"]]),
        "ffn_w_down": jnp.stack([gf0["w_dn"], gf1["w_dn"]]),
    }

    small_names = SMALL_REPL + SMALL_SHARDED
    gs = _pack([gl[n] for n in small_names], 128, 256, f32)
    gs8 = all_gather(gs, AXES, "gather_small_grads")
    gsum = sum_slots(gs8, "sum_small_grads")
    full_shapes = [gl[n].shape for n in small_names]
    gfull = dict(zip(small_names, _unpack(gsum, full_shapes)))
    dmod_all = gs8.reshape(8, -1)[:, :2 * 6 * D].reshape(8, 2, 6 * D)
    ncol = ada_w.shape[2]
    dmod_sh = jnp.moveaxis(lax.dynamic_slice(dmod_all, (0, 0, shard * ncol), (8, 2, ncol)), 0, 1)
    grads = {"ada_w": ada_bwd(c_all, dmod_sh)}
    for n in SMALL_REPL:
        grads[n] = gfull[n]
    for n in SMALL_SHARDED:
        sh_all = _to_shards(gfull[n], n)
        grads[n] = lax.dynamic_slice(sh_all, (shard,) + (0,) * (sh_all.ndim - 1), (1,) + sh_all.shape[1:])[0]

    gflat = jnp.concatenate([_to_shards(gl[n], n).reshape(NSH, -1) for n in BIG], axis=1)
    nel = gflat.shape[1]
    unit = 256 * 1024
    npad = -(-nel // unit) * unit
    gflat = jnp.pad(gflat, ((0, 0), (0, npad - nel))).astype(bf16).reshape(NSH, npad // 1024, 1024)
    gq = exchange(gflat, ("x", "y"), "reduce_xy")
    gpart = sum_slots(gq, "sum_chips", bf16)
    gc = all_gather(gpart, ("c",), "gather_grad_c")
    gsh = sum_slots(gc, "sum_pair").reshape(-1)
    off = 0
    for n in BIG:
        sz = int(np.prod(W[n].shape))
        grads[n] = gsh[off:off + sz].reshape(W[n].shape)
        off += sz

    delta, new_m, new_v = {}, {}, {}

    def as2d(a):
        return a.reshape(-1, a.shape[-1])

    for n in ["ada_w"] + BIG:
        d_, m_, v_ = adamw(as2d(W[n]), as2d(grads[n]), as2d(Mo[n]), as2d(Vo[n]), f"adamw_{n}")
        delta[n], new_m[n], new_v[n] = d_.reshape(W[n].shape), m_.reshape(W[n].shape), v_.reshape(W[n].shape)
    pk = lambda dd: _pack([dd[n] for n in small_names], 128, 256, f32)
    d_, m_, v_ = adamw(pk(W), pk(grads), pk(Mo), pk(Vo), "adamw_small")
    shp = [W[n].shape for n in small_names]
    for dst, src in ((delta, d_), (new_m, m_), (new_v, v_)):
        dst.update(zip(small_names, _unpack(src, shp)))

    return (loss, grad_x[None], *[grads[n] for n in W_NAMES], *[delta[n] for n in W_NAMES],
            *[new_m[n] for n in W_NAMES], *[new_v[n] for n in W_NAMES])
```

```python
import functools
import math

import numpy as np
import jax
import jax.numpy as jnp
from jax import lax
from jax.experimental import pallas as pl
from jax.experimental.pallas import tpu as pltpu

f32 = jnp.float32
bf16 = jnp.bfloat16
HI = lax.Precision.HIGHEST
MESH = pl.DeviceIdType.MESH

HEAD_DIM = 64
BLOCK = 128
A_Q_HEADS = 8
A_KV_HEADS = 2
A_WINDOW = 128
B_HEADS = 8
B_BRANCHES = ((128, 1), (512, 4), (2048, 16))
N_ATTN_HEADS = 16
ATTN_IN = 2304
S5_GROUP = 16
S5_GROUPS = 16
S5_WIDTH = 256
S5_STATE = 64
DN_HEADS = 6
DN_DK = 128
DN_CONV = 4
DN_CHUNK = 64
REC_IN = 3340
REC_PAD = 3584
FFN_CONV = 3
EPS = 1e-6
ADAM_LR = 0.001
ADAM_B1 = 0.9
ADAM_B2 = 0.999
ADAM_EPS = 1e-08
ADAM_WD = 0.01
ADAM_STEP = 10

LANE = 128
SUBLANE = 8
VMEM_LIMIT = 52 * 1024 * 1024
MM_FULL_K = 5632
MM_VMEM_BUDGET = 40 * 1024 * 1024


def _cp(*sem):
    return pltpu.CompilerParams(dimension_semantics=sem, vmem_limit_bytes=VMEM_LIMIT)


def _pick(dim, cap, unit=LANE):
    for t in (2048, 1024, 768, 512, 384, 256, 128, 64, 32, 16, 8):
        if t <= cap and t % unit == 0 and dim % t == 0:
            return t
    return dim


def _dot(a, b, dims=(((1,), (0,)), ((), ())), precision=None):
    return lax.dot_general(a, b, dims, precision=precision, preferred_element_type=f32)


NN = (((1,), (0,)), ((), ()))
NT = (((1,), (1,)), ((), ()))
TN = (((0,), (0,)), ((), ()))


def mm(a, b, *, name, ta=False, tb=False, a_win=None, b_win=None, out_dtypes=(f32,),
       epi=None, epi_mn=(), epi_n=(), tm_cap=1024, tn_cap=8192, tk_cap=None, ride=None):
    coll = _Coll(*ride) if ride else None
    a0, a1 = a.shape
    b0, b1 = b.shape
    aw = a_win or (0, a1)
    bw = b_win or (0, b1)
    if ta:
        K, M = a0, aw[1]
    else:
        M, K = a0, aw[1]
    if tb:
        N, K2 = b0, bw[1]
    else:
        K2, N = b0, bw[1]
    assert K == K2, (a.shape, b.shape, ta, tb, a_win, b_win)
    if tk_cap is None:
        tk_cap = K if K <= MM_FULL_K else 2048
    tk = _pick(K, tk_cap, SUBLANE if (ta and not tb) else LANE)
    nk = K // tk
    sa, sb = a.dtype.itemsize, b.dtype.itemsize
    so = sum(jnp.dtype(d).itemsize for d in out_dtypes)
    n_mn, n_n, n_out = len(epi_mn), len(epi_n), len(out_dtypes)

    def vmem(tm_, tn_):
        return 2 * (tm_ * tk * sa + tk * tn_ * sb + tm_ * tn_ * (so + 4 * n_mn)) + 2 * tm_ * tn_ * 4

    best = None
    for tm_ in (t for t in (1024, 512, 256, 128) if M % t == 0 and (not ta or aw[0] % t == 0)):
        for tn_ in (t for t in (N, N // 2, 1024, 768, 512, 384, 256, 128)
                    if t % LANE == 0 and N % t == 0 and (tb or bw[0] % t == 0)):
            if tm_ <= tm_cap and tn_ <= max(tn_cap, 0) and vmem(tm_, tn_) <= MM_VMEM_BUDGET:
                if best is None or (tm_ * tn_, tn_) > (best[0] * best[1], best[1]):
                    best = (tm_, tn_)
    assert best is not None, (name, M, N, K)
    tm, tn = best
    b_outer = tk * tn * sb > tm * tk * sa

    def ix(f):
        if b_outer:
            return lambda j, i, k: f(i, j, k)
        return f

    if ta:
        mo = aw[0] // tm
        a_spec = pl.BlockSpec((tk, tm), ix(lambda i, j, k: (k, i + mo)))
    else:
        assert aw[0] % tk == 0
        ko = aw[0] // tk
        a_spec = pl.BlockSpec((tm, tk), ix(lambda i, j, k: (i, k + ko)))
    if tb:
        assert bw[0] % tk == 0
        kob = bw[0] // tk
        b_spec = pl.BlockSpec((tn, tk), ix(lambda i, j, k: (j, k + kob)))
    else:
        no = bw[0] // tn
        b_spec = pl.BlockSpec((tk, tn), ix(lambda i, j, k: (k, j + no)))
    dims = (((0 if ta else 1,), (1 if tb else 0,)), ((), ()))

    gi, gj = M // tm, N // tn
    grid = (gj, gi, nk) if b_outer else (gi, gj, nk)
    n_ride = 1 if coll else 0

    def body(a_ref, b_ref, *rest):
        mn_refs = rest[:n_mn]
        n_refs = rest[n_mn:n_mn + n_n]
        o0 = n_mn + n_n + n_ride
        out_refs = rest[o0:o0 + n_out]
        if coll:
            ride_refs = (rest[o0 - 1], rest[o0 + n_out]) + tuple(rest[-3:])
            pid = [pl.program_id(t) for t in range(3)]

            @pl.when((pid[0] == 0) & (pid[1] == 0) & (pid[2] == 0))
            def _():
                coll.start(*ride_refs)

        def finish(r):
            if epi is None:
                outs = (r,)
            else:
                outs = epi(r, *[m[...] for m in mn_refs], *[v[...] for v in n_refs])
            for o_ref, o in zip(out_refs, outs):
                o_ref[...] = o.astype(o_ref.dtype)

        part = _dot(a_ref[...].astype(bf16), b_ref[...].astype(bf16), dims)
        if nk == 1:
            finish(part)
        else:
            acc = rest[o0 + n_out + n_ride]
            k = pl.program_id(2)

            @pl.when(k == 0)
            def _():
                acc[...] = part

            @pl.when(k > 0)
            def _():
                acc[...] += part

            @pl.when(k == nk - 1)
            def _():
                finish(acc[...])

        if coll:
            @pl.when((pid[0] == grid[0] - 1) & (pid[1] == grid[1] - 1) & (pid[2] == grid[2] - 1))
            def _():
                coll.wait(*ride_refs)

    mn_spec = pl.BlockSpec((tm, tn), ix(lambda i, j, k: (i, j)))
    n_spec = pl.BlockSpec((1, tn), ix(lambda i, j, k: (0, j)))
    hbm = pl.BlockSpec(memory_space=pl.ANY)
    outs = pl.pallas_call(
        body,
        grid=grid,
        in_specs=[a_spec, b_spec] + [mn_spec] * n_mn + [n_spec] * n_n + [hbm] * n_ride,
        out_specs=[mn_spec] * n_out + [hbm] * n_ride,
        out_shape=[jax.ShapeDtypeStruct((M, N), d) for d in out_dtypes] + ([coll.out_shape] if coll else []),
        scratch_shapes=([pltpu.VMEM((tm, tn), f32)] if nk > 1 else []) + (coll.scratch if coll else []),
        compiler_params=_cp(*(["arbitrary"] * 3 if coll else ["parallel", "parallel", "arbitrary"])),
        name=name,
    )(a, b, *epi_mn, *epi_n, *([ride[0]] if coll else []))
    return outs[0] if len(outs) == 1 else tuple(outs)


def rowwise(fn, *, name, L, tm, rows=(), consts=(), outs=(), sums=()):
    nb = L // tm
    in_specs = []
    arrs = []
    for arr, start, width, kind in rows:
        assert start % width == 0, (name, start, width)
        co = start // width
        hr = SUBLANE * (4 // arr.dtype.itemsize)
        hb = tm // hr
        if kind == "cur":
            in_specs.append(pl.BlockSpec((tm, width), lambda i, co=co: (i, co)))
        elif kind == "prev":
            in_specs.append(pl.BlockSpec((hr, width), lambda i, co=co, hb=hb: (jnp.maximum(i * hb - 1, 0), co)))
        else:
            last = L // hr - 1
            in_specs.append(pl.BlockSpec((hr, width), lambda i, co=co, hb=hb, last=last:
                                         (jnp.minimum((i + 1) * hb, last), co)))
        arrs.append(arr)
    for cst in consts:
        assert cst.ndim == 2
        in_specs.append(pl.BlockSpec(cst.shape, lambda i: (0, 0)))
        arrs.append(cst)
    n_rows, n_c, n_o, n_s = len(rows), len(consts), len(outs), len(sums)
    out_specs = [pl.BlockSpec((tm, w), lambda i: (i, 0)) for w, _ in outs]
    out_specs += [pl.BlockSpec(s, lambda i: (0, 0)) for s in sums]
    out_shape = [jax.ShapeDtypeStruct((L, w), d) for w, d in outs]
    out_shape += [jax.ShapeDtypeStruct(s, f32) for s in sums]

    def body(*refs):
        i = pl.program_id(0)
        vals = [r[...] for r in refs[:n_rows + n_c]]
        res = fn(i, nb, *vals)
        if not isinstance(res, (tuple, list)):
            res = (res,)
        o_refs = refs[n_rows + n_c:n_rows + n_c + n_o]
        s_refs = refs[n_rows + n_c + n_o:]
        for o_ref, o in zip(o_refs, res[:n_o]):
            o_ref[...] = o.astype(o_ref.dtype)
        if n_s:
            @pl.when(i == 0)
            def _():
                for s_ref in s_refs:
                    s_ref[...] = jnp.zeros_like(s_ref)

            for s_ref, s in zip(s_refs, res[n_o:]):
                s_ref[...] += s

    res = pl.pallas_call(
        body,
        grid=(nb,),
        in_specs=in_specs,
        out_specs=out_specs,
        out_shape=out_shape,
        compiler_params=_cp("arbitrary" if n_s else "parallel"),
        name=name,
    )(*arrs)
    return res[0] if len(res) == 1 else tuple(res)


def _shift_down(x, prev8, k):
    cat = jnp.concatenate([prev8, x], axis=0)
    return pltpu.roll(cat, k, 0)[prev8.shape[0]:, :]


def _shift_up(x, next8, k):
    n = x.shape[0]
    cat = jnp.concatenate([x, next8], axis=0)
    return pltpu.roll(cat, n + next8.shape[0] - k, 0)[:n, :]


def _colsum(x):
    return jnp.sum(x, axis=0, keepdims=True)


def _silu(x):
    return x * jax.nn.sigmoid(x)


def _modulate_fn(x, nw, sc, sh):
    r = lax.rsqrt(jnp.mean(x * x, axis=-1, keepdims=True) + EPS)
    return (x * r * nw) * (1.0 + sc) + sh


def modulate_fwd(x, nw, sc, sh, name):
    L, D = x.shape

    def fn(i, nb, xt, nwv, scv, shv):
        return _modulate_fn(xt, nwv, scv, shv)

    return rowwise(fn, name=name, L=L, tm=_pick(L, 512, SUBLANE), rows=[(x, 0, D, "cur")],
                   consts=[nw, sc, sh], outs=[(D, bf16)])


def modulate_bwd(x, nw, sc, sh, dh, dx_in, name):
    L, D = x.shape

    def fn(i, nb, xt, dht, dxt, nwv, scv, shv):
        _, vjp = jax.vjp(_modulate_fn, xt, nwv, scv, shv)
        dx, dnw, dsc, dsh = vjp(dht)
        return dxt + dx, dnw, dsc, dsh

    return rowwise(fn, name=name, L=L, tm=_pick(L, 256, SUBLANE),
                   rows=[(x, 0, D, "cur"), (dh, 0, D, "cur"), (dx_in, 0, D, "cur")],
                   consts=[nw, sc, sh], outs=[(D, f32)], sums=[(1, D)] * 3)


def resid_bwd(dx, y, g, name):
    L, D = dx.shape

    def fn(i, nb, dxt, yt, gv):
        return dxt * gv, _colsum(dxt * yt)

    return rowwise(fn, name=name, L=L, tm=_pick(L, 512, SUBLANE),
                   rows=[(dx, 0, D, "cur"), (y, 0, D, "cur")], consts=[g],
                   outs=[(D, bf16)], sums=[(1, D)])


def _resid_epi(acc, xt, gv):
    return acc, xt + gv * acc


def _stack_rows(rows, n=SUBLANE):
    c = rows[0].shape[1]
    ridx = lax.broadcasted_iota(jnp.int32, (n, c), 0)
    out = jnp.zeros((n, c), f32)
    for j, r in enumerate(rows):
        out = out + jnp.where(ridx == j, r, 0.0)
    return out


def _conv_causal(x, prev8, w):
    W = w.shape[0]
    y = x * w[W - 1:W, :]
    for j in range(W - 1):
        y = y + _shift_down(x, prev8, W - 1 - j) * w[j:j + 1, :]
    return y


def _conv_causal_bwd_x(dy, next8, w):
    W = w.shape[0]
    dx = dy * w[W - 1:W, :]
    for j in range(W - 1):
        dx = dx + _shift_up(dy, next8, W - 1 - j) * w[j:j + 1, :]
    return dx


def _conv_causal_bwd_w(dy, x, prev8, W):
    rows = [_colsum(dy * _shift_down(x, prev8, W - 1 - j)) for j in range(W - 1)]
    rows.append(_colsum(dy * x))
    return _stack_rows(rows)


def ffn_act_fwd(up, conv_w, name):
    L, F2 = up.shape
    F = F2 // 2

    def fn(i, nb, u, p8, w):
        c = _conv_causal(u.astype(f32), p8.astype(f32) * (i > 0).astype(f32), w)
        return _silu(c[:, :F]) * c[:, F:]

    return rowwise(fn, name=name, L=L, tm=_pick(L, 128, SUBLANE),
                   rows=[(up, 0, F2, "cur"), (up, 0, F2, "prev")], consts=[conv_w], outs=[(F, bf16)])


def ffn_act_conv_bwd(up, conv_w, dact, name):
    L, F2 = up.shape
    F = F2 // 2
    W = conv_w.shape[0]

    def fn(i, nb, u, da, p8, un8, dan8, w):
        tm, ext = u.shape[0], un8.shape[0]
        more = (i < nb - 1).astype(f32)
        u, da = u.astype(f32), da.astype(f32)
        p8 = p8.astype(f32) * (i > 0).astype(f32)
        c = _conv_causal(jnp.concatenate([u, un8.astype(f32) * more], axis=0), p8, w)
        dae = jnp.concatenate([da, dan8.astype(f32) * more], axis=0)
        a, b = c[:, :F], c[:, F:]
        sg = jax.nn.sigmoid(a)
        dc = jnp.concatenate([dae * b * (sg * (1.0 + a * (1.0 - sg))), dae * a * sg], axis=1)
        dx = dc[:tm] * w[W - 1:W, :]
        for j in range(W - 1):
            dx = dx + pltpu.roll(dc, tm + ext - (W - 1 - j), 0)[:tm] * w[j:j + 1, :]
        return dx, _conv_causal_bwd_w(dc[:tm], u, p8, W)

    return rowwise(fn, name=name, L=L, tm=_pick(L, 128, SUBLANE),
                   rows=[(up, 0, F2, "cur"), (dact, 0, F, "cur"), (up, 0, F2, "prev"), (up, 0, F2, "next"),
                         (dact, 0, F, "next")],
                   consts=[conv_w], outs=[(F2, bf16)], sums=[(SUBLANE, F2)])


ALIBI = [2.0 ** (-8.0 * (i + 1) / N_ATTN_HEADS) for i in range(N_ATTN_HEADS)]
NEG = -1e30


class _Band:
    def __init__(self, dilation, group_a):
        d = dilation
        self.d = d
        self.group_a = group_a
        if group_a:
            self.P, self.qw, self.hps, self.kvw = 1, 512, 8, 128
            self.qcol = lambda p: 0
            self.kcol = lambda p: 4
            self.vcol = lambda p: 5
            self.kv_of = lambda j: j // 4
            self.max_dist = A_WINDOW - 1
            sl = np.repeat(np.asarray(ALIBI[:8], np.float32), HEAD_DIM)[None, None, :]
        else:
            self.P, self.qw, self.hps, self.kvw = 2 * d, 256, 4, 256
            self.qcol = lambda p: lax.div(p, 2) * 9 + 3 + lax.rem(p, 2)
            self.kcol = lambda p: lax.div(p, 2) * 9 + 5 + lax.rem(p, 2)
            self.vcol = lambda p: lax.div(p, 2) * 9 + 7 + lax.rem(p, 2)
            self.kv_of = lambda j: j
            self.max_dist = BLOCK
            per = np.repeat(np.asarray(ALIBI[8:], np.float32), HEAD_DIM).reshape(2, 1, 256)
            sl = np.tile(per, (d, 1, 1))
        self.slopes = jnp.asarray(sl, f32)


def _band_mask(n, d, max_dist):
    qi = lax.broadcasted_iota(jnp.int32, (BLOCK, 2 * BLOCK), 0)
    kj = lax.broadcasted_iota(jnp.int32, (BLOCK, 2 * BLOCK), 1)
    dist = BLOCK + qi - kj
    valid = (dist >= 0) & (dist <= max_dist) & ((n > 0) | (kj >= BLOCK))
    return valid, -(d * dist).astype(f32)


def _rms64(x, w):
    r = lax.rsqrt(jnp.mean(x * x, axis=-1, keepdims=True) + EPS)
    xh = x * r
    return xh * w, xh, r


def _rms64_bwd(dy, xh, r, w):
    t = dy * w
    dw = jnp.sum(jnp.sum(dy * xh, axis=0), axis=0, keepdims=True)
    return r * (t - xh * jnp.mean(t * xh, axis=-1, keepdims=True)), dw


def _heads64(x, heads):
    return jnp.stack([x[:, h * 64:(h + 1) * 64] for h in heads])


def attn_fwd(hv, band, wq, wk, name):
    M = hv.shape[0]
    nb = M // BLOCK
    P, qw, hps = band.P, band.qw, band.hps
    d, max_dist, kv_of = band.d, band.max_dist, band.kv_of
    kv_heads = sorted({kv_of(j) for j in range(hps)})
    kv_pos = {h: i for i, h in enumerate(kv_heads)}
    gqa = len(kv_heads) != hps

    def body(q_ref, kp_ref, kc_ref, vp_ref, vc_ref, sl_ref, wq_ref, wk_ref, o_ref, lse_ref):
        n = pl.program_id(1)
        valid, negd = _band_mask(n, d, max_dist)
        kblk = jnp.concatenate([kp_ref[...], kc_ref[...]], axis=0)
        vblk = jnp.concatenate([vp_ref[...], vc_ref[...]], axis=0)
        q = _heads64(q_ref, range(hps))
        kn = _rms64(_heads64(kblk, kv_heads), wk_ref[...])[0].astype(bf16)
        v = _heads64(vblk, kv_heads).astype(bf16)
        kn_q = jnp.stack([kn[kv_pos[kv_of(j)]] for j in range(hps)]) if gqa else kn
        v_q = jnp.stack([v[kv_pos[kv_of(j)]] for j in range(hps)]) if gqa else v
        qn = _rms64(q, wq_ref[...])[0].astype(bf16)
        slope = jnp.stack([sl_ref[0, :, j * 64:j * 64 + 1] for j in range(hps)])
        s = _dot(qn, kn_q, BNT) * (HEAD_DIM ** -0.5) + slope * negd
        s = jnp.where(valid, s, NEG)
        m = jnp.max(s, axis=-1, keepdims=True)
        p = jnp.exp(s - m)
        l = jnp.sum(p, axis=-1, keepdims=True)
        o = _dot(p.astype(bf16), v_q, BNN) / l
        lse = m + jnp.log(l)
        for j in range(hps):
            o_ref[:, j * 64:(j + 1) * 64] = o[j]
            lse_ref[:, j * 64:(j + 1) * 64] = jnp.broadcast_to(lse[j], (BLOCK, 64))

    qcol, kcol, vcol, kvw = band.qcol, band.kcol, band.vcol, band.kvw
    in_specs = [
        pl.BlockSpec((BLOCK, qw), lambda p, n: (n, qcol(p))),
        pl.BlockSpec((BLOCK, kvw), lambda p, n: (jnp.maximum(n - 1, 0), kcol(p))),
        pl.BlockSpec((BLOCK, kvw), lambda p, n: (n, kcol(p))),
        pl.BlockSpec((BLOCK, kvw), lambda p, n: (jnp.maximum(n - 1, 0), vcol(p))),
        pl.BlockSpec((BLOCK, kvw), lambda p, n: (n, vcol(p))),
        pl.BlockSpec((1, 1, qw), lambda p, n: (p, 0, 0)),
        pl.BlockSpec((1, 64), lambda p, n: (0, 0)),
        pl.BlockSpec((1, 64), lambda p, n: (0, 0)),
    ]
    o_spec = pl.BlockSpec((BLOCK, qw), lambda p, n: (n, p))
    return pl.pallas_call(
        body, grid=(P, nb), in_specs=in_specs, out_specs=[o_spec, o_spec],
        out_shape=[jax.ShapeDtypeStruct((M, P * qw), f32)] * 2,
        compiler_params=_cp("parallel", "parallel"), name=name,
    )(hv, hv, hv, hv, hv, band.slopes, wq, wk)


def attn_bwd(hv, band, wq, wk, o, lse, do, dlse, dw0, name):
    M = hv.shape[0]
    nb = M // BLOCK
    P, qw, hps = band.P, band.qw, band.hps
    d, max_dist, kv_of = band.d, band.max_dist, band.kv_of
    kv_heads = sorted({kv_of(j) for j in range(hps)})
    kv_pos = {h: i for i, h in enumerate(kv_heads)}
    gqa = len(kv_heads) != hps

    def body(q_ref, kp_ref, kc_ref, vp_ref, vc_ref, sl_ref, wq_ref, wk_ref, o_ref, lse_ref, do_ref, dlse_ref,
             dwq0_ref, dwk0_ref, dq_ref, dk_ref, dv_ref, dwq_ref, dwk_ref, ck, cv):
        pp = pl.program_id(0)
        n = pl.program_id(1)

        @pl.when((pp == 0) & (n == 0))
        def _():
            dwq_ref[...] = dwq0_ref[...]
            dwk_ref[...] = dwk0_ref[...]

        @pl.when(n == 0)
        def _():
            ck[...] = jnp.zeros_like(ck)
            cv[...] = jnp.zeros_like(cv)

        @pl.when(n < nb)
        def _():
            valid, negd = _band_mask(n, d, max_dist)
            kblk = jnp.concatenate([kp_ref[...], kc_ref[...]], axis=0)
            vblk = jnp.concatenate([vp_ref[...], vc_ref[...]], axis=0)
            wqv, wkv = wq_ref[...], wk_ref[...]
            hs = range(hps)
            kn_f, kh, rk = _rms64(_heads64(kblk, kv_heads), wkv)
            kn = kn_f.astype(bf16)
            v = _heads64(vblk, kv_heads).astype(bf16)
            kn_q = jnp.stack([kn[kv_pos[kv_of(j)]] for j in hs]) if gqa else kn
            v_q = jnp.stack([v[kv_pos[kv_of(j)]] for j in hs]) if gqa else v
            qn_f, qh, rq = _rms64(_heads64(q_ref, hs), wqv)
            qn = qn_f.astype(bf16)
            col = lambda ref: jnp.stack([ref[:, j * 64:j * 64 + 1] for j in hs])
            slope = jnp.stack([sl_ref[0, :, j * 64:j * 64 + 1] for j in hs])
            s = _dot(qn, kn_q, BNT) * (HEAD_DIM ** -0.5) + slope * negd
            p = jnp.where(valid, jnp.exp(s - col(lse_ref)), 0.0)
            do_h = _heads64(do_ref, hs)
            delta = jnp.sum(do_h * _heads64(o_ref, hs), axis=-1, keepdims=True)
            do_b = do_h.astype(bf16)
            dp = _dot(do_b, v_q, BNT)
            ds = (p * (dp - delta + col(dlse_ref))).astype(bf16)
            dqn = _dot(ds, kn_q, BNN) * (HEAD_DIM ** -0.5)
            dkn_q = _dot(ds, qn, BTN) * (HEAD_DIM ** -0.5)
            dv_q = _dot(p.astype(bf16), do_b, BTN)
            if gqa:
                grp = lambda t: jnp.stack([sum(t[j] for j in hs if kv_of(j) == h) for h in kv_heads])
                dkn_q, dv_q = grp(dkn_q), grp(dv_q)
            dq, dwq_acc = _rms64_bwd(dqn, qh, rq, wqv)
            for j in hs:
                dq_ref[:, j * 64:(j + 1) * 64] = dq[j]
            dwq_ref[...] += dwq_acc
            dk_h, dwk_acc = _rms64_bwd(dkn_q, kh, rk, wkv)
            dwk_ref[...] += dwk_acc
            dk_all = jnp.concatenate([dk_h[i] for i in range(len(kv_heads))], axis=1)
            dv_all = jnp.concatenate([dv_q[i] for i in range(len(kv_heads))], axis=1)
            dk_ref[...] = ck[...] + dk_all[:BLOCK]
            dv_ref[...] = cv[...] + dv_all[:BLOCK]
            ck[...] = dk_all[BLOCK:]
            cv[...] = dv_all[BLOCK:]

        @pl.when(n == nb)
        def _():
            dk_ref[...] = ck[...]
            dv_ref[...] = cv[...]

    qcol, kcol, vcol, kvw = band.qcol, band.kcol, band.vcol, band.kvw
    cl = lambda n: jnp.minimum(n, nb - 1)
    pv = lambda n: jnp.maximum(jnp.minimum(n, nb - 1) - 1, 0)
    o_in = pl.BlockSpec((BLOCK, qw), lambda p, n: (cl(n), p))
    in_specs = [
        pl.BlockSpec((BLOCK, qw), lambda p, n: (cl(n), qcol(p))),
        pl.BlockSpec((BLOCK, kvw), lambda p, n: (pv(n), kcol(p))),
        pl.BlockSpec((BLOCK, kvw), lambda p, n: (cl(n), kcol(p))),
        pl.BlockSpec((BLOCK, kvw), lambda p, n: (pv(n), vcol(p))),
        pl.BlockSpec((BLOCK, kvw), lambda p, n: (cl(n), vcol(p))),
        pl.BlockSpec((1, 1, qw), lambda p, n: (p, 0, 0)),
        pl.BlockSpec((1, 64), lambda p, n: (0, 0)),
        pl.BlockSpec((1, 64), lambda p, n: (0, 0)),
        o_in, o_in, o_in, o_in,
        pl.BlockSpec((1, 64), lambda p, n: (0, 0)),
        pl.BlockSpec((1, 64), lambda p, n: (0, 0)),
    ]
    kv_out = pl.BlockSpec((BLOCK, kvw), lambda p, n: (jnp.maximum(n - 1, 0), p))
    w_out = pl.BlockSpec((1, 64), lambda p, n: (0, 0))
    return pl.pallas_call(
        body, grid=(P, nb + 1), in_specs=in_specs,
        out_specs=[o_in, kv_out, kv_out, w_out, w_out],
        out_shape=[jax.ShapeDtypeStruct((M, P * qw), f32), jax.ShapeDtypeStruct((M, P * kvw), f32),
                   jax.ShapeDtypeStruct((M, P * kvw), f32), jax.ShapeDtypeStruct((1, 64), f32),
                   jax.ShapeDtypeStruct((1, 64), f32)],
        scratch_shapes=[pltpu.VMEM((BLOCK, kvw), f32), pltpu.VMEM((BLOCK, kvw), f32)],
        compiler_params=_cp("arbitrary", "arbitrary"), name=name,
    )(hv, hv, hv, hv, hv, band.slopes, wq, wk, o, lse, do, dlse, *dw0)


class _Plan:
    def __init__(self, dilation, group_a, nq):
        self.d, self.nq = dilation, nq
        if group_a:
            self.P, self.nkv = 1, 1
            self.q0, self.k0, self.v0 = 0, 4, 5
            self.kv_of = lambda j: j // 4
            self.max_dist = A_WINDOW - 1
            slopes = ALIBI[:8]
        else:
            self.P, self.nkv = 4 // nq, nq
            self.q0, self.k0, self.v0 = 6, 10, 14
            self.kv_of = lambda j: j
            self.max_dist = BLOCK
            slopes = ALIBI[8:]
        self.hps = 2 * nq
        sl = np.repeat(np.asarray(slopes, np.float32), HEAD_DIM).reshape(self.P, 1, self.hps * HEAD_DIM)
        self.slopes = jnp.asarray(sl, f32)


def _rows(r, d):
    return pl.ds(r, BLOCK, stride=d) if d > 1 else pl.ds(0, BLOCK)


def _pairs(refs, rows):
    parts = []
    for ref in refs:
        blk = ref[rows, :]
        parts += [blk[:, :HEAD_DIM], blk[:, HEAD_DIM:]]
    return jnp.stack(parts)


def _pairs2(prev_refs, cur_refs, rows):
    parts = []
    for pr, cr in zip(prev_refs, cur_refs):
        blk = jnp.concatenate([pr[rows, :], cr[rows, :]], axis=0)
        parts += [blk[:, :HEAD_DIM], blk[:, HEAD_DIM:]]
    return jnp.stack(parts)


def _lane_pair(t, i):
    return jnp.concatenate([t[2 * i], t[2 * i + 1]], axis=1)


def attn2_fwd(hin, plan, wq, wk, name):
    L = hin.shape[0]
    d, nq, nkv, hps, P = plan.d, plan.nq, plan.nkv, plan.hps, plan.P
    R = BLOCK * d
    nb = L // R
    kv_of, max_dist = plan.kv_of, plan.max_dist
    gqa = 2 * nkv != hps

    def body(*refs):
        q_refs = refs[:nq]
        kp, kc = refs[nq:nq + nkv], refs[nq + nkv:nq + 2 * nkv]
        vp, vc = refs[nq + 2 * nkv:nq + 3 * nkv], refs[nq + 3 * nkv:nq + 4 * nkv]
        sl_ref, wq_ref, wk_ref, o_ref, lse_ref = refs[nq + 4 * nkv:nq + 4 * nkv + 5]
        o_refs = refs[nq + 4 * nkv + 5:2 * nq + 4 * nkv + 5]
        lse_refs = refs[2 * nq + 4 * nkv + 5:]
        n = pl.program_id(1)
        valid, negd = _band_mask(n, d, max_dist)
        slope = jnp.stack([sl_ref[0, :, j * 64:j * 64 + 1] for j in range(hps)])
        wqv, wkv = wq_ref[...], wk_ref[...]

        def residue(r, carry):
            rows = _rows(r, d)
            kn = _rms64(_pairs2(kp, kc, rows), wkv)[0].astype(bf16)
            v = _pairs2(vp, vc, rows).astype(bf16)
            if gqa:
                kn = jnp.stack([kn[kv_of(j)] for j in range(hps)])
                v = jnp.stack([v[kv_of(j)] for j in range(hps)])
            qn = _rms64(_pairs(q_refs, rows), wqv)[0].astype(bf16)
            s = _dot(qn, kn, BNT) * (HEAD_DIM ** -0.5) + slope * negd
            s = jnp.where(valid, s, NEG)
            m = jnp.max(s, axis=-1, keepdims=True)
            p = jnp.exp(s - m)
            l = jnp.sum(p, axis=-1, keepdims=True)
            o = _dot(p.astype(bf16), v, BNN) / l
            lse = jnp.broadcast_to(m + jnp.log(l), (hps, BLOCK, HEAD_DIM))
            for i in range(nq):
                o_refs[i][rows, :] = _lane_pair(o, i)
                lse_refs[i][rows, :] = _lane_pair(lse, i)
            return carry

        lax.fori_loop(0, d, residue, 0)
        for i in range(nq):
            o_ref[:, i * 128:(i + 1) * 128] = o_refs[i][...]
            lse_ref[:, i * 128:(i + 1) * 128] = lse_refs[i][...]

    col = lambda c0, i: (lambda p, n: (n, c0 + p * nq + i))
    prv = lambda c0, i: (lambda p, n: (jnp.maximum(n - 1, 0), c0 + p * nq + i))
    blk = lambda f: pl.BlockSpec((R, 128), f)
    in_specs = [blk(col(plan.q0, i)) for i in range(nq)]
    in_specs += [blk(prv(plan.k0, i)) for i in range(nkv)] + [blk(col(plan.k0, i)) for i in range(nkv)]
    in_specs += [blk(prv(plan.v0, i)) for i in range(nkv)] + [blk(col(plan.v0, i)) for i in range(nkv)]
    in_specs += [pl.BlockSpec((1, 1, hps * 64), lambda p, n: (p, 0, 0)),
                 pl.BlockSpec((1, 64), lambda p, n: (0, 0)), pl.BlockSpec((1, 64), lambda p, n: (0, 0))]
    wide = pl.BlockSpec((R, 128 * nq), lambda p, n: (n, p))
    return pl.pallas_call(
        body, grid=(P, nb), in_specs=in_specs, out_specs=[wide, wide],
        out_shape=[jax.ShapeDtypeStruct((L, 512), f32)] * 2,
        scratch_shapes=[pltpu.VMEM((R, 128), f32)] * (2 * nq),
        compiler_params=_cp("parallel", "parallel"), name=name,
    )(*([hin] * (nq + 4 * nkv)), plan.slopes, wq, wk)


def attn2_bwd(hin, plan, wq, wk, o, lse, do, dlse, dw0, name):
    L = hin.shape[0]
    d, nq, nkv, hps, P = plan.d, plan.nq, plan.nkv, plan.hps, plan.P
    R = BLOCK * d
    nb = L // R
    kv_of, max_dist = plan.kv_of, plan.max_dist
    nkh = 2 * nkv
    gqa = nkh != hps
    n_in = nq + 4 * nkv + 3 + 4 * nq + 2

    def body(*refs):
        q_refs = refs[:nq]
        kp, kc = refs[nq:nq + nkv], refs[nq + nkv:nq + 2 * nkv]
        vp, vc = refs[nq + 2 * nkv:nq + 3 * nkv], refs[nq + 3 * nkv:nq + 4 * nkv]
        b = nq + 4 * nkv
        sl_ref, wq_ref, wk_ref = refs[b:b + 3]
        b += 3
        o_refs, lse_refs = refs[b:b + nq], refs[b + nq:b + 2 * nq]
        do_refs, dlse_refs = refs[b + 2 * nq:b + 3 * nq], refs[b + 3 * nq:b + 4 * nq]
        dwq0_ref, dwk0_ref = refs[b + 4 * nq:b + 4 * nq + 2]
        dq_ref, dk_ref, dv_ref, dwq_ref, dwk_ref = refs[n_in:n_in + 5]
        sc = refs[n_in + 5:]
        dq_s, dk_s, dv_s = sc[:nq], sc[nq:nq + nkv], sc[nq + nkv:nq + 2 * nkv]
        ck, cv = sc[nq + 2 * nkv:nq + 3 * nkv], sc[nq + 3 * nkv:]
        pp = pl.program_id(0)
        n = pl.program_id(1)

        @pl.when((pp == 0) & (n == 0))
        def _():
            dwq_ref[...] = dwq0_ref[...]
            dwk_ref[...] = dwk0_ref[...]

        @pl.when(n == 0)
        def _():
            for c in (*ck, *cv):
                c[...] = jnp.zeros_like(c)

        @pl.when(n < nb)
        def _():
            valid, negd = _band_mask(n, d, max_dist)
            slope = jnp.stack([sl_ref[0, :, j * 64:j * 64 + 1] for j in range(hps)])
            wqv, wkv = wq_ref[...], wk_ref[...]
            hs = range(hps)

            def residue(r, carry):
                rows = _rows(r, d)
                kn_f, kh, rk = _rms64(_pairs2(kp, kc, rows), wkv)
                kn = kn_f.astype(bf16)
                v = _pairs2(vp, vc, rows).astype(bf16)
                if gqa:
                    kn = jnp.stack([kn[kv_of(j)] for j in hs])
                    v = jnp.stack([v[kv_of(j)] for j in hs])
                qn_f, qh, rq = _rms64(_pairs(q_refs, rows), wqv)
                qn = qn_f.astype(bf16)
                s = _dot(qn, kn, BNT) * (HEAD_DIM ** -0.5) + slope * negd
                p = jnp.where(valid, jnp.exp(s - _pairs(lse_refs, rows)[:, :, :1]), 0.0)
                do_h = _pairs(do_refs, rows)
                delta = jnp.sum(do_h * _pairs(o_refs, rows), axis=-1, keepdims=True)
                do_b = do_h.astype(bf16)
                dp = _dot(do_b, v, BNT)
                ds = (p * (dp - delta + _pairs(dlse_refs, rows)[:, :, :1])).astype(bf16)
                dqn = _dot(ds, kn, BNN) * (HEAD_DIM ** -0.5)
                dkn = _dot(ds, qn, BTN) * (HEAD_DIM ** -0.5)
                dvv = _dot(p.astype(bf16), do_b, BTN)
                if gqa:
                    grp = lambda t: jnp.stack([sum(t[j] for j in hs if kv_of(j) == h) for h in range(nkh)])
                    dkn, dvv = grp(dkn), grp(dvv)
                dq, dwq = _rms64_bwd(dqn, qh, rq, wqv)
                dk, dwk = _rms64_bwd(dkn, kh, rk, wkv)
                for i in range(nq):
                    dq_s[i][rows, :] = _lane_pair(dq, i)
                for i in range(nkv):
                    dk_s[i][rows, :] = ck[i][rows, :] + _lane_pair(dk[:, :BLOCK], i)
                    dv_s[i][rows, :] = cv[i][rows, :] + _lane_pair(dvv[:, :BLOCK], i)
                    ck[i][rows, :] = _lane_pair(dk[:, BLOCK:], i)
                    cv[i][rows, :] = _lane_pair(dvv[:, BLOCK:], i)
                return carry[0] + dwq, carry[1] + dwk

            zero = jnp.zeros((1, HEAD_DIM), f32)
            dwq_a, dwk_a = lax.fori_loop(0, d, residue, (zero, zero))
            dwq_ref[...] += dwq_a
            dwk_ref[...] += dwk_a
            for i in range(nq):
                dq_ref[:, i * 128:(i + 1) * 128] = dq_s[i][...]
            for i in range(nkv):
                dk_ref[:, i * 128:(i + 1) * 128] = dk_s[i][...]
                dv_ref[:, i * 128:(i + 1) * 128] = dv_s[i][...]

        @pl.when(n == nb)
        def _():
            for i in range(nkv):
                dk_ref[:, i * 128:(i + 1) * 128] = ck[i][...]
                dv_ref[:, i * 128:(i + 1) * 128] = cv[i][...]

    cl = lambda n: jnp.minimum(n, nb - 1)
    pv = lambda n: jnp.maximum(jnp.minimum(n, nb - 1) - 1, 0)
    col = lambda c0, i: (lambda p, n: (cl(n), c0 + p * nq + i))
    prv = lambda c0, i: (lambda p, n: (pv(n), c0 + p * nq + i))
    blk = lambda f: pl.BlockSpec((R, 128), f)
    w64 = pl.BlockSpec((1, 64), lambda p, n: (0, 0))
    in_specs = [blk(col(plan.q0, i)) for i in range(nq)]
    in_specs += [blk(prv(plan.k0, i)) for i in range(nkv)] + [blk(col(plan.k0, i)) for i in range(nkv)]
    in_specs += [blk(prv(plan.v0, i)) for i in range(nkv)] + [blk(col(plan.v0, i)) for i in range(nkv)]
    in_specs += [pl.BlockSpec((1, 1, hps * 64), lambda p, n: (p, 0, 0)), w64, w64]
    in_specs += [blk(col(0, i)) for i in range(nq)] * 4 + [w64, w64]
    kvw = 128 * nkv
    out_specs = [pl.BlockSpec((R, 128 * nq), lambda p, n: (cl(n), p)),
                 pl.BlockSpec((R, kvw), lambda p, n: (jnp.maximum(n - 1, 0), p)),
                 pl.BlockSpec((R, kvw), lambda p, n: (jnp.maximum(n - 1, 0), p)), w64, w64]
    same = lambda a: [a] * nq
    return pl.pallas_call(
        body, grid=(P, nb + 1), in_specs=in_specs, out_specs=out_specs,
        out_shape=[jax.ShapeDtypeStruct((L, 512), f32), jax.ShapeDtypeStruct((L, kvw * P), f32),
                   jax.ShapeDtypeStruct((L, kvw * P), f32), jax.ShapeDtypeStruct((1, 64), f32),
                   jax.ShapeDtypeStruct((1, 64), f32)],
        scratch_shapes=[pltpu.VMEM((R, 128), f32)] * (nq + 4 * nkv),
        compiler_params=_cp("arbitrary", "arbitrary"), name=name,
    )(*([hin] * (nq + 4 * nkv)), plan.slopes, wq, wk, *same(o), *same(lse), *same(do), *same(dlse), *dw0)


def _head_sum(x):
    c = x.shape[1]
    r = lax.broadcasted_iota(jnp.int32, (c, c), 0) // HEAD_DIM
    q = lax.broadcasted_iota(jnp.int32, (c, c), 1) // HEAD_DIM
    return _dot(x, (r == q).astype(f32), precision=HI)


def attn_merge_fwd(oa, la, obs, lbs, sinkb, name):
    L = oa.shape[0]

    def fn(i, nb, oa_t, la_t, o1, o2, o3, l1, l2, l3, sk):
        ya = oa_t * jax.nn.sigmoid(la_t - sk)
        m = jnp.maximum(jnp.maximum(l1, l2), l3)
        e1, e2, e3 = jnp.exp(l1 - m), jnp.exp(l2 - m), jnp.exp(l3 - m)
        yb = (e1 * o1 + e2 * o2 + e3 * o3) / (e1 + e2 + e3)
        return jnp.concatenate([ya, yb], axis=1)

    rows = [(a, 0, 512, "cur") for a in (oa, la, *obs, *lbs)]
    return rowwise(fn, name=name, L=L, tm=_pick(L, 256, SUBLANE), rows=rows, consts=[sinkb], outs=[(1024, bf16)])


def attn_merge_bwd(dcat, oa, la, obs, lbs, sinkb, name):
    L = oa.shape[0]

    def fn(i, nb, da, db, oa_t, la_t, o1, o2, o3, l1, l2, l3, sk):
        keep = jax.nn.sigmoid(la_t - sk)
        dla = _head_sum(da * oa_t) * keep * (1.0 - keep)
        m = jnp.maximum(jnp.maximum(l1, l2), l3)
        e1, e2, e3 = jnp.exp(l1 - m), jnp.exp(l2 - m), jnp.exp(l3 - m)
        z = e1 + e2 + e3
        w1, w2, w3 = e1 / z, e2 / z, e3 / z
        g1, g2, g3 = _head_sum(db * o1), _head_sum(db * o2), _head_sum(db * o3)
        gm = w1 * g1 + w2 * g2 + w3 * g3
        return (da * keep, dla, w1 * db, w2 * db, w3 * db,
                w1 * (g1 - gm), w2 * (g2 - gm), w3 * (g3 - gm), -_colsum(dla))

    rows = [(dcat, 0, 512, "cur"), (dcat, 512, 512, "cur")] + [(a, 0, 512, "cur") for a in (oa, la, *obs, *lbs)]
    return rowwise(fn, name=name, L=L, tm=_pick(L, 256, SUBLANE), rows=rows, consts=[sinkb],
                   outs=[(512, f32)] * 8, sums=[(1, 512)])


def attn_assemble(dqa, dka, dva, dqs, dks, dvs, name):
    L = dqa.shape[0]

    def fn(i, nb, qa, ka, va, q1, q2, q3, k1, k2, k3, v1, v2, v3):
        return jnp.concatenate([qa, ka, va, q1 + q2 + q3, k1 + k2 + k3, v1 + v2 + v3], axis=1)

    rows = [(dqa, 0, 512, "cur"), (dka, 0, 128, "cur"), (dva, 0, 128, "cur")]
    rows += [(a, 0, 512, "cur") for a in (*dqs, *dks, *dvs)]
    return rowwise(fn, name=name, L=L, tm=_pick(L, 256, SUBLANE), rows=rows, outs=[(ATTN_IN, bf16)])


def attention_fwd(hin, wqa, wka, wqb, wkb, sinkb):
    oa, la = attn2_fwd(hin, _Plan(1, True, 4), wqa, wka, "attn_a_fwd")
    obs, lbs = [], []
    for _, d in B_BRANCHES:
        o, l = attn2_fwd(hin, _Plan(d, False, 2), wqb, wkb, f"attn_b{d}_fwd")
        obs.append(o)
        lbs.append(l)
    ocat = attn_merge_fwd(oa, la, obs, lbs, sinkb, "attn_merge_fwd")
    return ocat, (oa, la, obs, lbs)


def attention_bwd(hin, wqa, wka, wqb, wkb, sinkb, saved, dcat):
    oa, la, obs, lbs = saved
    res = attn_merge_bwd(dcat, oa, la, obs, lbs, sinkb, "attn_merge_bwd")
    doa, dla, dos, dls, dsink = res[0], res[1], res[2:5], res[5:8], res[8]
    zero = jnp.zeros((1, 64), f32)
    dqa, dka, dva, dwqa, dwka = attn2_bwd(hin, _Plan(1, True, 4), wqa, wka, oa, la, doa, dla, (zero, zero), "attn_a_bwd")
    dqs, dks, dvs = [], [], []
    dwqb = dwkb = zero
    for g, (_, d) in enumerate(B_BRANCHES):
        dq, dk, dv, dwqb, dwkb = attn2_bwd(hin, _Plan(d, False, 2 if d < 16 else 1), wqb, wkb, obs[g], lbs[g],
                                           dos[g], dls[g], (dwqb, dwkb), f"attn_b{d}_bwd")
        dqs.append(dq)
        dks.append(dk)
        dvs.append(dv)
    dhin = attn_assemble(dqa, dka, dva, dqs, dks, dvs, "attn_assemble")
    return dhin, dwqa, dwka, dwqb, dwkb, dsink


NS = S5_GROUPS * S5_STATE


def _s5_param_fn(lr, li, ldt):
    dt = jnp.exp(ldt)
    mag, ang = jnp.exp(lr * dt), li * dt
    ab_re, ab_im = mag * jnp.cos(ang), mag * jnp.sin(ang)
    nr, ni = ab_re - 1.0, ab_im
    den = lr * lr + li * li
    return ab_re, ab_im, (nr * lr + ni * li) / den, (ni * lr - nr * li) / den


def s5_params_fwd(lr, li, ldt):
    def body(lr_ref, li_ref, ldt_ref, *outs):
        for o_ref, o in zip(outs, _s5_param_fn(lr_ref[...], li_ref[...], ldt_ref[...])):
            o_ref[...] = o

    return pl.pallas_call(body, out_shape=[jax.ShapeDtypeStruct(lr.shape, f32)] * 4, name="s5_params_fwd")(lr, li, ldt)


def s5_params_bwd(lr, li, ldt, cts):
    def body(lr_ref, li_ref, ldt_ref, c0, c1, c2, c3, dlr, dli, dldt):
        _, vjp = jax.vjp(_s5_param_fn, lr_ref[...], li_ref[...], ldt_ref[...])
        a, b, c = vjp((c0[...], c1[...], c2[...], c3[...]))
        dlr[...] = a
        dli[...] = b
        dldt[...] = c

    return pl.pallas_call(
        body, out_shape=[jax.ShapeDtypeStruct(lr.shape, f32), jax.ShapeDtypeStruct(li.shape, f32),
                         jax.ShapeDtypeStruct(ldt.shape, f32)], name="s5_params_bwd")(lr, li, ldt, *cts)


def _cmul(ar, ai, br, bi):
    return ar * br - ai * bi, ar * bi + ai * br


def s5_scan(z, ab_re, ab_im, f_re, f_im, *, reverse, name):
    L = z.shape[0]
    tm = _pick(L, 256, SUBLANE)
    nb = L // tm
    ng = tm // SUBLANE
    use_f = f_re is not None
    consts = [ab_re, ab_im] + ([f_re, f_im] if use_f else [])

    def body(*refs):
        z_ref = refs[0]
        c_refs = refs[1:1 + len(consts)]
        x_ref, car = refs[1 + len(consts)], refs[2 + len(consts)]
        i = pl.program_id(0)

        @pl.when(i == 0)
        def _():
            car[...] = jnp.zeros_like(car)

        a1 = (c_refs[0][...], c_refs[1][...])
        a2 = _cmul(*a1, *a1)
        a3 = _cmul(*a2, *a1)
        a4 = _cmul(*a2, *a2)
        pw = [a1, a2, a3, a4, _cmul(*a4, *a1), _cmul(*a4, *a2), _cmul(*a4, *a3), _cmul(*a4, *a4)]
        if reverse:
            pw = pw[::-1]
        pw_re = _stack_rows([p[0] for p in pw])
        pw_im = _stack_rows([p[1] for p in pw])
        ridx = lax.broadcasted_iota(jnp.int32, (SUBLANE, NS), 0)
        if use_f:
            fr, fi = c_refs[2][...], c_refs[3][...]

        def group(s, carry):
            cr, ci = carry
            g = (ng - 1 - s) if reverse else s
            r0 = pl.multiple_of(g * SUBLANE, SUBLANE)
            xr = z_ref[pl.ds(r0, SUBLANE), 0:NS]
            xi = z_ref[pl.ds(r0, SUBLANE), NS:2 * NS]
            if use_f:
                xr, xi = _cmul(fr, fi, xr, xi)
            for sft, (pr, pi) in ((1, a1), (2, a2), (4, a4)):
                if reverse:
                    keep = ridx < SUBLANE - sft
                    sr = jnp.where(keep, pltpu.roll(xr, SUBLANE - sft, 0), 0.0)
                    si = jnp.where(keep, pltpu.roll(xi, SUBLANE - sft, 0), 0.0)
                else:
                    keep = ridx >= sft
                    sr = jnp.where(keep, pltpu.roll(xr, sft, 0), 0.0)
                    si = jnp.where(keep, pltpu.roll(xi, sft, 0), 0.0)
                tr, ti = _cmul(pr, pi, sr, si)
                xr, xi = xr + tr, xi + ti
            tr, ti = _cmul(pw_re, pw_im, cr, ci)
            xr, xi = xr + tr, xi + ti
            x_ref[pl.ds(r0, SUBLANE), 0:NS] = xr
            x_ref[pl.ds(r0, SUBLANE), NS:2 * NS] = xi
            row = 0 if reverse else SUBLANE - 1
            return xr[row:row + 1, :], xi[row:row + 1, :]

        cr, ci = lax.fori_loop(0, ng, group, (car[0:1, 0:NS], car[0:1, NS:2 * NS]))
        car[0:1, 0:NS] = cr
        car[0:1, NS:2 * NS] = ci

    blk = (lambda i: (nb - 1 - i, 0)) if reverse else (lambda i: (i, 0))
    return pl.pallas_call(
        body, grid=(nb,),
        in_specs=[pl.BlockSpec((tm, 2 * NS), blk)] + [pl.BlockSpec((1, NS), lambda i: (0, 0))] * len(consts),
        out_specs=pl.BlockSpec((tm, 2 * NS), blk),
        out_shape=jax.ShapeDtypeStruct((L, 2 * NS), f32),
        scratch_shapes=[pltpu.VMEM((SUBLANE, 2 * NS), f32)],
        compiler_params=_cp("arbitrary"), name=name,
    )(z, *consts)


def _s5_post_fn(ypre, u, dvec, gw, gb):
    y = ypre + dvec * u
    g = jax.nn.gelu(y)
    z = _dot(g.astype(bf16), gw.astype(bf16)) + gb
    return g * jax.nn.sigmoid(z)


def s5_post_fwd(ypre, hin, dvec, gw, gb):
    L = ypre.shape[0]

    def fn(i, nb, yt, ut, dv, gwv, gbv):
        return _s5_post_fn(yt, ut, dv, gwv, gbv)

    return rowwise(fn, name="s5_post_fwd", L=L, tm=_pick(L, 512, SUBLANE),
                   rows=[(ypre, 0, S5_WIDTH, "cur"), (hin, 3072, S5_WIDTH, "cur")],
                   consts=[dvec, gw, gb], outs=[(S5_WIDTH, f32)])


def s5_post_bwd(ypre, hin, dvec, gw, gb, dycat):
    L = ypre.shape[0]

    def fn(i, nb, yt, ut, dyt, dv, gwv, gbv):
        _, vjp = jax.vjp(_s5_post_fn, yt, ut, dv, gwv, gbv)
        return vjp(dyt)

    return rowwise(fn, name="s5_post_bwd", L=L, tm=_pick(L, 512, SUBLANE),
                   rows=[(ypre, 0, S5_WIDTH, "cur"), (hin, 3072, S5_WIDTH, "cur"), (dycat, 0, S5_WIDTH, "cur")],
                   consts=[dvec, gw, gb], outs=[(S5_WIDTH, f32)] * 2,
                   sums=[(1, S5_WIDTH), (S5_WIDTH, S5_WIDTH), (1, S5_WIDTH)])


def s5_acc(G, X, bu, f_re, f_im):
    L = G.shape[0]

    def fn(i, nb, g, x, b, xp8, fr, fi):
        gr, gi = g[:, :NS], g[:, NS:]
        xp = _shift_down(x, xp8 * (i > 0).astype(f32), 1)
        xr, xi = xp[:, :NS], xp[:, NS:]
        br, bi = b[:, :NS], b[:, NS:]
        dbu = jnp.concatenate([fr * gr + fi * gi, fr * gi - fi * gr], axis=1)
        return (dbu, _colsum(xr * gr + xi * gi), _colsum(xr * gi - xi * gr),
                _colsum(br * gr + bi * gi), _colsum(br * gi - bi * gr))

    return rowwise(fn, name="s5_acc", L=L, tm=_pick(L, 256, SUBLANE),
                   rows=[(G, 0, 2 * NS, "cur"), (X, 0, 2 * NS, "cur"), (bu, 0, 2 * NS, "cur"), (X, 0, 2 * NS, "prev")],
                   consts=[f_re, f_im], outs=[(2 * NS, bf16)], sums=[(1, NS)] * 4)


def _s5_blockdiag(b_re, b_im, c_re, c_im):
    eye = jnp.eye(S5_GROUPS, dtype=f32)
    bb = lambda b: jnp.einsum("gpi,gh->gihp", b, eye).reshape(S5_WIDTH, NS)
    cc = lambda c: jnp.einsum("gip,gh->gphi", c, eye).reshape(NS, S5_WIDTH)
    return jnp.concatenate([bb(b_re), bb(b_im)], axis=1), jnp.concatenate([cc(c_re), -cc(c_im)], axis=0)


def _s5_blockdiag_grads(dB, dC):
    gb = lambda m: jnp.einsum("gigp->gpi", m.reshape(S5_GROUPS, S5_GROUP, S5_GROUPS, S5_STATE))
    gc = lambda m: jnp.einsum("gpgi->gip", m.reshape(S5_GROUPS, S5_STATE, S5_GROUPS, S5_GROUP))
    return gb(dB[:, :NS]), gb(dB[:, NS:]), gc(dC[:NS]), -gc(dC[NS:])


def s5_fwd(hin, prm):
    ab_re, ab_im, f_re, f_im = s5_params_fwd(prm["lr"], prm["li"], prm["ldt"])
    flat = lambda a: a.reshape(1, NS)
    ab_re, ab_im, f_re, f_im = flat(ab_re), flat(ab_im), flat(f_re), flat(f_im)
    Bblk, Cblk = _s5_blockdiag(prm["b_re"], prm["b_im"], prm["c_re"], prm["c_im"])
    bu = mm(hin, Bblk, name="s5_bu", a_win=(3072, S5_WIDTH))
    X = s5_scan(bu, ab_re, ab_im, f_re, f_im, reverse=False, name="s5_scan_fwd")
    ypre = mm(X, Cblk, name="s5_y")
    yc = s5_post_fwd(ypre, hin, prm["d"], prm["gw"], prm["gb"])
    return yc, (ab_re, ab_im, f_re, f_im, Bblk, Cblk, bu, X, ypre)


def s5_bwd(hin, prm, saved, dycat):
    ab_re, ab_im, f_re, f_im, Bblk, Cblk, bu, X, ypre = saved
    dypre, du_skip, dd, dgw, dgb = s5_post_bwd(ypre, hin, prm["d"], prm["gw"], prm["gb"], dycat)
    dX = mm(dypre, Cblk, tb=True, name="s5_dx")
    dC = mm(X, dypre, ta=True, name="s5_dc")
    G = s5_scan(dX, ab_re, -ab_im, None, None, reverse=True, name="s5_scan_bwd")
    dbu, dar, dai, dfr, dfi = s5_acc(G, X, bu, f_re, f_im)
    dB = mm(hin, dbu, ta=True, a_win=(3072, S5_WIDTH), name="s5_db")
    du_b = mm(dbu, Bblk, tb=True, name="s5_du")
    sh = prm["lr"].shape
    dlr, dli, dldt = s5_params_bwd(prm["lr"], prm["li"], prm["ldt"],
                                   [a.reshape(sh) for a in (dar, dai, dfr, dfi)])
    db_re, db_im, dc_re, dc_im = _s5_blockdiag_grads(dB, dC)
    grads = dict(lr=dlr, li=dli, ldt=dldt, b_re=db_re, b_im=db_im, c_re=dc_re, c_im=dc_im, d=dd, gw=dgw, gb=dgb)
    return du_skip, du_b, grads


DN_W = DN_HEADS * DN_DK
QKV_W = 3 * DN_W


def _softplus(x):
    return jnp.maximum(x, 0.0) + jnp.log(1.0 + jnp.exp(-jnp.abs(x)))


def _dn_pre(c, ab, alog, dtb):
    s = _silu(c)
    parts = []
    for h in range(2 * DN_HEADS):
        sh = s[:, h * 128:(h + 1) * 128]
        scale = DN_DK ** -0.5 if h < DN_HEADS else 1.0
        parts.append(sh * (lax.rsqrt(jnp.sum(sh * sh, axis=-1, keepdims=True) + EPS) * scale))
    parts.append(s[:, 2 * DN_W:])
    g = -jnp.exp(alog) * _softplus(ab[:, :128] + dtb)
    beta = jax.nn.sigmoid(ab[:, 128:])
    return jnp.concatenate(parts, axis=1), jnp.concatenate([g, beta], axis=1)


def _dn_pre_bwd(c, ab, alog, dtb, dqkv, dgb):
    sg = jax.nn.sigmoid(c)
    s = c * sg
    parts = []
    for h in range(2 * DN_HEADS):
        sh = s[:, h * 128:(h + 1) * 128]
        dy = dqkv[:, h * 128:(h + 1) * 128]
        scale = DN_DK ** -0.5 if h < DN_HEADS else 1.0
        r = lax.rsqrt(jnp.sum(sh * sh, axis=-1, keepdims=True) + EPS)
        parts.append(scale * r * (dy - sh * (r * r) * jnp.sum(dy * sh, axis=-1, keepdims=True)))
    parts.append(dqkv[:, 2 * DN_W:])
    dc = jnp.concatenate(parts, axis=1) * (sg * (1.0 + c * (1.0 - sg)))
    pre = ab[:, :128] + dtb
    ea = jnp.exp(alog)
    dg = dgb[:, :128]
    da = dg * (-ea) * jax.nn.sigmoid(pre)
    dalog = _colsum(dg * (-ea) * _softplus(pre))
    beta = jax.nn.sigmoid(ab[:, 128:])
    db = dgb[:, 128:] * beta * (1.0 - beta)
    return dc, jnp.concatenate([da, db], axis=1), dalog, _colsum(da)


def dn_pre_fwd(hin, conv_w, alog, dtb):
    L = hin.shape[0]

    def fn(i, nb, x, ab, p8, w, al, db):
        c = _conv_causal(x, p8 * (i > 0).astype(f32), w)
        return _dn_pre(c, ab, al, db)

    return rowwise(fn, name="dn_pre_fwd", L=L, tm=_pick(L, 256, SUBLANE),
                   rows=[(hin, 0, QKV_W, "cur"), (hin, 3328, 256, "cur"), (hin, 0, QKV_W, "prev")],
                   consts=[conv_w, alog, dtb], outs=[(QKV_W, f32), (256, f32)])


def dn_pre_bwd(hin, conv_w, alog, dtb, dqkv3, dg, dbeta):
    L = hin.shape[0]

    def fn(i, nb, x, ab, dq, dk, dv, dgt, dbt, p8, w, al, db):
        c = _conv_causal(x, p8 * (i > 0).astype(f32), w)
        return _dn_pre_bwd(c, ab, al, db, jnp.concatenate([dq, dk, dv], axis=1), jnp.concatenate([dgt, dbt], axis=1))

    rows = [(hin, 0, QKV_W, "cur"), (hin, 3328, 256, "cur")] + [(a, 0, DN_W, "cur") for a in dqkv3]
    rows += [(dg, 0, 128, "cur"), (dbeta, 0, 128, "cur"), (hin, 0, QKV_W, "prev")]
    return rowwise(fn, name="dn_pre_bwd", L=L, tm=_pick(L, 128, SUBLANE), rows=rows,
                   consts=[conv_w, alog, dtb], outs=[(QKV_W, f32), (256, f32)], sums=[(1, 128), (1, 128)])


def _split(a):
    hi = a.astype(bf16)
    return hi, (a - hi.astype(f32)).astype(bf16)


def _dot3_raw(a, b, dims):
    ah, al = _split(a)
    bh, bl = _split(b)
    return _dot(ah, bh, dims) + (_dot(ah, bl, dims) + _dot(al, bh, dims))


@functools.partial(jax.custom_vjp, nondiff_argnums=(2,))
def _dot3(a, b, dims=NN):
    return _dot3_raw(a, b, dims)


def _dot3_fwd(a, b, dims):
    return _dot3_raw(a, b, dims), (a, b)


BNN = (((2,), (1,)), ((0,), (0,)))
BNT = (((2,), (2,)), ((0,), (0,)))
BTN = (((1,), (1,)), ((0,), (0,)))


def _dot_bwd(raw, dims, res, g):
    a, b = res
    nn, nt, tn = (BNN, BNT, BTN) if dims[1][0] else (NN, NT, TN)
    if dims == nn:
        return raw(g, b, nt), raw(a, g, tn)
    if dims == nt:
        return raw(g, b, nn), raw(g, a, tn)
    assert dims == tn
    return raw(b, g, nt), raw(a, g, nn)


_dot3.defvjp(_dot3_fwd, functools.partial(_dot_bwd, _dot3_raw))


def _dot1_raw(a, b, dims):
    return _dot(a.astype(bf16), b.astype(bf16), dims)


@functools.partial(jax.custom_vjp, nondiff_argnums=(2,))
def _dot1(a, b, dims=NN):
    return _dot1_raw(a, b, dims)


_dot1.defvjp(lambda a, b, dims: (_dot1_raw(a, b, dims), (a, b)), functools.partial(_dot_bwd, _dot1_raw))


def _dn_chunk(q, k, v, gcol, bcol, S):
    C = q.shape[1]
    r = lax.broadcasted_iota(jnp.int32, (C, C), 0)
    c = lax.broadcasted_iota(jnp.int32, (C, C), 1)
    tril = (r >= c).astype(f32)
    strict = (r > c).astype(f32)
    eye = (r == c).astype(f32)
    hd = _dot3
    grow = jnp.sum(eye * gcol, axis=1, keepdims=True)
    Gcol = jnp.sum(tril * grow, axis=2, keepdims=True)
    Grow = jnp.sum(eye * Gcol, axis=1, keepdims=True)
    gamma = jnp.exp((Gcol - Grow) * tril) * tril
    ld = _dot1
    nmat = strict * bcol * ld(k, k, BNT) * gamma
    T = eye - nmat
    Pw = hd(nmat, nmat, BNN)
    for step in range(5):
        T = T + hd(T, Pw, BNN)
        if step < 4:
            Pw = hd(Pw, Pw, BNN)
    eG = jnp.exp(Gcol)
    u = hd(T, bcol * v, BNN)
    w = hd(T, (bcol * eG) * k, BNN)
    qk = ld(q, k, BNT) * gamma
    vnew = u - ld(w, S, BNN)
    o = ld(q * eG, S, BNN) + ld(qk, vnew, BNN)
    Glast = jnp.sum(gcol, axis=1, keepdims=True)
    S2 = S * jnp.exp(Glast) + ld(k * jnp.exp(Glast - Gcol), vnew, BTN)
    return o, S2


def _heads(x_ref):
    return jnp.stack([x_ref[:, h * 128:(h + 1) * 128] for h in range(DN_HEADS)])


def _head_cols(g_ref):
    return jnp.stack([g_ref[:, h:h + 1] for h in range(DN_HEADS)])


def dn_chunks_fwd(qkvn, gb):
    L = qkvn.shape[0]
    C = DN_CHUNK
    nc = L // C

    def body(q_ref, k_ref, v_ref, g_ref, b_ref, o_ref, sin_ref, S):
        n = pl.program_id(0)

        @pl.when(n == 0)
        def _():
            S[...] = jnp.zeros_like(S)

        s_in = S[...]
        sin_ref[...] = s_in
        o, s2 = _dn_chunk(_heads(q_ref), _heads(k_ref), _heads(v_ref), _head_cols(g_ref), _head_cols(b_ref), s_in)
        for h in range(DN_HEADS):
            o_ref[:, h * 128:(h + 1) * 128] = o[h]
        S[...] = s2

    blk = lambda j: pl.BlockSpec((C, DN_W), lambda n, j=j: (n, j))
    gblk = lambda j: pl.BlockSpec((C, 128), lambda n, j=j: (n, j))
    return pl.pallas_call(
        body, grid=(nc,),
        in_specs=[blk(0), blk(1), blk(2), gblk(0), gblk(1)],
        out_specs=[pl.BlockSpec((C, DN_W), lambda n: (n, 0)),
                   pl.BlockSpec((DN_HEADS, None, 128, 128), lambda n: (0, n, 0, 0))],
        out_shape=[jax.ShapeDtypeStruct((L, DN_W), f32), jax.ShapeDtypeStruct((DN_HEADS, nc, 128, 128), f32)],
        scratch_shapes=[pltpu.VMEM((DN_HEADS, 128, 128), f32)],
        compiler_params=_cp("arbitrary"), name="dn_chunks_fwd",
    )(qkvn, qkvn, qkvn, gb, gb)


def dn_chunks_bwd(qkvn, gb, s_in, do):
    L = qkvn.shape[0]
    C = DN_CHUNK
    nc = L // C

    def body(q_ref, k_ref, v_ref, g_ref, b_ref, sin_ref, do_ref, dq_ref, dk_ref, dv_ref, dg_ref, db_ref, dS):
        n = pl.program_id(0)

        @pl.when(n == 0)
        def _():
            dS[...] = jnp.zeros_like(dS)

        args = (_heads(q_ref), _heads(k_ref), _heads(v_ref), _head_cols(g_ref), _head_cols(b_ref), sin_ref[...])
        _, vjp = jax.vjp(_dn_chunk, *args)
        dq, dk, dv, dg, db, ds = vjp((_heads(do_ref), dS[...]))
        lane = lax.broadcasted_iota(jnp.int32, (C, 128), 1)
        dg_all = jnp.zeros((C, 128), f32)
        db_all = jnp.zeros((C, 128), f32)
        for h in range(DN_HEADS):
            sl = slice(h * 128, (h + 1) * 128)
            dq_ref[:, sl] = dq[h]
            dk_ref[:, sl] = dk[h]
            dv_ref[:, sl] = dv[h]
            dg_all = dg_all + jnp.where(lane == h, dg[h], 0.0)
            db_all = db_all + jnp.where(lane == h, db[h], 0.0)
        dS[...] = ds
        dg_ref[...] = dg_all
        db_ref[...] = db_all

    rv = lambda n: nc - 1 - n
    blk = lambda j: pl.BlockSpec((C, DN_W), lambda n, j=j: (rv(n), j))
    gblk = lambda j: pl.BlockSpec((C, 128), lambda n, j=j: (rv(n), j))
    oblk = pl.BlockSpec((C, DN_W), lambda n: (rv(n), 0))
    gout = pl.BlockSpec((C, 128), lambda n: (rv(n), 0))
    return pl.pallas_call(
        body, grid=(nc,),
        in_specs=[blk(0), blk(1), blk(2), gblk(0), gblk(1),
                  pl.BlockSpec((DN_HEADS, None, 128, 128), lambda n: (0, rv(n), 0, 0)), oblk],
        out_specs=[oblk] * 3 + [gout] * 2,
        out_shape=[jax.ShapeDtypeStruct((L, DN_W), f32)] * 3 + [jax.ShapeDtypeStruct((L, 128), f32)] * 2,
        scratch_shapes=[pltpu.VMEM((DN_HEADS, 128, 128), f32)],
        compiler_params=_cp("arbitrary"), name="dn_chunks_bwd",
    )(qkvn, qkvn, qkvn, gb, gb, s_in, do)


def _dn_post(o, z, w):
    parts = []
    for h in range(DN_HEADS):
        oh = o[:, h * 128:(h + 1) * 128]
        r = lax.rsqrt(jnp.mean(oh * oh, axis=-1, keepdims=True) + EPS)
        parts.append(oh * r * w)
    return jnp.concatenate(parts, axis=1) * _silu(z)


def dn_post_fwd(o, hin, yc, onorm):
    L = o.shape[0]

    def fn(i, nb, ot, zt, yct, w):
        return jnp.concatenate([yct, _dn_post(ot, zt, w)], axis=1)

    return rowwise(fn, name="dn_post_fwd", L=L, tm=_pick(L, 256, SUBLANE),
                   rows=[(o, 0, DN_W, "cur"), (hin, 2304, DN_W, "cur"), (yc, 0, S5_WIDTH, "cur")],
                   consts=[onorm], outs=[(1024, bf16)])


def dn_post_bwd(o, hin, onorm, dycat):
    L = o.shape[0]

    def fn(i, nb, ot, zt, d0, d1, d2, w):
        dy = jnp.concatenate([d0, d1, d2], axis=1)
        sg = jax.nn.sigmoid(zt)
        sz = zt * sg
        dos, dw = [], jnp.zeros((1, 128), f32)
        nrm = []
        for h in range(DN_HEADS):
            sl = slice(h * 128, (h + 1) * 128)
            oh = ot[:, sl]
            r = lax.rsqrt(jnp.mean(oh * oh, axis=-1, keepdims=True) + EPS)
            ohat = oh * r
            t = dy[:, sl] * sz[:, sl]
            dw = dw + _colsum(t * ohat)
            t = t * w
            dos.append(r * (t - ohat * jnp.mean(t * ohat, axis=-1, keepdims=True)))
            nrm.append(ohat * w)
        dz = dy * jnp.concatenate(nrm, axis=1) * (sg * (1.0 + zt * (1.0 - sg)))
        return jnp.concatenate(dos, axis=1), dz, dw

    rows = [(o, 0, DN_W, "cur"), (hin, 2304, DN_W, "cur")] + [(dycat, 256 * (1 + j), 256, "cur") for j in range(3)]
    return rowwise(fn, name="dn_post_bwd", L=L, tm=_pick(L, 256, SUBLANE), rows=rows,
                   consts=[onorm], outs=[(DN_W, f32), (DN_W, f32)], sums=[(1, 128)])


def conv_bwd_win(xarr, start, C, w, dc, name):
    L = xarr.shape[0]
    W = w.shape[0]

    def fn(i, nb, xt, dct, p8, n8, wv):
        dx = _conv_causal_bwd_x(dct, n8 * (i < nb - 1).astype(f32), wv)
        dw = _conv_causal_bwd_w(dct, xt, p8 * (i > 0).astype(f32), W)
        return dx, dw

    return rowwise(fn, name=name, L=L, tm=_pick(L, 128, SUBLANE),
                   rows=[(xarr, start, C, "cur"), (dc, 0, C, "cur"), (xarr, start, C, "prev"), (dc, 0, C, "next")],
                   consts=[w], outs=[(C, bf16)], sums=[(SUBLANE, C)])


def rec_assemble(dx_qkv, dz, du1, du2, dab):
    L = dz.shape[0]

    def fn(i, nb, a, b, c, d, e):
        return jnp.concatenate([a.astype(f32), b, c + d, e], axis=1)

    return rowwise(fn, name="rec_assemble", L=L, tm=_pick(L, 256, SUBLANE),
                   rows=[(dx_qkv, 0, QKV_W, "cur"), (dz, 0, DN_W, "cur"), (du1, 0, 256, "cur"),
                         (du2, 0, 256, "cur"), (dab, 0, 256, "cur")], outs=[(REC_PAD, bf16)])


def deltanet_fwd(hin, prm, yc):
    qkvn, gb = dn_pre_fwd(hin, prm["conv"], prm["alog"], prm["dtb"])
    o, s_in = dn_chunks_fwd(qkvn, gb)
    ycat = dn_post_fwd(o, hin, yc, prm["onorm"])
    return ycat, (qkvn, gb, o, s_in)


def deltanet_bwd(hin, prm, saved, dycat):
    qkvn, gb, o, s_in = saved
    do, dz, donorm = dn_post_bwd(o, hin, prm["onorm"], dycat)
    dq, dk, dv, dgH, dbH = dn_chunks_bwd(qkvn, gb, s_in, do)
    dc, dab, dalog, ddtb = dn_pre_bwd(hin, prm["conv"], prm["alog"], prm["dtb"], (dq, dk, dv), dgH, dbH)
    dx_qkv, dconv = conv_bwd_win(hin, 0, QKV_W, prm["conv"], dc, "dn_conv_bwd")
    return dx_qkv, dz, dab, dict(conv=dconv[:DN_CONV], alog=dalog, dtb=ddtb, onorm=donorm)


AXES = ("x", "y", "c")


class _Coll:
    def __init__(self, x, axes, mode):
        self.axes, self.mode = axes, mode
        self.P = 2 ** len(axes)
        shape = x.shape if mode == "gather" else x.shape[1:]
        self.out_shape = jax.ShapeDtypeStruct((self.P,) + tuple(shape), x.dtype)
        self.scratch = [pltpu.SemaphoreType.DMA((self.P - 1,)), pltpu.SemaphoreType.DMA((self.P - 1,)),
                        pltpu.SemaphoreType.DMA]

    def _copies(self, x_ref, out_ref, send_sems, recv_sems, local_sem, with_recvs):
        axes, k = self.axes, len(self.axes)
        co = {a: lax.axis_index(a) for a in AXES}
        me = 0
        for a in axes:
            me = me * 2 + co[a]
        src = (lambda j: x_ref) if self.mode == "gather" else (lambda j: x_ref.at[j])
        local = pltpu.make_async_copy(src(me), out_ref.at[me], local_sem)
        sends, recvs = [], []
        for m in range(1, self.P):
            tco = dict(co)
            t = 0
            for i, a in enumerate(axes):
                if (m >> (k - 1 - i)) & 1:
                    tco[a] = 1 - co[a]
                t = t * 2 + tco[a]
            dev = tuple(tco[a] for a in AXES)
            mk = functools.partial(pltpu.make_async_remote_copy, src_ref=src(t), send_sem=send_sems.at[m - 1],
                                   recv_sem=recv_sems.at[m - 1], device_id=dev, device_id_type=MESH)
            sends.append(mk(dst_ref=out_ref.at[me]))
            if with_recvs:
                recvs.append(mk(dst_ref=out_ref.at[t]))
        return local, sends, recvs

    def start(self, *refs):
        local, sends, _ = self._copies(*refs, with_recvs=False)
        local.start()
        for cp in sends:
            cp.start()

    def wait(self, *refs):
        local, sends, recvs = self._copies(*refs, with_recvs=True)
        for cp in recvs:
            cp.wait_recv()
        for cp in sends:
            cp.wait_send()
        local.wait()


def _collective(x, axes, mode, name):
    coll = _Coll(x, axes, mode)

    def body(*refs):
        coll.start(*refs)
        coll.wait(*refs)

    return pl.pallas_call(
        body, in_specs=[pl.BlockSpec(memory_space=pl.ANY)], out_specs=pl.BlockSpec(memory_space=pl.ANY),
        out_shape=coll.out_shape, scratch_shapes=coll.scratch, name=name,
    )(x)


def all_gather(x, axes, name):
    return _collective(x, axes, "gather", name)


def exchange(x, axes, name):
    return _collective(x, axes, "exchange", name)


def sum_slots(x, name, out_dtype=f32):
    P, R, C = x.shape
    tr = _pick(R, 256, 2 * SUBLANE)

    def body(x_ref, o_ref):
        acc = x_ref[0].astype(f32)
        for j in range(1, P):
            acc = acc + x_ref[j].astype(f32)
        o_ref[...] = acc.astype(o_ref.dtype)

    return pl.pallas_call(
        body, grid=(R // tr,), in_specs=[pl.BlockSpec((P, tr, C), lambda i: (0, i, 0))],
        out_specs=pl.BlockSpec((tr, C), lambda i: (i, 0)), out_shape=jax.ShapeDtypeStruct((R, C), out_dtype),
        compiler_params=_cp("parallel"), name=name,
    )(x)


def _pack(arrs, width, row_mult, dtype):
    flat = jnp.concatenate([a.astype(dtype).reshape(-1) for a in arrs])
    unit = width * row_mult
    n = -(-flat.shape[0] // unit) * unit
    return jnp.pad(flat, (0, n - flat.shape[0])).reshape(n // width, width)


def _unpack(flat, shapes):
    flat = flat.reshape(-1)
    out, off = [], 0
    for s in shapes:
        n = int(np.prod(s))
        out.append(flat[off:off + n].reshape(s))
        off += n
    return out


def ada_fwd(c_all, ada_w):
    def body(c_ref, w_ref, o_ref):
        cond = _silu(c_ref[...])
        for l in range(ada_w.shape[0]):
            o_ref[l] = _dot(cond, w_ref[l], precision=HI)

    return pl.pallas_call(body, out_shape=jax.ShapeDtypeStruct((ada_w.shape[0], c_all.shape[0], ada_w.shape[2]), f32),
                          compiler_params=pltpu.CompilerParams(vmem_limit_bytes=VMEM_LIMIT), name="ada_fwd")(c_all, ada_w)


def ada_bwd(c_all, dmod):
    def body(c_ref, d_ref, o_ref):
        cond = _silu(c_ref[...])
        for l in range(dmod.shape[0]):
            o_ref[l] = _dot(cond, d_ref[l], TN, precision=HI)

    return pl.pallas_call(body, out_shape=jax.ShapeDtypeStruct((dmod.shape[0], c_all.shape[1], dmod.shape[2]), f32),
                          compiler_params=pltpu.CompilerParams(vmem_limit_bytes=VMEM_LIMIT), name="ada_bwd")(c_all, dmod)


def loss_fwd_bwd(y, target):
    L, D = y.shape

    def fn(i, nb, yt, tt):
        e = yt - tt
        return e * (1.0 / D), jnp.sum(jnp.sum(e * e, axis=1, keepdims=True), axis=0, keepdims=True)

    return rowwise(fn, name="loss", L=L, tm=_pick(L, 512, SUBLANE), rows=[(y, 0, D, "cur"), (target, 0, D, "cur")],
                   outs=[(D, f32)], sums=[(1, 1)])


def adamw(w, g, m, v, name):
    R, C = w.shape

    def fn(i, nb, wt, gt, mt, vt):
        m2 = ADAM_B1 * mt + (1.0 - ADAM_B1) * gt
        v2 = ADAM_B2 * vt + (1.0 - ADAM_B2) * (gt * gt)
        m_hat = m2 / (1.0 - ADAM_B1 ** ADAM_STEP)
        v_hat = v2 / (1.0 - ADAM_B2 ** ADAM_STEP)
        delta = -ADAM_LR * (m_hat / (jnp.sqrt(v_hat) + ADAM_EPS) + ADAM_WD * wt)
        return delta, m2, v2

    return rowwise(fn, name=name, L=R, tm=_pick(R, 256, SUBLANE), rows=[(a, 0, C, "cur") for a in (w, g, m, v)],
                   outs=[(C, f32)] * 3)


W_NAMES = ["ada_w", "ada_b", "norm_mix", "norm_ffn", "attn_w_in", "attn_q_norm_a", "attn_k_norm_a", "attn_q_norm_b",
           "attn_k_norm_b", "attn_sinks", "attn_w_out", "rec_w_in", "s5_lambda_re", "s5_lambda_im", "s5_log_dt",
           "s5_b_re", "s5_b_im", "s5_c_re", "s5_c_im", "s5_d", "s5_glu_w", "s5_glu_b", "dn_conv", "dn_a_log",
           "dn_dt_bias", "dn_out_norm", "rec_w_out", "ffn_w_up", "ffn_conv", "ffn_w_down"]
BIG = ["attn_w_in", "attn_w_out", "rec_w_in", "rec_w_out", "ffn_w_up", "ffn_w_down"]
SMALL_SHARDED = ["s5_d", "s5_glu_w", "s5_glu_b", "dn_conv", "ffn_conv"]
SMALL_REPL = [n for n in W_NAMES if n not in BIG and n not in SMALL_SHARDED and n != "ada_w"]
NSH = 4
GRAD_WIRE = (bf16,)


SHARD_AXIS = {"attn_w_in": 2, "attn_w_out": 1, "rec_w_in": 2, "rec_w_out": 1, "ffn_w_up": 2, "ffn_w_down": 1,
              "s5_d": 1, "s5_glu_w": 1, "s5_glu_b": 1, "dn_conv": 2, "ffn_conv": 2}


def _unshard(g, name):
    ax = SHARD_AXIS[name.rstrip("01")]
    g = jnp.moveaxis(g, 0, ax)
    s = g.shape
    return g.reshape(s[:ax] + (s[ax] * s[ax + 1],) + s[ax + 2:])


def _to_shards(full, name):
    ax = SHARD_AXIS[name.rstrip("01")]
    s = full.shape
    g = full.reshape(s[:ax] + (NSH, s[ax] // NSH) + s[ax + 1:])
    return jnp.moveaxis(g, ax, 0)


def _rec_pad_cols(w):
    z6 = jnp.zeros(w.shape[:-1] + (122,), w.dtype)
    return jnp.concatenate([w[..., 256:3328], w[..., 0:256], w[..., 3328:3334], z6, w[..., 3334:3340], z6], axis=-1)


def _rec_unpad_cols(g):
    return jnp.concatenate([g[..., 3072:3328], g[..., 0:3072], g[..., 3328:3334], g[..., 3456:3462]], axis=-1)


def _ffn_fwd(x1, nf, sc, sh, gate, w_up, conv, w_dn, tag, rides=()):
    rides = list(rides) + [None, None]
    h2 = modulate_fwd(x1, nf, sc, sh, f"{tag}_mod2_fwd")
    up = mm(h2, w_up, name=f"{tag}_ffn_up", out_dtypes=(bf16,), ride=rides[0])
    up, got0 = up if rides[0] else (up, None)
    act = ffn_act_fwd(up, conv, f"{tag}_ffn_act_fwd")
    res = mm(act, w_dn, name=f"{tag}_ffn_down", out_dtypes=(f32, f32), epi=_resid_epi, epi_mn=[x1], epi_n=[gate],
             ride=rides[1])
    return res[1], (h2, up, act, res[0]), (got0, res[2] if rides[1] else None)


def _ffn_bwd(dx, x1, nf, sc, sh, gate, w_up, conv, w_dn, saved, tag, rides=()):
    rides = list(rides) + [None, None, None]
    take = lambda res, r: res if r else (res, None)
    h2, up, act, f = saved
    df, dgate = resid_bwd(dx, f, gate, f"{tag}_res2_bwd")
    dact = mm(df, w_dn, tb=True, name=f"{tag}_ffn_dact", out_dtypes=(bf16,))
    dw_dn, got0 = take(mm(act, df, ta=True, name=f"{tag}_ffn_dwdown", out_dtypes=GRAD_WIRE, ride=rides[0]), rides[0])
    dup, dconv = ffn_act_conv_bwd(up, conv, dact, f"{tag}_ffn_act_conv_bwd")
    dw_up, got1 = take(mm(h2, dup, ta=True, name=f"{tag}_ffn_dwup", out_dtypes=GRAD_WIRE, ride=rides[1]), rides[1])
    dh2, got2 = take(mm(dup, w_up, tb=True, name=f"{tag}_ffn_dh", ride=rides[2]), rides[2])
    dx, dnf, dsc, dsh = modulate_bwd(x1, nf, sc, sh, dh2, dx, f"{tag}_mod2_bwd")
    grads = dict(nf=dnf, sc=dsc, sh=dsh, gate=dgate, w_up=dw_up, conv=dconv[:FFN_CONV], w_dn=dw_dn)
    return dx, grads, (got0, got1, got2)


def kernel(x, c, ada_w, ada_b, norm_mix, norm_ffn, attn_w_in, attn_q_norm_a, attn_k_norm_a, attn_q_norm_b, attn_k_norm_b, attn_sinks, attn_w_out, rec_w_in, s5_lambda_re, s5_lambda_im, s5_log_dt, s5_b_re, s5_b_im, s5_c_re, s5_c_im, s5_d, s5_glu_w, s5_glu_b, dn_conv, dn_a_log, dn_dt_bias, dn_out_norm, rec_w_out, ffn_w_up, ffn_conv, ffn_w_down, loss_target, m_ada_w, m_ada_b, m_norm_mix, m_norm_ffn, m_attn_w_in, m_attn_q_norm_a, m_attn_k_norm_a, m_attn_q_norm_b, m_attn_k_norm_b, m_attn_sinks, m_attn_w_out, m_rec_w_in, m_s5_lambda_re, m_s5_lambda_im, m_s5_log_dt, m_s5_b_re, m_s5_b_im, m_s5_c_re, m_s5_c_im, m_s5_d, m_s5_glu_w, m_s5_glu_b, m_dn_conv, m_dn_a_log, m_dn_dt_bias, m_dn_out_norm, m_rec_w_out, m_ffn_w_up, m_ffn_conv, m_ffn_w_down, v_ada_w, v_ada_b, v_norm_mix, v_norm_ffn, v_attn_w_in, v_attn_q_norm_a, v_attn_k_norm_a, v_attn_q_norm_b, v_attn_k_norm_b, v_attn_sinks, v_attn_w_out, v_rec_w_in, v_s5_lambda_re, v_s5_lambda_im, v_s5_log_dt, v_s5_b_re, v_s5_b_im, v_s5_c_re, v_s5_c_im, v_s5_d, v_s5_glu_w, v_s5_glu_b, v_dn_conv, v_dn_a_log, v_dn_dt_bias, v_dn_out_norm, v_rec_w_out, v_ffn_w_up, v_ffn_conv, v_ffn_w_down):
    args = (ada_w, ada_b, norm_mix, norm_ffn, attn_w_in, attn_q_norm_a, attn_k_norm_a, attn_q_norm_b, attn_k_norm_b, attn_sinks, attn_w_out, rec_w_in, s5_lambda_re, s5_lambda_im, s5_log_dt, s5_b_re, s5_b_im, s5_c_re, s5_c_im, s5_d, s5_glu_w, s5_glu_b, dn_conv, dn_a_log, dn_dt_bias, dn_out_norm, rec_w_out, ffn_w_up, ffn_conv, ffn_w_down)
    ms = (m_ada_w, m_ada_b, m_norm_mix, m_norm_ffn, m_attn_w_in, m_attn_q_norm_a, m_attn_k_norm_a, m_attn_q_norm_b, m_attn_k_norm_b, m_attn_sinks, m_attn_w_out, m_rec_w_in, m_s5_lambda_re, m_s5_lambda_im, m_s5_log_dt, m_s5_b_re, m_s5_b_im, m_s5_c_re, m_s5_c_im, m_s5_d, m_s5_glu_w, m_s5_glu_b, m_dn_conv, m_dn_a_log, m_dn_dt_bias, m_dn_out_norm, m_rec_w_out, m_ffn_w_up, m_ffn_conv, m_ffn_w_down)
    vs = (v_ada_w, v_ada_b, v_norm_mix, v_norm_ffn, v_attn_w_in, v_attn_q_norm_a, v_attn_k_norm_a, v_attn_q_norm_b, v_attn_k_norm_b, v_attn_sinks, v_attn_w_out, v_rec_w_in, v_s5_lambda_re, v_s5_lambda_im, v_s5_log_dt, v_s5_b_re, v_s5_b_im, v_s5_c_re, v_s5_c_im, v_s5_d, v_s5_glu_w, v_s5_glu_b, v_dn_conv, v_dn_a_log, v_dn_dt_bias, v_dn_out_norm, v_rec_w_out, v_ffn_w_up, v_ffn_conv, v_ffn_w_down)
    W = dict(zip(W_NAMES, args))
    Mo = dict(zip(W_NAMES, ms))
    Vo = dict(zip(W_NAMES, vs))
    xi, yi, ci = lax.axis_index("x"), lax.axis_index("y"), lax.axis_index("c")
    shard = 2 * xi + yi
    me8 = 4 * xi + 2 * yi + ci
    xs = x[0]
    target = loss_target[0]
    L, D = xs.shape

    XY = ("x", "y")
    wparts = [
        [("attn_w_in", attn_w_in), ("attn_w_out", attn_w_out), ("ffn_w_up0", ffn_w_up[0:1]), ("ffn_w_down0", ffn_w_down[0:1])],
        [("rec_w_in", rec_w_in), ("rec_w_out", rec_w_out)],
        [("ffn_w_up1", ffn_w_up[1:2]), ("ffn_w_down1", ffn_w_down[1:2])],
    ]
    wpack = [_pack([a for _, a in p], 1024, 16, bf16) for p in wparts]
    Wf = {}

    def unpack_weights(gathered, part):
        flat = gathered.reshape(NSH, -1)
        off = 0
        for n, a in part:
            sz = int(np.prod(a.shape))
            Wf[n] = _unshard(flat[:, off:off + sz].reshape((NSH,) + a.shape), n)[0]
            off += sz

    unpack_weights(all_gather(wpack[0], XY, "gather_w0"), wparts[0])

    sflat = _pack([c] + [W[n] for n in SMALL_SHARDED], 1024, 8, f32)
    s8 = all_gather(sflat, AXES, "gather_small")
    s8f = s8.reshape(8, -1)
    c_all = s8f[:, :D]
    Ws = {}
    off = D
    for n in SMALL_SHARDED:
        sz = int(np.prod(W[n].shape))
        Ws[n] = _unshard(s8f[0::2, off:off + sz].reshape((NSH,) + W[n].shape), n)
        off += sz

    modp = ada_fwd(c_all, ada_w)
    modg = all_gather(modp, ("x", "y"), "gather_mod")
    mod_all = jnp.moveaxis(modg, 0, 2).reshape(2, 8, -1) + ada_b[:, None, :]
    mod = lax.dynamic_slice(mod_all, (0, me8, 0), (2, 1, mod_all.shape[2]))[:, 0, :]
    mods = [[mod[l:l + 1, j * D:(j + 1) * D] for j in range(6)] for l in range(2)]

    sh1, sc1, g1, sh2, sc2, g2_ = mods[0]
    nm0, nf0 = norm_mix[0:1], norm_ffn[0:1]
    sinkb = jnp.repeat(attn_sinks[0], HEAD_DIM)[None]
    h0 = modulate_fwd(xs, nm0, sc1, sh1, "l0_mod1_fwd")
    hin0 = mm(h0, Wf["attn_w_in"], name="l0_in_proj")
    ocat, att_saved = attention_fwd(hin0, attn_q_norm_a, attn_k_norm_a, attn_q_norm_b, attn_k_norm_b, sinkb)
    y0, x1 = mm(ocat, Wf["attn_w_out"], name="l0_out_proj", out_dtypes=(f32, f32), epi=_resid_epi,
                epi_mn=[xs], epi_n=[g1])
    x2, ffn0_saved, got = _ffn_fwd(x1, nf0, sc2, sh2, g2_, Wf["ffn_w_up0"], Ws["ffn_conv"][0], Wf["ffn_w_down0"], "l0",
                                   rides=[(wpack[1], XY, "gather"), (wpack[2], XY, "gather")])
    unpack_weights(got[0], wparts[1])
    unpack_weights(got[1], wparts[2])
    rec_w_in_p = _rec_pad_cols(Wf["rec_w_in"])

    th1, tc1, t1, th2, tc2, t2 = mods[1]
    nm1, nf1 = norm_mix[1:2], norm_ffn[1:2]
    pad128 = lambda a: jnp.pad(a, ((0, 0), (0, 128 - a.shape[1])))
    s5p = dict(lr=s5_lambda_re[0], li=s5_lambda_im[0], ldt=s5_log_dt[0][:, None], b_re=s5_b_re[0], b_im=s5_b_im[0],
               c_re=s5_c_re[0], c_im=s5_c_im[0], d=Ws["s5_d"], gw=Ws["s5_glu_w"][0], gb=Ws["s5_glu_b"])
    dnp = dict(conv=Ws["dn_conv"][0], alog=pad128(dn_a_log), dtb=pad128(dn_dt_bias), onorm=dn_out_norm)
    h1 = modulate_fwd(x2, nm1, tc1, th1, "l1_mod1_fwd")
    hin1 = mm(h1, rec_w_in_p, name="l1_in_proj")
    yc, s5_saved = s5_fwd(hin1, s5p)
    ycat, dn_saved = deltanet_fwd(hin1, dnp, yc)
    y1, x3 = mm(ycat, Wf["rec_w_out"], name="l1_out_proj", out_dtypes=(f32, f32), epi=_resid_epi,
                epi_mn=[x2], epi_n=[t1])
    x4, ffn1_saved, _ = _ffn_fwd(x3, nf1, tc2, th2, t2, Wf["ffn_w_up1"], Ws["ffn_conv"][1], Wf["ffn_w_down1"], "l1")

    dx, sse = loss_fwd_bwd(x4, target)
    loss = lax.psum(0.5 * sse[0, 0] / D, AXES)

    dx, gf1, _ = _ffn_bwd(dx, x3, nf1, tc2, th2, t2, Wf["ffn_w_up1"], Ws["ffn_conv"][1], Wf["ffn_w_down1"], ffn1_saved, "l1")
    dy1, dt1 = resid_bwd(dx, y1, t1, "l1_res1_bwd")
    dycat = mm(dy1, Wf["rec_w_out"], tb=True, name="l1_dycat")
    dw_rec_out = mm(ycat, dy1, ta=True, name="l1_dwout", out_dtypes=GRAD_WIRE)
    du_skip, du_b, s5g = s5_bwd(hin1, s5p, s5_saved, dycat)
    dx_qkv, dz, dab, dng = deltanet_bwd(hin1, dnp, dn_saved, dycat)
    dhin1 = rec_assemble(dx_qkv, dz, du_skip, du_b, dab)
    dw_rec_in = _rec_unpad_cols(mm(h1, dhin1, ta=True, name="l1_dwin", out_dtypes=GRAD_WIRE))
    dh1 = mm(dhin1, rec_w_in_p, tb=True, name="l1_dh")
    dx, dnm1, dtc1, dth1 = modulate_bwd(x2, nm1, tc1, th1, dh1, dx, "l1_mod1_bwd")

    def grad_part(items):
        flat = jnp.concatenate([_to_shards(g, n).reshape(NSH, -1) for n, g in items], axis=1)
        unit = 256 * 1024
        npad = -(-flat.shape[1] // unit) * unit
        return jnp.pad(flat, ((0, 0), (0, npad - flat.shape[1]))).reshape(NSH, npad // 1024, 1024)

    gparts = [[("rec_w_in", dw_rec_in[None]), ("rec_w_out", dw_rec_out[None])],
              [("ffn_w_up1", gf1["w_up"][None])], [("ffn_w_down1", gf1["w_dn"][None])]]
    dx, gf0, gq = _ffn_bwd(dx, x1, nf0, sc2, sh2, g2_, Wf["ffn_w_up0"], Ws["ffn_conv"][0], Wf["ffn_w_down0"], ffn0_saved,
                           "l0", rides=[(grad_part(p), XY, "exchange") for p in gparts])
    dy0, dg1 = resid_bwd(dx, y0, g1, "l0_res1_bwd")
    dcat = mm(dy0, Wf["attn_w_out"], tb=True, name="l0_dcat")
    dw_attn_out = mm(ocat, dy0, ta=True, name="l0_dwout", out_dtypes=GRAD_WIRE)
    dhin0, dwqa, dwka, dwqb, dwkb, dsinkb = attention_bwd(hin0, attn_q_norm_a, attn_k_norm_a, attn_q_norm_b,
                                                          attn_k_norm_b, sinkb, att_saved, dcat)
    dw_attn_in = mm(h0, dhin0, ta=True, name="l0_dwin", out_dtypes=GRAD_WIRE)
    dh0 = mm(dhin0, Wf["attn_w_in"], tb=True, name="l0_dh")
    grad_x, dnm0, dsc1, dsh1 = modulate_bwd(xs, nm0, sc1, sh1, dh0, dx, "l0_mod1_bwd")

    dmod = jnp.concatenate([
        jnp.concatenate([dsh1, dsc1, dg1, gf0["sh"], gf0["sc"], gf0["gate"]], axis=1),
        jnp.concatenate([dth1, dtc1, dt1, gf1["sh"], gf1["sc"], gf1["gate"]], axis=1)], axis=0)
    gl = {
        "ada_b": dmod,
        "norm_mix": jnp.concatenate([dnm0, dnm1], axis=0),
        "norm_ffn": jnp.concatenate([gf0["nf"], gf1["nf"]], axis=0),
        "attn_q_norm_a": dwqa, "attn_k_norm_a": dwka, "attn_q_norm_b": dwqb, "attn_k_norm_b": dwkb,
        "attn_sinks": dsinkb[:, ::HEAD_DIM],
        "s5_lambda_re": s5g["lr"][None], "s5_lambda_im": s5g["li"][None], "s5_log_dt": s5g["ldt"][:, 0][None],
        "s5_b_re": s5g["b_re"][None], "s5_b_im": s5g["b_im"][None], "s5_c_re": s5g["c_re"][None],
        "s5_c_im": s5g["c_im"][None],
        "dn_a_log": dng["alog"][:, :DN_HEADS], "dn_dt_bias": dng["dtb"][:, :DN_HEADS], "dn_out_norm": dng["onorm"],
        "s5_d": s5g["d"], "s5_glu_w": s5g["gw"][None], "s5_glu_b": s5g["gb"], "dn_conv": dng["conv"][None],
        "ffn_conv": jnp.stack([gf0["conv"], gf1["conv"]]),
    }

    small_names = SMALL_REPL + SMALL_SHARDED
    gs = _pack([gl[n] for n in small_names], 128, 256, f32)
    gs8 = all_gather(gs, AXES, "gather_small_grads")
    gsum = sum_slots(gs8, "sum_small_grads")
    full_shapes = [gl[n].shape for n in small_names]
    gfull = dict(zip(small_names, _unpack(gsum, full_shapes)))
    dmod_all = gs8.reshape(8, -1)[:, :2 * 6 * D].reshape(8, 2, 6 * D)
    ncol = ada_w.shape[2]
    dmod_sh = jnp.moveaxis(lax.dynamic_slice(dmod_all, (0, 0, shard * ncol), (8, 2, ncol)), 0, 1)
    grads = {"ada_w": ada_bwd(c_all, dmod_sh)}
    for n in SMALL_REPL:
        grads[n] = gfull[n]
    for n in SMALL_SHARDED:
        sh_all = _to_shards(gfull[n], n)
        grads[n] = lax.dynamic_slice(sh_all, (shard,) + (0,) * (sh_all.ndim - 1), (1,) + sh_all.shape[1:])[0]

    gparts.append([("attn_w_in", dw_attn_in[None]), ("attn_w_out", dw_attn_out[None]),
                   ("ffn_w_up0", gf0["w_up"][None]), ("ffn_w_down0", gf0["w_dn"][None])])
    gq = list(gq) + [exchange(grad_part(gparts[3]), XY, "reduce_xy")]
    gpart = jnp.concatenate([sum_slots(q, f"sum_chips{i}", bf16) for i, q in enumerate(gq)], axis=0)
    gc = all_gather(gpart, ("c",), "gather_grad_c")
    gsh = sum_slots(gc, "sum_pair")
    row, got = 0, {}
    for part, q in zip(gparts, gq):
        flat = gsh[row:row + q.shape[1]].reshape(-1)
        row += q.shape[1]
        off = 0
        for n, g in part:
            sz = g.size // NSH
            got[n] = flat[off:off + sz].reshape((1,) + g.shape[1:-2] + _to_shards(g, n).shape[-2:])
            off += sz
    for n in ("attn_w_in", "attn_w_out", "rec_w_in", "rec_w_out"):
        grads[n] = got[n]
    grads["ffn_w_up"] = jnp.concatenate([got["ffn_w_up0"], got["ffn_w_up1"]], axis=0)
    grads["ffn_w_down"] = jnp.concatenate([got["ffn_w_down0"], got["ffn_w_down1"]], axis=0)

    delta, new_m, new_v = {}, {}, {}

    def as2d(a):
        return a.reshape(-1, a.shape[-1])

    for n in ["ada_w"] + BIG:
        d_, m_, v_ = adamw(as2d(W[n]), as2d(grads[n]), as2d(Mo[n]), as2d(Vo[n]), f"adamw_{n}")
        delta[n], new_m[n], new_v[n] = d_.reshape(W[n].shape), m_.reshape(W[n].shape), v_.reshape(W[n].shape)
    pk = lambda dd: _pack([dd[n] for n in small_names], 128, 256, f32)
    d_, m_, v_ = adamw(pk(W), pk(grads), pk(Mo), pk(Vo), "adamw_small")
    shp = [W[n].shape for n in small_names]
    for dst, src in ((delta, d_), (new_m, m_), (new_v, v_)):
        dst.update(zip(small_names, _unpack(src, shp)))

    return (loss, grad_x[None], *[grads[n] for n in W_NAMES], *[delta[n] for n in W_NAMES],
            *[new_m[n] for n in W_NAMES], *[new_v[n] for n in W_NAMES])
```

```python
import functools
import math

import numpy as np
import jax
import jax.numpy as jnp
from jax import lax
from jax.experimental import pallas as pl
from jax.experimental.pallas import tpu as pltpu

f32 = jnp.float32
bf16 = jnp.bfloat16
HI = lax.Precision.HIGHEST
MESH = pl.DeviceIdType.MESH

HEAD_DIM = 64
BLOCK = 128
A_Q_HEADS = 8
A_KV_HEADS = 2
A_WINDOW = 128
B_HEADS = 8
B_BRANCHES = ((128, 1), (512, 4), (2048, 16))
N_ATTN_HEADS = 16
ATTN_IN = 2304
S5_GROUP = 16
S5_GROUPS = 16
S5_WIDTH = 256
S5_STATE = 64
DN_HEADS = 6
DN_DK = 128
DN_CONV = 4
DN_CHUNK = 64
REC_IN = 3340
REC_PAD = 3584
FFN_CONV = 3
EPS = 1e-6
ADAM_LR = 0.001
ADAM_B1 = 0.9
ADAM_B2 = 0.999
ADAM_EPS = 1e-08
ADAM_WD = 0.01
ADAM_STEP = 10

LANE = 128
SUBLANE = 8
VMEM_LIMIT = 52 * 1024 * 1024
MM_FULL_K = 5632
MM_VMEM_BUDGET = 40 * 1024 * 1024


def _cp(*sem):
    return pltpu.CompilerParams(dimension_semantics=sem, vmem_limit_bytes=VMEM_LIMIT)


def _pick(dim, cap, unit=LANE):
    for t in (2048, 1024, 768, 512, 384, 256, 128, 64, 32, 16, 8):
        if t <= cap and t % unit == 0 and dim % t == 0:
            return t
    return dim


def _dot(a, b, dims=(((1,), (0,)), ((), ())), precision=None):
    return lax.dot_general(a, b, dims, precision=precision, preferred_element_type=f32)


NN = (((1,), (0,)), ((), ()))
NT = (((1,), (1,)), ((), ()))
TN = (((0,), (0,)), ((), ()))


def mm(a, b, *, name, ta=False, tb=False, a_win=None, b_win=None, out_dtypes=(f32,),
       epi=None, epi_mn=(), epi_n=(), tm_cap=1024, tn_cap=8192, tk_cap=None, ride=None):
    coll = _Coll(*ride) if ride else None
    a0, a1 = a.shape
    b0, b1 = b.shape
    aw = a_win or (0, a1)
    bw = b_win or (0, b1)
    if ta:
        K, M = a0, aw[1]
    else:
        M, K = a0, aw[1]
    if tb:
        N, K2 = b0, bw[1]
    else:
        K2, N = b0, bw[1]
    assert K == K2, (a.shape, b.shape, ta, tb, a_win, b_win)
    if tk_cap is None:
        tk_cap = K if K <= MM_FULL_K else 2048
    tk = _pick(K, tk_cap, SUBLANE if (ta and not tb) else LANE)
    nk = K // tk
    sa, sb = a.dtype.itemsize, b.dtype.itemsize
    so = sum(jnp.dtype(d).itemsize for d in out_dtypes)
    n_mn, n_n, n_out = len(epi_mn), len(epi_n), len(out_dtypes)

    def vmem(tm_, tn_):
        return 2 * (tm_ * tk * sa + tk * tn_ * sb + tm_ * tn_ * (so + 4 * n_mn)) + 2 * tm_ * tn_ * 4

    best = None
    for tm_ in (t for t in (1024, 512, 256, 128) if M % t == 0 and (not ta or aw[0] % t == 0)):
        for tn_ in (t for t in (N, N // 2, 1024, 768, 512, 384, 256, 128)
                    if t % LANE == 0 and N % t == 0 and (tb or bw[0] % t == 0)):
            if tm_ <= tm_cap and tn_ <= max(tn_cap, 0) and vmem(tm_, tn_) <= MM_VMEM_BUDGET:
                if best is None or (tm_ * tn_, tn_) > (best[0] * best[1], best[1]):
                    best = (tm_, tn_)
    assert best is not None, (name, M, N, K)
    tm, tn = best
    b_outer = tk * tn * sb > tm * tk * sa

    def ix(f):
        if b_outer:
            return lambda j, i, k: f(i, j, k)
        return f

    if ta:
        mo = aw[0] // tm
        a_spec = pl.BlockSpec((tk, tm), ix(lambda i, j, k: (k, i + mo)))
    else:
        assert aw[0] % tk == 0
        ko = aw[0] // tk
        a_spec = pl.BlockSpec((tm, tk), ix(lambda i, j, k: (i, k + ko)))
    if tb:
        assert bw[0] % tk == 0
        kob = bw[0] // tk
        b_spec = pl.BlockSpec((tn, tk), ix(lambda i, j, k: (j, k + kob)))
    else:
        no = bw[0] // tn
        b_spec = pl.BlockSpec((tk, tn), ix(lambda i, j, k: (k, j + no)))
    dims = (((0 if ta else 1,), (1 if tb else 0,)), ((), ()))

    gi, gj = M // tm, N // tn
    grid = (gj, gi, nk) if b_outer else (gi, gj, nk)
    n_ride = 1 if coll else 0

    def body(a_ref, b_ref, *rest):
        mn_refs = rest[:n_mn]
        n_refs = rest[n_mn:n_mn + n_n]
        o0 = n_mn + n_n + n_ride
        out_refs = rest[o0:o0 + n_out]
        if coll:
            ride_refs = (rest[o0 - 1], rest[o0 + n_out]) + tuple(rest[-3:])
            pid = [pl.program_id(t) for t in range(3)]

            @pl.when((pid[0] == 0) & (pid[1] == 0) & (pid[2] == 0))
            def _():
                coll.start(*ride_refs)

        def finish(r):
            if epi is None:
                outs = (r,)
            else:
                outs = epi(r, *[m[...] for m in mn_refs], *[v[...] for v in n_refs])
            for o_ref, o in zip(out_refs, outs):
                o_ref[...] = o.astype(o_ref.dtype)

        part = _dot(a_ref[...].astype(bf16), b_ref[...].astype(bf16), dims)
        if nk == 1:
            finish(part)
        else:
            acc = rest[o0 + n_out + n_ride]
            k = pl.program_id(2)

            @pl.when(k == 0)
            def _():
                acc[...] = part

            @pl.when(k > 0)
            def _():
                acc[...] += part

            @pl.when(k == nk - 1)
            def _():
                finish(acc[...])

        if coll:
            @pl.when((pid[0] == grid[0] - 1) & (pid[1] == grid[1] - 1) & (pid[2] == grid[2] - 1))
            def _():
                coll.wait(*ride_refs)

    mn_spec = pl.BlockSpec((tm, tn), ix(lambda i, j, k: (i, j)))
    n_spec = pl.BlockSpec((1, tn), ix(lambda i, j, k: (0, j)))
    hbm = pl.BlockSpec(memory_space=pl.ANY)
    outs = pl.pallas_call(
        body,
        grid=grid,
        in_specs=[a_spec, b_spec] + [mn_spec] * n_mn + [n_spec] * n_n + [hbm] * n_ride,
        out_specs=[mn_spec] * n_out + [hbm] * n_ride,
        out_shape=[jax.ShapeDtypeStruct((M, N), d) for d in out_dtypes] + ([coll.out_shape] if coll else []),
        scratch_shapes=([pltpu.VMEM((tm, tn), f32)] if nk > 1 else []) + (coll.scratch if coll else []),
        compiler_params=_cp(*(["arbitrary"] * 3 if coll else ["parallel", "parallel", "arbitrary"])),
        name=name,
    )(a, b, *epi_mn, *epi_n, *([ride[0]] if coll else []))
    return outs[0] if len(outs) == 1 else tuple(outs)


def rowwise(fn, *, name, L, tm, rows=(), consts=(), outs=(), sums=()):
    nb = L // tm
    in_specs = []
    arrs = []
    for arr, start, width, kind in rows:
        assert start % width == 0, (name, start, width)
        co = start // width
        hr = SUBLANE * (4 // arr.dtype.itemsize)
        hb = tm // hr
        if kind == "cur":
            in_specs.append(pl.BlockSpec((tm, width), lambda i, co=co: (i, co)))
        elif kind == "prev":
            in_specs.append(pl.BlockSpec((hr, width), lambda i, co=co, hb=hb: (jnp.maximum(i * hb - 1, 0), co)))
        else:
            last = L // hr - 1
            in_specs.append(pl.BlockSpec((hr, width), lambda i, co=co, hb=hb, last=last:
                                         (jnp.minimum((i + 1) * hb, last), co)))
        arrs.append(arr)
    for cst in consts:
        assert cst.ndim == 2
        in_specs.append(pl.BlockSpec(cst.shape, lambda i: (0, 0)))
        arrs.append(cst)
    n_rows, n_c, n_o, n_s = len(rows), len(consts), len(outs), len(sums)
    out_specs = [pl.BlockSpec((tm, w), lambda i: (i, 0)) for w, _ in outs]
    out_specs += [pl.BlockSpec(s, lambda i: (0, 0)) for s in sums]
    out_shape = [jax.ShapeDtypeStruct((L, w), d) for w, d in outs]
    out_shape += [jax.ShapeDtypeStruct(s, f32) for s in sums]

    def body(*refs):
        i = pl.program_id(0)
        vals = [r[...] for r in refs[:n_rows + n_c]]
        res = fn(i, nb, *vals)
        if not isinstance(res, (tuple, list)):
            res = (res,)
        o_refs = refs[n_rows + n_c:n_rows + n_c + n_o]
        s_refs = refs[n_rows + n_c + n_o:]
        for o_ref, o in zip(o_refs, res[:n_o]):
            o_ref[...] = o.astype(o_ref.dtype)
        if n_s:
            @pl.when(i == 0)
            def _():
                for s_ref in s_refs:
                    s_ref[...] = jnp.zeros_like(s_ref)

            for s_ref, s in zip(s_refs, res[n_o:]):
                s_ref[...] += s

    res = pl.pallas_call(
        body,
        grid=(nb,),
        in_specs=in_specs,
        out_specs=out_specs,
        out_shape=out_shape,
        compiler_params=_cp("arbitrary" if n_s else "parallel"),
        name=name,
    )(*arrs)
    return res[0] if len(res) == 1 else tuple(res)


def _shift_down(x, prev8, k):
    cat = jnp.concatenate([prev8, x], axis=0)
    return pltpu.roll(cat, k, 0)[prev8.shape[0]:, :]


def _shift_up(x, next8, k):
    n = x.shape[0]
    cat = jnp.concatenate([x, next8], axis=0)
    return pltpu.roll(cat, n + next8.shape[0] - k, 0)[:n, :]


def _colsum(x):
    return jnp.sum(x, axis=0, keepdims=True)


def _silu(x):
    return x * jax.nn.sigmoid(x)


def _modulate_fn(x, nw, sc, sh):
    r = lax.rsqrt(jnp.mean(x * x, axis=-1, keepdims=True) + EPS)
    return (x * r * nw) * (1.0 + sc) + sh


def modulate_fwd(x, nw, sc, sh, name):
    L, D = x.shape

    def fn(i, nb, xt, nwv, scv, shv):
        return _modulate_fn(xt, nwv, scv, shv)

    return rowwise(fn, name=name, L=L, tm=_pick(L, 512, SUBLANE), rows=[(x, 0, D, "cur")],
                   consts=[nw, sc, sh], outs=[(D, bf16)])


def modulate_bwd(x, nw, sc, sh, dh, dx_in, name):
    L, D = x.shape

    def fn(i, nb, xt, dht, dxt, nwv, scv, shv):
        _, vjp = jax.vjp(_modulate_fn, xt, nwv, scv, shv)
        dx, dnw, dsc, dsh = vjp(dht)
        return dxt + dx, dnw, dsc, dsh

    return rowwise(fn, name=name, L=L, tm=_pick(L, 256, SUBLANE),
                   rows=[(x, 0, D, "cur"), (dh, 0, D, "cur"), (dx_in, 0, D, "cur")],
                   consts=[nw, sc, sh], outs=[(D, f32)], sums=[(1, D)] * 3)


def resid_bwd(dx, y, g, name):
    L, D = dx.shape

    def fn(i, nb, dxt, yt, gv):
        return dxt * gv, _colsum(dxt * yt)

    return rowwise(fn, name=name, L=L, tm=_pick(L, 512, SUBLANE),
                   rows=[(dx, 0, D, "cur"), (y, 0, D, "cur")], consts=[g],
                   outs=[(D, bf16)], sums=[(1, D)])


def _resid_epi(acc, xt, gv):
    return acc, xt + gv * acc


def _stack_rows(rows, n=SUBLANE):
    c = rows[0].shape[1]
    ridx = lax.broadcasted_iota(jnp.int32, (n, c), 0)
    out = jnp.zeros((n, c), f32)
    for j, r in enumerate(rows):
        out = out + jnp.where(ridx == j, r, 0.0)
    return out


def _conv_causal(x, prev8, w):
    W = w.shape[0]
    y = x * w[W - 1:W, :]
    for j in range(W - 1):
        y = y + _shift_down(x, prev8, W - 1 - j) * w[j:j + 1, :]
    return y


def _conv_causal_bwd_x(dy, next8, w):
    W = w.shape[0]
    dx = dy * w[W - 1:W, :]
    for j in range(W - 1):
        dx = dx + _shift_up(dy, next8, W - 1 - j) * w[j:j + 1, :]
    return dx


def _conv_causal_bwd_w(dy, x, prev8, W):
    rows = [_colsum(dy * _shift_down(x, prev8, W - 1 - j)) for j in range(W - 1)]
    rows.append(_colsum(dy * x))
    return _stack_rows(rows)


def ffn_act_fwd(up, conv_w, name):
    L, F2 = up.shape
    F = F2 // 2

    def fn(i, nb, u, p8, w):
        c = _conv_causal(u.astype(f32), p8.astype(f32) * (i > 0).astype(f32), w)
        return _silu(c[:, :F]) * c[:, F:]

    return rowwise(fn, name=name, L=L, tm=_pick(L, 128, SUBLANE),
                   rows=[(up, 0, F2, "cur"), (up, 0, F2, "prev")], consts=[conv_w], outs=[(F, bf16)])


def ffn_act_conv_bwd(up, conv_w, dact, name):
    L, F2 = up.shape
    F = F2 // 2
    W = conv_w.shape[0]

    def fn(i, nb, u, da, p8, un8, dan8, w):
        tm, ext = u.shape[0], un8.shape[0]
        more = (i < nb - 1).astype(f32)
        u, da = u.astype(f32), da.astype(f32)
        p8 = p8.astype(f32) * (i > 0).astype(f32)
        c = _conv_causal(jnp.concatenate([u, un8.astype(f32) * more], axis=0), p8, w)
        dae = jnp.concatenate([da, dan8.astype(f32) * more], axis=0)
        a, b = c[:, :F], c[:, F:]
        sg = jax.nn.sigmoid(a)
        dc = jnp.concatenate([dae * b * (sg * (1.0 + a * (1.0 - sg))), dae * a * sg], axis=1)
        dx = dc[:tm] * w[W - 1:W, :]
        for j in range(W - 1):
            dx = dx + pltpu.roll(dc, tm + ext - (W - 1 - j), 0)[:tm] * w[j:j + 1, :]
        return dx, _conv_causal_bwd_w(dc[:tm], u, p8, W)

    return rowwise(fn, name=name, L=L, tm=_pick(L, 128, SUBLANE),
                   rows=[(up, 0, F2, "cur"), (dact, 0, F, "cur"), (up, 0, F2, "prev"), (up, 0, F2, "next"),
                         (dact, 0, F, "next")],
                   consts=[conv_w], outs=[(F2, bf16)], sums=[(SUBLANE, F2)])


ALIBI = [2.0 ** (-8.0 * (i + 1) / N_ATTN_HEADS) for i in range(N_ATTN_HEADS)]
NEG = -1e30


class _Band:
    def __init__(self, dilation, group_a):
        d = dilation
        self.d = d
        self.group_a = group_a
        if group_a:
            self.P, self.qw, self.hps, self.kvw = 1, 512, 8, 128
            self.qcol = lambda p: 0
            self.kcol = lambda p: 4
            self.vcol = lambda p: 5
            self.kv_of = lambda j: j // 4
            self.max_dist = A_WINDOW - 1
            sl = np.repeat(np.asarray(ALIBI[:8], np.float32), HEAD_DIM)[None, None, :]
        else:
            self.P, self.qw, self.hps, self.kvw = 2 * d, 256, 4, 256
            self.qcol = lambda p: lax.div(p, 2) * 9 + 3 + lax.rem(p, 2)
            self.kcol = lambda p: lax.div(p, 2) * 9 + 5 + lax.rem(p, 2)
            self.vcol = lambda p: lax.div(p, 2) * 9 + 7 + lax.rem(p, 2)
            self.kv_of = lambda j: j
            self.max_dist = BLOCK
            per = np.repeat(np.asarray(ALIBI[8:], np.float32), HEAD_DIM).reshape(2, 1, 256)
            sl = np.tile(per, (d, 1, 1))
        self.slopes = jnp.asarray(sl, f32)


def _band_mask(n, d, max_dist):
    qi = lax.broadcasted_iota(jnp.int32, (BLOCK, 2 * BLOCK), 0)
    kj = lax.broadcasted_iota(jnp.int32, (BLOCK, 2 * BLOCK), 1)
    dist = BLOCK + qi - kj
    valid = (dist >= 0) & (dist <= max_dist) & ((n > 0) | (kj >= BLOCK))
    return valid, -(d * dist).astype(f32)


def _rms64(x, w):
    r = lax.rsqrt(jnp.mean(x * x, axis=-1, keepdims=True) + EPS)
    xh = x * r
    return xh * w, xh, r


def _rms64_bwd(dy, xh, r, w):
    t = dy * w
    dw = jnp.sum(jnp.sum(dy * xh, axis=0), axis=0, keepdims=True)
    return r * (t - xh * jnp.mean(t * xh, axis=-1, keepdims=True)), dw


def _heads64(x, heads):
    return jnp.stack([x[:, h * 64:(h + 1) * 64] for h in heads])


def attn_fwd(hv, band, wq, wk, name):
    M = hv.shape[0]
    nb = M // BLOCK
    P, qw, hps = band.P, band.qw, band.hps
    d, max_dist, kv_of = band.d, band.max_dist, band.kv_of
    kv_heads = sorted({kv_of(j) for j in range(hps)})
    kv_pos = {h: i for i, h in enumerate(kv_heads)}
    gqa = len(kv_heads) != hps

    def body(q_ref, kp_ref, kc_ref, vp_ref, vc_ref, sl_ref, wq_ref, wk_ref, o_ref, lse_ref):
        n = pl.program_id(1)
        valid, negd = _band_mask(n, d, max_dist)
        kblk = jnp.concatenate([kp_ref[...], kc_ref[...]], axis=0)
        vblk = jnp.concatenate([vp_ref[...], vc_ref[...]], axis=0)
        q = _heads64(q_ref, range(hps))
        kn = _rms64(_heads64(kblk, kv_heads), wk_ref[...])[0].astype(bf16)
        v = _heads64(vblk, kv_heads).astype(bf16)
        kn_q = jnp.stack([kn[kv_pos[kv_of(j)]] for j in range(hps)]) if gqa else kn
        v_q = jnp.stack([v[kv_pos[kv_of(j)]] for j in range(hps)]) if gqa else v
        qn = _rms64(q, wq_ref[...])[0].astype(bf16)
        slope = jnp.stack([sl_ref[0, :, j * 64:j * 64 + 1] for j in range(hps)])
        s = _dot(qn, kn_q, BNT) * (HEAD_DIM ** -0.5) + slope * negd
        s = jnp.where(valid, s, NEG)
        m = jnp.max(s, axis=-1, keepdims=True)
        p = jnp.exp(s - m)
        l = jnp.sum(p, axis=-1, keepdims=True)
        o = _dot(p.astype(bf16), v_q, BNN) / l
        lse = m + jnp.log(l)
        for j in range(hps):
            o_ref[:, j * 64:(j + 1) * 64] = o[j]
            lse_ref[:, j * 64:(j + 1) * 64] = jnp.broadcast_to(lse[j], (BLOCK, 64))

    qcol, kcol, vcol, kvw = band.qcol, band.kcol, band.vcol, band.kvw
    in_specs = [
        pl.BlockSpec((BLOCK, qw), lambda p, n: (n, qcol(p))),
        pl.BlockSpec((BLOCK, kvw), lambda p, n: (jnp.maximum(n - 1, 0), kcol(p))),
        pl.BlockSpec((BLOCK, kvw), lambda p, n: (n, kcol(p))),
        pl.BlockSpec((BLOCK, kvw), lambda p, n: (jnp.maximum(n - 1, 0), vcol(p))),
        pl.BlockSpec((BLOCK, kvw), lambda p, n: (n, vcol(p))),
        pl.BlockSpec((1, 1, qw), lambda p, n: (p, 0, 0)),
        pl.BlockSpec((1, 64), lambda p, n: (0, 0)),
        pl.BlockSpec((1, 64), lambda p, n: (0, 0)),
    ]
    o_spec = pl.BlockSpec((BLOCK, qw), lambda p, n: (n, p))
    return pl.pallas_call(
        body, grid=(P, nb), in_specs=in_specs, out_specs=[o_spec, o_spec],
        out_shape=[jax.ShapeDtypeStruct((M, P * qw), f32)] * 2,
        compiler_params=_cp("parallel", "parallel"), name=name,
    )(hv, hv, hv, hv, hv, band.slopes, wq, wk)


def attn_bwd(hv, band, wq, wk, o, lse, do, dlse, dw0, name):
    M = hv.shape[0]
    nb = M // BLOCK
    P, qw, hps = band.P, band.qw, band.hps
    d, max_dist, kv_of = band.d, band.max_dist, band.kv_of
    kv_heads = sorted({kv_of(j) for j in range(hps)})
    kv_pos = {h: i for i, h in enumerate(kv_heads)}
    gqa = len(kv_heads) != hps

    def body(q_ref, kp_ref, kc_ref, vp_ref, vc_ref, sl_ref, wq_ref, wk_ref, o_ref, lse_ref, do_ref, dlse_ref,
             dwq0_ref, dwk0_ref, dq_ref, dk_ref, dv_ref, dwq_ref, dwk_ref, ck, cv):
        pp = pl.program_id(0)
        n = pl.program_id(1)

        @pl.when((pp == 0) & (n == 0))
        def _():
            dwq_ref[...] = dwq0_ref[...]
            dwk_ref[...] = dwk0_ref[...]

        @pl.when(n == 0)
        def _():
            ck[...] = jnp.zeros_like(ck)
            cv[...] = jnp.zeros_like(cv)

        @pl.when(n < nb)
        def _():
            valid, negd = _band_mask(n, d, max_dist)
            kblk = jnp.concatenate([kp_ref[...], kc_ref[...]], axis=0)
            vblk = jnp.concatenate([vp_ref[...], vc_ref[...]], axis=0)
            wqv, wkv = wq_ref[...], wk_ref[...]
            hs = range(hps)
            kn_f, kh, rk = _rms64(_heads64(kblk, kv_heads), wkv)
            kn = kn_f.astype(bf16)
            v = _heads64(vblk, kv_heads).astype(bf16)
            kn_q = jnp.stack([kn[kv_pos[kv_of(j)]] for j in hs]) if gqa else kn
            v_q = jnp.stack([v[kv_pos[kv_of(j)]] for j in hs]) if gqa else v
            qn_f, qh, rq = _rms64(_heads64(q_ref, hs), wqv)
            qn = qn_f.astype(bf16)
            col = lambda ref: jnp.stack([ref[:, j * 64:j * 64 + 1] for j in hs])
            slope = jnp.stack([sl_ref[0, :, j * 64:j * 64 + 1] for j in hs])
            s = _dot(qn, kn_q, BNT) * (HEAD_DIM ** -0.5) + slope * negd
            p = jnp.where(valid, jnp.exp(s - col(lse_ref)), 0.0)
            do_h = _heads64(do_ref, hs)
            delta = jnp.sum(do_h * _heads64(o_ref, hs), axis=-1, keepdims=True)
            do_b = do_h.astype(bf16)
            dp = _dot(do_b, v_q, BNT)
            ds = (p * (dp - delta + col(dlse_ref))).astype(bf16)
            dqn = _dot(ds, kn_q, BNN) * (HEAD_DIM ** -0.5)
            dkn_q = _dot(ds, qn, BTN) * (HEAD_DIM ** -0.5)
            dv_q = _dot(p.astype(bf16), do_b, BTN)
            if gqa:
                grp = lambda t: jnp.stack([sum(t[j] for j in hs if kv_of(j) == h) for h in kv_heads])
                dkn_q, dv_q = grp(dkn_q), grp(dv_q)
            dq, dwq_acc = _rms64_bwd(dqn, qh, rq, wqv)
            for j in hs:
                dq_ref[:, j * 64:(j + 1) * 64] = dq[j]
            dwq_ref[...] += dwq_acc
            dk_h, dwk_acc = _rms64_bwd(dkn_q, kh, rk, wkv)
            dwk_ref[...] += dwk_acc
            dk_all = jnp.concatenate([dk_h[i] for i in range(len(kv_heads))], axis=1)
            dv_all = jnp.concatenate([dv_q[i] for i in range(len(kv_heads))], axis=1)
            dk_ref[...] = ck[...] + dk_all[:BLOCK]
            dv_ref[...] = cv[...] + dv_all[:BLOCK]
            ck[...] = dk_all[BLOCK:]
            cv[...] = dv_all[BLOCK:]

        @pl.when(n == nb)
        def _():
            dk_ref[...] = ck[...]
            dv_ref[...] = cv[...]

    qcol, kcol, vcol, kvw = band.qcol, band.kcol, band.vcol, band.kvw
    cl = lambda n: jnp.minimum(n, nb - 1)
    pv = lambda n: jnp.maximum(jnp.minimum(n, nb - 1) - 1, 0)
    o_in = pl.BlockSpec((BLOCK, qw), lambda p, n: (cl(n), p))
    in_specs = [
        pl.BlockSpec((BLOCK, qw), lambda p, n: (cl(n), qcol(p))),
        pl.BlockSpec((BLOCK, kvw), lambda p, n: (pv(n), kcol(p))),
        pl.BlockSpec((BLOCK, kvw), lambda p, n: (cl(n), kcol(p))),
        pl.BlockSpec((BLOCK, kvw), lambda p, n: (pv(n), vcol(p))),
        pl.BlockSpec((BLOCK, kvw), lambda p, n: (cl(n), vcol(p))),
        pl.BlockSpec((1, 1, qw), lambda p, n: (p, 0, 0)),
        pl.BlockSpec((1, 64), lambda p, n: (0, 0)),
        pl.BlockSpec((1, 64), lambda p, n: (0, 0)),
        o_in, o_in, o_in, o_in,
        pl.BlockSpec((1, 64), lambda p, n: (0, 0)),
        pl.BlockSpec((1, 64), lambda p, n: (0, 0)),
    ]
    kv_out = pl.BlockSpec((BLOCK, kvw), lambda p, n: (jnp.maximum(n - 1, 0), p))
    w_out = pl.BlockSpec((1, 64), lambda p, n: (0, 0))
    return pl.pallas_call(
        body, grid=(P, nb + 1), in_specs=in_specs,
        out_specs=[o_in, kv_out, kv_out, w_out, w_out],
        out_shape=[jax.ShapeDtypeStruct((M, P * qw), f32), jax.ShapeDtypeStruct((M, P * kvw), f32),
                   jax.ShapeDtypeStruct((M, P * kvw), f32), jax.ShapeDtypeStruct((1, 64), f32),
                   jax.ShapeDtypeStruct((1, 64), f32)],
        scratch_shapes=[pltpu.VMEM((BLOCK, kvw), f32), pltpu.VMEM((BLOCK, kvw), f32)],
        compiler_params=_cp("arbitrary", "arbitrary"), name=name,
    )(hv, hv, hv, hv, hv, band.slopes, wq, wk, o, lse, do, dlse, *dw0)


class _Plan:
    def __init__(self, dilation, group_a, nq):
        self.d, self.nq = dilation, nq
        if group_a:
            self.P, self.nkv = 1, 1
            self.q0, self.k0, self.v0 = 0, 4, 5
            self.kv_of = lambda j: j // 4
            self.max_dist = A_WINDOW - 1
            slopes = ALIBI[:8]
        else:
            self.P, self.nkv = 4 // nq, nq
            self.q0, self.k0, self.v0 = 6, 10, 14
            self.kv_of = lambda j: j
            self.max_dist = BLOCK
            slopes = ALIBI[8:]
        self.hps = 2 * nq
        sl = np.repeat(np.asarray(slopes, np.float32), HEAD_DIM).reshape(self.P, 1, self.hps * HEAD_DIM)
        self.slopes = jnp.asarray(sl, f32)


def _rows(r, d):
    return pl.ds(r, BLOCK, stride=d) if d > 1 else pl.ds(0, BLOCK)


def _pairs(refs, rows):
    parts = []
    for ref in refs:
        blk = ref[rows, :]
        parts += [blk[:, :HEAD_DIM], blk[:, HEAD_DIM:]]
    return jnp.stack(parts)


def _pairs2(prev_refs, cur_refs, rows):
    parts = []
    for pr, cr in zip(prev_refs, cur_refs):
        blk = jnp.concatenate([pr[rows, :], cr[rows, :]], axis=0)
        parts += [blk[:, :HEAD_DIM], blk[:, HEAD_DIM:]]
    return jnp.stack(parts)


def _lane_pair(t, i):
    return jnp.concatenate([t[2 * i], t[2 * i + 1]], axis=1)


def _riding(body, coll, n_in, n_out, grid):
    if coll is None:
        return body

    def wrapped(*refs):
        ride_refs = (refs[n_in], refs[n_in + 1 + n_out]) + tuple(refs[-3:])
        inner = refs[:n_in] + refs[n_in + 1:n_in + 1 + n_out] + refs[n_in + 2 + n_out:-3]
        pid = [pl.program_id(t) for t in range(len(grid))]

        @pl.when(functools.reduce(jnp.logical_and, [p == 0 for p in pid]))
        def _():
            coll.start(*ride_refs)

        body(*inner)

        @pl.when(functools.reduce(jnp.logical_and, [p == g - 1 for p, g in zip(pid, grid)]))
        def _():
            coll.wait(*ride_refs)

    return wrapped


def _ride_call(body, coll, ride, *, grid, in_specs, out_specs, out_shape, scratch_shapes, semantics, name, args):
    hbm = [pl.BlockSpec(memory_space=pl.ANY)] if coll else []
    return pl.pallas_call(
        _riding(body, coll, len(in_specs), len(out_specs), grid), grid=grid,
        in_specs=list(in_specs) + hbm, out_specs=list(out_specs) + hbm,
        out_shape=list(out_shape) + ([coll.out_shape] if coll else []),
        scratch_shapes=list(scratch_shapes) + (coll.scratch if coll else []),
        compiler_params=_cp(*(["arbitrary"] * len(grid) if coll else semantics)), name=name,
    )(*args, *([ride[0]] if coll else []))


def attn2_fwd(hin, plan, wq, wk, name, ride=None):
    coll = _Coll(*ride) if ride else None
    L = hin.shape[0]
    d, nq, nkv, hps, P = plan.d, plan.nq, plan.nkv, plan.hps, plan.P
    R = BLOCK * d
    nb = L // R
    kv_of, max_dist = plan.kv_of, plan.max_dist
    gqa = 2 * nkv != hps

    def body(*refs):
        q_refs = refs[:nq]
        kp, kc = refs[nq:nq + nkv], refs[nq + nkv:nq + 2 * nkv]
        vp, vc = refs[nq + 2 * nkv:nq + 3 * nkv], refs[nq + 3 * nkv:nq + 4 * nkv]
        sl_ref, wq_ref, wk_ref, o_ref, lse_ref = refs[nq + 4 * nkv:nq + 4 * nkv + 5]
        o_refs = refs[nq + 4 * nkv + 5:2 * nq + 4 * nkv + 5]
        lse_refs = refs[2 * nq + 4 * nkv + 5:]
        n = pl.program_id(1)
        valid, negd = _band_mask(n, d, max_dist)
        slope = jnp.stack([sl_ref[0, :, j * 64:j * 64 + 1] for j in range(hps)])
        wqv, wkv = wq_ref[...], wk_ref[...]

        def residue(r, carry):
            rows = _rows(r, d)
            kn = _rms64(_pairs2(kp, kc, rows), wkv)[0].astype(bf16)
            v = _pairs2(vp, vc, rows).astype(bf16)
            if gqa:
                kn = jnp.stack([kn[kv_of(j)] for j in range(hps)])
                v = jnp.stack([v[kv_of(j)] for j in range(hps)])
            qn = _rms64(_pairs(q_refs, rows), wqv)[0].astype(bf16)
            s = _dot(qn, kn, BNT) * (HEAD_DIM ** -0.5) + slope * negd
            s = jnp.where(valid, s, NEG)
            m = jnp.max(s, axis=-1, keepdims=True)
            p = jnp.exp(s - m)
            l = jnp.sum(p, axis=-1, keepdims=True)
            o = _dot(p.astype(bf16), v, BNN) / l
            lse = jnp.broadcast_to(m + jnp.log(l), (hps, BLOCK, HEAD_DIM))
            for i in range(nq):
                o_refs[i][rows, :] = _lane_pair(o, i)
                lse_refs[i][rows, :] = _lane_pair(lse, i)
            return carry

        lax.fori_loop(0, d, residue, 0)
        for i in range(nq):
            o_ref[:, i * 128:(i + 1) * 128] = o_refs[i][...]
            lse_ref[:, i * 128:(i + 1) * 128] = lse_refs[i][...]

    col = lambda c0, i: (lambda p, n: (n, c0 + p * nq + i))
    prv = lambda c0, i: (lambda p, n: (jnp.maximum(n - 1, 0), c0 + p * nq + i))
    blk = lambda f: pl.BlockSpec((R, 128), f)
    in_specs = [blk(col(plan.q0, i)) for i in range(nq)]
    in_specs += [blk(prv(plan.k0, i)) for i in range(nkv)] + [blk(col(plan.k0, i)) for i in range(nkv)]
    in_specs += [blk(prv(plan.v0, i)) for i in range(nkv)] + [blk(col(plan.v0, i)) for i in range(nkv)]
    in_specs += [pl.BlockSpec((1, 1, hps * 64), lambda p, n: (p, 0, 0)),
                 pl.BlockSpec((1, 64), lambda p, n: (0, 0)), pl.BlockSpec((1, 64), lambda p, n: (0, 0))]
    wide = pl.BlockSpec((R, 128 * nq), lambda p, n: (n, p))
    return _ride_call(
        body, coll, ride, grid=(P, nb), in_specs=in_specs, out_specs=[wide, wide],
        out_shape=[jax.ShapeDtypeStruct((L, 512), f32)] * 2,
        scratch_shapes=[pltpu.VMEM((R, 128), f32)] * (2 * nq),
        semantics=("parallel", "parallel"), name=name,
        args=[hin] * (nq + 4 * nkv) + [plan.slopes, wq, wk])


def attn2_bwd(hin, plan, wq, wk, o, lse, do, dlse, dw0, name, ride=None):
    coll = _Coll(*ride) if ride else None
    L = hin.shape[0]
    d, nq, nkv, hps, P = plan.d, plan.nq, plan.nkv, plan.hps, plan.P
    R = BLOCK * d
    nb = L // R
    kv_of, max_dist = plan.kv_of, plan.max_dist
    nkh = 2 * nkv
    gqa = nkh != hps
    n_in = nq + 4 * nkv + 3 + 4 * nq + 2

    def body(*refs):
        q_refs = refs[:nq]
        kp, kc = refs[nq:nq + nkv], refs[nq + nkv:nq + 2 * nkv]
        vp, vc = refs[nq + 2 * nkv:nq + 3 * nkv], refs[nq + 3 * nkv:nq + 4 * nkv]
        b = nq + 4 * nkv
        sl_ref, wq_ref, wk_ref = refs[b:b + 3]
        b += 3
        o_refs, lse_refs = refs[b:b + nq], refs[b + nq:b + 2 * nq]
        do_refs, dlse_refs = refs[b + 2 * nq:b + 3 * nq], refs[b + 3 * nq:b + 4 * nq]
        dwq0_ref, dwk0_ref = refs[b + 4 * nq:b + 4 * nq + 2]
        dq_ref, dk_ref, dv_ref, dwq_ref, dwk_ref = refs[n_in:n_in + 5]
        sc = refs[n_in + 5:]
        dq_s, dk_s, dv_s = sc[:nq], sc[nq:nq + nkv], sc[nq + nkv:nq + 2 * nkv]
        ck, cv = sc[nq + 2 * nkv:nq + 3 * nkv], sc[nq + 3 * nkv:]
        pp = pl.program_id(0)
        n = pl.program_id(1)

        @pl.when((pp == 0) & (n == 0))
        def _():
            dwq_ref[...] = dwq0_ref[...]
            dwk_ref[...] = dwk0_ref[...]

        @pl.when(n == 0)
        def _():
            for c in (*ck, *cv):
                c[...] = jnp.zeros_like(c)

        @pl.when(n < nb)
        def _():
            valid, negd = _band_mask(n, d, max_dist)
            slope = jnp.stack([sl_ref[0, :, j * 64:j * 64 + 1] for j in range(hps)])
            wqv, wkv = wq_ref[...], wk_ref[...]
            hs = range(hps)

            def residue(r, carry):
                rows = _rows(r, d)
                kn_f, kh, rk = _rms64(_pairs2(kp, kc, rows), wkv)
                kn = kn_f.astype(bf16)
                v = _pairs2(vp, vc, rows).astype(bf16)
                if gqa:
                    kn = jnp.stack([kn[kv_of(j)] for j in hs])
                    v = jnp.stack([v[kv_of(j)] for j in hs])
                qn_f, qh, rq = _rms64(_pairs(q_refs, rows), wqv)
                qn = qn_f.astype(bf16)
                s = _dot(qn, kn, BNT) * (HEAD_DIM ** -0.5) + slope * negd
                p = jnp.where(valid, jnp.exp(s - _pairs(lse_refs, rows)[:, :, :1]), 0.0)
                do_h = _pairs(do_refs, rows)
                delta = jnp.sum(do_h * _pairs(o_refs, rows), axis=-1, keepdims=True)
                do_b = do_h.astype(bf16)
                dp = _dot(do_b, v, BNT)
                ds = (p * (dp - delta + _pairs(dlse_refs, rows)[:, :, :1])).astype(bf16)
                dqn = _dot(ds, kn, BNN) * (HEAD_DIM ** -0.5)
                dkn = _dot(ds, qn, BTN) * (HEAD_DIM ** -0.5)
                dvv = _dot(p.astype(bf16), do_b, BTN)
                if gqa:
                    grp = lambda t: jnp.stack([sum(t[j] for j in hs if kv_of(j) == h) for h in range(nkh)])
                    dkn, dvv = grp(dkn), grp(dvv)
                dq, dwq = _rms64_bwd(dqn, qh, rq, wqv)
                dk, dwk = _rms64_bwd(dkn, kh, rk, wkv)
                for i in range(nq):
                    dq_s[i][rows, :] = _lane_pair(dq, i)
                for i in range(nkv):
                    dk_s[i][rows, :] = ck[i][rows, :] + _lane_pair(dk[:, :BLOCK], i)
                    dv_s[i][rows, :] = cv[i][rows, :] + _lane_pair(dvv[:, :BLOCK], i)
                    ck[i][rows, :] = _lane_pair(dk[:, BLOCK:], i)
                    cv[i][rows, :] = _lane_pair(dvv[:, BLOCK:], i)
                return carry[0] + dwq, carry[1] + dwk

            zero = jnp.zeros((1, HEAD_DIM), f32)
            dwq_a, dwk_a = lax.fori_loop(0, d, residue, (zero, zero))
            dwq_ref[...] += dwq_a
            dwk_ref[...] += dwk_a
            for i in range(nq):
                dq_ref[:, i * 128:(i + 1) * 128] = dq_s[i][...]
            for i in range(nkv):
                dk_ref[:, i * 128:(i + 1) * 128] = dk_s[i][...]
                dv_ref[:, i * 128:(i + 1) * 128] = dv_s[i][...]

        @pl.when(n == nb)
        def _():
            for i in range(nkv):
                dk_ref[:, i * 128:(i + 1) * 128] = ck[i][...]
                dv_ref[:, i * 128:(i + 1) * 128] = cv[i][...]

    cl = lambda n: jnp.minimum(n, nb - 1)
    pv = lambda n: jnp.maximum(jnp.minimum(n, nb - 1) - 1, 0)
    col = lambda c0, i: (lambda p, n: (cl(n), c0 + p * nq + i))
    prv = lambda c0, i: (lambda p, n: (pv(n), c0 + p * nq + i))
    blk = lambda f: pl.BlockSpec((R, 128), f)
    w64 = pl.BlockSpec((1, 64), lambda p, n: (0, 0))
    in_specs = [blk(col(plan.q0, i)) for i in range(nq)]
    in_specs += [blk(prv(plan.k0, i)) for i in range(nkv)] + [blk(col(plan.k0, i)) for i in range(nkv)]
    in_specs += [blk(prv(plan.v0, i)) for i in range(nkv)] + [blk(col(plan.v0, i)) for i in range(nkv)]
    in_specs += [pl.BlockSpec((1, 1, hps * 64), lambda p, n: (p, 0, 0)), w64, w64]
    in_specs += [blk(col(0, i)) for i in range(nq)] * 4 + [w64, w64]
    kvw = 128 * nkv
    out_specs = [pl.BlockSpec((R, 128 * nq), lambda p, n: (cl(n), p)),
                 pl.BlockSpec((R, kvw), lambda p, n: (jnp.maximum(n - 1, 0), p)),
                 pl.BlockSpec((R, kvw), lambda p, n: (jnp.maximum(n - 1, 0), p)), w64, w64]
    same = lambda a: [a] * nq
    return _ride_call(
        body, coll, ride, grid=(P, nb + 1), in_specs=in_specs, out_specs=out_specs,
        out_shape=[jax.ShapeDtypeStruct((L, 512), f32), jax.ShapeDtypeStruct((L, kvw * P), f32),
                   jax.ShapeDtypeStruct((L, kvw * P), f32), jax.ShapeDtypeStruct((1, 64), f32),
                   jax.ShapeDtypeStruct((1, 64), f32)],
        scratch_shapes=[pltpu.VMEM((R, 128), f32)] * (nq + 4 * nkv),
        semantics=("arbitrary", "arbitrary"), name=name,
        args=[hin] * (nq + 4 * nkv) + [plan.slopes, wq, wk, *same(o), *same(lse), *same(do), *same(dlse), *dw0])


def _head_sum(x):
    c = x.shape[1]
    r = lax.broadcasted_iota(jnp.int32, (c, c), 0) // HEAD_DIM
    q = lax.broadcasted_iota(jnp.int32, (c, c), 1) // HEAD_DIM
    return _dot(x, (r == q).astype(f32), precision=HI)


def attn_merge_fwd(oa, la, obs, lbs, sinkb, name):
    L = oa.shape[0]

    def fn(i, nb, oa_t, la_t, o1, o2, o3, l1, l2, l3, sk):
        ya = oa_t * jax.nn.sigmoid(la_t - sk)
        m = jnp.maximum(jnp.maximum(l1, l2), l3)
        e1, e2, e3 = jnp.exp(l1 - m), jnp.exp(l2 - m), jnp.exp(l3 - m)
        yb = (e1 * o1 + e2 * o2 + e3 * o3) / (e1 + e2 + e3)
        return jnp.concatenate([ya, yb], axis=1)

    rows = [(a, 0, 512, "cur") for a in (oa, la, *obs, *lbs)]
    return rowwise(fn, name=name, L=L, tm=_pick(L, 256, SUBLANE), rows=rows, consts=[sinkb], outs=[(1024, bf16)])


def attn_merge_bwd(dcat, oa, la, obs, lbs, sinkb, name):
    L = oa.shape[0]

    def fn(i, nb, da, db, oa_t, la_t, o1, o2, o3, l1, l2, l3, sk):
        keep = jax.nn.sigmoid(la_t - sk)
        dla = _head_sum(da * oa_t) * keep * (1.0 - keep)
        m = jnp.maximum(jnp.maximum(l1, l2), l3)
        e1, e2, e3 = jnp.exp(l1 - m), jnp.exp(l2 - m), jnp.exp(l3 - m)
        z = e1 + e2 + e3
        w1, w2, w3 = e1 / z, e2 / z, e3 / z
        g1, g2, g3 = _head_sum(db * o1), _head_sum(db * o2), _head_sum(db * o3)
        gm = w1 * g1 + w2 * g2 + w3 * g3
        return (da * keep, dla, w1 * db, w2 * db, w3 * db,
                w1 * (g1 - gm), w2 * (g2 - gm), w3 * (g3 - gm), -_colsum(dla))

    rows = [(dcat, 0, 512, "cur"), (dcat, 512, 512, "cur")] + [(a, 0, 512, "cur") for a in (oa, la, *obs, *lbs)]
    return rowwise(fn, name=name, L=L, tm=_pick(L, 256, SUBLANE), rows=rows, consts=[sinkb],
                   outs=[(512, f32)] * 8, sums=[(1, 512)])


def attn_assemble(dqa, dka, dva, dqs, dks, dvs, name):
    L = dqa.shape[0]

    def fn(i, nb, qa, ka, va, q1, q2, q3, k1, k2, k3, v1, v2, v3):
        return jnp.concatenate([qa, ka, va, q1 + q2 + q3, k1 + k2 + k3, v1 + v2 + v3], axis=1)

    rows = [(dqa, 0, 512, "cur"), (dka, 0, 128, "cur"), (dva, 0, 128, "cur")]
    rows += [(a, 0, 512, "cur") for a in (*dqs, *dks, *dvs)]
    return rowwise(fn, name=name, L=L, tm=_pick(L, 256, SUBLANE), rows=rows, outs=[(ATTN_IN, bf16)])


def attention_fwd(hin, wqa, wka, wqb, wkb, sinkb, ride=None):
    oa, la = attn2_fwd(hin, _Plan(1, True, 4), wqa, wka, "attn_a_fwd")
    obs, lbs, got = [], [], None
    for _, d in B_BRANCHES:
        res = attn2_fwd(hin, _Plan(d, False, 2), wqb, wkb, f"attn_b{d}_fwd", ride=ride if d == 1 else None)
        obs.append(res[0])
        lbs.append(res[1])
        got = res[2] if (d == 1 and ride) else got
    ocat = attn_merge_fwd(oa, la, obs, lbs, sinkb, "attn_merge_fwd")
    return ocat, (oa, la, obs, lbs), got


def attention_bwd(hin, wqa, wka, wqb, wkb, sinkb, saved, dcat, rides=None):
    rides = rides or {}
    gots = {}
    oa, la, obs, lbs = saved
    res = attn_merge_bwd(dcat, oa, la, obs, lbs, sinkb, "attn_merge_bwd")
    doa, dla, dos, dls, dsink = res[0], res[1], res[2:5], res[5:8], res[8]
    zero = jnp.zeros((1, 64), f32)
    dqa, dka, dva, dwqa, dwka = attn2_bwd(hin, _Plan(1, True, 4), wqa, wka, oa, la, doa, dla, (zero, zero), "attn_a_bwd")
    dqs, dks, dvs = [], [], []
    dwqb = dwkb = zero
    for g, (_, d) in enumerate(B_BRANCHES):
        res = attn2_bwd(hin, _Plan(d, False, 2 if d < 16 else 1), wqb, wkb, obs[g], lbs[g],
                        dos[g], dls[g], (dwqb, dwkb), f"attn_b{d}_bwd", ride=rides.get(d))
        dq, dk, dv, dwqb, dwkb = res[:5]
        if d in rides:
            gots[d] = res[5]
        dqs.append(dq)
        dks.append(dk)
        dvs.append(dv)
    dhin = attn_assemble(dqa, dka, dva, dqs, dks, dvs, "attn_assemble")
    return dhin, dwqa, dwka, dwqb, dwkb, dsink, gots


NS = S5_GROUPS * S5_STATE


def _s5_param_fn(lr, li, ldt):
    dt = jnp.exp(ldt)
    mag, ang = jnp.exp(lr * dt), li * dt
    ab_re, ab_im = mag * jnp.cos(ang), mag * jnp.sin(ang)
    nr, ni = ab_re - 1.0, ab_im
    den = lr * lr + li * li
    return ab_re, ab_im, (nr * lr + ni * li) / den, (ni * lr - nr * li) / den


def s5_params_fwd(lr, li, ldt):
    def body(lr_ref, li_ref, ldt_ref, *outs):
        for o_ref, o in zip(outs, _s5_param_fn(lr_ref[...], li_ref[...], ldt_ref[...])):
            o_ref[...] = o

    return pl.pallas_call(body, out_shape=[jax.ShapeDtypeStruct(lr.shape, f32)] * 4, name="s5_params_fwd")(lr, li, ldt)


def s5_params_bwd(lr, li, ldt, cts):
    def body(lr_ref, li_ref, ldt_ref, c0, c1, c2, c3, dlr, dli, dldt):
        _, vjp = jax.vjp(_s5_param_fn, lr_ref[...], li_ref[...], ldt_ref[...])
        a, b, c = vjp((c0[...], c1[...], c2[...], c3[...]))
        dlr[...] = a
        dli[...] = b
        dldt[...] = c

    return pl.pallas_call(
        body, out_shape=[jax.ShapeDtypeStruct(lr.shape, f32), jax.ShapeDtypeStruct(li.shape, f32),
                         jax.ShapeDtypeStruct(ldt.shape, f32)], name="s5_params_bwd")(lr, li, ldt, *cts)


def _cmul(ar, ai, br, bi):
    return ar * br - ai * bi, ar * bi + ai * br


def s5_scan(z, ab_re, ab_im, f_re, f_im, *, reverse, name):
    L = z.shape[0]
    tm = _pick(L, 256, SUBLANE)
    nb = L // tm
    ng = tm // SUBLANE
    use_f = f_re is not None
    consts = [ab_re, ab_im] + ([f_re, f_im] if use_f else [])

    def body(*refs):
        z_ref = refs[0]
        c_refs = refs[1:1 + len(consts)]
        x_ref, car = refs[1 + len(consts)], refs[2 + len(consts)]
        i = pl.program_id(0)

        @pl.when(i == 0)
        def _():
            car[...] = jnp.zeros_like(car)

        a1 = (c_refs[0][...], c_refs[1][...])
        a2 = _cmul(*a1, *a1)
        a3 = _cmul(*a2, *a1)
        a4 = _cmul(*a2, *a2)
        pw = [a1, a2, a3, a4, _cmul(*a4, *a1), _cmul(*a4, *a2), _cmul(*a4, *a3), _cmul(*a4, *a4)]
        if reverse:
            pw = pw[::-1]
        pw_re = _stack_rows([p[0] for p in pw])
        pw_im = _stack_rows([p[1] for p in pw])
        ridx = lax.broadcasted_iota(jnp.int32, (SUBLANE, NS), 0)
        if use_f:
            fr, fi = c_refs[2][...], c_refs[3][...]

        def group(s, carry):
            cr, ci = carry
            g = (ng - 1 - s) if reverse else s
            r0 = pl.multiple_of(g * SUBLANE, SUBLANE)
            xr = z_ref[pl.ds(r0, SUBLANE), 0:NS]
            xi = z_ref[pl.ds(r0, SUBLANE), NS:2 * NS]
            if use_f:
                xr, xi = _cmul(fr, fi, xr, xi)
            for sft, (pr, pi) in ((1, a1), (2, a2), (4, a4)):
                if reverse:
                    keep = ridx < SUBLANE - sft
                    sr = jnp.where(keep, pltpu.roll(xr, SUBLANE - sft, 0), 0.0)
                    si = jnp.where(keep, pltpu.roll(xi, SUBLANE - sft, 0), 0.0)
                else:
                    keep = ridx >= sft
                    sr = jnp.where(keep, pltpu.roll(xr, sft, 0), 0.0)
                    si = jnp.where(keep, pltpu.roll(xi, sft, 0), 0.0)
                tr, ti = _cmul(pr, pi, sr, si)
                xr, xi = xr + tr, xi + ti
            tr, ti = _cmul(pw_re, pw_im, cr, ci)
            xr, xi = xr + tr, xi + ti
            x_ref[pl.ds(r0, SUBLANE), 0:NS] = xr
            x_ref[pl.ds(r0, SUBLANE), NS:2 * NS] = xi
            row = 0 if reverse else SUBLANE - 1
            return xr[row:row + 1, :], xi[row:row + 1, :]

        cr, ci = lax.fori_loop(0, ng, group, (car[0:1, 0:NS], car[0:1, NS:2 * NS]))
        car[0:1, 0:NS] = cr
        car[0:1, NS:2 * NS] = ci

    blk = (lambda i: (nb - 1 - i, 0)) if reverse else (lambda i: (i, 0))
    return pl.pallas_call(
        body, grid=(nb,),
        in_specs=[pl.BlockSpec((tm, 2 * NS), blk)] + [pl.BlockSpec((1, NS), lambda i: (0, 0))] * len(consts),
        out_specs=pl.BlockSpec((tm, 2 * NS), blk),
        out_shape=jax.ShapeDtypeStruct((L, 2 * NS), f32),
        scratch_shapes=[pltpu.VMEM((SUBLANE, 2 * NS), f32)],
        compiler_params=_cp("arbitrary"), name=name,
    )(z, *consts)


def _s5_post_fn(ypre, u, dvec, gw, gb):
    y = ypre + dvec * u
    g = jax.nn.gelu(y)
    z = _dot(g.astype(bf16), gw.astype(bf16)) + gb
    return g * jax.nn.sigmoid(z)


def s5_post_fwd(ypre, hin, dvec, gw, gb):
    L = ypre.shape[0]

    def fn(i, nb, yt, ut, dv, gwv, gbv):
        return _s5_post_fn(yt, ut, dv, gwv, gbv)

    return rowwise(fn, name="s5_post_fwd", L=L, tm=_pick(L, 512, SUBLANE),
                   rows=[(ypre, 0, S5_WIDTH, "cur"), (hin, 3072, S5_WIDTH, "cur")],
                   consts=[dvec, gw, gb], outs=[(S5_WIDTH, f32)])


def s5_post_bwd(ypre, hin, dvec, gw, gb, dycat):
    L = ypre.shape[0]

    def fn(i, nb, yt, ut, dyt, dv, gwv, gbv):
        _, vjp = jax.vjp(_s5_post_fn, yt, ut, dv, gwv, gbv)
        return vjp(dyt)

    return rowwise(fn, name="s5_post_bwd", L=L, tm=_pick(L, 512, SUBLANE),
                   rows=[(ypre, 0, S5_WIDTH, "cur"), (hin, 3072, S5_WIDTH, "cur"), (dycat, 0, S5_WIDTH, "cur")],
                   consts=[dvec, gw, gb], outs=[(S5_WIDTH, f32)] * 2,
                   sums=[(1, S5_WIDTH), (S5_WIDTH, S5_WIDTH), (1, S5_WIDTH)])


def s5_acc(G, X, bu, f_re, f_im):
    L = G.shape[0]

    def fn(i, nb, g, x, b, xp8, fr, fi):
        gr, gi = g[:, :NS], g[:, NS:]
        xp = _shift_down(x, xp8 * (i > 0).astype(f32), 1)
        xr, xi = xp[:, :NS], xp[:, NS:]
        br, bi = b[:, :NS], b[:, NS:]
        dbu = jnp.concatenate([fr * gr + fi * gi, fr * gi - fi * gr], axis=1)
        return (dbu, _colsum(xr * gr + xi * gi), _colsum(xr * gi - xi * gr),
                _colsum(br * gr + bi * gi), _colsum(br * gi - bi * gr))

    return rowwise(fn, name="s5_acc", L=L, tm=_pick(L, 256, SUBLANE),
                   rows=[(G, 0, 2 * NS, "cur"), (X, 0, 2 * NS, "cur"), (bu, 0, 2 * NS, "cur"), (X, 0, 2 * NS, "prev")],
                   consts=[f_re, f_im], outs=[(2 * NS, bf16)], sums=[(1, NS)] * 4)


def _s5_blockdiag(b_re, b_im, c_re, c_im):
    eye = jnp.eye(S5_GROUPS, dtype=f32)
    bb = lambda b: jnp.einsum("gpi,gh->gihp", b, eye).reshape(S5_WIDTH, NS)
    cc = lambda c: jnp.einsum("gip,gh->gphi", c, eye).reshape(NS, S5_WIDTH)
    return jnp.concatenate([bb(b_re), bb(b_im)], axis=1), jnp.concatenate([cc(c_re), -cc(c_im)], axis=0)


def _s5_blockdiag_grads(dB, dC):
    gb = lambda m: jnp.einsum("gigp->gpi", m.reshape(S5_GROUPS, S5_GROUP, S5_GROUPS, S5_STATE))
    gc = lambda m: jnp.einsum("gpgi->gip", m.reshape(S5_GROUPS, S5_STATE, S5_GROUPS, S5_GROUP))
    return gb(dB[:, :NS]), gb(dB[:, NS:]), gc(dC[:NS]), -gc(dC[NS:])


def s5_fwd(hin, prm):
    ab_re, ab_im, f_re, f_im = s5_params_fwd(prm["lr"], prm["li"], prm["ldt"])
    flat = lambda a: a.reshape(1, NS)
    ab_re, ab_im, f_re, f_im = flat(ab_re), flat(ab_im), flat(f_re), flat(f_im)
    Bblk, Cblk = _s5_blockdiag(prm["b_re"], prm["b_im"], prm["c_re"], prm["c_im"])
    bu = mm(hin, Bblk, name="s5_bu", a_win=(3072, S5_WIDTH))
    X = s5_scan(bu, ab_re, ab_im, f_re, f_im, reverse=False, name="s5_scan_fwd")
    ypre = mm(X, Cblk, name="s5_y")
    yc = s5_post_fwd(ypre, hin, prm["d"], prm["gw"], prm["gb"])
    return yc, (ab_re, ab_im, f_re, f_im, Bblk, Cblk, bu, X, ypre)


def s5_bwd(hin, prm, saved, dycat):
    ab_re, ab_im, f_re, f_im, Bblk, Cblk, bu, X, ypre = saved
    dypre, du_skip, dd, dgw, dgb = s5_post_bwd(ypre, hin, prm["d"], prm["gw"], prm["gb"], dycat)
    dX = mm(dypre, Cblk, tb=True, name="s5_dx")
    dC = mm(X, dypre, ta=True, name="s5_dc")
    G = s5_scan(dX, ab_re, -ab_im, None, None, reverse=True, name="s5_scan_bwd")
    dbu, dar, dai, dfr, dfi = s5_acc(G, X, bu, f_re, f_im)
    dB = mm(hin, dbu, ta=True, a_win=(3072, S5_WIDTH), name="s5_db")
    du_b = mm(dbu, Bblk, tb=True, name="s5_du")
    sh = prm["lr"].shape
    dlr, dli, dldt = s5_params_bwd(prm["lr"], prm["li"], prm["ldt"],
                                   [a.reshape(sh) for a in (dar, dai, dfr, dfi)])
    db_re, db_im, dc_re, dc_im = _s5_blockdiag_grads(dB, dC)
    grads = dict(lr=dlr, li=dli, ldt=dldt, b_re=db_re, b_im=db_im, c_re=dc_re, c_im=dc_im, d=dd, gw=dgw, gb=dgb)
    return du_skip, du_b, grads


DN_W = DN_HEADS * DN_DK
QKV_W = 3 * DN_W


def _softplus(x):
    return jnp.maximum(x, 0.0) + jnp.log(1.0 + jnp.exp(-jnp.abs(x)))


def _dn_pre(c, ab, alog, dtb):
    s = _silu(c)
    parts = []
    for h in range(2 * DN_HEADS):
        sh = s[:, h * 128:(h + 1) * 128]
        scale = DN_DK ** -0.5 if h < DN_HEADS else 1.0
        parts.append(sh * (lax.rsqrt(jnp.sum(sh * sh, axis=-1, keepdims=True) + EPS) * scale))
    parts.append(s[:, 2 * DN_W:])
    g = -jnp.exp(alog) * _softplus(ab[:, :128] + dtb)
    beta = jax.nn.sigmoid(ab[:, 128:])
    return jnp.concatenate(parts, axis=1), jnp.concatenate([g, beta], axis=1)


def _dn_pre_bwd(c, ab, alog, dtb, dqkv, dgb):
    sg = jax.nn.sigmoid(c)
    s = c * sg
    parts = []
    for h in range(2 * DN_HEADS):
        sh = s[:, h * 128:(h + 1) * 128]
        dy = dqkv[:, h * 128:(h + 1) * 128]
        scale = DN_DK ** -0.5 if h < DN_HEADS else 1.0
        r = lax.rsqrt(jnp.sum(sh * sh, axis=-1, keepdims=True) + EPS)
        parts.append(scale * r * (dy - sh * (r * r) * jnp.sum(dy * sh, axis=-1, keepdims=True)))
    parts.append(dqkv[:, 2 * DN_W:])
    dc = jnp.concatenate(parts, axis=1) * (sg * (1.0 + c * (1.0 - sg)))
    pre = ab[:, :128] + dtb
    ea = jnp.exp(alog)
    dg = dgb[:, :128]
    da = dg * (-ea) * jax.nn.sigmoid(pre)
    dalog = _colsum(dg * (-ea) * _softplus(pre))
    beta = jax.nn.sigmoid(ab[:, 128:])
    db = dgb[:, 128:] * beta * (1.0 - beta)
    return dc, jnp.concatenate([da, db], axis=1), dalog, _colsum(da)


def dn_pre_fwd(hin, conv_w, alog, dtb):
    L = hin.shape[0]

    def fn(i, nb, x, ab, p8, w, al, db):
        c = _conv_causal(x, p8 * (i > 0).astype(f32), w)
        return _dn_pre(c, ab, al, db)

    return rowwise(fn, name="dn_pre_fwd", L=L, tm=_pick(L, 256, SUBLANE),
                   rows=[(hin, 0, QKV_W, "cur"), (hin, 3328, 256, "cur"), (hin, 0, QKV_W, "prev")],
                   consts=[conv_w, alog, dtb], outs=[(QKV_W, f32), (256, f32)])


def dn_pre_bwd(hin, conv_w, alog, dtb, dqkv3, dg, dbeta):
    L = hin.shape[0]

    def fn(i, nb, x, ab, dq, dk, dv, dgt, dbt, p8, w, al, db):
        c = _conv_causal(x, p8 * (i > 0).astype(f32), w)
        return _dn_pre_bwd(c, ab, al, db, jnp.concatenate([dq, dk, dv], axis=1), jnp.concatenate([dgt, dbt], axis=1))

    rows = [(hin, 0, QKV_W, "cur"), (hin, 3328, 256, "cur")] + [(a, 0, DN_W, "cur") for a in dqkv3]
    rows += [(dg, 0, 128, "cur"), (dbeta, 0, 128, "cur"), (hin, 0, QKV_W, "prev")]
    return rowwise(fn, name="dn_pre_bwd", L=L, tm=_pick(L, 128, SUBLANE), rows=rows,
                   consts=[conv_w, alog, dtb], outs=[(QKV_W, f32), (256, f32)], sums=[(1, 128), (1, 128)])


def _split(a):
    hi = a.astype(bf16)
    return hi, (a - hi.astype(f32)).astype(bf16)


def _dot3_raw(a, b, dims):
    ah, al = _split(a)
    bh, bl = _split(b)
    return _dot(ah, bh, dims) + (_dot(ah, bl, dims) + _dot(al, bh, dims))


@functools.partial(jax.custom_vjp, nondiff_argnums=(2,))
def _dot3(a, b, dims=NN):
    return _dot3_raw(a, b, dims)


def _dot3_fwd(a, b, dims):
    return _dot3_raw(a, b, dims), (a, b)


BNN = (((2,), (1,)), ((0,), (0,)))
BNT = (((2,), (2,)), ((0,), (0,)))
BTN = (((1,), (1,)), ((0,), (0,)))


def _dot_bwd(raw, dims, res, g):
    a, b = res
    nn, nt, tn = (BNN, BNT, BTN) if dims[1][0] else (NN, NT, TN)
    if dims == nn:
        return raw(g, b, nt), raw(a, g, tn)
    if dims == nt:
        return raw(g, b, nn), raw(g, a, tn)
    assert dims == tn
    return raw(b, g, nt), raw(a, g, nn)


_dot3.defvjp(_dot3_fwd, functools.partial(_dot_bwd, _dot3_raw))


def _dot1_raw(a, b, dims):
    return _dot(a.astype(bf16), b.astype(bf16), dims)


@functools.partial(jax.custom_vjp, nondiff_argnums=(2,))
def _dot1(a, b, dims=NN):
    return _dot1_raw(a, b, dims)


_dot1.defvjp(lambda a, b, dims: (_dot1_raw(a, b, dims), (a, b)), functools.partial(_dot_bwd, _dot1_raw))


def _dn_chunk(q, k, v, gcol, bcol, S):
    C = q.shape[1]
    r = lax.broadcasted_iota(jnp.int32, (C, C), 0)
    c = lax.broadcasted_iota(jnp.int32, (C, C), 1)
    tril = (r >= c).astype(f32)
    strict = (r > c).astype(f32)
    eye = (r == c).astype(f32)
    hd = _dot3
    grow = jnp.sum(eye * gcol, axis=1, keepdims=True)
    Gcol = jnp.sum(tril * grow, axis=2, keepdims=True)
    Grow = jnp.sum(eye * Gcol, axis=1, keepdims=True)
    gamma = jnp.exp((Gcol - Grow) * tril) * tril
    ld = _dot1
    nmat = strict * bcol * ld(k, k, BNT) * gamma
    T = eye - nmat
    Pw = hd(nmat, nmat, BNN)
    for step in range(5):
        T = T + hd(T, Pw, BNN)
        if step < 4:
            Pw = hd(Pw, Pw, BNN)
    eG = jnp.exp(Gcol)
    u = hd(T, bcol * v, BNN)
    w = hd(T, (bcol * eG) * k, BNN)
    qk = ld(q, k, BNT) * gamma
    vnew = u - ld(w, S, BNN)
    o = ld(q * eG, S, BNN) + ld(qk, vnew, BNN)
    Glast = jnp.sum(gcol, axis=1, keepdims=True)
    S2 = S * jnp.exp(Glast) + ld(k * jnp.exp(Glast - Gcol), vnew, BTN)
    return o, S2


def _heads(x_ref):
    return jnp.stack([x_ref[:, h * 128:(h + 1) * 128] for h in range(DN_HEADS)])


def _head_cols(g_ref):
    return jnp.stack([g_ref[:, h:h + 1] for h in range(DN_HEADS)])


def dn_chunks_fwd(qkvn, gb):
    L = qkvn.shape[0]
    C = DN_CHUNK
    nc = L // C

    def body(q_ref, k_ref, v_ref, g_ref, b_ref, o_ref, sin_ref, S):
        n = pl.program_id(0)

        @pl.when(n == 0)
        def _():
            S[...] = jnp.zeros_like(S)

        s_in = S[...]
        sin_ref[...] = s_in
        o, s2 = _dn_chunk(_heads(q_ref), _heads(k_ref), _heads(v_ref), _head_cols(g_ref), _head_cols(b_ref), s_in)
        for h in range(DN_HEADS):
            o_ref[:, h * 128:(h + 1) * 128] = o[h]
        S[...] = s2

    blk = lambda j: pl.BlockSpec((C, DN_W), lambda n, j=j: (n, j))
    gblk = lambda j: pl.BlockSpec((C, 128), lambda n, j=j: (n, j))
    return pl.pallas_call(
        body, grid=(nc,),
        in_specs=[blk(0), blk(1), blk(2), gblk(0), gblk(1)],
        out_specs=[pl.BlockSpec((C, DN_W), lambda n: (n, 0)),
                   pl.BlockSpec((DN_HEADS, None, 128, 128), lambda n: (0, n, 0, 0))],
        out_shape=[jax.ShapeDtypeStruct((L, DN_W), f32), jax.ShapeDtypeStruct((DN_HEADS, nc, 128, 128), f32)],
        scratch_shapes=[pltpu.VMEM((DN_HEADS, 128, 128), f32)],
        compiler_params=_cp("arbitrary"), name="dn_chunks_fwd",
    )(qkvn, qkvn, qkvn, gb, gb)


def dn_chunks_bwd(qkvn, gb, s_in, do):
    L = qkvn.shape[0]
    C = DN_CHUNK
    nc = L // C

    def body(q_ref, k_ref, v_ref, g_ref, b_ref, sin_ref, do_ref, dq_ref, dk_ref, dv_ref, dg_ref, db_ref, dS):
        n = pl.program_id(0)

        @pl.when(n == 0)
        def _():
            dS[...] = jnp.zeros_like(dS)

        args = (_heads(q_ref), _heads(k_ref), _heads(v_ref), _head_cols(g_ref), _head_cols(b_ref), sin_ref[...])
        _, vjp = jax.vjp(_dn_chunk, *args)
        dq, dk, dv, dg, db, ds = vjp((_heads(do_ref), dS[...]))
        lane = lax.broadcasted_iota(jnp.int32, (C, 128), 1)
        dg_all = jnp.zeros((C, 128), f32)
        db_all = jnp.zeros((C, 128), f32)
        for h in range(DN_HEADS):
            sl = slice(h * 128, (h + 1) * 128)
            dq_ref[:, sl] = dq[h]
            dk_ref[:, sl] = dk[h]
            dv_ref[:, sl] = dv[h]
            dg_all = dg_all + jnp.where(lane == h, dg[h], 0.0)
            db_all = db_all + jnp.where(lane == h, db[h], 0.0)
        dS[...] = ds
        dg_ref[...] = dg_all
        db_ref[...] = db_all

    rv = lambda n: nc - 1 - n
    blk = lambda j: pl.BlockSpec((C, DN_W), lambda n, j=j: (rv(n), j))
    gblk = lambda j: pl.BlockSpec((C, 128), lambda n, j=j: (rv(n), j))
    oblk = pl.BlockSpec((C, DN_W), lambda n: (rv(n), 0))
    gout = pl.BlockSpec((C, 128), lambda n: (rv(n), 0))
    return pl.pallas_call(
        body, grid=(nc,),
        in_specs=[blk(0), blk(1), blk(2), gblk(0), gblk(1),
                  pl.BlockSpec((DN_HEADS, None, 128, 128), lambda n: (0, rv(n), 0, 0)), oblk],
        out_specs=[oblk] * 3 + [gout] * 2,
        out_shape=[jax.ShapeDtypeStruct((L, DN_W), f32)] * 3 + [jax.ShapeDtypeStruct((L, 128), f32)] * 2,
        scratch_shapes=[pltpu.VMEM((DN_HEADS, 128, 128), f32)],
        compiler_params=_cp("arbitrary"), name="dn_chunks_bwd",
    )(qkvn, qkvn, qkvn, gb, gb, s_in, do)


def _dn_post(o, z, w):
    parts = []
    for h in range(DN_HEADS):
        oh = o[:, h * 128:(h + 1) * 128]
        r = lax.rsqrt(jnp.mean(oh * oh, axis=-1, keepdims=True) + EPS)
        parts.append(oh * r * w)
    return jnp.concatenate(parts, axis=1) * _silu(z)


def dn_post_fwd(o, hin, yc, onorm):
    L = o.shape[0]

    def fn(i, nb, ot, zt, yct, w):
        return jnp.concatenate([yct, _dn_post(ot, zt, w)], axis=1)

    return rowwise(fn, name="dn_post_fwd", L=L, tm=_pick(L, 256, SUBLANE),
                   rows=[(o, 0, DN_W, "cur"), (hin, 2304, DN_W, "cur"), (yc, 0, S5_WIDTH, "cur")],
                   consts=[onorm], outs=[(1024, bf16)])


def dn_post_bwd(o, hin, onorm, dycat):
    L = o.shape[0]

    def fn(i, nb, ot, zt, d0, d1, d2, w):
        dy = jnp.concatenate([d0, d1, d2], axis=1)
        sg = jax.nn.sigmoid(zt)
        sz = zt * sg
        dos, dw = [], jnp.zeros((1, 128), f32)
        nrm = []
        for h in range(DN_HEADS):
            sl = slice(h * 128, (h + 1) * 128)
            oh = ot[:, sl]
            r = lax.rsqrt(jnp.mean(oh * oh, axis=-1, keepdims=True) + EPS)
            ohat = oh * r
            t = dy[:, sl] * sz[:, sl]
            dw = dw + _colsum(t * ohat)
            t = t * w
            dos.append(r * (t - ohat * jnp.mean(t * ohat, axis=-1, keepdims=True)))
            nrm.append(ohat * w)
        dz = dy * jnp.concatenate(nrm, axis=1) * (sg * (1.0 + zt * (1.0 - sg)))
        return jnp.concatenate(dos, axis=1), dz, dw

    rows = [(o, 0, DN_W, "cur"), (hin, 2304, DN_W, "cur")] + [(dycat, 256 * (1 + j), 256, "cur") for j in range(3)]
    return rowwise(fn, name="dn_post_bwd", L=L, tm=_pick(L, 256, SUBLANE), rows=rows,
                   consts=[onorm], outs=[(DN_W, f32), (DN_W, f32)], sums=[(1, 128)])


def conv_bwd_win(xarr, start, C, w, dc, name):
    L = xarr.shape[0]
    W = w.shape[0]

    def fn(i, nb, xt, dct, p8, n8, wv):
        dx = _conv_causal_bwd_x(dct, n8 * (i < nb - 1).astype(f32), wv)
        dw = _conv_causal_bwd_w(dct, xt, p8 * (i > 0).astype(f32), W)
        return dx, dw

    return rowwise(fn, name=name, L=L, tm=_pick(L, 128, SUBLANE),
                   rows=[(xarr, start, C, "cur"), (dc, 0, C, "cur"), (xarr, start, C, "prev"), (dc, 0, C, "next")],
                   consts=[w], outs=[(C, bf16)], sums=[(SUBLANE, C)])


def rec_assemble(dx_qkv, dz, du1, du2, dab):
    L = dz.shape[0]

    def fn(i, nb, a, b, c, d, e):
        return jnp.concatenate([a.astype(f32), b, c + d, e], axis=1)

    return rowwise(fn, name="rec_assemble", L=L, tm=_pick(L, 256, SUBLANE),
                   rows=[(dx_qkv, 0, QKV_W, "cur"), (dz, 0, DN_W, "cur"), (du1, 0, 256, "cur"),
                         (du2, 0, 256, "cur"), (dab, 0, 256, "cur")], outs=[(REC_PAD, bf16)])


def deltanet_fwd(hin, prm, yc):
    qkvn, gb = dn_pre_fwd(hin, prm["conv"], prm["alog"], prm["dtb"])
    o, s_in = dn_chunks_fwd(qkvn, gb)
    ycat = dn_post_fwd(o, hin, yc, prm["onorm"])
    return ycat, (qkvn, gb, o, s_in)


def deltanet_bwd(hin, prm, saved, dycat):
    qkvn, gb, o, s_in = saved
    do, dz, donorm = dn_post_bwd(o, hin, prm["onorm"], dycat)
    dq, dk, dv, dgH, dbH = dn_chunks_bwd(qkvn, gb, s_in, do)
    dc, dab, dalog, ddtb = dn_pre_bwd(hin, prm["conv"], prm["alog"], prm["dtb"], (dq, dk, dv), dgH, dbH)
    dx_qkv, dconv = conv_bwd_win(hin, 0, QKV_W, prm["conv"], dc, "dn_conv_bwd")
    return dx_qkv, dz, dab, dict(conv=dconv[:DN_CONV], alog=dalog, dtb=ddtb, onorm=donorm)


AXES = ("x", "y", "c")


class _Coll:
    def __init__(self, x, axes, mode):
        self.axes, self.mode = axes, mode
        self.P = 2 ** len(axes)
        shape = x.shape if mode == "gather" else x.shape[1:]
        self.out_shape = jax.ShapeDtypeStruct((self.P,) + tuple(shape), x.dtype)
        self.scratch = [pltpu.SemaphoreType.DMA((self.P - 1,)), pltpu.SemaphoreType.DMA((self.P - 1,)),
                        pltpu.SemaphoreType.DMA]

    def _copies(self, x_ref, out_ref, send_sems, recv_sems, local_sem, with_recvs):
        axes, k = self.axes, len(self.axes)
        co = {a: lax.axis_index(a) for a in AXES}
        me = 0
        for a in axes:
            me = me * 2 + co[a]
        src = (lambda j: x_ref) if self.mode == "gather" else (lambda j: x_ref.at[j])
        local = pltpu.make_async_copy(src(me), out_ref.at[me], local_sem)
        sends, recvs = [], []
        for m in range(1, self.P):
            tco = dict(co)
            t = 0
            for i, a in enumerate(axes):
                if (m >> (k - 1 - i)) & 1:
                    tco[a] = 1 - co[a]
                t = t * 2 + tco[a]
            dev = tuple(tco[a] for a in AXES)
            mk = functools.partial(pltpu.make_async_remote_copy, src_ref=src(t), send_sem=send_sems.at[m - 1],
                                   recv_sem=recv_sems.at[m - 1], device_id=dev, device_id_type=MESH)
            sends.append(mk(dst_ref=out_ref.at[me]))
            if with_recvs:
                recvs.append(mk(dst_ref=out_ref.at[t]))
        return local, sends, recvs

    def start(self, *refs):
        local, sends, _ = self._copies(*refs, with_recvs=False)
        local.start()
        for cp in sends:
            cp.start()

    def wait(self, *refs):
        local, sends, recvs = self._copies(*refs, with_recvs=True)
        for cp in recvs:
            cp.wait_recv()
        for cp in sends:
            cp.wait_send()
        local.wait()


def _collective(x, axes, mode, name):
    coll = _Coll(x, axes, mode)

    def body(*refs):
        coll.start(*refs)
        coll.wait(*refs)

    return pl.pallas_call(
        body, in_specs=[pl.BlockSpec(memory_space=pl.ANY)], out_specs=pl.BlockSpec(memory_space=pl.ANY),
        out_shape=coll.out_shape, scratch_shapes=coll.scratch, name=name,
    )(x)


def all_gather(x, axes, name):
    return _collective(x, axes, "gather", name)


def exchange(x, axes, name):
    return _collective(x, axes, "exchange", name)


def sum_slots(x, name, out_dtype=f32):
    P, R, C = x.shape
    tr = _pick(R, 256, 2 * SUBLANE)

    def body(x_ref, o_ref):
        acc = x_ref[0].astype(f32)
        for j in range(1, P):
            acc = acc + x_ref[j].astype(f32)
        o_ref[...] = acc.astype(o_ref.dtype)

    return pl.pallas_call(
        body, grid=(R // tr,), in_specs=[pl.BlockSpec((P, tr, C), lambda i: (0, i, 0))],
        out_specs=pl.BlockSpec((tr, C), lambda i: (i, 0)), out_shape=jax.ShapeDtypeStruct((R, C), out_dtype),
        compiler_params=_cp("parallel"), name=name,
    )(x)


def _pack(arrs, width, row_mult, dtype):
    flat = jnp.concatenate([a.astype(dtype).reshape(-1) for a in arrs])
    unit = width * row_mult
    n = -(-flat.shape[0] // unit) * unit
    return jnp.pad(flat, (0, n - flat.shape[0])).reshape(n // width, width)


def _unpack(flat, shapes):
    flat = flat.reshape(-1)
    out, off = [], 0
    for s in shapes:
        n = int(np.prod(s))
        out.append(flat[off:off + n].reshape(s))
        off += n
    return out


def ada_fwd(c_all, ada_w):
    def body(c_ref, w_ref, o_ref):
        cond = _silu(c_ref[...])
        for l in range(ada_w.shape[0]):
            o_ref[l] = _dot(cond, w_ref[l], precision=HI)

    return pl.pallas_call(body, out_shape=jax.ShapeDtypeStruct((ada_w.shape[0], c_all.shape[0], ada_w.shape[2]), f32),
                          compiler_params=pltpu.CompilerParams(vmem_limit_bytes=VMEM_LIMIT), name="ada_fwd")(c_all, ada_w)


def ada_bwd(c_all, dmod):
    def body(c_ref, d_ref, o_ref):
        cond = _silu(c_ref[...])
        for l in range(dmod.shape[0]):
            o_ref[l] = _dot(cond, d_ref[l], TN, precision=HI)

    return pl.pallas_call(body, out_shape=jax.ShapeDtypeStruct((dmod.shape[0], c_all.shape[1], dmod.shape[2]), f32),
                          compiler_params=pltpu.CompilerParams(vmem_limit_bytes=VMEM_LIMIT), name="ada_bwd")(c_all, dmod)


def loss_fwd_bwd(y, target):
    L, D = y.shape

    def fn(i, nb, yt, tt):
        e = yt - tt
        return e * (1.0 / D), jnp.sum(jnp.sum(e * e, axis=1, keepdims=True), axis=0, keepdims=True)

    return rowwise(fn, name="loss", L=L, tm=_pick(L, 512, SUBLANE), rows=[(y, 0, D, "cur"), (target, 0, D, "cur")],
                   outs=[(D, f32)], sums=[(1, 1)])


def adamw(w, g, m, v, name):
    R, C = w.shape

    def fn(i, nb, wt, gt, mt, vt):
        m2 = ADAM_B1 * mt + (1.0 - ADAM_B1) * gt
        v2 = ADAM_B2 * vt + (1.0 - ADAM_B2) * (gt * gt)
        m_hat = m2 / (1.0 - ADAM_B1 ** ADAM_STEP)
        v_hat = v2 / (1.0 - ADAM_B2 ** ADAM_STEP)
        delta = -ADAM_LR * (m_hat / (jnp.sqrt(v_hat) + ADAM_EPS) + ADAM_WD * wt)
        return delta, m2, v2

    return rowwise(fn, name=name, L=R, tm=_pick(R, 256, SUBLANE), rows=[(a, 0, C, "cur") for a in (w, g, m, v)],
                   outs=[(C, f32)] * 3)


W_NAMES = ["ada_w", "ada_b", "norm_mix", "norm_ffn", "attn_w_in", "attn_q_norm_a", "attn_k_norm_a", "attn_q_norm_b",
           "attn_k_norm_b", "attn_sinks", "attn_w_out", "rec_w_in", "s5_lambda_re", "s5_lambda_im", "s5_log_dt",
           "s5_b_re", "s5_b_im", "s5_c_re", "s5_c_im", "s5_d", "s5_glu_w", "s5_glu_b", "dn_conv", "dn_a_log",
           "dn_dt_bias", "dn_out_norm", "rec_w_out", "ffn_w_up", "ffn_conv", "ffn_w_down"]
BIG = ["attn_w_in", "attn_w_out", "rec_w_in", "rec_w_out", "ffn_w_up", "ffn_w_down"]
SMALL_SHARDED = ["s5_d", "s5_glu_w", "s5_glu_b", "dn_conv", "ffn_conv"]
SMALL_REPL = [n for n in W_NAMES if n not in BIG and n not in SMALL_SHARDED and n != "ada_w"]
NSH = 4
GRAD_WIRE = (bf16,)


SHARD_AXIS = {"attn_w_in": 2, "attn_w_out": 1, "rec_w_in": 2, "rec_w_out": 1, "ffn_w_up": 2, "ffn_w_down": 1,
              "s5_d": 1, "s5_glu_w": 1, "s5_glu_b": 1, "dn_conv": 2, "ffn_conv": 2}


def _unshard(g, name):
    ax = SHARD_AXIS[name.rstrip("01")]
    g = jnp.moveaxis(g, 0, ax)
    s = g.shape
    return g.reshape(s[:ax] + (s[ax] * s[ax + 1],) + s[ax + 2:])


def _to_shards(full, name):
    ax = SHARD_AXIS[name.rstrip("01")]
    s = full.shape
    g = full.reshape(s[:ax] + (NSH, s[ax] // NSH) + s[ax + 1:])
    return jnp.moveaxis(g, ax, 0)


def _rec_pad_cols(w):
    z6 = jnp.zeros(w.shape[:-1] + (122,), w.dtype)
    return jnp.concatenate([w[..., 256:3328], w[..., 0:256], w[..., 3328:3334], z6, w[..., 3334:3340], z6], axis=-1)


def _rec_unpad_cols(g):
    return jnp.concatenate([g[..., 3072:3328], g[..., 0:3072], g[..., 3328:3334], g[..., 3456:3462]], axis=-1)


def _ffn_fwd(x1, nf, sc, sh, gate, w_up, conv, w_dn, tag, rides=()):
    rides = list(rides) + [None, None]
    h2 = modulate_fwd(x1, nf, sc, sh, f"{tag}_mod2_fwd")
    up = mm(h2, w_up, name=f"{tag}_ffn_up", out_dtypes=(bf16,), ride=rides[0])
    up, got0 = up if rides[0] else (up, None)
    act = ffn_act_fwd(up, conv, f"{tag}_ffn_act_fwd")
    res = mm(act, w_dn, name=f"{tag}_ffn_down", out_dtypes=(f32, f32), epi=_resid_epi, epi_mn=[x1], epi_n=[gate],
             ride=rides[1])
    return res[1], (h2, up, act, res[0]), (got0, res[2] if rides[1] else None)


def _ffn_bwd(dx, x1, nf, sc, sh, gate, w_up, conv, w_dn, saved, tag, rides=()):
    rides = list(rides) + [None, None, None]
    take = lambda res, r: res if r else (res, None)
    h2, up, act, f = saved
    df, dgate = resid_bwd(dx, f, gate, f"{tag}_res2_bwd")
    dact = mm(df, w_dn, tb=True, name=f"{tag}_ffn_dact", out_dtypes=(bf16,))
    dw_dn, got0 = take(mm(act, df, ta=True, name=f"{tag}_ffn_dwdown", out_dtypes=GRAD_WIRE, ride=rides[0]), rides[0])
    dup, dconv = ffn_act_conv_bwd(up, conv, dact, f"{tag}_ffn_act_conv_bwd")
    dw_up, got1 = take(mm(h2, dup, ta=True, name=f"{tag}_ffn_dwup", out_dtypes=GRAD_WIRE, ride=rides[1]), rides[1])
    dh2, got2 = take(mm(dup, w_up, tb=True, name=f"{tag}_ffn_dh", ride=rides[2]), rides[2])
    dx, dnf, dsc, dsh = modulate_bwd(x1, nf, sc, sh, dh2, dx, f"{tag}_mod2_bwd")
    grads = dict(nf=dnf, sc=dsc, sh=dsh, gate=dgate, w_up=dw_up, conv=dconv[:FFN_CONV], w_dn=dw_dn)
    return dx, grads, (got0, got1, got2)


def kernel(x, c, ada_w, ada_b, norm_mix, norm_ffn, attn_w_in, attn_q_norm_a, attn_k_norm_a, attn_q_norm_b, attn_k_norm_b, attn_sinks, attn_w_out, rec_w_in, s5_lambda_re, s5_lambda_im, s5_log_dt, s5_b_re, s5_b_im, s5_c_re, s5_c_im, s5_d, s5_glu_w, s5_glu_b, dn_conv, dn_a_log, dn_dt_bias, dn_out_norm, rec_w_out, ffn_w_up, ffn_conv, ffn_w_down, loss_target, m_ada_w, m_ada_b, m_norm_mix, m_norm_ffn, m_attn_w_in, m_attn_q_norm_a, m_attn_k_norm_a, m_attn_q_norm_b, m_attn_k_norm_b, m_attn_sinks, m_attn_w_out, m_rec_w_in, m_s5_lambda_re, m_s5_lambda_im, m_s5_log_dt, m_s5_b_re, m_s5_b_im, m_s5_c_re, m_s5_c_im, m_s5_d, m_s5_glu_w, m_s5_glu_b, m_dn_conv, m_dn_a_log, m_dn_dt_bias, m_dn_out_norm, m_rec_w_out, m_ffn_w_up, m_ffn_conv, m_ffn_w_down, v_ada_w, v_ada_b, v_norm_mix, v_norm_ffn, v_attn_w_in, v_attn_q_norm_a, v_attn_k_norm_a, v_attn_q_norm_b, v_attn_k_norm_b, v_attn_sinks, v_attn_w_out, v_rec_w_in, v_s5_lambda_re, v_s5_lambda_im, v_s5_log_dt, v_s5_b_re, v_s5_b_im, v_s5_c_re, v_s5_c_im, v_s5_d, v_s5_glu_w, v_s5_glu_b, v_dn_conv, v_dn_a_log, v_dn_dt_bias, v_dn_out_norm, v_rec_w_out, v_ffn_w_up, v_ffn_conv, v_ffn_w_down):
    args = (ada_w, ada_b, norm_mix, norm_ffn, attn_w_in, attn_q_norm_a, attn_k_norm_a, attn_q_norm_b, attn_k_norm_b, attn_sinks, attn_w_out, rec_w_in, s5_lambda_re, s5_lambda_im, s5_log_dt, s5_b_re, s5_b_im, s5_c_re, s5_c_im, s5_d, s5_glu_w, s5_glu_b, dn_conv, dn_a_log, dn_dt_bias, dn_out_norm, rec_w_out, ffn_w_up, ffn_conv, ffn_w_down)
    ms = (m_ada_w, m_ada_b, m_norm_mix, m_norm_ffn, m_attn_w_in, m_attn_q_norm_a, m_attn_k_norm_a, m_attn_q_norm_b, m_attn_k_norm_b, m_attn_sinks, m_attn_w_out, m_rec_w_in, m_s5_lambda_re, m_s5_lambda_im, m_s5_log_dt, m_s5_b_re, m_s5_b_im, m_s5_c_re, m_s5_c_im, m_s5_d, m_s5_glu_w, m_s5_glu_b, m_dn_conv, m_dn_a_log, m_dn_dt_bias, m_dn_out_norm, m_rec_w_out, m_ffn_w_up, m_ffn_conv, m_ffn_w_down)
    vs = (v_ada_w, v_ada_b, v_norm_mix, v_norm_ffn, v_attn_w_in, v_attn_q_norm_a, v_attn_k_norm_a, v_attn_q_norm_b, v_attn_k_norm_b, v_attn_sinks, v_attn_w_out, v_rec_w_in, v_s5_lambda_re, v_s5_lambda_im, v_s5_log_dt, v_s5_b_re, v_s5_b_im, v_s5_c_re, v_s5_c_im, v_s5_d, v_s5_glu_w, v_s5_glu_b, v_dn_conv, v_dn_a_log, v_dn_dt_bias, v_dn_out_norm, v_rec_w_out, v_ffn_w_up, v_ffn_conv, v_ffn_w_down)
    W = dict(zip(W_NAMES, args))
    Mo = dict(zip(W_NAMES, ms))
    Vo = dict(zip(W_NAMES, vs))
    xi, yi, ci = lax.axis_index("x"), lax.axis_index("y"), lax.axis_index("c")
    shard = 2 * xi + yi
    me8 = 4 * xi + 2 * yi + ci
    xs = x[0]
    target = loss_target[0]
    L, D = xs.shape

    XY = ("x", "y")
    wparts = [
        [("attn_w_in", attn_w_in), ("attn_w_out", attn_w_out)],
        [("rec_w_in", rec_w_in), ("rec_w_out", rec_w_out)],
        [("ffn_w_up1", ffn_w_up[1:2]), ("ffn_w_down1", ffn_w_down[1:2])],
        [("ffn_w_up0", ffn_w_up[0:1]), ("ffn_w_down0", ffn_w_down[0:1])],
    ]
    wpack = [_pack([a for _, a in p], 1024, 16, bf16) for p in wparts]
    Wf = {}

    def unpack_weights(gathered, part):
        flat = gathered.reshape(NSH, -1)
        off = 0
        for n, a in part:
            sz = int(np.prod(a.shape))
            Wf[n] = _unshard(flat[:, off:off + sz].reshape((NSH,) + a.shape), n)[0]
            off += sz

    unpack_weights(all_gather(wpack[0], XY, "gather_w0"), wparts[0])

    sflat = _pack([c] + [W[n] for n in SMALL_SHARDED], 1024, 8, f32)
    s8 = all_gather(sflat, AXES, "gather_small")
    s8f = s8.reshape(8, -1)
    c_all = s8f[:, :D]
    Ws = {}
    off = D
    for n in SMALL_SHARDED:
        sz = int(np.prod(W[n].shape))
        Ws[n] = _unshard(s8f[0::2, off:off + sz].reshape((NSH,) + W[n].shape), n)
        off += sz

    modp = ada_fwd(c_all, ada_w)
    modg = all_gather(modp, ("x", "y"), "gather_mod")
    mod_all = jnp.moveaxis(modg, 0, 2).reshape(2, 8, -1) + ada_b[:, None, :]
    mod = lax.dynamic_slice(mod_all, (0, me8, 0), (2, 1, mod_all.shape[2]))[:, 0, :]
    mods = [[mod[l:l + 1, j * D:(j + 1) * D] for j in range(6)] for l in range(2)]

    sh1, sc1, g1, sh2, sc2, g2_ = mods[0]
    nm0, nf0 = norm_mix[0:1], norm_ffn[0:1]
    sinkb = jnp.repeat(attn_sinks[0], HEAD_DIM)[None]
    h0 = modulate_fwd(xs, nm0, sc1, sh1, "l0_mod1_fwd")
    hin0 = mm(h0, Wf["attn_w_in"], name="l0_in_proj")
    ocat, att_saved, got = attention_fwd(hin0, attn_q_norm_a, attn_k_norm_a, attn_q_norm_b, attn_k_norm_b, sinkb,
                                         ride=(wpack[3], XY, "gather"))
    unpack_weights(got, wparts[3])
    y0, x1 = mm(ocat, Wf["attn_w_out"], name="l0_out_proj", out_dtypes=(f32, f32), epi=_resid_epi,
                epi_mn=[xs], epi_n=[g1])
    x2, ffn0_saved, got = _ffn_fwd(x1, nf0, sc2, sh2, g2_, Wf["ffn_w_up0"], Ws["ffn_conv"][0], Wf["ffn_w_down0"], "l0",
                                   rides=[(wpack[1], XY, "gather"), (wpack[2], XY, "gather")])
    unpack_weights(got[0], wparts[1])
    unpack_weights(got[1], wparts[2])
    rec_w_in_p = _rec_pad_cols(Wf["rec_w_in"])

    th1, tc1, t1, th2, tc2, t2 = mods[1]
    nm1, nf1 = norm_mix[1:2], norm_ffn[1:2]
    pad128 = lambda a: jnp.pad(a, ((0, 0), (0, 128 - a.shape[1])))
    s5p = dict(lr=s5_lambda_re[0], li=s5_lambda_im[0], ldt=s5_log_dt[0][:, None], b_re=s5_b_re[0], b_im=s5_b_im[0],
               c_re=s5_c_re[0], c_im=s5_c_im[0], d=Ws["s5_d"], gw=Ws["s5_glu_w"][0], gb=Ws["s5_glu_b"])
    dnp = dict(conv=Ws["dn_conv"][0], alog=pad128(dn_a_log), dtb=pad128(dn_dt_bias), onorm=dn_out_norm)
    h1 = modulate_fwd(x2, nm1, tc1, th1, "l1_mod1_fwd")
    hin1 = mm(h1, rec_w_in_p, name="l1_in_proj")
    yc, s5_saved = s5_fwd(hin1, s5p)
    ycat, dn_saved = deltanet_fwd(hin1, dnp, yc)
    y1, x3 = mm(ycat, Wf["rec_w_out"], name="l1_out_proj", out_dtypes=(f32, f32), epi=_resid_epi,
                epi_mn=[x2], epi_n=[t1])
    x4, ffn1_saved, _ = _ffn_fwd(x3, nf1, tc2, th2, t2, Wf["ffn_w_up1"], Ws["ffn_conv"][1], Wf["ffn_w_down1"], "l1")

    dx, sse = loss_fwd_bwd(x4, target)
    loss = lax.psum(0.5 * sse[0, 0] / D, AXES)

    dx, gf1, _ = _ffn_bwd(dx, x3, nf1, tc2, th2, t2, Wf["ffn_w_up1"], Ws["ffn_conv"][1], Wf["ffn_w_down1"], ffn1_saved, "l1")
    dy1, dt1 = resid_bwd(dx, y1, t1, "l1_res1_bwd")
    dycat = mm(dy1, Wf["rec_w_out"], tb=True, name="l1_dycat")
    dw_rec_out = mm(ycat, dy1, ta=True, name="l1_dwout", out_dtypes=GRAD_WIRE)
    du_skip, du_b, s5g = s5_bwd(hin1, s5p, s5_saved, dycat)
    dx_qkv, dz, dab, dng = deltanet_bwd(hin1, dnp, dn_saved, dycat)
    dhin1 = rec_assemble(dx_qkv, dz, du_skip, du_b, dab)
    dw_rec_in = _rec_unpad_cols(mm(h1, dhin1, ta=True, name="l1_dwin", out_dtypes=GRAD_WIRE))
    dh1 = mm(dhin1, rec_w_in_p, tb=True, name="l1_dh")
    dx, dnm1, dtc1, dth1 = modulate_bwd(x2, nm1, tc1, th1, dh1, dx, "l1_mod1_bwd")

    def grad_part(items):
        flat = jnp.concatenate([_to_shards(g, n).reshape(NSH, -1) for n, g in items], axis=1)
        unit = 256 * 1024
        npad = -(-flat.shape[1] // unit) * unit
        return jnp.pad(flat, ((0, 0), (0, npad - flat.shape[1]))).reshape(NSH, npad // 1024, 1024)

    gparts = [[("rec_w_in", dw_rec_in[None]), ("rec_w_out", dw_rec_out[None])],
              [("ffn_w_up1", gf1["w_up"][None])], [("ffn_w_down1", gf1["w_dn"][None])]]
    dx, gf0, gq = _ffn_bwd(dx, x1, nf0, sc2, sh2, g2_, Wf["ffn_w_up0"], Ws["ffn_conv"][0], Wf["ffn_w_down0"], ffn0_saved,
                           "l0", rides=[(grad_part(p), XY, "exchange") for p in gparts])
    dy0, dg1 = resid_bwd(dx, y0, g1, "l0_res1_bwd")
    dcat = mm(dy0, Wf["attn_w_out"], tb=True, name="l0_dcat")
    dw_attn_out = mm(ocat, dy0, ta=True, name="l0_dwout", out_dtypes=GRAD_WIRE)
    gparts.append([("ffn_w_up0", gf0["w_up"][None]), ("ffn_w_down0", gf0["w_dn"][None])])
    gsum1 = jnp.concatenate([sum_slots(q, f"sum_chips{i}", bf16) for i, q in enumerate(gq)], axis=0)
    dhin0, dwqa, dwka, dwqb, dwkb, dsinkb, gots = attention_bwd(
        hin0, attn_q_norm_a, attn_k_norm_a, attn_q_norm_b, attn_k_norm_b, sinkb, att_saved, dcat,
        rides={1: (gsum1, ("c",), "gather"), 16: (grad_part(gparts[3]), XY, "exchange")})
    dw_attn_in = mm(h0, dhin0, ta=True, name="l0_dwin", out_dtypes=GRAD_WIRE)
    dh0 = mm(dhin0, Wf["attn_w_in"], tb=True, name="l0_dh")
    grad_x, dnm0, dsc1, dsh1 = modulate_bwd(xs, nm0, sc1, sh1, dh0, dx, "l0_mod1_bwd")

    dmod = jnp.concatenate([
        jnp.concatenate([dsh1, dsc1, dg1, gf0["sh"], gf0["sc"], gf0["gate"]], axis=1),
        jnp.concatenate([dth1, dtc1, dt1, gf1["sh"], gf1["sc"], gf1["gate"]], axis=1)], axis=0)
    gl = {
        "ada_b": dmod,
        "norm_mix": jnp.concatenate([dnm0, dnm1], axis=0),
        "norm_ffn": jnp.concatenate([gf0["nf"], gf1["nf"]], axis=0),
        "attn_q_norm_a": dwqa, "attn_k_norm_a": dwka, "attn_q_norm_b": dwqb, "attn_k_norm_b": dwkb,
        "attn_sinks": dsinkb[:, ::HEAD_DIM],
        "s5_lambda_re": s5g["lr"][None], "s5_lambda_im": s5g["li"][None], "s5_log_dt": s5g["ldt"][:, 0][None],
        "s5_b_re": s5g["b_re"][None], "s5_b_im": s5g["b_im"][None], "s5_c_re": s5g["c_re"][None],
        "s5_c_im": s5g["c_im"][None],
        "dn_a_log": dng["alog"][:, :DN_HEADS], "dn_dt_bias": dng["dtb"][:, :DN_HEADS], "dn_out_norm": dng["onorm"],
        "s5_d": s5g["d"], "s5_glu_w": s5g["gw"][None], "s5_glu_b": s5g["gb"], "dn_conv": dng["conv"][None],
        "ffn_conv": jnp.stack([gf0["conv"], gf1["conv"]]),
    }

    small_names = SMALL_REPL + SMALL_SHARDED
    gs = _pack([gl[n] for n in small_names], 128, 256, f32)
    gs8 = all_gather(gs, AXES, "gather_small_grads")
    gsum = sum_slots(gs8, "sum_small_grads")
    full_shapes = [gl[n].shape for n in small_names]
    gfull = dict(zip(small_names, _unpack(gsum, full_shapes)))
    dmod_all = gs8.reshape(8, -1)[:, :2 * 6 * D].reshape(8, 2, 6 * D)
    ncol = ada_w.shape[2]
    dmod_sh = jnp.moveaxis(lax.dynamic_slice(dmod_all, (0, 0, shard * ncol), (8, 2, ncol)), 0, 1)
    grads = {"ada_w": ada_bwd(c_all, dmod_sh)}
    for n in SMALL_REPL:
        grads[n] = gfull[n]
    for n in SMALL_SHARDED:
        sh_all = _to_shards(gfull[n], n)
        grads[n] = lax.dynamic_slice(sh_all, (shard,) + (0,) * (sh_all.ndim - 1), (1,) + sh_all.shape[1:])[0]

    gparts.append([("attn_w_in", dw_attn_in[None]), ("attn_w_out", dw_attn_out[None])])
    gq = list(gq) + [gots[16], exchange(grad_part(gparts[4]), XY, "reduce_xy")]
    gsum0 = jnp.concatenate([sum_slots(q, f"sum_chips{i}", bf16) for i, q in enumerate(gq[3:], start=3)], axis=0)
    gc0 = all_gather(gsum0, ("c",), "gather_grad_c")
    gsh = jnp.concatenate([sum_slots(gots[1], "sum_pair1"), sum_slots(gc0, "sum_pair0")], axis=0)
    row, got = 0, {}
    for part, q in zip(gparts, gq):
        flat = gsh[row:row + q.shape[1]].reshape(-1)
        row += q.shape[1]
        off = 0
        for n, g in part:
            sz = g.size // NSH
            got[n] = flat[off:off + sz].reshape((1,) + g.shape[1:-2] + _to_shards(g, n).shape[-2:])
            off += sz
    for n in ("attn_w_in", "attn_w_out", "rec_w_in", "rec_w_out"):
        grads[n] = got[n]
    grads["ffn_w_up"] = jnp.concatenate([got["ffn_w_up0"], got["ffn_w_up1"]], axis=0)
    grads["ffn_w_down"] = jnp.concatenate([got["ffn_w_down0"], got["ffn_w_down1"]], axis=0)

    delta, new_m, new_v = {}, {}, {}

    def as2d(a):
        return a.reshape(-1, a.shape[-1])

    for n in ["ada_w"] + BIG:
        d_, m_, v_ = adamw(as2d(W[n]), as2d(grads[n]), as2d(Mo[n]), as2d(Vo[n]), f"adamw_{n}")
        delta[n], new_m[n], new_v[n] = d_.reshape(W[n].shape), m_.reshape(W[n].shape), v_.reshape(W[n].shape)
    pk = lambda dd: _pack([dd[n] for n in small_names], 128, 256, f32)
    d_, m_, v_ = adamw(pk(W), pk(grads), pk(Mo), pk(Vo), "adamw_small")
    shp = [W[n].shape for n in small_names]
    for dst, src in ((delta, d_), (new_m, m_), (new_v, v_)):
        dst.update(zip(small_names, _unpack(src, shp)))

    return (loss, grad_x[None], *[grads[n] for n in W_NAMES], *[delta[n] for n in W_NAMES],
            *[new_m[n] for n in W_NAMES], *[new_v[n] for n in W_NAMES])
```

```python
import functools
import math

import numpy as np
import jax
import jax.numpy as jnp
from jax import lax
from jax.experimental import pallas as pl
from jax.experimental.pallas import tpu as pltpu

f32 = jnp.float32
bf16 = jnp.bfloat16
HI = lax.Precision.HIGHEST
MESH = pl.DeviceIdType.MESH

HEAD_DIM = 64
BLOCK = 128
A_Q_HEADS = 8
A_KV_HEADS = 2
A_WINDOW = 128
B_HEADS = 8
B_BRANCHES = ((128, 1), (512, 4), (2048, 16))
N_ATTN_HEADS = 16
ATTN_IN = 2304
S5_GROUP = 16
S5_GROUPS = 16
S5_WIDTH = 256
S5_STATE = 64
DN_HEADS = 6
DN_DK = 128
DN_CONV = 4
DN_CHUNK = 64
REC_IN = 3340
REC_PAD = 3584
FFN_CONV = 3
EPS = 1e-6
ADAM_LR = 0.001
ADAM_B1 = 0.9
ADAM_B2 = 0.999
ADAM_EPS = 1e-08
ADAM_WD = 0.01
ADAM_STEP = 10

LANE = 128
SUBLANE = 8
VMEM_LIMIT = 52 * 1024 * 1024
MM_FULL_K = 5632
MM_VMEM_BUDGET = 40 * 1024 * 1024


def _cp(*sem):
    return pltpu.CompilerParams(dimension_semantics=sem, vmem_limit_bytes=VMEM_LIMIT)


def _pick(dim, cap, unit=LANE):
    for t in (2048, 1024, 768, 512, 384, 256, 128, 64, 32, 16, 8):
        if t <= cap and t % unit == 0 and dim % t == 0:
            return t
    return dim


def _dot(a, b, dims=(((1,), (0,)), ((), ())), precision=None):
    return lax.dot_general(a, b, dims, precision=precision, preferred_element_type=f32)


NN = (((1,), (0,)), ((), ()))
NT = (((1,), (1,)), ((), ()))
TN = (((0,), (0,)), ((), ()))


def mm(a, b, *, name, ta=False, tb=False, a_win=None, b_win=None, out_dtypes=(f32,),
       epi=None, epi_mn=(), epi_n=(), tm_cap=1024, tn_cap=8192, tk_cap=None, ride=None):
    coll = _Coll(*ride) if ride else None
    a0, a1 = a.shape
    b0, b1 = b.shape
    aw = a_win or (0, a1)
    bw = b_win or (0, b1)
    if ta:
        K, M = a0, aw[1]
    else:
        M, K = a0, aw[1]
    if tb:
        N, K2 = b0, bw[1]
    else:
        K2, N = b0, bw[1]
    assert K == K2, (a.shape, b.shape, ta, tb, a_win, b_win)
    if tk_cap is None:
        tk_cap = K if K <= MM_FULL_K else 2048
    tk = _pick(K, tk_cap, SUBLANE if (ta and not tb) else LANE)
    nk = K // tk
    sa, sb = a.dtype.itemsize, b.dtype.itemsize
    so = sum(jnp.dtype(d).itemsize for d in out_dtypes)
    n_mn, n_n, n_out = len(epi_mn), len(epi_n), len(out_dtypes)

    def vmem(tm_, tn_):
        return 2 * (tm_ * tk * sa + tk * tn_ * sb + tm_ * tn_ * (so + 4 * n_mn)) + 2 * tm_ * tn_ * 4

    best = None
    for tm_ in (t for t in (1024, 512, 256, 128) if M % t == 0 and (not ta or aw[0] % t == 0)):
        for tn_ in (t for t in (N, N // 2, 1024, 768, 512, 384, 256, 128)
                    if t % LANE == 0 and N % t == 0 and (tb or bw[0] % t == 0)):
            if tm_ <= tm_cap and tn_ <= max(tn_cap, 0) and vmem(tm_, tn_) <= MM_VMEM_BUDGET:
                if best is None or (tm_ * tn_, tn_) > (best[0] * best[1], best[1]):
                    best = (tm_, tn_)
    assert best is not None, (name, M, N, K)
    tm, tn = best
    b_outer = tk * tn * sb > tm * tk * sa

    def ix(f):
        if b_outer:
            return lambda j, i, k: f(i, j, k)
        return f

    if ta:
        mo = aw[0] // tm
        a_spec = pl.BlockSpec((tk, tm), ix(lambda i, j, k: (k, i + mo)))
    else:
        assert aw[0] % tk == 0
        ko = aw[0] // tk
        a_spec = pl.BlockSpec((tm, tk), ix(lambda i, j, k: (i, k + ko)))
    if tb:
        assert bw[0] % tk == 0
        kob = bw[0] // tk
        b_spec = pl.BlockSpec((tn, tk), ix(lambda i, j, k: (j, k + kob)))
    else:
        no = bw[0] // tn
        b_spec = pl.BlockSpec((tk, tn), ix(lambda i, j, k: (k, j + no)))
    dims = (((0 if ta else 1,), (1 if tb else 0,)), ((), ()))

    gi, gj = M // tm, N // tn
    grid = (gj, gi, nk) if b_outer else (gi, gj, nk)

    def body(a_ref, b_ref, *rest):
        mn_refs = rest[:n_mn]
        n_refs = rest[n_mn:n_mn + n_n]
        o0 = n_mn + n_n
        out_refs = rest[o0:o0 + n_out]

        def finish(r):
            if epi is None:
                outs = (r,)
            else:
                outs = epi(r, *[m[...] for m in mn_refs], *[v[...] for v in n_refs])
            for o_ref, o in zip(out_refs, outs):
                o_ref[...] = o.astype(o_ref.dtype)

        part = _dot(a_ref[...].astype(bf16), b_ref[...].astype(bf16), dims)
        if nk == 1:
            finish(part)
        else:
            acc = rest[o0 + n_out]
            k = pl.program_id(2)

            @pl.when(k == 0)
            def _():
                acc[...] = part

            @pl.when(k > 0)
            def _():
                acc[...] += part

            @pl.when(k == nk - 1)
            def _():
                finish(acc[...])

    mn_spec = pl.BlockSpec((tm, tn), ix(lambda i, j, k: (i, j)))
    n_spec = pl.BlockSpec((1, tn), ix(lambda i, j, k: (0, j)))
    outs = _ride_call(
        body, coll, ride, grid=grid,
        in_specs=[a_spec, b_spec] + [mn_spec] * n_mn + [n_spec] * n_n, out_specs=[mn_spec] * n_out,
        out_shape=[jax.ShapeDtypeStruct((M, N), d) for d in out_dtypes],
        scratch_shapes=[pltpu.VMEM((tm, tn), f32)] if nk > 1 else [],
        semantics=("parallel", "parallel", "arbitrary"), name=name, args=[a, b, *epi_mn, *epi_n])
    return outs[0] if len(outs) == 1 else tuple(outs)


def rowwise(fn, *, name, L, tm, rows=(), consts=(), outs=(), sums=()):
    nb = L // tm
    in_specs = []
    arrs = []
    for arr, start, width, kind in rows:
        assert start % width == 0, (name, start, width)
        co = start // width
        hr = SUBLANE * (4 // arr.dtype.itemsize)
        hb = tm // hr
        if kind == "cur":
            in_specs.append(pl.BlockSpec((tm, width), lambda i, co=co: (i, co)))
        elif kind == "prev":
            in_specs.append(pl.BlockSpec((hr, width), lambda i, co=co, hb=hb: (jnp.maximum(i * hb - 1, 0), co)))
        else:
            last = L // hr - 1
            in_specs.append(pl.BlockSpec((hr, width), lambda i, co=co, hb=hb, last=last:
                                         (jnp.minimum((i + 1) * hb, last), co)))
        arrs.append(arr)
    for cst in consts:
        assert cst.ndim == 2
        in_specs.append(pl.BlockSpec(cst.shape, lambda i: (0, 0)))
        arrs.append(cst)
    n_rows, n_c, n_o, n_s = len(rows), len(consts), len(outs), len(sums)
    out_specs = [pl.BlockSpec((tm, w), lambda i: (i, 0)) for w, _ in outs]
    out_specs += [pl.BlockSpec(s, lambda i: (0, 0)) for s in sums]
    out_shape = [jax.ShapeDtypeStruct((L, w), d) for w, d in outs]
    out_shape += [jax.ShapeDtypeStruct(s, f32) for s in sums]

    def body(*refs):
        i = pl.program_id(0)
        vals = [r[...] for r in refs[:n_rows + n_c]]
        res = fn(i, nb, *vals)
        if not isinstance(res, (tuple, list)):
            res = (res,)
        o_refs = refs[n_rows + n_c:n_rows + n_c + n_o]
        s_refs = refs[n_rows + n_c + n_o:]
        for o_ref, o in zip(o_refs, res[:n_o]):
            o_ref[...] = o.astype(o_ref.dtype)
        if n_s:
            @pl.when(i == 0)
            def _():
                for s_ref in s_refs:
                    s_ref[...] = jnp.zeros_like(s_ref)

            for s_ref, s in zip(s_refs, res[n_o:]):
                s_ref[...] += s

    res = pl.pallas_call(
        body,
        grid=(nb,),
        in_specs=in_specs,
        out_specs=out_specs,
        out_shape=out_shape,
        compiler_params=_cp("arbitrary" if n_s else "parallel"),
        name=name,
    )(*arrs)
    return res[0] if len(res) == 1 else tuple(res)


def _shift_down(x, prev8, k):
    cat = jnp.concatenate([prev8, x], axis=0)
    return pltpu.roll(cat, k, 0)[prev8.shape[0]:, :]


def _shift_up(x, next8, k):
    n = x.shape[0]
    cat = jnp.concatenate([x, next8], axis=0)
    return pltpu.roll(cat, n + next8.shape[0] - k, 0)[:n, :]


def _colsum(x):
    return jnp.sum(x, axis=0, keepdims=True)


def _silu(x):
    return x * jax.nn.sigmoid(x)


def _modulate_fn(x, nw, sc, sh):
    r = lax.rsqrt(jnp.mean(x * x, axis=-1, keepdims=True) + EPS)
    return (x * r * nw) * (1.0 + sc) + sh


def modulate_fwd(x, nw, sc, sh, name):
    L, D = x.shape

    def fn(i, nb, xt, nwv, scv, shv):
        return _modulate_fn(xt, nwv, scv, shv)

    return rowwise(fn, name=name, L=L, tm=_pick(L, 512, SUBLANE), rows=[(x, 0, D, "cur")],
                   consts=[nw, sc, sh], outs=[(D, bf16)])


def modulate_bwd(x, nw, sc, sh, dh, dx_in, name):
    L, D = x.shape

    def fn(i, nb, xt, dht, dxt, nwv, scv, shv):
        _, vjp = jax.vjp(_modulate_fn, xt, nwv, scv, shv)
        dx, dnw, dsc, dsh = vjp(dht)
        return dxt + dx, dnw, dsc, dsh

    return rowwise(fn, name=name, L=L, tm=_pick(L, 256, SUBLANE),
                   rows=[(x, 0, D, "cur"), (dh, 0, D, "cur"), (dx_in, 0, D, "cur")],
                   consts=[nw, sc, sh], outs=[(D, f32)], sums=[(1, D)] * 3)


def resid_bwd(dx, y, g, name):
    L, D = dx.shape

    def fn(i, nb, dxt, yt, gv):
        return dxt * gv, _colsum(dxt * yt)

    return rowwise(fn, name=name, L=L, tm=_pick(L, 512, SUBLANE),
                   rows=[(dx, 0, D, "cur"), (y, 0, D, "cur")], consts=[g],
                   outs=[(D, bf16)], sums=[(1, D)])


def _resid_epi(acc, xt, gv):
    return acc, xt + gv * acc


def _stack_rows(rows, n=SUBLANE):
    c = rows[0].shape[1]
    ridx = lax.broadcasted_iota(jnp.int32, (n, c), 0)
    out = jnp.zeros((n, c), f32)
    for j, r in enumerate(rows):
        out = out + jnp.where(ridx == j, r, 0.0)
    return out


def _conv_causal(x, prev8, w):
    W = w.shape[0]
    y = x * w[W - 1:W, :]
    for j in range(W - 1):
        y = y + _shift_down(x, prev8, W - 1 - j) * w[j:j + 1, :]
    return y


def _conv_causal_bwd_x(dy, next8, w):
    W = w.shape[0]
    dx = dy * w[W - 1:W, :]
    for j in range(W - 1):
        dx = dx + _shift_up(dy, next8, W - 1 - j) * w[j:j + 1, :]
    return dx


def _conv_causal_bwd_w(dy, x, prev8, W):
    rows = [_colsum(dy * _shift_down(x, prev8, W - 1 - j)) for j in range(W - 1)]
    rows.append(_colsum(dy * x))
    return _stack_rows(rows)


def ffn_act_fwd(up, conv_w, name):
    L, F2 = up.shape
    F = F2 // 2

    def fn(i, nb, u, p8, w):
        c = _conv_causal(u.astype(f32), p8.astype(f32) * (i > 0).astype(f32), w)
        return _silu(c[:, :F]) * c[:, F:]

    return rowwise(fn, name=name, L=L, tm=_pick(L, 128, SUBLANE),
                   rows=[(up, 0, F2, "cur"), (up, 0, F2, "prev")], consts=[conv_w], outs=[(F, bf16)])


def ffn_act_conv_bwd(up, conv_w, dact, name):
    L, F2 = up.shape
    F = F2 // 2
    W = conv_w.shape[0]

    def fn(i, nb, u, da, p8, un8, dan8, w):
        tm, ext = u.shape[0], un8.shape[0]
        more = (i < nb - 1).astype(f32)
        u, da = u.astype(f32), da.astype(f32)
        p8 = p8.astype(f32) * (i > 0).astype(f32)
        c = _conv_causal(jnp.concatenate([u, un8.astype(f32) * more], axis=0), p8, w)
        dae = jnp.concatenate([da, dan8.astype(f32) * more], axis=0)
        a, b = c[:, :F], c[:, F:]
        sg = jax.nn.sigmoid(a)
        dc = jnp.concatenate([dae * b * (sg * (1.0 + a * (1.0 - sg))), dae * a * sg], axis=1)
        dx = dc[:tm] * w[W - 1:W, :]
        for j in range(W - 1):
            dx = dx + pltpu.roll(dc, tm + ext - (W - 1 - j), 0)[:tm] * w[j:j + 1, :]
        return dx, _conv_causal_bwd_w(dc[:tm], u, p8, W)

    return rowwise(fn, name=name, L=L, tm=_pick(L, 128, SUBLANE),
                   rows=[(up, 0, F2, "cur"), (dact, 0, F, "cur"), (up, 0, F2, "prev"), (up, 0, F2, "next"),
                         (dact, 0, F, "next")],
                   consts=[conv_w], outs=[(F2, bf16)], sums=[(SUBLANE, F2)])


ALIBI = [2.0 ** (-8.0 * (i + 1) / N_ATTN_HEADS) for i in range(N_ATTN_HEADS)]
NEG = -1e30


def _band_mask(n, d, max_dist):
    qi = lax.broadcasted_iota(jnp.int32, (BLOCK, 2 * BLOCK), 0)
    kj = lax.broadcasted_iota(jnp.int32, (BLOCK, 2 * BLOCK), 1)
    dist = BLOCK + qi - kj
    valid = (dist >= 0) & (dist <= max_dist) & ((n > 0) | (kj >= BLOCK))
    return valid, -(d * dist).astype(f32)


def _rms64(x, w):
    r = lax.rsqrt(jnp.mean(x * x, axis=-1, keepdims=True) + EPS)
    xh = x * r
    return xh * w, xh, r


def _rms64_bwd(dy, xh, r, w):
    t = dy * w
    dw = jnp.sum(jnp.sum(dy * xh, axis=0), axis=0, keepdims=True)
    return r * (t - xh * jnp.mean(t * xh, axis=-1, keepdims=True)), dw


class _Plan:
    def __init__(self, dilation, group_a, nq):
        self.d, self.nq = dilation, nq
        if group_a:
            self.P, self.nkv = 1, 1
            self.q0, self.k0, self.v0 = 0, 4, 5
            self.kv_of = lambda j: j // 4
            self.max_dist = A_WINDOW - 1
            slopes = ALIBI[:8]
        else:
            self.P, self.nkv = 4 // nq, nq
            self.q0, self.k0, self.v0 = 6, 10, 14
            self.kv_of = lambda j: j
            self.max_dist = BLOCK
            slopes = ALIBI[8:]
        self.hps = 2 * nq
        sl = np.repeat(np.asarray(slopes, np.float32), HEAD_DIM).reshape(self.P, 1, self.hps * HEAD_DIM)
        self.slopes = jnp.asarray(sl, f32)


def _rows(r, d):
    return pl.ds(r, BLOCK, stride=d) if d > 1 else pl.ds(0, BLOCK)


def _pairs(refs, rows):
    parts = []
    for ref in refs:
        blk = ref[rows, :]
        parts += [blk[:, :HEAD_DIM], blk[:, HEAD_DIM:]]
    return jnp.stack(parts)


def _pairs2(prev_refs, cur_refs, rows):
    parts = []
    for pr, cr in zip(prev_refs, cur_refs):
        blk = jnp.concatenate([pr[rows, :], cr[rows, :]], axis=0)
        parts += [blk[:, :HEAD_DIM], blk[:, HEAD_DIM:]]
    return jnp.stack(parts)


def _lane_pair(t, i):
    return jnp.concatenate([t[2 * i], t[2 * i + 1]], axis=1)


def _riding(body, coll, n_in, n_out, grid):
    if coll is None:
        return body

    def wrapped(*refs):
        ride_refs = (refs[n_in], refs[n_in + 1 + n_out]) + tuple(refs[-3:])
        inner = refs[:n_in] + refs[n_in + 1:n_in + 1 + n_out] + refs[n_in + 2 + n_out:-3]
        pid = [pl.program_id(t) for t in range(len(grid))]

        @pl.when(functools.reduce(jnp.logical_and, [p == 0 for p in pid]))
        def _():
            coll.start(*ride_refs)

        body(*inner)

        @pl.when(functools.reduce(jnp.logical_and, [p == g - 1 for p, g in zip(pid, grid)]))
        def _():
            coll.wait(*ride_refs)

    return wrapped


def _ride_call(body, coll, ride, *, grid, in_specs, out_specs, out_shape, scratch_shapes, semantics, name, args):
    hbm = [pl.BlockSpec(memory_space=pl.ANY)] if coll else []
    return pl.pallas_call(
        _riding(body, coll, len(in_specs), len(out_specs), grid), grid=grid,
        in_specs=list(in_specs) + hbm, out_specs=list(out_specs) + hbm,
        out_shape=list(out_shape) + ([coll.out_shape] if coll else []),
        scratch_shapes=list(scratch_shapes) + (coll.scratch if coll else []),
        compiler_params=_cp(*(["arbitrary"] * len(grid) if coll else semantics)), name=name,
    )(*args, *([ride[0]] if coll else []))


def attn2_fwd(hin, plan, wq, wk, name, ride=None):
    coll = _Coll(*ride) if ride else None
    L = hin.shape[0]
    d, nq, nkv, hps, P = plan.d, plan.nq, plan.nkv, plan.hps, plan.P
    R = BLOCK * d
    nb = L // R
    kv_of, max_dist = plan.kv_of, plan.max_dist
    gqa = 2 * nkv != hps

    def body(*refs):
        q_refs = refs[:nq]
        kp, kc = refs[nq:nq + nkv], refs[nq + nkv:nq + 2 * nkv]
        vp, vc = refs[nq + 2 * nkv:nq + 3 * nkv], refs[nq + 3 * nkv:nq + 4 * nkv]
        sl_ref, wq_ref, wk_ref, o_ref, lse_ref = refs[nq + 4 * nkv:nq + 4 * nkv + 5]
        o_refs = refs[nq + 4 * nkv + 5:2 * nq + 4 * nkv + 5]
        lse_refs = refs[2 * nq + 4 * nkv + 5:]
        n = pl.program_id(1)
        valid, negd = _band_mask(n, d, max_dist)
        slope = jnp.stack([sl_ref[0, :, j * 64:j * 64 + 1] for j in range(hps)])
        wqv, wkv = wq_ref[...], wk_ref[...]

        def residue(r, carry):
            rows = _rows(r, d)
            kn = _rms64(_pairs2(kp, kc, rows), wkv)[0].astype(bf16)
            v = _pairs2(vp, vc, rows).astype(bf16)
            if gqa:
                kn = jnp.stack([kn[kv_of(j)] for j in range(hps)])
                v = jnp.stack([v[kv_of(j)] for j in range(hps)])
            qn = _rms64(_pairs(q_refs, rows), wqv)[0].astype(bf16)
            s = _dot(qn, kn, BNT) * (HEAD_DIM ** -0.5) + slope * negd
            s = jnp.where(valid, s, NEG)
            m = jnp.max(s, axis=-1, keepdims=True)
            p = jnp.exp(s - m)
            l = jnp.sum(p, axis=-1, keepdims=True)
            o = _dot(p.astype(bf16), v, BNN) / l
            lse = jnp.broadcast_to(m + jnp.log(l), (hps, BLOCK, HEAD_DIM))
            for i in range(nq):
                o_refs[i][rows, :] = _lane_pair(o, i)
                lse_refs[i][rows, :] = _lane_pair(lse, i)
            return carry

        lax.fori_loop(0, d, residue, 0)
        for i in range(nq):
            o_ref[:, i * 128:(i + 1) * 128] = o_refs[i][...]
            lse_ref[:, i * 128:(i + 1) * 128] = lse_refs[i][...]

    col = lambda c0, i: (lambda p, n: (n, c0 + p * nq + i))
    prv = lambda c0, i: (lambda p, n: (jnp.maximum(n - 1, 0), c0 + p * nq + i))
    blk = lambda f: pl.BlockSpec((R, 128), f)
    in_specs = [blk(col(plan.q0, i)) for i in range(nq)]
    in_specs += [blk(prv(plan.k0, i)) for i in range(nkv)] + [blk(col(plan.k0, i)) for i in range(nkv)]
    in_specs += [blk(prv(plan.v0, i)) for i in range(nkv)] + [blk(col(plan.v0, i)) for i in range(nkv)]
    in_specs += [pl.BlockSpec((1, 1, hps * 64), lambda p, n: (p, 0, 0)),
                 pl.BlockSpec((1, 64), lambda p, n: (0, 0)), pl.BlockSpec((1, 64), lambda p, n: (0, 0))]
    wide = pl.BlockSpec((R, 128 * nq), lambda p, n: (n, p))
    return _ride_call(
        body, coll, ride, grid=(P, nb), in_specs=in_specs, out_specs=[wide, wide],
        out_shape=[jax.ShapeDtypeStruct((L, 512), f32)] * 2,
        scratch_shapes=[pltpu.VMEM((R, 128), f32)] * (2 * nq),
        semantics=("parallel", "parallel"), name=name,
        args=[hin] * (nq + 4 * nkv) + [plan.slopes, wq, wk])


def attn2_bwd(hin, plan, wq, wk, o, lse, do, dlse, dw0, name, ride=None):
    coll = _Coll(*ride) if ride else None
    L = hin.shape[0]
    d, nq, nkv, hps, P = plan.d, plan.nq, plan.nkv, plan.hps, plan.P
    R = BLOCK * d
    nb = L // R
    kv_of, max_dist = plan.kv_of, plan.max_dist
    nkh = 2 * nkv
    gqa = nkh != hps
    n_in = nq + 4 * nkv + 3 + 4 * nq + 2

    def body(*refs):
        q_refs = refs[:nq]
        kp, kc = refs[nq:nq + nkv], refs[nq + nkv:nq + 2 * nkv]
        vp, vc = refs[nq + 2 * nkv:nq + 3 * nkv], refs[nq + 3 * nkv:nq + 4 * nkv]
        b = nq + 4 * nkv
        sl_ref, wq_ref, wk_ref = refs[b:b + 3]
        b += 3
        o_refs, lse_refs = refs[b:b + nq], refs[b + nq:b + 2 * nq]
        do_refs, dlse_refs = refs[b + 2 * nq:b + 3 * nq], refs[b + 3 * nq:b + 4 * nq]
        dwq0_ref, dwk0_ref = refs[b + 4 * nq:b + 4 * nq + 2]
        dq_ref, dk_ref, dv_ref, dwq_ref, dwk_ref = refs[n_in:n_in + 5]
        sc = refs[n_in + 5:]
        dq_s, dk_s, dv_s = sc[:nq], sc[nq:nq + nkv], sc[nq + nkv:nq + 2 * nkv]
        ck, cv = sc[nq + 2 * nkv:nq + 3 * nkv], sc[nq + 3 * nkv:]
        pp = pl.program_id(0)
        n = pl.program_id(1)

        @pl.when((pp == 0) & (n == 0))
        def _():
            dwq_ref[...] = dwq0_ref[...]
            dwk_ref[...] = dwk0_ref[...]

        @pl.when(n == 0)
        def _():
            for c in (*ck, *cv):
                c[...] = jnp.zeros_like(c)

        @pl.when(n < nb)
        def _():
            valid, negd = _band_mask(n, d, max_dist)
            slope = jnp.stack([sl_ref[0, :, j * 64:j * 64 + 1] for j in range(hps)])
            wqv, wkv = wq_ref[...], wk_ref[...]
            hs = range(hps)

            def residue(r, carry):
                rows = _rows(r, d)
                kn_f, kh, rk = _rms64(_pairs2(kp, kc, rows), wkv)
                kn = kn_f.astype(bf16)
                v = _pairs2(vp, vc, rows).astype(bf16)
                if gqa:
                    kn = jnp.stack([kn[kv_of(j)] for j in hs])
                    v = jnp.stack([v[kv_of(j)] for j in hs])
                qn_f, qh, rq = _rms64(_pairs(q_refs, rows), wqv)
                qn = qn_f.astype(bf16)
                s = _dot(qn, kn, BNT) * (HEAD_DIM ** -0.5) + slope * negd
                p = jnp.where(valid, jnp.exp(s - _pairs(lse_refs, rows)[:, :, :1]), 0.0)
                do_h = _pairs(do_refs, rows)
                delta = jnp.sum(do_h * _pairs(o_refs, rows), axis=-1, keepdims=True)
                do_b = do_h.astype(bf16)
                dp = _dot(do_b, v, BNT)
                ds = (p * (dp - delta + _pairs(dlse_refs, rows)[:, :, :1])).astype(bf16)
                dqn = _dot(ds, kn, BNN) * (HEAD_DIM ** -0.5)
                dkn = _dot(ds, qn, BTN) * (HEAD_DIM ** -0.5)
                dvv = _dot(p.astype(bf16), do_b, BTN)
                if gqa:
                    grp = lambda t: jnp.stack([sum(t[j] for j in hs if kv_of(j) == h) for h in range(nkh)])
                    dkn, dvv = grp(dkn), grp(dvv)
                dq, dwq = _rms64_bwd(dqn, qh, rq, wqv)
                dk, dwk = _rms64_bwd(dkn, kh, rk, wkv)
                for i in range(nq):
                    dq_s[i][rows, :] = _lane_pair(dq, i)
                for i in range(nkv):
                    dk_s[i][rows, :] = ck[i][rows, :] + _lane_pair(dk[:, :BLOCK], i)
                    dv_s[i][rows, :] = cv[i][rows, :] + _lane_pair(dvv[:, :BLOCK], i)
                    ck[i][rows, :] = _lane_pair(dk[:, BLOCK:], i)
                    cv[i][rows, :] = _lane_pair(dvv[:, BLOCK:], i)
                return carry[0] + dwq, carry[1] + dwk

            zero = jnp.zeros((1, HEAD_DIM), f32)
            dwq_a, dwk_a = lax.fori_loop(0, d, residue, (zero, zero))
            dwq_ref[...] += dwq_a
            dwk_ref[...] += dwk_a
            for i in range(nq):
                dq_ref[:, i * 128:(i + 1) * 128] = dq_s[i][...]
            for i in range(nkv):
                dk_ref[:, i * 128:(i + 1) * 128] = dk_s[i][...]
                dv_ref[:, i * 128:(i + 1) * 128] = dv_s[i][...]

        @pl.when(n == nb)
        def _():
            for i in range(nkv):
                dk_ref[:, i * 128:(i + 1) * 128] = ck[i][...]
                dv_ref[:, i * 128:(i + 1) * 128] = cv[i][...]

    cl = lambda n: jnp.minimum(n, nb - 1)
    pv = lambda n: jnp.maximum(jnp.minimum(n, nb - 1) - 1, 0)
    col = lambda c0, i: (lambda p, n: (cl(n), c0 + p * nq + i))
    prv = lambda c0, i: (lambda p, n: (pv(n), c0 + p * nq + i))
    blk = lambda f: pl.BlockSpec((R, 128), f)
    w64 = pl.BlockSpec((1, 64), lambda p, n: (0, 0))
    in_specs = [blk(col(plan.q0, i)) for i in range(nq)]
    in_specs += [blk(prv(plan.k0, i)) for i in range(nkv)] + [blk(col(plan.k0, i)) for i in range(nkv)]
    in_specs += [blk(prv(plan.v0, i)) for i in range(nkv)] + [blk(col(plan.v0, i)) for i in range(nkv)]
    in_specs += [pl.BlockSpec((1, 1, hps * 64), lambda p, n: (p, 0, 0)), w64, w64]
    in_specs += [blk(col(0, i)) for i in range(nq)] * 4 + [w64, w64]
    kvw = 128 * nkv
    out_specs = [pl.BlockSpec((R, 128 * nq), lambda p, n: (cl(n), p)),
                 pl.BlockSpec((R, kvw), lambda p, n: (jnp.maximum(n - 1, 0), p)),
                 pl.BlockSpec((R, kvw), lambda p, n: (jnp.maximum(n - 1, 0), p)), w64, w64]
    same = lambda a: [a] * nq
    return _ride_call(
        body, coll, ride, grid=(P, nb + 1), in_specs=in_specs, out_specs=out_specs,
        out_shape=[jax.ShapeDtypeStruct((L, 512), f32), jax.ShapeDtypeStruct((L, kvw * P), f32),
                   jax.ShapeDtypeStruct((L, kvw * P), f32), jax.ShapeDtypeStruct((1, 64), f32),
                   jax.ShapeDtypeStruct((1, 64), f32)],
        scratch_shapes=[pltpu.VMEM((R, 128), f32)] * (nq + 4 * nkv),
        semantics=("arbitrary", "arbitrary"), name=name,
        args=[hin] * (nq + 4 * nkv) + [plan.slopes, wq, wk, *same(o), *same(lse), *same(do), *same(dlse), *dw0])


def _head_sum(x):
    c = x.shape[1]
    r = lax.broadcasted_iota(jnp.int32, (c, c), 0) // HEAD_DIM
    q = lax.broadcasted_iota(jnp.int32, (c, c), 1) // HEAD_DIM
    return _dot(x, (r == q).astype(f32), precision=HI)


def attn_merge_fwd(oa, la, obs, lbs, sinkb, name):
    L = oa.shape[0]

    def fn(i, nb, oa_t, la_t, o1, o2, o3, l1, l2, l3, sk):
        ya = oa_t * jax.nn.sigmoid(la_t - sk)
        m = jnp.maximum(jnp.maximum(l1, l2), l3)
        e1, e2, e3 = jnp.exp(l1 - m), jnp.exp(l2 - m), jnp.exp(l3 - m)
        yb = (e1 * o1 + e2 * o2 + e3 * o3) / (e1 + e2 + e3)
        return jnp.concatenate([ya, yb], axis=1)

    rows = [(a, 0, 512, "cur") for a in (oa, la, *obs, *lbs)]
    return rowwise(fn, name=name, L=L, tm=_pick(L, 256, SUBLANE), rows=rows, consts=[sinkb], outs=[(1024, bf16)])


def attn_merge_bwd(dcat, oa, la, obs, lbs, sinkb, name):
    L = oa.shape[0]

    def fn(i, nb, da, db, oa_t, la_t, o1, o2, o3, l1, l2, l3, sk):
        keep = jax.nn.sigmoid(la_t - sk)
        dla = _head_sum(da * oa_t) * keep * (1.0 - keep)
        m = jnp.maximum(jnp.maximum(l1, l2), l3)
        e1, e2, e3 = jnp.exp(l1 - m), jnp.exp(l2 - m), jnp.exp(l3 - m)
        z = e1 + e2 + e3
        w1, w2, w3 = e1 / z, e2 / z, e3 / z
        g1, g2, g3 = _head_sum(db * o1), _head_sum(db * o2), _head_sum(db * o3)
        gm = w1 * g1 + w2 * g2 + w3 * g3
        return (da * keep, dla, w1 * db, w2 * db, w3 * db,
                w1 * (g1 - gm), w2 * (g2 - gm), w3 * (g3 - gm), -_colsum(dla))

    rows = [(dcat, 0, 512, "cur"), (dcat, 512, 512, "cur")] + [(a, 0, 512, "cur") for a in (oa, la, *obs, *lbs)]
    return rowwise(fn, name=name, L=L, tm=_pick(L, 256, SUBLANE), rows=rows, consts=[sinkb],
                   outs=[(512, f32)] * 8, sums=[(1, 512)])


def attn_assemble(dqa, dka, dva, dqs, dks, dvs, name):
    L = dqa.shape[0]

    def fn(i, nb, qa, ka, va, q1, q2, q3, k1, k2, k3, v1, v2, v3):
        return jnp.concatenate([qa, ka, va, q1 + q2 + q3, k1 + k2 + k3, v1 + v2 + v3], axis=1)

    rows = [(dqa, 0, 512, "cur"), (dka, 0, 128, "cur"), (dva, 0, 128, "cur")]
    rows += [(a, 0, 512, "cur") for a in (*dqs, *dks, *dvs)]
    return rowwise(fn, name=name, L=L, tm=_pick(L, 256, SUBLANE), rows=rows, outs=[(ATTN_IN, bf16)])


def attention_fwd(hin, wqa, wka, wqb, wkb, sinkb, ride=None):
    oa, la = attn2_fwd(hin, _Plan(1, True, 4), wqa, wka, "attn_a_fwd")
    obs, lbs, got = [], [], None
    for _, d in B_BRANCHES:
        res = attn2_fwd(hin, _Plan(d, False, 2), wqb, wkb, f"attn_b{d}_fwd", ride=ride if d == 1 else None)
        obs.append(res[0])
        lbs.append(res[1])
        got = res[2] if (d == 1 and ride) else got
    ocat = attn_merge_fwd(oa, la, obs, lbs, sinkb, "attn_merge_fwd")
    return ocat, (oa, la, obs, lbs), got


def attention_bwd(hin, wqa, wka, wqb, wkb, sinkb, saved, dcat, rides=None):
    rides = dict(rides or {})
    gots = {}

    def ride_of(key):
        r = rides.get(key)
        return r(gots) if callable(r) else r

    oa, la, obs, lbs = saved
    res = attn_merge_bwd(dcat, oa, la, obs, lbs, sinkb, "attn_merge_bwd")
    doa, dla, dos, dls, dsink = res[0], res[1], res[2:5], res[5:8], res[8]
    zero = jnp.zeros((1, 64), f32)
    res = attn2_bwd(hin, _Plan(1, True, 4), wqa, wka, oa, la, doa, dla, (zero, zero), "attn_a_bwd", ride=ride_of("a"))
    dqa, dka, dva, dwqa, dwka = res[:5]
    if "a" in rides:
        gots["a"] = res[5]
    dqs, dks, dvs = [], [], []
    dwqb = dwkb = zero
    for g, (_, d) in enumerate(B_BRANCHES):
        res = attn2_bwd(hin, _Plan(d, False, 2 if d < 16 else 1), wqb, wkb, obs[g], lbs[g],
                        dos[g], dls[g], (dwqb, dwkb), f"attn_b{d}_bwd", ride=ride_of(d))
        dq, dk, dv, dwqb, dwkb = res[:5]
        if d in rides:
            gots[d] = res[5]
        dqs.append(dq)
        dks.append(dk)
        dvs.append(dv)
    dhin = attn_assemble(dqa, dka, dva, dqs, dks, dvs, "attn_assemble")
    return dhin, dwqa, dwka, dwqb, dwkb, dsink, gots


NS = S5_GROUPS * S5_STATE


def _s5_param_fn(lr, li, ldt):
    dt = jnp.exp(ldt)
    mag, ang = jnp.exp(lr * dt), li * dt
    ab_re, ab_im = mag * jnp.cos(ang), mag * jnp.sin(ang)
    nr, ni = ab_re - 1.0, ab_im
    den = lr * lr + li * li
    return ab_re, ab_im, (nr * lr + ni * li) / den, (ni * lr - nr * li) / den


def s5_params_fwd(lr, li, ldt):
    def body(lr_ref, li_ref, ldt_ref, *outs):
        for o_ref, o in zip(outs, _s5_param_fn(lr_ref[...], li_ref[...], ldt_ref[...])):
            o_ref[...] = o

    return pl.pallas_call(body, out_shape=[jax.ShapeDtypeStruct(lr.shape, f32)] * 4, name="s5_params_fwd")(lr, li, ldt)


def s5_params_bwd(lr, li, ldt, cts):
    def body(lr_ref, li_ref, ldt_ref, c0, c1, c2, c3, dlr, dli, dldt):
        _, vjp = jax.vjp(_s5_param_fn, lr_ref[...], li_ref[...], ldt_ref[...])
        a, b, c = vjp((c0[...], c1[...], c2[...], c3[...]))
        dlr[...] = a
        dli[...] = b
        dldt[...] = c

    return pl.pallas_call(
        body, out_shape=[jax.ShapeDtypeStruct(lr.shape, f32), jax.ShapeDtypeStruct(li.shape, f32),
                         jax.ShapeDtypeStruct(ldt.shape, f32)], name="s5_params_bwd")(lr, li, ldt, *cts)


def _cmul(ar, ai, br, bi):
    return ar * br - ai * bi, ar * bi + ai * br


def s5_scan(z, ab_re, ab_im, f_re, f_im, *, reverse, name):
    L = z.shape[0]
    tm = _pick(L, 256, SUBLANE)
    nb = L // tm
    ng = tm // SUBLANE
    use_f = f_re is not None
    consts = [ab_re, ab_im] + ([f_re, f_im] if use_f else [])

    def body(*refs):
        z_ref = refs[0]
        c_refs = refs[1:1 + len(consts)]
        x_ref, car = refs[1 + len(consts)], refs[2 + len(consts)]
        i = pl.program_id(0)

        @pl.when(i == 0)
        def _():
            car[...] = jnp.zeros_like(car)

        a1 = (c_refs[0][...], c_refs[1][...])
        a2 = _cmul(*a1, *a1)
        a3 = _cmul(*a2, *a1)
        a4 = _cmul(*a2, *a2)
        pw = [a1, a2, a3, a4, _cmul(*a4, *a1), _cmul(*a4, *a2), _cmul(*a4, *a3), _cmul(*a4, *a4)]
        if reverse:
            pw = pw[::-1]
        pw_re = _stack_rows([p[0] for p in pw])
        pw_im = _stack_rows([p[1] for p in pw])
        ridx = lax.broadcasted_iota(jnp.int32, (SUBLANE, NS), 0)
        if use_f:
            fr, fi = c_refs[2][...], c_refs[3][...]

        def group(s, carry):
            cr, ci = carry
            g = (ng - 1 - s) if reverse else s
            r0 = pl.multiple_of(g * SUBLANE, SUBLANE)
            xr = z_ref[pl.ds(r0, SUBLANE), 0:NS]
            xi = z_ref[pl.ds(r0, SUBLANE), NS:2 * NS]
            if use_f:
                xr, xi = _cmul(fr, fi, xr, xi)
            for sft, (pr, pi) in ((1, a1), (2, a2), (4, a4)):
                if reverse:
                    keep = ridx < SUBLANE - sft
                    sr = jnp.where(keep, pltpu.roll(xr, SUBLANE - sft, 0), 0.0)
                    si = jnp.where(keep, pltpu.roll(xi, SUBLANE - sft, 0), 0.0)
                else:
                    keep = ridx >= sft
                    sr = jnp.where(keep, pltpu.roll(xr, sft, 0), 0.0)
                    si = jnp.where(keep, pltpu.roll(xi, sft, 0), 0.0)
                tr, ti = _cmul(pr, pi, sr, si)
                xr, xi = xr + tr, xi + ti
            tr, ti = _cmul(pw_re, pw_im, cr, ci)
            xr, xi = xr + tr, xi + ti
            x_ref[pl.ds(r0, SUBLANE), 0:NS] = xr
            x_ref[pl.ds(r0, SUBLANE), NS:2 * NS] = xi
            row = 0 if reverse else SUBLANE - 1
            return xr[row:row + 1, :], xi[row:row + 1, :]

        cr, ci = lax.fori_loop(0, ng, group, (car[0:1, 0:NS], car[0:1, NS:2 * NS]))
        car[0:1, 0:NS] = cr
        car[0:1, NS:2 * NS] = ci

    blk = (lambda i: (nb - 1 - i, 0)) if reverse else (lambda i: (i, 0))
    return pl.pallas_call(
        body, grid=(nb,),
        in_specs=[pl.BlockSpec((tm, 2 * NS), blk)] + [pl.BlockSpec((1, NS), lambda i: (0, 0))] * len(consts),
        out_specs=pl.BlockSpec((tm, 2 * NS), blk),
        out_shape=jax.ShapeDtypeStruct((L, 2 * NS), f32),
        scratch_shapes=[pltpu.VMEM((SUBLANE, 2 * NS), f32)],
        compiler_params=_cp("arbitrary"), name=name,
    )(z, *consts)


def _s5_post_fn(ypre, u, dvec, gw, gb):
    y = ypre + dvec * u
    g = jax.nn.gelu(y)
    z = _dot(g.astype(bf16), gw.astype(bf16)) + gb
    return g * jax.nn.sigmoid(z)


def s5_post_fwd(ypre, hin, dvec, gw, gb):
    L = ypre.shape[0]

    def fn(i, nb, yt, ut, dv, gwv, gbv):
        return _s5_post_fn(yt, ut, dv, gwv, gbv)

    return rowwise(fn, name="s5_post_fwd", L=L, tm=_pick(L, 512, SUBLANE),
                   rows=[(ypre, 0, S5_WIDTH, "cur"), (hin, 3072, S5_WIDTH, "cur")],
                   consts=[dvec, gw, gb], outs=[(S5_WIDTH, f32)])


def s5_post_bwd(ypre, hin, dvec, gw, gb, dycat):
    L = ypre.shape[0]

    def fn(i, nb, yt, ut, dyt, dv, gwv, gbv):
        _, vjp = jax.vjp(_s5_post_fn, yt, ut, dv, gwv, gbv)
        return vjp(dyt)

    return rowwise(fn, name="s5_post_bwd", L=L, tm=_pick(L, 512, SUBLANE),
                   rows=[(ypre, 0, S5_WIDTH, "cur"), (hin, 3072, S5_WIDTH, "cur"), (dycat, 0, S5_WIDTH, "cur")],
                   consts=[dvec, gw, gb], outs=[(S5_WIDTH, f32)] * 2,
                   sums=[(1, S5_WIDTH), (S5_WIDTH, S5_WIDTH), (1, S5_WIDTH)])


def s5_acc(G, X, bu, f_re, f_im):
    L = G.shape[0]

    def fn(i, nb, g, x, b, xp8, fr, fi):
        gr, gi = g[:, :NS], g[:, NS:]
        xp = _shift_down(x, xp8 * (i > 0).astype(f32), 1)
        xr, xi = xp[:, :NS], xp[:, NS:]
        br, bi = b[:, :NS], b[:, NS:]
        dbu = jnp.concatenate([fr * gr + fi * gi, fr * gi - fi * gr], axis=1)
        return (dbu, _colsum(xr * gr + xi * gi), _colsum(xr * gi - xi * gr),
                _colsum(br * gr + bi * gi), _colsum(br * gi - bi * gr))

    return rowwise(fn, name="s5_acc", L=L, tm=_pick(L, 256, SUBLANE),
                   rows=[(G, 0, 2 * NS, "cur"), (X, 0, 2 * NS, "cur"), (bu, 0, 2 * NS, "cur"), (X, 0, 2 * NS, "prev")],
                   consts=[f_re, f_im], outs=[(2 * NS, bf16)], sums=[(1, NS)] * 4)


def _s5_blockdiag(b_re, b_im, c_re, c_im):
    eye = jnp.eye(S5_GROUPS, dtype=f32)
    bb = lambda b: jnp.einsum("gpi,gh->gihp", b, eye).reshape(S5_WIDTH, NS)
    cc = lambda c: jnp.einsum("gip,gh->gphi", c, eye).reshape(NS, S5_WIDTH)
    return jnp.concatenate([bb(b_re), bb(b_im)], axis=1), jnp.concatenate([cc(c_re), -cc(c_im)], axis=0)


def _s5_blockdiag_grads(dB, dC):
    gb = lambda m: jnp.einsum("gigp->gpi", m.reshape(S5_GROUPS, S5_GROUP, S5_GROUPS, S5_STATE))
    gc = lambda m: jnp.einsum("gpgi->gip", m.reshape(S5_GROUPS, S5_STATE, S5_GROUPS, S5_GROUP))
    return gb(dB[:, :NS]), gb(dB[:, NS:]), gc(dC[:NS]), -gc(dC[NS:])


def s5_fwd(hin, prm):
    ab_re, ab_im, f_re, f_im = s5_params_fwd(prm["lr"], prm["li"], prm["ldt"])
    flat = lambda a: a.reshape(1, NS)
    ab_re, ab_im, f_re, f_im = flat(ab_re), flat(ab_im), flat(f_re), flat(f_im)
    Bblk, Cblk = _s5_blockdiag(prm["b_re"], prm["b_im"], prm["c_re"], prm["c_im"])
    bu = mm(hin, Bblk, name="s5_bu", a_win=(3072, S5_WIDTH))
    X = s5_scan(bu, ab_re, ab_im, f_re, f_im, reverse=False, name="s5_scan_fwd")
    ypre = mm(X, Cblk, name="s5_y")
    yc = s5_post_fwd(ypre, hin, prm["d"], prm["gw"], prm["gb"])
    return yc, (ab_re, ab_im, f_re, f_im, Bblk, Cblk, bu, X, ypre)


def s5_bwd(hin, prm, saved, dycat):
    ab_re, ab_im, f_re, f_im, Bblk, Cblk, bu, X, ypre = saved
    dypre, du_skip, dd, dgw, dgb = s5_post_bwd(ypre, hin, prm["d"], prm["gw"], prm["gb"], dycat)
    dX = mm(dypre, Cblk, tb=True, name="s5_dx")
    dC = mm(X, dypre, ta=True, name="s5_dc")
    G = s5_scan(dX, ab_re, -ab_im, None, None, reverse=True, name="s5_scan_bwd")
    dbu, dar, dai, dfr, dfi = s5_acc(G, X, bu, f_re, f_im)
    dB = mm(hin, dbu, ta=True, a_win=(3072, S5_WIDTH), name="s5_db")
    du_b = mm(dbu, Bblk, tb=True, name="s5_du")
    sh = prm["lr"].shape
    dlr, dli, dldt = s5_params_bwd(prm["lr"], prm["li"], prm["ldt"],
                                   [a.reshape(sh) for a in (dar, dai, dfr, dfi)])
    db_re, db_im, dc_re, dc_im = _s5_blockdiag_grads(dB, dC)
    grads = dict(lr=dlr, li=dli, ldt=dldt, b_re=db_re, b_im=db_im, c_re=dc_re, c_im=dc_im, d=dd, gw=dgw, gb=dgb)
    return du_skip, du_b, grads


DN_W = DN_HEADS * DN_DK
QKV_W = 3 * DN_W


def _softplus(x):
    return jnp.maximum(x, 0.0) + jnp.log(1.0 + jnp.exp(-jnp.abs(x)))


def _dn_pre(c, ab, alog, dtb):
    s = _silu(c)
    parts = []
    for h in range(2 * DN_HEADS):
        sh = s[:, h * 128:(h + 1) * 128]
        scale = DN_DK ** -0.5 if h < DN_HEADS else 1.0
        parts.append(sh * (lax.rsqrt(jnp.sum(sh * sh, axis=-1, keepdims=True) + EPS) * scale))
    parts.append(s[:, 2 * DN_W:])
    g = -jnp.exp(alog) * _softplus(ab[:, :128] + dtb)
    beta = jax.nn.sigmoid(ab[:, 128:])
    return jnp.concatenate(parts, axis=1), jnp.concatenate([g, beta], axis=1)


def _dn_pre_bwd(c, ab, alog, dtb, dqkv, dgb):
    sg = jax.nn.sigmoid(c)
    s = c * sg
    parts = []
    for h in range(2 * DN_HEADS):
        sh = s[:, h * 128:(h + 1) * 128]
        dy = dqkv[:, h * 128:(h + 1) * 128]
        scale = DN_DK ** -0.5 if h < DN_HEADS else 1.0
        r = lax.rsqrt(jnp.sum(sh * sh, axis=-1, keepdims=True) + EPS)
        parts.append(scale * r * (dy - sh * (r * r) * jnp.sum(dy * sh, axis=-1, keepdims=True)))
    parts.append(dqkv[:, 2 * DN_W:])
    dc = jnp.concatenate(parts, axis=1) * (sg * (1.0 + c * (1.0 - sg)))
    pre = ab[:, :128] + dtb
    ea = jnp.exp(alog)
    dg = dgb[:, :128]
    da = dg * (-ea) * jax.nn.sigmoid(pre)
    dalog = _colsum(dg * (-ea) * _softplus(pre))
    beta = jax.nn.sigmoid(ab[:, 128:])
    db = dgb[:, 128:] * beta * (1.0 - beta)
    return dc, jnp.concatenate([da, db], axis=1), dalog, _colsum(da)


def dn_pre_fwd(hin, conv_w, alog, dtb):
    L = hin.shape[0]

    def fn(i, nb, x, ab, p8, w, al, db):
        c = _conv_causal(x, p8 * (i > 0).astype(f32), w)
        return _dn_pre(c, ab, al, db)

    return rowwise(fn, name="dn_pre_fwd", L=L, tm=_pick(L, 256, SUBLANE),
                   rows=[(hin, 0, QKV_W, "cur"), (hin, 3328, 256, "cur"), (hin, 0, QKV_W, "prev")],
                   consts=[conv_w, alog, dtb], outs=[(QKV_W, f32), (256, f32)])


def dn_pre_bwd(hin, conv_w, alog, dtb, dqkv3, dg, dbeta):
    L = hin.shape[0]

    def fn(i, nb, x, ab, dq, dk, dv, dgt, dbt, p8, w, al, db):
        c = _conv_causal(x, p8 * (i > 0).astype(f32), w)
        return _dn_pre_bwd(c, ab, al, db, jnp.concatenate([dq, dk, dv], axis=1), jnp.concatenate([dgt, dbt], axis=1))

    rows = [(hin, 0, QKV_W, "cur"), (hin, 3328, 256, "cur")] + [(a, 0, DN_W, "cur") for a in dqkv3]
    rows += [(dg, 0, 128, "cur"), (dbeta, 0, 128, "cur"), (hin, 0, QKV_W, "prev")]
    return rowwise(fn, name="dn_pre_bwd", L=L, tm=_pick(L, 128, SUBLANE), rows=rows,
                   consts=[conv_w, alog, dtb], outs=[(QKV_W, f32), (256, f32)], sums=[(1, 128), (1, 128)])


def _split(a):
    hi = a.astype(bf16)
    return hi, (a - hi.astype(f32)).astype(bf16)


def _dot3_raw(a, b, dims):
    ah, al = _split(a)
    bh, bl = _split(b)
    return _dot(ah, bh, dims) + (_dot(ah, bl, dims) + _dot(al, bh, dims))


@functools.partial(jax.custom_vjp, nondiff_argnums=(2,))
def _dot3(a, b, dims=NN):
    return _dot3_raw(a, b, dims)


def _dot3_fwd(a, b, dims):
    return _dot3_raw(a, b, dims), (a, b)


BNN = (((2,), (1,)), ((0,), (0,)))
BNT = (((2,), (2,)), ((0,), (0,)))
BTN = (((1,), (1,)), ((0,), (0,)))


def _dot_bwd(raw, dims, res, g):
    a, b = res
    nn, nt, tn = (BNN, BNT, BTN) if dims[1][0] else (NN, NT, TN)
    if dims == nn:
        return raw(g, b, nt), raw(a, g, tn)
    if dims == nt:
        return raw(g, b, nn), raw(g, a, tn)
    assert dims == tn
    return raw(b, g, nt), raw(a, g, nn)


_dot3.defvjp(_dot3_fwd, functools.partial(_dot_bwd, _dot3_raw))


def _dot1_raw(a, b, dims):
    return _dot(a.astype(bf16), b.astype(bf16), dims)


@functools.partial(jax.custom_vjp, nondiff_argnums=(2,))
def _dot1(a, b, dims=NN):
    return _dot1_raw(a, b, dims)


_dot1.defvjp(lambda a, b, dims: (_dot1_raw(a, b, dims), (a, b)), functools.partial(_dot_bwd, _dot1_raw))


def _dn_chunk(q, k, v, gcol, bcol, S):
    C = q.shape[1]
    r = lax.broadcasted_iota(jnp.int32, (C, C), 0)
    c = lax.broadcasted_iota(jnp.int32, (C, C), 1)
    tril = (r >= c).astype(f32)
    strict = (r > c).astype(f32)
    eye = (r == c).astype(f32)
    hd = _dot3
    grow = jnp.sum(eye * gcol, axis=1, keepdims=True)
    Gcol = jnp.sum(tril * grow, axis=2, keepdims=True)
    Grow = jnp.sum(eye * Gcol, axis=1, keepdims=True)
    gamma = jnp.exp((Gcol - Grow) * tril) * tril
    ld = _dot1
    nmat = strict * bcol * ld(k, k, BNT) * gamma
    T = eye - nmat
    Pw = hd(nmat, nmat, BNN)
    for step in range(5):
        T = T + hd(T, Pw, BNN)
        if step < 4:
            Pw = hd(Pw, Pw, BNN)
    eG = jnp.exp(Gcol)
    u = hd(T, bcol * v, BNN)
    w = hd(T, (bcol * eG) * k, BNN)
    qk = ld(q, k, BNT) * gamma
    vnew = u - ld(w, S, BNN)
    o = ld(q * eG, S, BNN) + ld(qk, vnew, BNN)
    Glast = jnp.sum(gcol, axis=1, keepdims=True)
    S2 = S * jnp.exp(Glast) + ld(k * jnp.exp(Glast - Gcol), vnew, BTN)
    return o, S2


def _heads(x_ref):
    return jnp.stack([x_ref[:, h * 128:(h + 1) * 128] for h in range(DN_HEADS)])


def _head_cols(g_ref):
    return jnp.stack([g_ref[:, h:h + 1] for h in range(DN_HEADS)])


def dn_chunks_fwd(qkvn, gb):
    L = qkvn.shape[0]
    C = DN_CHUNK
    nc = L // C

    def body(q_ref, k_ref, v_ref, g_ref, b_ref, o_ref, sin_ref, S):
        n = pl.program_id(0)

        @pl.when(n == 0)
        def _():
            S[...] = jnp.zeros_like(S)

        s_in = S[...]
        sin_ref[...] = s_in
        o, s2 = _dn_chunk(_heads(q_ref), _heads(k_ref), _heads(v_ref), _head_cols(g_ref), _head_cols(b_ref), s_in)
        for h in range(DN_HEADS):
            o_ref[:, h * 128:(h + 1) * 128] = o[h]
        S[...] = s2

    blk = lambda j: pl.BlockSpec((C, DN_W), lambda n, j=j: (n, j))
    gblk = lambda j: pl.BlockSpec((C, 128), lambda n, j=j: (n, j))
    return pl.pallas_call(
        body, grid=(nc,),
        in_specs=[blk(0), blk(1), blk(2), gblk(0), gblk(1)],
        out_specs=[pl.BlockSpec((C, DN_W), lambda n: (n, 0)),
                   pl.BlockSpec((DN_HEADS, None, 128, 128), lambda n: (0, n, 0, 0))],
        out_shape=[jax.ShapeDtypeStruct((L, DN_W), f32), jax.ShapeDtypeStruct((DN_HEADS, nc, 128, 128), f32)],
        scratch_shapes=[pltpu.VMEM((DN_HEADS, 128, 128), f32)],
        compiler_params=_cp("arbitrary"), name="dn_chunks_fwd",
    )(qkvn, qkvn, qkvn, gb, gb)


def dn_chunks_bwd(qkvn, gb, s_in, do):
    L = qkvn.shape[0]
    C = DN_CHUNK
    nc = L // C

    def body(q_ref, k_ref, v_ref, g_ref, b_ref, sin_ref, do_ref, dq_ref, dk_ref, dv_ref, dg_ref, db_ref, dS):
        n = pl.program_id(0)

        @pl.when(n == 0)
        def _():
            dS[...] = jnp.zeros_like(dS)

        args = (_heads(q_ref), _heads(k_ref), _heads(v_ref), _head_cols(g_ref), _head_cols(b_ref), sin_ref[...])
        _, vjp = jax.vjp(_dn_chunk, *args)
        dq, dk, dv, dg, db, ds = vjp((_heads(do_ref), dS[...]))
        lane = lax.broadcasted_iota(jnp.int32, (C, 128), 1)
        dg_all = jnp.zeros((C, 128), f32)
        db_all = jnp.zeros((C, 128), f32)
        for h in range(DN_HEADS):
            sl = slice(h * 128, (h + 1) * 128)
            dq_ref[:, sl] = dq[h]
            dk_ref[:, sl] = dk[h]
            dv_ref[:, sl] = dv[h]
            dg_all = dg_all + jnp.where(lane == h, dg[h], 0.0)
            db_all = db_all + jnp.where(lane == h, db[h], 0.0)
        dS[...] = ds
        dg_ref[...] = dg_all
        db_ref[...] = db_all

    rv = lambda n: nc - 1 - n
    blk = lambda j: pl.BlockSpec((C, DN_W), lambda n, j=j: (rv(n), j))
    gblk = lambda j: pl.BlockSpec((C, 128), lambda n, j=j: (rv(n), j))
    oblk = pl.BlockSpec((C, DN_W), lambda n: (rv(n), 0))
    gout = pl.BlockSpec((C, 128), lambda n: (rv(n), 0))
    return pl.pallas_call(
        body, grid=(nc,),
        in_specs=[blk(0), blk(1), blk(2), gblk(0), gblk(1),
                  pl.BlockSpec((DN_HEADS, None, 128, 128), lambda n: (0, rv(n), 0, 0)), oblk],
        out_specs=[oblk] * 3 + [gout] * 2,
        out_shape=[jax.ShapeDtypeStruct((L, DN_W), f32)] * 3 + [jax.ShapeDtypeStruct((L, 128), f32)] * 2,
        scratch_shapes=[pltpu.VMEM((DN_HEADS, 128, 128), f32)],
        compiler_params=_cp("arbitrary"), name="dn_chunks_bwd",
    )(qkvn, qkvn, qkvn, gb, gb, s_in, do)


def _dn_post(o, z, w):
    parts = []
    for h in range(DN_HEADS):
        oh = o[:, h * 128:(h + 1) * 128]
        r = lax.rsqrt(jnp.mean(oh * oh, axis=-1, keepdims=True) + EPS)
        parts.append(oh * r * w)
    return jnp.concatenate(parts, axis=1) * _silu(z)


def dn_post_fwd(o, hin, yc, onorm):
    L = o.shape[0]

    def fn(i, nb, ot, zt, yct, w):
        return jnp.concatenate([yct, _dn_post(ot, zt, w)], axis=1)

    return rowwise(fn, name="dn_post_fwd", L=L, tm=_pick(L, 256, SUBLANE),
                   rows=[(o, 0, DN_W, "cur"), (hin, 2304, DN_W, "cur"), (yc, 0, S5_WIDTH, "cur")],
                   consts=[onorm], outs=[(1024, bf16)])


def dn_post_bwd(o, hin, onorm, dycat):
    L = o.shape[0]

    def fn(i, nb, ot, zt, d0, d1, d2, w):
        dy = jnp.concatenate([d0, d1, d2], axis=1)
        sg = jax.nn.sigmoid(zt)
        sz = zt * sg
        dos, dw = [], jnp.zeros((1, 128), f32)
        nrm = []
        for h in range(DN_HEADS):
            sl = slice(h * 128, (h + 1) * 128)
            oh = ot[:, sl]
            r = lax.rsqrt(jnp.mean(oh * oh, axis=-1, keepdims=True) + EPS)
            ohat = oh * r
            t = dy[:, sl] * sz[:, sl]
            dw = dw + _colsum(t * ohat)
            t = t * w
            dos.append(r * (t - ohat * jnp.mean(t * ohat, axis=-1, keepdims=True)))
            nrm.append(ohat * w)
        dz = dy * jnp.concatenate(nrm, axis=1) * (sg * (1.0 + zt * (1.0 - sg)))
        return jnp.concatenate(dos, axis=1), dz, dw

    rows = [(o, 0, DN_W, "cur"), (hin, 2304, DN_W, "cur")] + [(dycat, 256 * (1 + j), 256, "cur") for j in range(3)]
    return rowwise(fn, name="dn_post_bwd", L=L, tm=_pick(L, 256, SUBLANE), rows=rows,
                   consts=[onorm], outs=[(DN_W, f32), (DN_W, f32)], sums=[(1, 128)])


def conv_bwd_win(xarr, start, C, w, dc, name):
    L = xarr.shape[0]
    W = w.shape[0]

    def fn(i, nb, xt, dct, p8, n8, wv):
        dx = _conv_causal_bwd_x(dct, n8 * (i < nb - 1).astype(f32), wv)
        dw = _conv_causal_bwd_w(dct, xt, p8 * (i > 0).astype(f32), W)
        return dx, dw

    return rowwise(fn, name=name, L=L, tm=_pick(L, 128, SUBLANE),
                   rows=[(xarr, start, C, "cur"), (dc, 0, C, "cur"), (xarr, start, C, "prev"), (dc, 0, C, "next")],
                   consts=[w], outs=[(C, bf16)], sums=[(SUBLANE, C)])


def rec_assemble(dx_qkv, dz, du1, du2, dab):
    L = dz.shape[0]

    def fn(i, nb, a, b, c, d, e):
        return jnp.concatenate([a.astype(f32), b, c + d, e], axis=1)

    return rowwise(fn, name="rec_assemble", L=L, tm=_pick(L, 256, SUBLANE),
                   rows=[(dx_qkv, 0, QKV_W, "cur"), (dz, 0, DN_W, "cur"), (du1, 0, 256, "cur"),
                         (du2, 0, 256, "cur"), (dab, 0, 256, "cur")], outs=[(REC_PAD, bf16)])


def deltanet_fwd(hin, prm, yc):
    qkvn, gb = dn_pre_fwd(hin, prm["conv"], prm["alog"], prm["dtb"])
    o, s_in = dn_chunks_fwd(qkvn, gb)
    ycat = dn_post_fwd(o, hin, yc, prm["onorm"])
    return ycat, (qkvn, gb, o, s_in)


def deltanet_bwd(hin, prm, saved, dycat):
    qkvn, gb, o, s_in = saved
    do, dz, donorm = dn_post_bwd(o, hin, prm["onorm"], dycat)
    dq, dk, dv, dgH, dbH = dn_chunks_bwd(qkvn, gb, s_in, do)
    dc, dab, dalog, ddtb = dn_pre_bwd(hin, prm["conv"], prm["alog"], prm["dtb"], (dq, dk, dv), dgH, dbH)
    dx_qkv, dconv = conv_bwd_win(hin, 0, QKV_W, prm["conv"], dc, "dn_conv_bwd")
    return dx_qkv, dz, dab, dict(conv=dconv[:DN_CONV], alog=dalog, dtb=ddtb, onorm=donorm)


AXES = ("x", "y", "c")


class _Coll:
    def __init__(self, x, axes, mode):
        self.axes, self.mode = axes, mode
        self.P = 2 ** len(axes)
        shape = x.shape if mode == "gather" else x.shape[1:]
        self.out_shape = jax.ShapeDtypeStruct((self.P,) + tuple(shape), x.dtype)
        self.scratch = [pltpu.SemaphoreType.DMA((self.P - 1,)), pltpu.SemaphoreType.DMA((self.P - 1,)),
                        pltpu.SemaphoreType.DMA]

    def _copies(self, x_ref, out_ref, send_sems, recv_sems, local_sem, with_recvs):
        axes, k = self.axes, len(self.axes)
        co = {a: lax.axis_index(a) for a in AXES}
        me = 0
        for a in axes:
            me = me * 2 + co[a]
        src = (lambda j: x_ref) if self.mode == "gather" else (lambda j: x_ref.at[j])
        local = pltpu.make_async_copy(src(me), out_ref.at[me], local_sem)
        sends, recvs = [], []
        for m in range(1, self.P):
            tco = dict(co)
            t = 0
            for i, a in enumerate(axes):
                if (m >> (k - 1 - i)) & 1:
                    tco[a] = 1 - co[a]
                t = t * 2 + tco[a]
            dev = tuple(tco[a] for a in AXES)
            mk = functools.partial(pltpu.make_async_remote_copy, src_ref=src(t), send_sem=send_sems.at[m - 1],
                                   recv_sem=recv_sems.at[m - 1], device_id=dev, device_id_type=MESH)
            sends.append(mk(dst_ref=out_ref.at[me]))
            if with_recvs:
                recvs.append(mk(dst_ref=out_ref.at[t]))
        return local, sends, recvs

    def start(self, *refs):
        local, sends, _ = self._copies(*refs, with_recvs=False)
        local.start()
        for cp in sends:
            cp.start()

    def wait(self, *refs):
        local, sends, recvs = self._copies(*refs, with_recvs=True)
        for cp in recvs:
            cp.wait_recv()
        for cp in sends:
            cp.wait_send()
        local.wait()


def _collective(x, axes, mode, name):
    coll = _Coll(x, axes, mode)

    def body(*refs):
        coll.start(*refs)
        coll.wait(*refs)

    return pl.pallas_call(
        body, in_specs=[pl.BlockSpec(memory_space=pl.ANY)], out_specs=pl.BlockSpec(memory_space=pl.ANY),
        out_shape=coll.out_shape, scratch_shapes=coll.scratch, name=name,
    )(x)


def all_gather(x, axes, name):
    return _collective(x, axes, "gather", name)


def exchange(x, axes, name):
    return _collective(x, axes, "exchange", name)


def sum_slots(x, name, out_dtype=f32):
    P, R, C = x.shape
    tr = _pick(R, 256, 2 * SUBLANE)

    def body(x_ref, o_ref):
        acc = x_ref[0].astype(f32)
        for j in range(1, P):
            acc = acc + x_ref[j].astype(f32)
        o_ref[...] = acc.astype(o_ref.dtype)

    return pl.pallas_call(
        body, grid=(R // tr,), in_specs=[pl.BlockSpec((P, tr, C), lambda i: (0, i, 0))],
        out_specs=pl.BlockSpec((tr, C), lambda i: (i, 0)), out_shape=jax.ShapeDtypeStruct((R, C), out_dtype),
        compiler_params=_cp("parallel"), name=name,
    )(x)


def _pack(arrs, width, row_mult, dtype):
    flat = jnp.concatenate([a.astype(dtype).reshape(-1) for a in arrs])
    unit = width * row_mult
    n = -(-flat.shape[0] // unit) * unit
    return jnp.pad(flat, (0, n - flat.shape[0])).reshape(n // width, width)


def _unpack(flat, shapes):
    flat = flat.reshape(-1)
    out, off = [], 0
    for s in shapes:
        n = int(np.prod(s))
        out.append(flat[off:off + n].reshape(s))
        off += n
    return out


def ada_fwd(c_all, ada_w):
    def body(c_ref, w_ref, o_ref):
        cond = _silu(c_ref[...])
        for l in range(ada_w.shape[0]):
            o_ref[l] = _dot(cond, w_ref[l], precision=HI)

    return pl.pallas_call(body, out_shape=jax.ShapeDtypeStruct((ada_w.shape[0], c_all.shape[0], ada_w.shape[2]), f32),
                          compiler_params=pltpu.CompilerParams(vmem_limit_bytes=VMEM_LIMIT), name="ada_fwd")(c_all, ada_w)


def ada_bwd(c_all, dmod):
    def body(c_ref, d_ref, o_ref):
        cond = _silu(c_ref[...])
        for l in range(dmod.shape[0]):
            o_ref[l] = _dot(cond, d_ref[l], TN, precision=HI)

    return pl.pallas_call(body, out_shape=jax.ShapeDtypeStruct((dmod.shape[0], c_all.shape[1], dmod.shape[2]), f32),
                          compiler_params=pltpu.CompilerParams(vmem_limit_bytes=VMEM_LIMIT), name="ada_bwd")(c_all, dmod)


def loss_fwd_bwd(y, target):
    L, D = y.shape

    def fn(i, nb, yt, tt):
        e = yt - tt
        return e * (1.0 / D), jnp.sum(jnp.sum(e * e, axis=1, keepdims=True), axis=0, keepdims=True)

    return rowwise(fn, name="loss", L=L, tm=_pick(L, 512, SUBLANE), rows=[(y, 0, D, "cur"), (target, 0, D, "cur")],
                   outs=[(D, f32)], sums=[(1, 1)])


def adamw(w, g, m, v, name):
    R, C = w.shape

    def fn(i, nb, wt, gt, mt, vt):
        m2 = ADAM_B1 * mt + (1.0 - ADAM_B1) * gt
        v2 = ADAM_B2 * vt + (1.0 - ADAM_B2) * (gt * gt)
        m_hat = m2 / (1.0 - ADAM_B1 ** ADAM_STEP)
        v_hat = v2 / (1.0 - ADAM_B2 ** ADAM_STEP)
        delta = -ADAM_LR * (m_hat / (jnp.sqrt(v_hat) + ADAM_EPS) + ADAM_WD * wt)
        return delta, m2, v2

    return rowwise(fn, name=name, L=R, tm=_pick(R, 256, SUBLANE), rows=[(a, 0, C, "cur") for a in (w, g, m, v)],
                   outs=[(C, f32)] * 3)


W_NAMES = ["ada_w", "ada_b", "norm_mix", "norm_ffn", "attn_w_in", "attn_q_norm_a", "attn_k_norm_a", "attn_q_norm_b",
           "attn_k_norm_b", "attn_sinks", "attn_w_out", "rec_w_in", "s5_lambda_re", "s5_lambda_im", "s5_log_dt",
           "s5_b_re", "s5_b_im", "s5_c_re", "s5_c_im", "s5_d", "s5_glu_w", "s5_glu_b", "dn_conv", "dn_a_log",
           "dn_dt_bias", "dn_out_norm", "rec_w_out", "ffn_w_up", "ffn_conv", "ffn_w_down"]
BIG = ["attn_w_in", "attn_w_out", "rec_w_in", "rec_w_out", "ffn_w_up", "ffn_w_down"]
SMALL_SHARDED = ["s5_d", "s5_glu_w", "s5_glu_b", "dn_conv", "ffn_conv"]
SMALL_REPL = [n for n in W_NAMES if n not in BIG and n not in SMALL_SHARDED and n != "ada_w"]
NSH = 4
GRAD_WIRE = (bf16,)


SHARD_AXIS = {"attn_w_in": 2, "attn_w_out": 1, "rec_w_in": 2, "rec_w_out": 1, "ffn_w_up": 2, "ffn_w_down": 1,
              "s5_d": 1, "s5_glu_w": 1, "s5_glu_b": 1, "dn_conv": 2, "ffn_conv": 2}


def _unshard(g, name):
    ax = SHARD_AXIS[name.rstrip("01")]
    g = jnp.moveaxis(g, 0, ax)
    s = g.shape
    return g.reshape(s[:ax] + (s[ax] * s[ax + 1],) + s[ax + 2:])


def _to_shards(full, name):
    ax = SHARD_AXIS[name.rstrip("01")]
    s = full.shape
    g = full.reshape(s[:ax] + (NSH, s[ax] // NSH) + s[ax + 1:])
    return jnp.moveaxis(g, ax, 0)


def _rec_pad_cols(w):
    z6 = jnp.zeros(w.shape[:-1] + (122,), w.dtype)
    return jnp.concatenate([w[..., 256:3328], w[..., 0:256], w[..., 3328:3334], z6, w[..., 3334:3340], z6], axis=-1)


def _rec_unpad_cols(g):
    return jnp.concatenate([g[..., 3072:3328], g[..., 0:3072], g[..., 3328:3334], g[..., 3456:3462]], axis=-1)


def _ffn_fwd(x1, nf, sc, sh, gate, w_up, conv, w_dn, tag, rides=()):
    rides = list(rides) + [None, None]
    h2 = modulate_fwd(x1, nf, sc, sh, f"{tag}_mod2_fwd")
    up = mm(h2, w_up, name=f"{tag}_ffn_up", out_dtypes=(bf16,), ride=rides[0])
    up, got0 = up if rides[0] else (up, None)
    act = ffn_act_fwd(up, conv, f"{tag}_ffn_act_fwd")
    res = mm(act, w_dn, name=f"{tag}_ffn_down", out_dtypes=(f32, f32), epi=_resid_epi, epi_mn=[x1], epi_n=[gate],
             ride=rides[1])
    return res[1], (h2, up, act, res[0]), (got0, res[2] if rides[1] else None)


def _ffn_bwd(dx, x1, nf, sc, sh, gate, w_up, conv, w_dn, saved, tag, rides=()):
    rides = list(rides) + [None, None, None]
    take = lambda res, r: res if r else (res, None)
    h2, up, act, f = saved
    df, dgate = resid_bwd(dx, f, gate, f"{tag}_res2_bwd")
    dact = mm(df, w_dn, tb=True, name=f"{tag}_ffn_dact", out_dtypes=(bf16,))
    dw_dn, got0 = take(mm(act, df, ta=True, name=f"{tag}_ffn_dwdown", out_dtypes=GRAD_WIRE, ride=rides[0]), rides[0])
    dup, dconv = ffn_act_conv_bwd(up, conv, dact, f"{tag}_ffn_act_conv_bwd")
    dw_up, got1 = take(mm(h2, dup, ta=True, name=f"{tag}_ffn_dwup", out_dtypes=GRAD_WIRE, ride=rides[1]), rides[1])
    if callable(rides[2]):
        rides[2] = rides[2](dw_dn)
    dh2, got2 = take(mm(dup, w_up, tb=True, name=f"{tag}_ffn_dh", ride=rides[2]), rides[2])
    dx, dnf, dsc, dsh = modulate_bwd(x1, nf, sc, sh, dh2, dx, f"{tag}_mod2_bwd")
    grads = dict(nf=dnf, sc=dsc, sh=dsh, gate=dgate, w_up=dw_up, conv=dconv[:FFN_CONV], w_dn=dw_dn)
    return dx, grads, (got0, got1, got2)


def kernel(x, c, ada_w, ada_b, norm_mix, norm_ffn, attn_w_in, attn_q_norm_a, attn_k_norm_a, attn_q_norm_b, attn_k_norm_b, attn_sinks, attn_w_out, rec_w_in, s5_lambda_re, s5_lambda_im, s5_log_dt, s5_b_re, s5_b_im, s5_c_re, s5_c_im, s5_d, s5_glu_w, s5_glu_b, dn_conv, dn_a_log, dn_dt_bias, dn_out_norm, rec_w_out, ffn_w_up, ffn_conv, ffn_w_down, loss_target, m_ada_w, m_ada_b, m_norm_mix, m_norm_ffn, m_attn_w_in, m_attn_q_norm_a, m_attn_k_norm_a, m_attn_q_norm_b, m_attn_k_norm_b, m_attn_sinks, m_attn_w_out, m_rec_w_in, m_s5_lambda_re, m_s5_lambda_im, m_s5_log_dt, m_s5_b_re, m_s5_b_im, m_s5_c_re, m_s5_c_im, m_s5_d, m_s5_glu_w, m_s5_glu_b, m_dn_conv, m_dn_a_log, m_dn_dt_bias, m_dn_out_norm, m_rec_w_out, m_ffn_w_up, m_ffn_conv, m_ffn_w_down, v_ada_w, v_ada_b, v_norm_mix, v_norm_ffn, v_attn_w_in, v_attn_q_norm_a, v_attn_k_norm_a, v_attn_q_norm_b, v_attn_k_norm_b, v_attn_sinks, v_attn_w_out, v_rec_w_in, v_s5_lambda_re, v_s5_lambda_im, v_s5_log_dt, v_s5_b_re, v_s5_b_im, v_s5_c_re, v_s5_c_im, v_s5_d, v_s5_glu_w, v_s5_glu_b, v_dn_conv, v_dn_a_log, v_dn_dt_bias, v_dn_out_norm, v_rec_w_out, v_ffn_w_up, v_ffn_conv, v_ffn_w_down):
    args = (ada_w, ada_b, norm_mix, norm_ffn, attn_w_in, attn_q_norm_a, attn_k_norm_a, attn_q_norm_b, attn_k_norm_b, attn_sinks, attn_w_out, rec_w_in, s5_lambda_re, s5_lambda_im, s5_log_dt, s5_b_re, s5_b_im, s5_c_re, s5_c_im, s5_d, s5_glu_w, s5_glu_b, dn_conv, dn_a_log, dn_dt_bias, dn_out_norm, rec_w_out, ffn_w_up, ffn_conv, ffn_w_down)
    ms = (m_ada_w, m_ada_b, m_norm_mix, m_norm_ffn, m_attn_w_in, m_attn_q_norm_a, m_attn_k_norm_a, m_attn_q_norm_b, m_attn_k_norm_b, m_attn_sinks, m_attn_w_out, m_rec_w_in, m_s5_lambda_re, m_s5_lambda_im, m_s5_log_dt, m_s5_b_re, m_s5_b_im, m_s5_c_re, m_s5_c_im, m_s5_d, m_s5_glu_w, m_s5_glu_b, m_dn_conv, m_dn_a_log, m_dn_dt_bias, m_dn_out_norm, m_rec_w_out, m_ffn_w_up, m_ffn_conv, m_ffn_w_down)
    vs = (v_ada_w, v_ada_b, v_norm_mix, v_norm_ffn, v_attn_w_in, v_attn_q_norm_a, v_attn_k_norm_a, v_attn_q_norm_b, v_attn_k_norm_b, v_attn_sinks, v_attn_w_out, v_rec_w_in, v_s5_lambda_re, v_s5_lambda_im, v_s5_log_dt, v_s5_b_re, v_s5_b_im, v_s5_c_re, v_s5_c_im, v_s5_d, v_s5_glu_w, v_s5_glu_b, v_dn_conv, v_dn_a_log, v_dn_dt_bias, v_dn_out_norm, v_rec_w_out, v_ffn_w_up, v_ffn_conv, v_ffn_w_down)
    W = dict(zip(W_NAMES, args))
    Mo = dict(zip(W_NAMES, ms))
    Vo = dict(zip(W_NAMES, vs))
    xi, yi, ci = lax.axis_index("x"), lax.axis_index("y"), lax.axis_index("c")
    shard = 2 * xi + yi
    me8 = 4 * xi + 2 * yi + ci
    xs = x[0]
    target = loss_target[0]
    L, D = xs.shape

    XY = ("x", "y")
    wparts = [
        [("attn_w_in", attn_w_in), ("attn_w_out", attn_w_out)],
        [("rec_w_in", rec_w_in), ("rec_w_out", rec_w_out)],
        [("ffn_w_up1", ffn_w_up[1:2]), ("ffn_w_down1", ffn_w_down[1:2])],
        [("ffn_w_up0", ffn_w_up[0:1]), ("ffn_w_down0", ffn_w_down[0:1])],
    ]
    wpack = [_pack([a for _, a in p], 1024, 16, bf16) for p in wparts]
    Wf = {}

    def unpack_weights(gathered, part):
        flat = gathered.reshape(NSH, -1)
        off = 0
        for n, a in part:
            sz = int(np.prod(a.shape))
            Wf[n] = _unshard(flat[:, off:off + sz].reshape((NSH,) + a.shape), n)[0]
            off += sz

    unpack_weights(all_gather(wpack[0], XY, "gather_w0"), wparts[0])

    sflat = _pack([c] + [W[n] for n in SMALL_SHARDED], 1024, 8, f32)
    s8 = all_gather(sflat, AXES, "gather_small")
    s8f = s8.reshape(8, -1)
    c_all = s8f[:, :D]
    Ws = {}
    off = D
    for n in SMALL_SHARDED:
        sz = int(np.prod(W[n].shape))
        Ws[n] = _unshard(s8f[0::2, off:off + sz].reshape((NSH,) + W[n].shape), n)
        off += sz

    modp = ada_fwd(c_all, ada_w)
    modg = all_gather(modp, ("x", "y"), "gather_mod")
    mod_all = jnp.moveaxis(modg, 0, 2).reshape(2, 8, -1) + ada_b[:, None, :]
    mod = lax.dynamic_slice(mod_all, (0, me8, 0), (2, 1, mod_all.shape[2]))[:, 0, :]
    mods = [[mod[l:l + 1, j * D:(j + 1) * D] for j in range(6)] for l in range(2)]

    sh1, sc1, g1, sh2, sc2, g2_ = mods[0]
    nm0, nf0 = norm_mix[0:1], norm_ffn[0:1]
    sinkb = jnp.repeat(attn_sinks[0], HEAD_DIM)[None]
    h0 = modulate_fwd(xs, nm0, sc1, sh1, "l0_mod1_fwd")
    hin0 = mm(h0, Wf["attn_w_in"], name="l0_in_proj")
    ocat, att_saved, got = attention_fwd(hin0, attn_q_norm_a, attn_k_norm_a, attn_q_norm_b, attn_k_norm_b, sinkb,
                                         ride=(wpack[3], XY, "gather"))
    unpack_weights(got, wparts[3])
    y0, x1 = mm(ocat, Wf["attn_w_out"], name="l0_out_proj", out_dtypes=(f32, f32), epi=_resid_epi,
                epi_mn=[xs], epi_n=[g1])
    x2, ffn0_saved, got = _ffn_fwd(x1, nf0, sc2, sh2, g2_, Wf["ffn_w_up0"], Ws["ffn_conv"][0], Wf["ffn_w_down0"], "l0",
                                   rides=[(wpack[1], XY, "gather"), (wpack[2], XY, "gather")])
    unpack_weights(got[0], wparts[1])
    unpack_weights(got[1], wparts[2])
    rec_w_in_p = _rec_pad_cols(Wf["rec_w_in"])

    th1, tc1, t1, th2, tc2, t2 = mods[1]
    nm1, nf1 = norm_mix[1:2], norm_ffn[1:2]
    pad128 = lambda a: jnp.pad(a, ((0, 0), (0, 128 - a.shape[1])))
    s5p = dict(lr=s5_lambda_re[0], li=s5_lambda_im[0], ldt=s5_log_dt[0][:, None], b_re=s5_b_re[0], b_im=s5_b_im[0],
               c_re=s5_c_re[0], c_im=s5_c_im[0], d=Ws["s5_d"], gw=Ws["s5_glu_w"][0], gb=Ws["s5_glu_b"])
    dnp = dict(conv=Ws["dn_conv"][0], alog=pad128(dn_a_log), dtb=pad128(dn_dt_bias), onorm=dn_out_norm)
    h1 = modulate_fwd(x2, nm1, tc1, th1, "l1_mod1_fwd")
    hin1 = mm(h1, rec_w_in_p, name="l1_in_proj")
    yc, s5_saved = s5_fwd(hin1, s5p)
    ycat, dn_saved = deltanet_fwd(hin1, dnp, yc)
    y1, x3 = mm(ycat, Wf["rec_w_out"], name="l1_out_proj", out_dtypes=(f32, f32), epi=_resid_epi,
                epi_mn=[x2], epi_n=[t1])
    x4, ffn1_saved, _ = _ffn_fwd(x3, nf1, tc2, th2, t2, Wf["ffn_w_up1"], Ws["ffn_conv"][1], Wf["ffn_w_down1"], "l1")

    dx, sse = loss_fwd_bwd(x4, target)
    loss = lax.psum(0.5 * sse[0, 0] / D, AXES)

    dx, gf1, _ = _ffn_bwd(dx, x3, nf1, tc2, th2, t2, Wf["ffn_w_up1"], Ws["ffn_conv"][1], Wf["ffn_w_down1"], ffn1_saved, "l1")
    dy1, dt1 = resid_bwd(dx, y1, t1, "l1_res1_bwd")
    dycat = mm(dy1, Wf["rec_w_out"], tb=True, name="l1_dycat")
    dw_rec_out = mm(ycat, dy1, ta=True, name="l1_dwout", out_dtypes=GRAD_WIRE)
    du_skip, du_b, s5g = s5_bwd(hin1, s5p, s5_saved, dycat)
    dx_qkv, dz, dab, dng = deltanet_bwd(hin1, dnp, dn_saved, dycat)
    dhin1 = rec_assemble(dx_qkv, dz, du_skip, du_b, dab)
    dw_rec_in = _rec_unpad_cols(mm(h1, dhin1, ta=True, name="l1_dwin", out_dtypes=GRAD_WIRE))
    dh1 = mm(dhin1, rec_w_in_p, tb=True, name="l1_dh")
    dx, dnm1, dtc1, dth1 = modulate_bwd(x2, nm1, tc1, th1, dh1, dx, "l1_mod1_bwd")

    def grad_part(items):
        flat = jnp.concatenate([_to_shards(g, n).reshape(NSH, -1) for n, g in items], axis=1)
        unit = 256 * 1024
        npad = -(-flat.shape[1] // unit) * unit
        return jnp.pad(flat, ((0, 0), (0, npad - flat.shape[1]))).reshape(NSH, npad // 1024, 1024)

    w_dn1 = gf1["w_dn"][None]
    part2 = lambda dw_dn0: (grad_part([("ffn_w_down1", w_dn1), ("ffn_w_down0", dw_dn0[None])]), XY, "exchange")
    gparts = [[("rec_w_in", dw_rec_in[None]), ("rec_w_out", dw_rec_out[None])], [("ffn_w_up1", gf1["w_up"][None])]]
    dx, gf0, gq = _ffn_bwd(dx, x1, nf0, sc2, sh2, g2_, Wf["ffn_w_up0"], Ws["ffn_conv"][0], Wf["ffn_w_down0"], ffn0_saved,
                           "l0", rides=[(grad_part(gparts[0]), XY, "exchange"), (grad_part(gparts[1]), XY, "exchange"), part2])
    gparts.append([("ffn_w_down1", w_dn1), ("ffn_w_down0", gf0["w_dn"][None])])
    dy0, dg1 = resid_bwd(dx, y0, g1, "l0_res1_bwd")
    dcat = mm(dy0, Wf["attn_w_out"], tb=True, name="l0_dcat")
    dw_attn_out = mm(ocat, dy0, ta=True, name="l0_dwout", out_dtypes=GRAD_WIRE)
    gparts += [[("ffn_w_up0", gf0["w_up"][None])], [("attn_w_out", dw_attn_out[None])]]
    chip_sum = lambda qs, i0: jnp.concatenate([sum_slots(q, f"sum_chips{i0 + i}", bf16) for i, q in enumerate(qs)], axis=0)
    dhin0, dwqa, dwka, dwqb, dwkb, dsinkb, gots = attention_bwd(
        hin0, attn_q_norm_a, attn_k_norm_a, attn_q_norm_b, attn_k_norm_b, sinkb, att_saved, dcat,
        rides={"a": (grad_part(gparts[3]), XY, "exchange"), 1: (chip_sum(gq, 0), ("c",), "gather"),
               4: (grad_part(gparts[4]), XY, "exchange"), 16: lambda g: (chip_sum([g["a"]], 3), ("c",), "gather")})
    dw_attn_in = mm(h0, dhin0, ta=True, name="l0_dwin", out_dtypes=GRAD_WIRE)
    gparts.append([("attn_w_in", dw_attn_in[None])])
    dh0, gq5 = mm(dhin0, Wf["attn_w_in"], tb=True, name="l0_dh", ride=(grad_part(gparts[5]), XY, "exchange"))
    grad_x, dnm0, dsc1, dsh1 = modulate_bwd(xs, nm0, sc1, sh1, dh0, dx, "l0_mod1_bwd")

    dmod = jnp.concatenate([
        jnp.concatenate([dsh1, dsc1, dg1, gf0["sh"], gf0["sc"], gf0["gate"]], axis=1),
        jnp.concatenate([dth1, dtc1, dt1, gf1["sh"], gf1["sc"], gf1["gate"]], axis=1)], axis=0)
    gl = {
        "ada_b": dmod,
        "norm_mix": jnp.concatenate([dnm0, dnm1], axis=0),
        "norm_ffn": jnp.concatenate([gf0["nf"], gf1["nf"]], axis=0),
        "attn_q_norm_a": dwqa, "attn_k_norm_a": dwka, "attn_q_norm_b": dwqb, "attn_k_norm_b": dwkb,
        "attn_sinks": dsinkb[:, ::HEAD_DIM],
        "s5_lambda_re": s5g["lr"][None], "s5_lambda_im": s5g["li"][None], "s5_log_dt": s5g["ldt"][:, 0][None],
        "s5_b_re": s5g["b_re"][None], "s5_b_im": s5g["b_im"][None], "s5_c_re": s5g["c_re"][None],
        "s5_c_im": s5g["c_im"][None],
        "dn_a_log": dng["alog"][:, :DN_HEADS], "dn_dt_bias": dng["dtb"][:, :DN_HEADS], "dn_out_norm": dng["onorm"],
        "s5_d": s5g["d"], "s5_glu_w": s5g["gw"][None], "s5_glu_b": s5g["gb"], "dn_conv": dng["conv"][None],
        "ffn_conv": jnp.stack([gf0["conv"], gf1["conv"]]),
    }

    small_names = SMALL_REPL + SMALL_SHARDED
    gs = _pack([gl[n] for n in small_names], 128, 256, f32)
    gs8 = all_gather(gs, AXES, "gather_small_grads")
    gsum = sum_slots(gs8, "sum_small_grads")
    full_shapes = [gl[n].shape for n in small_names]
    gfull = dict(zip(small_names, _unpack(gsum, full_shapes)))
    dmod_all = gs8.reshape(8, -1)[:, :2 * 6 * D].reshape(8, 2, 6 * D)
    ncol = ada_w.shape[2]
    dmod_sh = jnp.moveaxis(lax.dynamic_slice(dmod_all, (0, 0, shard * ncol), (8, 2, ncol)), 0, 1)
    grads = {"ada_w": ada_bwd(c_all, dmod_sh)}
    for n in SMALL_REPL:
        grads[n] = gfull[n]
    for n in SMALL_SHARDED:
        sh_all = _to_shards(gfull[n], n)
        grads[n] = lax.dynamic_slice(sh_all, (shard,) + (0,) * (sh_all.ndim - 1), (1,) + sh_all.shape[1:])[0]

    gq = list(gq) + [gots["a"], gots[4], gq5]
    gc45 = all_gather(chip_sum(gq[4:], 4), ("c",), "gather_grad_c")
    gsh = jnp.concatenate([sum_slots(gots[1], "sum_pair012"), sum_slots(gots[16], "sum_pair3"),
                           sum_slots(gc45, "sum_pair45")], axis=0)
    row, got = 0, {}
    for part, q in zip(gparts, gq):
        flat = gsh[row:row + q.shape[1]].reshape(-1)
        row += q.shape[1]
        off = 0
        for n, g in part:
            sz = g.size // NSH
            got[n] = flat[off:off + sz].reshape((1,) + g.shape[1:-2] + _to_shards(g, n).shape[-2:])
            off += sz
    for n in ("attn_w_in", "attn_w_out", "rec_w_in", "rec_w_out"):
        grads[n] = got[n]
    grads["ffn_w_up"] = jnp.concatenate([got["ffn_w_up0"], got["ffn_w_up1"]], axis=0)
    grads["ffn_w_down"] = jnp.concatenate([got["ffn_w_down0"], got["ffn_w_down1"]], axis=0)

    delta, new_m, new_v = {}, {}, {}

    def as2d(a):
        return a.reshape(-1, a.shape[-1])

    for n in ["ada_w"] + BIG:
        d_, m_, v_ = adamw(as2d(W[n]), as2d(grads[n]), as2d(Mo[n]), as2d(Vo[n]), f"adamw_{n}")
        delta[n], new_m[n], new_v[n] = d_.reshape(W[n].shape), m_.reshape(W[n].shape), v_.reshape(W[n].shape)
    pk = lambda dd: _pack([dd[n] for n in small_names], 128, 256, f32)
    d_, m_, v_ = adamw(pk(W), pk(grads), pk(Mo), pk(Vo), "adamw_small")
    shp = [W[n].shape for n in small_names]
    for dst, src in ((delta, d_), (new_m, m_), (new_v, v_)):
        dst.update(zip(small_names, _unpack(src, shp)))

    return (loss, grad_x[None], *[grads[n] for n in W_NAMES], *[delta[n] for n in W_NAMES],
            *[new_m[n] for n in W_NAMES], *[new_v[n] for n in W_NAMES])
```

```python
import functools
import math

import numpy as np
import jax
import jax.numpy as jnp
from jax import lax
from jax.experimental import pallas as pl
from jax.experimental.pallas import tpu as pltpu

f32 = jnp.float32
bf16 = jnp.bfloat16
HI = lax.Precision.HIGHEST
MESH = pl.DeviceIdType.MESH

HEAD_DIM = 64
BLOCK = 128
A_Q_HEADS = 8
A_KV_HEADS = 2
A_WINDOW = 128
B_HEADS = 8
B_BRANCHES = ((128, 1), (512, 4), (2048, 16))
N_ATTN_HEADS = 16
ATTN_IN = 2304
S5_GROUP = 16
S5_GROUPS = 16
S5_WIDTH = 256
S5_STATE = 64
DN_HEADS = 6
DN_DK = 128
DN_CONV = 4
DN_CHUNK = 64
REC_IN = 3340
REC_PAD = 3584
FFN_CONV = 3
EPS = 1e-6
ADAM_LR = 0.001
ADAM_B1 = 0.9
ADAM_B2 = 0.999
ADAM_EPS = 1e-08
ADAM_WD = 0.01
ADAM_STEP = 10

LANE = 128
SUBLANE = 8
VMEM_LIMIT = 52 * 1024 * 1024
MM_FULL_K = 5632
MM_VMEM_BUDGET = 40 * 1024 * 1024


def _cp(*sem):
    return pltpu.CompilerParams(dimension_semantics=sem, vmem_limit_bytes=VMEM_LIMIT)


def _pick(dim, cap, unit=LANE):
    for t in (2048, 1024, 768, 512, 384, 256, 128, 64, 32, 16, 8):
        if t <= cap and t % unit == 0 and dim % t == 0:
            return t
    return dim


def _dot(a, b, dims=(((1,), (0,)), ((), ())), precision=None):
    return lax.dot_general(a, b, dims, precision=precision, preferred_element_type=f32)


NN = (((1,), (0,)), ((), ()))
NT = (((1,), (1,)), ((), ()))
TN = (((0,), (0,)), ((), ()))


def mm(a, b, *, name, ta=False, tb=False, a_win=None, b_win=None, out_dtypes=(f32,),
       epi=None, epi_mn=(), epi_n=(), tm_cap=1024, tn_cap=8192, tk_cap=None, ride=None):
    coll = _Coll(*ride) if ride else None
    a0, a1 = a.shape
    b0, b1 = b.shape
    aw = a_win or (0, a1)
    bw = b_win or (0, b1)
    if ta:
        K, M = a0, aw[1]
    else:
        M, K = a0, aw[1]
    if tb:
        N, K2 = b0, bw[1]
    else:
        K2, N = b0, bw[1]
    assert K == K2, (a.shape, b.shape, ta, tb, a_win, b_win)
    if tk_cap is None:
        tk_cap = K if K <= MM_FULL_K else 2048
    tk = _pick(K, tk_cap, SUBLANE if (ta and not tb) else LANE)
    nk = K // tk
    sa, sb = a.dtype.itemsize, b.dtype.itemsize
    so = sum(jnp.dtype(d).itemsize for d in out_dtypes)
    n_mn, n_n, n_out = len(epi_mn), len(epi_n), len(out_dtypes)

    def vmem(tm_, tn_):
        return 2 * (tm_ * tk * sa + tk * tn_ * sb + tm_ * tn_ * (so + 4 * n_mn)) + 2 * tm_ * tn_ * 4

    best = None
    for tm_ in (t for t in (1024, 512, 256, 128) if M % t == 0 and (not ta or aw[0] % t == 0)):
        for tn_ in (t for t in (N, N // 2, 1024, 768, 512, 384, 256, 128)
                    if t % LANE == 0 and N % t == 0 and (tb or bw[0] % t == 0)):
            if tm_ <= tm_cap and tn_ <= max(tn_cap, 0) and vmem(tm_, tn_) <= MM_VMEM_BUDGET:
                if best is None or (tm_ * tn_, tn_) > (best[0] * best[1], best[1]):
                    best = (tm_, tn_)
    assert best is not None, (name, M, N, K)
    tm, tn = best
    b_outer = tk * tn * sb > tm * tk * sa

    def ix(f):
        if b_outer:
            return lambda j, i, k: f(i, j, k)
        return f

    if ta:
        mo = aw[0] // tm
        a_spec = pl.BlockSpec((tk, tm), ix(lambda i, j, k: (k, i + mo)))
    else:
        assert aw[0] % tk == 0
        ko = aw[0] // tk
        a_spec = pl.BlockSpec((tm, tk), ix(lambda i, j, k: (i, k + ko)))
    if tb:
        assert bw[0] % tk == 0
        kob = bw[0] // tk
        b_spec = pl.BlockSpec((tn, tk), ix(lambda i, j, k: (j, k + kob)))
    else:
        no = bw[0] // tn
        b_spec = pl.BlockSpec((tk, tn), ix(lambda i, j, k: (k, j + no)))
    dims = (((0 if ta else 1,), (1 if tb else 0,)), ((), ()))

    gi, gj = M // tm, N // tn
    grid = (gj, gi, nk) if b_outer else (gi, gj, nk)

    def body(a_ref, b_ref, *rest):
        mn_refs = rest[:n_mn]
        n_refs = rest[n_mn:n_mn + n_n]
        o0 = n_mn + n_n
        out_refs = rest[o0:o0 + n_out]

        def finish(r):
            if epi is None:
                outs = (r,)
            else:
                outs = epi(r, *[m[...] for m in mn_refs], *[v[...] for v in n_refs])
            for o_ref, o in zip(out_refs, outs):
                o_ref[...] = o.astype(o_ref.dtype)

        part = _dot(a_ref[...].astype(bf16), b_ref[...].astype(bf16), dims)
        if nk == 1:
            finish(part)
        else:
            acc = rest[o0 + n_out]
            k = pl.program_id(2)

            @pl.when(k == 0)
            def _():
                acc[...] = part

            @pl.when(k > 0)
            def _():
                acc[...] += part

            @pl.when(k == nk - 1)
            def _():
                finish(acc[...])

    mn_spec = pl.BlockSpec((tm, tn), ix(lambda i, j, k: (i, j)))
    n_spec = pl.BlockSpec((1, tn), ix(lambda i, j, k: (0, j)))
    outs = _ride_call(
        body, coll, ride, grid=grid,
        in_specs=[a_spec, b_spec] + [mn_spec] * n_mn + [n_spec] * n_n, out_specs=[mn_spec] * n_out,
        out_shape=[jax.ShapeDtypeStruct((M, N), d) for d in out_dtypes],
        scratch_shapes=[pltpu.VMEM((tm, tn), f32)] if nk > 1 else [],
        semantics=("parallel", "parallel", "arbitrary"), name=name, args=[a, b, *epi_mn, *epi_n])
    return outs[0] if len(outs) == 1 else tuple(outs)


def rowwise(fn, *, name, L, tm, rows=(), consts=(), outs=(), sums=()):
    nb = L // tm
    in_specs = []
    arrs = []
    for arr, start, width, kind in rows:
        assert start % width == 0, (name, start, width)
        co = start // width
        hr = SUBLANE * (4 // arr.dtype.itemsize)
        hb = tm // hr
        if kind == "cur":
            in_specs.append(pl.BlockSpec((tm, width), lambda i, co=co: (i, co)))
        elif kind == "prev":
            in_specs.append(pl.BlockSpec((hr, width), lambda i, co=co, hb=hb: (jnp.maximum(i * hb - 1, 0), co)))
        else:
            last = L // hr - 1
            in_specs.append(pl.BlockSpec((hr, width), lambda i, co=co, hb=hb, last=last:
                                         (jnp.minimum((i + 1) * hb, last), co)))
        arrs.append(arr)
    for cst in consts:
        assert cst.ndim == 2
        in_specs.append(pl.BlockSpec(cst.shape, lambda i: (0, 0)))
        arrs.append(cst)
    n_rows, n_c, n_o, n_s = len(rows), len(consts), len(outs), len(sums)
    out_specs = [pl.BlockSpec((tm, w), lambda i: (i, 0)) for w, _ in outs]
    out_specs += [pl.BlockSpec(s, lambda i: (0, 0)) for s in sums]
    out_shape = [jax.ShapeDtypeStruct((L, w), d) for w, d in outs]
    out_shape += [jax.ShapeDtypeStruct(s, f32) for s in sums]

    def body(*refs):
        i = pl.program_id(0)
        vals = [r[...] for r in refs[:n_rows + n_c]]
        res = fn(i, nb, *vals)
        if not isinstance(res, (tuple, list)):
            res = (res,)
        o_refs = refs[n_rows + n_c:n_rows + n_c + n_o]
        s_refs = refs[n_rows + n_c + n_o:]
        for o_ref, o in zip(o_refs, res[:n_o]):
            o_ref[...] = o.astype(o_ref.dtype)
        if n_s:
            @pl.when(i == 0)
            def _():
                for s_ref in s_refs:
                    s_ref[...] = jnp.zeros_like(s_ref)

            for s_ref, s in zip(s_refs, res[n_o:]):
                s_ref[...] += s

    res = pl.pallas_call(
        body,
        grid=(nb,),
        in_specs=in_specs,
        out_specs=out_specs,
        out_shape=out_shape,
        compiler_params=_cp("arbitrary" if n_s else "parallel"),
        name=name,
    )(*arrs)
    return res[0] if len(res) == 1 else tuple(res)


def _shift_down(x, prev8, k):
    cat = jnp.concatenate([prev8, x], axis=0)
    return pltpu.roll(cat, k, 0)[prev8.shape[0]:, :]


def _shift_up(x, next8, k):
    n = x.shape[0]
    cat = jnp.concatenate([x, next8], axis=0)
    return pltpu.roll(cat, n + next8.shape[0] - k, 0)[:n, :]


def _colsum(x):
    return jnp.sum(x, axis=0, keepdims=True)


def _silu(x):
    return x * jax.nn.sigmoid(x)


def _modulate_fn(x, nw, sc, sh):
    r = lax.rsqrt(jnp.mean(x * x, axis=-1, keepdims=True) + EPS)
    return (x * r * nw) * (1.0 + sc) + sh


def modulate_fwd(x, nw, sc, sh, name):
    L, D = x.shape

    def fn(i, nb, xt, nwv, scv, shv):
        return _modulate_fn(xt, nwv, scv, shv)

    return rowwise(fn, name=name, L=L, tm=_pick(L, 512, SUBLANE), rows=[(x, 0, D, "cur")],
                   consts=[nw, sc, sh], outs=[(D, bf16)])


def modulate_bwd(x, nw, sc, sh, dh, dx_in, name):
    L, D = x.shape

    def fn(i, nb, xt, dht, dxt, nwv, scv, shv):
        _, vjp = jax.vjp(_modulate_fn, xt, nwv, scv, shv)
        dx, dnw, dsc, dsh = vjp(dht)
        return dxt + dx, dnw, dsc, dsh

    return rowwise(fn, name=name, L=L, tm=_pick(L, 256, SUBLANE),
                   rows=[(x, 0, D, "cur"), (dh, 0, D, "cur"), (dx_in, 0, D, "cur")],
                   consts=[nw, sc, sh], outs=[(D, f32)], sums=[(1, D)] * 3)


def resid_bwd(dx, y, g, name):
    L, D = dx.shape

    def fn(i, nb, dxt, yt, gv):
        return dxt * gv, _colsum(dxt * yt)

    return rowwise(fn, name=name, L=L, tm=_pick(L, 512, SUBLANE),
                   rows=[(dx, 0, D, "cur"), (y, 0, D, "cur")], consts=[g],
                   outs=[(D, bf16)], sums=[(1, D)])


def _resid_epi(acc, xt, gv):
    return acc, xt + gv * acc


def _stack_rows(rows, n=SUBLANE):
    c = rows[0].shape[1]
    ridx = lax.broadcasted_iota(jnp.int32, (n, c), 0)
    out = jnp.zeros((n, c), f32)
    for j, r in enumerate(rows):
        out = out + jnp.where(ridx == j, r, 0.0)
    return out


def _conv_causal(x, prev8, w):
    W = w.shape[0]
    y = x * w[W - 1:W, :]
    for j in range(W - 1):
        y = y + _shift_down(x, prev8, W - 1 - j) * w[j:j + 1, :]
    return y


def _conv_causal_bwd_x(dy, next8, w):
    W = w.shape[0]
    dx = dy * w[W - 1:W, :]
    for j in range(W - 1):
        dx = dx + _shift_up(dy, next8, W - 1 - j) * w[j:j + 1, :]
    return dx


def _conv_causal_bwd_w(dy, x, prev8, W):
    rows = [_colsum(dy * _shift_down(x, prev8, W - 1 - j)) for j in range(W - 1)]
    rows.append(_colsum(dy * x))
    return _stack_rows(rows)


def ffn_act_fwd(up, conv_w, name):
    L, F2 = up.shape
    F = F2 // 2

    def fn(i, nb, u, p8, w):
        c = _conv_causal(u.astype(f32), p8.astype(f32) * (i > 0).astype(f32), w)
        return _silu(c[:, :F]) * c[:, F:]

    return rowwise(fn, name=name, L=L, tm=_pick(L, 128, SUBLANE),
                   rows=[(up, 0, F2, "cur"), (up, 0, F2, "prev")], consts=[conv_w], outs=[(F, bf16)])


def ffn_act_conv_bwd(up, conv_w, dact, name):
    L, F2 = up.shape
    F = F2 // 2
    W = conv_w.shape[0]

    def fn(i, nb, u, da, p8, un8, dan8, w):
        tm, ext = u.shape[0], un8.shape[0]
        more = (i < nb - 1).astype(f32)
        u, da = u.astype(f32), da.astype(f32)
        p8 = p8.astype(f32) * (i > 0).astype(f32)
        c = _conv_causal(jnp.concatenate([u, un8.astype(f32) * more], axis=0), p8, w)
        dae = jnp.concatenate([da, dan8.astype(f32) * more], axis=0)
        a, b = c[:, :F], c[:, F:]
        sg = jax.nn.sigmoid(a)
        dc = jnp.concatenate([dae * b * (sg * (1.0 + a * (1.0 - sg))), dae * a * sg], axis=1)
        dx = dc[:tm] * w[W - 1:W, :]
        for j in range(W - 1):
            dx = dx + pltpu.roll(dc, tm + ext - (W - 1 - j), 0)[:tm] * w[j:j + 1, :]
        return dx, _conv_causal_bwd_w(dc[:tm], u, p8, W)

    return rowwise(fn, name=name, L=L, tm=_pick(L, 128, SUBLANE),
                   rows=[(up, 0, F2, "cur"), (dact, 0, F, "cur"), (up, 0, F2, "prev"), (up, 0, F2, "next"),
                         (dact, 0, F, "next")],
                   consts=[conv_w], outs=[(F2, bf16)], sums=[(SUBLANE, F2)])


ALIBI = [2.0 ** (-8.0 * (i + 1) / N_ATTN_HEADS) for i in range(N_ATTN_HEADS)]
NEG = -1e30


def _band_mask(n, d, max_dist):
    qi = lax.broadcasted_iota(jnp.int32, (BLOCK, 2 * BLOCK), 0)
    kj = lax.broadcasted_iota(jnp.int32, (BLOCK, 2 * BLOCK), 1)
    dist = BLOCK + qi - kj
    valid = (dist >= 0) & (dist <= max_dist) & ((n > 0) | (kj >= BLOCK))
    return valid, -(d * dist).astype(f32)


def _rms64(x, w):
    r = lax.rsqrt(jnp.mean(x * x, axis=-1, keepdims=True) + EPS)
    xh = x * r
    return xh * w, xh, r


def _rms64_bwd(dy, xh, r, w):
    t = dy * w
    dw = jnp.sum(jnp.sum(dy * xh, axis=0), axis=0, keepdims=True)
    return r * (t - xh * jnp.mean(t * xh, axis=-1, keepdims=True)), dw


class _Plan:
    def __init__(self, dilation, group_a, nq):
        self.d, self.nq = dilation, nq
        if group_a:
            self.P, self.nkv = 1, 1
            self.q0, self.k0, self.v0 = 0, 4, 5
            self.kv_of = lambda j: j // 4
            self.max_dist = A_WINDOW - 1
            slopes = ALIBI[:8]
        else:
            self.P, self.nkv = 4 // nq, nq
            self.q0, self.k0, self.v0 = 6, 10, 14
            self.kv_of = lambda j: j
            self.max_dist = BLOCK
            slopes = ALIBI[8:]
        self.hps = 2 * nq
        sl = np.repeat(np.asarray(slopes, np.float32), HEAD_DIM).reshape(self.P, 1, self.hps * HEAD_DIM)
        self.slopes = jnp.asarray(sl, f32)


def _rows(r, d):
    return pl.ds(r, BLOCK, stride=d) if d > 1 else pl.ds(0, BLOCK)


def _pairs(refs, rows):
    parts = []
    for ref in refs:
        blk = ref[rows, :]
        parts += [blk[:, :HEAD_DIM], blk[:, HEAD_DIM:]]
    return jnp.stack(parts)


def _pairs2(prev_refs, cur_refs, rows):
    parts = []
    for pr, cr in zip(prev_refs, cur_refs):
        blk = jnp.concatenate([pr[rows, :], cr[rows, :]], axis=0)
        parts += [blk[:, :HEAD_DIM], blk[:, HEAD_DIM:]]
    return jnp.stack(parts)


def _lane_pair(t, i):
    return jnp.concatenate([t[2 * i], t[2 * i + 1]], axis=1)


def _riding(body, coll, n_in, n_out, grid):
    if coll is None:
        return body

    def wrapped(*refs):
        ride_refs = (refs[n_in], refs[n_in + 1 + n_out]) + tuple(refs[-3:])
        inner = refs[:n_in] + refs[n_in + 1:n_in + 1 + n_out] + refs[n_in + 2 + n_out:-3]
        pid = [pl.program_id(t) for t in range(len(grid))]

        @pl.when(functools.reduce(jnp.logical_and, [p == 0 for p in pid]))
        def _():
            coll.start(*ride_refs)

        body(*inner)

        @pl.when(functools.reduce(jnp.logical_and, [p == g - 1 for p, g in zip(pid, grid)]))
        def _():
            coll.wait(*ride_refs)

    return wrapped


def _ride_call(body, coll, ride, *, grid, in_specs, out_specs, out_shape, scratch_shapes, semantics, name, args):
    hbm = [pl.BlockSpec(memory_space=pl.ANY)] if coll else []
    return pl.pallas_call(
        _riding(body, coll, len(in_specs), len(out_specs), grid), grid=grid,
        in_specs=list(in_specs) + hbm, out_specs=list(out_specs) + hbm,
        out_shape=list(out_shape) + ([coll.out_shape] if coll else []),
        scratch_shapes=list(scratch_shapes) + (coll.scratch if coll else []),
        compiler_params=_cp(*(["arbitrary"] * len(grid) if coll else semantics)), name=name,
    )(*args, *([ride[0]] if coll else []))


def attn2_fwd(hin, plan, wq, wk, name, ride=None):
    coll = _Coll(*ride) if ride else None
    L = hin.shape[0]
    d, nq, nkv, hps, P = plan.d, plan.nq, plan.nkv, plan.hps, plan.P
    R = BLOCK * d
    nb = L // R
    kv_of, max_dist = plan.kv_of, plan.max_dist
    gqa = 2 * nkv != hps

    def body(*refs):
        q_refs = refs[:nq]
        kp, kc = refs[nq:nq + nkv], refs[nq + nkv:nq + 2 * nkv]
        vp, vc = refs[nq + 2 * nkv:nq + 3 * nkv], refs[nq + 3 * nkv:nq + 4 * nkv]
        sl_ref, wq_ref, wk_ref, o_ref, lse_ref = refs[nq + 4 * nkv:nq + 4 * nkv + 5]
        o_refs = refs[nq + 4 * nkv + 5:2 * nq + 4 * nkv + 5]
        lse_refs = refs[2 * nq + 4 * nkv + 5:]
        n = pl.program_id(1)
        valid, negd = _band_mask(n, d, max_dist)
        slope = jnp.stack([sl_ref[0, :, j * 64:j * 64 + 1] for j in range(hps)])
        wqv, wkv = wq_ref[...], wk_ref[...]

        def residue(r, carry):
            rows = _rows(r, d)
            kn = _rms64(_pairs2(kp, kc, rows), wkv)[0].astype(bf16)
            v = _pairs2(vp, vc, rows).astype(bf16)
            if gqa:
                kn = jnp.stack([kn[kv_of(j)] for j in range(hps)])
                v = jnp.stack([v[kv_of(j)] for j in range(hps)])
            qn = _rms64(_pairs(q_refs, rows), wqv)[0].astype(bf16)
            s = _dot(qn, kn, BNT) * (HEAD_DIM ** -0.5) + slope * negd
            s = jnp.where(valid, s, NEG)
            m = jnp.max(s, axis=-1, keepdims=True)
            p = jnp.exp(s - m)
            l = jnp.sum(p, axis=-1, keepdims=True)
            o = _dot(p.astype(bf16), v, BNN) / l
            lse = jnp.broadcast_to(m + jnp.log(l), (hps, BLOCK, HEAD_DIM))
            for i in range(nq):
                o_refs[i][rows, :] = _lane_pair(o, i)
                lse_refs[i][rows, :] = _lane_pair(lse, i)
            return carry

        lax.fori_loop(0, d, residue, 0)
        for i in range(nq):
            o_ref[:, i * 128:(i + 1) * 128] = o_refs[i][...]
            lse_ref[:, i * 128:(i + 1) * 128] = lse_refs[i][...]

    col = lambda c0, i: (lambda p, n: (n, c0 + p * nq + i))
    prv = lambda c0, i: (lambda p, n: (jnp.maximum(n - 1, 0), c0 + p * nq + i))
    blk = lambda f: pl.BlockSpec((R, 128), f)
    in_specs = [blk(col(plan.q0, i)) for i in range(nq)]
    in_specs += [blk(prv(plan.k0, i)) for i in range(nkv)] + [blk(col(plan.k0, i)) for i in range(nkv)]
    in_specs += [blk(prv(plan.v0, i)) for i in range(nkv)] + [blk(col(plan.v0, i)) for i in range(nkv)]
    in_specs += [pl.BlockSpec((1, 1, hps * 64), lambda p, n: (p, 0, 0)),
                 pl.BlockSpec((1, 64), lambda p, n: (0, 0)), pl.BlockSpec((1, 64), lambda p, n: (0, 0))]
    wide = pl.BlockSpec((R, 128 * nq), lambda p, n: (n, p))
    return _ride_call(
        body, coll, ride, grid=(P, nb), in_specs=in_specs, out_specs=[wide, wide],
        out_shape=[jax.ShapeDtypeStruct((L, 512), f32)] * 2,
        scratch_shapes=[pltpu.VMEM((R, 128), f32)] * (2 * nq),
        semantics=("parallel", "parallel"), name=name,
        args=[hin] * (nq + 4 * nkv) + [plan.slopes, wq, wk])


def attn2_bwd(hin, plan, wq, wk, o, lse, do, dlse, dw0, name, ride=None):
    coll = _Coll(*ride) if ride else None
    L = hin.shape[0]
    d, nq, nkv, hps, P = plan.d, plan.nq, plan.nkv, plan.hps, plan.P
    R = BLOCK * d
    nb = L // R
    kv_of, max_dist = plan.kv_of, plan.max_dist
    nkh = 2 * nkv
    gqa = nkh != hps
    n_in = nq + 4 * nkv + 3 + 4 * nq + 2

    def body(*refs):
        q_refs = refs[:nq]
        kp, kc = refs[nq:nq + nkv], refs[nq + nkv:nq + 2 * nkv]
        vp, vc = refs[nq + 2 * nkv:nq + 3 * nkv], refs[nq + 3 * nkv:nq + 4 * nkv]
        b = nq + 4 * nkv
        sl_ref, wq_ref, wk_ref = refs[b:b + 3]
        b += 3
        o_refs, lse_refs = refs[b:b + nq], refs[b + nq:b + 2 * nq]
        do_refs, dlse_refs = refs[b + 2 * nq:b + 3 * nq], refs[b + 3 * nq:b + 4 * nq]
        dwq0_ref, dwk0_ref = refs[b + 4 * nq:b + 4 * nq + 2]
        dq_ref, dk_ref, dv_ref, dwq_ref, dwk_ref = refs[n_in:n_in + 5]
        sc = refs[n_in + 5:]
        dq_s, dk_s, dv_s = sc[:nq], sc[nq:nq + nkv], sc[nq + nkv:nq + 2 * nkv]
        ck, cv = sc[nq + 2 * nkv:nq + 3 * nkv], sc[nq + 3 * nkv:]
        pp = pl.program_id(0)
        n = pl.program_id(1)

        @pl.when((pp == 0) & (n == 0))
        def _():
            dwq_ref[...] = dwq0_ref[...]
            dwk_ref[...] = dwk0_ref[...]

        @pl.when(n == 0)
        def _():
            for c in (*ck, *cv):
                c[...] = jnp.zeros_like(c)

        @pl.when(n < nb)
        def _():
            valid, negd = _band_mask(n, d, max_dist)
            slope = jnp.stack([sl_ref[0, :, j * 64:j * 64 + 1] for j in range(hps)])
            wqv, wkv = wq_ref[...], wk_ref[...]
            hs = range(hps)

            def residue(r, carry):
                rows = _rows(r, d)
                kn_f, kh, rk = _rms64(_pairs2(kp, kc, rows), wkv)
                kn = kn_f.astype(bf16)
                v = _pairs2(vp, vc, rows).astype(bf16)
                if gqa:
                    kn = jnp.stack([kn[kv_of(j)] for j in hs])
                    v = jnp.stack([v[kv_of(j)] for j in hs])
                qn_f, qh, rq = _rms64(_pairs(q_refs, rows), wqv)
                qn = qn_f.astype(bf16)
                s = _dot(qn, kn, BNT) * (HEAD_DIM ** -0.5) + slope * negd
                p = jnp.where(valid, jnp.exp(s - _pairs(lse_refs, rows)[:, :, :1]), 0.0)
                do_h = _pairs(do_refs, rows)
                delta = jnp.sum(do_h * _pairs(o_refs, rows), axis=-1, keepdims=True)
                do_b = do_h.astype(bf16)
                dp = _dot(do_b, v, BNT)
                ds = (p * (dp - delta + _pairs(dlse_refs, rows)[:, :, :1])).astype(bf16)
                dqn = _dot(ds, kn, BNN) * (HEAD_DIM ** -0.5)
                dkn = _dot(ds, qn, BTN) * (HEAD_DIM ** -0.5)
                dvv = _dot(p.astype(bf16), do_b, BTN)
                if gqa:
                    grp = lambda t: jnp.stack([sum(t[j] for j in hs if kv_of(j) == h) for h in range(nkh)])
                    dkn, dvv = grp(dkn), grp(dvv)
                dq, dwq = _rms64_bwd(dqn, qh, rq, wqv)
                dk, dwk = _rms64_bwd(dkn, kh, rk, wkv)
                for i in range(nq):
                    dq_s[i][rows, :] = _lane_pair(dq, i)
                for i in range(nkv):
                    dk_s[i][rows, :] = ck[i][rows, :] + _lane_pair(dk[:, :BLOCK], i)
                    dv_s[i][rows, :] = cv[i][rows, :] + _lane_pair(dvv[:, :BLOCK], i)
                    ck[i][rows, :] = _lane_pair(dk[:, BLOCK:], i)
                    cv[i][rows, :] = _lane_pair(dvv[:, BLOCK:], i)
                return carry[0] + dwq, carry[1] + dwk

            zero = jnp.zeros((1, HEAD_DIM), f32)
            dwq_a, dwk_a = lax.fori_loop(0, d, residue, (zero, zero))
            dwq_ref[...] += dwq_a
            dwk_ref[...] += dwk_a
            for i in range(nq):
                dq_ref[:, i * 128:(i + 1) * 128] = dq_s[i][...]
            for i in range(nkv):
                dk_ref[:, i * 128:(i + 1) * 128] = dk_s[i][...]
                dv_ref[:, i * 128:(i + 1) * 128] = dv_s[i][...]

        @pl.when(n == nb)
        def _():
            for i in range(nkv):
                dk_ref[:, i * 128:(i + 1) * 128] = ck[i][...]
                dv_ref[:, i * 128:(i + 1) * 128] = cv[i][...]

    cl = lambda n: jnp.minimum(n, nb - 1)
    pv = lambda n: jnp.maximum(jnp.minimum(n, nb - 1) - 1, 0)
    col = lambda c0, i: (lambda p, n: (cl(n), c0 + p * nq + i))
    prv = lambda c0, i: (lambda p, n: (pv(n), c0 + p * nq + i))
    blk = lambda f: pl.BlockSpec((R, 128), f)
    w64 = pl.BlockSpec((1, 64), lambda p, n: (0, 0))
    in_specs = [blk(col(plan.q0, i)) for i in range(nq)]
    in_specs += [blk(prv(plan.k0, i)) for i in range(nkv)] + [blk(col(plan.k0, i)) for i in range(nkv)]
    in_specs += [blk(prv(plan.v0, i)) for i in range(nkv)] + [blk(col(plan.v0, i)) for i in range(nkv)]
    in_specs += [pl.BlockSpec((1, 1, hps * 64), lambda p, n: (p, 0, 0)), w64, w64]
    in_specs += [blk(col(0, i)) for i in range(nq)] * 4 + [w64, w64]
    kvw = 128 * nkv
    out_specs = [pl.BlockSpec((R, 128 * nq), lambda p, n: (cl(n), p)),
                 pl.BlockSpec((R, kvw), lambda p, n: (jnp.maximum(n - 1, 0), p)),
                 pl.BlockSpec((R, kvw), lambda p, n: (jnp.maximum(n - 1, 0), p)), w64, w64]
    same = lambda a: [a] * nq
    return _ride_call(
        body, coll, ride, grid=(P, nb + 1), in_specs=in_specs, out_specs=out_specs,
        out_shape=[jax.ShapeDtypeStruct((L, 512), f32), jax.ShapeDtypeStruct((L, kvw * P), f32),
                   jax.ShapeDtypeStruct((L, kvw * P), f32), jax.ShapeDtypeStruct((1, 64), f32),
                   jax.ShapeDtypeStruct((1, 64), f32)],
        scratch_shapes=[pltpu.VMEM((R, 128), f32)] * (nq + 4 * nkv),
        semantics=("arbitrary", "arbitrary"), name=name,
        args=[hin] * (nq + 4 * nkv) + [plan.slopes, wq, wk, *same(o), *same(lse), *same(do), *same(dlse), *dw0])


def _head_sum(x):
    c = x.shape[1]
    r = lax.broadcasted_iota(jnp.int32, (c, c), 0) // HEAD_DIM
    q = lax.broadcasted_iota(jnp.int32, (c, c), 1) // HEAD_DIM
    return _dot(x, (r == q).astype(f32), precision=HI)


def attn_merge_fwd(oa, la, obs, lbs, sinkb, name):
    L = oa.shape[0]

    def fn(i, nb, oa_t, la_t, o1, o2, o3, l1, l2, l3, sk):
        ya = oa_t * jax.nn.sigmoid(la_t - sk)
        m = jnp.maximum(jnp.maximum(l1, l2), l3)
        e1, e2, e3 = jnp.exp(l1 - m), jnp.exp(l2 - m), jnp.exp(l3 - m)
        yb = (e1 * o1 + e2 * o2 + e3 * o3) / (e1 + e2 + e3)
        return jnp.concatenate([ya, yb], axis=1)

    rows = [(a, 0, 512, "cur") for a in (oa, la, *obs, *lbs)]
    return rowwise(fn, name=name, L=L, tm=_pick(L, 256, SUBLANE), rows=rows, consts=[sinkb], outs=[(1024, bf16)])


def attn_merge_bwd(dcat, oa, la, obs, lbs, sinkb, name):
    L = oa.shape[0]

    def fn(i, nb, da, db, oa_t, la_t, o1, o2, o3, l1, l2, l3, sk):
        keep = jax.nn.sigmoid(la_t - sk)
        dla = _head_sum(da * oa_t) * keep * (1.0 - keep)
        m = jnp.maximum(jnp.maximum(l1, l2), l3)
        e1, e2, e3 = jnp.exp(l1 - m), jnp.exp(l2 - m), jnp.exp(l3 - m)
        z = e1 + e2 + e3
        w1, w2, w3 = e1 / z, e2 / z, e3 / z
        g1, g2, g3 = _head_sum(db * o1), _head_sum(db * o2), _head_sum(db * o3)
        gm = w1 * g1 + w2 * g2 + w3 * g3
        return (da * keep, dla, w1 * db, w2 * db, w3 * db,
                w1 * (g1 - gm), w2 * (g2 - gm), w3 * (g3 - gm), -_colsum(dla))

    rows = [(dcat, 0, 512, "cur"), (dcat, 512, 512, "cur")] + [(a, 0, 512, "cur") for a in (oa, la, *obs, *lbs)]
    return rowwise(fn, name=name, L=L, tm=_pick(L, 256, SUBLANE), rows=rows, consts=[sinkb],
                   outs=[(512, f32)] * 8, sums=[(1, 512)])


def attn_assemble(dqa, dka, dva, dqs, dks, dvs, name):
    L = dqa.shape[0]

    def fn(i, nb, qa, ka, va, q1, q2, q3, k1, k2, k3, v1, v2, v3):
        return jnp.concatenate([qa, ka, va, q1 + q2 + q3, k1 + k2 + k3, v1 + v2 + v3], axis=1)

    rows = [(dqa, 0, 512, "cur"), (dka, 0, 128, "cur"), (dva, 0, 128, "cur")]
    rows += [(a, 0, 512, "cur") for a in (*dqs, *dks, *dvs)]
    return rowwise(fn, name=name, L=L, tm=_pick(L, 256, SUBLANE), rows=rows, outs=[(ATTN_IN, bf16)])


def attention_fwd(hin, wqa, wka, wqb, wkb, sinkb, ride=None):
    oa, la = attn2_fwd(hin, _Plan(1, True, 4), wqa, wka, "attn_a_fwd")
    obs, lbs, got = [], [], None
    for _, d in B_BRANCHES:
        res = attn2_fwd(hin, _Plan(d, False, 2), wqb, wkb, f"attn_b{d}_fwd", ride=ride if d == 1 else None)
        obs.append(res[0])
        lbs.append(res[1])
        got = res[2] if (d == 1 and ride) else got
    ocat = attn_merge_fwd(oa, la, obs, lbs, sinkb, "attn_merge_fwd")
    return ocat, (oa, la, obs, lbs), got


def attention_bwd(hin, wqa, wka, wqb, wkb, sinkb, saved, dcat, rides=None):
    rides = dict(rides or {})
    gots = {}

    def ride_of(key):
        r = rides.get(key)
        return r(gots) if callable(r) else r

    oa, la, obs, lbs = saved
    res = attn_merge_bwd(dcat, oa, la, obs, lbs, sinkb, "attn_merge_bwd")
    doa, dla, dos, dls, dsink = res[0], res[1], res[2:5], res[5:8], res[8]
    zero = jnp.zeros((1, 64), f32)
    res = attn2_bwd(hin, _Plan(1, True, 4), wqa, wka, oa, la, doa, dla, (zero, zero), "attn_a_bwd", ride=ride_of("a"))
    dqa, dka, dva, dwqa, dwka = res[:5]
    if "a" in rides:
        gots["a"] = res[5]
    dqs, dks, dvs = [], [], []
    dwqb = dwkb = zero
    for g, (_, d) in enumerate(B_BRANCHES):
        res = attn2_bwd(hin, _Plan(d, False, 2 if d < 16 else 1), wqb, wkb, obs[g], lbs[g],
                        dos[g], dls[g], (dwqb, dwkb), f"attn_b{d}_bwd", ride=ride_of(d))
        dq, dk, dv, dwqb, dwkb = res[:5]
        if d in rides:
            gots[d] = res[5]
        dqs.append(dq)
        dks.append(dk)
        dvs.append(dv)
    dhin = attn_assemble(dqa, dka, dva, dqs, dks, dvs, "attn_assemble")
    return dhin, dwqa, dwka, dwqb, dwkb, dsink, gots


NS = S5_GROUPS * S5_STATE


def _s5_param_fn(lr, li, ldt):
    dt = jnp.exp(ldt)
    mag, ang = jnp.exp(lr * dt), li * dt
    ab_re, ab_im = mag * jnp.cos(ang), mag * jnp.sin(ang)
    nr, ni = ab_re - 1.0, ab_im
    den = lr * lr + li * li
    return ab_re, ab_im, (nr * lr + ni * li) / den, (ni * lr - nr * li) / den


def s5_params_fwd(lr, li, ldt):
    def body(lr_ref, li_ref, ldt_ref, *outs):
        for o_ref, o in zip(outs, _s5_param_fn(lr_ref[...], li_ref[...], ldt_ref[...])):
            o_ref[...] = o

    return pl.pallas_call(body, out_shape=[jax.ShapeDtypeStruct(lr.shape, f32)] * 4, name="s5_params_fwd")(lr, li, ldt)


def s5_params_bwd(lr, li, ldt, cts):
    def body(lr_ref, li_ref, ldt_ref, c0, c1, c2, c3, dlr, dli, dldt):
        _, vjp = jax.vjp(_s5_param_fn, lr_ref[...], li_ref[...], ldt_ref[...])
        a, b, c = vjp((c0[...], c1[...], c2[...], c3[...]))
        dlr[...] = a
        dli[...] = b
        dldt[...] = c

    return pl.pallas_call(
        body, out_shape=[jax.ShapeDtypeStruct(lr.shape, f32), jax.ShapeDtypeStruct(li.shape, f32),
                         jax.ShapeDtypeStruct(ldt.shape, f32)], name="s5_params_bwd")(lr, li, ldt, *cts)


def _cmul(ar, ai, br, bi):
    return ar * br - ai * bi, ar * bi + ai * br


def s5_scan(z, ab_re, ab_im, f_re, f_im, *, reverse, name):
    L = z.shape[0]
    tm = _pick(L, 256, SUBLANE)
    nb = L // tm
    ng = tm // SUBLANE
    use_f = f_re is not None
    consts = [ab_re, ab_im] + ([f_re, f_im] if use_f else [])

    def body(*refs):
        z_ref = refs[0]
        c_refs = refs[1:1 + len(consts)]
        x_ref, car = refs[1 + len(consts)], refs[2 + len(consts)]
        i = pl.program_id(0)

        @pl.when(i == 0)
        def _():
            car[...] = jnp.zeros_like(car)

        a1 = (c_refs[0][...], c_refs[1][...])
        a2 = _cmul(*a1, *a1)
        a3 = _cmul(*a2, *a1)
        a4 = _cmul(*a2, *a2)
        pw = [a1, a2, a3, a4, _cmul(*a4, *a1), _cmul(*a4, *a2), _cmul(*a4, *a3), _cmul(*a4, *a4)]
        if reverse:
            pw = pw[::-1]
        pw_re = _stack_rows([p[0] for p in pw])
        pw_im = _stack_rows([p[1] for p in pw])
        ridx = lax.broadcasted_iota(jnp.int32, (SUBLANE, NS), 0)
        if use_f:
            fr, fi = c_refs[2][...], c_refs[3][...]

        def group(s, carry):
            cr, ci = carry
            g = (ng - 1 - s) if reverse else s
            r0 = pl.multiple_of(g * SUBLANE, SUBLANE)
            xr = z_ref[pl.ds(r0, SUBLANE), 0:NS]
            xi = z_ref[pl.ds(r0, SUBLANE), NS:2 * NS]
            if use_f:
                xr, xi = _cmul(fr, fi, xr, xi)
            for sft, (pr, pi) in ((1, a1), (2, a2), (4, a4)):
                if reverse:
                    keep = ridx < SUBLANE - sft
                    sr = jnp.where(keep, pltpu.roll(xr, SUBLANE - sft, 0), 0.0)
                    si = jnp.where(keep, pltpu.roll(xi, SUBLANE - sft, 0), 0.0)
                else:
                    keep = ridx >= sft
                    sr = jnp.where(keep, pltpu.roll(xr, sft, 0), 0.0)
                    si = jnp.where(keep, pltpu.roll(xi, sft, 0), 0.0)
                tr, ti = _cmul(pr, pi, sr, si)
                xr, xi = xr + tr, xi + ti
            tr, ti = _cmul(pw_re, pw_im, cr, ci)
            xr, xi = xr + tr, xi + ti
            x_ref[pl.ds(r0, SUBLANE), 0:NS] = xr
            x_ref[pl.ds(r0, SUBLANE), NS:2 * NS] = xi
            row = 0 if reverse else SUBLANE - 1
            return xr[row:row + 1, :], xi[row:row + 1, :]

        cr, ci = lax.fori_loop(0, ng, group, (car[0:1, 0:NS], car[0:1, NS:2 * NS]))
        car[0:1, 0:NS] = cr
        car[0:1, NS:2 * NS] = ci

    blk = (lambda i: (nb - 1 - i, 0)) if reverse else (lambda i: (i, 0))
    return pl.pallas_call(
        body, grid=(nb,),
        in_specs=[pl.BlockSpec((tm, 2 * NS), blk)] + [pl.BlockSpec((1, NS), lambda i: (0, 0))] * len(consts),
        out_specs=pl.BlockSpec((tm, 2 * NS), blk),
        out_shape=jax.ShapeDtypeStruct((L, 2 * NS), f32),
        scratch_shapes=[pltpu.VMEM((SUBLANE, 2 * NS), f32)],
        compiler_params=_cp("arbitrary"), name=name,
    )(z, *consts)


def _s5_post_fn(ypre, u, dvec, gw, gb):
    y = ypre + dvec * u
    g = jax.nn.gelu(y)
    z = _dot(g.astype(bf16), gw.astype(bf16)) + gb
    return g * jax.nn.sigmoid(z)


def s5_post_fwd(ypre, hin, dvec, gw, gb):
    L = ypre.shape[0]

    def fn(i, nb, yt, ut, dv, gwv, gbv):
        return _s5_post_fn(yt, ut, dv, gwv, gbv)

    return rowwise(fn, name="s5_post_fwd", L=L, tm=_pick(L, 512, SUBLANE),
                   rows=[(ypre, 0, S5_WIDTH, "cur"), (hin, 3072, S5_WIDTH, "cur")],
                   consts=[dvec, gw, gb], outs=[(S5_WIDTH, f32)])


def s5_post_bwd(ypre, hin, dvec, gw, gb, dycat):
    L = ypre.shape[0]

    def fn(i, nb, yt, ut, dyt, dv, gwv, gbv):
        _, vjp = jax.vjp(_s5_post_fn, yt, ut, dv, gwv, gbv)
        return vjp(dyt)

    return rowwise(fn, name="s5_post_bwd", L=L, tm=_pick(L, 512, SUBLANE),
                   rows=[(ypre, 0, S5_WIDTH, "cur"), (hin, 3072, S5_WIDTH, "cur"), (dycat, 0, S5_WIDTH, "cur")],
                   consts=[dvec, gw, gb], outs=[(S5_WIDTH, f32)] * 2,
                   sums=[(1, S5_WIDTH), (S5_WIDTH, S5_WIDTH), (1, S5_WIDTH)])


def s5_acc(G, X, bu, f_re, f_im):
    L = G.shape[0]

    def fn(i, nb, g, x, b, xp8, fr, fi):
        gr, gi = g[:, :NS], g[:, NS:]
        xp = _shift_down(x, xp8 * (i > 0).astype(f32), 1)
        xr, xi = xp[:, :NS], xp[:, NS:]
        br, bi = b[:, :NS], b[:, NS:]
        dbu = jnp.concatenate([fr * gr + fi * gi, fr * gi - fi * gr], axis=1)
        return (dbu, _colsum(xr * gr + xi * gi), _colsum(xr * gi - xi * gr),
                _colsum(br * gr + bi * gi), _colsum(br * gi - bi * gr))

    return rowwise(fn, name="s5_acc", L=L, tm=_pick(L, 256, SUBLANE),
                   rows=[(G, 0, 2 * NS, "cur"), (X, 0, 2 * NS, "cur"), (bu, 0, 2 * NS, "cur"), (X, 0, 2 * NS, "prev")],
                   consts=[f_re, f_im], outs=[(2 * NS, bf16)], sums=[(1, NS)] * 4)


def _s5_blockdiag(b_re, b_im, c_re, c_im):
    eye = jnp.eye(S5_GROUPS, dtype=f32)
    bb = lambda b: jnp.einsum("gpi,gh->gihp", b, eye).reshape(S5_WIDTH, NS)
    cc = lambda c: jnp.einsum("gip,gh->gphi", c, eye).reshape(NS, S5_WIDTH)
    return jnp.concatenate([bb(b_re), bb(b_im)], axis=1), jnp.concatenate([cc(c_re), -cc(c_im)], axis=0)


def _s5_blockdiag_grads(dB, dC):
    gb = lambda m: jnp.einsum("gigp->gpi", m.reshape(S5_GROUPS, S5_GROUP, S5_GROUPS, S5_STATE))
    gc = lambda m: jnp.einsum("gpgi->gip", m.reshape(S5_GROUPS, S5_STATE, S5_GROUPS, S5_GROUP))
    return gb(dB[:, :NS]), gb(dB[:, NS:]), gc(dC[:NS]), -gc(dC[NS:])


def s5_fwd(hin, prm):
    ab_re, ab_im, f_re, f_im = s5_params_fwd(prm["lr"], prm["li"], prm["ldt"])
    flat = lambda a: a.reshape(1, NS)
    ab_re, ab_im, f_re, f_im = flat(ab_re), flat(ab_im), flat(f_re), flat(f_im)
    Bblk, Cblk = _s5_blockdiag(prm["b_re"], prm["b_im"], prm["c_re"], prm["c_im"])
    bu = mm(hin, Bblk, name="s5_bu", a_win=(3072, S5_WIDTH))
    X = s5_scan(bu, ab_re, ab_im, f_re, f_im, reverse=False, name="s5_scan_fwd")
    ypre = mm(X, Cblk, name="s5_y")
    yc = s5_post_fwd(ypre, hin, prm["d"], prm["gw"], prm["gb"])
    return yc, (ab_re, ab_im, f_re, f_im, Bblk, Cblk, bu, X, ypre)


def s5_bwd(hin, prm, saved, dycat):
    ab_re, ab_im, f_re, f_im, Bblk, Cblk, bu, X, ypre = saved
    dypre, du_skip, dd, dgw, dgb = s5_post_bwd(ypre, hin, prm["d"], prm["gw"], prm["gb"], dycat)
    dX = mm(dypre, Cblk, tb=True, name="s5_dx")
    dC = mm(X, dypre, ta=True, name="s5_dc")
    G = s5_scan(dX, ab_re, -ab_im, None, None, reverse=True, name="s5_scan_bwd")
    dbu, dar, dai, dfr, dfi = s5_acc(G, X, bu, f_re, f_im)
    dB = mm(hin, dbu, ta=True, a_win=(3072, S5_WIDTH), name="s5_db")
    du_b = mm(dbu, Bblk, tb=True, name="s5_du")
    sh = prm["lr"].shape
    dlr, dli, dldt = s5_params_bwd(prm["lr"], prm["li"], prm["ldt"],
                                   [a.reshape(sh) for a in (dar, dai, dfr, dfi)])
    db_re, db_im, dc_re, dc_im = _s5_blockdiag_grads(dB, dC)
    grads = dict(lr=dlr, li=dli, ldt=dldt, b_re=db_re, b_im=db_im, c_re=dc_re, c_im=dc_im, d=dd, gw=dgw, gb=dgb)
    return du_skip, du_b, grads


DN_W = DN_HEADS * DN_DK
QKV_W = 3 * DN_W


def _softplus(x):
    return jnp.maximum(x, 0.0) + jnp.log(1.0 + jnp.exp(-jnp.abs(x)))


def _dn_pre(c, ab, alog, dtb):
    s = _silu(c)
    parts = []
    for h in range(2 * DN_HEADS):
        sh = s[:, h * 128:(h + 1) * 128]
        scale = DN_DK ** -0.5 if h < DN_HEADS else 1.0
        parts.append(sh * (lax.rsqrt(jnp.sum(sh * sh, axis=-1, keepdims=True) + EPS) * scale))
    parts.append(s[:, 2 * DN_W:])
    g = -jnp.exp(alog) * _softplus(ab[:, :128] + dtb)
    beta = jax.nn.sigmoid(ab[:, 128:])
    return jnp.concatenate(parts, axis=1), jnp.concatenate([g, beta], axis=1)


def _dn_pre_bwd(c, ab, alog, dtb, dqkv, dgb):
    sg = jax.nn.sigmoid(c)
    s = c * sg
    parts = []
    for h in range(2 * DN_HEADS):
        sh = s[:, h * 128:(h + 1) * 128]
        dy = dqkv[:, h * 128:(h + 1) * 128]
        scale = DN_DK ** -0.5 if h < DN_HEADS else 1.0
        r = lax.rsqrt(jnp.sum(sh * sh, axis=-1, keepdims=True) + EPS)
        parts.append(scale * r * (dy - sh * (r * r) * jnp.sum(dy * sh, axis=-1, keepdims=True)))
    parts.append(dqkv[:, 2 * DN_W:])
    dc = jnp.concatenate(parts, axis=1) * (sg * (1.0 + c * (1.0 - sg)))
    pre = ab[:, :128] + dtb
    ea = jnp.exp(alog)
    dg = dgb[:, :128]
    da = dg * (-ea) * jax.nn.sigmoid(pre)
    dalog = _colsum(dg * (-ea) * _softplus(pre))
    beta = jax.nn.sigmoid(ab[:, 128:])
    db = dgb[:, 128:] * beta * (1.0 - beta)
    return dc, jnp.concatenate([da, db], axis=1), dalog, _colsum(da)


def dn_pre_fwd(hin, conv_w, alog, dtb):
    L = hin.shape[0]

    def fn(i, nb, x, ab, p8, w, al, db):
        c = _conv_causal(x, p8 * (i > 0).astype(f32), w)
        return _dn_pre(c, ab, al, db)

    return rowwise(fn, name="dn_pre_fwd", L=L, tm=_pick(L, 256, SUBLANE),
                   rows=[(hin, 0, QKV_W, "cur"), (hin, 3328, 256, "cur"), (hin, 0, QKV_W, "prev")],
                   consts=[conv_w, alog, dtb], outs=[(QKV_W, f32), (256, f32)])


def dn_pre_bwd(hin, conv_w, alog, dtb, dqkv3, dg, dbeta):
    L = hin.shape[0]

    def fn(i, nb, x, ab, dq, dk, dv, dgt, dbt, p8, w, al, db):
        c = _conv_causal(x, p8 * (i > 0).astype(f32), w)
        return _dn_pre_bwd(c, ab, al, db, jnp.concatenate([dq, dk, dv], axis=1), jnp.concatenate([dgt, dbt], axis=1))

    rows = [(hin, 0, QKV_W, "cur"), (hin, 3328, 256, "cur")] + [(a, 0, DN_W, "cur") for a in dqkv3]
    rows += [(dg, 0, 128, "cur"), (dbeta, 0, 128, "cur"), (hin, 0, QKV_W, "prev")]
    return rowwise(fn, name="dn_pre_bwd", L=L, tm=_pick(L, 128, SUBLANE), rows=rows,
                   consts=[conv_w, alog, dtb], outs=[(QKV_W, f32), (256, f32)], sums=[(1, 128), (1, 128)])


def _split(a):
    hi = a.astype(bf16)
    return hi, (a - hi.astype(f32)).astype(bf16)


def _dot3_raw(a, b, dims):
    ah, al = _split(a)
    bh, bl = _split(b)
    return _dot(ah, bh, dims) + (_dot(ah, bl, dims) + _dot(al, bh, dims))


@functools.partial(jax.custom_vjp, nondiff_argnums=(2,))
def _dot3(a, b, dims=NN):
    return _dot3_raw(a, b, dims)


def _dot3_fwd(a, b, dims):
    return _dot3_raw(a, b, dims), (a, b)


BNN = (((2,), (1,)), ((0,), (0,)))
BNT = (((2,), (2,)), ((0,), (0,)))
BTN = (((1,), (1,)), ((0,), (0,)))


def _dot_bwd(raw, dims, res, g):
    a, b = res
    nn, nt, tn = (BNN, BNT, BTN) if dims[1][0] else (NN, NT, TN)
    if dims == nn:
        return raw(g, b, nt), raw(a, g, tn)
    if dims == nt:
        return raw(g, b, nn), raw(g, a, tn)
    assert dims == tn
    return raw(b, g, nt), raw(a, g, nn)


_dot3.defvjp(_dot3_fwd, functools.partial(_dot_bwd, _dot3_raw))


def _dot1_raw(a, b, dims):
    return _dot(a.astype(bf16), b.astype(bf16), dims)


@functools.partial(jax.custom_vjp, nondiff_argnums=(2,))
def _dot1(a, b, dims=NN):
    return _dot1_raw(a, b, dims)


_dot1.defvjp(lambda a, b, dims: (_dot1_raw(a, b, dims), (a, b)), functools.partial(_dot_bwd, _dot1_raw))


def _unit_lower_inverse(nmat):
    C = nmat.shape[-1]
    eye = (lax.broadcasted_iota(jnp.int32, (C, C), 0) == lax.broadcasted_iota(jnp.int32, (C, C), 1)).astype(f32)
    T = eye - nmat
    Pw = _dot3(nmat, nmat, BNN)
    for step in range(5):
        T = T + _dot3(T, Pw, BNN)
        if step < 4:
            Pw = _dot3(Pw, Pw, BNN)
    return T


@jax.custom_vjp
def _inverse_known(nmat, T):
    return T


def _inverse_known_bwd(T, g):
    return -_dot3(_dot3(T, g, BTN), T, BNT), jnp.zeros_like(T)


_inverse_known.defvjp(lambda nmat, T: (T, T), _inverse_known_bwd)


def _dn_chunk(q, k, v, gcol, bcol, S, T_known=None):
    C = q.shape[1]
    r = lax.broadcasted_iota(jnp.int32, (C, C), 0)
    c = lax.broadcasted_iota(jnp.int32, (C, C), 1)
    tril = (r >= c).astype(f32)
    strict = (r > c).astype(f32)
    eye = (r == c).astype(f32)
    hd = _dot3
    grow = jnp.sum(eye * gcol, axis=1, keepdims=True)
    Gcol = jnp.sum(tril * grow, axis=2, keepdims=True)
    Grow = jnp.sum(eye * Gcol, axis=1, keepdims=True)
    gamma = jnp.exp((Gcol - Grow) * tril) * tril
    ld = _dot1
    nmat = strict * bcol * ld(k, k, BNT) * gamma
    T = _unit_lower_inverse(nmat) if T_known is None else _inverse_known(nmat, T_known)
    eG = jnp.exp(Gcol)
    u = hd(T, bcol * v, BNN)
    w = hd(T, (bcol * eG) * k, BNN)
    qk = ld(q, k, BNT) * gamma
    vnew = u - ld(w, S, BNN)
    o = ld(q * eG, S, BNN) + ld(qk, vnew, BNN)
    Glast = jnp.sum(gcol, axis=1, keepdims=True)
    S2 = S * jnp.exp(Glast) + ld(k * jnp.exp(Glast - Gcol), vnew, BTN)
    return o, S2, T


def _heads(x_ref):
    return jnp.stack([x_ref[:, h * 128:(h + 1) * 128] for h in range(DN_HEADS)])


def _head_cols(g_ref):
    return jnp.stack([g_ref[:, h:h + 1] for h in range(DN_HEADS)])


def dn_chunks_fwd(qkvn, gb):
    L = qkvn.shape[0]
    C = DN_CHUNK
    nc = L // C

    def body(q_ref, k_ref, v_ref, g_ref, b_ref, o_ref, sin_ref, t_ref, S):
        n = pl.program_id(0)

        @pl.when(n == 0)
        def _():
            S[...] = jnp.zeros_like(S)

        s_in = S[...]
        sin_ref[...] = s_in
        o, s2, t = _dn_chunk(_heads(q_ref), _heads(k_ref), _heads(v_ref), _head_cols(g_ref), _head_cols(b_ref), s_in)
        for h in range(DN_HEADS):
            o_ref[:, h * 128:(h + 1) * 128] = o[h]
        t_ref[...] = t
        S[...] = s2

    blk = lambda j: pl.BlockSpec((C, DN_W), lambda n, j=j: (n, j))
    gblk = lambda j: pl.BlockSpec((C, 128), lambda n, j=j: (n, j))
    return pl.pallas_call(
        body, grid=(nc,),
        in_specs=[blk(0), blk(1), blk(2), gblk(0), gblk(1)],
        out_specs=[pl.BlockSpec((C, DN_W), lambda n: (n, 0)),
                   pl.BlockSpec((DN_HEADS, None, 128, 128), lambda n: (0, n, 0, 0)),
                   pl.BlockSpec((DN_HEADS, None, C, C), lambda n: (0, n, 0, 0))],
        out_shape=[jax.ShapeDtypeStruct((L, DN_W), f32), jax.ShapeDtypeStruct((DN_HEADS, nc, 128, 128), f32),
                   jax.ShapeDtypeStruct((DN_HEADS, nc, C, C), f32)],
        scratch_shapes=[pltpu.VMEM((DN_HEADS, 128, 128), f32)],
        compiler_params=_cp("arbitrary"), name="dn_chunks_fwd",
    )(qkvn, qkvn, qkvn, gb, gb)


def dn_chunks_bwd(qkvn, gb, s_in, t_inv, do):
    L = qkvn.shape[0]
    C = DN_CHUNK
    nc = L // C

    def body(q_ref, k_ref, v_ref, g_ref, b_ref, sin_ref, t_ref, do_ref, dq_ref, dk_ref, dv_ref, dg_ref, db_ref, dS):
        n = pl.program_id(0)

        @pl.when(n == 0)
        def _():
            dS[...] = jnp.zeros_like(dS)

        args = (_heads(q_ref), _heads(k_ref), _heads(v_ref), _head_cols(g_ref), _head_cols(b_ref), sin_ref[...])
        t_known = t_ref[...]
        _, vjp = jax.vjp(lambda *a: _dn_chunk(*a, T_known=t_known)[:2], *args)
        dq, dk, dv, dg, db, ds = vjp((_heads(do_ref), dS[...]))
        lane = lax.broadcasted_iota(jnp.int32, (C, 128), 1)
        dg_all = jnp.zeros((C, 128), f32)
        db_all = jnp.zeros((C, 128), f32)
        for h in range(DN_HEADS):
            sl = slice(h * 128, (h + 1) * 128)
            dq_ref[:, sl] = dq[h]
            dk_ref[:, sl] = dk[h]
            dv_ref[:, sl] = dv[h]
            dg_all = dg_all + jnp.where(lane == h, dg[h], 0.0)
            db_all = db_all + jnp.where(lane == h, db[h], 0.0)
        dS[...] = ds
        dg_ref[...] = dg_all
        db_ref[...] = db_all

    rv = lambda n: nc - 1 - n
    blk = lambda j: pl.BlockSpec((C, DN_W), lambda n, j=j: (rv(n), j))
    gblk = lambda j: pl.BlockSpec((C, 128), lambda n, j=j: (rv(n), j))
    oblk = pl.BlockSpec((C, DN_W), lambda n: (rv(n), 0))
    gout = pl.BlockSpec((C, 128), lambda n: (rv(n), 0))
    return pl.pallas_call(
        body, grid=(nc,),
        in_specs=[blk(0), blk(1), blk(2), gblk(0), gblk(1),
                  pl.BlockSpec((DN_HEADS, None, 128, 128), lambda n: (0, rv(n), 0, 0)),
                  pl.BlockSpec((DN_HEADS, None, C, C), lambda n: (0, rv(n), 0, 0)), oblk],
        out_specs=[oblk] * 3 + [gout] * 2,
        out_shape=[jax.ShapeDtypeStruct((L, DN_W), f32)] * 3 + [jax.ShapeDtypeStruct((L, 128), f32)] * 2,
        scratch_shapes=[pltpu.VMEM((DN_HEADS, 128, 128), f32)],
        compiler_params=_cp("arbitrary"), name="dn_chunks_bwd",
    )(qkvn, qkvn, qkvn, gb, gb, s_in, t_inv, do)


def _dn_post(o, z, w):
    parts = []
    for h in range(DN_HEADS):
        oh = o[:, h * 128:(h + 1) * 128]
        r = lax.rsqrt(jnp.mean(oh * oh, axis=-1, keepdims=True) + EPS)
        parts.append(oh * r * w)
    return jnp.concatenate(parts, axis=1) * _silu(z)


def dn_post_fwd(o, hin, yc, onorm):
    L = o.shape[0]

    def fn(i, nb, ot, zt, yct, w):
        return jnp.concatenate([yct, _dn_post(ot, zt, w)], axis=1)

    return rowwise(fn, name="dn_post_fwd", L=L, tm=_pick(L, 256, SUBLANE),
                   rows=[(o, 0, DN_W, "cur"), (hin, 2304, DN_W, "cur"), (yc, 0, S5_WIDTH, "cur")],
                   consts=[onorm], outs=[(1024, bf16)])


def dn_post_bwd(o, hin, onorm, dycat):
    L = o.shape[0]

    def fn(i, nb, ot, zt, d0, d1, d2, w):
        dy = jnp.concatenate([d0, d1, d2], axis=1)
        sg = jax.nn.sigmoid(zt)
        sz = zt * sg
        dos, dw = [], jnp.zeros((1, 128), f32)
        nrm = []
        for h in range(DN_HEADS):
            sl = slice(h * 128, (h + 1) * 128)
            oh = ot[:, sl]
            r = lax.rsqrt(jnp.mean(oh * oh, axis=-1, keepdims=True) + EPS)
            ohat = oh * r
            t = dy[:, sl] * sz[:, sl]
            dw = dw + _colsum(t * ohat)
            t = t * w
            dos.append(r * (t - ohat * jnp.mean(t * ohat, axis=-1, keepdims=True)))
            nrm.append(ohat * w)
        dz = dy * jnp.concatenate(nrm, axis=1) * (sg * (1.0 + zt * (1.0 - sg)))
        return jnp.concatenate(dos, axis=1), dz, dw

    rows = [(o, 0, DN_W, "cur"), (hin, 2304, DN_W, "cur")] + [(dycat, 256 * (1 + j), 256, "cur") for j in range(3)]
    return rowwise(fn, name="dn_post_bwd", L=L, tm=_pick(L, 256, SUBLANE), rows=rows,
                   consts=[onorm], outs=[(DN_W, f32), (DN_W, f32)], sums=[(1, 128)])


def conv_bwd_win(xarr, start, C, w, dc, name):
    L = xarr.shape[0]
    W = w.shape[0]

    def fn(i, nb, xt, dct, p8, n8, wv):
        dx = _conv_causal_bwd_x(dct, n8 * (i < nb - 1).astype(f32), wv)
        dw = _conv_causal_bwd_w(dct, xt, p8 * (i > 0).astype(f32), W)
        return dx, dw

    return rowwise(fn, name=name, L=L, tm=_pick(L, 128, SUBLANE),
                   rows=[(xarr, start, C, "cur"), (dc, 0, C, "cur"), (xarr, start, C, "prev"), (dc, 0, C, "next")],
                   consts=[w], outs=[(C, bf16)], sums=[(SUBLANE, C)])


def rec_assemble(dx_qkv, dz, du1, du2, dab):
    L = dz.shape[0]

    def fn(i, nb, a, b, c, d, e):
        return jnp.concatenate([a.astype(f32), b, c + d, e], axis=1)

    return rowwise(fn, name="rec_assemble", L=L, tm=_pick(L, 256, SUBLANE),
                   rows=[(dx_qkv, 0, QKV_W, "cur"), (dz, 0, DN_W, "cur"), (du1, 0, 256, "cur"),
                         (du2, 0, 256, "cur"), (dab, 0, 256, "cur")], outs=[(REC_PAD, bf16)])


def deltanet_fwd(hin, prm, yc):
    qkvn, gb = dn_pre_fwd(hin, prm["conv"], prm["alog"], prm["dtb"])
    o, s_in, t_inv = dn_chunks_fwd(qkvn, gb)
    ycat = dn_post_fwd(o, hin, yc, prm["onorm"])
    return ycat, (qkvn, gb, o, s_in, t_inv)


def deltanet_bwd(hin, prm, saved, dycat):
    qkvn, gb, o, s_in, t_inv = saved
    do, dz, donorm = dn_post_bwd(o, hin, prm["onorm"], dycat)
    dq, dk, dv, dgH, dbH = dn_chunks_bwd(qkvn, gb, s_in, t_inv, do)
    dc, dab, dalog, ddtb = dn_pre_bwd(hin, prm["conv"], prm["alog"], prm["dtb"], (dq, dk, dv), dgH, dbH)
    dx_qkv, dconv = conv_bwd_win(hin, 0, QKV_W, prm["conv"], dc, "dn_conv_bwd")
    return dx_qkv, dz, dab, dict(conv=dconv[:DN_CONV], alog=dalog, dtb=ddtb, onorm=donorm)


AXES = ("x", "y", "c")


class _Coll:
    def __init__(self, x, axes, mode):
        self.axes, self.mode = axes, mode
        self.P = 2 ** len(axes)
        shape = x.shape if mode == "gather" else x.shape[1:]
        self.out_shape = jax.ShapeDtypeStruct((self.P,) + tuple(shape), x.dtype)
        self.scratch = [pltpu.SemaphoreType.DMA((self.P - 1,)), pltpu.SemaphoreType.DMA((self.P - 1,)),
                        pltpu.SemaphoreType.DMA]

    def _copies(self, x_ref, out_ref, send_sems, recv_sems, local_sem, with_recvs):
        axes, k = self.axes, len(self.axes)
        co = {a: lax.axis_index(a) for a in AXES}
        me = 0
        for a in axes:
            me = me * 2 + co[a]
        src = (lambda j: x_ref) if self.mode == "gather" else (lambda j: x_ref.at[j])
        local = pltpu.make_async_copy(src(me), out_ref.at[me], local_sem)
        sends, recvs = [], []
        for m in range(1, self.P):
            tco = dict(co)
            t = 0
            for i, a in enumerate(axes):
                if (m >> (k - 1 - i)) & 1:
                    tco[a] = 1 - co[a]
                t = t * 2 + tco[a]
            dev = tuple(tco[a] for a in AXES)
            mk = functools.partial(pltpu.make_async_remote_copy, src_ref=src(t), send_sem=send_sems.at[m - 1],
                                   recv_sem=recv_sems.at[m - 1], device_id=dev, device_id_type=MESH)
            sends.append(mk(dst_ref=out_ref.at[me]))
            if with_recvs:
                recvs.append(mk(dst_ref=out_ref.at[t]))
        return local, sends, recvs

    def start(self, *refs):
        local, sends, _ = self._copies(*refs, with_recvs=False)
        local.start()
        for cp in sends:
            cp.start()

    def wait(self, *refs):
        local, sends, recvs = self._copies(*refs, with_recvs=True)
        for cp in recvs:
            cp.wait_recv()
        for cp in sends:
            cp.wait_send()
        local.wait()


def _collective(x, axes, mode, name):
    coll = _Coll(x, axes, mode)

    def body(*refs):
        coll.start(*refs)
        coll.wait(*refs)

    return pl.pallas_call(
        body, in_specs=[pl.BlockSpec(memory_space=pl.ANY)], out_specs=pl.BlockSpec(memory_space=pl.ANY),
        out_shape=coll.out_shape, scratch_shapes=coll.scratch, name=name,
    )(x)


def all_gather(x, axes, name):
    return _collective(x, axes, "gather", name)


def exchange(x, axes, name):
    return _collective(x, axes, "exchange", name)


def sum_slots(x, name, out_dtype=f32):
    P, R, C = x.shape
    tr = _pick(R, 256, 2 * SUBLANE)

    def body(x_ref, o_ref):
        acc = x_ref[0].astype(f32)
        for j in range(1, P):
            acc = acc + x_ref[j].astype(f32)
        o_ref[...] = acc.astype(o_ref.dtype)

    return pl.pallas_call(
        body, grid=(R // tr,), in_specs=[pl.BlockSpec((P, tr, C), lambda i: (0, i, 0))],
        out_specs=pl.BlockSpec((tr, C), lambda i: (i, 0)), out_shape=jax.ShapeDtypeStruct((R, C), out_dtype),
        compiler_params=_cp("parallel"), name=name,
    )(x)


def _pack(arrs, width, row_mult, dtype):
    flat = jnp.concatenate([a.astype(dtype).reshape(-1) for a in arrs])
    unit = width * row_mult
    n = -(-flat.shape[0] // unit) * unit
    return jnp.pad(flat, (0, n - flat.shape[0])).reshape(n // width, width)


def _unpack(flat, shapes):
    flat = flat.reshape(-1)
    out, off = [], 0
    for s in shapes:
        n = int(np.prod(s))
        out.append(flat[off:off + n].reshape(s))
        off += n
    return out


def ada_fwd(c_all, ada_w):
    def body(c_ref, w_ref, o_ref):
        cond = _silu(c_ref[...])
        for l in range(ada_w.shape[0]):
            o_ref[l] = _dot(cond, w_ref[l], precision=HI)

    return pl.pallas_call(body, out_shape=jax.ShapeDtypeStruct((ada_w.shape[0], c_all.shape[0], ada_w.shape[2]), f32),
                          compiler_params=pltpu.CompilerParams(vmem_limit_bytes=VMEM_LIMIT), name="ada_fwd")(c_all, ada_w)


def ada_bwd(c_all, dmod):
    def body(c_ref, d_ref, o_ref):
        cond = _silu(c_ref[...])
        for l in range(dmod.shape[0]):
            o_ref[l] = _dot(cond, d_ref[l], TN, precision=HI)

    return pl.pallas_call(body, out_shape=jax.ShapeDtypeStruct((dmod.shape[0], c_all.shape[1], dmod.shape[2]), f32),
                          compiler_params=pltpu.CompilerParams(vmem_limit_bytes=VMEM_LIMIT), name="ada_bwd")(c_all, dmod)


def loss_fwd_bwd(y, target):
    L, D = y.shape

    def fn(i, nb, yt, tt):
        e = yt - tt
        return e * (1.0 / D), jnp.sum(jnp.sum(e * e, axis=1, keepdims=True), axis=0, keepdims=True)

    return rowwise(fn, name="loss", L=L, tm=_pick(L, 512, SUBLANE), rows=[(y, 0, D, "cur"), (target, 0, D, "cur")],
                   outs=[(D, f32)], sums=[(1, 1)])


def adamw(w, g, m, v, name):
    R, C = w.shape

    def fn(i, nb, wt, gt, mt, vt):
        m2 = ADAM_B1 * mt + (1.0 - ADAM_B1) * gt
        v2 = ADAM_B2 * vt + (1.0 - ADAM_B2) * (gt * gt)
        m_hat = m2 / (1.0 - ADAM_B1 ** ADAM_STEP)
        v_hat = v2 / (1.0 - ADAM_B2 ** ADAM_STEP)
        delta = -ADAM_LR * (m_hat / (jnp.sqrt(v_hat) + ADAM_EPS) + ADAM_WD * wt)
        return delta, m2, v2

    return rowwise(fn, name=name, L=R, tm=_pick(R, 256, SUBLANE), rows=[(a, 0, C, "cur") for a in (w, g, m, v)],
                   outs=[(C, f32)] * 3)


W_NAMES = ["ada_w", "ada_b", "norm_mix", "norm_ffn", "attn_w_in", "attn_q_norm_a", "attn_k_norm_a", "attn_q_norm_b",
           "attn_k_norm_b", "attn_sinks", "attn_w_out", "rec_w_in", "s5_lambda_re", "s5_lambda_im", "s5_log_dt",
           "s5_b_re", "s5_b_im", "s5_c_re", "s5_c_im", "s5_d", "s5_glu_w", "s5_glu_b", "dn_conv", "dn_a_log",
           "dn_dt_bias", "dn_out_norm", "rec_w_out", "ffn_w_up", "ffn_conv", "ffn_w_down"]
BIG = ["attn_w_in", "attn_w_out", "rec_w_in", "rec_w_out", "ffn_w_up", "ffn_w_down"]
SMALL_SHARDED = ["s5_d", "s5_glu_w", "s5_glu_b", "dn_conv", "ffn_conv"]
SMALL_REPL = [n for n in W_NAMES if n not in BIG and n not in SMALL_SHARDED and n != "ada_w"]
NSH = 4
GRAD_WIRE = (bf16,)


SHARD_AXIS = {"attn_w_in": 2, "attn_w_out": 1, "rec_w_in": 2, "rec_w_out": 1, "ffn_w_up": 2, "ffn_w_down": 1,
              "s5_d": 1, "s5_glu_w": 1, "s5_glu_b": 1, "dn_conv": 2, "ffn_conv": 2}


def _unshard(g, name):
    ax = SHARD_AXIS[name.rstrip("01")]
    g = jnp.moveaxis(g, 0, ax)
    s = g.shape
    return g.reshape(s[:ax] + (s[ax] * s[ax + 1],) + s[ax + 2:])


def _to_shards(full, name):
    ax = SHARD_AXIS[name.rstrip("01")]
    s = full.shape
    g = full.reshape(s[:ax] + (NSH, s[ax] // NSH) + s[ax + 1:])
    return jnp.moveaxis(g, ax, 0)


def _rec_pad_cols(w):
    z6 = jnp.zeros(w.shape[:-1] + (122,), w.dtype)
    return jnp.concatenate([w[..., 256:3328], w[..., 0:256], w[..., 3328:3334], z6, w[..., 3334:3340], z6], axis=-1)


def _rec_unpad_cols(g):
    return jnp.concatenate([g[..., 3072:3328], g[..., 0:3072], g[..., 3328:3334], g[..., 3456:3462]], axis=-1)


def _ffn_fwd(x1, nf, sc, sh, gate, w_up, conv, w_dn, tag, rides=()):
    rides = list(rides) + [None, None]
    h2 = modulate_fwd(x1, nf, sc, sh, f"{tag}_mod2_fwd")
    up = mm(h2, w_up, name=f"{tag}_ffn_up", out_dtypes=(bf16,), ride=rides[0])
    up, got0 = up if rides[0] else (up, None)
    act = ffn_act_fwd(up, conv, f"{tag}_ffn_act_fwd")
    res = mm(act, w_dn, name=f"{tag}_ffn_down", out_dtypes=(f32, f32), epi=_resid_epi, epi_mn=[x1], epi_n=[gate],
             ride=rides[1])
    return res[1], (h2, up, act, res[0]), (got0, res[2] if rides[1] else None)


def _ffn_bwd(dx, x1, nf, sc, sh, gate, w_up, conv, w_dn, saved, tag, rides=()):
    rides = list(rides) + [None, None, None]
    take = lambda res, r: res if r else (res, None)
    h2, up, act, f = saved
    df, dgate = resid_bwd(dx, f, gate, f"{tag}_res2_bwd")
    dact = mm(df, w_dn, tb=True, name=f"{tag}_ffn_dact", out_dtypes=(bf16,))
    dw_dn, got0 = take(mm(act, df, ta=True, name=f"{tag}_ffn_dwdown", out_dtypes=GRAD_WIRE, ride=rides[0]), rides[0])
    dup, dconv = ffn_act_conv_bwd(up, conv, dact, f"{tag}_ffn_act_conv_bwd")
    dw_up, got1 = take(mm(h2, dup, ta=True, name=f"{tag}_ffn_dwup", out_dtypes=GRAD_WIRE, ride=rides[1]), rides[1])
    if callable(rides[2]):
        rides[2] = rides[2](dw_dn)
    dh2, got2 = take(mm(dup, w_up, tb=True, name=f"{tag}_ffn_dh", ride=rides[2]), rides[2])
    dx, dnf, dsc, dsh = modulate_bwd(x1, nf, sc, sh, dh2, dx, f"{tag}_mod2_bwd")
    grads = dict(nf=dnf, sc=dsc, sh=dsh, gate=dgate, w_up=dw_up, conv=dconv[:FFN_CONV], w_dn=dw_dn)
    return dx, grads, (got0, got1, got2)


def kernel(x, c, ada_w, ada_b, norm_mix, norm_ffn, attn_w_in, attn_q_norm_a, attn_k_norm_a, attn_q_norm_b, attn_k_norm_b, attn_sinks, attn_w_out, rec_w_in, s5_lambda_re, s5_lambda_im, s5_log_dt, s5_b_re, s5_b_im, s5_c_re, s5_c_im, s5_d, s5_glu_w, s5_glu_b, dn_conv, dn_a_log, dn_dt_bias, dn_out_norm, rec_w_out, ffn_w_up, ffn_conv, ffn_w_down, loss_target, m_ada_w, m_ada_b, m_norm_mix, m_norm_ffn, m_attn_w_in, m_attn_q_norm_a, m_attn_k_norm_a, m_attn_q_norm_b, m_attn_k_norm_b, m_attn_sinks, m_attn_w_out, m_rec_w_in, m_s5_lambda_re, m_s5_lambda_im, m_s5_log_dt, m_s5_b_re, m_s5_b_im, m_s5_c_re, m_s5_c_im, m_s5_d, m_s5_glu_w, m_s5_glu_b, m_dn_conv, m_dn_a_log, m_dn_dt_bias, m_dn_out_norm, m_rec_w_out, m_ffn_w_up, m_ffn_conv, m_ffn_w_down, v_ada_w, v_ada_b, v_norm_mix, v_norm_ffn, v_attn_w_in, v_attn_q_norm_a, v_attn_k_norm_a, v_attn_q_norm_b, v_attn_k_norm_b, v_attn_sinks, v_attn_w_out, v_rec_w_in, v_s5_lambda_re, v_s5_lambda_im, v_s5_log_dt, v_s5_b_re, v_s5_b_im, v_s5_c_re, v_s5_c_im, v_s5_d, v_s5_glu_w, v_s5_glu_b, v_dn_conv, v_dn_a_log, v_dn_dt_bias, v_dn_out_norm, v_rec_w_out, v_ffn_w_up, v_ffn_conv, v_ffn_w_down):
    args = (ada_w, ada_b, norm_mix, norm_ffn, attn_w_in, attn_q_norm_a, attn_k_norm_a, attn_q_norm_b, attn_k_norm_b, attn_sinks, attn_w_out, rec_w_in, s5_lambda_re, s5_lambda_im, s5_log_dt, s5_b_re, s5_b_im, s5_c_re, s5_c_im, s5_d, s5_glu_w, s5_glu_b, dn_conv, dn_a_log, dn_dt_bias, dn_out_norm, rec_w_out, ffn_w_up, ffn_conv, ffn_w_down)
    ms = (m_ada_w, m_ada_b, m_norm_mix, m_norm_ffn, m_attn_w_in, m_attn_q_norm_a, m_attn_k_norm_a, m_attn_q_norm_b, m_attn_k_norm_b, m_attn_sinks, m_attn_w_out, m_rec_w_in, m_s5_lambda_re, m_s5_lambda_im, m_s5_log_dt, m_s5_b_re, m_s5_b_im, m_s5_c_re, m_s5_c_im, m_s5_d, m_s5_glu_w, m_s5_glu_b, m_dn_conv, m_dn_a_log, m_dn_dt_bias, m_dn_out_norm, m_rec_w_out, m_ffn_w_up, m_ffn_conv, m_ffn_w_down)
    vs = (v_ada_w, v_ada_b, v_norm_mix, v_norm_ffn, v_attn_w_in, v_attn_q_norm_a, v_attn_k_norm_a, v_attn_q_norm_b, v_attn_k_norm_b, v_attn_sinks, v_attn_w_out, v_rec_w_in, v_s5_lambda_re, v_s5_lambda_im, v_s5_log_dt, v_s5_b_re, v_s5_b_im, v_s5_c_re, v_s5_c_im, v_s5_d, v_s5_glu_w, v_s5_glu_b, v_dn_conv, v_dn_a_log, v_dn_dt_bias, v_dn_out_norm, v_rec_w_out, v_ffn_w_up, v_ffn_conv, v_ffn_w_down)
    W = dict(zip(W_NAMES, args))
    Mo = dict(zip(W_NAMES, ms))
    Vo = dict(zip(W_NAMES, vs))
    xi, yi, ci = lax.axis_index("x"), lax.axis_index("y"), lax.axis_index("c")
    shard = 2 * xi + yi
    me8 = 4 * xi + 2 * yi + ci
    xs = x[0]
    target = loss_target[0]
    L, D = xs.shape

    XY = ("x", "y")
    wparts = [
        [("attn_w_in", attn_w_in), ("attn_w_out", attn_w_out)],
        [("rec_w_in", rec_w_in), ("rec_w_out", rec_w_out)],
        [("ffn_w_up1", ffn_w_up[1:2]), ("ffn_w_down1", ffn_w_down[1:2])],
        [("ffn_w_up0", ffn_w_up[0:1]), ("ffn_w_down0", ffn_w_down[0:1])],
    ]
    wpack = [_pack([a for _, a in p], 1024, 16, bf16) for p in wparts]
    Wf = {}

    def unpack_weights(gathered, part):
        flat = gathered.reshape(NSH, -1)
        off = 0
        for n, a in part:
            sz = int(np.prod(a.shape))
            Wf[n] = _unshard(flat[:, off:off + sz].reshape((NSH,) + a.shape), n)[0]
            off += sz

    unpack_weights(all_gather(wpack[0], XY, "gather_w0"), wparts[0])

    sflat = _pack([c] + [W[n] for n in SMALL_SHARDED], 1024, 8, f32)
    s8 = all_gather(sflat, AXES, "gather_small")
    s8f = s8.reshape(8, -1)
    c_all = s8f[:, :D]
    Ws = {}
    off = D
    for n in SMALL_SHARDED:
        sz = int(np.prod(W[n].shape))
        Ws[n] = _unshard(s8f[0::2, off:off + sz].reshape((NSH,) + W[n].shape), n)
        off += sz

    modp = ada_fwd(c_all, ada_w)
    modg = all_gather(modp, ("x", "y"), "gather_mod")
    mod_all = jnp.moveaxis(modg, 0, 2).reshape(2, 8, -1) + ada_b[:, None, :]
    mod = lax.dynamic_slice(mod_all, (0, me8, 0), (2, 1, mod_all.shape[2]))[:, 0, :]
    mods = [[mod[l:l + 1, j * D:(j + 1) * D] for j in range(6)] for l in range(2)]

    sh1, sc1, g1, sh2, sc2, g2_ = mods[0]
    nm0, nf0 = norm_mix[0:1], norm_ffn[0:1]
    sinkb = jnp.repeat(attn_sinks[0], HEAD_DIM)[None]
    h0 = modulate_fwd(xs, nm0, sc1, sh1, "l0_mod1_fwd")
    hin0 = mm(h0, Wf["attn_w_in"], name="l0_in_proj")
    ocat, att_saved, got = attention_fwd(hin0, attn_q_norm_a, attn_k_norm_a, attn_q_norm_b, attn_k_norm_b, sinkb,
                                         ride=(wpack[3], XY, "gather"))
    unpack_weights(got, wparts[3])
    y0, x1 = mm(ocat, Wf["attn_w_out"], name="l0_out_proj", out_dtypes=(f32, f32), epi=_resid_epi,
                epi_mn=[xs], epi_n=[g1])
    x2, ffn0_saved, got = _ffn_fwd(x1, nf0, sc2, sh2, g2_, Wf["ffn_w_up0"], Ws["ffn_conv"][0], Wf["ffn_w_down0"], "l0",
                                   rides=[(wpack[1], XY, "gather"), (wpack[2], XY, "gather")])
    unpack_weights(got[0], wparts[1])
    unpack_weights(got[1], wparts[2])
    rec_w_in_p = _rec_pad_cols(Wf["rec_w_in"])

    th1, tc1, t1, th2, tc2, t2 = mods[1]
    nm1, nf1 = norm_mix[1:2], norm_ffn[1:2]
    pad128 = lambda a: jnp.pad(a, ((0, 0), (0, 128 - a.shape[1])))
    s5p = dict(lr=s5_lambda_re[0], li=s5_lambda_im[0], ldt=s5_log_dt[0][:, None], b_re=s5_b_re[0], b_im=s5_b_im[0],
               c_re=s5_c_re[0], c_im=s5_c_im[0], d=Ws["s5_d"], gw=Ws["s5_glu_w"][0], gb=Ws["s5_glu_b"])
    dnp = dict(conv=Ws["dn_conv"][0], alog=pad128(dn_a_log), dtb=pad128(dn_dt_bias), onorm=dn_out_norm)
    h1 = modulate_fwd(x2, nm1, tc1, th1, "l1_mod1_fwd")
    hin1 = mm(h1, rec_w_in_p, name="l1_in_proj")
    yc, s5_saved = s5_fwd(hin1, s5p)
    ycat, dn_saved = deltanet_fwd(hin1, dnp, yc)
    y1, x3 = mm(ycat, Wf["rec_w_out"], name="l1_out_proj", out_dtypes=(f32, f32), epi=_resid_epi,
                epi_mn=[x2], epi_n=[t1])
    x4, ffn1_saved, _ = _ffn_fwd(x3, nf1, tc2, th2, t2, Wf["ffn_w_up1"], Ws["ffn_conv"][1], Wf["ffn_w_down1"], "l1")

    dx, sse = loss_fwd_bwd(x4, target)
    loss = lax.psum(0.5 * sse[0, 0] / D, AXES)

    dx, gf1, _ = _ffn_bwd(dx, x3, nf1, tc2, th2, t2, Wf["ffn_w_up1"], Ws["ffn_conv"][1], Wf["ffn_w_down1"], ffn1_saved, "l1")
    dy1, dt1 = resid_bwd(dx, y1, t1, "l1_res1_bwd")
    dycat = mm(dy1, Wf["rec_w_out"], tb=True, name="l1_dycat")
    dw_rec_out = mm(ycat, dy1, ta=True, name="l1_dwout", out_dtypes=GRAD_WIRE)
    du_skip, du_b, s5g = s5_bwd(hin1, s5p, s5_saved, dycat)
    dx_qkv, dz, dab, dng = deltanet_bwd(hin1, dnp, dn_saved, dycat)
    dhin1 = rec_assemble(dx_qkv, dz, du_skip, du_b, dab)
    dw_rec_in = _rec_unpad_cols(mm(h1, dhin1, ta=True, name="l1_dwin", out_dtypes=GRAD_WIRE))
    dh1 = mm(dhin1, rec_w_in_p, tb=True, name="l1_dh")
    dx, dnm1, dtc1, dth1 = modulate_bwd(x2, nm1, tc1, th1, dh1, dx, "l1_mod1_bwd")

    def grad_part(items):
        flat = jnp.concatenate([_to_shards(g, n).reshape(NSH, -1) for n, g in items], axis=1)
        unit = 256 * 1024
        npad = -(-flat.shape[1] // unit) * unit
        return jnp.pad(flat, ((0, 0), (0, npad - flat.shape[1]))).reshape(NSH, npad // 1024, 1024)

    w_dn1 = gf1["w_dn"][None]
    part2 = lambda dw_dn0: (grad_part([("ffn_w_down1", w_dn1), ("ffn_w_down0", dw_dn0[None])]), XY, "exchange")
    gparts = [[("rec_w_in", dw_rec_in[None]), ("rec_w_out", dw_rec_out[None])], [("ffn_w_up1", gf1["w_up"][None])]]
    dx, gf0, gq = _ffn_bwd(dx, x1, nf0, sc2, sh2, g2_, Wf["ffn_w_up0"], Ws["ffn_conv"][0], Wf["ffn_w_down0"], ffn0_saved,
                           "l0", rides=[(grad_part(gparts[0]), XY, "exchange"), (grad_part(gparts[1]), XY, "exchange"), part2])
    gparts.append([("ffn_w_down1", w_dn1), ("ffn_w_down0", gf0["w_dn"][None])])
    dy0, dg1 = resid_bwd(dx, y0, g1, "l0_res1_bwd")
    dcat = mm(dy0, Wf["attn_w_out"], tb=True, name="l0_dcat")
    dw_attn_out = mm(ocat, dy0, ta=True, name="l0_dwout", out_dtypes=GRAD_WIRE)
    gparts += [[("ffn_w_up0", gf0["w_up"][None])], [("attn_w_out", dw_attn_out[None])]]
    chip_sum = lambda qs, i0: jnp.concatenate([sum_slots(q, f"sum_chips{i0 + i}", bf16) for i, q in enumerate(qs)], axis=0)
    dhin0, dwqa, dwka, dwqb, dwkb, dsinkb, gots = attention_bwd(
        hin0, attn_q_norm_a, attn_k_norm_a, attn_q_norm_b, attn_k_norm_b, sinkb, att_saved, dcat,
        rides={"a": (grad_part(gparts[3]), XY, "exchange"), 1: (chip_sum(gq, 0), ("c",), "gather"),
               4: (grad_part(gparts[4]), XY, "exchange"), 16: lambda g: (chip_sum([g["a"]], 3), ("c",), "gather")})
    dw_attn_in = mm(h0, dhin0, ta=True, name="l0_dwin", out_dtypes=GRAD_WIRE)
    gparts.append([("attn_w_in", dw_attn_in[None])])
    dh0, gq5 = mm(dhin0, Wf["attn_w_in"], tb=True, name="l0_dh", ride=(grad_part(gparts[5]), XY, "exchange"))
    grad_x, dnm0, dsc1, dsh1 = modulate_bwd(xs, nm0, sc1, sh1, dh0, dx, "l0_mod1_bwd")

    dmod = jnp.concatenate([
        jnp.concatenate([dsh1, dsc1, dg1, gf0["sh"], gf0["sc"], gf0["gate"]], axis=1),
        jnp.concatenate([dth1, dtc1, dt1, gf1["sh"], gf1["sc"], gf1["gate"]], axis=1)], axis=0)
    gl = {
        "ada_b": dmod,
        "norm_mix": jnp.concatenate([dnm0, dnm1], axis=0),
        "norm_ffn": jnp.concatenate([gf0["nf"], gf1["nf"]], axis=0),
        "attn_q_norm_a": dwqa, "attn_k_norm_a": dwka, "attn_q_norm_b": dwqb, "attn_k_norm_b": dwkb,
        "attn_sinks": dsinkb[:, ::HEAD_DIM],
        "s5_lambda_re": s5g["lr"][None], "s5_lambda_im": s5g["li"][None], "s5_log_dt": s5g["ldt"][:, 0][None],
        "s5_b_re": s5g["b_re"][None], "s5_b_im": s5g["b_im"][None], "s5_c_re": s5g["c_re"][None],
        "s5_c_im": s5g["c_im"][None],
        "dn_a_log": dng["alog"][:, :DN_HEADS], "dn_dt_bias": dng["dtb"][:, :DN_HEADS], "dn_out_norm": dng["onorm"],
        "s5_d": s5g["d"], "s5_glu_w": s5g["gw"][None], "s5_glu_b": s5g["gb"], "dn_conv": dng["conv"][None],
        "ffn_conv": jnp.stack([gf0["conv"], gf1["conv"]]),
    }

    small_names = SMALL_REPL + SMALL_SHARDED
    gs = _pack([gl[n] for n in small_names], 128, 256, f32)
    gs8 = all_gather(gs, AXES, "gather_small_grads")
    gsum = sum_slots(gs8, "sum_small_grads")
    full_shapes = [gl[n].shape for n in small_names]
    gfull = dict(zip(small_names, _unpack(gsum, full_shapes)))
    dmod_all = gs8.reshape(8, -1)[:, :2 * 6 * D].reshape(8, 2, 6 * D)
    ncol = ada_w.shape[2]
    dmod_sh = jnp.moveaxis(lax.dynamic_slice(dmod_all, (0, 0, shard * ncol), (8, 2, ncol)), 0, 1)
    grads = {"ada_w": ada_bwd(c_all, dmod_sh)}
    for n in SMALL_REPL:
        grads[n] = gfull[n]
    for n in SMALL_SHARDED:
        sh_all = _to_shards(gfull[n], n)
        grads[n] = lax.dynamic_slice(sh_all, (shard,) + (0,) * (sh_all.ndim - 1), (1,) + sh_all.shape[1:])[0]

    gq = list(gq) + [gots["a"], gots[4], gq5]
    gc45 = all_gather(chip_sum(gq[4:], 4), ("c",), "gather_grad_c")
    gsh = jnp.concatenate([sum_slots(gots[1], "sum_pair012"), sum_slots(gots[16], "sum_pair3"),
                           sum_slots(gc45, "sum_pair45")], axis=0)
    row, got = 0, {}
    for part, q in zip(gparts, gq):
        flat = gsh[row:row + q.shape[1]].reshape(-1)
        row += q.shape[1]
        off = 0
        for n, g in part:
            sz = g.size // NSH
            got[n] = flat[off:off + sz].reshape((1,) + g.shape[1:-2] + _to_shards(g, n).shape[-2:])
            off += sz
    for n in ("attn_w_in", "attn_w_out", "rec_w_in", "rec_w_out"):
        grads[n] = got[n]
    grads["ffn_w_up"] = jnp.concatenate([got["ffn_w_up0"], got["ffn_w_up1"]], axis=0)
    grads["ffn_w_down"] = jnp.concatenate([got["ffn_w_down0"], got["ffn_w_down1"]], axis=0)

    delta, new_m, new_v = {}, {}, {}

    def as2d(a):
        return a.reshape(-1, a.shape[-1])

    for n in ["ada_w"] + BIG:
        d_, m_, v_ = adamw(as2d(W[n]), as2d(grads[n]), as2d(Mo[n]), as2d(Vo[n]), f"adamw_{n}")
        delta[n], new_m[n], new_v[n] = d_.reshape(W[n].shape), m_.reshape(W[n].shape), v_.reshape(W[n].shape)
    pk = lambda dd: _pack([dd[n] for n in small_names], 128, 256, f32)
    d_, m_, v_ = adamw(pk(W), pk(grads), pk(Mo), pk(Vo), "adamw_small")
    shp = [W[n].shape for n in small_names]
    for dst, src in ((delta, d_), (new_m, m_), (new_v, v_)):
        dst.update(zip(small_names, _unpack(src, shp)))

    return (loss, grad_x[None], *[grads[n] for n in W_NAMES], *[delta[n] for n in W_NAMES],
            *[new_m[n] for n in W_NAMES], *[new_v[n] for n in W_NAMES])
```

```python
import functools
import math

import numpy as np
import jax
import jax.numpy as jnp
from jax import lax
from jax.experimental import pallas as pl
from jax.experimental.pallas import tpu as pltpu

f32 = jnp.float32
bf16 = jnp.bfloat16
HI = lax.Precision.HIGHEST
MESH = pl.DeviceIdType.MESH

HEAD_DIM = 64
BLOCK = 128
A_Q_HEADS = 8
A_KV_HEADS = 2
A_WINDOW = 128
B_HEADS = 8
B_BRANCHES = ((128, 1), (512, 4), (2048, 16))
N_ATTN_HEADS = 16
ATTN_IN = 2304
S5_GROUP = 16
S5_GROUPS = 16
S5_WIDTH = 256
S5_STATE = 64
DN_HEADS = 6
DN_DK = 128
DN_CONV = 4
DN_CHUNK = 64
REC_IN = 3340
REC_PAD = 3584
FFN_CONV = 3
EPS = 1e-6
ADAM_LR = 0.001
ADAM_B1 = 0.9
ADAM_B2 = 0.999
ADAM_EPS = 1e-08
ADAM_WD = 0.01
ADAM_STEP = 10

LANE = 128
SUBLANE = 8
VMEM_LIMIT = 52 * 1024 * 1024
MM_FULL_K = 5632
MM_VMEM_BUDGET = 40 * 1024 * 1024


def _cp(*sem):
    return pltpu.CompilerParams(dimension_semantics=sem, vmem_limit_bytes=VMEM_LIMIT)


def _pick(dim, cap, unit=LANE):
    for t in (2048, 1024, 768, 512, 384, 256, 128, 64, 32, 16, 8):
        if t <= cap and t % unit == 0 and dim % t == 0:
            return t
    return dim


def _dot(a, b, dims=(((1,), (0,)), ((), ())), precision=None):
    return lax.dot_general(a, b, dims, precision=precision, preferred_element_type=f32)


NN = (((1,), (0,)), ((), ()))
NT = (((1,), (1,)), ((), ()))
TN = (((0,), (0,)), ((), ()))


def mm(a, b, *, name, ta=False, tb=False, a_win=None, b_win=None, out_dtypes=(f32,),
       epi=None, epi_mn=(), epi_n=(), tm_cap=1024, tn_cap=8192, tk_cap=None, ride=None):
    coll = _Coll(*ride) if ride else None
    a0, a1 = a.shape
    b0, b1 = b.shape
    aw = a_win or (0, a1)
    bw = b_win or (0, b1)
    if ta:
        K, M = a0, aw[1]
    else:
        M, K = a0, aw[1]
    if tb:
        N, K2 = b0, bw[1]
    else:
        K2, N = b0, bw[1]
    assert K == K2, (a.shape, b.shape, ta, tb, a_win, b_win)
    if tk_cap is None:
        tk_cap = K if K <= MM_FULL_K else 2048
    tk = _pick(K, tk_cap, SUBLANE if (ta and not tb) else LANE)
    nk = K // tk
    sa, sb = a.dtype.itemsize, b.dtype.itemsize
    so = sum(jnp.dtype(d).itemsize for d in out_dtypes)
    n_mn, n_n, n_out = len(epi_mn), len(epi_n), len(out_dtypes)

    def vmem(tm_, tn_):
        return 2 * (tm_ * tk * sa + tk * tn_ * sb + tm_ * tn_ * (so + 4 * n_mn)) + 2 * tm_ * tn_ * 4

    best = None
    for tm_ in (t for t in (1024, 512, 256, 128) if M % t == 0 and (not ta or aw[0] % t == 0)):
        for tn_ in (t for t in (N, N // 2, 1024, 768, 512, 384, 256, 128)
                    if t % LANE == 0 and N % t == 0 and (tb or bw[0] % t == 0)):
            if tm_ <= tm_cap and tn_ <= max(tn_cap, 0) and vmem(tm_, tn_) <= MM_VMEM_BUDGET:
                if best is None or (tm_ * tn_, tn_) > (best[0] * best[1], best[1]):
                    best = (tm_, tn_)
    assert best is not None, (name, M, N, K)
    tm, tn = best
    b_outer = tk * tn * sb > tm * tk * sa

    def ix(f):
        if b_outer:
            return lambda j, i, k: f(i, j, k)
        return f

    if ta:
        mo = aw[0] // tm
        a_spec = pl.BlockSpec((tk, tm), ix(lambda i, j, k: (k, i + mo)))
    else:
        assert aw[0] % tk == 0
        ko = aw[0] // tk
        a_spec = pl.BlockSpec((tm, tk), ix(lambda i, j, k: (i, k + ko)))
    if tb:
        assert bw[0] % tk == 0
        kob = bw[0] // tk
        b_spec = pl.BlockSpec((tn, tk), ix(lambda i, j, k: (j, k + kob)))
    else:
        no = bw[0] // tn
        b_spec = pl.BlockSpec((tk, tn), ix(lambda i, j, k: (k, j + no)))
    dims = (((0 if ta else 1,), (1 if tb else 0,)), ((), ()))

    gi, gj = M // tm, N // tn
    grid = (gj, gi, nk) if b_outer else (gi, gj, nk)

    def body(a_ref, b_ref, *rest):
        mn_refs = rest[:n_mn]
        n_refs = rest[n_mn:n_mn + n_n]
        o0 = n_mn + n_n
        out_refs = rest[o0:o0 + n_out]

        def finish(r):
            if epi is None:
                outs = (r,)
            else:
                outs = epi(r, *[m[...] for m in mn_refs], *[v[...] for v in n_refs])
            for o_ref, o in zip(out_refs, outs):
                o_ref[...] = o.astype(o_ref.dtype)

        part = _dot(a_ref[...].astype(bf16), b_ref[...].astype(bf16), dims)
        if nk == 1:
            finish(part)
        else:
            acc = rest[o0 + n_out]
            k = pl.program_id(2)

            @pl.when(k == 0)
            def _():
                acc[...] = part

            @pl.when(k > 0)
            def _():
                acc[...] += part

            @pl.when(k == nk - 1)
            def _():
                finish(acc[...])

    mn_spec = pl.BlockSpec((tm, tn), ix(lambda i, j, k: (i, j)))
    n_spec = pl.BlockSpec((1, tn), ix(lambda i, j, k: (0, j)))
    outs = _ride_call(
        body, coll, ride, grid=grid,
        in_specs=[a_spec, b_spec] + [mn_spec] * n_mn + [n_spec] * n_n, out_specs=[mn_spec] * n_out,
        out_shape=[jax.ShapeDtypeStruct((M, N), d) for d in out_dtypes],
        scratch_shapes=[pltpu.VMEM((tm, tn), f32)] if nk > 1 else [],
        semantics=("parallel", "parallel", "arbitrary"), name=name, args=[a, b, *epi_mn, *epi_n])
    return outs[0] if len(outs) == 1 else tuple(outs)


def rowwise(fn, *, name, L, tm, rows=(), consts=(), outs=(), sums=()):
    nb = L // tm
    in_specs = []
    arrs = []
    for arr, start, width, kind in rows:
        assert start % width == 0, (name, start, width)
        co = start // width
        hr = SUBLANE * (4 // arr.dtype.itemsize)
        hb = tm // hr
        if kind == "cur":
            in_specs.append(pl.BlockSpec((tm, width), lambda i, co=co: (i, co)))
        elif kind == "prev":
            in_specs.append(pl.BlockSpec((hr, width), lambda i, co=co, hb=hb: (jnp.maximum(i * hb - 1, 0), co)))
        else:
            last = L // hr - 1
            in_specs.append(pl.BlockSpec((hr, width), lambda i, co=co, hb=hb, last=last:
                                         (jnp.minimum((i + 1) * hb, last), co)))
        arrs.append(arr)
    for cst in consts:
        assert cst.ndim == 2
        in_specs.append(pl.BlockSpec(cst.shape, lambda i: (0, 0)))
        arrs.append(cst)
    n_rows, n_c, n_o, n_s = len(rows), len(consts), len(outs), len(sums)
    out_specs = [pl.BlockSpec((tm, w), lambda i: (i, 0)) for w, _ in outs]
    out_specs += [pl.BlockSpec(s, lambda i: (0, 0)) for s in sums]
    out_shape = [jax.ShapeDtypeStruct((L, w), d) for w, d in outs]
    out_shape += [jax.ShapeDtypeStruct(s, f32) for s in sums]

    def body(*refs):
        i = pl.program_id(0)
        vals = [r[...] for r in refs[:n_rows + n_c]]
        res = fn(i, nb, *vals)
        if not isinstance(res, (tuple, list)):
            res = (res,)
        o_refs = refs[n_rows + n_c:n_rows + n_c + n_o]
        s_refs = refs[n_rows + n_c + n_o:]
        for o_ref, o in zip(o_refs, res[:n_o]):
            o_ref[...] = o.astype(o_ref.dtype)
        if n_s:
            @pl.when(i == 0)
            def _():
                for s_ref in s_refs:
                    s_ref[...] = jnp.zeros_like(s_ref)

            for s_ref, s in zip(s_refs, res[n_o:]):
                s_ref[...] += s

    res = pl.pallas_call(
        body,
        grid=(nb,),
        in_specs=in_specs,
        out_specs=out_specs,
        out_shape=out_shape,
        compiler_params=_cp("arbitrary" if n_s else "parallel"),
        name=name,
    )(*arrs)
    return res[0] if len(res) == 1 else tuple(res)


def _shift_down(x, prev8, k):
    cat = jnp.concatenate([prev8, x], axis=0)
    return pltpu.roll(cat, k, 0)[prev8.shape[0]:, :]


def _shift_up(x, next8, k):
    n = x.shape[0]
    cat = jnp.concatenate([x, next8], axis=0)
    return pltpu.roll(cat, n + next8.shape[0] - k, 0)[:n, :]


def _colsum(x):
    return jnp.sum(x, axis=0, keepdims=True)


def _silu(x):
    return x * jax.nn.sigmoid(x)


def _modulate_fn(x, nw, sc, sh):
    r = lax.rsqrt(jnp.mean(x * x, axis=-1, keepdims=True) + EPS)
    return (x * r * nw) * (1.0 + sc) + sh


def modulate_fwd(x, nw, sc, sh, name):
    L, D = x.shape

    def fn(i, nb, xt, nwv, scv, shv):
        return _modulate_fn(xt, nwv, scv, shv)

    return rowwise(fn, name=name, L=L, tm=_pick(L, 512, SUBLANE), rows=[(x, 0, D, "cur")],
                   consts=[nw, sc, sh], outs=[(D, bf16)])


def modulate_bwd(x, nw, sc, sh, dh, dx_in, name):
    L, D = x.shape

    def fn(i, nb, xt, dht, dxt, nwv, scv, shv):
        _, vjp = jax.vjp(_modulate_fn, xt, nwv, scv, shv)
        dx, dnw, dsc, dsh = vjp(dht)
        return dxt + dx, dnw, dsc, dsh

    return rowwise(fn, name=name, L=L, tm=_pick(L, 256, SUBLANE),
                   rows=[(x, 0, D, "cur"), (dh, 0, D, "cur"), (dx_in, 0, D, "cur")],
                   consts=[nw, sc, sh], outs=[(D, f32)], sums=[(1, D)] * 3)


def resid_bwd(dx, y, g, name):
    L, D = dx.shape

    def fn(i, nb, dxt, yt, gv):
        return dxt * gv, _colsum(dxt * yt)

    return rowwise(fn, name=name, L=L, tm=_pick(L, 512, SUBLANE),
                   rows=[(dx, 0, D, "cur"), (y, 0, D, "cur")], consts=[g],
                   outs=[(D, bf16)], sums=[(1, D)])


def _resid_epi(acc, xt, gv):
    return acc, xt + gv * acc


def _stack_rows(rows, n=SUBLANE):
    c = rows[0].shape[1]
    ridx = lax.broadcasted_iota(jnp.int32, (n, c), 0)
    out = jnp.zeros((n, c), f32)
    for j, r in enumerate(rows):
        out = out + jnp.where(ridx == j, r, 0.0)
    return out


def _conv_causal(x, prev8, w):
    W = w.shape[0]
    y = x * w[W - 1:W, :]
    for j in range(W - 1):
        y = y + _shift_down(x, prev8, W - 1 - j) * w[j:j + 1, :]
    return y


def _conv_causal_bwd_x(dy, next8, w):
    W = w.shape[0]
    dx = dy * w[W - 1:W, :]
    for j in range(W - 1):
        dx = dx + _shift_up(dy, next8, W - 1 - j) * w[j:j + 1, :]
    return dx


def _conv_causal_bwd_w(dy, x, prev8, W):
    rows = [_colsum(dy * _shift_down(x, prev8, W - 1 - j)) for j in range(W - 1)]
    rows.append(_colsum(dy * x))
    return _stack_rows(rows)


def ffn_act_fwd(up, conv_w, name):
    L, F2 = up.shape
    F = F2 // 2

    def fn(i, nb, u, p8, w):
        c = _conv_causal(u.astype(f32), p8.astype(f32) * (i > 0).astype(f32), w)
        return _silu(c[:, :F]) * c[:, F:]

    return rowwise(fn, name=name, L=L, tm=_pick(L, 128, SUBLANE),
                   rows=[(up, 0, F2, "cur"), (up, 0, F2, "prev")], consts=[conv_w], outs=[(F, bf16)])


def ffn_act_conv_bwd(up, conv_w, dact, name):
    L, F2 = up.shape
    F = F2 // 2
    W = conv_w.shape[0]

    def fn(i, nb, u, da, p8, un8, dan8, w):
        tm, ext = u.shape[0], un8.shape[0]
        more = (i < nb - 1).astype(f32)
        u, da = u.astype(f32), da.astype(f32)
        p8 = p8.astype(f32) * (i > 0).astype(f32)
        c = _conv_causal(jnp.concatenate([u, un8.astype(f32) * more], axis=0), p8, w)
        dae = jnp.concatenate([da, dan8.astype(f32) * more], axis=0)
        a, b = c[:, :F], c[:, F:]
        sg = jax.nn.sigmoid(a)
        dc = jnp.concatenate([dae * b * (sg * (1.0 + a * (1.0 - sg))), dae * a * sg], axis=1)
        dx = dc[:tm] * w[W - 1:W, :]
        for j in range(W - 1):
            dx = dx + pltpu.roll(dc, tm + ext - (W - 1 - j), 0)[:tm] * w[j:j + 1, :]
        return dx, _conv_causal_bwd_w(dc[:tm], u, p8, W)

    return rowwise(fn, name=name, L=L, tm=_pick(L, 128, SUBLANE),
                   rows=[(up, 0, F2, "cur"), (dact, 0, F, "cur"), (up, 0, F2, "prev"), (up, 0, F2, "next"),
                         (dact, 0, F, "next")],
                   consts=[conv_w], outs=[(F2, bf16)], sums=[(SUBLANE, F2)])


ALIBI = [2.0 ** (-8.0 * (i + 1) / N_ATTN_HEADS) for i in range(N_ATTN_HEADS)]
NEG = -1e30


def _band_mask(n, d, max_dist):
    qi = lax.broadcasted_iota(jnp.int32, (BLOCK, 2 * BLOCK), 0)
    kj = lax.broadcasted_iota(jnp.int32, (BLOCK, 2 * BLOCK), 1)
    dist = BLOCK + qi - kj
    valid = (dist >= 0) & (dist <= max_dist) & ((n > 0) | (kj >= BLOCK))
    return valid, -(d * dist).astype(f32)


def _rms64(x, w):
    r = lax.rsqrt(jnp.mean(x * x, axis=-1, keepdims=True) + EPS)
    xh = x * r
    return xh * w, xh, r


def _rms64_bwd(dy, xh, r, w):
    t = dy * w
    dw = jnp.sum(jnp.sum(dy * xh, axis=0), axis=0, keepdims=True)
    return r * (t - xh * jnp.mean(t * xh, axis=-1, keepdims=True)), dw


class _Plan:
    def __init__(self, dilation, group_a, nq):
        self.d, self.nq = dilation, nq
        if group_a:
            self.P, self.nkv = 1, 1
            self.q0, self.k0, self.v0 = 0, 4, 5
            self.kv_of = lambda j: j // 4
            self.max_dist = A_WINDOW - 1
            slopes = ALIBI[:8]
        else:
            self.P, self.nkv = 4 // nq, nq
            self.q0, self.k0, self.v0 = 6, 10, 14
            self.kv_of = lambda j: j
            self.max_dist = BLOCK
            slopes = ALIBI[8:]
        self.hps = 2 * nq
        sl = np.repeat(np.asarray(slopes, np.float32), HEAD_DIM).reshape(self.P, 1, self.hps * HEAD_DIM)
        self.slopes = jnp.asarray(sl, f32)


def _rows(r, d):
    return pl.ds(r, BLOCK, stride=d) if d > 1 else pl.ds(0, BLOCK)


def _pairs(refs, rows):
    parts = []
    for ref in refs:
        blk = ref[rows, :]
        parts += [blk[:, :HEAD_DIM], blk[:, HEAD_DIM:]]
    return jnp.stack(parts)


def _pairs2(prev_refs, cur_refs, rows):
    parts = []
    for pr, cr in zip(prev_refs, cur_refs):
        blk = jnp.concatenate([pr[rows, :], cr[rows, :]], axis=0)
        parts += [blk[:, :HEAD_DIM], blk[:, HEAD_DIM:]]
    return jnp.stack(parts)


def _lane_pair(t, i):
    return jnp.concatenate([t[2 * i], t[2 * i + 1]], axis=1)


def _riding(body, coll, n_in, n_out, grid):
    if coll is None:
        return body

    def wrapped(*refs):
        ride_refs = (refs[n_in], refs[n_in + 1 + n_out]) + tuple(refs[-3:])
        inner = refs[:n_in] + refs[n_in + 1:n_in + 1 + n_out] + refs[n_in + 2 + n_out:-3]
        pid = [pl.program_id(t) for t in range(len(grid))]

        @pl.when(functools.reduce(jnp.logical_and, [p == 0 for p in pid]))
        def _():
            coll.start(*ride_refs)

        body(*inner)

        @pl.when(functools.reduce(jnp.logical_and, [p == g - 1 for p, g in zip(pid, grid)]))
        def _():
            coll.wait(*ride_refs)

    return wrapped


def _ride_call(body, coll, ride, *, grid, in_specs, out_specs, out_shape, scratch_shapes, semantics, name, args):
    hbm = [pl.BlockSpec(memory_space=pl.ANY)] if coll else []
    return pl.pallas_call(
        _riding(body, coll, len(in_specs), len(out_specs), grid), grid=grid,
        in_specs=list(in_specs) + hbm, out_specs=list(out_specs) + hbm,
        out_shape=list(out_shape) + ([coll.out_shape] if coll else []),
        scratch_shapes=list(scratch_shapes) + (coll.scratch if coll else []),
        compiler_params=_cp(*(["arbitrary"] * len(grid) if coll else semantics)), name=name,
    )(*args, *([ride[0]] if coll else []))


def attn2_fwd(hin, plan, wq, wk, name, ride=None):
    coll = _Coll(*ride) if ride else None
    L = hin.shape[0]
    d, nq, nkv, hps, P = plan.d, plan.nq, plan.nkv, plan.hps, plan.P
    R = BLOCK * d
    nb = L // R
    kv_of, max_dist = plan.kv_of, plan.max_dist
    gqa = 2 * nkv != hps

    def body(*refs):
        q_refs = refs[:nq]
        kp, kc = refs[nq:nq + nkv], refs[nq + nkv:nq + 2 * nkv]
        vp, vc = refs[nq + 2 * nkv:nq + 3 * nkv], refs[nq + 3 * nkv:nq + 4 * nkv]
        sl_ref, wq_ref, wk_ref, o_ref, lse_ref = refs[nq + 4 * nkv:nq + 4 * nkv + 5]
        o_refs = refs[nq + 4 * nkv + 5:2 * nq + 4 * nkv + 5]
        lse_refs = refs[2 * nq + 4 * nkv + 5:]
        n = pl.program_id(1)
        valid, negd = _band_mask(n, d, max_dist)
        slope = jnp.stack([sl_ref[0, :, j * 64:j * 64 + 1] for j in range(hps)])
        wqv, wkv = wq_ref[...], wk_ref[...]

        def residue(r, carry):
            rows = _rows(r, d)
            kn = _rms64(_pairs2(kp, kc, rows), wkv)[0].astype(bf16)
            v = _pairs2(vp, vc, rows).astype(bf16)
            if gqa:
                kn = jnp.stack([kn[kv_of(j)] for j in range(hps)])
                v = jnp.stack([v[kv_of(j)] for j in range(hps)])
            qn = _rms64(_pairs(q_refs, rows), wqv)[0].astype(bf16)
            s = _dot(qn, kn, BNT) * (HEAD_DIM ** -0.5) + slope * negd
            s = jnp.where(valid, s, NEG)
            m = jnp.max(s, axis=-1, keepdims=True)
            p = jnp.exp(s - m)
            l = jnp.sum(p, axis=-1, keepdims=True)
            o = _dot(p.astype(bf16), v, BNN) / l
            lse = jnp.broadcast_to(m + jnp.log(l), (hps, BLOCK, HEAD_DIM))
            for i in range(nq):
                o_refs[i][rows, :] = _lane_pair(o, i)
                lse_refs[i][rows, :] = _lane_pair(lse, i)
            return carry

        lax.fori_loop(0, d, residue, 0)
        for i in range(nq):
            o_ref[:, i * 128:(i + 1) * 128] = o_refs[i][...]
            lse_ref[:, i * 128:(i + 1) * 128] = lse_refs[i][...]

    col = lambda c0, i: (lambda p, n: (n, c0 + p * nq + i))
    prv = lambda c0, i: (lambda p, n: (jnp.maximum(n - 1, 0), c0 + p * nq + i))
    blk = lambda f: pl.BlockSpec((R, 128), f)
    in_specs = [blk(col(plan.q0, i)) for i in range(nq)]
    in_specs += [blk(prv(plan.k0, i)) for i in range(nkv)] + [blk(col(plan.k0, i)) for i in range(nkv)]
    in_specs += [blk(prv(plan.v0, i)) for i in range(nkv)] + [blk(col(plan.v0, i)) for i in range(nkv)]
    in_specs += [pl.BlockSpec((1, 1, hps * 64), lambda p, n: (p, 0, 0)),
                 pl.BlockSpec((1, 64), lambda p, n: (0, 0)), pl.BlockSpec((1, 64), lambda p, n: (0, 0))]
    wide = pl.BlockSpec((R, 128 * nq), lambda p, n: (n, p))
    return _ride_call(
        body, coll, ride, grid=(P, nb), in_specs=in_specs, out_specs=[wide, wide],
        out_shape=[jax.ShapeDtypeStruct((L, 512), f32)] * 2,
        scratch_shapes=[pltpu.VMEM((R, 128), f32)] * (2 * nq),
        semantics=("parallel", "parallel"), name=name,
        args=[hin] * (nq + 4 * nkv) + [plan.slopes, wq, wk])


def attn2_bwd(hin, plan, wq, wk, o, lse, do, dlse, dw0, name, ride=None):
    coll = _Coll(*ride) if ride else None
    L = hin.shape[0]
    d, nq, nkv, hps, P = plan.d, plan.nq, plan.nkv, plan.hps, plan.P
    R = BLOCK * d
    nb = L // R
    kv_of, max_dist = plan.kv_of, plan.max_dist
    nkh = 2 * nkv
    gqa = nkh != hps
    n_in = nq + 4 * nkv + 3 + 4 * nq + 2

    def body(*refs):
        q_refs = refs[:nq]
        kp, kc = refs[nq:nq + nkv], refs[nq + nkv:nq + 2 * nkv]
        vp, vc = refs[nq + 2 * nkv:nq + 3 * nkv], refs[nq + 3 * nkv:nq + 4 * nkv]
        b = nq + 4 * nkv
        sl_ref, wq_ref, wk_ref = refs[b:b + 3]
        b += 3
        o_refs, lse_refs = refs[b:b + nq], refs[b + nq:b + 2 * nq]
        do_refs, dlse_refs = refs[b + 2 * nq:b + 3 * nq], refs[b + 3 * nq:b + 4 * nq]
        dwq0_ref, dwk0_ref = refs[b + 4 * nq:b + 4 * nq + 2]
        dq_ref, dk_ref, dv_ref, dwq_ref, dwk_ref = refs[n_in:n_in + 5]
        sc = refs[n_in + 5:]
        dq_s, dk_s, dv_s = sc[:nq], sc[nq:nq + nkv], sc[nq + nkv:nq + 2 * nkv]
        ck, cv = sc[nq + 2 * nkv:nq + 3 * nkv], sc[nq + 3 * nkv:]
        pp = pl.program_id(0)
        n = pl.program_id(1)

        @pl.when((pp == 0) & (n == 0))
        def _():
            dwq_ref[...] = dwq0_ref[...]
            dwk_ref[...] = dwk0_ref[...]

        @pl.when(n == 0)
        def _():
            for c in (*ck, *cv):
                c[...] = jnp.zeros_like(c)

        @pl.when(n < nb)
        def _():
            valid, negd = _band_mask(n, d, max_dist)
            slope = jnp.stack([sl_ref[0, :, j * 64:j * 64 + 1] for j in range(hps)])
            wqv, wkv = wq_ref[...], wk_ref[...]
            hs = range(hps)

            def residue(r, carry):
                rows = _rows(r, d)
                kn_f, kh, rk = _rms64(_pairs2(kp, kc, rows), wkv)
                kn = kn_f.astype(bf16)
                v = _pairs2(vp, vc, rows).astype(bf16)
                if gqa:
                    kn = jnp.stack([kn[kv_of(j)] for j in hs])
                    v = jnp.stack([v[kv_of(j)] for j in hs])
                qn_f, qh, rq = _rms64(_pairs(q_refs, rows), wqv)
                qn = qn_f.astype(bf16)
                s = _dot(qn, kn, BNT) * (HEAD_DIM ** -0.5) + slope * negd
                p = jnp.where(valid, jnp.exp(s - _pairs(lse_refs, rows)[:, :, :1]), 0.0)
                do_h = _pairs(do_refs, rows)
                delta = jnp.sum(do_h * _pairs(o_refs, rows), axis=-1, keepdims=True)
                do_b = do_h.astype(bf16)
                dp = _dot(do_b, v, BNT)
                ds = (p * (dp - delta + _pairs(dlse_refs, rows)[:, :, :1])).astype(bf16)
                dqn = _dot(ds, kn, BNN) * (HEAD_DIM ** -0.5)
                dkn = _dot(ds, qn, BTN) * (HEAD_DIM ** -0.5)
                dvv = _dot(p.astype(bf16), do_b, BTN)
                if gqa:
                    grp = lambda t: jnp.stack([sum(t[j] for j in hs if kv_of(j) == h) for h in range(nkh)])
                    dkn, dvv = grp(dkn), grp(dvv)
                dq, dwq = _rms64_bwd(dqn, qh, rq, wqv)
                dk, dwk = _rms64_bwd(dkn, kh, rk, wkv)
                for i in range(nq):
                    dq_s[i][rows, :] = _lane_pair(dq, i)
                for i in range(nkv):
                    dk_s[i][rows, :] = ck[i][rows, :] + _lane_pair(dk[:, :BLOCK], i)
                    dv_s[i][rows, :] = cv[i][rows, :] + _lane_pair(dvv[:, :BLOCK], i)
                    ck[i][rows, :] = _lane_pair(dk[:, BLOCK:], i)
                    cv[i][rows, :] = _lane_pair(dvv[:, BLOCK:], i)
                return carry[0] + dwq, carry[1] + dwk

            zero = jnp.zeros((1, HEAD_DIM), f32)
            dwq_a, dwk_a = lax.fori_loop(0, d, residue, (zero, zero))
            dwq_ref[...] += dwq_a
            dwk_ref[...] += dwk_a
            for i in range(nq):
                dq_ref[:, i * 128:(i + 1) * 128] = dq_s[i][...]
            for i in range(nkv):
                dk_ref[:, i * 128:(i + 1) * 128] = dk_s[i][...]
                dv_ref[:, i * 128:(i + 1) * 128] = dv_s[i][...]

        @pl.when(n == nb)
        def _():
            for i in range(nkv):
                dk_ref[:, i * 128:(i + 1) * 128] = ck[i][...]
                dv_ref[:, i * 128:(i + 1) * 128] = cv[i][...]

    cl = lambda n: jnp.minimum(n, nb - 1)
    pv = lambda n: jnp.maximum(jnp.minimum(n, nb - 1) - 1, 0)
    col = lambda c0, i: (lambda p, n: (cl(n), c0 + p * nq + i))
    prv = lambda c0, i: (lambda p, n: (pv(n), c0 + p * nq + i))
    blk = lambda f: pl.BlockSpec((R, 128), f)
    w64 = pl.BlockSpec((1, 64), lambda p, n: (0, 0))
    in_specs = [blk(col(plan.q0, i)) for i in range(nq)]
    in_specs += [blk(prv(plan.k0, i)) for i in range(nkv)] + [blk(col(plan.k0, i)) for i in range(nkv)]
    in_specs += [blk(prv(plan.v0, i)) for i in range(nkv)] + [blk(col(plan.v0, i)) for i in range(nkv)]
    in_specs += [pl.BlockSpec((1, 1, hps * 64), lambda p, n: (p, 0, 0)), w64, w64]
    in_specs += [blk(col(0, i)) for i in range(nq)] * 4 + [w64, w64]
    kvw = 128 * nkv
    out_specs = [pl.BlockSpec((R, 128 * nq), lambda p, n: (cl(n), p)),
                 pl.BlockSpec((R, kvw), lambda p, n: (jnp.maximum(n - 1, 0), p)),
                 pl.BlockSpec((R, kvw), lambda p, n: (jnp.maximum(n - 1, 0), p)), w64, w64]
    same = lambda a: [a] * nq
    return _ride_call(
        body, coll, ride, grid=(P, nb + 1), in_specs=in_specs, out_specs=out_specs,
        out_shape=[jax.ShapeDtypeStruct((L, 512), f32), jax.ShapeDtypeStruct((L, kvw * P), f32),
                   jax.ShapeDtypeStruct((L, kvw * P), f32), jax.ShapeDtypeStruct((1, 64), f32),
                   jax.ShapeDtypeStruct((1, 64), f32)],
        scratch_shapes=[pltpu.VMEM((R, 128), f32)] * (nq + 4 * nkv),
        semantics=("arbitrary", "arbitrary"), name=name,
        args=[hin] * (nq + 4 * nkv) + [plan.slopes, wq, wk, *same(o), *same(lse), *same(do), *same(dlse), *dw0])


def _head_sum(x):
    c = x.shape[1]
    r = lax.broadcasted_iota(jnp.int32, (c, c), 0) // HEAD_DIM
    q = lax.broadcasted_iota(jnp.int32, (c, c), 1) // HEAD_DIM
    ones = (r == q).astype(bf16)
    hi, lo = _split(x)
    return _dot(hi, ones) + _dot(lo, ones)


def attn_merge_fwd(oa, la, obs, lbs, sinkb, name):
    L = oa.shape[0]

    def fn(i, nb, oa_t, la_t, o1, o2, o3, l1, l2, l3, sk):
        ya = oa_t * jax.nn.sigmoid(la_t - sk)
        m = jnp.maximum(jnp.maximum(l1, l2), l3)
        e1, e2, e3 = jnp.exp(l1 - m), jnp.exp(l2 - m), jnp.exp(l3 - m)
        yb = (e1 * o1 + e2 * o2 + e3 * o3) / (e1 + e2 + e3)
        return jnp.concatenate([ya, yb], axis=1)

    rows = [(a, 0, 512, "cur") for a in (oa, la, *obs, *lbs)]
    return rowwise(fn, name=name, L=L, tm=_pick(L, 256, SUBLANE), rows=rows, consts=[sinkb], outs=[(1024, bf16)])


def attn_merge_bwd(dcat, oa, la, obs, lbs, sinkb, name):
    L = oa.shape[0]

    def fn(i, nb, da, db, oa_t, la_t, o1, o2, o3, l1, l2, l3, sk):
        keep = jax.nn.sigmoid(la_t - sk)
        dla = _head_sum(da * oa_t) * keep * (1.0 - keep)
        m = jnp.maximum(jnp.maximum(l1, l2), l3)
        e1, e2, e3 = jnp.exp(l1 - m), jnp.exp(l2 - m), jnp.exp(l3 - m)
        z = e1 + e2 + e3
        w1, w2, w3 = e1 / z, e2 / z, e3 / z
        g1, g2, g3 = _head_sum(db * o1), _head_sum(db * o2), _head_sum(db * o3)
        gm = w1 * g1 + w2 * g2 + w3 * g3
        return (da * keep, dla, w1 * db, w2 * db, w3 * db,
                w1 * (g1 - gm), w2 * (g2 - gm), w3 * (g3 - gm), -_colsum(dla))

    rows = [(dcat, 0, 512, "cur"), (dcat, 512, 512, "cur")] + [(a, 0, 512, "cur") for a in (oa, la, *obs, *lbs)]
    return rowwise(fn, name=name, L=L, tm=_pick(L, 256, SUBLANE), rows=rows, consts=[sinkb],
                   outs=[(512, f32)] * 8, sums=[(1, 512)])


def attn_assemble(dqa, dka, dva, dqs, dks, dvs, name):
    L = dqa.shape[0]

    def fn(i, nb, qa, ka, va, q1, q2, q3, k1, k2, k3, v1, v2, v3):
        return jnp.concatenate([qa, ka, va, q1 + q2 + q3, k1 + k2 + k3, v1 + v2 + v3], axis=1)

    rows = [(dqa, 0, 512, "cur"), (dka, 0, 128, "cur"), (dva, 0, 128, "cur")]
    rows += [(a, 0, 512, "cur") for a in (*dqs, *dks, *dvs)]
    return rowwise(fn, name=name, L=L, tm=_pick(L, 256, SUBLANE), rows=rows, outs=[(ATTN_IN, bf16)])


def attention_fwd(hin, wqa, wka, wqb, wkb, sinkb, ride=None):
    oa, la = attn2_fwd(hin, _Plan(1, True, 4), wqa, wka, "attn_a_fwd")
    obs, lbs, got = [], [], None
    for _, d in B_BRANCHES:
        res = attn2_fwd(hin, _Plan(d, False, 2), wqb, wkb, f"attn_b{d}_fwd", ride=ride if d == 1 else None)
        obs.append(res[0])
        lbs.append(res[1])
        got = res[2] if (d == 1 and ride) else got
    ocat = attn_merge_fwd(oa, la, obs, lbs, sinkb, "attn_merge_fwd")
    return ocat, (oa, la, obs, lbs), got


def attention_bwd(hin, wqa, wka, wqb, wkb, sinkb, saved, dcat, rides=None):
    rides = dict(rides or {})
    gots = {}

    def ride_of(key):
        r = rides.get(key)
        return r(gots) if callable(r) else r

    oa, la, obs, lbs = saved
    res = attn_merge_bwd(dcat, oa, la, obs, lbs, sinkb, "attn_merge_bwd")
    doa, dla, dos, dls, dsink = res[0], res[1], res[2:5], res[5:8], res[8]
    zero = jnp.zeros((1, 64), f32)
    res = attn2_bwd(hin, _Plan(1, True, 4), wqa, wka, oa, la, doa, dla, (zero, zero), "attn_a_bwd", ride=ride_of("a"))
    dqa, dka, dva, dwqa, dwka = res[:5]
    if "a" in rides:
        gots["a"] = res[5]
    dqs, dks, dvs = [], [], []
    dwqb = dwkb = zero
    for g, (_, d) in enumerate(B_BRANCHES):
        res = attn2_bwd(hin, _Plan(d, False, 2 if d < 16 else 1), wqb, wkb, obs[g], lbs[g],
                        dos[g], dls[g], (dwqb, dwkb), f"attn_b{d}_bwd", ride=ride_of(d))
        dq, dk, dv, dwqb, dwkb = res[:5]
        if d in rides:
            gots[d] = res[5]
        dqs.append(dq)
        dks.append(dk)
        dvs.append(dv)
    dhin = attn_assemble(dqa, dka, dva, dqs, dks, dvs, "attn_assemble")
    return dhin, dwqa, dwka, dwqb, dwkb, dsink, gots


NS = S5_GROUPS * S5_STATE


def _s5_param_fn(lr, li, ldt):
    dt = jnp.exp(ldt)
    mag, ang = jnp.exp(lr * dt), li * dt
    ab_re, ab_im = mag * jnp.cos(ang), mag * jnp.sin(ang)
    nr, ni = ab_re - 1.0, ab_im
    den = lr * lr + li * li
    return ab_re, ab_im, (nr * lr + ni * li) / den, (ni * lr - nr * li) / den


def s5_params_fwd(lr, li, ldt):
    def body(lr_ref, li_ref, ldt_ref, *outs):
        for o_ref, o in zip(outs, _s5_param_fn(lr_ref[...], li_ref[...], ldt_ref[...])):
            o_ref[...] = o

    return pl.pallas_call(body, out_shape=[jax.ShapeDtypeStruct(lr.shape, f32)] * 4, name="s5_params_fwd")(lr, li, ldt)


def s5_params_bwd(lr, li, ldt, cts):
    def body(lr_ref, li_ref, ldt_ref, c0, c1, c2, c3, dlr, dli, dldt):
        _, vjp = jax.vjp(_s5_param_fn, lr_ref[...], li_ref[...], ldt_ref[...])
        a, b, c = vjp((c0[...], c1[...], c2[...], c3[...]))
        dlr[...] = a
        dli[...] = b
        dldt[...] = c

    return pl.pallas_call(
        body, out_shape=[jax.ShapeDtypeStruct(lr.shape, f32), jax.ShapeDtypeStruct(li.shape, f32),
                         jax.ShapeDtypeStruct(ldt.shape, f32)], name="s5_params_bwd")(lr, li, ldt, *cts)


def _cmul(ar, ai, br, bi):
    return ar * br - ai * bi, ar * bi + ai * br


def s5_scan(z, ab_re, ab_im, f_re, f_im, *, reverse, name):
    L = z.shape[0]
    tm = _pick(L, 256, SUBLANE)
    nb = L // tm
    ng = tm // SUBLANE
    use_f = f_re is not None
    consts = [ab_re, ab_im] + ([f_re, f_im] if use_f else [])

    def body(*refs):
        z_ref = refs[0]
        c_refs = refs[1:1 + len(consts)]
        x_ref, car = refs[1 + len(consts)], refs[2 + len(consts)]
        i = pl.program_id(0)

        @pl.when(i == 0)
        def _():
            car[...] = jnp.zeros_like(car)

        a1 = (c_refs[0][...], c_refs[1][...])
        a2 = _cmul(*a1, *a1)
        a3 = _cmul(*a2, *a1)
        a4 = _cmul(*a2, *a2)
        pw = [a1, a2, a3, a4, _cmul(*a4, *a1), _cmul(*a4, *a2), _cmul(*a4, *a3), _cmul(*a4, *a4)]
        if reverse:
            pw = pw[::-1]
        pw_re = _stack_rows([p[0] for p in pw])
        pw_im = _stack_rows([p[1] for p in pw])
        ridx = lax.broadcasted_iota(jnp.int32, (SUBLANE, NS), 0)
        if use_f:
            fr, fi = c_refs[2][...], c_refs[3][...]

        def group(s, carry):
            cr, ci = carry
            g = (ng - 1 - s) if reverse else s
            r0 = pl.multiple_of(g * SUBLANE, SUBLANE)
            xr = z_ref[pl.ds(r0, SUBLANE), 0:NS]
            xi = z_ref[pl.ds(r0, SUBLANE), NS:2 * NS]
            if use_f:
                xr, xi = _cmul(fr, fi, xr, xi)
            for sft, (pr, pi) in ((1, a1), (2, a2), (4, a4)):
                if reverse:
                    keep = ridx < SUBLANE - sft
                    sr = jnp.where(keep, pltpu.roll(xr, SUBLANE - sft, 0), 0.0)
                    si = jnp.where(keep, pltpu.roll(xi, SUBLANE - sft, 0), 0.0)
                else:
                    keep = ridx >= sft
                    sr = jnp.where(keep, pltpu.roll(xr, sft, 0), 0.0)
                    si = jnp.where(keep, pltpu.roll(xi, sft, 0), 0.0)
                tr, ti = _cmul(pr, pi, sr, si)
                xr, xi = xr + tr, xi + ti
            tr, ti = _cmul(pw_re, pw_im, cr, ci)
            xr, xi = xr + tr, xi + ti
            x_ref[pl.ds(r0, SUBLANE), 0:NS] = xr
            x_ref[pl.ds(r0, SUBLANE), NS:2 * NS] = xi
            row = 0 if reverse else SUBLANE - 1
            return xr[row:row + 1, :], xi[row:row + 1, :]

        cr, ci = lax.fori_loop(0, ng, group, (car[0:1, 0:NS], car[0:1, NS:2 * NS]))
        car[0:1, 0:NS] = cr
        car[0:1, NS:2 * NS] = ci

    blk = (lambda i: (nb - 1 - i, 0)) if reverse else (lambda i: (i, 0))
    return pl.pallas_call(
        body, grid=(nb,),
        in_specs=[pl.BlockSpec((tm, 2 * NS), blk)] + [pl.BlockSpec((1, NS), lambda i: (0, 0))] * len(consts),
        out_specs=pl.BlockSpec((tm, 2 * NS), blk),
        out_shape=jax.ShapeDtypeStruct((L, 2 * NS), f32),
        scratch_shapes=[pltpu.VMEM((SUBLANE, 2 * NS), f32)],
        compiler_params=_cp("arbitrary"), name=name,
    )(z, *consts)


def _s5_post_fn(ypre, u, dvec, gw, gb):
    y = ypre + dvec * u
    g = jax.nn.gelu(y)
    z = _dot(g.astype(bf16), gw.astype(bf16)) + gb
    return g * jax.nn.sigmoid(z)


def s5_post_fwd(ypre, hin, dvec, gw, gb):
    L = ypre.shape[0]

    def fn(i, nb, yt, ut, dv, gwv, gbv):
        return _s5_post_fn(yt, ut, dv, gwv, gbv)

    return rowwise(fn, name="s5_post_fwd", L=L, tm=_pick(L, 512, SUBLANE),
                   rows=[(ypre, 0, S5_WIDTH, "cur"), (hin, 3072, S5_WIDTH, "cur")],
                   consts=[dvec, gw, gb], outs=[(S5_WIDTH, f32)])


def s5_post_bwd(ypre, hin, dvec, gw, gb, dycat):
    L = ypre.shape[0]

    def fn(i, nb, yt, ut, dyt, dv, gwv, gbv):
        _, vjp = jax.vjp(_s5_post_fn, yt, ut, dv, gwv, gbv)
        return vjp(dyt)

    return rowwise(fn, name="s5_post_bwd", L=L, tm=_pick(L, 512, SUBLANE),
                   rows=[(ypre, 0, S5_WIDTH, "cur"), (hin, 3072, S5_WIDTH, "cur"), (dycat, 0, S5_WIDTH, "cur")],
                   consts=[dvec, gw, gb], outs=[(S5_WIDTH, f32)] * 2,
                   sums=[(1, S5_WIDTH), (S5_WIDTH, S5_WIDTH), (1, S5_WIDTH)])


def s5_acc(G, X, bu, f_re, f_im):
    L = G.shape[0]

    def fn(i, nb, g, x, b, xp8, fr, fi):
        gr, gi = g[:, :NS], g[:, NS:]
        xp = _shift_down(x, xp8 * (i > 0).astype(f32), 1)
        xr, xi = xp[:, :NS], xp[:, NS:]
        br, bi = b[:, :NS], b[:, NS:]
        dbu = jnp.concatenate([fr * gr + fi * gi, fr * gi - fi * gr], axis=1)
        return (dbu, _colsum(xr * gr + xi * gi), _colsum(xr * gi - xi * gr),
                _colsum(br * gr + bi * gi), _colsum(br * gi - bi * gr))

    return rowwise(fn, name="s5_acc", L=L, tm=_pick(L, 256, SUBLANE),
                   rows=[(G, 0, 2 * NS, "cur"), (X, 0, 2 * NS, "cur"), (bu, 0, 2 * NS, "cur"), (X, 0, 2 * NS, "prev")],
                   consts=[f_re, f_im], outs=[(2 * NS, bf16)], sums=[(1, NS)] * 4)


def _s5_blockdiag(b_re, b_im, c_re, c_im):
    eye = jnp.eye(S5_GROUPS, dtype=f32)
    bb = lambda b: jnp.einsum("gpi,gh->gihp", b, eye).reshape(S5_WIDTH, NS)
    cc = lambda c: jnp.einsum("gip,gh->gphi", c, eye).reshape(NS, S5_WIDTH)
    return jnp.concatenate([bb(b_re), bb(b_im)], axis=1), jnp.concatenate([cc(c_re), -cc(c_im)], axis=0)


def _s5_blockdiag_grads(dB, dC):
    gb = lambda m: jnp.einsum("gigp->gpi", m.reshape(S5_GROUPS, S5_GROUP, S5_GROUPS, S5_STATE))
    gc = lambda m: jnp.einsum("gpgi->gip", m.reshape(S5_GROUPS, S5_STATE, S5_GROUPS, S5_GROUP))
    return gb(dB[:, :NS]), gb(dB[:, NS:]), gc(dC[:NS]), -gc(dC[NS:])


def s5_fwd(hin, prm):
    ab_re, ab_im, f_re, f_im = s5_params_fwd(prm["lr"], prm["li"], prm["ldt"])
    flat = lambda a: a.reshape(1, NS)
    ab_re, ab_im, f_re, f_im = flat(ab_re), flat(ab_im), flat(f_re), flat(f_im)
    Bblk, Cblk = _s5_blockdiag(prm["b_re"], prm["b_im"], prm["c_re"], prm["c_im"])
    bu = mm(hin, Bblk, name="s5_bu", a_win=(3072, S5_WIDTH))
    X = s5_scan(bu, ab_re, ab_im, f_re, f_im, reverse=False, name="s5_scan_fwd")
    ypre = mm(X, Cblk, name="s5_y")
    yc = s5_post_fwd(ypre, hin, prm["d"], prm["gw"], prm["gb"])
    return yc, (ab_re, ab_im, f_re, f_im, Bblk, Cblk, bu, X, ypre)


def s5_bwd(hin, prm, saved, dycat):
    ab_re, ab_im, f_re, f_im, Bblk, Cblk, bu, X, ypre = saved
    dypre, du_skip, dd, dgw, dgb = s5_post_bwd(ypre, hin, prm["d"], prm["gw"], prm["gb"], dycat)
    dX = mm(dypre, Cblk, tb=True, name="s5_dx")
    dC = mm(X, dypre, ta=True, name="s5_dc")
    G = s5_scan(dX, ab_re, -ab_im, None, None, reverse=True, name="s5_scan_bwd")
    dbu, dar, dai, dfr, dfi = s5_acc(G, X, bu, f_re, f_im)
    dB = mm(hin, dbu, ta=True, a_win=(3072, S5_WIDTH), name="s5_db")
    du_b = mm(dbu, Bblk, tb=True, name="s5_du")
    sh = prm["lr"].shape
    dlr, dli, dldt = s5_params_bwd(prm["lr"], prm["li"], prm["ldt"],
                                   [a.reshape(sh) for a in (dar, dai, dfr, dfi)])
    db_re, db_im, dc_re, dc_im = _s5_blockdiag_grads(dB, dC)
    grads = dict(lr=dlr, li=dli, ldt=dldt, b_re=db_re, b_im=db_im, c_re=dc_re, c_im=dc_im, d=dd, gw=dgw, gb=dgb)
    return du_skip, du_b, grads


DN_W = DN_HEADS * DN_DK
QKV_W = 3 * DN_W


def _softplus(x):
    return jnp.maximum(x, 0.0) + jnp.log(1.0 + jnp.exp(-jnp.abs(x)))


def _dn_pre(c, ab, alog, dtb):
    s = _silu(c)
    parts = []
    for h in range(2 * DN_HEADS):
        sh = s[:, h * 128:(h + 1) * 128]
        scale = DN_DK ** -0.5 if h < DN_HEADS else 1.0
        parts.append(sh * (lax.rsqrt(jnp.sum(sh * sh, axis=-1, keepdims=True) + EPS) * scale))
    parts.append(s[:, 2 * DN_W:])
    g = -jnp.exp(alog) * _softplus(ab[:, :128] + dtb)
    beta = jax.nn.sigmoid(ab[:, 128:])
    return jnp.concatenate(parts, axis=1), jnp.concatenate([g, beta], axis=1)


def _dn_pre_bwd(c, ab, alog, dtb, dqkv, dgb):
    sg = jax.nn.sigmoid(c)
    s = c * sg
    parts = []
    for h in range(2 * DN_HEADS):
        sh = s[:, h * 128:(h + 1) * 128]
        dy = dqkv[:, h * 128:(h + 1) * 128]
        scale = DN_DK ** -0.5 if h < DN_HEADS else 1.0
        r = lax.rsqrt(jnp.sum(sh * sh, axis=-1, keepdims=True) + EPS)
        parts.append(scale * r * (dy - sh * (r * r) * jnp.sum(dy * sh, axis=-1, keepdims=True)))
    parts.append(dqkv[:, 2 * DN_W:])
    dc = jnp.concatenate(parts, axis=1) * (sg * (1.0 + c * (1.0 - sg)))
    pre = ab[:, :128] + dtb
    ea = jnp.exp(alog)
    dg = dgb[:, :128]
    da = dg * (-ea) * jax.nn.sigmoid(pre)
    dalog = _colsum(dg * (-ea) * _softplus(pre))
    beta = jax.nn.sigmoid(ab[:, 128:])
    db = dgb[:, 128:] * beta * (1.0 - beta)
    return dc, jnp.concatenate([da, db], axis=1), dalog, _colsum(da)


def dn_pre_fwd(hin, conv_w, alog, dtb):
    L = hin.shape[0]

    def fn(i, nb, x, ab, p8, w, al, db):
        c = _conv_causal(x, p8 * (i > 0).astype(f32), w)
        return _dn_pre(c, ab, al, db)

    return rowwise(fn, name="dn_pre_fwd", L=L, tm=_pick(L, 256, SUBLANE),
                   rows=[(hin, 0, QKV_W, "cur"), (hin, 3328, 256, "cur"), (hin, 0, QKV_W, "prev")],
                   consts=[conv_w, alog, dtb], outs=[(QKV_W, f32), (256, f32)])


def dn_pre_bwd(hin, conv_w, alog, dtb, dqkv3, dg, dbeta):
    L = hin.shape[0]

    def fn(i, nb, x, ab, dq, dk, dv, dgt, dbt, p8, w, al, db):
        c = _conv_causal(x, p8 * (i > 0).astype(f32), w)
        return _dn_pre_bwd(c, ab, al, db, jnp.concatenate([dq, dk, dv], axis=1), jnp.concatenate([dgt, dbt], axis=1))

    rows = [(hin, 0, QKV_W, "cur"), (hin, 3328, 256, "cur")] + [(a, 0, DN_W, "cur") for a in dqkv3]
    rows += [(dg, 0, 128, "cur"), (dbeta, 0, 128, "cur"), (hin, 0, QKV_W, "prev")]
    return rowwise(fn, name="dn_pre_bwd", L=L, tm=_pick(L, 128, SUBLANE), rows=rows,
                   consts=[conv_w, alog, dtb], outs=[(QKV_W, f32), (256, f32)], sums=[(1, 128), (1, 128)])


def _split(a):
    hi = a.astype(bf16)
    return hi, (a - hi.astype(f32)).astype(bf16)


def _dot3_raw(a, b, dims):
    ah, al = _split(a)
    bh, bl = _split(b)
    return _dot(ah, bh, dims) + (_dot(ah, bl, dims) + _dot(al, bh, dims))


@functools.partial(jax.custom_vjp, nondiff_argnums=(2,))
def _dot3(a, b, dims=NN):
    return _dot3_raw(a, b, dims)


def _dot3_fwd(a, b, dims):
    return _dot3_raw(a, b, dims), (a, b)


BNN = (((2,), (1,)), ((0,), (0,)))
BNT = (((2,), (2,)), ((0,), (0,)))
BTN = (((1,), (1,)), ((0,), (0,)))


def _dot_bwd(raw, dims, res, g):
    a, b = res
    nn, nt, tn = (BNN, BNT, BTN) if dims[1][0] else (NN, NT, TN)
    if dims == nn:
        return raw(g, b, nt), raw(a, g, tn)
    if dims == nt:
        return raw(g, b, nn), raw(g, a, tn)
    assert dims == tn
    return raw(b, g, nt), raw(a, g, nn)


_dot3.defvjp(_dot3_fwd, functools.partial(_dot_bwd, _dot3_raw))


def _dot1_raw(a, b, dims):
    return _dot(a.astype(bf16), b.astype(bf16), dims)


@functools.partial(jax.custom_vjp, nondiff_argnums=(2,))
def _dot1(a, b, dims=NN):
    return _dot1_raw(a, b, dims)


_dot1.defvjp(lambda a, b, dims: (_dot1_raw(a, b, dims), (a, b)), functools.partial(_dot_bwd, _dot1_raw))


def _unit_lower_inverse(nmat):
    C = nmat.shape[-1]
    eye = (lax.broadcasted_iota(jnp.int32, (C, C), 0) == lax.broadcasted_iota(jnp.int32, (C, C), 1)).astype(f32)
    T = eye - nmat
    Pw = _dot3(nmat, nmat, BNN)
    for step in range(5):
        T = T + _dot3(T, Pw, BNN)
        if step < 4:
            Pw = _dot3(Pw, Pw, BNN)
    return T


@jax.custom_vjp
def _inverse_known(nmat, T):
    return T


def _inverse_known_bwd(T, g):
    return -_dot3(_dot3(T, g, BTN), T, BNT), jnp.zeros_like(T)


_inverse_known.defvjp(lambda nmat, T: (T, T), _inverse_known_bwd)


def _dn_chunk(q, k, v, gcol, bcol, S, T_known=None):
    C = q.shape[1]
    r = lax.broadcasted_iota(jnp.int32, (C, C), 0)
    c = lax.broadcasted_iota(jnp.int32, (C, C), 1)
    tril = (r >= c).astype(f32)
    strict = (r > c).astype(f32)
    eye = (r == c).astype(f32)
    hd = _dot3
    grow = jnp.sum(eye * gcol, axis=1, keepdims=True)
    Gcol = jnp.sum(tril * grow, axis=2, keepdims=True)
    Grow = jnp.sum(eye * Gcol, axis=1, keepdims=True)
    gamma = jnp.exp((Gcol - Grow) * tril) * tril
    ld = _dot1
    nmat = strict * bcol * ld(k, k, BNT) * gamma
    T = _unit_lower_inverse(nmat) if T_known is None else _inverse_known(nmat, T_known)
    eG = jnp.exp(Gcol)
    u = hd(T, bcol * v, BNN)
    w = hd(T, (bcol * eG) * k, BNN)
    qk = ld(q, k, BNT) * gamma
    vnew = u - ld(w, S, BNN)
    o = ld(q * eG, S, BNN) + ld(qk, vnew, BNN)
    Glast = jnp.sum(gcol, axis=1, keepdims=True)
    S2 = S * jnp.exp(Glast) + ld(k * jnp.exp(Glast - Gcol), vnew, BTN)
    return o, S2, T


def _heads(x_ref):
    return jnp.stack([x_ref[:, h * 128:(h + 1) * 128] for h in range(DN_HEADS)])


def _head_cols(g_ref):
    return jnp.stack([g_ref[:, h:h + 1] for h in range(DN_HEADS)])


def dn_chunks_fwd(qkvn, gb):
    L = qkvn.shape[0]
    C = DN_CHUNK
    nc = L // C

    def body(q_ref, k_ref, v_ref, g_ref, b_ref, o_ref, sin_ref, t_ref, S):
        n = pl.program_id(0)

        @pl.when(n == 0)
        def _():
            S[...] = jnp.zeros_like(S)

        s_in = S[...]
        sin_ref[...] = s_in
        o, s2, t = _dn_chunk(_heads(q_ref), _heads(k_ref), _heads(v_ref), _head_cols(g_ref), _head_cols(b_ref), s_in)
        for h in range(DN_HEADS):
            o_ref[:, h * 128:(h + 1) * 128] = o[h]
        t_ref[...] = t
        S[...] = s2

    blk = lambda j: pl.BlockSpec((C, DN_W), lambda n, j=j: (n, j))
    gblk = lambda j: pl.BlockSpec((C, 128), lambda n, j=j: (n, j))
    return pl.pallas_call(
        body, grid=(nc,),
        in_specs=[blk(0), blk(1), blk(2), gblk(0), gblk(1)],
        out_specs=[pl.BlockSpec((C, DN_W), lambda n: (n, 0)),
                   pl.BlockSpec((DN_HEADS, None, 128, 128), lambda n: (0, n, 0, 0)),
                   pl.BlockSpec((DN_HEADS, None, C, C), lambda n: (0, n, 0, 0))],
        out_shape=[jax.ShapeDtypeStruct((L, DN_W), f32), jax.ShapeDtypeStruct((DN_HEADS, nc, 128, 128), f32),
                   jax.ShapeDtypeStruct((DN_HEADS, nc, C, C), f32)],
        scratch_shapes=[pltpu.VMEM((DN_HEADS, 128, 128), f32)],
        compiler_params=_cp("arbitrary"), name="dn_chunks_fwd",
    )(qkvn, qkvn, qkvn, gb, gb)


def dn_chunks_bwd(qkvn, gb, s_in, t_inv, do):
    L = qkvn.shape[0]
    C = DN_CHUNK
    nc = L // C

    def body(q_ref, k_ref, v_ref, g_ref, b_ref, sin_ref, t_ref, do_ref, dq_ref, dk_ref, dv_ref, dg_ref, db_ref, dS):
        n = pl.program_id(0)

        @pl.when(n == 0)
        def _():
            dS[...] = jnp.zeros_like(dS)

        args = (_heads(q_ref), _heads(k_ref), _heads(v_ref), _head_cols(g_ref), _head_cols(b_ref), sin_ref[...])
        t_known = t_ref[...]
        _, vjp = jax.vjp(lambda *a: _dn_chunk(*a, T_known=t_known)[:2], *args)
        dq, dk, dv, dg, db, ds = vjp((_heads(do_ref), dS[...]))
        lane = lax.broadcasted_iota(jnp.int32, (C, 128), 1)
        dg_all = jnp.zeros((C, 128), f32)
        db_all = jnp.zeros((C, 128), f32)
        for h in range(DN_HEADS):
            sl = slice(h * 128, (h + 1) * 128)
            dq_ref[:, sl] = dq[h]
            dk_ref[:, sl] = dk[h]
            dv_ref[:, sl] = dv[h]
            dg_all = dg_all + jnp.where(lane == h, dg[h], 0.0)
            db_all = db_all + jnp.where(lane == h, db[h], 0.0)
        dS[...] = ds
        dg_ref[...] = dg_all
        db_ref[...] = db_all

    rv = lambda n: nc - 1 - n
    blk = lambda j: pl.BlockSpec((C, DN_W), lambda n, j=j: (rv(n), j))
    gblk = lambda j: pl.BlockSpec((C, 128), lambda n, j=j: (rv(n), j))
    oblk = pl.BlockSpec((C, DN_W), lambda n: (rv(n), 0))
    gout = pl.BlockSpec((C, 128), lambda n: (rv(n), 0))
    return pl.pallas_call(
        body, grid=(nc,),
        in_specs=[blk(0), blk(1), blk(2), gblk(0), gblk(1),
                  pl.BlockSpec((DN_HEADS, None, 128, 128), lambda n: (0, rv(n), 0, 0)),
                  pl.BlockSpec((DN_HEADS, None, C, C), lambda n: (0, rv(n), 0, 0)), oblk],
        out_specs=[oblk] * 3 + [gout] * 2,
        out_shape=[jax.ShapeDtypeStruct((L, DN_W), f32)] * 3 + [jax.ShapeDtypeStruct((L, 128), f32)] * 2,
        scratch_shapes=[pltpu.VMEM((DN_HEADS, 128, 128), f32)],
        compiler_params=_cp("arbitrary"), name="dn_chunks_bwd",
    )(qkvn, qkvn, qkvn, gb, gb, s_in, t_inv, do)


def _dn_post(o, z, w):
    parts = []
    for h in range(DN_HEADS):
        oh = o[:, h * 128:(h + 1) * 128]
        r = lax.rsqrt(jnp.mean(oh * oh, axis=-1, keepdims=True) + EPS)
        parts.append(oh * r * w)
    return jnp.concatenate(parts, axis=1) * _silu(z)


def dn_post_fwd(o, hin, yc, onorm):
    L = o.shape[0]

    def fn(i, nb, ot, zt, yct, w):
        return jnp.concatenate([yct, _dn_post(ot, zt, w)], axis=1)

    return rowwise(fn, name="dn_post_fwd", L=L, tm=_pick(L, 256, SUBLANE),
                   rows=[(o, 0, DN_W, "cur"), (hin, 2304, DN_W, "cur"), (yc, 0, S5_WIDTH, "cur")],
                   consts=[onorm], outs=[(1024, bf16)])


def dn_post_bwd(o, hin, onorm, dycat):
    L = o.shape[0]

    def fn(i, nb, ot, zt, d0, d1, d2, w):
        dy = jnp.concatenate([d0, d1, d2], axis=1)
        sg = jax.nn.sigmoid(zt)
        sz = zt * sg
        dos, dw = [], jnp.zeros((1, 128), f32)
        nrm = []
        for h in range(DN_HEADS):
            sl = slice(h * 128, (h + 1) * 128)
            oh = ot[:, sl]
            r = lax.rsqrt(jnp.mean(oh * oh, axis=-1, keepdims=True) + EPS)
            ohat = oh * r
            t = dy[:, sl] * sz[:, sl]
            dw = dw + _colsum(t * ohat)
            t = t * w
            dos.append(r * (t - ohat * jnp.mean(t * ohat, axis=-1, keepdims=True)))
            nrm.append(ohat * w)
        dz = dy * jnp.concatenate(nrm, axis=1) * (sg * (1.0 + zt * (1.0 - sg)))
        return jnp.concatenate(dos, axis=1), dz, dw

    rows = [(o, 0, DN_W, "cur"), (hin, 2304, DN_W, "cur")] + [(dycat, 256 * (1 + j), 256, "cur") for j in range(3)]
    return rowwise(fn, name="dn_post_bwd", L=L, tm=_pick(L, 256, SUBLANE), rows=rows,
                   consts=[onorm], outs=[(DN_W, f32), (DN_W, f32)], sums=[(1, 128)])


def conv_bwd_win(xarr, start, C, w, dc, name):
    L = xarr.shape[0]
    W = w.shape[0]

    def fn(i, nb, xt, dct, p8, n8, wv):
        dx = _conv_causal_bwd_x(dct, n8 * (i < nb - 1).astype(f32), wv)
        dw = _conv_causal_bwd_w(dct, xt, p8 * (i > 0).astype(f32), W)
        return dx, dw

    return rowwise(fn, name=name, L=L, tm=_pick(L, 128, SUBLANE),
                   rows=[(xarr, start, C, "cur"), (dc, 0, C, "cur"), (xarr, start, C, "prev"), (dc, 0, C, "next")],
                   consts=[w], outs=[(C, bf16)], sums=[(SUBLANE, C)])


def rec_assemble(dx_qkv, dz, du1, du2, dab):
    L = dz.shape[0]

    def fn(i, nb, a, b, c, d, e):
        return jnp.concatenate([a.astype(f32), b, c + d, e], axis=1)

    return rowwise(fn, name="rec_assemble", L=L, tm=_pick(L, 256, SUBLANE),
                   rows=[(dx_qkv, 0, QKV_W, "cur"), (dz, 0, DN_W, "cur"), (du1, 0, 256, "cur"),
                         (du2, 0, 256, "cur"), (dab, 0, 256, "cur")], outs=[(REC_PAD, bf16)])


def deltanet_fwd(hin, prm, yc):
    qkvn, gb = dn_pre_fwd(hin, prm["conv"], prm["alog"], prm["dtb"])
    o, s_in, t_inv = dn_chunks_fwd(qkvn, gb)
    ycat = dn_post_fwd(o, hin, yc, prm["onorm"])
    return ycat, (qkvn, gb, o, s_in, t_inv)


def deltanet_bwd(hin, prm, saved, dycat):
    qkvn, gb, o, s_in, t_inv = saved
    do, dz, donorm = dn_post_bwd(o, hin, prm["onorm"], dycat)
    dq, dk, dv, dgH, dbH = dn_chunks_bwd(qkvn, gb, s_in, t_inv, do)
    dc, dab, dalog, ddtb = dn_pre_bwd(hin, prm["conv"], prm["alog"], prm["dtb"], (dq, dk, dv), dgH, dbH)
    dx_qkv, dconv = conv_bwd_win(hin, 0, QKV_W, prm["conv"], dc, "dn_conv_bwd")
    return dx_qkv, dz, dab, dict(conv=dconv[:DN_CONV], alog=dalog, dtb=ddtb, onorm=donorm)


AXES = ("x", "y", "c")


class _Coll:
    def __init__(self, x, axes, mode):
        self.axes, self.mode = axes, mode
        self.P = 2 ** len(axes)
        shape = x.shape if mode == "gather" else x.shape[1:]
        self.out_shape = jax.ShapeDtypeStruct((self.P,) + tuple(shape), x.dtype)
        self.scratch = [pltpu.SemaphoreType.DMA((self.P - 1,)), pltpu.SemaphoreType.DMA((self.P - 1,)),
                        pltpu.SemaphoreType.DMA]

    def _copies(self, x_ref, out_ref, send_sems, recv_sems, local_sem, with_recvs):
        axes, k = self.axes, len(self.axes)
        co = {a: lax.axis_index(a) for a in AXES}
        me = 0
        for a in axes:
            me = me * 2 + co[a]
        src = (lambda j: x_ref) if self.mode == "gather" else (lambda j: x_ref.at[j])
        local = pltpu.make_async_copy(src(me), out_ref.at[me], local_sem)
        sends, recvs = [], []
        for m in range(1, self.P):
            tco = dict(co)
            t = 0
            for i, a in enumerate(axes):
                if (m >> (k - 1 - i)) & 1:
                    tco[a] = 1 - co[a]
                t = t * 2 + tco[a]
            dev = tuple(tco[a] for a in AXES)
            mk = functools.partial(pltpu.make_async_remote_copy, src_ref=src(t), send_sem=send_sems.at[m - 1],
                                   recv_sem=recv_sems.at[m - 1], device_id=dev, device_id_type=MESH)
            sends.append(mk(dst_ref=out_ref.at[me]))
            if with_recvs:
                recvs.append(mk(dst_ref=out_ref.at[t]))
        return local, sends, recvs

    def start(self, *refs):
        local, sends, _ = self._copies(*refs, with_recvs=False)
        local.start()
        for cp in sends:
            cp.start()

    def wait(self, *refs):
        local, sends, recvs = self._copies(*refs, with_recvs=True)
        for cp in recvs:
            cp.wait_recv()
        for cp in sends:
            cp.wait_send()
        local.wait()


def _collective(x, axes, mode, name):
    coll = _Coll(x, axes, mode)

    def body(*refs):
        coll.start(*refs)
        coll.wait(*refs)

    return pl.pallas_call(
        body, in_specs=[pl.BlockSpec(memory_space=pl.ANY)], out_specs=pl.BlockSpec(memory_space=pl.ANY),
        out_shape=coll.out_shape, scratch_shapes=coll.scratch, name=name,
    )(x)


def all_gather(x, axes, name):
    return _collective(x, axes, "gather", name)


def exchange(x, axes, name):
    return _collective(x, axes, "exchange", name)


def sum_slots(x, name, out_dtype=f32):
    P, R, C = x.shape
    tr = _pick(R, 256, 2 * SUBLANE)

    def body(x_ref, o_ref):
        acc = x_ref[0].astype(f32)
        for j in range(1, P):
            acc = acc + x_ref[j].astype(f32)
        o_ref[...] = acc.astype(o_ref.dtype)

    return pl.pallas_call(
        body, grid=(R // tr,), in_specs=[pl.BlockSpec((P, tr, C), lambda i: (0, i, 0))],
        out_specs=pl.BlockSpec((tr, C), lambda i: (i, 0)), out_shape=jax.ShapeDtypeStruct((R, C), out_dtype),
        compiler_params=_cp("parallel"), name=name,
    )(x)


def _pack(arrs, width, row_mult, dtype):
    flat = jnp.concatenate([a.astype(dtype).reshape(-1) for a in arrs])
    unit = width * row_mult
    n = -(-flat.shape[0] // unit) * unit
    return jnp.pad(flat, (0, n - flat.shape[0])).reshape(n // width, width)


def _unpack(flat, shapes):
    flat = flat.reshape(-1)
    out, off = [], 0
    for s in shapes:
        n = int(np.prod(s))
        out.append(flat[off:off + n].reshape(s))
        off += n
    return out


def ada_fwd(c_all, ada_w):
    def body(c_ref, w_ref, o_ref):
        cond = _silu(c_ref[...])
        for l in range(ada_w.shape[0]):
            o_ref[l] = _dot(cond, w_ref[l], precision=HI)

    return pl.pallas_call(body, out_shape=jax.ShapeDtypeStruct((ada_w.shape[0], c_all.shape[0], ada_w.shape[2]), f32),
                          compiler_params=pltpu.CompilerParams(vmem_limit_bytes=VMEM_LIMIT), name="ada_fwd")(c_all, ada_w)


def ada_bwd(c_all, dmod):
    def body(c_ref, d_ref, o_ref):
        cond = _silu(c_ref[...])
        for l in range(dmod.shape[0]):
            o_ref[l] = _dot(cond, d_ref[l], TN, precision=HI)

    return pl.pallas_call(body, out_shape=jax.ShapeDtypeStruct((dmod.shape[0], c_all.shape[1], dmod.shape[2]), f32),
                          compiler_params=pltpu.CompilerParams(vmem_limit_bytes=VMEM_LIMIT), name="ada_bwd")(c_all, dmod)


def loss_fwd_bwd(y, target):
    L, D = y.shape

    def fn(i, nb, yt, tt):
        e = yt - tt
        return e * (1.0 / D), jnp.sum(jnp.sum(e * e, axis=1, keepdims=True), axis=0, keepdims=True)

    return rowwise(fn, name="loss", L=L, tm=_pick(L, 512, SUBLANE), rows=[(y, 0, D, "cur"), (target, 0, D, "cur")],
                   outs=[(D, f32)], sums=[(1, 1)])


def adamw(w, g, m, v, name):
    R, C = w.shape

    def fn(i, nb, wt, gt, mt, vt):
        m2 = ADAM_B1 * mt + (1.0 - ADAM_B1) * gt
        v2 = ADAM_B2 * vt + (1.0 - ADAM_B2) * (gt * gt)
        m_hat = m2 / (1.0 - ADAM_B1 ** ADAM_STEP)
        v_hat = v2 / (1.0 - ADAM_B2 ** ADAM_STEP)
        delta = -ADAM_LR * (m_hat / (jnp.sqrt(v_hat) + ADAM_EPS) + ADAM_WD * wt)
        return delta, m2, v2

    return rowwise(fn, name=name, L=R, tm=_pick(R, 256, SUBLANE), rows=[(a, 0, C, "cur") for a in (w, g, m, v)],
                   outs=[(C, f32)] * 3)


W_NAMES = ["ada_w", "ada_b", "norm_mix", "norm_ffn", "attn_w_in", "attn_q_norm_a", "attn_k_norm_a", "attn_q_norm_b",
           "attn_k_norm_b", "attn_sinks", "attn_w_out", "rec_w_in", "s5_lambda_re", "s5_lambda_im", "s5_log_dt",
           "s5_b_re", "s5_b_im", "s5_c_re", "s5_c_im", "s5_d", "s5_glu_w", "s5_glu_b", "dn_conv", "dn_a_log",
           "dn_dt_bias", "dn_out_norm", "rec_w_out", "ffn_w_up", "ffn_conv", "ffn_w_down"]
BIG = ["attn_w_in", "attn_w_out", "rec_w_in", "rec_w_out", "ffn_w_up", "ffn_w_down"]
SMALL_SHARDED = ["s5_d", "s5_glu_w", "s5_glu_b", "dn_conv", "ffn_conv"]
SMALL_REPL = [n for n in W_NAMES if n not in BIG and n not in SMALL_SHARDED and n != "ada_w"]
NSH = 4
GRAD_WIRE = (bf16,)


SHARD_AXIS = {"attn_w_in": 2, "attn_w_out": 1, "rec_w_in": 2, "rec_w_out": 1, "ffn_w_up": 2, "ffn_w_down": 1,
              "s5_d": 1, "s5_glu_w": 1, "s5_glu_b": 1, "dn_conv": 2, "ffn_conv": 2}


def _unshard(g, name):
    ax = SHARD_AXIS[name.rstrip("01")]
    g = jnp.moveaxis(g, 0, ax)
    s = g.shape
    return g.reshape(s[:ax] + (s[ax] * s[ax + 1],) + s[ax + 2:])


def _to_shards(full, name):
    ax = SHARD_AXIS[name.rstrip("01")]
    s = full.shape
    g = full.reshape(s[:ax] + (NSH, s[ax] // NSH) + s[ax + 1:])
    return jnp.moveaxis(g, ax, 0)


def _rec_pad_cols(w):
    z6 = jnp.zeros(w.shape[:-1] + (122,), w.dtype)
    return jnp.concatenate([w[..., 256:3328], w[..., 0:256], w[..., 3328:3334], z6, w[..., 3334:3340], z6], axis=-1)


def _rec_unpad_cols(g):
    return jnp.concatenate([g[..., 3072:3328], g[..., 0:3072], g[..., 3328:3334], g[..., 3456:3462]], axis=-1)


def _ffn_fwd(x1, nf, sc, sh, gate, w_up, conv, w_dn, tag, rides=()):
    rides = list(rides) + [None, None]
    h2 = modulate_fwd(x1, nf, sc, sh, f"{tag}_mod2_fwd")
    up = mm(h2, w_up, name=f"{tag}_ffn_up", out_dtypes=(bf16,), ride=rides[0])
    up, got0 = up if rides[0] else (up, None)
    act = ffn_act_fwd(up, conv, f"{tag}_ffn_act_fwd")
    res = mm(act, w_dn, name=f"{tag}_ffn_down", out_dtypes=(f32, f32), epi=_resid_epi, epi_mn=[x1], epi_n=[gate],
             ride=rides[1])
    return res[1], (h2, up, act, res[0]), (got0, res[2] if rides[1] else None)


def _ffn_bwd(dx, x1, nf, sc, sh, gate, w_up, conv, w_dn, saved, tag, rides=()):
    rides = list(rides) + [None, None, None]
    take = lambda res, r: res if r else (res, None)
    h2, up, act, f = saved
    df, dgate = resid_bwd(dx, f, gate, f"{tag}_res2_bwd")
    dact = mm(df, w_dn, tb=True, name=f"{tag}_ffn_dact", out_dtypes=(bf16,))
    dw_dn, got0 = take(mm(act, df, ta=True, name=f"{tag}_ffn_dwdown", out_dtypes=GRAD_WIRE, ride=rides[0]), rides[0])
    dup, dconv = ffn_act_conv_bwd(up, conv, dact, f"{tag}_ffn_act_conv_bwd")
    dw_up, got1 = take(mm(h2, dup, ta=True, name=f"{tag}_ffn_dwup", out_dtypes=GRAD_WIRE, ride=rides[1]), rides[1])
    if callable(rides[2]):
        rides[2] = rides[2](dw_dn)
    dh2, got2 = take(mm(dup, w_up, tb=True, name=f"{tag}_ffn_dh", ride=rides[2]), rides[2])
    dx, dnf, dsc, dsh = modulate_bwd(x1, nf, sc, sh, dh2, dx, f"{tag}_mod2_bwd")
    grads = dict(nf=dnf, sc=dsc, sh=dsh, gate=dgate, w_up=dw_up, conv=dconv[:FFN_CONV], w_dn=dw_dn)
    return dx, grads, (got0, got1, got2)


def kernel(x, c, ada_w, ada_b, norm_mix, norm_ffn, attn_w_in, attn_q_norm_a, attn_k_norm_a, attn_q_norm_b, attn_k_norm_b, attn_sinks, attn_w_out, rec_w_in, s5_lambda_re, s5_lambda_im, s5_log_dt, s5_b_re, s5_b_im, s5_c_re, s5_c_im, s5_d, s5_glu_w, s5_glu_b, dn_conv, dn_a_log, dn_dt_bias, dn_out_norm, rec_w_out, ffn_w_up, ffn_conv, ffn_w_down, loss_target, m_ada_w, m_ada_b, m_norm_mix, m_norm_ffn, m_attn_w_in, m_attn_q_norm_a, m_attn_k_norm_a, m_attn_q_norm_b, m_attn_k_norm_b, m_attn_sinks, m_attn_w_out, m_rec_w_in, m_s5_lambda_re, m_s5_lambda_im, m_s5_log_dt, m_s5_b_re, m_s5_b_im, m_s5_c_re, m_s5_c_im, m_s5_d, m_s5_glu_w, m_s5_glu_b, m_dn_conv, m_dn_a_log, m_dn_dt_bias, m_dn_out_norm, m_rec_w_out, m_ffn_w_up, m_ffn_conv, m_ffn_w_down, v_ada_w, v_ada_b, v_norm_mix, v_norm_ffn, v_attn_w_in, v_attn_q_norm_a, v_attn_k_norm_a, v_attn_q_norm_b, v_attn_k_norm_b, v_attn_sinks, v_attn_w_out, v_rec_w_in, v_s5_lambda_re, v_s5_lambda_im, v_s5_log_dt, v_s5_b_re, v_s5_b_im, v_s5_c_re, v_s5_c_im, v_s5_d, v_s5_glu_w, v_s5_glu_b, v_dn_conv, v_dn_a_log, v_dn_dt_bias, v_dn_out_norm, v_rec_w_out, v_ffn_w_up, v_ffn_conv, v_ffn_w_down):
    args = (ada_w, ada_b, norm_mix, norm_ffn, attn_w_in, attn_q_norm_a, attn_k_norm_a, attn_q_norm_b, attn_k_norm_b, attn_sinks, attn_w_out, rec_w_in, s5_lambda_re, s5_lambda_im, s5_log_dt, s5_b_re, s5_b_im, s5_c_re, s5_c_im, s5_d, s5_glu_w, s5_glu_b, dn_conv, dn_a_log, dn_dt_bias, dn_out_norm, rec_w_out, ffn_w_up, ffn_conv, ffn_w_down)
    ms = (m_ada_w, m_ada_b, m_norm_mix, m_norm_ffn, m_attn_w_in, m_attn_q_norm_a, m_attn_k_norm_a, m_attn_q_norm_b, m_attn_k_norm_b, m_attn_sinks, m_attn_w_out, m_rec_w_in, m_s5_lambda_re, m_s5_lambda_im, m_s5_log_dt, m_s5_b_re, m_s5_b_im, m_s5_c_re, m_s5_c_im, m_s5_d, m_s5_glu_w, m_s5_glu_b, m_dn_conv, m_dn_a_log, m_dn_dt_bias, m_dn_out_norm, m_rec_w_out, m_ffn_w_up, m_ffn_conv, m_ffn_w_down)
    vs = (v_ada_w, v_ada_b, v_norm_mix, v_norm_ffn, v_attn_w_in, v_attn_q_norm_a, v_attn_k_norm_a, v_attn_q_norm_b, v_attn_k_norm_b, v_attn_sinks, v_attn_w_out, v_rec_w_in, v_s5_lambda_re, v_s5_lambda_im, v_s5_log_dt, v_s5_b_re, v_s5_b_im, v_s5_c_re, v_s5_c_im, v_s5_d, v_s5_glu_w, v_s5_glu_b, v_dn_conv, v_dn_a_log, v_dn_dt_bias, v_dn_out_norm, v_rec_w_out, v_ffn_w_up, v_ffn_conv, v_ffn_w_down)
    W = dict(zip(W_NAMES, args))
    Mo = dict(zip(W_NAMES, ms))
    Vo = dict(zip(W_NAMES, vs))
    xi, yi, ci = lax.axis_index("x"), lax.axis_index("y"), lax.axis_index("c")
    shard = 2 * xi + yi
    me8 = 4 * xi + 2 * yi + ci
    xs = x[0]
    target = loss_target[0]
    L, D = xs.shape

    XY = ("x", "y")
    wparts = [
        [("attn_w_in", attn_w_in), ("attn_w_out", attn_w_out)],
        [("rec_w_in", rec_w_in), ("rec_w_out", rec_w_out)],
        [("ffn_w_up1", ffn_w_up[1:2]), ("ffn_w_down1", ffn_w_down[1:2])],
        [("ffn_w_up0", ffn_w_up[0:1]), ("ffn_w_down0", ffn_w_down[0:1])],
    ]
    wpack = [_pack([a for _, a in p], 1024, 16, bf16) for p in wparts]
    Wf = {}

    def unpack_weights(gathered, part):
        flat = gathered.reshape(NSH, -1)
        off = 0
        for n, a in part:
            sz = int(np.prod(a.shape))
            Wf[n] = _unshard(flat[:, off:off + sz].reshape((NSH,) + a.shape), n)[0]
            off += sz

    unpack_weights(all_gather(wpack[0], XY, "gather_w0"), wparts[0])

    sflat = _pack([c] + [W[n] for n in SMALL_SHARDED], 1024, 8, f32)
    s8 = all_gather(sflat, AXES, "gather_small")
    s8f = s8.reshape(8, -1)
    c_all = s8f[:, :D]
    Ws = {}
    off = D
    for n in SMALL_SHARDED:
        sz = int(np.prod(W[n].shape))
        Ws[n] = _unshard(s8f[0::2, off:off + sz].reshape((NSH,) + W[n].shape), n)
        off += sz

    modp = ada_fwd(c_all, ada_w)
    modg = all_gather(modp, ("x", "y"), "gather_mod")
    mod_all = jnp.moveaxis(modg, 0, 2).reshape(2, 8, -1) + ada_b[:, None, :]
    mod = lax.dynamic_slice(mod_all, (0, me8, 0), (2, 1, mod_all.shape[2]))[:, 0, :]
    mods = [[mod[l:l + 1, j * D:(j + 1) * D] for j in range(6)] for l in range(2)]

    sh1, sc1, g1, sh2, sc2, g2_ = mods[0]
    nm0, nf0 = norm_mix[0:1], norm_ffn[0:1]
    sinkb = jnp.repeat(attn_sinks[0], HEAD_DIM)[None]
    h0 = modulate_fwd(xs, nm0, sc1, sh1, "l0_mod1_fwd")
    hin0 = mm(h0, Wf["attn_w_in"], name="l0_in_proj")
    ocat, att_saved, got = attention_fwd(hin0, attn_q_norm_a, attn_k_norm_a, attn_q_norm_b, attn_k_norm_b, sinkb,
                                         ride=(wpack[3], XY, "gather"))
    unpack_weights(got, wparts[3])
    y0, x1 = mm(ocat, Wf["attn_w_out"], name="l0_out_proj", out_dtypes=(f32, f32), epi=_resid_epi,
                epi_mn=[xs], epi_n=[g1])
    x2, ffn0_saved, got = _ffn_fwd(x1, nf0, sc2, sh2, g2_, Wf["ffn_w_up0"], Ws["ffn_conv"][0], Wf["ffn_w_down0"], "l0",
                                   rides=[(wpack[1], XY, "gather"), (wpack[2], XY, "gather")])
    unpack_weights(got[0], wparts[1])
    unpack_weights(got[1], wparts[2])
    rec_w_in_p = _rec_pad_cols(Wf["rec_w_in"])

    th1, tc1, t1, th2, tc2, t2 = mods[1]
    nm1, nf1 = norm_mix[1:2], norm_ffn[1:2]
    pad128 = lambda a: jnp.pad(a, ((0, 0), (0, 128 - a.shape[1])))
    s5p = dict(lr=s5_lambda_re[0], li=s5_lambda_im[0], ldt=s5_log_dt[0][:, None], b_re=s5_b_re[0], b_im=s5_b_im[0],
               c_re=s5_c_re[0], c_im=s5_c_im[0], d=Ws["s5_d"], gw=Ws["s5_glu_w"][0], gb=Ws["s5_glu_b"])
    dnp = dict(conv=Ws["dn_conv"][0], alog=pad128(dn_a_log), dtb=pad128(dn_dt_bias), onorm=dn_out_norm)
    h1 = modulate_fwd(x2, nm1, tc1, th1, "l1_mod1_fwd")
    hin1 = mm(h1, rec_w_in_p, name="l1_in_proj")
    yc, s5_saved = s5_fwd(hin1, s5p)
    ycat, dn_saved = deltanet_fwd(hin1, dnp, yc)
    y1, x3 = mm(ycat, Wf["rec_w_out"], name="l1_out_proj", out_dtypes=(f32, f32), epi=_resid_epi,
                epi_mn=[x2], epi_n=[t1])
    x4, ffn1_saved, _ = _ffn_fwd(x3, nf1, tc2, th2, t2, Wf["ffn_w_up1"], Ws["ffn_conv"][1], Wf["ffn_w_down1"], "l1")

    dx, sse = loss_fwd_bwd(x4, target)
    loss = lax.psum(0.5 * sse[0, 0] / D, AXES)

    dx, gf1, _ = _ffn_bwd(dx, x3, nf1, tc2, th2, t2, Wf["ffn_w_up1"], Ws["ffn_conv"][1], Wf["ffn_w_down1"], ffn1_saved, "l1")
    dy1, dt1 = resid_bwd(dx, y1, t1, "l1_res1_bwd")
    dycat = mm(dy1, Wf["rec_w_out"], tb=True, name="l1_dycat")
    dw_rec_out = mm(ycat, dy1, ta=True, name="l1_dwout", out_dtypes=GRAD_WIRE)
    du_skip, du_b, s5g = s5_bwd(hin1, s5p, s5_saved, dycat)
    dx_qkv, dz, dab, dng = deltanet_bwd(hin1, dnp, dn_saved, dycat)
    dhin1 = rec_assemble(dx_qkv, dz, du_skip, du_b, dab)
    dw_rec_in = _rec_unpad_cols(mm(h1, dhin1, ta=True, name="l1_dwin", out_dtypes=GRAD_WIRE))
    dh1 = mm(dhin1, rec_w_in_p, tb=True, name="l1_dh")
    dx, dnm1, dtc1, dth1 = modulate_bwd(x2, nm1, tc1, th1, dh1, dx, "l1_mod1_bwd")

    def grad_part(items):
        flat = jnp.concatenate([_to_shards(g, n).reshape(NSH, -1) for n, g in items], axis=1)
        unit = 256 * 1024
        npad = -(-flat.shape[1] // unit) * unit
        return jnp.pad(flat, ((0, 0), (0, npad - flat.shape[1]))).reshape(NSH, npad // 1024, 1024)

    w_dn1 = gf1["w_dn"][None]
    part2 = lambda dw_dn0: (grad_part([("ffn_w_down1", w_dn1), ("ffn_w_down0", dw_dn0[None])]), XY, "exchange")
    gparts = [[("rec_w_in", dw_rec_in[None]), ("rec_w_out", dw_rec_out[None])], [("ffn_w_up1", gf1["w_up"][None])]]
    dx, gf0, gq = _ffn_bwd(dx, x1, nf0, sc2, sh2, g2_, Wf["ffn_w_up0"], Ws["ffn_conv"][0], Wf["ffn_w_down0"], ffn0_saved,
                           "l0", rides=[(grad_part(gparts[0]), XY, "exchange"), (grad_part(gparts[1]), XY, "exchange"), part2])
    gparts.append([("ffn_w_down1", w_dn1), ("ffn_w_down0", gf0["w_dn"][None])])
    dy0, dg1 = resid_bwd(dx, y0, g1, "l0_res1_bwd")
    dcat = mm(dy0, Wf["attn_w_out"], tb=True, name="l0_dcat")
    dw_attn_out = mm(ocat, dy0, ta=True, name="l0_dwout", out_dtypes=GRAD_WIRE)
    gparts += [[("ffn_w_up0", gf0["w_up"][None])], [("attn_w_out", dw_attn_out[None])]]
    chip_sum = lambda qs, i0: jnp.concatenate([sum_slots(q, f"sum_chips{i0 + i}", bf16) for i, q in enumerate(qs)], axis=0)
    dhin0, dwqa, dwka, dwqb, dwkb, dsinkb, gots = attention_bwd(
        hin0, attn_q_norm_a, attn_k_norm_a, attn_q_norm_b, attn_k_norm_b, sinkb, att_saved, dcat,
        rides={"a": (grad_part(gparts[3]), XY, "exchange"), 1: (chip_sum(gq, 0), ("c",), "gather"),
               4: (grad_part(gparts[4]), XY, "exchange"), 16: lambda g: (chip_sum([g["a"]], 3), ("c",), "gather")})
    dw_attn_in = mm(h0, dhin0, ta=True, name="l0_dwin", out_dtypes=GRAD_WIRE)
    gparts.append([("attn_w_in", dw_attn_in[None])])
    dh0, gq5 = mm(dhin0, Wf["attn_w_in"], tb=True, name="l0_dh", ride=(grad_part(gparts[5]), XY, "exchange"))
    grad_x, dnm0, dsc1, dsh1 = modulate_bwd(xs, nm0, sc1, sh1, dh0, dx, "l0_mod1_bwd")

    dmod = jnp.concatenate([
        jnp.concatenate([dsh1, dsc1, dg1, gf0["sh"], gf0["sc"], gf0["gate"]], axis=1),
        jnp.concatenate([dth1, dtc1, dt1, gf1["sh"], gf1["sc"], gf1["gate"]], axis=1)], axis=0)
    gl = {
        "ada_b": dmod,
        "norm_mix": jnp.concatenate([dnm0, dnm1], axis=0),
        "norm_ffn": jnp.concatenate([gf0["nf"], gf1["nf"]], axis=0),
        "attn_q_norm_a": dwqa, "attn_k_norm_a": dwka, "attn_q_norm_b": dwqb, "attn_k_norm_b": dwkb,
        "attn_sinks": dsinkb[:, ::HEAD_DIM],
        "s5_lambda_re": s5g["lr"][None], "s5_lambda_im": s5g["li"][None], "s5_log_dt": s5g["ldt"][:, 0][None],
        "s5_b_re": s5g["b_re"][None], "s5_b_im": s5g["b_im"][None], "s5_c_re": s5g["c_re"][None],
        "s5_c_im": s5g["c_im"][None],
        "dn_a_log": dng["alog"][:, :DN_HEADS], "dn_dt_bias": dng["dtb"][:, :DN_HEADS], "dn_out_norm": dng["onorm"],
        "s5_d": s5g["d"], "s5_glu_w": s5g["gw"][None], "s5_glu_b": s5g["gb"], "dn_conv": dng["conv"][None],
        "ffn_conv": jnp.stack([gf0["conv"], gf1["conv"]]),
    }

    small_names = SMALL_REPL + SMALL_SHARDED
    gs = _pack([gl[n] for n in small_names], 128, 256, f32)
    gs8 = all_gather(gs, AXES, "gather_small_grads")
    gsum = sum_slots(gs8, "sum_small_grads")
    full_shapes = [gl[n].shape for n in small_names]
    gfull = dict(zip(small_names, _unpack(gsum, full_shapes)))
    dmod_all = gs8.reshape(8, -1)[:, :2 * 6 * D].reshape(8, 2, 6 * D)
    ncol = ada_w.shape[2]
    dmod_sh = jnp.moveaxis(lax.dynamic_slice(dmod_all, (0, 0, shard * ncol), (8, 2, ncol)), 0, 1)
    grads = {"ada_w": ada_bwd(c_all, dmod_sh)}
    for n in SMALL_REPL:
        grads[n] = gfull[n]
    for n in SMALL_SHARDED:
        sh_all = _to_shards(gfull[n], n)
        grads[n] = lax.dynamic_slice(sh_all, (shard,) + (0,) * (sh_all.ndim - 1), (1,) + sh_all.shape[1:])[0]

    gq = list(gq) + [gots["a"], gots[4], gq5]
    gc45 = all_gather(chip_sum(gq[4:], 4), ("c",), "gather_grad_c")
    gsh = jnp.concatenate([sum_slots(gots[1], "sum_pair012"), sum_slots(gots[16], "sum_pair3"),
                           sum_slots(gc45, "sum_pair45")], axis=0)
    row, got = 0, {}
    for part, q in zip(gparts, gq):
        flat = gsh[row:row + q.shape[1]].reshape(-1)
        row += q.shape[1]
        off = 0
        for n, g in part:
            sz = g.size // NSH
            got[n] = flat[off:off + sz].reshape((1,) + g.shape[1:-2] + _to_shards(g, n).shape[-2:])
            off += sz
    for n in ("attn_w_in", "attn_w_out", "rec_w_in", "rec_w_out"):
        grads[n] = got[n]
    grads["ffn_w_up"] = jnp.concatenate([got["ffn_w_up0"], got["ffn_w_up1"]], axis=0)
    grads["ffn_w_down"] = jnp.concatenate([got["ffn_w_down0"], got["ffn_w_down1"]], axis=0)

    delta, new_m, new_v = {}, {}, {}

    def as2d(a):
        return a.reshape(-1, a.shape[-1])

    for n in ["ada_w"] + BIG:
        d_, m_, v_ = adamw(as2d(W[n]), as2d(grads[n]), as2d(Mo[n]), as2d(Vo[n]), f"adamw_{n}")
        delta[n], new_m[n], new_v[n] = d_.reshape(W[n].shape), m_.reshape(W[n].shape), v_.reshape(W[n].shape)
    pk = lambda dd: _pack([dd[n] for n in small_names], 128, 256, f32)
    d_, m_, v_ = adamw(pk(W), pk(grads), pk(Mo), pk(Vo), "adamw_small")
    shp = [W[n].shape for n in small_names]
    for dst, src in ((delta, d_), (new_m, m_), (new_v, v_)):
        dst.update(zip(small_names, _unpack(src, shp)))

    return (loss, grad_x[None], *[grads[n] for n in W_NAMES], *[delta[n] for n in W_NAMES],
            *[new_m[n] for n in W_NAMES], *[new_v[n] for n in W_NAMES])
```

```python
import functools
import math

import numpy as np
import jax
import jax.numpy as jnp
from jax import lax
from jax.experimental import pallas as pl
from jax.experimental.pallas import tpu as pltpu

f32 = jnp.float32
bf16 = jnp.bfloat16
HI = lax.Precision.HIGHEST
MESH = pl.DeviceIdType.MESH

HEAD_DIM = 64
BLOCK = 128
A_Q_HEADS = 8
A_KV_HEADS = 2
A_WINDOW = 128
B_HEADS = 8
B_BRANCHES = ((128, 1), (512, 4), (2048, 16))
N_ATTN_HEADS = 16
ATTN_IN = 2304
S5_GROUP = 16
S5_GROUPS = 16
S5_WIDTH = 256
S5_STATE = 64
DN_HEADS = 6
DN_DK = 128
DN_CONV = 4
DN_CHUNK = 64
REC_IN = 3340
REC_PAD = 3584
FFN_CONV = 3
EPS = 1e-6
ADAM_LR = 0.001
ADAM_B1 = 0.9
ADAM_B2 = 0.999
ADAM_EPS = 1e-08
ADAM_WD = 0.01
ADAM_STEP = 10

LANE = 128
SUBLANE = 8
VMEM_LIMIT = 52 * 1024 * 1024
MM_FULL_K = 5632
MM_VMEM_BUDGET = 40 * 1024 * 1024


def _cp(*sem):
    return pltpu.CompilerParams(dimension_semantics=sem, vmem_limit_bytes=VMEM_LIMIT)


def _pick(dim, cap, unit=LANE):
    for t in (2048, 1024, 768, 512, 384, 256, 128, 64, 32, 16, 8):
        if t <= cap and t % unit == 0 and dim % t == 0:
            return t
    return dim


def _dot(a, b, dims=(((1,), (0,)), ((), ())), precision=None):
    return lax.dot_general(a, b, dims, precision=precision, preferred_element_type=f32)


NN = (((1,), (0,)), ((), ()))
NT = (((1,), (1,)), ((), ()))
TN = (((0,), (0,)), ((), ()))


def mm(a, b, *, name, ta=False, tb=False, a_win=None, b_win=None, out_dtypes=(f32,),
       epi=None, epi_mn=(), epi_n=(), tm_cap=1024, tn_cap=8192, tk_cap=None, ride=None):
    coll = _Coll(*ride) if ride else None
    a0, a1 = a.shape
    b0, b1 = b.shape
    aw = a_win or (0, a1)
    bw = b_win or (0, b1)
    if ta:
        K, M = a0, aw[1]
    else:
        M, K = a0, aw[1]
    if tb:
        N, K2 = b0, bw[1]
    else:
        K2, N = b0, bw[1]
    assert K == K2, (a.shape, b.shape, ta, tb, a_win, b_win)
    if tk_cap is None:
        tk_cap = K if K <= MM_FULL_K else 2048
    tk = _pick(K, tk_cap, SUBLANE if (ta and not tb) else LANE)
    nk = K // tk
    sa, sb = a.dtype.itemsize, b.dtype.itemsize
    so = sum(jnp.dtype(d).itemsize for d in out_dtypes)
    n_mn, n_n, n_out = len(epi_mn), len(epi_n), len(out_dtypes)

    def vmem(tm_, tn_):
        return 2 * (tm_ * tk * sa + tk * tn_ * sb + tm_ * tn_ * (so + 4 * n_mn)) + 2 * tm_ * tn_ * 4

    best = None
    for tm_ in (t for t in (1024, 512, 256, 128) if M % t == 0 and (not ta or aw[0] % t == 0)):
        for tn_ in (t for t in (N, N // 2, 1024, 768, 512, 384, 256, 128)
                    if t % LANE == 0 and N % t == 0 and (tb or bw[0] % t == 0)):
            if tm_ <= tm_cap and tn_ <= max(tn_cap, 0) and vmem(tm_, tn_) <= MM_VMEM_BUDGET:
                if best is None or (tm_ * tn_, tn_) > (best[0] * best[1], best[1]):
                    best = (tm_, tn_)
    assert best is not None, (name, M, N, K)
    tm, tn = best
    b_outer = tk * tn * sb > tm * tk * sa

    def ix(f):
        if b_outer:
            return lambda j, i, k: f(i, j, k)
        return f

    if ta:
        mo = aw[0] // tm
        a_spec = pl.BlockSpec((tk, tm), ix(lambda i, j, k: (k, i + mo)))
    else:
        assert aw[0] % tk == 0
        ko = aw[0] // tk
        a_spec = pl.BlockSpec((tm, tk), ix(lambda i, j, k: (i, k + ko)))
    if tb:
        assert bw[0] % tk == 0
        kob = bw[0] // tk
        b_spec = pl.BlockSpec((tn, tk), ix(lambda i, j, k: (j, k + kob)))
    else:
        no = bw[0] // tn
        b_spec = pl.BlockSpec((tk, tn), ix(lambda i, j, k: (k, j + no)))
    dims = (((0 if ta else 1,), (1 if tb else 0,)), ((), ()))

    gi, gj = M // tm, N // tn
    grid = (gj, gi, nk) if b_outer else (gi, gj, nk)

    def body(a_ref, b_ref, *rest):
        mn_refs = rest[:n_mn]
        n_refs = rest[n_mn:n_mn + n_n]
        o0 = n_mn + n_n
        out_refs = rest[o0:o0 + n_out]

        def finish(r):
            if epi is None:
                outs = (r,)
            else:
                outs = epi(r, *[m[...] for m in mn_refs], *[v[...] for v in n_refs])
            for o_ref, o in zip(out_refs, outs):
                o_ref[...] = o.astype(o_ref.dtype)

        part = _dot(a_ref[...].astype(bf16), b_ref[...].astype(bf16), dims)
        if nk == 1:
            finish(part)
        else:
            acc = rest[o0 + n_out]
            k = pl.program_id(2)

            @pl.when(k == 0)
            def _():
                acc[...] = part

            @pl.when(k > 0)
            def _():
                acc[...] += part

            @pl.when(k == nk - 1)
            def _():
                finish(acc[...])

    mn_spec = pl.BlockSpec((tm, tn), ix(lambda i, j, k: (i, j)))
    n_spec = pl.BlockSpec((1, tn), ix(lambda i, j, k: (0, j)))
    outs = _ride_call(
        body, coll, ride, grid=grid,
        in_specs=[a_spec, b_spec] + [mn_spec] * n_mn + [n_spec] * n_n, out_specs=[mn_spec] * n_out,
        out_shape=[jax.ShapeDtypeStruct((M, N), d) for d in out_dtypes],
        scratch_shapes=[pltpu.VMEM((tm, tn), f32)] if nk > 1 else [],
        semantics=("parallel", "parallel", "arbitrary"), name=name, args=[a, b, *epi_mn, *epi_n])
    return outs[0] if len(outs) == 1 else tuple(outs)


def rowwise(fn, *, name, L, tm, rows=(), consts=(), outs=(), sums=()):
    nb = L // tm
    in_specs = []
    arrs = []
    for arr, start, width, kind in rows:
        assert start % width == 0, (name, start, width)
        co = start // width
        hr = SUBLANE * (4 // arr.dtype.itemsize)
        hb = tm // hr
        if kind == "cur":
            in_specs.append(pl.BlockSpec((tm, width), lambda i, co=co: (i, co)))
        elif kind == "prev":
            in_specs.append(pl.BlockSpec((hr, width), lambda i, co=co, hb=hb: (jnp.maximum(i * hb - 1, 0), co)))
        else:
            last = L // hr - 1
            in_specs.append(pl.BlockSpec((hr, width), lambda i, co=co, hb=hb, last=last:
                                         (jnp.minimum((i + 1) * hb, last), co)))
        arrs.append(arr)
    for cst in consts:
        assert cst.ndim == 2
        in_specs.append(pl.BlockSpec(cst.shape, lambda i: (0, 0)))
        arrs.append(cst)
    n_rows, n_c, n_o, n_s = len(rows), len(consts), len(outs), len(sums)
    out_specs = [pl.BlockSpec((tm, w), lambda i: (i, 0)) for w, _ in outs]
    out_specs += [pl.BlockSpec(s, lambda i: (0, 0)) for s in sums]
    out_shape = [jax.ShapeDtypeStruct((L, w), d) for w, d in outs]
    out_shape += [jax.ShapeDtypeStruct(s, f32) for s in sums]

    def body(*refs):
        i = pl.program_id(0)
        vals = [r[...] for r in refs[:n_rows + n_c]]
        res = fn(i, nb, *vals)
        if not isinstance(res, (tuple, list)):
            res = (res,)
        o_refs = refs[n_rows + n_c:n_rows + n_c + n_o]
        s_refs = refs[n_rows + n_c + n_o:]
        for o_ref, o in zip(o_refs, res[:n_o]):
            o_ref[...] = o.astype(o_ref.dtype)
        if n_s:
            @pl.when(i == 0)
            def _():
                for s_ref in s_refs:
                    s_ref[...] = jnp.zeros_like(s_ref)

            for s_ref, s in zip(s_refs, res[n_o:]):
                s_ref[...] += s

    res = pl.pallas_call(
        body,
        grid=(nb,),
        in_specs=in_specs,
        out_specs=out_specs,
        out_shape=out_shape,
        compiler_params=_cp("arbitrary" if n_s else "parallel"),
        name=name,
    )(*arrs)
    return res[0] if len(res) == 1 else tuple(res)


def _shift_down(x, prev8, k):
    cat = jnp.concatenate([prev8, x], axis=0)
    return pltpu.roll(cat, k, 0)[prev8.shape[0]:, :]


def _shift_up(x, next8, k):
    n = x.shape[0]
    cat = jnp.concatenate([x, next8], axis=0)
    return pltpu.roll(cat, n + next8.shape[0] - k, 0)[:n, :]


def _colsum(x):
    return jnp.sum(x, axis=0, keepdims=True)


def _silu(x):
    return x * jax.nn.sigmoid(x)


def _modulate_fn(x, nw, sc, sh):
    r = lax.rsqrt(jnp.mean(x * x, axis=-1, keepdims=True) + EPS)
    return (x * r * nw) * (1.0 + sc) + sh


def modulate_fwd(x, nw, sc, sh, name):
    L, D = x.shape

    def fn(i, nb, xt, nwv, scv, shv):
        return _modulate_fn(xt, nwv, scv, shv)

    return rowwise(fn, name=name, L=L, tm=_pick(L, 512, SUBLANE), rows=[(x, 0, D, "cur")],
                   consts=[nw, sc, sh], outs=[(D, bf16)])


def modulate_bwd(x, nw, sc, sh, dh, dx_in, name):
    L, D = x.shape

    def fn(i, nb, xt, dht, dxt, nwv, scv, shv):
        _, vjp = jax.vjp(_modulate_fn, xt, nwv, scv, shv)
        dx, dnw, dsc, dsh = vjp(dht)
        return dxt + dx, dnw, dsc, dsh

    return rowwise(fn, name=name, L=L, tm=_pick(L, 256, SUBLANE),
                   rows=[(x, 0, D, "cur"), (dh, 0, D, "cur"), (dx_in, 0, D, "cur")],
                   consts=[nw, sc, sh], outs=[(D, f32)], sums=[(1, D)] * 3)


def resid_bwd(dx, y, g, name):
    L, D = dx.shape

    def fn(i, nb, dxt, yt, gv):
        return dxt * gv, _colsum(dxt * yt)

    return rowwise(fn, name=name, L=L, tm=_pick(L, 512, SUBLANE),
                   rows=[(dx, 0, D, "cur"), (y, 0, D, "cur")], consts=[g],
                   outs=[(D, bf16)], sums=[(1, D)])


def _resid_epi(acc, xt, gv):
    return acc, xt + gv * acc


def _stack_rows(rows, n=SUBLANE):
    c = rows[0].shape[1]
    ridx = lax.broadcasted_iota(jnp.int32, (n, c), 0)
    out = jnp.zeros((n, c), f32)
    for j, r in enumerate(rows):
        out = out + jnp.where(ridx == j, r, 0.0)
    return out


def _conv_causal(x, prev8, w):
    W = w.shape[0]
    y = x * w[W - 1:W, :]
    for j in range(W - 1):
        y = y + _shift_down(x, prev8, W - 1 - j) * w[j:j + 1, :]
    return y


def _conv_causal_bwd_x(dy, next8, w):
    W = w.shape[0]
    dx = dy * w[W - 1:W, :]
    for j in range(W - 1):
        dx = dx + _shift_up(dy, next8, W - 1 - j) * w[j:j + 1, :]
    return dx


def _conv_causal_bwd_w(dy, x, prev8, W):
    rows = [_colsum(dy * _shift_down(x, prev8, W - 1 - j)) for j in range(W - 1)]
    rows.append(_colsum(dy * x))
    return _stack_rows(rows)


def ffn_act_fwd(up, conv_w, name):
    L, F2 = up.shape
    F = F2 // 2

    def fn(i, nb, u, p8, w):
        c = _conv_causal(u.astype(f32), p8.astype(f32) * (i > 0).astype(f32), w)
        return _silu(c[:, :F]) * c[:, F:]

    return rowwise(fn, name=name, L=L, tm=_pick(L, 128, SUBLANE),
                   rows=[(up, 0, F2, "cur"), (up, 0, F2, "prev")], consts=[conv_w], outs=[(F, bf16)])


def ffn_act_conv_bwd(up, conv_w, dact, name):
    L, F2 = up.shape
    F = F2 // 2
    W = conv_w.shape[0]

    def fn(i, nb, u, da, p8, un8, dan8, w):
        tm, ext = u.shape[0], un8.shape[0]
        more = (i < nb - 1).astype(f32)
        u, da = u.astype(f32), da.astype(f32)
        p8 = p8.astype(f32) * (i > 0).astype(f32)
        c = _conv_causal(jnp.concatenate([u, un8.astype(f32) * more], axis=0), p8, w)
        dae = jnp.concatenate([da, dan8.astype(f32) * more], axis=0)
        a, b = c[:, :F], c[:, F:]
        sg = jax.nn.sigmoid(a)
        dc = jnp.concatenate([dae * b * (sg * (1.0 + a * (1.0 - sg))), dae * a * sg], axis=1)
        dx = dc[:tm] * w[W - 1:W, :]
        for j in range(W - 1):
            dx = dx + pltpu.roll(dc, tm + ext - (W - 1 - j), 0)[:tm] * w[j:j + 1, :]
        return dx, _conv_causal_bwd_w(dc[:tm], u, p8, W)

    return rowwise(fn, name=name, L=L, tm=_pick(L, 128, SUBLANE),
                   rows=[(up, 0, F2, "cur"), (dact, 0, F, "cur"), (up, 0, F2, "prev"), (up, 0, F2, "next"),
                         (dact, 0, F, "next")],
                   consts=[conv_w], outs=[(F2, bf16)], sums=[(SUBLANE, F2)])


ALIBI = [2.0 ** (-8.0 * (i + 1) / N_ATTN_HEADS) for i in range(N_ATTN_HEADS)]
NEG = -1e30


def _band_mask(n, d, max_dist):
    qi = lax.broadcasted_iota(jnp.int32, (BLOCK, 2 * BLOCK), 0)
    kj = lax.broadcasted_iota(jnp.int32, (BLOCK, 2 * BLOCK), 1)
    dist = BLOCK + qi - kj
    valid = (dist >= 0) & (dist <= max_dist) & ((n > 0) | (kj >= BLOCK))
    return valid, -(d * dist).astype(f32)


def _rms64(x, w):
    r = lax.rsqrt(jnp.mean(x * x, axis=-1, keepdims=True) + EPS)
    xh = x * r
    return xh * w, xh, r


def _rms64_bwd(dy, xh, r, w):
    t = dy * w
    dw = jnp.sum(jnp.sum(dy * xh, axis=0), axis=0, keepdims=True)
    return r * (t - xh * jnp.mean(t * xh, axis=-1, keepdims=True)), dw


class _Plan:
    def __init__(self, dilation, group_a, nq):
        self.d, self.nq = dilation, nq
        if group_a:
            self.P, self.nkv = 1, 1
            self.q0, self.k0, self.v0 = 0, 4, 5
            self.kv_of = lambda j: j // 4
            self.max_dist = A_WINDOW - 1
            slopes = ALIBI[:8]
        else:
            self.P, self.nkv = 4 // nq, nq
            self.q0, self.k0, self.v0 = 6, 10, 14
            self.kv_of = lambda j: j
            self.max_dist = BLOCK
            slopes = ALIBI[8:]
        self.hps = 2 * nq
        sl = np.repeat(np.asarray(slopes, np.float32), HEAD_DIM).reshape(self.P, 1, self.hps * HEAD_DIM)
        self.slopes = jnp.asarray(sl, f32)


def _rows(r, d):
    return pl.ds(r, BLOCK, stride=d) if d > 1 else pl.ds(0, BLOCK)


def _pairs(refs, rows):
    parts = []
    for ref in refs:
        blk = ref[rows, :]
        parts += [blk[:, :HEAD_DIM], blk[:, HEAD_DIM:]]
    return jnp.stack(parts)


def _pairs2(prev_refs, cur_refs, rows):
    parts = []
    for pr, cr in zip(prev_refs, cur_refs):
        blk = jnp.concatenate([pr[rows, :], cr[rows, :]], axis=0)
        parts += [blk[:, :HEAD_DIM], blk[:, HEAD_DIM:]]
    return jnp.stack(parts)


def _lane_pair(t, i):
    return jnp.concatenate([t[2 * i], t[2 * i + 1]], axis=1)


def _riding(body, coll, n_in, n_out, grid):
    if coll is None:
        return body

    def wrapped(*refs):
        ride_refs = (refs[n_in], refs[n_in + 1 + n_out]) + tuple(refs[-3:])
        inner = refs[:n_in] + refs[n_in + 1:n_in + 1 + n_out] + refs[n_in + 2 + n_out:-3]
        pid = [pl.program_id(t) for t in range(len(grid))]

        @pl.when(functools.reduce(jnp.logical_and, [p == 0 for p in pid]))
        def _():
            coll.start(*ride_refs)

        body(*inner)

        @pl.when(functools.reduce(jnp.logical_and, [p == g - 1 for p, g in zip(pid, grid)]))
        def _():
            coll.wait(*ride_refs)

    return wrapped


def _ride_call(body, coll, ride, *, grid, in_specs, out_specs, out_shape, scratch_shapes, semantics, name, args):
    hbm = [pl.BlockSpec(memory_space=pl.ANY)] if coll else []
    return pl.pallas_call(
        _riding(body, coll, len(in_specs), len(out_specs), grid), grid=grid,
        in_specs=list(in_specs) + hbm, out_specs=list(out_specs) + hbm,
        out_shape=list(out_shape) + ([coll.out_shape] if coll else []),
        scratch_shapes=list(scratch_shapes) + (coll.scratch if coll else []),
        compiler_params=_cp(*(["arbitrary"] * len(grid) if coll else semantics)), name=name,
    )(*args, *([ride[0]] if coll else []))


def attn2_fwd(hin, plan, wq, wk, name, ride=None):
    coll = _Coll(*ride) if ride else None
    L = hin.shape[0]
    d, nq, nkv, hps, P = plan.d, plan.nq, plan.nkv, plan.hps, plan.P
    R = BLOCK * d
    nb = L // R
    kv_of, max_dist = plan.kv_of, plan.max_dist
    gqa = 2 * nkv != hps

    def body(*refs):
        q_refs = refs[:nq]
        kp, kc = refs[nq:nq + nkv], refs[nq + nkv:nq + 2 * nkv]
        vp, vc = refs[nq + 2 * nkv:nq + 3 * nkv], refs[nq + 3 * nkv:nq + 4 * nkv]
        sl_ref, wq_ref, wk_ref, o_ref, lse_ref = refs[nq + 4 * nkv:nq + 4 * nkv + 5]
        o_refs = refs[nq + 4 * nkv + 5:2 * nq + 4 * nkv + 5]
        lse_refs = refs[2 * nq + 4 * nkv + 5:]
        n = pl.program_id(1)
        valid, negd = _band_mask(n, d, max_dist)
        slope = jnp.stack([sl_ref[0, :, j * 64:j * 64 + 1] for j in range(hps)])
        wqv, wkv = wq_ref[...], wk_ref[...]

        def residue(r, carry):
            rows = _rows(r, d)
            kn = _rms64(_pairs2(kp, kc, rows), wkv)[0].astype(bf16)
            v = _pairs2(vp, vc, rows).astype(bf16)
            if gqa:
                kn = jnp.stack([kn[kv_of(j)] for j in range(hps)])
                v = jnp.stack([v[kv_of(j)] for j in range(hps)])
            qn = _rms64(_pairs(q_refs, rows), wqv)[0].astype(bf16)
            s = _dot(qn, kn, BNT) * (HEAD_DIM ** -0.5) + slope * negd
            s = jnp.where(valid, s, NEG)
            m = jnp.max(s, axis=-1, keepdims=True)
            p = jnp.exp(s - m)
            l = jnp.sum(p, axis=-1, keepdims=True)
            o = _dot(p.astype(bf16), v, BNN) / l
            lse = jnp.broadcast_to(m + jnp.log(l), (hps, BLOCK, HEAD_DIM))
            for i in range(nq):
                o_refs[i][rows, :] = _lane_pair(o, i)
                lse_refs[i][rows, :] = _lane_pair(lse, i)
            return carry

        lax.fori_loop(0, d, residue, 0)
        for i in range(nq):
            o_ref[:, i * 128:(i + 1) * 128] = o_refs[i][...]
            lse_ref[:, i * 128:(i + 1) * 128] = lse_refs[i][...]

    col = lambda c0, i: (lambda p, n: (n, c0 + p * nq + i))
    prv = lambda c0, i: (lambda p, n: (jnp.maximum(n - 1, 0), c0 + p * nq + i))
    blk = lambda f: pl.BlockSpec((R, 128), f)
    in_specs = [blk(col(plan.q0, i)) for i in range(nq)]
    in_specs += [blk(prv(plan.k0, i)) for i in range(nkv)] + [blk(col(plan.k0, i)) for i in range(nkv)]
    in_specs += [blk(prv(plan.v0, i)) for i in range(nkv)] + [blk(col(plan.v0, i)) for i in range(nkv)]
    in_specs += [pl.BlockSpec((1, 1, hps * 64), lambda p, n: (p, 0, 0)),
                 pl.BlockSpec((1, 64), lambda p, n: (0, 0)), pl.BlockSpec((1, 64), lambda p, n: (0, 0))]
    wide = pl.BlockSpec((R, 128 * nq), lambda p, n: (n, p))
    return _ride_call(
        body, coll, ride, grid=(P, nb), in_specs=in_specs, out_specs=[wide, wide],
        out_shape=[jax.ShapeDtypeStruct((L, 512), f32)] * 2,
        scratch_shapes=[pltpu.VMEM((R, 128), f32)] * (2 * nq),
        semantics=("parallel", "parallel"), name=name,
        args=[hin] * (nq + 4 * nkv) + [plan.slopes, wq, wk])


def attn2_bwd(hin, plan, wq, wk, o, lse, do, dlse, dw0, name, ride=None):
    coll = _Coll(*ride) if ride else None
    L = hin.shape[0]
    d, nq, nkv, hps, P = plan.d, plan.nq, plan.nkv, plan.hps, plan.P
    R = BLOCK * d
    nb = L // R
    kv_of, max_dist = plan.kv_of, plan.max_dist
    nkh = 2 * nkv
    gqa = nkh != hps
    n_in = nq + 4 * nkv + 3 + 4 * nq + 2

    def body(*refs):
        q_refs = refs[:nq]
        kp, kc = refs[nq:nq + nkv], refs[nq + nkv:nq + 2 * nkv]
        vp, vc = refs[nq + 2 * nkv:nq + 3 * nkv], refs[nq + 3 * nkv:nq + 4 * nkv]
        b = nq + 4 * nkv
        sl_ref, wq_ref, wk_ref = refs[b:b + 3]
        b += 3
        o_refs, lse_refs = refs[b:b + nq], refs[b + nq:b + 2 * nq]
        do_refs, dlse_refs = refs[b + 2 * nq:b + 3 * nq], refs[b + 3 * nq:b + 4 * nq]
        dwq0_ref, dwk0_ref = refs[b + 4 * nq:b + 4 * nq + 2]
        dq_ref, dk_ref, dv_ref, dwq_ref, dwk_ref = refs[n_in:n_in + 5]
        sc = refs[n_in + 5:]
        dq_s, dk_s, dv_s = sc[:nq], sc[nq:nq + nkv], sc[nq + nkv:nq + 2 * nkv]
        ck, cv = sc[nq + 2 * nkv:nq + 3 * nkv], sc[nq + 3 * nkv:]
        pp = pl.program_id(0)
        n = pl.program_id(1)

        @pl.when((pp == 0) & (n == 0))
        def _():
            dwq_ref[...] = dwq0_ref[...]
            dwk_ref[...] = dwk0_ref[...]

        @pl.when(n == 0)
        def _():
            for c in (*ck, *cv):
                c[...] = jnp.zeros_like(c)

        @pl.when(n < nb)
        def _():
            valid, negd = _band_mask(n, d, max_dist)
            slope = jnp.stack([sl_ref[0, :, j * 64:j * 64 + 1] for j in range(hps)])
            wqv, wkv = wq_ref[...], wk_ref[...]
            hs = range(hps)

            def residue(r, carry):
                rows = _rows(r, d)
                kn_f, kh, rk = _rms64(_pairs2(kp, kc, rows), wkv)
                kn = kn_f.astype(bf16)
                v = _pairs2(vp, vc, rows).astype(bf16)
                if gqa:
                    kn = jnp.stack([kn[kv_of(j)] for j in hs])
                    v = jnp.stack([v[kv_of(j)] for j in hs])
                qn_f, qh, rq = _rms64(_pairs(q_refs, rows), wqv)
                qn = qn_f.astype(bf16)
                s = _dot(qn, kn, BNT) * (HEAD_DIM ** -0.5) + slope * negd
                p = jnp.where(valid, jnp.exp(s - _pairs(lse_refs, rows)[:, :, :1]), 0.0)
                do_h = _pairs(do_refs, rows)
                delta = jnp.sum(do_h * _pairs(o_refs, rows), axis=-1, keepdims=True)
                do_b = do_h.astype(bf16)
                dp = _dot(do_b, v, BNT)
                ds = (p * (dp - delta + _pairs(dlse_refs, rows)[:, :, :1])).astype(bf16)
                dqn = _dot(ds, kn, BNN) * (HEAD_DIM ** -0.5)
                dkn = _dot(ds, qn, BTN) * (HEAD_DIM ** -0.5)
                dvv = _dot(p.astype(bf16), do_b, BTN)
                if gqa:
                    grp = lambda t: jnp.stack([sum(t[j] for j in hs if kv_of(j) == h) for h in range(nkh)])
                    dkn, dvv = grp(dkn), grp(dvv)
                dq, dwq = _rms64_bwd(dqn, qh, rq, wqv)
                dk, dwk = _rms64_bwd(dkn, kh, rk, wkv)
                for i in range(nq):
                    dq_s[i][rows, :] = _lane_pair(dq, i)
                for i in range(nkv):
                    dk_s[i][rows, :] = ck[i][rows, :] + _lane_pair(dk[:, :BLOCK], i)
                    dv_s[i][rows, :] = cv[i][rows, :] + _lane_pair(dvv[:, :BLOCK], i)
                    ck[i][rows, :] = _lane_pair(dk[:, BLOCK:], i)
                    cv[i][rows, :] = _lane_pair(dvv[:, BLOCK:], i)
                return carry[0] + dwq, carry[1] + dwk

            zero = jnp.zeros((1, HEAD_DIM), f32)
            dwq_a, dwk_a = lax.fori_loop(0, d, residue, (zero, zero))
            dwq_ref[...] += dwq_a
            dwk_ref[...] += dwk_a
            for i in range(nq):
                dq_ref[:, i * 128:(i + 1) * 128] = dq_s[i][...]
            for i in range(nkv):
                dk_ref[:, i * 128:(i + 1) * 128] = dk_s[i][...]
                dv_ref[:, i * 128:(i + 1) * 128] = dv_s[i][...]

        @pl.when(n == nb)
        def _():
            for i in range(nkv):
                dk_ref[:, i * 128:(i + 1) * 128] = ck[i][...]
                dv_ref[:, i * 128:(i + 1) * 128] = cv[i][...]

    cl = lambda n: jnp.minimum(n, nb - 1)
    pv = lambda n: jnp.maximum(jnp.minimum(n, nb - 1) - 1, 0)
    col = lambda c0, i: (lambda p, n: (cl(n), c0 + p * nq + i))
    prv = lambda c0, i: (lambda p, n: (pv(n), c0 + p * nq + i))
    blk = lambda f: pl.BlockSpec((R, 128), f)
    w64 = pl.BlockSpec((1, 64), lambda p, n: (0, 0))
    in_specs = [blk(col(plan.q0, i)) for i in range(nq)]
    in_specs += [blk(prv(plan.k0, i)) for i in range(nkv)] + [blk(col(plan.k0, i)) for i in range(nkv)]
    in_specs += [blk(prv(plan.v0, i)) for i in range(nkv)] + [blk(col(plan.v0, i)) for i in range(nkv)]
    in_specs += [pl.BlockSpec((1, 1, hps * 64), lambda p, n: (p, 0, 0)), w64, w64]
    in_specs += [blk(col(0, i)) for i in range(nq)] * 4 + [w64, w64]
    kvw = 128 * nkv
    out_specs = [pl.BlockSpec((R, 128 * nq), lambda p, n: (cl(n), p)),
                 pl.BlockSpec((R, kvw), lambda p, n: (jnp.maximum(n - 1, 0), p)),
                 pl.BlockSpec((R, kvw), lambda p, n: (jnp.maximum(n - 1, 0), p)), w64, w64]
    same = lambda a: [a] * nq
    return _ride_call(
        body, coll, ride, grid=(P, nb + 1), in_specs=in_specs, out_specs=out_specs,
        out_shape=[jax.ShapeDtypeStruct((L, 512), f32), jax.ShapeDtypeStruct((L, kvw * P), f32),
                   jax.ShapeDtypeStruct((L, kvw * P), f32), jax.ShapeDtypeStruct((1, 64), f32),
                   jax.ShapeDtypeStruct((1, 64), f32)],
        scratch_shapes=[pltpu.VMEM((R, 128), f32)] * (nq + 4 * nkv),
        semantics=("arbitrary", "arbitrary"), name=name,
        args=[hin] * (nq + 4 * nkv) + [plan.slopes, wq, wk, *same(o), *same(lse), *same(do), *same(dlse), *dw0])


def _head_sum(x):
    c = x.shape[1]
    r = lax.broadcasted_iota(jnp.int32, (c, c), 0) // HEAD_DIM
    q = lax.broadcasted_iota(jnp.int32, (c, c), 1) // HEAD_DIM
    ones = (r == q).astype(bf16)
    hi, lo = _split(x)
    return _dot(hi, ones) + _dot(lo, ones)


def attn_merge_fwd(oa, la, obs, lbs, sinkb, name):
    L = oa.shape[0]

    def fn(i, nb, oa_t, la_t, o1, o2, o3, l1, l2, l3, sk):
        ya = oa_t * jax.nn.sigmoid(la_t - sk)
        m = jnp.maximum(jnp.maximum(l1, l2), l3)
        e1, e2, e3 = jnp.exp(l1 - m), jnp.exp(l2 - m), jnp.exp(l3 - m)
        yb = (e1 * o1 + e2 * o2 + e3 * o3) / (e1 + e2 + e3)
        return jnp.concatenate([ya, yb], axis=1)

    rows = [(a, 0, 512, "cur") for a in (oa, la, *obs, *lbs)]
    return rowwise(fn, name=name, L=L, tm=_pick(L, 256, SUBLANE), rows=rows, consts=[sinkb], outs=[(1024, bf16)])


def attn_merge_bwd(dcat, oa, la, obs, lbs, sinkb, name):
    L = oa.shape[0]

    def fn(i, nb, da, db, oa_t, la_t, o1, o2, o3, l1, l2, l3, sk):
        keep = jax.nn.sigmoid(la_t - sk)
        dla = _head_sum(da * oa_t) * keep * (1.0 - keep)
        m = jnp.maximum(jnp.maximum(l1, l2), l3)
        e1, e2, e3 = jnp.exp(l1 - m), jnp.exp(l2 - m), jnp.exp(l3 - m)
        z = e1 + e2 + e3
        w1, w2, w3 = e1 / z, e2 / z, e3 / z
        g1, g2, g3 = _head_sum(db * o1), _head_sum(db * o2), _head_sum(db * o3)
        gm = w1 * g1 + w2 * g2 + w3 * g3
        return (da * keep, dla, w1 * db, w2 * db, w3 * db,
                w1 * (g1 - gm), w2 * (g2 - gm), w3 * (g3 - gm), -_colsum(dla))

    rows = [(dcat, 0, 512, "cur"), (dcat, 512, 512, "cur")] + [(a, 0, 512, "cur") for a in (oa, la, *obs, *lbs)]
    return rowwise(fn, name=name, L=L, tm=_pick(L, 256, SUBLANE), rows=rows, consts=[sinkb],
                   outs=[(512, f32)] * 8, sums=[(1, 512)])


def attn_assemble(dqa, dka, dva, dqs, dks, dvs, name):
    L = dqa.shape[0]

    def fn(i, nb, qa, ka, va, q1, q2, q3, k1, k2, k3, v1, v2, v3):
        return jnp.concatenate([qa, ka, va, q1 + q2 + q3, k1 + k2 + k3, v1 + v2 + v3], axis=1)

    rows = [(dqa, 0, 512, "cur"), (dka, 0, 128, "cur"), (dva, 0, 128, "cur")]
    rows += [(a, 0, 512, "cur") for a in (*dqs, *dks, *dvs)]
    return rowwise(fn, name=name, L=L, tm=_pick(L, 256, SUBLANE), rows=rows, outs=[(ATTN_IN, bf16)])


def attention_fwd(hin, wqa, wka, wqb, wkb, sinkb, rides=None):
    rides = rides or {}
    oa, la = attn2_fwd(hin, _Plan(1, True, 4), wqa, wka, "attn_a_fwd")
    obs, lbs, gots = [], [], {}
    for _, d in B_BRANCHES:
        res = attn2_fwd(hin, _Plan(d, False, 2), wqb, wkb, f"attn_b{d}_fwd", ride=rides.get(d))
        obs.append(res[0])
        lbs.append(res[1])
        if d in rides:
            gots[d] = res[2]
    ocat = attn_merge_fwd(oa, la, obs, lbs, sinkb, "attn_merge_fwd")
    return ocat, (oa, la, obs, lbs), gots


def attention_bwd(hin, wqa, wka, wqb, wkb, sinkb, saved, dcat, rides=None):
    rides = dict(rides or {})
    gots = {}

    def ride_of(key):
        r = rides.get(key)
        return r(gots) if callable(r) else r

    oa, la, obs, lbs = saved
    res = attn_merge_bwd(dcat, oa, la, obs, lbs, sinkb, "attn_merge_bwd")
    doa, dla, dos, dls, dsink = res[0], res[1], res[2:5], res[5:8], res[8]
    zero = jnp.zeros((1, 64), f32)
    res = attn2_bwd(hin, _Plan(1, True, 4), wqa, wka, oa, la, doa, dla, (zero, zero), "attn_a_bwd", ride=ride_of("a"))
    dqa, dka, dva, dwqa, dwka = res[:5]
    if "a" in rides:
        gots["a"] = res[5]
    dqs, dks, dvs = [], [], []
    dwqb = dwkb = zero
    for g, (_, d) in enumerate(B_BRANCHES):
        res = attn2_bwd(hin, _Plan(d, False, 2 if d < 16 else 1), wqb, wkb, obs[g], lbs[g],
                        dos[g], dls[g], (dwqb, dwkb), f"attn_b{d}_bwd", ride=ride_of(d))
        dq, dk, dv, dwqb, dwkb = res[:5]
        if d in rides:
            gots[d] = res[5]
        dqs.append(dq)
        dks.append(dk)
        dvs.append(dv)
    dhin = attn_assemble(dqa, dka, dva, dqs, dks, dvs, "attn_assemble")
    return dhin, dwqa, dwka, dwqb, dwkb, dsink, gots


NS = S5_GROUPS * S5_STATE


def _s5_param_fn(lr, li, ldt):
    dt = jnp.exp(ldt)
    mag, ang = jnp.exp(lr * dt), li * dt
    ab_re, ab_im = mag * jnp.cos(ang), mag * jnp.sin(ang)
    nr, ni = ab_re - 1.0, ab_im
    den = lr * lr + li * li
    return ab_re, ab_im, (nr * lr + ni * li) / den, (ni * lr - nr * li) / den


def s5_params_fwd(lr, li, ldt):
    def body(lr_ref, li_ref, ldt_ref, *outs):
        for o_ref, o in zip(outs, _s5_param_fn(lr_ref[...], li_ref[...], ldt_ref[...])):
            o_ref[...] = o

    return pl.pallas_call(body, out_shape=[jax.ShapeDtypeStruct(lr.shape, f32)] * 4, name="s5_params_fwd")(lr, li, ldt)


def s5_params_bwd(lr, li, ldt, cts):
    def body(lr_ref, li_ref, ldt_ref, c0, c1, c2, c3, dlr, dli, dldt):
        _, vjp = jax.vjp(_s5_param_fn, lr_ref[...], li_ref[...], ldt_ref[...])
        a, b, c = vjp((c0[...], c1[...], c2[...], c3[...]))
        dlr[...] = a
        dli[...] = b
        dldt[...] = c

    return pl.pallas_call(
        body, out_shape=[jax.ShapeDtypeStruct(lr.shape, f32), jax.ShapeDtypeStruct(li.shape, f32),
                         jax.ShapeDtypeStruct(ldt.shape, f32)], name="s5_params_bwd")(lr, li, ldt, *cts)


def _cmul(ar, ai, br, bi):
    return ar * br - ai * bi, ar * bi + ai * br


def s5_scan(z, ab_re, ab_im, f_re, f_im, *, reverse, name):
    L = z.shape[0]
    tm = _pick(L, 256, SUBLANE)
    nb = L // tm
    ng = tm // SUBLANE
    use_f = f_re is not None
    consts = [ab_re, ab_im] + ([f_re, f_im] if use_f else [])

    def body(*refs):
        z_ref = refs[0]
        c_refs = refs[1:1 + len(consts)]
        x_ref, car = refs[1 + len(consts)], refs[2 + len(consts)]
        i = pl.program_id(0)

        @pl.when(i == 0)
        def _():
            car[...] = jnp.zeros_like(car)

        a1 = (c_refs[0][...], c_refs[1][...])
        a2 = _cmul(*a1, *a1)
        a3 = _cmul(*a2, *a1)
        a4 = _cmul(*a2, *a2)
        pw = [a1, a2, a3, a4, _cmul(*a4, *a1), _cmul(*a4, *a2), _cmul(*a4, *a3), _cmul(*a4, *a4)]
        if reverse:
            pw = pw[::-1]
        pw_re = _stack_rows([p[0] for p in pw])
        pw_im = _stack_rows([p[1] for p in pw])
        ridx = lax.broadcasted_iota(jnp.int32, (SUBLANE, NS), 0)
        if use_f:
            fr, fi = c_refs[2][...], c_refs[3][...]

        def group(s, carry):
            cr, ci = carry
            g = (ng - 1 - s) if reverse else s
            r0 = pl.multiple_of(g * SUBLANE, SUBLANE)
            xr = z_ref[pl.ds(r0, SUBLANE), 0:NS]
            xi = z_ref[pl.ds(r0, SUBLANE), NS:2 * NS]
            if use_f:
                xr, xi = _cmul(fr, fi, xr, xi)
            for sft, (pr, pi) in ((1, a1), (2, a2), (4, a4)):
                if reverse:
                    keep = ridx < SUBLANE - sft
                    sr = jnp.where(keep, pltpu.roll(xr, SUBLANE - sft, 0), 0.0)
                    si = jnp.where(keep, pltpu.roll(xi, SUBLANE - sft, 0), 0.0)
                else:
                    keep = ridx >= sft
                    sr = jnp.where(keep, pltpu.roll(xr, sft, 0), 0.0)
                    si = jnp.where(keep, pltpu.roll(xi, sft, 0), 0.0)
                tr, ti = _cmul(pr, pi, sr, si)
                xr, xi = xr + tr, xi + ti
            tr, ti = _cmul(pw_re, pw_im, cr, ci)
            xr, xi = xr + tr, xi + ti
            x_ref[pl.ds(r0, SUBLANE), 0:NS] = xr
            x_ref[pl.ds(r0, SUBLANE), NS:2 * NS] = xi
            row = 0 if reverse else SUBLANE - 1
            return xr[row:row + 1, :], xi[row:row + 1, :]

        cr, ci = lax.fori_loop(0, ng, group, (car[0:1, 0:NS], car[0:1, NS:2 * NS]))
        car[0:1, 0:NS] = cr
        car[0:1, NS:2 * NS] = ci

    blk = (lambda i: (nb - 1 - i, 0)) if reverse else (lambda i: (i, 0))
    return pl.pallas_call(
        body, grid=(nb,),
        in_specs=[pl.BlockSpec((tm, 2 * NS), blk)] + [pl.BlockSpec((1, NS), lambda i: (0, 0))] * len(consts),
        out_specs=pl.BlockSpec((tm, 2 * NS), blk),
        out_shape=jax.ShapeDtypeStruct((L, 2 * NS), f32),
        scratch_shapes=[pltpu.VMEM((SUBLANE, 2 * NS), f32)],
        compiler_params=_cp("arbitrary"), name=name,
    )(z, *consts)


def _s5_post_fn(ypre, u, dvec, gw, gb):
    y = ypre + dvec * u
    g = jax.nn.gelu(y)
    z = _dot(g.astype(bf16), gw.astype(bf16)) + gb
    return g * jax.nn.sigmoid(z)


def s5_post_fwd(ypre, hin, dvec, gw, gb):
    L = ypre.shape[0]

    def fn(i, nb, yt, ut, dv, gwv, gbv):
        return _s5_post_fn(yt, ut, dv, gwv, gbv)

    return rowwise(fn, name="s5_post_fwd", L=L, tm=_pick(L, 512, SUBLANE),
                   rows=[(ypre, 0, S5_WIDTH, "cur"), (hin, 3072, S5_WIDTH, "cur")],
                   consts=[dvec, gw, gb], outs=[(S5_WIDTH, f32)])


def s5_post_bwd(ypre, hin, dvec, gw, gb, dycat):
    L = ypre.shape[0]

    def fn(i, nb, yt, ut, dyt, dv, gwv, gbv):
        _, vjp = jax.vjp(_s5_post_fn, yt, ut, dv, gwv, gbv)
        return vjp(dyt)

    return rowwise(fn, name="s5_post_bwd", L=L, tm=_pick(L, 512, SUBLANE),
                   rows=[(ypre, 0, S5_WIDTH, "cur"), (hin, 3072, S5_WIDTH, "cur"), (dycat, 0, S5_WIDTH, "cur")],
                   consts=[dvec, gw, gb], outs=[(S5_WIDTH, f32)] * 2,
                   sums=[(1, S5_WIDTH), (S5_WIDTH, S5_WIDTH), (1, S5_WIDTH)])


def s5_acc(G, X, bu, f_re, f_im):
    L = G.shape[0]

    def fn(i, nb, g, x, b, xp8, fr, fi):
        gr, gi = g[:, :NS], g[:, NS:]
        xp = _shift_down(x, xp8 * (i > 0).astype(f32), 1)
        xr, xi = xp[:, :NS], xp[:, NS:]
        br, bi = b[:, :NS], b[:, NS:]
        dbu = jnp.concatenate([fr * gr + fi * gi, fr * gi - fi * gr], axis=1)
        return (dbu, _colsum(xr * gr + xi * gi), _colsum(xr * gi - xi * gr),
                _colsum(br * gr + bi * gi), _colsum(br * gi - bi * gr))

    return rowwise(fn, name="s5_acc", L=L, tm=_pick(L, 256, SUBLANE),
                   rows=[(G, 0, 2 * NS, "cur"), (X, 0, 2 * NS, "cur"), (bu, 0, 2 * NS, "cur"), (X, 0, 2 * NS, "prev")],
                   consts=[f_re, f_im], outs=[(2 * NS, bf16)], sums=[(1, NS)] * 4)


def _s5_blockdiag(b_re, b_im, c_re, c_im):
    eye = jnp.eye(S5_GROUPS, dtype=f32)
    bb = lambda b: jnp.einsum("gpi,gh->gihp", b, eye).reshape(S5_WIDTH, NS)
    cc = lambda c: jnp.einsum("gip,gh->gphi", c, eye).reshape(NS, S5_WIDTH)
    return jnp.concatenate([bb(b_re), bb(b_im)], axis=1), jnp.concatenate([cc(c_re), -cc(c_im)], axis=0)


def _s5_blockdiag_grads(dB, dC):
    gb = lambda m: jnp.einsum("gigp->gpi", m.reshape(S5_GROUPS, S5_GROUP, S5_GROUPS, S5_STATE))
    gc = lambda m: jnp.einsum("gpgi->gip", m.reshape(S5_GROUPS, S5_STATE, S5_GROUPS, S5_GROUP))
    return gb(dB[:, :NS]), gb(dB[:, NS:]), gc(dC[:NS]), -gc(dC[NS:])


def s5_fwd(hin, prm):
    ab_re, ab_im, f_re, f_im = s5_params_fwd(prm["lr"], prm["li"], prm["ldt"])
    flat = lambda a: a.reshape(1, NS)
    ab_re, ab_im, f_re, f_im = flat(ab_re), flat(ab_im), flat(f_re), flat(f_im)
    Bblk, Cblk = _s5_blockdiag(prm["b_re"], prm["b_im"], prm["c_re"], prm["c_im"])
    bu = mm(hin, Bblk, name="s5_bu", a_win=(3072, S5_WIDTH))
    X = s5_scan(bu, ab_re, ab_im, f_re, f_im, reverse=False, name="s5_scan_fwd")
    ypre = mm(X, Cblk, name="s5_y")
    yc = s5_post_fwd(ypre, hin, prm["d"], prm["gw"], prm["gb"])
    return yc, (ab_re, ab_im, f_re, f_im, Bblk, Cblk, bu, X, ypre)


def s5_bwd(hin, prm, saved, dycat):
    ab_re, ab_im, f_re, f_im, Bblk, Cblk, bu, X, ypre = saved
    dypre, du_skip, dd, dgw, dgb = s5_post_bwd(ypre, hin, prm["d"], prm["gw"], prm["gb"], dycat)
    dX = mm(dypre, Cblk, tb=True, name="s5_dx")
    dC = mm(X, dypre, ta=True, name="s5_dc")
    G = s5_scan(dX, ab_re, -ab_im, None, None, reverse=True, name="s5_scan_bwd")
    dbu, dar, dai, dfr, dfi = s5_acc(G, X, bu, f_re, f_im)
    dB = mm(hin, dbu, ta=True, a_win=(3072, S5_WIDTH), name="s5_db")
    du_b = mm(dbu, Bblk, tb=True, name="s5_du")
    sh = prm["lr"].shape
    dlr, dli, dldt = s5_params_bwd(prm["lr"], prm["li"], prm["ldt"],
                                   [a.reshape(sh) for a in (dar, dai, dfr, dfi)])
    db_re, db_im, dc_re, dc_im = _s5_blockdiag_grads(dB, dC)
    grads = dict(lr=dlr, li=dli, ldt=dldt, b_re=db_re, b_im=db_im, c_re=dc_re, c_im=dc_im, d=dd, gw=dgw, gb=dgb)
    return du_skip, du_b, grads


DN_W = DN_HEADS * DN_DK
QKV_W = 3 * DN_W


def _softplus(x):
    return jnp.maximum(x, 0.0) + jnp.log(1.0 + jnp.exp(-jnp.abs(x)))


def _dn_pre(c, ab, alog, dtb):
    s = _silu(c)
    parts = []
    for h in range(2 * DN_HEADS):
        sh = s[:, h * 128:(h + 1) * 128]
        scale = DN_DK ** -0.5 if h < DN_HEADS else 1.0
        parts.append(sh * (lax.rsqrt(jnp.sum(sh * sh, axis=-1, keepdims=True) + EPS) * scale))
    parts.append(s[:, 2 * DN_W:])
    g = -jnp.exp(alog) * _softplus(ab[:, :128] + dtb)
    beta = jax.nn.sigmoid(ab[:, 128:])
    return jnp.concatenate(parts, axis=1), jnp.concatenate([g, beta], axis=1)


def _dn_pre_bwd(c, ab, alog, dtb, dqkv, dgb):
    sg = jax.nn.sigmoid(c)
    s = c * sg
    parts = []
    for h in range(2 * DN_HEADS):
        sh = s[:, h * 128:(h + 1) * 128]
        dy = dqkv[:, h * 128:(h + 1) * 128]
        scale = DN_DK ** -0.5 if h < DN_HEADS else 1.0
        r = lax.rsqrt(jnp.sum(sh * sh, axis=-1, keepdims=True) + EPS)
        parts.append(scale * r * (dy - sh * (r * r) * jnp.sum(dy * sh, axis=-1, keepdims=True)))
    parts.append(dqkv[:, 2 * DN_W:])
    dc = jnp.concatenate(parts, axis=1) * (sg * (1.0 + c * (1.0 - sg)))
    pre = ab[:, :128] + dtb
    ea = jnp.exp(alog)
    dg = dgb[:, :128]
    da = dg * (-ea) * jax.nn.sigmoid(pre)
    dalog = _colsum(dg * (-ea) * _softplus(pre))
    beta = jax.nn.sigmoid(ab[:, 128:])
    db = dgb[:, 128:] * beta * (1.0 - beta)
    return dc, jnp.concatenate([da, db], axis=1), dalog, _colsum(da)


def dn_pre_fwd(hin, conv_w, alog, dtb):
    L = hin.shape[0]

    def fn(i, nb, x, ab, p8, w, al, db):
        c = _conv_causal(x, p8 * (i > 0).astype(f32), w)
        return _dn_pre(c, ab, al, db)

    return rowwise(fn, name="dn_pre_fwd", L=L, tm=_pick(L, 256, SUBLANE),
                   rows=[(hin, 0, QKV_W, "cur"), (hin, 3328, 256, "cur"), (hin, 0, QKV_W, "prev")],
                   consts=[conv_w, alog, dtb], outs=[(QKV_W, f32), (256, f32)])


def dn_pre_bwd(hin, conv_w, alog, dtb, dqkv3, dg, dbeta):
    L = hin.shape[0]

    def fn(i, nb, x, ab, dq, dk, dv, dgt, dbt, p8, w, al, db):
        c = _conv_causal(x, p8 * (i > 0).astype(f32), w)
        return _dn_pre_bwd(c, ab, al, db, jnp.concatenate([dq, dk, dv], axis=1), jnp.concatenate([dgt, dbt], axis=1))

    rows = [(hin, 0, QKV_W, "cur"), (hin, 3328, 256, "cur")] + [(a, 0, DN_W, "cur") for a in dqkv3]
    rows += [(dg, 0, 128, "cur"), (dbeta, 0, 128, "cur"), (hin, 0, QKV_W, "prev")]
    return rowwise(fn, name="dn_pre_bwd", L=L, tm=_pick(L, 128, SUBLANE), rows=rows,
                   consts=[conv_w, alog, dtb], outs=[(QKV_W, f32), (256, f32)], sums=[(1, 128), (1, 128)])


def _split(a):
    hi = a.astype(bf16)
    return hi, (a - hi.astype(f32)).astype(bf16)


def _dot3_raw(a, b, dims):
    ah, al = _split(a)
    bh, bl = _split(b)
    return _dot(ah, bh, dims) + (_dot(ah, bl, dims) + _dot(al, bh, dims))


@functools.partial(jax.custom_vjp, nondiff_argnums=(2,))
def _dot3(a, b, dims=NN):
    return _dot3_raw(a, b, dims)


def _dot3_fwd(a, b, dims):
    return _dot3_raw(a, b, dims), (a, b)


BNN = (((2,), (1,)), ((0,), (0,)))
BNT = (((2,), (2,)), ((0,), (0,)))
BTN = (((1,), (1,)), ((0,), (0,)))


def _dot_bwd(raw, dims, res, g):
    a, b = res
    nn, nt, tn = (BNN, BNT, BTN) if dims[1][0] else (NN, NT, TN)
    if dims == nn:
        return raw(g, b, nt), raw(a, g, tn)
    if dims == nt:
        return raw(g, b, nn), raw(g, a, tn)
    assert dims == tn
    return raw(b, g, nt), raw(a, g, nn)


_dot3.defvjp(_dot3_fwd, functools.partial(_dot_bwd, _dot3_raw))


def _dot1_raw(a, b, dims):
    return _dot(a.astype(bf16), b.astype(bf16), dims)


@functools.partial(jax.custom_vjp, nondiff_argnums=(2,))
def _dot1(a, b, dims=NN):
    return _dot1_raw(a, b, dims)


_dot1.defvjp(lambda a, b, dims: (_dot1_raw(a, b, dims), (a, b)), functools.partial(_dot_bwd, _dot1_raw))


def _unit_lower_inverse(nmat):
    C = nmat.shape[-1]
    eye = (lax.broadcasted_iota(jnp.int32, (C, C), 0) == lax.broadcasted_iota(jnp.int32, (C, C), 1)).astype(f32)
    T = eye - nmat
    Pw = _dot3(nmat, nmat, BNN)
    for step in range(5):
        T = T + _dot3(T, Pw, BNN)
        if step < 4:
            Pw = _dot3(Pw, Pw, BNN)
    return T


@jax.custom_vjp
def _inverse_known(nmat, T):
    return T


def _inverse_known_bwd(T, g):
    return -_dot3(_dot3(T, g, BTN), T, BNT), jnp.zeros_like(T)


_inverse_known.defvjp(lambda nmat, T: (T, T), _inverse_known_bwd)


def _dn_chunk(q, k, v, gcol, bcol, S, T_known=None):
    C = q.shape[1]
    r = lax.broadcasted_iota(jnp.int32, (C, C), 0)
    c = lax.broadcasted_iota(jnp.int32, (C, C), 1)
    tril = (r >= c).astype(f32)
    strict = (r > c).astype(f32)
    eye = (r == c).astype(f32)
    hd = _dot3
    grow = jnp.sum(eye * gcol, axis=1, keepdims=True)
    Gcol = jnp.sum(tril * grow, axis=2, keepdims=True)
    Grow = jnp.sum(eye * Gcol, axis=1, keepdims=True)
    gamma = jnp.exp((Gcol - Grow) * tril) * tril
    ld = _dot1
    nmat = strict * bcol * ld(k, k, BNT) * gamma
    T = _unit_lower_inverse(nmat) if T_known is None else _inverse_known(nmat, T_known)
    eG = jnp.exp(Gcol)
    u = hd(T, bcol * v, BNN)
    w = hd(T, (bcol * eG) * k, BNN)
    qk = ld(q, k, BNT) * gamma
    vnew = u - ld(w, S, BNN)
    o = ld(q * eG, S, BNN) + ld(qk, vnew, BNN)
    Glast = jnp.sum(gcol, axis=1, keepdims=True)
    S2 = S * jnp.exp(Glast) + ld(k * jnp.exp(Glast - Gcol), vnew, BTN)
    return o, S2, T


def _heads(x_ref):
    return jnp.stack([x_ref[:, h * 128:(h + 1) * 128] for h in range(DN_HEADS)])


def _head_cols(g_ref):
    return jnp.stack([g_ref[:, h:h + 1] for h in range(DN_HEADS)])


def dn_chunks_fwd(qkvn, gb):
    L = qkvn.shape[0]
    C = DN_CHUNK
    nc = L // C

    def body(q_ref, k_ref, v_ref, g_ref, b_ref, o_ref, sin_ref, t_ref, S):
        n = pl.program_id(0)

        @pl.when(n == 0)
        def _():
            S[...] = jnp.zeros_like(S)

        s_in = S[...]
        sin_ref[...] = s_in
        o, s2, t = _dn_chunk(_heads(q_ref), _heads(k_ref), _heads(v_ref), _head_cols(g_ref), _head_cols(b_ref), s_in)
        for h in range(DN_HEADS):
            o_ref[:, h * 128:(h + 1) * 128] = o[h]
        t_ref[...] = t
        S[...] = s2

    blk = lambda j: pl.BlockSpec((C, DN_W), lambda n, j=j: (n, j))
    gblk = lambda j: pl.BlockSpec((C, 128), lambda n, j=j: (n, j))
    return pl.pallas_call(
        body, grid=(nc,),
        in_specs=[blk(0), blk(1), blk(2), gblk(0), gblk(1)],
        out_specs=[pl.BlockSpec((C, DN_W), lambda n: (n, 0)),
                   pl.BlockSpec((DN_HEADS, None, 128, 128), lambda n: (0, n, 0, 0)),
                   pl.BlockSpec((DN_HEADS, None, C, C), lambda n: (0, n, 0, 0))],
        out_shape=[jax.ShapeDtypeStruct((L, DN_W), f32), jax.ShapeDtypeStruct((DN_HEADS, nc, 128, 128), f32),
                   jax.ShapeDtypeStruct((DN_HEADS, nc, C, C), f32)],
        scratch_shapes=[pltpu.VMEM((DN_HEADS, 128, 128), f32)],
        compiler_params=_cp("arbitrary"), name="dn_chunks_fwd",
    )(qkvn, qkvn, qkvn, gb, gb)


def dn_chunks_bwd(qkvn, gb, s_in, t_inv, do):
    L = qkvn.shape[0]
    C = DN_CHUNK
    nc = L // C

    def body(q_ref, k_ref, v_ref, g_ref, b_ref, sin_ref, t_ref, do_ref, dq_ref, dk_ref, dv_ref, dg_ref, db_ref, dS):
        n = pl.program_id(0)

        @pl.when(n == 0)
        def _():
            dS[...] = jnp.zeros_like(dS)

        args = (_heads(q_ref), _heads(k_ref), _heads(v_ref), _head_cols(g_ref), _head_cols(b_ref), sin_ref[...])
        t_known = t_ref[...]
        _, vjp = jax.vjp(lambda *a: _dn_chunk(*a, T_known=t_known)[:2], *args)
        dq, dk, dv, dg, db, ds = vjp((_heads(do_ref), dS[...]))
        lane = lax.broadcasted_iota(jnp.int32, (C, 128), 1)
        dg_all = jnp.zeros((C, 128), f32)
        db_all = jnp.zeros((C, 128), f32)
        for h in range(DN_HEADS):
            sl = slice(h * 128, (h + 1) * 128)
            dq_ref[:, sl] = dq[h]
            dk_ref[:, sl] = dk[h]
            dv_ref[:, sl] = dv[h]
            dg_all = dg_all + jnp.where(lane == h, dg[h], 0.0)
            db_all = db_all + jnp.where(lane == h, db[h], 0.0)
        dS[...] = ds
        dg_ref[...] = dg_all
        db_ref[...] = db_all

    rv = lambda n: nc - 1 - n
    blk = lambda j: pl.BlockSpec((C, DN_W), lambda n, j=j: (rv(n), j))
    gblk = lambda j: pl.BlockSpec((C, 128), lambda n, j=j: (rv(n), j))
    oblk = pl.BlockSpec((C, DN_W), lambda n: (rv(n), 0))
    gout = pl.BlockSpec((C, 128), lambda n: (rv(n), 0))
    return pl.pallas_call(
        body, grid=(nc,),
        in_specs=[blk(0), blk(1), blk(2), gblk(0), gblk(1),
                  pl.BlockSpec((DN_HEADS, None, 128, 128), lambda n: (0, rv(n), 0, 0)),
                  pl.BlockSpec((DN_HEADS, None, C, C), lambda n: (0, rv(n), 0, 0)), oblk],
        out_specs=[oblk] * 3 + [gout] * 2,
        out_shape=[jax.ShapeDtypeStruct((L, DN_W), f32)] * 3 + [jax.ShapeDtypeStruct((L, 128), f32)] * 2,
        scratch_shapes=[pltpu.VMEM((DN_HEADS, 128, 128), f32)],
        compiler_params=_cp("arbitrary"), name="dn_chunks_bwd",
    )(qkvn, qkvn, qkvn, gb, gb, s_in, t_inv, do)


def _dn_post(o, z, w):
    parts = []
    for h in range(DN_HEADS):
        oh = o[:, h * 128:(h + 1) * 128]
        r = lax.rsqrt(jnp.mean(oh * oh, axis=-1, keepdims=True) + EPS)
        parts.append(oh * r * w)
    return jnp.concatenate(parts, axis=1) * _silu(z)


def dn_post_fwd(o, hin, yc, onorm):
    L = o.shape[0]

    def fn(i, nb, ot, zt, yct, w):
        return jnp.concatenate([yct, _dn_post(ot, zt, w)], axis=1)

    return rowwise(fn, name="dn_post_fwd", L=L, tm=_pick(L, 256, SUBLANE),
                   rows=[(o, 0, DN_W, "cur"), (hin, 2304, DN_W, "cur"), (yc, 0, S5_WIDTH, "cur")],
                   consts=[onorm], outs=[(1024, bf16)])


def dn_post_bwd(o, hin, onorm, dycat):
    L = o.shape[0]

    def fn(i, nb, ot, zt, d0, d1, d2, w):
        dy = jnp.concatenate([d0, d1, d2], axis=1)
        sg = jax.nn.sigmoid(zt)
        sz = zt * sg
        dos, dw = [], jnp.zeros((1, 128), f32)
        nrm = []
        for h in range(DN_HEADS):
            sl = slice(h * 128, (h + 1) * 128)
            oh = ot[:, sl]
            r = lax.rsqrt(jnp.mean(oh * oh, axis=-1, keepdims=True) + EPS)
            ohat = oh * r
            t = dy[:, sl] * sz[:, sl]
            dw = dw + _colsum(t * ohat)
            t = t * w
            dos.append(r * (t - ohat * jnp.mean(t * ohat, axis=-1, keepdims=True)))
            nrm.append(ohat * w)
        dz = dy * jnp.concatenate(nrm, axis=1) * (sg * (1.0 + zt * (1.0 - sg)))
        return jnp.concatenate(dos, axis=1), dz, dw

    rows = [(o, 0, DN_W, "cur"), (hin, 2304, DN_W, "cur")] + [(dycat, 256 * (1 + j), 256, "cur") for j in range(3)]
    return rowwise(fn, name="dn_post_bwd", L=L, tm=_pick(L, 256, SUBLANE), rows=rows,
                   consts=[onorm], outs=[(DN_W, f32), (DN_W, f32)], sums=[(1, 128)])


def conv_bwd_win(xarr, start, C, w, dc, name):
    L = xarr.shape[0]
    W = w.shape[0]

    def fn(i, nb, xt, dct, p8, n8, wv):
        dx = _conv_causal_bwd_x(dct, n8 * (i < nb - 1).astype(f32), wv)
        dw = _conv_causal_bwd_w(dct, xt, p8 * (i > 0).astype(f32), W)
        return dx, dw

    return rowwise(fn, name=name, L=L, tm=_pick(L, 128, SUBLANE),
                   rows=[(xarr, start, C, "cur"), (dc, 0, C, "cur"), (xarr, start, C, "prev"), (dc, 0, C, "next")],
                   consts=[w], outs=[(C, bf16)], sums=[(SUBLANE, C)])


def rec_assemble(dx_qkv, dz, du1, du2, dab):
    L = dz.shape[0]

    def fn(i, nb, a, b, c, d, e):
        return jnp.concatenate([a.astype(f32), b, c + d, e], axis=1)

    return rowwise(fn, name="rec_assemble", L=L, tm=_pick(L, 256, SUBLANE),
                   rows=[(dx_qkv, 0, QKV_W, "cur"), (dz, 0, DN_W, "cur"), (du1, 0, 256, "cur"),
                         (du2, 0, 256, "cur"), (dab, 0, 256, "cur")], outs=[(REC_PAD, bf16)])


def deltanet_fwd(hin, prm, yc):
    qkvn, gb = dn_pre_fwd(hin, prm["conv"], prm["alog"], prm["dtb"])
    o, s_in, t_inv = dn_chunks_fwd(qkvn, gb)
    ycat = dn_post_fwd(o, hin, yc, prm["onorm"])
    return ycat, (qkvn, gb, o, s_in, t_inv)


def deltanet_bwd(hin, prm, saved, dycat):
    qkvn, gb, o, s_in, t_inv = saved
    do, dz, donorm = dn_post_bwd(o, hin, prm["onorm"], dycat)
    dq, dk, dv, dgH, dbH = dn_chunks_bwd(qkvn, gb, s_in, t_inv, do)
    dc, dab, dalog, ddtb = dn_pre_bwd(hin, prm["conv"], prm["alog"], prm["dtb"], (dq, dk, dv), dgH, dbH)
    dx_qkv, dconv = conv_bwd_win(hin, 0, QKV_W, prm["conv"], dc, "dn_conv_bwd")
    return dx_qkv, dz, dab, dict(conv=dconv[:DN_CONV], alog=dalog, dtb=ddtb, onorm=donorm)


AXES = ("x", "y", "c")


class _Coll:
    def __init__(self, x, axes, mode):
        self.axes, self.mode = axes, mode
        self.P = 2 ** len(axes)
        shape = x.shape if mode == "gather" else x.shape[1:]
        self.out_shape = jax.ShapeDtypeStruct((self.P,) + tuple(shape), x.dtype)
        self.scratch = [pltpu.SemaphoreType.DMA((self.P - 1,)), pltpu.SemaphoreType.DMA((self.P - 1,)),
                        pltpu.SemaphoreType.DMA]

    def _copies(self, x_ref, out_ref, send_sems, recv_sems, local_sem, with_recvs):
        axes, k = self.axes, len(self.axes)
        co = {a: lax.axis_index(a) for a in AXES}
        me = 0
        for a in axes:
            me = me * 2 + co[a]
        src = (lambda j: x_ref) if self.mode == "gather" else (lambda j: x_ref.at[j])
        local = pltpu.make_async_copy(src(me), out_ref.at[me], local_sem)
        sends, recvs = [], []
        for m in range(1, self.P):
            tco = dict(co)
            t = 0
            for i, a in enumerate(axes):
                if (m >> (k - 1 - i)) & 1:
                    tco[a] = 1 - co[a]
                t = t * 2 + tco[a]
            dev = tuple(tco[a] for a in AXES)
            mk = functools.partial(pltpu.make_async_remote_copy, src_ref=src(t), send_sem=send_sems.at[m - 1],
                                   recv_sem=recv_sems.at[m - 1], device_id=dev, device_id_type=MESH)
            sends.append(mk(dst_ref=out_ref.at[me]))
            if with_recvs:
                recvs.append(mk(dst_ref=out_ref.at[t]))
        return local, sends, recvs

    def start(self, *refs):
        local, sends, _ = self._copies(*refs, with_recvs=False)
        local.start()
        for cp in sends:
            cp.start()

    def wait(self, *refs):
        local, sends, recvs = self._copies(*refs, with_recvs=True)
        for cp in recvs:
            cp.wait_recv()
        for cp in sends:
            cp.wait_send()
        local.wait()


def _collective(x, axes, mode, name):
    coll = _Coll(x, axes, mode)

    def body(*refs):
        coll.start(*refs)
        coll.wait(*refs)

    return pl.pallas_call(
        body, in_specs=[pl.BlockSpec(memory_space=pl.ANY)], out_specs=pl.BlockSpec(memory_space=pl.ANY),
        out_shape=coll.out_shape, scratch_shapes=coll.scratch, name=name,
    )(x)


def all_gather(x, axes, name):
    return _collective(x, axes, "gather", name)


def exchange(x, axes, name):
    return _collective(x, axes, "exchange", name)


def sum_slots(x, name, out_dtype=f32):
    P, R, C = x.shape
    tr = _pick(R, 256, 2 * SUBLANE)

    def body(x_ref, o_ref):
        acc = x_ref[0].astype(f32)
        for j in range(1, P):
            acc = acc + x_ref[j].astype(f32)
        o_ref[...] = acc.astype(o_ref.dtype)

    return pl.pallas_call(
        body, grid=(R // tr,), in_specs=[pl.BlockSpec((P, tr, C), lambda i: (0, i, 0))],
        out_specs=pl.BlockSpec((tr, C), lambda i: (i, 0)), out_shape=jax.ShapeDtypeStruct((R, C), out_dtype),
        compiler_params=_cp("parallel"), name=name,
    )(x)


def _pack(arrs, width, row_mult, dtype):
    flat = jnp.concatenate([a.astype(dtype).reshape(-1) for a in arrs])
    unit = width * row_mult
    n = -(-flat.shape[0] // unit) * unit
    return jnp.pad(flat, (0, n - flat.shape[0])).reshape(n // width, width)


def _unpack(flat, shapes):
    flat = flat.reshape(-1)
    out, off = [], 0
    for s in shapes:
        n = int(np.prod(s))
        out.append(flat[off:off + n].reshape(s))
        off += n
    return out


def ada_fwd(c_all, ada_w):
    def body(c_ref, w_ref, o_ref):
        cond = _silu(c_ref[...])
        for l in range(ada_w.shape[0]):
            o_ref[l] = _dot(cond, w_ref[l], precision=HI)

    return pl.pallas_call(body, out_shape=jax.ShapeDtypeStruct((ada_w.shape[0], c_all.shape[0], ada_w.shape[2]), f32),
                          compiler_params=pltpu.CompilerParams(vmem_limit_bytes=VMEM_LIMIT), name="ada_fwd")(c_all, ada_w)


def ada_bwd(c_all, dmod):
    def body(c_ref, d_ref, o_ref):
        cond = _silu(c_ref[...])
        for l in range(dmod.shape[0]):
            o_ref[l] = _dot(cond, d_ref[l], TN, precision=HI)

    return pl.pallas_call(body, out_shape=jax.ShapeDtypeStruct((dmod.shape[0], c_all.shape[1], dmod.shape[2]), f32),
                          compiler_params=pltpu.CompilerParams(vmem_limit_bytes=VMEM_LIMIT), name="ada_bwd")(c_all, dmod)


def loss_fwd_bwd(y, target):
    L, D = y.shape

    def fn(i, nb, yt, tt):
        e = yt - tt
        return e * (1.0 / D), jnp.sum(jnp.sum(e * e, axis=1, keepdims=True), axis=0, keepdims=True)

    return rowwise(fn, name="loss", L=L, tm=_pick(L, 512, SUBLANE), rows=[(y, 0, D, "cur"), (target, 0, D, "cur")],
                   outs=[(D, f32)], sums=[(1, 1)])


def adamw(w, g, m, v, name):
    R, C = w.shape

    def fn(i, nb, wt, gt, mt, vt):
        m2 = ADAM_B1 * mt + (1.0 - ADAM_B1) * gt
        v2 = ADAM_B2 * vt + (1.0 - ADAM_B2) * (gt * gt)
        m_hat = m2 / (1.0 - ADAM_B1 ** ADAM_STEP)
        v_hat = v2 / (1.0 - ADAM_B2 ** ADAM_STEP)
        delta = -ADAM_LR * (m_hat / (jnp.sqrt(v_hat) + ADAM_EPS) + ADAM_WD * wt)
        return delta, m2, v2

    return rowwise(fn, name=name, L=R, tm=_pick(R, 256, SUBLANE), rows=[(a, 0, C, "cur") for a in (w, g, m, v)],
                   outs=[(C, f32)] * 3)


W_NAMES = ["ada_w", "ada_b", "norm_mix", "norm_ffn", "attn_w_in", "attn_q_norm_a", "attn_k_norm_a", "attn_q_norm_b",
           "attn_k_norm_b", "attn_sinks", "attn_w_out", "rec_w_in", "s5_lambda_re", "s5_lambda_im", "s5_log_dt",
           "s5_b_re", "s5_b_im", "s5_c_re", "s5_c_im", "s5_d", "s5_glu_w", "s5_glu_b", "dn_conv", "dn_a_log",
           "dn_dt_bias", "dn_out_norm", "rec_w_out", "ffn_w_up", "ffn_conv", "ffn_w_down"]
BIG = ["attn_w_in", "attn_w_out", "rec_w_in", "rec_w_out", "ffn_w_up", "ffn_w_down"]
SMALL_SHARDED = ["s5_d", "s5_glu_w", "s5_glu_b", "dn_conv", "ffn_conv"]
SMALL_REPL = [n for n in W_NAMES if n not in BIG and n not in SMALL_SHARDED and n != "ada_w"]
NSH = 4
GRAD_WIRE = (bf16,)


SHARD_AXIS = {"attn_w_in": 2, "attn_w_out": 1, "rec_w_in": 2, "rec_w_out": 1, "ffn_w_up": 2, "ffn_w_down": 1,
              "s5_d": 1, "s5_glu_w": 1, "s5_glu_b": 1, "dn_conv": 2, "ffn_conv": 2}


def _unshard(g, name):
    ax = SHARD_AXIS[name.rstrip("01")]
    g = jnp.moveaxis(g, 0, ax)
    s = g.shape
    return g.reshape(s[:ax] + (s[ax] * s[ax + 1],) + s[ax + 2:])


def _to_shards(full, name):
    ax = SHARD_AXIS[name.rstrip("01")]
    s = full.shape
    g = full.reshape(s[:ax] + (NSH, s[ax] // NSH) + s[ax + 1:])
    return jnp.moveaxis(g, ax, 0)


def _rec_pad_cols(w):
    z6 = jnp.zeros(w.shape[:-1] + (122,), w.dtype)
    return jnp.concatenate([w[..., 256:3328], w[..., 0:256], w[..., 3328:3334], z6, w[..., 3334:3340], z6], axis=-1)


def _rec_unpad_cols(g):
    return jnp.concatenate([g[..., 3072:3328], g[..., 0:3072], g[..., 3328:3334], g[..., 3456:3462]], axis=-1)


def _ffn_fwd(x1, nf, sc, sh, gate, w_up, conv, w_dn, tag, rides=()):
    rides = list(rides) + [None, None]
    h2 = modulate_fwd(x1, nf, sc, sh, f"{tag}_mod2_fwd")
    up = mm(h2, w_up, name=f"{tag}_ffn_up", out_dtypes=(bf16,), ride=rides[0])
    up, got0 = up if rides[0] else (up, None)
    act = ffn_act_fwd(up, conv, f"{tag}_ffn_act_fwd")
    res = mm(act, w_dn, name=f"{tag}_ffn_down", out_dtypes=(f32, f32), epi=_resid_epi, epi_mn=[x1], epi_n=[gate],
             ride=rides[1])
    return res[1], (h2, up, act, res[0]), (got0, res[2] if rides[1] else None)


def _ffn_bwd(dx, x1, nf, sc, sh, gate, w_up, conv, w_dn, saved, tag, rides=()):
    rides = list(rides) + [None, None, None]
    take = lambda res, r: res if r else (res, None)
    h2, up, act, f = saved
    df, dgate = resid_bwd(dx, f, gate, f"{tag}_res2_bwd")
    dact = mm(df, w_dn, tb=True, name=f"{tag}_ffn_dact", out_dtypes=(bf16,))
    dw_dn, got0 = take(mm(act, df, ta=True, name=f"{tag}_ffn_dwdown", out_dtypes=GRAD_WIRE, ride=rides[0]), rides[0])
    dup, dconv = ffn_act_conv_bwd(up, conv, dact, f"{tag}_ffn_act_conv_bwd")
    dw_up, got1 = take(mm(h2, dup, ta=True, name=f"{tag}_ffn_dwup", out_dtypes=GRAD_WIRE, ride=rides[1]), rides[1])
    if callable(rides[2]):
        rides[2] = rides[2](dw_dn)
    dh2, got2 = take(mm(dup, w_up, tb=True, name=f"{tag}_ffn_dh", ride=rides[2]), rides[2])
    dx, dnf, dsc, dsh = modulate_bwd(x1, nf, sc, sh, dh2, dx, f"{tag}_mod2_bwd")
    grads = dict(nf=dnf, sc=dsc, sh=dsh, gate=dgate, w_up=dw_up, conv=dconv[:FFN_CONV], w_dn=dw_dn)
    return dx, grads, (got0, got1, got2)


def kernel(x, c, ada_w, ada_b, norm_mix, norm_ffn, attn_w_in, attn_q_norm_a, attn_k_norm_a, attn_q_norm_b, attn_k_norm_b, attn_sinks, attn_w_out, rec_w_in, s5_lambda_re, s5_lambda_im, s5_log_dt, s5_b_re, s5_b_im, s5_c_re, s5_c_im, s5_d, s5_glu_w, s5_glu_b, dn_conv, dn_a_log, dn_dt_bias, dn_out_norm, rec_w_out, ffn_w_up, ffn_conv, ffn_w_down, loss_target, m_ada_w, m_ada_b, m_norm_mix, m_norm_ffn, m_attn_w_in, m_attn_q_norm_a, m_attn_k_norm_a, m_attn_q_norm_b, m_attn_k_norm_b, m_attn_sinks, m_attn_w_out, m_rec_w_in, m_s5_lambda_re, m_s5_lambda_im, m_s5_log_dt, m_s5_b_re, m_s5_b_im, m_s5_c_re, m_s5_c_im, m_s5_d, m_s5_glu_w, m_s5_glu_b, m_dn_conv, m_dn_a_log, m_dn_dt_bias, m_dn_out_norm, m_rec_w_out, m_ffn_w_up, m_ffn_conv, m_ffn_w_down, v_ada_w, v_ada_b, v_norm_mix, v_norm_ffn, v_attn_w_in, v_attn_q_norm_a, v_attn_k_norm_a, v_attn_q_norm_b, v_attn_k_norm_b, v_attn_sinks, v_attn_w_out, v_rec_w_in, v_s5_lambda_re, v_s5_lambda_im, v_s5_log_dt, v_s5_b_re, v_s5_b_im, v_s5_c_re, v_s5_c_im, v_s5_d, v_s5_glu_w, v_s5_glu_b, v_dn_conv, v_dn_a_log, v_dn_dt_bias, v_dn_out_norm, v_rec_w_out, v_ffn_w_up, v_ffn_conv, v_ffn_w_down):
    args = (ada_w, ada_b, norm_mix, norm_ffn, attn_w_in, attn_q_norm_a, attn_k_norm_a, attn_q_norm_b, attn_k_norm_b, attn_sinks, attn_w_out, rec_w_in, s5_lambda_re, s5_lambda_im, s5_log_dt, s5_b_re, s5_b_im, s5_c_re, s5_c_im, s5_d, s5_glu_w, s5_glu_b, dn_conv, dn_a_log, dn_dt_bias, dn_out_norm, rec_w_out, ffn_w_up, ffn_conv, ffn_w_down)
    ms = (m_ada_w, m_ada_b, m_norm_mix, m_norm_ffn, m_attn_w_in, m_attn_q_norm_a, m_attn_k_norm_a, m_attn_q_norm_b, m_attn_k_norm_b, m_attn_sinks, m_attn_w_out, m_rec_w_in, m_s5_lambda_re, m_s5_lambda_im, m_s5_log_dt, m_s5_b_re, m_s5_b_im, m_s5_c_re, m_s5_c_im, m_s5_d, m_s5_glu_w, m_s5_glu_b, m_dn_conv, m_dn_a_log, m_dn_dt_bias, m_dn_out_norm, m_rec_w_out, m_ffn_w_up, m_ffn_conv, m_ffn_w_down)
    vs = (v_ada_w, v_ada_b, v_norm_mix, v_norm_ffn, v_attn_w_in, v_attn_q_norm_a, v_attn_k_norm_a, v_attn_q_norm_b, v_attn_k_norm_b, v_attn_sinks, v_attn_w_out, v_rec_w_in, v_s5_lambda_re, v_s5_lambda_im, v_s5_log_dt, v_s5_b_re, v_s5_b_im, v_s5_c_re, v_s5_c_im, v_s5_d, v_s5_glu_w, v_s5_glu_b, v_dn_conv, v_dn_a_log, v_dn_dt_bias, v_dn_out_norm, v_rec_w_out, v_ffn_w_up, v_ffn_conv, v_ffn_w_down)
    W = dict(zip(W_NAMES, args))
    Mo = dict(zip(W_NAMES, ms))
    Vo = dict(zip(W_NAMES, vs))
    xi, yi, ci = lax.axis_index("x"), lax.axis_index("y"), lax.axis_index("c")
    shard = 2 * xi + yi
    me8 = 4 * xi + 2 * yi + ci
    xs = x[0]
    target = loss_target[0]
    L, D = xs.shape

    XY = ("x", "y")
    wparts = [
        [("attn_w_in", attn_w_in), ("attn_w_out", attn_w_out)],
        [("rec_w_in", rec_w_in), ("rec_w_out", rec_w_out)],
        [("ffn_w_up1", ffn_w_up[1:2]), ("ffn_w_down1", ffn_w_down[1:2])],
        [("ffn_w_up0", ffn_w_up[0:1]), ("ffn_w_down0", ffn_w_down[0:1])],
    ]
    wpack = [_pack([a for _, a in p], 1024, 16, bf16) for p in wparts]
    Wf = {}

    def unpack_weights(gathered, part):
        flat = gathered.reshape(NSH, -1)
        off = 0
        for n, a in part:
            sz = int(np.prod(a.shape))
            Wf[n] = _unshard(flat[:, off:off + sz].reshape((NSH,) + a.shape), n)[0]
            off += sz

    unpack_weights(all_gather(wpack[0], XY, "gather_w0"), wparts[0])

    sflat = _pack([c] + [W[n] for n in SMALL_SHARDED], 1024, 8, f32)
    s8 = all_gather(sflat, AXES, "gather_small")
    s8f = s8.reshape(8, -1)
    c_all = s8f[:, :D]
    Ws = {}
    off = D
    for n in SMALL_SHARDED:
        sz = int(np.prod(W[n].shape))
        Ws[n] = _unshard(s8f[0::2, off:off + sz].reshape((NSH,) + W[n].shape), n)
        off += sz

    modp = ada_fwd(c_all, ada_w)
    modg = all_gather(modp, ("x", "y"), "gather_mod")
    mod_all = jnp.moveaxis(modg, 0, 2).reshape(2, 8, -1) + ada_b[:, None, :]
    mod = lax.dynamic_slice(mod_all, (0, me8, 0), (2, 1, mod_all.shape[2]))[:, 0, :]
    mods = [[mod[l:l + 1, j * D:(j + 1) * D] for j in range(6)] for l in range(2)]

    sh1, sc1, g1, sh2, sc2, g2_ = mods[0]
    nm0, nf0 = norm_mix[0:1], norm_ffn[0:1]
    sinkb = jnp.repeat(attn_sinks[0], HEAD_DIM)[None]
    h0 = modulate_fwd(xs, nm0, sc1, sh1, "l0_mod1_fwd")
    hin0 = mm(h0, Wf["attn_w_in"], name="l0_in_proj")
    ocat, att_saved, got = attention_fwd(hin0, attn_q_norm_a, attn_k_norm_a, attn_q_norm_b, attn_k_norm_b, sinkb,
                                         rides={1: (wpack[3], XY, "gather"), 4: (wpack[2], XY, "gather")})
    unpack_weights(got[1], wparts[3])
    unpack_weights(got[4], wparts[2])
    y0, x1 = mm(ocat, Wf["attn_w_out"], name="l0_out_proj", out_dtypes=(f32, f32), epi=_resid_epi,
                epi_mn=[xs], epi_n=[g1])
    x2, ffn0_saved, got = _ffn_fwd(x1, nf0, sc2, sh2, g2_, Wf["ffn_w_up0"], Ws["ffn_conv"][0], Wf["ffn_w_down0"], "l0",
                                   rides=[(wpack[1], XY, "gather")])
    unpack_weights(got[0], wparts[1])
    rec_w_in_p = _rec_pad_cols(Wf["rec_w_in"])

    th1, tc1, t1, th2, tc2, t2 = mods[1]
    nm1, nf1 = norm_mix[1:2], norm_ffn[1:2]
    pad128 = lambda a: jnp.pad(a, ((0, 0), (0, 128 - a.shape[1])))
    s5p = dict(lr=s5_lambda_re[0], li=s5_lambda_im[0], ldt=s5_log_dt[0][:, None], b_re=s5_b_re[0], b_im=s5_b_im[0],
               c_re=s5_c_re[0], c_im=s5_c_im[0], d=Ws["s5_d"], gw=Ws["s5_glu_w"][0], gb=Ws["s5_glu_b"])
    dnp = dict(conv=Ws["dn_conv"][0], alog=pad128(dn_a_log), dtb=pad128(dn_dt_bias), onorm=dn_out_norm)
    h1 = modulate_fwd(x2, nm1, tc1, th1, "l1_mod1_fwd")
    hin1 = mm(h1, rec_w_in_p, name="l1_in_proj")
    yc, s5_saved = s5_fwd(hin1, s5p)
    ycat, dn_saved = deltanet_fwd(hin1, dnp, yc)
    y1, x3 = mm(ycat, Wf["rec_w_out"], name="l1_out_proj", out_dtypes=(f32, f32), epi=_resid_epi,
                epi_mn=[x2], epi_n=[t1])
    x4, ffn1_saved, _ = _ffn_fwd(x3, nf1, tc2, th2, t2, Wf["ffn_w_up1"], Ws["ffn_conv"][1], Wf["ffn_w_down1"], "l1")

    dx, sse = loss_fwd_bwd(x4, target)
    loss = lax.psum(0.5 * sse[0, 0] / D, AXES)

    dx, gf1, _ = _ffn_bwd(dx, x3, nf1, tc2, th2, t2, Wf["ffn_w_up1"], Ws["ffn_conv"][1], Wf["ffn_w_down1"], ffn1_saved, "l1")
    dy1, dt1 = resid_bwd(dx, y1, t1, "l1_res1_bwd")
    dycat = mm(dy1, Wf["rec_w_out"], tb=True, name="l1_dycat")
    dw_rec_out = mm(ycat, dy1, ta=True, name="l1_dwout", out_dtypes=GRAD_WIRE)
    du_skip, du_b, s5g = s5_bwd(hin1, s5p, s5_saved, dycat)
    dx_qkv, dz, dab, dng = deltanet_bwd(hin1, dnp, dn_saved, dycat)
    dhin1 = rec_assemble(dx_qkv, dz, du_skip, du_b, dab)
    dw_rec_in = _rec_unpad_cols(mm(h1, dhin1, ta=True, name="l1_dwin", out_dtypes=GRAD_WIRE))
    dh1 = mm(dhin1, rec_w_in_p, tb=True, name="l1_dh")
    dx, dnm1, dtc1, dth1 = modulate_bwd(x2, nm1, tc1, th1, dh1, dx, "l1_mod1_bwd")

    def grad_part(items):
        flat = jnp.concatenate([_to_shards(g, n).reshape(NSH, -1) for n, g in items], axis=1)
        unit = 256 * 1024
        npad = -(-flat.shape[1] // unit) * unit
        return jnp.pad(flat, ((0, 0), (0, npad - flat.shape[1]))).reshape(NSH, npad // 1024, 1024)

    w_dn1 = gf1["w_dn"][None]
    part2 = lambda dw_dn0: (grad_part([("ffn_w_down1", w_dn1), ("ffn_w_down0", dw_dn0[None])]), XY, "exchange")
    gparts = [[("rec_w_in", dw_rec_in[None]), ("rec_w_out", dw_rec_out[None])], [("ffn_w_up1", gf1["w_up"][None])]]
    dx, gf0, gq = _ffn_bwd(dx, x1, nf0, sc2, sh2, g2_, Wf["ffn_w_up0"], Ws["ffn_conv"][0], Wf["ffn_w_down0"], ffn0_saved,
                           "l0", rides=[(grad_part(gparts[0]), XY, "exchange"), (grad_part(gparts[1]), XY, "exchange"), part2])
    gparts.append([("ffn_w_down1", w_dn1), ("ffn_w_down0", gf0["w_dn"][None])])
    dy0, dg1 = resid_bwd(dx, y0, g1, "l0_res1_bwd")
    dcat = mm(dy0, Wf["attn_w_out"], tb=True, name="l0_dcat")
    dw_attn_out = mm(ocat, dy0, ta=True, name="l0_dwout", out_dtypes=GRAD_WIRE)
    gparts += [[("ffn_w_up0", gf0["w_up"][None])], [("attn_w_out", dw_attn_out[None])]]
    chip_sum = lambda qs, i0: jnp.concatenate([sum_slots(q, f"sum_chips{i0 + i}", bf16) for i, q in enumerate(qs)], axis=0)
    dhin0, dwqa, dwka, dwqb, dwkb, dsinkb, gots = attention_bwd(
        hin0, attn_q_norm_a, attn_k_norm_a, attn_q_norm_b, attn_k_norm_b, sinkb, att_saved, dcat,
        rides={"a": (grad_part(gparts[3]), XY, "exchange"), 1: (chip_sum(gq, 0), ("c",), "gather"),
               4: (grad_part(gparts[4]), XY, "exchange"), 16: lambda g: (chip_sum([g["a"]], 3), ("c",), "gather")})
    dw_attn_in = mm(h0, dhin0, ta=True, name="l0_dwin", out_dtypes=GRAD_WIRE)
    gparts.append([("attn_w_in", dw_attn_in[None])])
    dh0, gq5 = mm(dhin0, Wf["attn_w_in"], tb=True, name="l0_dh", ride=(grad_part(gparts[5]), XY, "exchange"))
    grad_x, dnm0, dsc1, dsh1 = modulate_bwd(xs, nm0, sc1, sh1, dh0, dx, "l0_mod1_bwd")

    dmod = jnp.concatenate([
        jnp.concatenate([dsh1, dsc1, dg1, gf0["sh"], gf0["sc"], gf0["gate"]], axis=1),
        jnp.concatenate([dth1, dtc1, dt1, gf1["sh"], gf1["sc"], gf1["gate"]], axis=1)], axis=0)
    gl = {
        "ada_b": dmod,
        "norm_mix": jnp.concatenate([dnm0, dnm1], axis=0),
        "norm_ffn": jnp.concatenate([gf0["nf"], gf1["nf"]], axis=0),
        "attn_q_norm_a": dwqa, "attn_k_norm_a": dwka, "attn_q_norm_b": dwqb, "attn_k_norm_b": dwkb,
        "attn_sinks": dsinkb[:, ::HEAD_DIM],
        "s5_lambda_re": s5g["lr"][None], "s5_lambda_im": s5g["li"][None], "s5_log_dt": s5g["ldt"][:, 0][None],
        "s5_b_re": s5g["b_re"][None], "s5_b_im": s5g["b_im"][None], "s5_c_re": s5g["c_re"][None],
        "s5_c_im": s5g["c_im"][None],
        "dn_a_log": dng["alog"][:, :DN_HEADS], "dn_dt_bias": dng["dtb"][:, :DN_HEADS], "dn_out_norm": dng["onorm"],
        "s5_d": s5g["d"], "s5_glu_w": s5g["gw"][None], "s5_glu_b": s5g["gb"], "dn_conv": dng["conv"][None],
        "ffn_conv": jnp.stack([gf0["conv"], gf1["conv"]]),
    }

    small_names = SMALL_REPL + SMALL_SHARDED
    gs = _pack([gl[n] for n in small_names], 128, 256, f32)
    gs8 = all_gather(gs, AXES, "gather_small_grads")
    gsum = sum_slots(gs8, "sum_small_grads")
    full_shapes = [gl[n].shape for n in small_names]
    gfull = dict(zip(small_names, _unpack(gsum, full_shapes)))
    dmod_all = gs8.reshape(8, -1)[:, :2 * 6 * D].reshape(8, 2, 6 * D)
    ncol = ada_w.shape[2]
    dmod_sh = jnp.moveaxis(lax.dynamic_slice(dmod_all, (0, 0, shard * ncol), (8, 2, ncol)), 0, 1)
    grads = {"ada_w": ada_bwd(c_all, dmod_sh)}
    for n in SMALL_REPL:
        grads[n] = gfull[n]
    for n in SMALL_SHARDED:
        sh_all = _to_shards(gfull[n], n)
        grads[n] = lax.dynamic_slice(sh_all, (shard,) + (0,) * (sh_all.ndim - 1), (1,) + sh_all.shape[1:])[0]

    gq = list(gq) + [gots["a"], gots[4], gq5]
    gc45 = all_gather(chip_sum(gq[4:], 4), ("c",), "gather_grad_c")
    gsh = jnp.concatenate([sum_slots(gots[1], "sum_pair012"), sum_slots(gots[16], "sum_pair3"),
                           sum_slots(gc45, "sum_pair45")], axis=0)
    row, got = 0, {}
    for part, q in zip(gparts, gq):
        flat = gsh[row:row + q.shape[1]].reshape(-1)
        row += q.shape[1]
        off = 0
        for n, g in part:
            sz = g.size // NSH
            got[n] = flat[off:off + sz].reshape((1,) + g.shape[1:-2] + _to_shards(g, n).shape[-2:])
            off += sz
    for n in ("attn_w_in", "attn_w_out", "rec_w_in", "rec_w_out"):
        grads[n] = got[n]
    grads["ffn_w_up"] = jnp.concatenate([got["ffn_w_up0"], got["ffn_w_up1"]], axis=0)
    grads["ffn_w_down"] = jnp.concatenate([got["ffn_w_down0"], got["ffn_w_down1"]], axis=0)

    delta, new_m, new_v = {}, {}, {}

    def as2d(a):
        return a.reshape(-1, a.shape[-1])

    for n in ["ada_w"] + BIG:
        d_, m_, v_ = adamw(as2d(W[n]), as2d(grads[n]), as2d(Mo[n]), as2d(Vo[n]), f"adamw_{n}")
        delta[n], new_m[n], new_v[n] = d_.reshape(W[n].shape), m_.reshape(W[n].shape), v_.reshape(W[n].shape)
    pk = lambda dd: _pack([dd[n] for n in small_names], 128, 256, f32)
    d_, m_, v_ = adamw(pk(W), pk(grads), pk(Mo), pk(Vo), "adamw_small")
    shp = [W[n].shape for n in small_names]
    for dst, src in ((delta, d_), (new_m, m_), (new_v, v_)):
        dst.update(zip(small_names, _unpack(src, shp)))

    return (loss, grad_x[None], *[grads[n] for n in W_NAMES], *[delta[n] for n in W_NAMES],
            *[new_m[n] for n in W_NAMES], *[new_v[n] for n in W_NAMES])
```

```python
import functools
import math

import numpy as np
import jax
import jax.numpy as jnp
from jax import lax
from jax.experimental import pallas as pl
from jax.experimental.pallas import tpu as pltpu

f32 = jnp.float32
bf16 = jnp.bfloat16
HI = lax.Precision.HIGHEST
MESH = pl.DeviceIdType.MESH

HEAD_DIM = 64
BLOCK = 128
A_Q_HEADS = 8
A_KV_HEADS = 2
A_WINDOW = 128
B_HEADS = 8
B_BRANCHES = ((128, 1), (512, 4), (2048, 16))
N_ATTN_HEADS = 16
ATTN_IN = 2304
S5_GROUP = 16
S5_GROUPS = 16
S5_WIDTH = 256
S5_STATE = 64
DN_HEADS = 6
DN_DK = 128
DN_CONV = 4
DN_CHUNK = 64
REC_IN = 3340
REC_PAD = 3584
FFN_CONV = 3
EPS = 1e-6
ADAM_LR = 0.001
ADAM_B1 = 0.9
ADAM_B2 = 0.999
ADAM_EPS = 1e-08
ADAM_WD = 0.01
ADAM_STEP = 10

LANE = 128
SUBLANE = 8
VMEM_LIMIT = 52 * 1024 * 1024
MM_FULL_K = 5632
MM_VMEM_BUDGET = 40 * 1024 * 1024


def _cp(*sem):
    return pltpu.CompilerParams(dimension_semantics=sem, vmem_limit_bytes=VMEM_LIMIT)


def _pick(dim, cap, unit=LANE):
    for t in (2048, 1024, 768, 512, 384, 256, 128, 64, 32, 16, 8):
        if t <= cap and t % unit == 0 and dim % t == 0:
            return t
    return dim


def _dot(a, b, dims=(((1,), (0,)), ((), ())), precision=None):
    return lax.dot_general(a, b, dims, precision=precision, preferred_element_type=f32)


NN = (((1,), (0,)), ((), ()))
NT = (((1,), (1,)), ((), ()))
TN = (((0,), (0,)), ((), ()))


def mm(a, b, *, name, ta=False, tb=False, a_win=None, b_win=None, out_dtypes=(f32,),
       epi=None, epi_mn=(), epi_n=(), tm_cap=1024, tn_cap=8192, tk_cap=None, ride=None):
    coll = _Coll(*ride) if ride else None
    a0, a1 = a.shape
    b0, b1 = b.shape
    aw = a_win or (0, a1)
    bw = b_win or (0, b1)
    if ta:
        K, M = a0, aw[1]
    else:
        M, K = a0, aw[1]
    if tb:
        N, K2 = b0, bw[1]
    else:
        K2, N = b0, bw[1]
    assert K == K2, (a.shape, b.shape, ta, tb, a_win, b_win)
    if tk_cap is None:
        tk_cap = K if K <= MM_FULL_K else 2048
    tk = _pick(K, tk_cap, SUBLANE if (ta and not tb) else LANE)
    nk = K // tk
    sa, sb = a.dtype.itemsize, b.dtype.itemsize
    so = sum(jnp.dtype(d).itemsize for d in out_dtypes)
    n_mn, n_n, n_out = len(epi_mn), len(epi_n), len(out_dtypes)

    def vmem(tm_, tn_):
        return 2 * (tm_ * tk * sa + tk * tn_ * sb + tm_ * tn_ * (so + 4 * n_mn)) + 2 * tm_ * tn_ * 4

    best = None
    for tm_ in (t for t in (1024, 512, 256, 128) if M % t == 0 and (not ta or aw[0] % t == 0)):
        for tn_ in (t for t in (N, N // 2, 1024, 768, 512, 384, 256, 128)
                    if t % LANE == 0 and N % t == 0 and (tb or bw[0] % t == 0)):
            if tm_ <= tm_cap and tn_ <= max(tn_cap, 0) and vmem(tm_, tn_) <= MM_VMEM_BUDGET:
                if best is None or (tm_ * tn_, tn_) > (best[0] * best[1], best[1]):
                    best = (tm_, tn_)
    assert best is not None, (name, M, N, K)
    tm, tn = best
    b_outer = tk * tn * sb > tm * tk * sa

    def ix(f):
        if b_outer:
            return lambda j, i, k: f(i, j, k)
        return f

    if ta:
        mo = aw[0] // tm
        a_spec = pl.BlockSpec((tk, tm), ix(lambda i, j, k: (k, i + mo)))
    else:
        assert aw[0] % tk == 0
        ko = aw[0] // tk
        a_spec = pl.BlockSpec((tm, tk), ix(lambda i, j, k: (i, k + ko)))
    if tb:
        assert bw[0] % tk == 0
        kob = bw[0] // tk
        b_spec = pl.BlockSpec((tn, tk), ix(lambda i, j, k: (j, k + kob)))
    else:
        no = bw[0] // tn
        b_spec = pl.BlockSpec((tk, tn), ix(lambda i, j, k: (k, j + no)))
    dims = (((0 if ta else 1,), (1 if tb else 0,)), ((), ()))

    gi, gj = M // tm, N // tn
    grid = (gj, gi, nk) if b_outer else (gi, gj, nk)

    def body(a_ref, b_ref, *rest):
        mn_refs = rest[:n_mn]
        n_refs = rest[n_mn:n_mn + n_n]
        o0 = n_mn + n_n
        out_refs = rest[o0:o0 + n_out]

        def finish(r):
            if epi is None:
                outs = (r,)
            else:
                outs = epi(r, *[m[...] for m in mn_refs], *[v[...] for v in n_refs])
            for o_ref, o in zip(out_refs, outs):
                o_ref[...] = o.astype(o_ref.dtype)

        part = _dot(a_ref[...].astype(bf16), b_ref[...].astype(bf16), dims)
        if nk == 1:
            finish(part)
        else:
            acc = rest[o0 + n_out]
            k = pl.program_id(2)

            @pl.when(k == 0)
            def _():
                acc[...] = part

            @pl.when(k > 0)
            def _():
                acc[...] += part

            @pl.when(k == nk - 1)
            def _():
                finish(acc[...])

    mn_spec = pl.BlockSpec((tm, tn), ix(lambda i, j, k: (i, j)))
    n_spec = pl.BlockSpec((1, tn), ix(lambda i, j, k: (0, j)))
    outs = _ride_call(
        body, coll, ride, grid=grid,
        in_specs=[a_spec, b_spec] + [mn_spec] * n_mn + [n_spec] * n_n, out_specs=[mn_spec] * n_out,
        out_shape=[jax.ShapeDtypeStruct((M, N), d) for d in out_dtypes],
        scratch_shapes=[pltpu.VMEM((tm, tn), f32)] if nk > 1 else [],
        semantics=("parallel", "parallel", "arbitrary"), name=name, args=[a, b, *epi_mn, *epi_n])
    return outs[0] if len(outs) == 1 else tuple(outs)


def rowwise(fn, *, name, L, tm, rows=(), consts=(), outs=(), sums=()):
    nb = L // tm
    in_specs = []
    arrs = []
    for arr, start, width, kind in rows:
        assert start % width == 0, (name, start, width)
        co = start // width
        hr = SUBLANE * (4 // arr.dtype.itemsize)
        hb = tm // hr
        if kind == "cur":
            in_specs.append(pl.BlockSpec((tm, width), lambda i, co=co: (i, co)))
        elif kind == "prev":
            in_specs.append(pl.BlockSpec((hr, width), lambda i, co=co, hb=hb: (jnp.maximum(i * hb - 1, 0), co)))
        else:
            last = L // hr - 1
            in_specs.append(pl.BlockSpec((hr, width), lambda i, co=co, hb=hb, last=last:
                                         (jnp.minimum((i + 1) * hb, last), co)))
        arrs.append(arr)
    for cst in consts:
        assert cst.ndim == 2
        in_specs.append(pl.BlockSpec(cst.shape, lambda i: (0, 0)))
        arrs.append(cst)
    n_rows, n_c, n_o, n_s = len(rows), len(consts), len(outs), len(sums)
    out_specs = [pl.BlockSpec((tm, w), lambda i: (i, 0)) for w, _ in outs]
    out_specs += [pl.BlockSpec(s, lambda i: (0, 0)) for s in sums]
    out_shape = [jax.ShapeDtypeStruct((L, w), d) for w, d in outs]
    out_shape += [jax.ShapeDtypeStruct(s, f32) for s in sums]

    def body(*refs):
        i = pl.program_id(0)
        vals = [r[...] for r in refs[:n_rows + n_c]]
        res = fn(i, nb, *vals)
        if not isinstance(res, (tuple, list)):
            res = (res,)
        o_refs = refs[n_rows + n_c:n_rows + n_c + n_o]
        s_refs = refs[n_rows + n_c + n_o:]
        for o_ref, o in zip(o_refs, res[:n_o]):
            o_ref[...] = o.astype(o_ref.dtype)
        if n_s:
            @pl.when(i == 0)
            def _():
                for s_ref in s_refs:
                    s_ref[...] = jnp.zeros_like(s_ref)

            for s_ref, s in zip(s_refs, res[n_o:]):
                s_ref[...] += s

    res = pl.pallas_call(
        body,
        grid=(nb,),
        in_specs=in_specs,
        out_specs=out_specs,
        out_shape=out_shape,
        compiler_params=_cp("arbitrary" if n_s else "parallel"),
        name=name,
    )(*arrs)
    return res[0] if len(res) == 1 else tuple(res)


def _shift_down(x, prev8, k):
    cat = jnp.concatenate([prev8, x], axis=0)
    return pltpu.roll(cat, k, 0)[prev8.shape[0]:, :]


def _colsum(x):
    return jnp.sum(x, axis=0, keepdims=True)


def _silu(x):
    return x * jax.nn.sigmoid(x)


def _modulate_fn(x, nw, sc, sh):
    r = lax.rsqrt(jnp.mean(x * x, axis=-1, keepdims=True) + EPS)
    return (x * r * nw) * (1.0 + sc) + sh


def modulate_fwd(x, nw, sc, sh, name):
    L, D = x.shape

    def fn(i, nb, xt, nwv, scv, shv):
        return _modulate_fn(xt, nwv, scv, shv)

    return rowwise(fn, name=name, L=L, tm=_pick(L, 512, SUBLANE), rows=[(x, 0, D, "cur")],
                   consts=[nw, sc, sh], outs=[(D, bf16)])


def modulate_bwd(x, nw, sc, sh, dh, dx_in, name):
    L, D = x.shape

    def fn(i, nb, xt, dht, dxt, nwv, scv, shv):
        _, vjp = jax.vjp(_modulate_fn, xt, nwv, scv, shv)
        dx, dnw, dsc, dsh = vjp(dht)
        return dxt + dx, dnw, dsc, dsh

    return rowwise(fn, name=name, L=L, tm=_pick(L, 256, SUBLANE),
                   rows=[(x, 0, D, "cur"), (dh, 0, D, "cur"), (dx_in, 0, D, "cur")],
                   consts=[nw, sc, sh], outs=[(D, f32)], sums=[(1, D)] * 3)


def resid_bwd(dx, y, g, name):
    L, D = dx.shape

    def fn(i, nb, dxt, yt, gv):
        return dxt * gv, _colsum(dxt * yt)

    return rowwise(fn, name=name, L=L, tm=_pick(L, 512, SUBLANE),
                   rows=[(dx, 0, D, "cur"), (y, 0, D, "cur")], consts=[g],
                   outs=[(D, bf16)], sums=[(1, D)])


def _resid_epi(acc, xt, gv):
    return acc, xt + gv * acc


def _stack_rows(rows, n=SUBLANE):
    c = rows[0].shape[1]
    ridx = lax.broadcasted_iota(jnp.int32, (n, c), 0)
    out = jnp.zeros((n, c), f32)
    for j, r in enumerate(rows):
        out = out + jnp.where(ridx == j, r, 0.0)
    return out


def _conv_causal(x, prev8, w):
    W = w.shape[0]
    y = x * w[W - 1:W, :]
    for j in range(W - 1):
        y = y + _shift_down(x, prev8, W - 1 - j) * w[j:j + 1, :]
    return y


def _conv_causal_bwd_x(dy_ext, tm, w):
    W = w.shape[0]
    n = dy_ext.shape[0]
    dx = dy_ext[:tm] * w[W - 1:W, :]
    for j in range(W - 1):
        dx = dx + pltpu.roll(dy_ext, n - (W - 1 - j), 0)[:tm] * w[j:j + 1, :]
    return dx


def _conv_causal_bwd_w(dy, x, prev8, W):
    rows = [_colsum(dy * _shift_down(x, prev8, W - 1 - j)) for j in range(W - 1)]
    rows.append(_colsum(dy * x))
    return _stack_rows(rows)


def ffn_act_fwd(up, conv_w, name):
    L, F2 = up.shape
    F = F2 // 2

    def fn(i, nb, u, p8, w):
        c = _conv_causal(u.astype(f32), p8.astype(f32) * (i > 0).astype(f32), w)
        return _silu(c[:, :F]) * c[:, F:]

    return rowwise(fn, name=name, L=L, tm=_pick(L, 128, SUBLANE),
                   rows=[(up, 0, F2, "cur"), (up, 0, F2, "prev")], consts=[conv_w], outs=[(F, bf16)])


def ffn_act_conv_bwd(up, conv_w, dact, name):
    L, F2 = up.shape
    F = F2 // 2
    W = conv_w.shape[0]

    def fn(i, nb, u, da, p8, un8, dan8, w):
        tm = u.shape[0]
        more = (i < nb - 1).astype(f32)
        u, da = u.astype(f32), da.astype(f32)
        p8 = p8.astype(f32) * (i > 0).astype(f32)
        c = _conv_causal(jnp.concatenate([u, un8.astype(f32) * more], axis=0), p8, w)
        dae = jnp.concatenate([da, dan8.astype(f32) * more], axis=0)
        a, b = c[:, :F], c[:, F:]
        sg = jax.nn.sigmoid(a)
        dc = jnp.concatenate([dae * b * (sg * (1.0 + a * (1.0 - sg))), dae * a * sg], axis=1)
        return _conv_causal_bwd_x(dc, tm, w), _conv_causal_bwd_w(dc[:tm], u, p8, W)

    return rowwise(fn, name=name, L=L, tm=_pick(L, 128, SUBLANE),
                   rows=[(up, 0, F2, "cur"), (dact, 0, F, "cur"), (up, 0, F2, "prev"), (up, 0, F2, "next"),
                         (dact, 0, F, "next")],
                   consts=[conv_w], outs=[(F2, bf16)], sums=[(SUBLANE, F2)])


ALIBI = [2.0 ** (-8.0 * (i + 1) / N_ATTN_HEADS) for i in range(N_ATTN_HEADS)]
NEG = -1e30


def _band_mask(n, d, max_dist):
    qi = lax.broadcasted_iota(jnp.int32, (BLOCK, 2 * BLOCK), 0)
    kj = lax.broadcasted_iota(jnp.int32, (BLOCK, 2 * BLOCK), 1)
    dist = BLOCK + qi - kj
    valid = (dist >= 0) & (dist <= max_dist) & ((n > 0) | (kj >= BLOCK))
    return valid, -(d * dist).astype(f32)


def _rms64(x, w):
    r = lax.rsqrt(jnp.mean(x * x, axis=-1, keepdims=True) + EPS)
    xh = x * r
    return xh * w, xh, r


def _rms64_bwd(dy, xh, r, w):
    t = dy * w
    dw = jnp.sum(jnp.sum(dy * xh, axis=0), axis=0, keepdims=True)
    return r * (t - xh * jnp.mean(t * xh, axis=-1, keepdims=True)), dw


class _Plan:
    def __init__(self, dilation, group_a, nq):
        self.d, self.nq = dilation, nq
        if group_a:
            self.P, self.nkv = 1, 1
            self.q0, self.k0, self.v0 = 0, 4, 5
            self.kv_of = lambda j: j // 4
            self.max_dist = A_WINDOW - 1
            slopes = ALIBI[:8]
        else:
            self.P, self.nkv = 4 // nq, nq
            self.q0, self.k0, self.v0 = 6, 10, 14
            self.kv_of = lambda j: j
            self.max_dist = BLOCK
            slopes = ALIBI[8:]
        self.hps = 2 * nq
        sl = np.repeat(np.asarray(slopes, np.float32), HEAD_DIM).reshape(self.P, 1, self.hps * HEAD_DIM)
        self.slopes = jnp.asarray(sl, f32)


def _rows(r, d):
    return pl.ds(r, BLOCK, stride=d) if d > 1 else pl.ds(0, BLOCK)


def _pairs(refs, rows):
    parts = []
    for ref in refs:
        blk = ref[rows, :]
        parts += [blk[:, :HEAD_DIM], blk[:, HEAD_DIM:]]
    return jnp.stack(parts)


def _pairs2(prev_refs, cur_refs, rows):
    parts = []
    for pr, cr in zip(prev_refs, cur_refs):
        blk = jnp.concatenate([pr[rows, :], cr[rows, :]], axis=0)
        parts += [blk[:, :HEAD_DIM], blk[:, HEAD_DIM:]]
    return jnp.stack(parts)


def _lane_pair(t, i):
    return jnp.concatenate([t[2 * i], t[2 * i + 1]], axis=1)


def _riding(body, coll, n_in, n_out, grid):
    if coll is None:
        return body

    def wrapped(*refs):
        ride_refs = (refs[n_in], refs[n_in + 1 + n_out]) + tuple(refs[-3:])
        inner = refs[:n_in] + refs[n_in + 1:n_in + 1 + n_out] + refs[n_in + 2 + n_out:-3]
        pid = [pl.program_id(t) for t in range(len(grid))]

        @pl.when(functools.reduce(jnp.logical_and, [p == 0 for p in pid]))
        def _():
            coll.start(*ride_refs)

        body(*inner)

        @pl.when(functools.reduce(jnp.logical_and, [p == g - 1 for p, g in zip(pid, grid)]))
        def _():
            coll.wait(*ride_refs)

    return wrapped


def _ride_call(body, coll, ride, *, grid, in_specs, out_specs, out_shape, scratch_shapes, semantics, name, args):
    hbm = [pl.BlockSpec(memory_space=pl.ANY)] if coll else []
    return pl.pallas_call(
        _riding(body, coll, len(in_specs), len(out_specs), grid), grid=grid,
        in_specs=list(in_specs) + hbm, out_specs=list(out_specs) + hbm,
        out_shape=list(out_shape) + ([coll.out_shape] if coll else []),
        scratch_shapes=list(scratch_shapes) + (coll.scratch if coll else []),
        compiler_params=_cp(*(["arbitrary"] * len(grid) if coll else semantics)), name=name,
    )(*args, *([ride[0]] if coll else []))


def attn2_fwd(hin, plan, wq, wk, name, ride=None):
    coll = _Coll(*ride) if ride else None
    L = hin.shape[0]
    d, nq, nkv, hps, P = plan.d, plan.nq, plan.nkv, plan.hps, plan.P
    R = BLOCK * d
    nb = L // R
    kv_of, max_dist = plan.kv_of, plan.max_dist
    gqa = 2 * nkv != hps

    def body(*refs):
        q_refs = refs[:nq]
        kp, kc = refs[nq:nq + nkv], refs[nq + nkv:nq + 2 * nkv]
        vp, vc = refs[nq + 2 * nkv:nq + 3 * nkv], refs[nq + 3 * nkv:nq + 4 * nkv]
        sl_ref, wq_ref, wk_ref, o_ref, lse_ref = refs[nq + 4 * nkv:nq + 4 * nkv + 5]
        o_refs = refs[nq + 4 * nkv + 5:2 * nq + 4 * nkv + 5]
        lse_refs = refs[2 * nq + 4 * nkv + 5:]
        n = pl.program_id(1)
        valid, negd = _band_mask(n, d, max_dist)
        slope = jnp.stack([sl_ref[0, :, j * 64:j * 64 + 1] for j in range(hps)])
        wqv, wkv = wq_ref[...], wk_ref[...]

        def residue(r, carry):
            rows = _rows(r, d)
            kn = _rms64(_pairs2(kp, kc, rows), wkv)[0].astype(bf16)
            v = _pairs2(vp, vc, rows).astype(bf16)
            if gqa:
                kn = jnp.stack([kn[kv_of(j)] for j in range(hps)])
                v = jnp.stack([v[kv_of(j)] for j in range(hps)])
            qn = _rms64(_pairs(q_refs, rows), wqv)[0].astype(bf16)
            s = _dot(qn, kn, BNT) * (HEAD_DIM ** -0.5) + slope * negd
            s = jnp.where(valid, s, NEG)
            m = jnp.max(s, axis=-1, keepdims=True)
            p = jnp.exp(s - m)
            l = jnp.sum(p, axis=-1, keepdims=True)
            o = _dot(p.astype(bf16), v, BNN) / l
            lse = jnp.broadcast_to(m + jnp.log(l), (hps, BLOCK, HEAD_DIM))
            for i in range(nq):
                o_refs[i][rows, :] = _lane_pair(o, i)
                lse_refs[i][rows, :] = _lane_pair(lse, i)
            return carry

        lax.fori_loop(0, d, residue, 0)
        for i in range(nq):
            o_ref[:, i * 128:(i + 1) * 128] = o_refs[i][...]
            lse_ref[:, i * 128:(i + 1) * 128] = lse_refs[i][...]

    col = lambda c0, i: (lambda p, n: (n, c0 + p * nq + i))
    prv = lambda c0, i: (lambda p, n: (jnp.maximum(n - 1, 0), c0 + p * nq + i))
    blk = lambda f: pl.BlockSpec((R, 128), f)
    in_specs = [blk(col(plan.q0, i)) for i in range(nq)]
    in_specs += [blk(prv(plan.k0, i)) for i in range(nkv)] + [blk(col(plan.k0, i)) for i in range(nkv)]
    in_specs += [blk(prv(plan.v0, i)) for i in range(nkv)] + [blk(col(plan.v0, i)) for i in range(nkv)]
    in_specs += [pl.BlockSpec((1, 1, hps * 64), lambda p, n: (p, 0, 0)),
                 pl.BlockSpec((1, 64), lambda p, n: (0, 0)), pl.BlockSpec((1, 64), lambda p, n: (0, 0))]
    wide = pl.BlockSpec((R, 128 * nq), lambda p, n: (n, p))
    return _ride_call(
        body, coll, ride, grid=(P, nb), in_specs=in_specs, out_specs=[wide, wide],
        out_shape=[jax.ShapeDtypeStruct((L, 512), f32)] * 2,
        scratch_shapes=[pltpu.VMEM((R, 128), f32)] * (2 * nq),
        semantics=("parallel", "parallel"), name=name,
        args=[hin] * (nq + 4 * nkv) + [plan.slopes, wq, wk])


def attn2_bwd(hin, plan, wq, wk, o, lse, do, dlse, dw0, name, ride=None):
    coll = _Coll(*ride) if ride else None
    L = hin.shape[0]
    d, nq, nkv, hps, P = plan.d, plan.nq, plan.nkv, plan.hps, plan.P
    R = BLOCK * d
    nb = L // R
    kv_of, max_dist = plan.kv_of, plan.max_dist
    nkh = 2 * nkv
    gqa = nkh != hps
    n_in = nq + 4 * nkv + 3 + 4 * nq + 2

    def body(*refs):
        q_refs = refs[:nq]
        kp, kc = refs[nq:nq + nkv], refs[nq + nkv:nq + 2 * nkv]
        vp, vc = refs[nq + 2 * nkv:nq + 3 * nkv], refs[nq + 3 * nkv:nq + 4 * nkv]
        b = nq + 4 * nkv
        sl_ref, wq_ref, wk_ref = refs[b:b + 3]
        b += 3
        o_refs, lse_refs = refs[b:b + nq], refs[b + nq:b + 2 * nq]
        do_refs, dlse_refs = refs[b + 2 * nq:b + 3 * nq], refs[b + 3 * nq:b + 4 * nq]
        dwq0_ref, dwk0_ref = refs[b + 4 * nq:b + 4 * nq + 2]
        dq_ref, dk_ref, dv_ref, dwq_ref, dwk_ref = refs[n_in:n_in + 5]
        sc = refs[n_in + 5:]
        dq_s, dk_s, dv_s = sc[:nq], sc[nq:nq + nkv], sc[nq + nkv:nq + 2 * nkv]
        ck, cv = sc[nq + 2 * nkv:nq + 3 * nkv], sc[nq + 3 * nkv:]
        pp = pl.program_id(0)
        n = pl.program_id(1)

        @pl.when((pp == 0) & (n == 0))
        def _():
            dwq_ref[...] = dwq0_ref[...]
            dwk_ref[...] = dwk0_ref[...]

        @pl.when(n == 0)
        def _():
            for c in (*ck, *cv):
                c[...] = jnp.zeros_like(c)

        @pl.when(n < nb)
        def _():
            valid, negd = _band_mask(n, d, max_dist)
            slope = jnp.stack([sl_ref[0, :, j * 64:j * 64 + 1] for j in range(hps)])
            wqv, wkv = wq_ref[...], wk_ref[...]
            hs = range(hps)

            def residue(r, carry):
                rows = _rows(r, d)
                kn_f, kh, rk = _rms64(_pairs2(kp, kc, rows), wkv)
                kn = kn_f.astype(bf16)
                v = _pairs2(vp, vc, rows).astype(bf16)
                if gqa:
                    kn = jnp.stack([kn[kv_of(j)] for j in hs])
                    v = jnp.stack([v[kv_of(j)] for j in hs])
                qn_f, qh, rq = _rms64(_pairs(q_refs, rows), wqv)
                qn = qn_f.astype(bf16)
                s = _dot(qn, kn, BNT) * (HEAD_DIM ** -0.5) + slope * negd
                p = jnp.where(valid, jnp.exp(s - _pairs(lse_refs, rows)[:, :, :1]), 0.0)
                do_h = _pairs(do_refs, rows)
                delta = jnp.sum(do_h * _pairs(o_refs, rows), axis=-1, keepdims=True)
                do_b = do_h.astype(bf16)
                dp = _dot(do_b, v, BNT)
                ds = (p * (dp - delta + _pairs(dlse_refs, rows)[:, :, :1])).astype(bf16)
                dqn = _dot(ds, kn, BNN) * (HEAD_DIM ** -0.5)
                dkn = _dot(ds, qn, BTN) * (HEAD_DIM ** -0.5)
                dvv = _dot(p.astype(bf16), do_b, BTN)
                if gqa:
                    grp = lambda t: jnp.stack([sum(t[j] for j in hs if kv_of(j) == h) for h in range(nkh)])
                    dkn, dvv = grp(dkn), grp(dvv)
                dq, dwq = _rms64_bwd(dqn, qh, rq, wqv)
                dk, dwk = _rms64_bwd(dkn, kh, rk, wkv)
                for i in range(nq):
                    dq_s[i][rows, :] = _lane_pair(dq, i)
                for i in range(nkv):
                    dk_s[i][rows, :] = ck[i][rows, :] + _lane_pair(dk[:, :BLOCK], i)
                    dv_s[i][rows, :] = cv[i][rows, :] + _lane_pair(dvv[:, :BLOCK], i)
                    ck[i][rows, :] = _lane_pair(dk[:, BLOCK:], i)
                    cv[i][rows, :] = _lane_pair(dvv[:, BLOCK:], i)
                return carry[0] + dwq, carry[1] + dwk

            zero = jnp.zeros((1, HEAD_DIM), f32)
            dwq_a, dwk_a = lax.fori_loop(0, d, residue, (zero, zero))
            dwq_ref[...] += dwq_a
            dwk_ref[...] += dwk_a
            for i in range(nq):
                dq_ref[:, i * 128:(i + 1) * 128] = dq_s[i][...]
            for i in range(nkv):
                dk_ref[:, i * 128:(i + 1) * 128] = dk_s[i][...]
                dv_ref[:, i * 128:(i + 1) * 128] = dv_s[i][...]

        @pl.when(n == nb)
        def _():
            for i in range(nkv):
                dk_ref[:, i * 128:(i + 1) * 128] = ck[i][...]
                dv_ref[:, i * 128:(i + 1) * 128] = cv[i][...]

    cl = lambda n: jnp.minimum(n, nb - 1)
    pv = lambda n: jnp.maximum(jnp.minimum(n, nb - 1) - 1, 0)
    col = lambda c0, i: (lambda p, n: (cl(n), c0 + p * nq + i))
    prv = lambda c0, i: (lambda p, n: (pv(n), c0 + p * nq + i))
    blk = lambda f: pl.BlockSpec((R, 128), f)
    w64 = pl.BlockSpec((1, 64), lambda p, n: (0, 0))
    in_specs = [blk(col(plan.q0, i)) for i in range(nq)]
    in_specs += [blk(prv(plan.k0, i)) for i in range(nkv)] + [blk(col(plan.k0, i)) for i in range(nkv)]
    in_specs += [blk(prv(plan.v0, i)) for i in range(nkv)] + [blk(col(plan.v0, i)) for i in range(nkv)]
    in_specs += [pl.BlockSpec((1, 1, hps * 64), lambda p, n: (p, 0, 0)), w64, w64]
    in_specs += [blk(col(0, i)) for i in range(nq)] * 4 + [w64, w64]
    kvw = 128 * nkv
    out_specs = [pl.BlockSpec((R, 128 * nq), lambda p, n: (cl(n), p)),
                 pl.BlockSpec((R, kvw), lambda p, n: (jnp.maximum(n - 1, 0), p)),
                 pl.BlockSpec((R, kvw), lambda p, n: (jnp.maximum(n - 1, 0), p)), w64, w64]
    same = lambda a: [a] * nq
    return _ride_call(
        body, coll, ride, grid=(P, nb + 1), in_specs=in_specs, out_specs=out_specs,
        out_shape=[jax.ShapeDtypeStruct((L, 512), f32), jax.ShapeDtypeStruct((L, kvw * P), f32),
                   jax.ShapeDtypeStruct((L, kvw * P), f32), jax.ShapeDtypeStruct((1, 64), f32),
                   jax.ShapeDtypeStruct((1, 64), f32)],
        scratch_shapes=[pltpu.VMEM((R, 128), f32)] * (nq + 4 * nkv),
        semantics=("arbitrary", "arbitrary"), name=name,
        args=[hin] * (nq + 4 * nkv) + [plan.slopes, wq, wk, *same(o), *same(lse), *same(do), *same(dlse), *dw0])


def _head_sum(x):
    c = x.shape[1]
    r = lax.broadcasted_iota(jnp.int32, (c, c), 0) // HEAD_DIM
    q = lax.broadcasted_iota(jnp.int32, (c, c), 1) // HEAD_DIM
    ones = (r == q).astype(bf16)
    hi, lo = _split(x)
    return _dot(hi, ones) + _dot(lo, ones)


def attn_merge_fwd(oa, la, obs, lbs, sinkb, name):
    L = oa.shape[0]

    def fn(i, nb, oa_t, la_t, o1, o2, o3, l1, l2, l3, sk):
        ya = oa_t * jax.nn.sigmoid(la_t - sk)
        m = jnp.maximum(jnp.maximum(l1, l2), l3)
        e1, e2, e3 = jnp.exp(l1 - m), jnp.exp(l2 - m), jnp.exp(l3 - m)
        yb = (e1 * o1 + e2 * o2 + e3 * o3) / (e1 + e2 + e3)
        return jnp.concatenate([ya, yb], axis=1)

    rows = [(a, 0, 512, "cur") for a in (oa, la, *obs, *lbs)]
    return rowwise(fn, name=name, L=L, tm=_pick(L, 256, SUBLANE), rows=rows, consts=[sinkb], outs=[(1024, bf16)])


def attn_merge_bwd(dcat, oa, la, obs, lbs, sinkb, name):
    L = oa.shape[0]

    def fn(i, nb, da, db, oa_t, la_t, o1, o2, o3, l1, l2, l3, sk):
        keep = jax.nn.sigmoid(la_t - sk)
        dla = _head_sum(da * oa_t) * keep * (1.0 - keep)
        m = jnp.maximum(jnp.maximum(l1, l2), l3)
        e1, e2, e3 = jnp.exp(l1 - m), jnp.exp(l2 - m), jnp.exp(l3 - m)
        z = e1 + e2 + e3
        w1, w2, w3 = e1 / z, e2 / z, e3 / z
        g1, g2, g3 = _head_sum(db * o1), _head_sum(db * o2), _head_sum(db * o3)
        gm = w1 * g1 + w2 * g2 + w3 * g3
        return (da * keep, dla, w1 * db, w2 * db, w3 * db,
                w1 * (g1 - gm), w2 * (g2 - gm), w3 * (g3 - gm), -_colsum(dla))

    rows = [(dcat, 0, 512, "cur"), (dcat, 512, 512, "cur")] + [(a, 0, 512, "cur") for a in (oa, la, *obs, *lbs)]
    return rowwise(fn, name=name, L=L, tm=_pick(L, 256, SUBLANE), rows=rows, consts=[sinkb],
                   outs=[(512, f32)] * 8, sums=[(1, 512)])


def attn_assemble(dqa, dka, dva, dqs, dks, dvs, name):
    L = dqa.shape[0]

    def fn(i, nb, qa, ka, va, q1, q2, q3, k1, k2, k3, v1, v2, v3):
        return jnp.concatenate([qa, ka, va, q1 + q2 + q3, k1 + k2 + k3, v1 + v2 + v3], axis=1)

    rows = [(dqa, 0, 512, "cur"), (dka, 0, 128, "cur"), (dva, 0, 128, "cur")]
    rows += [(a, 0, 512, "cur") for a in (*dqs, *dks, *dvs)]
    return rowwise(fn, name=name, L=L, tm=_pick(L, 256, SUBLANE), rows=rows, outs=[(ATTN_IN, bf16)])


def attention_fwd(hin, wqa, wka, wqb, wkb, sinkb, rides=None):
    rides = rides or {}
    oa, la = attn2_fwd(hin, _Plan(1, True, 4), wqa, wka, "attn_a_fwd")
    obs, lbs, gots = [], [], {}
    for _, d in B_BRANCHES:
        res = attn2_fwd(hin, _Plan(d, False, 2), wqb, wkb, f"attn_b{d}_fwd", ride=rides.get(d))
        obs.append(res[0])
        lbs.append(res[1])
        if d in rides:
            gots[d] = res[2]
    ocat = attn_merge_fwd(oa, la, obs, lbs, sinkb, "attn_merge_fwd")
    return ocat, (oa, la, obs, lbs), gots


def attention_bwd(hin, wqa, wka, wqb, wkb, sinkb, saved, dcat, rides=None):
    rides = dict(rides or {})
    gots = {}

    def ride_of(key):
        r = rides.get(key)
        return r(gots) if callable(r) else r

    oa, la, obs, lbs = saved
    res = attn_merge_bwd(dcat, oa, la, obs, lbs, sinkb, "attn_merge_bwd")
    doa, dla, dos, dls, dsink = res[0], res[1], res[2:5], res[5:8], res[8]
    zero = jnp.zeros((1, 64), f32)
    res = attn2_bwd(hin, _Plan(1, True, 4), wqa, wka, oa, la, doa, dla, (zero, zero), "attn_a_bwd", ride=ride_of("a"))
    dqa, dka, dva, dwqa, dwka = res[:5]
    if "a" in rides:
        gots["a"] = res[5]
    dqs, dks, dvs = [], [], []
    dwqb = dwkb = zero
    for g, (_, d) in enumerate(B_BRANCHES):
        res = attn2_bwd(hin, _Plan(d, False, 2 if d < 16 else 1), wqb, wkb, obs[g], lbs[g],
                        dos[g], dls[g], (dwqb, dwkb), f"attn_b{d}_bwd", ride=ride_of(d))
        dq, dk, dv, dwqb, dwkb = res[:5]
        if d in rides:
            gots[d] = res[5]
        dqs.append(dq)
        dks.append(dk)
        dvs.append(dv)
    dhin = attn_assemble(dqa, dka, dva, dqs, dks, dvs, "attn_assemble")
    return dhin, dwqa, dwka, dwqb, dwkb, dsink, gots


NS = S5_GROUPS * S5_STATE


def _s5_param_fn(lr, li, ldt):
    dt = jnp.exp(ldt)
    mag, ang = jnp.exp(lr * dt), li * dt
    ab_re, ab_im = mag * jnp.cos(ang), mag * jnp.sin(ang)
    nr, ni = ab_re - 1.0, ab_im
    den = lr * lr + li * li
    return ab_re, ab_im, (nr * lr + ni * li) / den, (ni * lr - nr * li) / den


def s5_params_fwd(lr, li, ldt):
    def body(lr_ref, li_ref, ldt_ref, *outs):
        for o_ref, o in zip(outs, _s5_param_fn(lr_ref[...], li_ref[...], ldt_ref[...])):
            o_ref[...] = o

    return pl.pallas_call(body, out_shape=[jax.ShapeDtypeStruct(lr.shape, f32)] * 4, name="s5_params_fwd")(lr, li, ldt)


def s5_params_bwd(lr, li, ldt, cts):
    def body(lr_ref, li_ref, ldt_ref, c0, c1, c2, c3, dlr, dli, dldt):
        _, vjp = jax.vjp(_s5_param_fn, lr_ref[...], li_ref[...], ldt_ref[...])
        a, b, c = vjp((c0[...], c1[...], c2[...], c3[...]))
        dlr[...] = a
        dli[...] = b
        dldt[...] = c

    return pl.pallas_call(
        body, out_shape=[jax.ShapeDtypeStruct(lr.shape, f32), jax.ShapeDtypeStruct(li.shape, f32),
                         jax.ShapeDtypeStruct(ldt.shape, f32)], name="s5_params_bwd")(lr, li, ldt, *cts)


def _cmul(ar, ai, br, bi):
    return ar * br - ai * bi, ar * bi + ai * br


def s5_scan(z, ab_re, ab_im, f_re, f_im, *, reverse, name):
    L = z.shape[0]
    tm = _pick(L, 256, SUBLANE)
    nb = L // tm
    ng = tm // SUBLANE
    use_f = f_re is not None
    consts = [ab_re, ab_im] + ([f_re, f_im] if use_f else [])

    def body(*refs):
        z_ref = refs[0]
        c_refs = refs[1:1 + len(consts)]
        x_ref, car = refs[1 + len(consts)], refs[2 + len(consts)]
        i = pl.program_id(0)

        @pl.when(i == 0)
        def _():
            car[...] = jnp.zeros_like(car)

        a1 = (c_refs[0][...], c_refs[1][...])
        a2 = _cmul(*a1, *a1)
        a3 = _cmul(*a2, *a1)
        a4 = _cmul(*a2, *a2)
        pw = [a1, a2, a3, a4, _cmul(*a4, *a1), _cmul(*a4, *a2), _cmul(*a4, *a3), _cmul(*a4, *a4)]
        if reverse:
            pw = pw[::-1]
        pw_re = _stack_rows([p[0] for p in pw])
        pw_im = _stack_rows([p[1] for p in pw])
        ridx = lax.broadcasted_iota(jnp.int32, (SUBLANE, NS), 0)
        if use_f:
            fr, fi = c_refs[2][...], c_refs[3][...]

        def group(s, carry):
            cr, ci = carry
            g = (ng - 1 - s) if reverse else s
            r0 = pl.multiple_of(g * SUBLANE, SUBLANE)
            xr = z_ref[pl.ds(r0, SUBLANE), 0:NS]
            xi = z_ref[pl.ds(r0, SUBLANE), NS:2 * NS]
            if use_f:
                xr, xi = _cmul(fr, fi, xr, xi)
            for sft, (pr, pi) in ((1, a1), (2, a2), (4, a4)):
                if reverse:
                    keep = ridx < SUBLANE - sft
                    sr = jnp.where(keep, pltpu.roll(xr, SUBLANE - sft, 0), 0.0)
                    si = jnp.where(keep, pltpu.roll(xi, SUBLANE - sft, 0), 0.0)
                else:
                    keep = ridx >= sft
                    sr = jnp.where(keep, pltpu.roll(xr, sft, 0), 0.0)
                    si = jnp.where(keep, pltpu.roll(xi, sft, 0), 0.0)
                tr, ti = _cmul(pr, pi, sr, si)
                xr, xi = xr + tr, xi + ti
            tr, ti = _cmul(pw_re, pw_im, cr, ci)
            xr, xi = xr + tr, xi + ti
            x_ref[pl.ds(r0, SUBLANE), 0:NS] = xr
            x_ref[pl.ds(r0, SUBLANE), NS:2 * NS] = xi
            row = 0 if reverse else SUBLANE - 1
            return xr[row:row + 1, :], xi[row:row + 1, :]

        cr, ci = lax.fori_loop(0, ng, group, (car[0:1, 0:NS], car[0:1, NS:2 * NS]))
        car[0:1, 0:NS] = cr
        car[0:1, NS:2 * NS] = ci

    blk = (lambda i: (nb - 1 - i, 0)) if reverse else (lambda i: (i, 0))
    return pl.pallas_call(
        body, grid=(nb,),
        in_specs=[pl.BlockSpec((tm, 2 * NS), blk)] + [pl.BlockSpec((1, NS), lambda i: (0, 0))] * len(consts),
        out_specs=pl.BlockSpec((tm, 2 * NS), blk),
        out_shape=jax.ShapeDtypeStruct((L, 2 * NS), f32),
        scratch_shapes=[pltpu.VMEM((SUBLANE, 2 * NS), f32)],
        compiler_params=_cp("arbitrary"), name=name,
    )(z, *consts)


def _s5_post_fn(ypre, u, dvec, gw, gb):
    y = ypre + dvec * u
    g = jax.nn.gelu(y)
    z = _dot(g.astype(bf16), gw.astype(bf16)) + gb
    return g * jax.nn.sigmoid(z)


def s5_post_fwd(ypre, hin, dvec, gw, gb):
    L = ypre.shape[0]

    def fn(i, nb, yt, ut, dv, gwv, gbv):
        return _s5_post_fn(yt, ut, dv, gwv, gbv)

    return rowwise(fn, name="s5_post_fwd", L=L, tm=_pick(L, 512, SUBLANE),
                   rows=[(ypre, 0, S5_WIDTH, "cur"), (hin, 3072, S5_WIDTH, "cur")],
                   consts=[dvec, gw, gb], outs=[(S5_WIDTH, f32)])


def s5_post_bwd(ypre, hin, dvec, gw, gb, dycat):
    L = ypre.shape[0]

    def fn(i, nb, yt, ut, dyt, dv, gwv, gbv):
        _, vjp = jax.vjp(_s5_post_fn, yt, ut, dv, gwv, gbv)
        return vjp(dyt)

    return rowwise(fn, name="s5_post_bwd", L=L, tm=_pick(L, 512, SUBLANE),
                   rows=[(ypre, 0, S5_WIDTH, "cur"), (hin, 3072, S5_WIDTH, "cur"), (dycat, 0, S5_WIDTH, "cur")],
                   consts=[dvec, gw, gb], outs=[(S5_WIDTH, f32)] * 2,
                   sums=[(1, S5_WIDTH), (S5_WIDTH, S5_WIDTH), (1, S5_WIDTH)])


def s5_acc(G, X, bu, f_re, f_im):
    L = G.shape[0]

    def fn(i, nb, g, x, b, xp8, fr, fi):
        gr, gi = g[:, :NS], g[:, NS:]
        xp = _shift_down(x, xp8 * (i > 0).astype(f32), 1)
        xr, xi = xp[:, :NS], xp[:, NS:]
        br, bi = b[:, :NS], b[:, NS:]
        dbu = jnp.concatenate([fr * gr + fi * gi, fr * gi - fi * gr], axis=1)
        return (dbu, _colsum(xr * gr + xi * gi), _colsum(xr * gi - xi * gr),
                _colsum(br * gr + bi * gi), _colsum(br * gi - bi * gr))

    return rowwise(fn, name="s5_acc", L=L, tm=_pick(L, 256, SUBLANE),
                   rows=[(G, 0, 2 * NS, "cur"), (X, 0, 2 * NS, "cur"), (bu, 0, 2 * NS, "cur"), (X, 0, 2 * NS, "prev")],
                   consts=[f_re, f_im], outs=[(2 * NS, bf16)], sums=[(1, NS)] * 4)


def _s5_blockdiag(b_re, b_im, c_re, c_im):
    eye = jnp.eye(S5_GROUPS, dtype=f32)
    bb = lambda b: jnp.einsum("gpi,gh->gihp", b, eye).reshape(S5_WIDTH, NS)
    cc = lambda c: jnp.einsum("gip,gh->gphi", c, eye).reshape(NS, S5_WIDTH)
    return jnp.concatenate([bb(b_re), bb(b_im)], axis=1), jnp.concatenate([cc(c_re), -cc(c_im)], axis=0)


def _s5_blockdiag_grads(dB, dC):
    gb = lambda m: jnp.einsum("gigp->gpi", m.reshape(S5_GROUPS, S5_GROUP, S5_GROUPS, S5_STATE))
    gc = lambda m: jnp.einsum("gpgi->gip", m.reshape(S5_GROUPS, S5_STATE, S5_GROUPS, S5_GROUP))
    return gb(dB[:, :NS]), gb(dB[:, NS:]), gc(dC[:NS]), -gc(dC[NS:])


def s5_fwd(hin, prm):
    ab_re, ab_im, f_re, f_im = s5_params_fwd(prm["lr"], prm["li"], prm["ldt"])
    flat = lambda a: a.reshape(1, NS)
    ab_re, ab_im, f_re, f_im = flat(ab_re), flat(ab_im), flat(f_re), flat(f_im)
    Bblk, Cblk = _s5_blockdiag(prm["b_re"], prm["b_im"], prm["c_re"], prm["c_im"])
    bu = mm(hin, Bblk, name="s5_bu", a_win=(3072, S5_WIDTH))
    X = s5_scan(bu, ab_re, ab_im, f_re, f_im, reverse=False, name="s5_scan_fwd")
    ypre = mm(X, Cblk, name="s5_y")
    yc = s5_post_fwd(ypre, hin, prm["d"], prm["gw"], prm["gb"])
    return yc, (ab_re, ab_im, f_re, f_im, Bblk, Cblk, bu, X, ypre)


def s5_bwd(hin, prm, saved, dycat):
    ab_re, ab_im, f_re, f_im, Bblk, Cblk, bu, X, ypre = saved
    dypre, du_skip, dd, dgw, dgb = s5_post_bwd(ypre, hin, prm["d"], prm["gw"], prm["gb"], dycat)
    dX = mm(dypre, Cblk, tb=True, name="s5_dx")
    dC = mm(X, dypre, ta=True, name="s5_dc")
    G = s5_scan(dX, ab_re, -ab_im, None, None, reverse=True, name="s5_scan_bwd")
    dbu, dar, dai, dfr, dfi = s5_acc(G, X, bu, f_re, f_im)
    dB = mm(hin, dbu, ta=True, a_win=(3072, S5_WIDTH), name="s5_db")
    du_b = mm(dbu, Bblk, tb=True, name="s5_du")
    sh = prm["lr"].shape
    dlr, dli, dldt = s5_params_bwd(prm["lr"], prm["li"], prm["ldt"],
                                   [a.reshape(sh) for a in (dar, dai, dfr, dfi)])
    db_re, db_im, dc_re, dc_im = _s5_blockdiag_grads(dB, dC)
    grads = dict(lr=dlr, li=dli, ldt=dldt, b_re=db_re, b_im=db_im, c_re=dc_re, c_im=dc_im, d=dd, gw=dgw, gb=dgb)
    return du_skip, du_b, grads


DN_W = DN_HEADS * DN_DK
QKV_W = 3 * DN_W


def _softplus(x):
    return jnp.maximum(x, 0.0) + jnp.log(1.0 + jnp.exp(-jnp.abs(x)))


def _dn_pre(c, ab, alog, dtb):
    s = _silu(c)
    parts = []
    for h in range(2 * DN_HEADS):
        sh = s[:, h * 128:(h + 1) * 128]
        scale = DN_DK ** -0.5 if h < DN_HEADS else 1.0
        parts.append(sh * (lax.rsqrt(jnp.sum(sh * sh, axis=-1, keepdims=True) + EPS) * scale))
    parts.append(s[:, 2 * DN_W:])
    g = -jnp.exp(alog) * _softplus(ab[:, :128] + dtb)
    beta = jax.nn.sigmoid(ab[:, 128:])
    return jnp.concatenate(parts, axis=1), jnp.concatenate([g, beta], axis=1)


def _dn_pre_bwd(c, ab, alog, dtb, dqkv, dgb):
    sg = jax.nn.sigmoid(c)
    s = c * sg
    parts = []
    for h in range(2 * DN_HEADS):
        sh = s[:, h * 128:(h + 1) * 128]
        dy = dqkv[:, h * 128:(h + 1) * 128]
        scale = DN_DK ** -0.5 if h < DN_HEADS else 1.0
        r = lax.rsqrt(jnp.sum(sh * sh, axis=-1, keepdims=True) + EPS)
        parts.append(scale * r * (dy - sh * (r * r) * jnp.sum(dy * sh, axis=-1, keepdims=True)))
    parts.append(dqkv[:, 2 * DN_W:])
    dc = jnp.concatenate(parts, axis=1) * (sg * (1.0 + c * (1.0 - sg)))
    pre = ab[:, :128] + dtb
    ea = jnp.exp(alog)
    dg = dgb[:, :128]
    da = dg * (-ea) * jax.nn.sigmoid(pre)
    dalog = _colsum(dg * (-ea) * _softplus(pre))
    beta = jax.nn.sigmoid(ab[:, 128:])
    db = dgb[:, 128:] * beta * (1.0 - beta)
    return dc, jnp.concatenate([da, db], axis=1), dalog, _colsum(da)


def dn_pre_fwd(hin, conv_w, alog, dtb):
    L = hin.shape[0]

    def fn(i, nb, x, ab, p8, w, al, db):
        c = _conv_causal(x, p8 * (i > 0).astype(f32), w)
        return _dn_pre(c, ab, al, db)

    return rowwise(fn, name="dn_pre_fwd", L=L, tm=_pick(L, 256, SUBLANE),
                   rows=[(hin, 0, QKV_W, "cur"), (hin, 3328, 256, "cur"), (hin, 0, QKV_W, "prev")],
                   consts=[conv_w, alog, dtb], outs=[(QKV_W, f32), (256, f32)])


def dn_pre_bwd(hin, conv_w, alog, dtb, dqkv3, dg, dbeta):
    L = hin.shape[0]
    W = conv_w.shape[0]

    def fn(i, nb, x, ab, dq, dk, dv, dgt, dbt, p8, xn, dqn, dkn, dvn, w, al, db):
        tm, ext = x.shape[0], xn.shape[0]
        more = (i < nb - 1).astype(f32)
        p8 = p8 * (i > 0).astype(f32)
        c = _conv_causal(jnp.concatenate([x, xn * more], axis=0), p8, w)
        dqkv = jnp.concatenate([jnp.concatenate([dq, dk, dv], axis=1),
                                jnp.concatenate([dqn, dkn, dvn], axis=1) * more], axis=0)
        zpad = lambda t: jnp.concatenate([t, jnp.zeros((ext, t.shape[1]), f32)], axis=0)
        dc, dab, dalog, ddtb = _dn_pre_bwd(c, zpad(ab), al, db, dqkv, zpad(jnp.concatenate([dgt, dbt], axis=1)))
        return _conv_causal_bwd_x(dc, tm, w), dab[:tm], _conv_causal_bwd_w(dc[:tm], x, p8, W), dalog, ddtb

    rows = [(hin, 0, QKV_W, "cur"), (hin, 3328, 256, "cur")] + [(a, 0, DN_W, "cur") for a in dqkv3]
    rows += [(dg, 0, 128, "cur"), (dbeta, 0, 128, "cur"), (hin, 0, QKV_W, "prev"), (hin, 0, QKV_W, "next")]
    rows += [(a, 0, DN_W, "next") for a in dqkv3]
    return rowwise(fn, name="dn_pre_bwd", L=L, tm=_pick(L, 128, SUBLANE), rows=rows,
                   consts=[conv_w, alog, dtb], outs=[(QKV_W, bf16), (256, f32)],
                   sums=[(SUBLANE, QKV_W), (1, 128), (1, 128)])


def _split(a):
    hi = a.astype(bf16)
    return hi, (a - hi.astype(f32)).astype(bf16)


def _dot3_raw(a, b, dims):
    ah, al = _split(a)
    bh, bl = _split(b)
    return _dot(ah, bh, dims) + (_dot(ah, bl, dims) + _dot(al, bh, dims))


@functools.partial(jax.custom_vjp, nondiff_argnums=(2,))
def _dot3(a, b, dims=NN):
    return _dot3_raw(a, b, dims)


def _dot3_fwd(a, b, dims):
    return _dot3_raw(a, b, dims), (a, b)


BNN = (((2,), (1,)), ((0,), (0,)))
BNT = (((2,), (2,)), ((0,), (0,)))
BTN = (((1,), (1,)), ((0,), (0,)))


def _dot_bwd(raw, dims, res, g):
    a, b = res
    nn, nt, tn = (BNN, BNT, BTN) if dims[1][0] else (NN, NT, TN)
    if dims == nn:
        return raw(g, b, nt), raw(a, g, tn)
    if dims == nt:
        return raw(g, b, nn), raw(g, a, tn)
    assert dims == tn
    return raw(b, g, nt), raw(a, g, nn)


_dot3.defvjp(_dot3_fwd, functools.partial(_dot_bwd, _dot3_raw))


def _dot1_raw(a, b, dims):
    return _dot(a.astype(bf16), b.astype(bf16), dims)


@functools.partial(jax.custom_vjp, nondiff_argnums=(2,))
def _dot1(a, b, dims=NN):
    return _dot1_raw(a, b, dims)


_dot1.defvjp(lambda a, b, dims: (_dot1_raw(a, b, dims), (a, b)), functools.partial(_dot_bwd, _dot1_raw))


def _unit_lower_inverse(nmat):
    C = nmat.shape[-1]
    eye = (lax.broadcasted_iota(jnp.int32, (C, C), 0) == lax.broadcasted_iota(jnp.int32, (C, C), 1)).astype(f32)
    T = eye - nmat
    Pw = _dot3(nmat, nmat, BNN)
    for step in range(5):
        T = T + _dot3(T, Pw, BNN)
        if step < 4:
            Pw = _dot3(Pw, Pw, BNN)
    return T


@jax.custom_vjp
def _inverse_known(nmat, T):
    return T


def _inverse_known_bwd(T, g):
    return -_dot3(_dot3(T, g, BTN), T, BNT), jnp.zeros_like(T)


_inverse_known.defvjp(lambda nmat, T: (T, T), _inverse_known_bwd)


def _dn_chunk(q, k, v, gcol, bcol, S, T_known=None):
    C = q.shape[1]
    r = lax.broadcasted_iota(jnp.int32, (C, C), 0)
    c = lax.broadcasted_iota(jnp.int32, (C, C), 1)
    tril = (r >= c).astype(f32)
    strict = (r > c).astype(f32)
    eye = (r == c).astype(f32)
    hd = _dot3
    grow = jnp.sum(eye * gcol, axis=1, keepdims=True)
    Gcol = jnp.sum(tril * grow, axis=2, keepdims=True)
    Grow = jnp.sum(eye * Gcol, axis=1, keepdims=True)
    gamma = jnp.exp((Gcol - Grow) * tril) * tril
    ld = _dot1
    nmat = strict * bcol * ld(k, k, BNT) * gamma
    T = _unit_lower_inverse(nmat) if T_known is None else _inverse_known(nmat, T_known)
    eG = jnp.exp(Gcol)
    u = hd(T, bcol * v, BNN)
    w = hd(T, (bcol * eG) * k, BNN)
    qk = ld(q, k, BNT) * gamma
    vnew = u - ld(w, S, BNN)
    o = ld(q * eG, S, BNN) + ld(qk, vnew, BNN)
    Glast = jnp.sum(gcol, axis=1, keepdims=True)
    S2 = S * jnp.exp(Glast) + ld(k * jnp.exp(Glast - Gcol), vnew, BTN)
    return o, S2, T


def _heads(x_ref):
    return jnp.stack([x_ref[:, h * 128:(h + 1) * 128] for h in range(DN_HEADS)])


def _head_cols(g_ref):
    return jnp.stack([g_ref[:, h:h + 1] for h in range(DN_HEADS)])


def dn_chunks_fwd(qkvn, gb):
    L = qkvn.shape[0]
    C = DN_CHUNK
    nc = L // C

    def body(q_ref, k_ref, v_ref, g_ref, b_ref, o_ref, sin_ref, t_ref, S):
        n = pl.program_id(0)

        @pl.when(n == 0)
        def _():
            S[...] = jnp.zeros_like(S)

        s_in = S[...]
        sin_ref[...] = s_in
        o, s2, t = _dn_chunk(_heads(q_ref), _heads(k_ref), _heads(v_ref), _head_cols(g_ref), _head_cols(b_ref), s_in)
        for h in range(DN_HEADS):
            o_ref[:, h * 128:(h + 1) * 128] = o[h]
        t_ref[...] = t
        S[...] = s2

    blk = lambda j: pl.BlockSpec((C, DN_W), lambda n, j=j: (n, j))
    gblk = lambda j: pl.BlockSpec((C, 128), lambda n, j=j: (n, j))
    return pl.pallas_call(
        body, grid=(nc,),
        in_specs=[blk(0), blk(1), blk(2), gblk(0), gblk(1)],
        out_specs=[pl.BlockSpec((C, DN_W), lambda n: (n, 0)),
                   pl.BlockSpec((DN_HEADS, None, 128, 128), lambda n: (0, n, 0, 0)),
                   pl.BlockSpec((DN_HEADS, None, C, C), lambda n: (0, n, 0, 0))],
        out_shape=[jax.ShapeDtypeStruct((L, DN_W), f32), jax.ShapeDtypeStruct((DN_HEADS, nc, 128, 128), f32),
                   jax.ShapeDtypeStruct((DN_HEADS, nc, C, C), f32)],
        scratch_shapes=[pltpu.VMEM((DN_HEADS, 128, 128), f32)],
        compiler_params=_cp("arbitrary"), name="dn_chunks_fwd",
    )(qkvn, qkvn, qkvn, gb, gb)


def dn_chunks_bwd(qkvn, gb, s_in, t_inv, do):
    L = qkvn.shape[0]
    C = DN_CHUNK
    nc = L // C

    def body(q_ref, k_ref, v_ref, g_ref, b_ref, sin_ref, t_ref, do_ref, dq_ref, dk_ref, dv_ref, dg_ref, db_ref, dS):
        n = pl.program_id(0)

        @pl.when(n == 0)
        def _():
            dS[...] = jnp.zeros_like(dS)

        args = (_heads(q_ref), _heads(k_ref), _heads(v_ref), _head_cols(g_ref), _head_cols(b_ref), sin_ref[...])
        t_known = t_ref[...]
        _, vjp = jax.vjp(lambda *a: _dn_chunk(*a, T_known=t_known)[:2], *args)
        dq, dk, dv, dg, db, ds = vjp((_heads(do_ref), dS[...]))
        lane = lax.broadcasted_iota(jnp.int32, (C, 128), 1)
        dg_all = jnp.zeros((C, 128), f32)
        db_all = jnp.zeros((C, 128), f32)
        for h in range(DN_HEADS):
            sl = slice(h * 128, (h + 1) * 128)
            dq_ref[:, sl] = dq[h]
            dk_ref[:, sl] = dk[h]
            dv_ref[:, sl] = dv[h]
            dg_all = dg_all + jnp.where(lane == h, dg[h], 0.0)
            db_all = db_all + jnp.where(lane == h, db[h], 0.0)
        dS[...] = ds
        dg_ref[...] = dg_all
        db_ref[...] = db_all

    rv = lambda n: nc - 1 - n
    blk = lambda j: pl.BlockSpec((C, DN_W), lambda n, j=j: (rv(n), j))
    gblk = lambda j: pl.BlockSpec((C, 128), lambda n, j=j: (rv(n), j))
    oblk = pl.BlockSpec((C, DN_W), lambda n: (rv(n), 0))
    gout = pl.BlockSpec((C, 128), lambda n: (rv(n), 0))
    return pl.pallas_call(
        body, grid=(nc,),
        in_specs=[blk(0), blk(1), blk(2), gblk(0), gblk(1),
                  pl.BlockSpec((DN_HEADS, None, 128, 128), lambda n: (0, rv(n), 0, 0)),
                  pl.BlockSpec((DN_HEADS, None, C, C), lambda n: (0, rv(n), 0, 0)), oblk],
        out_specs=[oblk] * 3 + [gout] * 2,
        out_shape=[jax.ShapeDtypeStruct((L, DN_W), f32)] * 3 + [jax.ShapeDtypeStruct((L, 128), f32)] * 2,
        scratch_shapes=[pltpu.VMEM((DN_HEADS, 128, 128), f32)],
        compiler_params=_cp("arbitrary"), name="dn_chunks_bwd",
    )(qkvn, qkvn, qkvn, gb, gb, s_in, t_inv, do)


def _dn_post(o, z, w):
    parts = []
    for h in range(DN_HEADS):
        oh = o[:, h * 128:(h + 1) * 128]
        r = lax.rsqrt(jnp.mean(oh * oh, axis=-1, keepdims=True) + EPS)
        parts.append(oh * r * w)
    return jnp.concatenate(parts, axis=1) * _silu(z)


def dn_post_fwd(o, hin, yc, onorm):
    L = o.shape[0]

    def fn(i, nb, ot, zt, yct, w):
        return jnp.concatenate([yct, _dn_post(ot, zt, w)], axis=1)

    return rowwise(fn, name="dn_post_fwd", L=L, tm=_pick(L, 256, SUBLANE),
                   rows=[(o, 0, DN_W, "cur"), (hin, 2304, DN_W, "cur"), (yc, 0, S5_WIDTH, "cur")],
                   consts=[onorm], outs=[(1024, bf16)])


def dn_post_bwd(o, hin, onorm, dycat):
    L = o.shape[0]

    def fn(i, nb, ot, zt, d0, d1, d2, w):
        dy = jnp.concatenate([d0, d1, d2], axis=1)
        sg = jax.nn.sigmoid(zt)
        sz = zt * sg
        dos, dw = [], jnp.zeros((1, 128), f32)
        nrm = []
        for h in range(DN_HEADS):
            sl = slice(h * 128, (h + 1) * 128)
            oh = ot[:, sl]
            r = lax.rsqrt(jnp.mean(oh * oh, axis=-1, keepdims=True) + EPS)
            ohat = oh * r
            t = dy[:, sl] * sz[:, sl]
            dw = dw + _colsum(t * ohat)
            t = t * w
            dos.append(r * (t - ohat * jnp.mean(t * ohat, axis=-1, keepdims=True)))
            nrm.append(ohat * w)
        dz = dy * jnp.concatenate(nrm, axis=1) * (sg * (1.0 + zt * (1.0 - sg)))
        return jnp.concatenate(dos, axis=1), dz, dw

    rows = [(o, 0, DN_W, "cur"), (hin, 2304, DN_W, "cur")] + [(dycat, 256 * (1 + j), 256, "cur") for j in range(3)]
    return rowwise(fn, name="dn_post_bwd", L=L, tm=_pick(L, 256, SUBLANE), rows=rows,
                   consts=[onorm], outs=[(DN_W, f32), (DN_W, f32)], sums=[(1, 128)])


def rec_assemble(dx_qkv, dz, du1, du2, dab):
    L = dz.shape[0]

    def fn(i, nb, a, b, c, d, e):
        return jnp.concatenate([a.astype(f32), b, c + d, e], axis=1)

    return rowwise(fn, name="rec_assemble", L=L, tm=_pick(L, 256, SUBLANE),
                   rows=[(dx_qkv, 0, QKV_W, "cur"), (dz, 0, DN_W, "cur"), (du1, 0, 256, "cur"),
                         (du2, 0, 256, "cur"), (dab, 0, 256, "cur")], outs=[(REC_PAD, bf16)])


def deltanet_fwd(hin, prm, yc):
    qkvn, gb = dn_pre_fwd(hin, prm["conv"], prm["alog"], prm["dtb"])
    o, s_in, t_inv = dn_chunks_fwd(qkvn, gb)
    ycat = dn_post_fwd(o, hin, yc, prm["onorm"])
    return ycat, (qkvn, gb, o, s_in, t_inv)


def deltanet_bwd(hin, prm, saved, dycat):
    qkvn, gb, o, s_in, t_inv = saved
    do, dz, donorm = dn_post_bwd(o, hin, prm["onorm"], dycat)
    dq, dk, dv, dgH, dbH = dn_chunks_bwd(qkvn, gb, s_in, t_inv, do)
    dx_qkv, dab, dconv, dalog, ddtb = dn_pre_bwd(hin, prm["conv"], prm["alog"], prm["dtb"], (dq, dk, dv), dgH, dbH)
    return dx_qkv, dz, dab, dict(conv=dconv[:DN_CONV], alog=dalog, dtb=ddtb, onorm=donorm)


AXES = ("x", "y", "c")


class _Coll:
    def __init__(self, x, axes, mode):
        self.axes, self.mode = axes, mode
        self.P = 2 ** len(axes)
        shape = x.shape if mode == "gather" else x.shape[1:]
        self.out_shape = jax.ShapeDtypeStruct((self.P,) + tuple(shape), x.dtype)
        self.scratch = [pltpu.SemaphoreType.DMA((self.P - 1,)), pltpu.SemaphoreType.DMA((self.P - 1,)),
                        pltpu.SemaphoreType.DMA]

    def _copies(self, x_ref, out_ref, send_sems, recv_sems, local_sem, with_recvs):
        axes, k = self.axes, len(self.axes)
        co = {a: lax.axis_index(a) for a in AXES}
        me = 0
        for a in axes:
            me = me * 2 + co[a]
        src = (lambda j: x_ref) if self.mode == "gather" else (lambda j: x_ref.at[j])
        local = pltpu.make_async_copy(src(me), out_ref.at[me], local_sem)
        sends, recvs = [], []
        for m in range(1, self.P):
            tco = dict(co)
            t = 0
            for i, a in enumerate(axes):
                if (m >> (k - 1 - i)) & 1:
                    tco[a] = 1 - co[a]
                t = t * 2 + tco[a]
            dev = tuple(tco[a] for a in AXES)
            mk = functools.partial(pltpu.make_async_remote_copy, src_ref=src(t), send_sem=send_sems.at[m - 1],
                                   recv_sem=recv_sems.at[m - 1], device_id=dev, device_id_type=MESH)
            sends.append(mk(dst_ref=out_ref.at[me]))
            if with_recvs:
                recvs.append(mk(dst_ref=out_ref.at[t]))
        return local, sends, recvs

    def start(self, *refs):
        local, sends, _ = self._copies(*refs, with_recvs=False)
        local.start()
        for cp in sends:
            cp.start()

    def wait(self, *refs):
        local, sends, recvs = self._copies(*refs, with_recvs=True)
        for cp in recvs:
            cp.wait_recv()
        for cp in sends:
            cp.wait_send()
        local.wait()


def _collective(x, axes, mode, name):
    coll = _Coll(x, axes, mode)

    def body(*refs):
        coll.start(*refs)
        coll.wait(*refs)

    return pl.pallas_call(
        body, in_specs=[pl.BlockSpec(memory_space=pl.ANY)], out_specs=pl.BlockSpec(memory_space=pl.ANY),
        out_shape=coll.out_shape, scratch_shapes=coll.scratch, name=name,
    )(x)


def all_gather(x, axes, name):
    return _collective(x, axes, "gather", name)


def exchange(x, axes, name):
    return _collective(x, axes, "exchange", name)


def sum_slots(x, name, out_dtype=f32):
    P, R, C = x.shape
    tr = _pick(R, 256, 2 * SUBLANE)

    def body(x_ref, o_ref):
        acc = x_ref[0].astype(f32)
        for j in range(1, P):
            acc = acc + x_ref[j].astype(f32)
        o_ref[...] = acc.astype(o_ref.dtype)

    return pl.pallas_call(
        body, grid=(R // tr,), in_specs=[pl.BlockSpec((P, tr, C), lambda i: (0, i, 0))],
        out_specs=pl.BlockSpec((tr, C), lambda i: (i, 0)), out_shape=jax.ShapeDtypeStruct((R, C), out_dtype),
        compiler_params=_cp("parallel"), name=name,
    )(x)


def _pack(arrs, width, row_mult, dtype):
    flat = jnp.concatenate([a.astype(dtype).reshape(-1) for a in arrs])
    unit = width * row_mult
    n = -(-flat.shape[0] // unit) * unit
    return jnp.pad(flat, (0, n - flat.shape[0])).reshape(n // width, width)


def _unpack(flat, shapes):
    flat = flat.reshape(-1)
    out, off = [], 0
    for s in shapes:
        n = int(np.prod(s))
        out.append(flat[off:off + n].reshape(s))
        off += n
    return out


def ada_fwd(c_all, ada_w):
    def body(c_ref, w_ref, o_ref):
        cond = _silu(c_ref[...])
        for l in range(ada_w.shape[0]):
            o_ref[l] = _dot(cond, w_ref[l], precision=HI)

    return pl.pallas_call(body, out_shape=jax.ShapeDtypeStruct((ada_w.shape[0], c_all.shape[0], ada_w.shape[2]), f32),
                          compiler_params=pltpu.CompilerParams(vmem_limit_bytes=VMEM_LIMIT), name="ada_fwd")(c_all, ada_w)


def ada_bwd(c_all, dmod):
    def body(c_ref, d_ref, o_ref):
        cond = _silu(c_ref[...])
        for l in range(dmod.shape[0]):
            o_ref[l] = _dot(cond, d_ref[l], TN, precision=HI)

    return pl.pallas_call(body, out_shape=jax.ShapeDtypeStruct((dmod.shape[0], c_all.shape[1], dmod.shape[2]), f32),
                          compiler_params=pltpu.CompilerParams(vmem_limit_bytes=VMEM_LIMIT), name="ada_bwd")(c_all, dmod)


def loss_fwd_bwd(y, target):
    L, D = y.shape

    def fn(i, nb, yt, tt):
        e = yt - tt
        return e * (1.0 / D), jnp.sum(jnp.sum(e * e, axis=1, keepdims=True), axis=0, keepdims=True)

    return rowwise(fn, name="loss", L=L, tm=_pick(L, 512, SUBLANE), rows=[(y, 0, D, "cur"), (target, 0, D, "cur")],
                   outs=[(D, f32)], sums=[(1, 1)])


def adamw(w, g, m, v, name):
    R, C = w.shape

    def fn(i, nb, wt, gt, mt, vt):
        m2 = ADAM_B1 * mt + (1.0 - ADAM_B1) * gt
        v2 = ADAM_B2 * vt + (1.0 - ADAM_B2) * (gt * gt)
        m_hat = m2 / (1.0 - ADAM_B1 ** ADAM_STEP)
        v_hat = v2 / (1.0 - ADAM_B2 ** ADAM_STEP)
        delta = -ADAM_LR * (m_hat / (jnp.sqrt(v_hat) + ADAM_EPS) + ADAM_WD * wt)
        return delta, m2, v2

    return rowwise(fn, name=name, L=R, tm=_pick(R, 256, SUBLANE), rows=[(a, 0, C, "cur") for a in (w, g, m, v)],
                   outs=[(C, f32)] * 3)


W_NAMES = ["ada_w", "ada_b", "norm_mix", "norm_ffn", "attn_w_in", "attn_q_norm_a", "attn_k_norm_a", "attn_q_norm_b",
           "attn_k_norm_b", "attn_sinks", "attn_w_out", "rec_w_in", "s5_lambda_re", "s5_lambda_im", "s5_log_dt",
           "s5_b_re", "s5_b_im", "s5_c_re", "s5_c_im", "s5_d", "s5_glu_w", "s5_glu_b", "dn_conv", "dn_a_log",
           "dn_dt_bias", "dn_out_norm", "rec_w_out", "ffn_w_up", "ffn_conv", "ffn_w_down"]
BIG = ["attn_w_in", "attn_w_out", "rec_w_in", "rec_w_out", "ffn_w_up", "ffn_w_down"]
SMALL_SHARDED = ["s5_d", "s5_glu_w", "s5_glu_b", "dn_conv", "ffn_conv"]
SMALL_REPL = [n for n in W_NAMES if n not in BIG and n not in SMALL_SHARDED and n != "ada_w"]
NSH = 4
GRAD_WIRE = (bf16,)


SHARD_AXIS = {"attn_w_in": 2, "attn_w_out": 1, "rec_w_in": 2, "rec_w_out": 1, "ffn_w_up": 2, "ffn_w_down": 1,
              "s5_d": 1, "s5_glu_w": 1, "s5_glu_b": 1, "dn_conv": 2, "ffn_conv": 2}


def _unshard(g, name):
    ax = SHARD_AXIS[name.rstrip("01")]
    g = jnp.moveaxis(g, 0, ax)
    s = g.shape
    return g.reshape(s[:ax] + (s[ax] * s[ax + 1],) + s[ax + 2:])


def _to_shards(full, name):
    ax = SHARD_AXIS[name.rstrip("01")]
    s = full.shape
    g = full.reshape(s[:ax] + (NSH, s[ax] // NSH) + s[ax + 1:])
    return jnp.moveaxis(g, ax, 0)


def _rec_pad_cols(w):
    z6 = jnp.zeros(w.shape[:-1] + (122,), w.dtype)
    return jnp.concatenate([w[..., 256:3328], w[..., 0:256], w[..., 3328:3334], z6, w[..., 3334:3340], z6], axis=-1)


def _rec_unpad_cols(g):
    return jnp.concatenate([g[..., 3072:3328], g[..., 0:3072], g[..., 3328:3334], g[..., 3456:3462]], axis=-1)


def _ffn_fwd(x1, nf, sc, sh, gate, w_up, conv, w_dn, tag, rides=()):
    rides = list(rides) + [None, None]
    h2 = modulate_fwd(x1, nf, sc, sh, f"{tag}_mod2_fwd")
    up = mm(h2, w_up, name=f"{tag}_ffn_up", out_dtypes=(bf16,), ride=rides[0])
    up, got0 = up if rides[0] else (up, None)
    act = ffn_act_fwd(up, conv, f"{tag}_ffn_act_fwd")
    res = mm(act, w_dn, name=f"{tag}_ffn_down", out_dtypes=(f32, f32), epi=_resid_epi, epi_mn=[x1], epi_n=[gate],
             ride=rides[1])
    return res[1], (h2, up, act, res[0]), (got0, res[2] if rides[1] else None)


def _ffn_bwd(dx, x1, nf, sc, sh, gate, w_up, conv, w_dn, saved, tag, rides=()):
    rides = list(rides) + [None, None, None]
    take = lambda res, r: res if r else (res, None)
    h2, up, act, f = saved
    df, dgate = resid_bwd(dx, f, gate, f"{tag}_res2_bwd")
    dact = mm(df, w_dn, tb=True, name=f"{tag}_ffn_dact", out_dtypes=(bf16,))
    dw_dn, got0 = take(mm(act, df, ta=True, name=f"{tag}_ffn_dwdown", out_dtypes=GRAD_WIRE, ride=rides[0]), rides[0])
    dup, dconv = ffn_act_conv_bwd(up, conv, dact, f"{tag}_ffn_act_conv_bwd")
    dw_up, got1 = take(mm(h2, dup, ta=True, name=f"{tag}_ffn_dwup", out_dtypes=GRAD_WIRE, ride=rides[1]), rides[1])
    if callable(rides[2]):
        rides[2] = rides[2](dw_dn)
    dh2, got2 = take(mm(dup, w_up, tb=True, name=f"{tag}_ffn_dh", ride=rides[2]), rides[2])
    dx, dnf, dsc, dsh = modulate_bwd(x1, nf, sc, sh, dh2, dx, f"{tag}_mod2_bwd")
    grads = dict(nf=dnf, sc=dsc, sh=dsh, gate=dgate, w_up=dw_up, conv=dconv[:FFN_CONV], w_dn=dw_dn)
    return dx, grads, (got0, got1, got2)


def kernel(x, c, ada_w, ada_b, norm_mix, norm_ffn, attn_w_in, attn_q_norm_a, attn_k_norm_a, attn_q_norm_b, attn_k_norm_b, attn_sinks, attn_w_out, rec_w_in, s5_lambda_re, s5_lambda_im, s5_log_dt, s5_b_re, s5_b_im, s5_c_re, s5_c_im, s5_d, s5_glu_w, s5_glu_b, dn_conv, dn_a_log, dn_dt_bias, dn_out_norm, rec_w_out, ffn_w_up, ffn_conv, ffn_w_down, loss_target, m_ada_w, m_ada_b, m_norm_mix, m_norm_ffn, m_attn_w_in, m_attn_q_norm_a, m_attn_k_norm_a, m_attn_q_norm_b, m_attn_k_norm_b, m_attn_sinks, m_attn_w_out, m_rec_w_in, m_s5_lambda_re, m_s5_lambda_im, m_s5_log_dt, m_s5_b_re, m_s5_b_im, m_s5_c_re, m_s5_c_im, m_s5_d, m_s5_glu_w, m_s5_glu_b, m_dn_conv, m_dn_a_log, m_dn_dt_bias, m_dn_out_norm, m_rec_w_out, m_ffn_w_up, m_ffn_conv, m_ffn_w_down, v_ada_w, v_ada_b, v_norm_mix, v_norm_ffn, v_attn_w_in, v_attn_q_norm_a, v_attn_k_norm_a, v_attn_q_norm_b, v_attn_k_norm_b, v_attn_sinks, v_attn_w_out, v_rec_w_in, v_s5_lambda_re, v_s5_lambda_im, v_s5_log_dt, v_s5_b_re, v_s5_b_im, v_s5_c_re, v_s5_c_im, v_s5_d, v_s5_glu_w, v_s5_glu_b, v_dn_conv, v_dn_a_log, v_dn_dt_bias, v_dn_out_norm, v_rec_w_out, v_ffn_w_up, v_ffn_conv, v_ffn_w_down):
    args = (ada_w, ada_b, norm_mix, norm_ffn, attn_w_in, attn_q_norm_a, attn_k_norm_a, attn_q_norm_b, attn_k_norm_b, attn_sinks, attn_w_out, rec_w_in, s5_lambda_re, s5_lambda_im, s5_log_dt, s5_b_re, s5_b_im, s5_c_re, s5_c_im, s5_d, s5_glu_w, s5_glu_b, dn_conv, dn_a_log, dn_dt_bias, dn_out_norm, rec_w_out, ffn_w_up, ffn_conv, ffn_w_down)
    ms = (m_ada_w, m_ada_b, m_norm_mix, m_norm_ffn, m_attn_w_in, m_attn_q_norm_a, m_attn_k_norm_a, m_attn_q_norm_b, m_attn_k_norm_b, m_attn_sinks, m_attn_w_out, m_rec_w_in, m_s5_lambda_re, m_s5_lambda_im, m_s5_log_dt, m_s5_b_re, m_s5_b_im, m_s5_c_re, m_s5_c_im, m_s5_d, m_s5_glu_w, m_s5_glu_b, m_dn_conv, m_dn_a_log, m_dn_dt_bias, m_dn_out_norm, m_rec_w_out, m_ffn_w_up, m_ffn_conv, m_ffn_w_down)
    vs = (v_ada_w, v_ada_b, v_norm_mix, v_norm_ffn, v_attn_w_in, v_attn_q_norm_a, v_attn_k_norm_a, v_attn_q_norm_b, v_attn_k_norm_b, v_attn_sinks, v_attn_w_out, v_rec_w_in, v_s5_lambda_re, v_s5_lambda_im, v_s5_log_dt, v_s5_b_re, v_s5_b_im, v_s5_c_re, v_s5_c_im, v_s5_d, v_s5_glu_w, v_s5_glu_b, v_dn_conv, v_dn_a_log, v_dn_dt_bias, v_dn_out_norm, v_rec_w_out, v_ffn_w_up, v_ffn_conv, v_ffn_w_down)
    W = dict(zip(W_NAMES, args))
    Mo = dict(zip(W_NAMES, ms))
    Vo = dict(zip(W_NAMES, vs))
    xi, yi, ci = lax.axis_index("x"), lax.axis_index("y"), lax.axis_index("c")
    shard = 2 * xi + yi
    me8 = 4 * xi + 2 * yi + ci
    xs = x[0]
    target = loss_target[0]
    L, D = xs.shape

    XY = ("x", "y")
    wparts = [
        [("attn_w_in", attn_w_in), ("attn_w_out", attn_w_out)],
        [("rec_w_in", rec_w_in), ("rec_w_out", rec_w_out)],
        [("ffn_w_up1", ffn_w_up[1:2]), ("ffn_w_down1", ffn_w_down[1:2])],
        [("ffn_w_up0", ffn_w_up[0:1]), ("ffn_w_down0", ffn_w_down[0:1])],
    ]
    wpack = [_pack([a for _, a in p], 1024, 16, bf16) for p in wparts]
    Wf = {}

    def unpack_weights(gathered, part):
        flat = gathered.reshape(NSH, -1)
        off = 0
        for n, a in part:
            sz = int(np.prod(a.shape))
            Wf[n] = _unshard(flat[:, off:off + sz].reshape((NSH,) + a.shape), n)[0]
            off += sz

    unpack_weights(all_gather(wpack[0], XY, "gather_w0"), wparts[0])

    sflat = _pack([c] + [W[n] for n in SMALL_SHARDED], 1024, 8, f32)
    s8 = all_gather(sflat, AXES, "gather_small")
    s8f = s8.reshape(8, -1)
    c_all = s8f[:, :D]
    Ws = {}
    off = D
    for n in SMALL_SHARDED:
        sz = int(np.prod(W[n].shape))
        Ws[n] = _unshard(s8f[0::2, off:off + sz].reshape((NSH,) + W[n].shape), n)
        off += sz

    modp = ada_fwd(c_all, ada_w)
    modg = all_gather(modp, ("x", "y"), "gather_mod")
    mod_all = jnp.moveaxis(modg, 0, 2).reshape(2, 8, -1) + ada_b[:, None, :]
    mod = lax.dynamic_slice(mod_all, (0, me8, 0), (2, 1, mod_all.shape[2]))[:, 0, :]
    mods = [[mod[l:l + 1, j * D:(j + 1) * D] for j in range(6)] for l in range(2)]

    sh1, sc1, g1, sh2, sc2, g2_ = mods[0]
    nm0, nf0 = norm_mix[0:1], norm_ffn[0:1]
    sinkb = jnp.repeat(attn_sinks[0], HEAD_DIM)[None]
    h0 = modulate_fwd(xs, nm0, sc1, sh1, "l0_mod1_fwd")
    hin0 = mm(h0, Wf["attn_w_in"], name="l0_in_proj")
    ocat, att_saved, got = attention_fwd(hin0, attn_q_norm_a, attn_k_norm_a, attn_q_norm_b, attn_k_norm_b, sinkb,
                                         rides={1: (wpack[3], XY, "gather"), 4: (wpack[2], XY, "gather")})
    unpack_weights(got[1], wparts[3])
    unpack_weights(got[4], wparts[2])
    y0, x1 = mm(ocat, Wf["attn_w_out"], name="l0_out_proj", out_dtypes=(f32, f32), epi=_resid_epi,
                epi_mn=[xs], epi_n=[g1])
    x2, ffn0_saved, got = _ffn_fwd(x1, nf0, sc2, sh2, g2_, Wf["ffn_w_up0"], Ws["ffn_conv"][0], Wf["ffn_w_down0"], "l0",
                                   rides=[(wpack[1], XY, "gather")])
    unpack_weights(got[0], wparts[1])
    rec_w_in_p = _rec_pad_cols(Wf["rec_w_in"])

    th1, tc1, t1, th2, tc2, t2 = mods[1]
    nm1, nf1 = norm_mix[1:2], norm_ffn[1:2]
    pad128 = lambda a: jnp.pad(a, ((0, 0), (0, 128 - a.shape[1])))
    s5p = dict(lr=s5_lambda_re[0], li=s5_lambda_im[0], ldt=s5_log_dt[0][:, None], b_re=s5_b_re[0], b_im=s5_b_im[0],
               c_re=s5_c_re[0], c_im=s5_c_im[0], d=Ws["s5_d"], gw=Ws["s5_glu_w"][0], gb=Ws["s5_glu_b"])
    dnp = dict(conv=Ws["dn_conv"][0], alog=pad128(dn_a_log), dtb=pad128(dn_dt_bias), onorm=dn_out_norm)
    h1 = modulate_fwd(x2, nm1, tc1, th1, "l1_mod1_fwd")
    hin1 = mm(h1, rec_w_in_p, name="l1_in_proj")
    yc, s5_saved = s5_fwd(hin1, s5p)
    ycat, dn_saved = deltanet_fwd(hin1, dnp, yc)
    y1, x3 = mm(ycat, Wf["rec_w_out"], name="l1_out_proj", out_dtypes=(f32, f32), epi=_resid_epi,
                epi_mn=[x2], epi_n=[t1])
    x4, ffn1_saved, _ = _ffn_fwd(x3, nf1, tc2, th2, t2, Wf["ffn_w_up1"], Ws["ffn_conv"][1], Wf["ffn_w_down1"], "l1")

    dx, sse = loss_fwd_bwd(x4, target)
    loss = lax.psum(0.5 * sse[0, 0] / D, AXES)

    dx, gf1, _ = _ffn_bwd(dx, x3, nf1, tc2, th2, t2, Wf["ffn_w_up1"], Ws["ffn_conv"][1], Wf["ffn_w_down1"], ffn1_saved, "l1")
    dy1, dt1 = resid_bwd(dx, y1, t1, "l1_res1_bwd")
    dycat = mm(dy1, Wf["rec_w_out"], tb=True, name="l1_dycat")
    dw_rec_out = mm(ycat, dy1, ta=True, name="l1_dwout", out_dtypes=GRAD_WIRE)
    du_skip, du_b, s5g = s5_bwd(hin1, s5p, s5_saved, dycat)
    dx_qkv, dz, dab, dng = deltanet_bwd(hin1, dnp, dn_saved, dycat)
    dhin1 = rec_assemble(dx_qkv, dz, du_skip, du_b, dab)
    dw_rec_in = _rec_unpad_cols(mm(h1, dhin1, ta=True, name="l1_dwin", out_dtypes=GRAD_WIRE))
    dh1 = mm(dhin1, rec_w_in_p, tb=True, name="l1_dh")
    dx, dnm1, dtc1, dth1 = modulate_bwd(x2, nm1, tc1, th1, dh1, dx, "l1_mod1_bwd")

    def grad_part(items):
        flat = jnp.concatenate([_to_shards(g, n).reshape(NSH, -1) for n, g in items], axis=1)
        unit = 256 * 1024
        npad = -(-flat.shape[1] // unit) * unit
        return jnp.pad(flat, ((0, 0), (0, npad - flat.shape[1]))).reshape(NSH, npad // 1024, 1024)

    w_dn1 = gf1["w_dn"][None]
    part2 = lambda dw_dn0: (grad_part([("ffn_w_down1", w_dn1), ("ffn_w_down0", dw_dn0[None])]), XY, "exchange")
    gparts = [[("rec_w_in", dw_rec_in[None]), ("rec_w_out", dw_rec_out[None])], [("ffn_w_up1", gf1["w_up"][None])]]
    dx, gf0, gq = _ffn_bwd(dx, x1, nf0, sc2, sh2, g2_, Wf["ffn_w_up0"], Ws["ffn_conv"][0], Wf["ffn_w_down0"], ffn0_saved,
                           "l0", rides=[(grad_part(gparts[0]), XY, "exchange"), (grad_part(gparts[1]), XY, "exchange"), part2])
    gparts.append([("ffn_w_down1", w_dn1), ("ffn_w_down0", gf0["w_dn"][None])])
    dy0, dg1 = resid_bwd(dx, y0, g1, "l0_res1_bwd")
    dcat = mm(dy0, Wf["attn_w_out"], tb=True, name="l0_dcat")
    dw_attn_out = mm(ocat, dy0, ta=True, name="l0_dwout", out_dtypes=GRAD_WIRE)
    gparts += [[("ffn_w_up0", gf0["w_up"][None])], [("attn_w_out", dw_attn_out[None])]]
    chip_sum = lambda qs, i0: jnp.concatenate([sum_slots(q, f"sum_chips{i0 + i}", bf16) for i, q in enumerate(qs)], axis=0)
    dhin0, dwqa, dwka, dwqb, dwkb, dsinkb, gots = attention_bwd(
        hin0, attn_q_norm_a, attn_k_norm_a, attn_q_norm_b, attn_k_norm_b, sinkb, att_saved, dcat,
        rides={"a": (grad_part(gparts[3]), XY, "exchange"), 1: (chip_sum(gq, 0), ("c",), "gather"),
               4: (grad_part(gparts[4]), XY, "exchange"), 16: lambda g: (chip_sum([g["a"]], 3), ("c",), "gather")})
    dw_attn_in = mm(h0, dhin0, ta=True, name="l0_dwin", out_dtypes=GRAD_WIRE)
    gparts.append([("attn_w_in", dw_attn_in[None])])
    dh0, gq5 = mm(dhin0, Wf["attn_w_in"], tb=True, name="l0_dh", ride=(grad_part(gparts[5]), XY, "exchange"))
    grad_x, dnm0, dsc1, dsh1 = modulate_bwd(xs, nm0, sc1, sh1, dh0, dx, "l0_mod1_bwd")

    dmod = jnp.concatenate([
        jnp.concatenate([dsh1, dsc1, dg1, gf0["sh"], gf0["sc"], gf0["gate"]], axis=1),
        jnp.concatenate([dth1, dtc1, dt1, gf1["sh"], gf1["sc"], gf1["gate"]], axis=1)], axis=0)
    gl = {
        "ada_b": dmod,
        "norm_mix": jnp.concatenate([dnm0, dnm1], axis=0),
        "norm_ffn": jnp.concatenate([gf0["nf"], gf1["nf"]], axis=0),
        "attn_q_norm_a": dwqa, "attn_k_norm_a": dwka, "attn_q_norm_b": dwqb, "attn_k_norm_b": dwkb,
        "attn_sinks": dsinkb[:, ::HEAD_DIM],
        "s5_lambda_re": s5g["lr"][None], "s5_lambda_im": s5g["li"][None], "s5_log_dt": s5g["ldt"][:, 0][None],
        "s5_b_re": s5g["b_re"][None], "s5_b_im": s5g["b_im"][None], "s5_c_re": s5g["c_re"][None],
        "s5_c_im": s5g["c_im"][None],
        "dn_a_log": dng["alog"][:, :DN_HEADS], "dn_dt_bias": dng["dtb"][:, :DN_HEADS], "dn_out_norm": dng["onorm"],
        "s5_d": s5g["d"], "s5_glu_w": s5g["gw"][None], "s5_glu_b": s5g["gb"], "dn_conv": dng["conv"][None],
        "ffn_conv": jnp.stack([gf0["conv"], gf1["conv"]]),
    }

    small_names = SMALL_REPL + SMALL_SHARDED
    gs = _pack([gl[n] for n in small_names], 128, 256, f32)
    gs8 = all_gather(gs, AXES, "gather_small_grads")
    gsum = sum_slots(gs8, "sum_small_grads")
    full_shapes = [gl[n].shape for n in small_names]
    gfull = dict(zip(small_names, _unpack(gsum, full_shapes)))
    dmod_all = gs8.reshape(8, -1)[:, :2 * 6 * D].reshape(8, 2, 6 * D)
    ncol = ada_w.shape[2]
    dmod_sh = jnp.moveaxis(lax.dynamic_slice(dmod_all, (0, 0, shard * ncol), (8, 2, ncol)), 0, 1)
    grads = {"ada_w": ada_bwd(c_all, dmod_sh)}
    for n in SMALL_REPL:
        grads[n] = gfull[n]
    for n in SMALL_SHARDED:
        sh_all = _to_shards(gfull[n], n)
        grads[n] = lax.dynamic_slice(sh_all, (shard,) + (0,) * (sh_all.ndim - 1), (1,) + sh_all.shape[1:])[0]

    gq = list(gq) + [gots["a"], gots[4], gq5]
    gc45 = all_gather(chip_sum(gq[4:], 4), ("c",), "gather_grad_c")
    gsh = jnp.concatenate([sum_slots(gots[1], "sum_pair012"), sum_slots(gots[16], "sum_pair3"),
                           sum_slots(gc45, "sum_pair45")], axis=0)
    row, got = 0, {}
    for part, q in zip(gparts, gq):
        flat = gsh[row:row + q.shape[1]].reshape(-1)
        row += q.shape[1]
        off = 0
        for n, g in part:
            sz = g.size // NSH
            got[n] = flat[off:off + sz].reshape((1,) + g.shape[1:-2] + _to_shards(g, n).shape[-2:])
            off += sz
    for n in ("attn_w_in", "attn_w_out", "rec_w_in", "rec_w_out"):
        grads[n] = got[n]
    grads["ffn_w_up"] = jnp.concatenate([got["ffn_w_up0"], got["ffn_w_up1"]], axis=0)
    grads["ffn_w_down"] = jnp.concatenate([got["ffn_w_down0"], got["ffn_w_down1"]], axis=0)

    delta, new_m, new_v = {}, {}, {}

    def as2d(a):
        return a.reshape(-1, a.shape[-1])

    for n in ["ada_w"] + BIG:
        d_, m_, v_ = adamw(as2d(W[n]), as2d(grads[n]), as2d(Mo[n]), as2d(Vo[n]), f"adamw_{n}")
        delta[n], new_m[n], new_v[n] = d_.reshape(W[n].shape), m_.reshape(W[n].shape), v_.reshape(W[n].shape)
    pk = lambda dd: _pack([dd[n] for n in small_names], 128, 256, f32)
    d_, m_, v_ = adamw(pk(W), pk(grads), pk(Mo), pk(Vo), "adamw_small")
    shp = [W[n].shape for n in small_names]
    for dst, src in ((delta, d_), (new_m, m_), (new_v, v_)):
        dst.update(zip(small_names, _unpack(src, shp)))

    return (loss, grad_x[None], *[grads[n] for n in W_NAMES], *[delta[n] for n in W_NAMES],
            *[new_m[n] for n in W_NAMES], *[new_v[n] for n in W_NAMES])
```

```python
import functools
import math

import numpy as np
import jax
import jax.numpy as jnp
from jax import lax
from jax.experimental import pallas as pl
from jax.experimental.pallas import tpu as pltpu

f32 = jnp.float32
bf16 = jnp.bfloat16
HI = lax.Precision.HIGHEST
MESH = pl.DeviceIdType.MESH

HEAD_DIM = 64
BLOCK = 128
A_Q_HEADS = 8
A_KV_HEADS = 2
A_WINDOW = 128
B_HEADS = 8
B_BRANCHES = ((128, 1), (512, 4), (2048, 16))
N_ATTN_HEADS = 16
ATTN_IN = 2304
S5_GROUP = 16
S5_GROUPS = 16
S5_WIDTH = 256
S5_STATE = 64
DN_HEADS = 6
DN_DK = 128
DN_CONV = 4
DN_CHUNK = 64
REC_IN = 3340
REC_PAD = 3584
FFN_CONV = 3
EPS = 1e-6
ADAM_LR = 0.001
ADAM_B1 = 0.9
ADAM_B2 = 0.999
ADAM_EPS = 1e-08
ADAM_WD = 0.01
ADAM_STEP = 10

LANE = 128
SUBLANE = 8
VMEM_LIMIT = 52 * 1024 * 1024
MM_FULL_K = 5632
MM_VMEM_BUDGET = 40 * 1024 * 1024


def _cp(*sem):
    return pltpu.CompilerParams(dimension_semantics=sem, vmem_limit_bytes=VMEM_LIMIT)


def _pick(dim, cap, unit=LANE):
    for t in (2048, 1024, 768, 512, 384, 256, 128, 64, 32, 16, 8):
        if t <= cap and t % unit == 0 and dim % t == 0:
            return t
    return dim


def _dot(a, b, dims=(((1,), (0,)), ((), ())), precision=None):
    return lax.dot_general(a, b, dims, precision=precision, preferred_element_type=f32)


NN = (((1,), (0,)), ((), ()))
NT = (((1,), (1,)), ((), ()))
TN = (((0,), (0,)), ((), ()))


def mm(a, b, *, name, ta=False, tb=False, a_win=None, b_win=None, out_dtypes=(f32,),
       epi=None, epi_mn=(), epi_n=(), tm_cap=1024, tn_cap=8192, tk_cap=None, ride=None):
    coll = _Coll(*ride) if ride else None
    a0, a1 = a.shape
    b0, b1 = b.shape
    aw = a_win or (0, a1)
    bw = b_win or (0, b1)
    if ta:
        K, M = a0, aw[1]
    else:
        M, K = a0, aw[1]
    if tb:
        N, K2 = b0, bw[1]
    else:
        K2, N = b0, bw[1]
    assert K == K2, (a.shape, b.shape, ta, tb, a_win, b_win)
    if tk_cap is None:
        tk_cap = K if K <= MM_FULL_K else 2048
    tk = _pick(K, tk_cap, SUBLANE if (ta and not tb) else LANE)
    nk = K // tk
    sa, sb = a.dtype.itemsize, b.dtype.itemsize
    so = sum(jnp.dtype(d).itemsize for d in out_dtypes)
    n_mn, n_n, n_out = len(epi_mn), len(epi_n), len(out_dtypes)

    def vmem(tm_, tn_):
        return 2 * (tm_ * tk * sa + tk * tn_ * sb + tm_ * tn_ * (so + 4 * n_mn)) + 2 * tm_ * tn_ * 4

    best = None
    for tm_ in (t for t in (1024, 512, 256, 128) if M % t == 0 and (not ta or aw[0] % t == 0)):
        for tn_ in (t for t in (N, N // 2, 1024, 768, 512, 384, 256, 128)
                    if t % LANE == 0 and N % t == 0 and (tb or bw[0] % t == 0)):
            if tm_ <= tm_cap and tn_ <= max(tn_cap, 0) and vmem(tm_, tn_) <= MM_VMEM_BUDGET:
                if best is None or (tm_ * tn_, tn_) > (best[0] * best[1], best[1]):
                    best = (tm_, tn_)
    assert best is not None, (name, M, N, K)
    tm, tn = best
    b_outer = tk * tn * sb > tm * tk * sa

    def ix(f):
        if b_outer:
            return lambda j, i, k: f(i, j, k)
        return f

    if ta:
        mo = aw[0] // tm
        a_spec = pl.BlockSpec((tk, tm), ix(lambda i, j, k: (k, i + mo)))
    else:
        assert aw[0] % tk == 0
        ko = aw[0] // tk
        a_spec = pl.BlockSpec((tm, tk), ix(lambda i, j, k: (i, k + ko)))
    if tb:
        assert bw[0] % tk == 0
        kob = bw[0] // tk
        b_spec = pl.BlockSpec((tn, tk), ix(lambda i, j, k: (j, k + kob)))
    else:
        no = bw[0] // tn
        b_spec = pl.BlockSpec((tk, tn), ix(lambda i, j, k: (k, j + no)))
    dims = (((0 if ta else 1,), (1 if tb else 0,)), ((), ()))

    gi, gj = M // tm, N // tn
    grid = (gj, gi, nk) if b_outer else (gi, gj, nk)

    def body(a_ref, b_ref, *rest):
        mn_refs = rest[:n_mn]
        n_refs = rest[n_mn:n_mn + n_n]
        o0 = n_mn + n_n
        out_refs = rest[o0:o0 + n_out]

        def finish(r):
            if epi is None:
                outs = (r,)
            else:
                outs = epi(r, *[m[...] for m in mn_refs], *[v[...] for v in n_refs])
            for o_ref, o in zip(out_refs, outs):
                o_ref[...] = o.astype(o_ref.dtype)

        part = _dot(a_ref[...].astype(bf16), b_ref[...].astype(bf16), dims)
        if nk == 1:
            finish(part)
        else:
            acc = rest[o0 + n_out]
            k = pl.program_id(2)

            @pl.when(k == 0)
            def _():
                acc[...] = part

            @pl.when(k > 0)
            def _():
                acc[...] += part

            @pl.when(k == nk - 1)
            def _():
                finish(acc[...])

    mn_spec = pl.BlockSpec((tm, tn), ix(lambda i, j, k: (i, j)))
    n_spec = pl.BlockSpec((1, tn), ix(lambda i, j, k: (0, j)))
    outs = _ride_call(
        body, coll, ride, grid=grid,
        in_specs=[a_spec, b_spec] + [mn_spec] * n_mn + [n_spec] * n_n, out_specs=[mn_spec] * n_out,
        out_shape=[jax.ShapeDtypeStruct((M, N), d) for d in out_dtypes],
        scratch_shapes=[pltpu.VMEM((tm, tn), f32)] if nk > 1 else [],
        semantics=("parallel", "parallel", "arbitrary"), name=name, args=[a, b, *epi_mn, *epi_n])
    return outs[0] if len(outs) == 1 else tuple(outs)


def rowwise(fn, *, name, L, tm, rows=(), consts=(), outs=(), sums=()):
    nb = L // tm
    in_specs = []
    arrs = []
    for arr, start, width, kind in rows:
        assert start % width == 0, (name, start, width)
        co = start // width
        hr = SUBLANE * (4 // arr.dtype.itemsize)
        hb = tm // hr
        if kind == "cur":
            in_specs.append(pl.BlockSpec((tm, width), lambda i, co=co: (i, co)))
        elif kind == "prev":
            in_specs.append(pl.BlockSpec((hr, width), lambda i, co=co, hb=hb: (jnp.maximum(i * hb - 1, 0), co)))
        else:
            last = L // hr - 1
            in_specs.append(pl.BlockSpec((hr, width), lambda i, co=co, hb=hb, last=last:
                                         (jnp.minimum((i + 1) * hb, last), co)))
        arrs.append(arr)
    for cst in consts:
        assert cst.ndim == 2
        in_specs.append(pl.BlockSpec(cst.shape, lambda i: (0, 0)))
        arrs.append(cst)
    n_rows, n_c, n_o, n_s = len(rows), len(consts), len(outs), len(sums)
    out_specs = [pl.BlockSpec((tm, w), lambda i: (i, 0)) for w, _ in outs]
    out_specs += [pl.BlockSpec(s, lambda i: (0, 0)) for s in sums]
    out_shape = [jax.ShapeDtypeStruct((L, w), d) for w, d in outs]
    out_shape += [jax.ShapeDtypeStruct(s, f32) for s in sums]

    def body(*refs):
        i = pl.program_id(0)
        vals = [r[...] for r in refs[:n_rows + n_c]]
        res = fn(i, nb, *vals)
        if not isinstance(res, (tuple, list)):
            res = (res,)
        o_refs = refs[n_rows + n_c:n_rows + n_c + n_o]
        s_refs = refs[n_rows + n_c + n_o:]
        for o_ref, o in zip(o_refs, res[:n_o]):
            o_ref[...] = o.astype(o_ref.dtype)
        if n_s:
            @pl.when(i == 0)
            def _():
                for s_ref in s_refs:
                    s_ref[...] = jnp.zeros_like(s_ref)

            for s_ref, s in zip(s_refs, res[n_o:]):
                s_ref[...] += s

    res = pl.pallas_call(
        body,
        grid=(nb,),
        in_specs=in_specs,
        out_specs=out_specs,
        out_shape=out_shape,
        compiler_params=_cp("arbitrary" if n_s else "parallel"),
        name=name,
    )(*arrs)
    return res[0] if len(res) == 1 else tuple(res)


def _shift_down(x, prev8, k):
    cat = jnp.concatenate([prev8, x], axis=0)
    return pltpu.roll(cat, k, 0)[prev8.shape[0]:, :]


def _colsum(x):
    return jnp.sum(x, axis=0, keepdims=True)


def _silu(x):
    return x * jax.nn.sigmoid(x)


def _modulate_fn(x, nw, sc, sh):
    r = lax.rsqrt(jnp.mean(x * x, axis=-1, keepdims=True) + EPS)
    return (x * r * nw) * (1.0 + sc) + sh


def modulate_fwd(x, nw, sc, sh, name):
    L, D = x.shape

    def fn(i, nb, xt, nwv, scv, shv):
        return _modulate_fn(xt, nwv, scv, shv)

    return rowwise(fn, name=name, L=L, tm=_pick(L, 512, SUBLANE), rows=[(x, 0, D, "cur")],
                   consts=[nw, sc, sh], outs=[(D, bf16)])


def modulate_bwd(x, nw, sc, sh, dh, dx_in, name):
    L, D = x.shape

    def fn(i, nb, xt, dht, dxt, nwv, scv, shv):
        _, vjp = jax.vjp(_modulate_fn, xt, nwv, scv, shv)
        dx, dnw, dsc, dsh = vjp(dht)
        return dxt + dx, dnw, dsc, dsh

    return rowwise(fn, name=name, L=L, tm=_pick(L, 256, SUBLANE),
                   rows=[(x, 0, D, "cur"), (dh, 0, D, "cur"), (dx_in, 0, D, "cur")],
                   consts=[nw, sc, sh], outs=[(D, f32)], sums=[(1, D)] * 3)


def resid_bwd(dx, y, g, name):
    L, D = dx.shape

    def fn(i, nb, dxt, yt, gv):
        return dxt * gv, _colsum(dxt * yt)

    return rowwise(fn, name=name, L=L, tm=_pick(L, 512, SUBLANE),
                   rows=[(dx, 0, D, "cur"), (y, 0, D, "cur")], consts=[g],
                   outs=[(D, bf16)], sums=[(1, D)])


def _resid_epi(acc, xt, gv):
    return acc, xt + gv * acc


def _stack_rows(rows, n=SUBLANE):
    c = rows[0].shape[1]
    ridx = lax.broadcasted_iota(jnp.int32, (n, c), 0)
    out = jnp.zeros((n, c), f32)
    for j, r in enumerate(rows):
        out = out + jnp.where(ridx == j, r, 0.0)
    return out


def _conv_causal(x, prev8, w):
    W = w.shape[0]
    y = x * w[W - 1:W, :]
    for j in range(W - 1):
        y = y + _shift_down(x, prev8, W - 1 - j) * w[j:j + 1, :]
    return y


def _conv_causal_bwd_x(dy_ext, tm, w):
    W = w.shape[0]
    n = dy_ext.shape[0]
    dx = dy_ext[:tm] * w[W - 1:W, :]
    for j in range(W - 1):
        dx = dx + pltpu.roll(dy_ext, n - (W - 1 - j), 0)[:tm] * w[j:j + 1, :]
    return dx


def _conv_causal_bwd_w(dy, x, prev8, W):
    rows = [_colsum(dy * _shift_down(x, prev8, W - 1 - j)) for j in range(W - 1)]
    rows.append(_colsum(dy * x))
    return _stack_rows(rows)


def ffn_act_fwd(up, conv_w, name):
    L, F2 = up.shape
    F = F2 // 2

    def fn(i, nb, u, p8, w):
        c = _conv_causal(u.astype(f32), p8.astype(f32) * (i > 0).astype(f32), w)
        return _silu(c[:, :F]) * c[:, F:]

    return rowwise(fn, name=name, L=L, tm=_pick(L, 128, SUBLANE),
                   rows=[(up, 0, F2, "cur"), (up, 0, F2, "prev")], consts=[conv_w], outs=[(F, bf16)])


def ffn_act_conv_bwd(up, conv_w, dact, name):
    L, F2 = up.shape
    F = F2 // 2
    W = conv_w.shape[0]

    def fn(i, nb, u, da, p8, un8, dan8, w):
        tm = u.shape[0]
        more = (i < nb - 1).astype(f32)
        u, da = u.astype(f32), da.astype(f32)
        p8 = p8.astype(f32) * (i > 0).astype(f32)
        c = _conv_causal(jnp.concatenate([u, un8.astype(f32) * more], axis=0), p8, w)
        dae = jnp.concatenate([da, dan8.astype(f32) * more], axis=0)
        a, b = c[:, :F], c[:, F:]
        sg = jax.nn.sigmoid(a)
        dc = jnp.concatenate([dae * b * (sg * (1.0 + a * (1.0 - sg))), dae * a * sg], axis=1)
        return _conv_causal_bwd_x(dc, tm, w), _conv_causal_bwd_w(dc[:tm], u, p8, W)

    return rowwise(fn, name=name, L=L, tm=_pick(L, 128, SUBLANE),
                   rows=[(up, 0, F2, "cur"), (dact, 0, F, "cur"), (up, 0, F2, "prev"), (up, 0, F2, "next"),
                         (dact, 0, F, "next")],
                   consts=[conv_w], outs=[(F2, bf16)], sums=[(SUBLANE, F2)])


ALIBI = [2.0 ** (-8.0 * (i + 1) / N_ATTN_HEADS) for i in range(N_ATTN_HEADS)]
NEG = -1e30


def _band_mask(n, d, max_dist):
    qi = lax.broadcasted_iota(jnp.int32, (BLOCK, 2 * BLOCK), 0)
    kj = lax.broadcasted_iota(jnp.int32, (BLOCK, 2 * BLOCK), 1)
    dist = BLOCK + qi - kj
    valid = (dist >= 0) & (dist <= max_dist) & ((n > 0) | (kj >= BLOCK))
    return valid, -(d * dist).astype(f32)


def _rms64(x, w):
    r = lax.rsqrt(jnp.mean(x * x, axis=-1, keepdims=True) + EPS)
    xh = x * r
    return xh * w, xh, r


def _rms64_bwd(dy, xh, r, w):
    t = dy * w
    dw = jnp.sum(jnp.sum(dy * xh, axis=0), axis=0, keepdims=True)
    return r * (t - xh * jnp.mean(t * xh, axis=-1, keepdims=True)), dw


class _Plan:
    def __init__(self, dilation, group_a, nq):
        self.d, self.nq = dilation, nq
        if group_a:
            self.P, self.nkv = 1, 1
            self.q0, self.k0, self.v0 = 0, 4, 5
            self.kv_of = lambda j: j // 4
            self.max_dist = A_WINDOW - 1
            slopes = ALIBI[:8]
        else:
            self.P, self.nkv = 4 // nq, nq
            self.q0, self.k0, self.v0 = 6, 10, 14
            self.kv_of = lambda j: j
            self.max_dist = BLOCK
            slopes = ALIBI[8:]
        self.hps = 2 * nq
        sl = np.repeat(np.asarray(slopes, np.float32), HEAD_DIM).reshape(self.P, 1, self.hps * HEAD_DIM)
        self.slopes = jnp.asarray(sl, f32)


def _rows(r, d):
    return pl.ds(r, BLOCK, stride=d) if d > 1 else pl.ds(0, BLOCK)


def _pairs(refs, rows):
    parts = []
    for ref in refs:
        blk = ref[rows, :]
        parts += [blk[:, :HEAD_DIM], blk[:, HEAD_DIM:]]
    return jnp.stack(parts)


def _pairs2(prev_refs, cur_refs, rows):
    parts = []
    for pr, cr in zip(prev_refs, cur_refs):
        blk = jnp.concatenate([pr[rows, :], cr[rows, :]], axis=0)
        parts += [blk[:, :HEAD_DIM], blk[:, HEAD_DIM:]]
    return jnp.stack(parts)


def _lane_pair(t, i):
    return jnp.concatenate([t[2 * i], t[2 * i + 1]], axis=1)


def _riding(body, coll, n_in, n_out, grid):
    if coll is None:
        return body

    def wrapped(*refs):
        ride_refs = (refs[n_in], refs[n_in + 1 + n_out]) + tuple(refs[-3:])
        inner = refs[:n_in] + refs[n_in + 1:n_in + 1 + n_out] + refs[n_in + 2 + n_out:-3]
        pid = [pl.program_id(t) for t in range(len(grid))]

        @pl.when(functools.reduce(jnp.logical_and, [p == 0 for p in pid]))
        def _():
            coll.start(*ride_refs)

        body(*inner)

        @pl.when(functools.reduce(jnp.logical_and, [p == g - 1 for p, g in zip(pid, grid)]))
        def _():
            coll.wait(*ride_refs)

    return wrapped


def _ride_call(body, coll, ride, *, grid, in_specs, out_specs, out_shape, scratch_shapes, semantics, name, args):
    hbm = [pl.BlockSpec(memory_space=pl.ANY)] if coll else []
    return pl.pallas_call(
        _riding(body, coll, len(in_specs), len(out_specs), grid), grid=grid,
        in_specs=list(in_specs) + hbm, out_specs=list(out_specs) + hbm,
        out_shape=list(out_shape) + ([coll.out_shape] if coll else []),
        scratch_shapes=list(scratch_shapes) + (coll.scratch if coll else []),
        compiler_params=_cp(*(["arbitrary"] * len(grid) if coll else semantics)), name=name,
    )(*args, *([ride[0]] if coll else []))


def attn2_fwd(hin, plan, wq, wk, name, ride=None):
    coll = _Coll(*ride) if ride else None
    L = hin.shape[0]
    d, nq, nkv, hps, P = plan.d, plan.nq, plan.nkv, plan.hps, plan.P
    R = BLOCK * d
    nb = L // R
    kv_of, max_dist = plan.kv_of, plan.max_dist
    gqa = 2 * nkv != hps

    def body(*refs):
        q_refs = refs[:nq]
        kp, kc = refs[nq:nq + nkv], refs[nq + nkv:nq + 2 * nkv]
        vp, vc = refs[nq + 2 * nkv:nq + 3 * nkv], refs[nq + 3 * nkv:nq + 4 * nkv]
        sl_ref, wq_ref, wk_ref, o_ref, lse_ref = refs[nq + 4 * nkv:nq + 4 * nkv + 5]
        o_refs = refs[nq + 4 * nkv + 5:2 * nq + 4 * nkv + 5]
        lse_refs = refs[2 * nq + 4 * nkv + 5:]
        n = pl.program_id(1)
        valid, negd = _band_mask(n, d, max_dist)
        slope = jnp.stack([sl_ref[0, :, j * 64:j * 64 + 1] for j in range(hps)])
        wqv, wkv = wq_ref[...], wk_ref[...]

        def residue(r, carry):
            rows = _rows(r, d)
            kn = _rms64(_pairs2(kp, kc, rows), wkv)[0].astype(bf16)
            v = _pairs2(vp, vc, rows).astype(bf16)
            if gqa:
                kn = jnp.stack([kn[kv_of(j)] for j in range(hps)])
                v = jnp.stack([v[kv_of(j)] for j in range(hps)])
            qn = _rms64(_pairs(q_refs, rows), wqv)[0].astype(bf16)
            s = _dot(qn, kn, BNT) * (HEAD_DIM ** -0.5) + slope * negd
            s = jnp.where(valid, s, NEG)
            m = jnp.max(s, axis=-1, keepdims=True)
            p = jnp.exp(s - m)
            l = jnp.sum(p, axis=-1, keepdims=True)
            o = _dot(p.astype(bf16), v, BNN) / l
            lse = jnp.broadcast_to(m + jnp.log(l), (hps, BLOCK, HEAD_DIM))
            for i in range(nq):
                o_refs[i][rows, :] = _lane_pair(o, i)
                lse_refs[i][rows, :] = _lane_pair(lse, i)
            return carry

        lax.fori_loop(0, d, residue, 0)
        for i in range(nq):
            o_ref[:, i * 128:(i + 1) * 128] = o_refs[i][...]
            lse_ref[:, i * 128:(i + 1) * 128] = lse_refs[i][...]

    col = lambda c0, i: (lambda p, n: (n, c0 + p * nq + i))
    prv = lambda c0, i: (lambda p, n: (jnp.maximum(n - 1, 0), c0 + p * nq + i))
    blk = lambda f: pl.BlockSpec((R, 128), f)
    in_specs = [blk(col(plan.q0, i)) for i in range(nq)]
    in_specs += [blk(prv(plan.k0, i)) for i in range(nkv)] + [blk(col(plan.k0, i)) for i in range(nkv)]
    in_specs += [blk(prv(plan.v0, i)) for i in range(nkv)] + [blk(col(plan.v0, i)) for i in range(nkv)]
    in_specs += [pl.BlockSpec((1, 1, hps * 64), lambda p, n: (p, 0, 0)),
                 pl.BlockSpec((1, 64), lambda p, n: (0, 0)), pl.BlockSpec((1, 64), lambda p, n: (0, 0))]
    wide = pl.BlockSpec((R, 128 * nq), lambda p, n: (n, p))
    return _ride_call(
        body, coll, ride, grid=(P, nb), in_specs=in_specs, out_specs=[wide, wide],
        out_shape=[jax.ShapeDtypeStruct((L, 512), f32)] * 2,
        scratch_shapes=[pltpu.VMEM((R, 128), f32)] * (2 * nq),
        semantics=("parallel", "parallel"), name=name,
        args=[hin] * (nq + 4 * nkv) + [plan.slopes, wq, wk])


def attn2_bwd(hin, plan, wq, wk, o, lse, do, dlse, dw0, name, ride=None):
    coll = _Coll(*ride) if ride else None
    L = hin.shape[0]
    d, nq, nkv, hps, P = plan.d, plan.nq, plan.nkv, plan.hps, plan.P
    R = BLOCK * d
    nb = L // R
    kv_of, max_dist = plan.kv_of, plan.max_dist
    nkh = 2 * nkv
    gqa = nkh != hps
    n_in = nq + 4 * nkv + 3 + 4 * nq + 2

    def body(*refs):
        q_refs = refs[:nq]
        kp, kc = refs[nq:nq + nkv], refs[nq + nkv:nq + 2 * nkv]
        vp, vc = refs[nq + 2 * nkv:nq + 3 * nkv], refs[nq + 3 * nkv:nq + 4 * nkv]
        b = nq + 4 * nkv
        sl_ref, wq_ref, wk_ref = refs[b:b + 3]
        b += 3
        o_refs, lse_refs = refs[b:b + nq], refs[b + nq:b + 2 * nq]
        do_refs, dlse_refs = refs[b + 2 * nq:b + 3 * nq], refs[b + 3 * nq:b + 4 * nq]
        dwq0_ref, dwk0_ref = refs[b + 4 * nq:b + 4 * nq + 2]
        dq_ref, dk_ref, dv_ref, dwq_ref, dwk_ref = refs[n_in:n_in + 5]
        sc = refs[n_in + 5:]
        dq_s, dk_s, dv_s = sc[:nq], sc[nq:nq + nkv], sc[nq + nkv:nq + 2 * nkv]
        ck, cv = sc[nq + 2 * nkv:nq + 3 * nkv], sc[nq + 3 * nkv:]
        pp = pl.program_id(0)
        n = pl.program_id(1)

        @pl.when((pp == 0) & (n == 0))
        def _():
            dwq_ref[...] = dwq0_ref[...]
            dwk_ref[...] = dwk0_ref[...]

        @pl.when(n == 0)
        def _():
            for c in (*ck, *cv):
                c[...] = jnp.zeros_like(c)

        @pl.when(n < nb)
        def _():
            valid, negd = _band_mask(n, d, max_dist)
            slope = jnp.stack([sl_ref[0, :, j * 64:j * 64 + 1] for j in range(hps)])
            wqv, wkv = wq_ref[...], wk_ref[...]
            hs = range(hps)

            def residue(r, carry):
                rows = _rows(r, d)
                kn_f, kh, rk = _rms64(_pairs2(kp, kc, rows), wkv)
                kn = kn_f.astype(bf16)
                v = _pairs2(vp, vc, rows).astype(bf16)
                if gqa:
                    kn = jnp.stack([kn[kv_of(j)] for j in hs])
                    v = jnp.stack([v[kv_of(j)] for j in hs])
                qn_f, qh, rq = _rms64(_pairs(q_refs, rows), wqv)
                qn = qn_f.astype(bf16)
                s = _dot(qn, kn, BNT) * (HEAD_DIM ** -0.5) + slope * negd
                p = jnp.where(valid, jnp.exp(s - _pairs(lse_refs, rows)[:, :, :1]), 0.0)
                do_h = _pairs(do_refs, rows)
                delta = jnp.sum(do_h * _pairs(o_refs, rows), axis=-1, keepdims=True)
                do_b = do_h.astype(bf16)
                dp = _dot(do_b, v, BNT)
                ds = (p * (dp - delta + _pairs(dlse_refs, rows)[:, :, :1])).astype(bf16)
                dqn = _dot(ds, kn, BNN) * (HEAD_DIM ** -0.5)
                dkn = _dot(ds, qn, BTN) * (HEAD_DIM ** -0.5)
                dvv = _dot(p.astype(bf16), do_b, BTN)
                if gqa:
                    grp = lambda t: jnp.stack([sum(t[j] for j in hs if kv_of(j) == h) for h in range(nkh)])
                    dkn, dvv = grp(dkn), grp(dvv)
                dq, dwq = _rms64_bwd(dqn, qh, rq, wqv)
                dk, dwk = _rms64_bwd(dkn, kh, rk, wkv)
                for i in range(nq):
                    dq_s[i][rows, :] = _lane_pair(dq, i)
                for i in range(nkv):
                    dk_s[i][rows, :] = ck[i][rows, :] + _lane_pair(dk[:, :BLOCK], i)
                    dv_s[i][rows, :] = cv[i][rows, :] + _lane_pair(dvv[:, :BLOCK], i)
                    ck[i][rows, :] = _lane_pair(dk[:, BLOCK:], i)
                    cv[i][rows, :] = _lane_pair(dvv[:, BLOCK:], i)
                return carry[0] + dwq, carry[1] + dwk

            zero = jnp.zeros((1, HEAD_DIM), f32)
            dwq_a, dwk_a = lax.fori_loop(0, d, residue, (zero, zero))
            dwq_ref[...] += dwq_a
            dwk_ref[...] += dwk_a
            for i in range(nq):
                dq_ref[:, i * 128:(i + 1) * 128] = dq_s[i][...]
            for i in range(nkv):
                dk_ref[:, i * 128:(i + 1) * 128] = dk_s[i][...]
                dv_ref[:, i * 128:(i + 1) * 128] = dv_s[i][...]

        @pl.when(n == nb)
        def _():
            for i in range(nkv):
                dk_ref[:, i * 128:(i + 1) * 128] = ck[i][...]
                dv_ref[:, i * 128:(i + 1) * 128] = cv[i][...]

    cl = lambda n: jnp.minimum(n, nb - 1)
    pv = lambda n: jnp.maximum(jnp.minimum(n, nb - 1) - 1, 0)
    col = lambda c0, i: (lambda p, n: (cl(n), c0 + p * nq + i))
    prv = lambda c0, i: (lambda p, n: (pv(n), c0 + p * nq + i))
    blk = lambda f: pl.BlockSpec((R, 128), f)
    w64 = pl.BlockSpec((1, 64), lambda p, n: (0, 0))
    in_specs = [blk(col(plan.q0, i)) for i in range(nq)]
    in_specs += [blk(prv(plan.k0, i)) for i in range(nkv)] + [blk(col(plan.k0, i)) for i in range(nkv)]
    in_specs += [blk(prv(plan.v0, i)) for i in range(nkv)] + [blk(col(plan.v0, i)) for i in range(nkv)]
    in_specs += [pl.BlockSpec((1, 1, hps * 64), lambda p, n: (p, 0, 0)), w64, w64]
    in_specs += [blk(col(0, i)) for i in range(nq)] * 4 + [w64, w64]
    kvw = 128 * nkv
    out_specs = [pl.BlockSpec((R, 128 * nq), lambda p, n: (cl(n), p)),
                 pl.BlockSpec((R, kvw), lambda p, n: (jnp.maximum(n - 1, 0), p)),
                 pl.BlockSpec((R, kvw), lambda p, n: (jnp.maximum(n - 1, 0), p)), w64, w64]
    same = lambda a: [a] * nq
    return _ride_call(
        body, coll, ride, grid=(P, nb + 1), in_specs=in_specs, out_specs=out_specs,
        out_shape=[jax.ShapeDtypeStruct((L, 512), f32), jax.ShapeDtypeStruct((L, kvw * P), f32),
                   jax.ShapeDtypeStruct((L, kvw * P), f32), jax.ShapeDtypeStruct((1, 64), f32),
                   jax.ShapeDtypeStruct((1, 64), f32)],
        scratch_shapes=[pltpu.VMEM((R, 128), f32)] * (nq + 4 * nkv),
        semantics=("arbitrary", "arbitrary"), name=name,
        args=[hin] * (nq + 4 * nkv) + [plan.slopes, wq, wk, *same(o), *same(lse), *same(do), *same(dlse), *dw0])


def _head_sum(x):
    c = x.shape[1]
    r = lax.broadcasted_iota(jnp.int32, (c, c), 0) // HEAD_DIM
    q = lax.broadcasted_iota(jnp.int32, (c, c), 1) // HEAD_DIM
    ones = (r == q).astype(bf16)
    hi, lo = _split(x)
    return _dot(hi, ones) + _dot(lo, ones)


def attn_merge_fwd(oa, la, obs, lbs, sinkb, name):
    L = oa.shape[0]

    def fn(i, nb, oa_t, la_t, o1, o2, o3, l1, l2, l3, sk):
        ya = oa_t * jax.nn.sigmoid(la_t - sk)
        m = jnp.maximum(jnp.maximum(l1, l2), l3)
        e1, e2, e3 = jnp.exp(l1 - m), jnp.exp(l2 - m), jnp.exp(l3 - m)
        yb = (e1 * o1 + e2 * o2 + e3 * o3) / (e1 + e2 + e3)
        return jnp.concatenate([ya, yb], axis=1)

    rows = [(a, 0, 512, "cur") for a in (oa, la, *obs, *lbs)]
    return rowwise(fn, name=name, L=L, tm=_pick(L, 256, SUBLANE), rows=rows, consts=[sinkb], outs=[(1024, bf16)])


def attn_merge_bwd(dcat, oa, la, obs, lbs, sinkb, name):
    L = oa.shape[0]

    def fn(i, nb, da, db, oa_t, la_t, o1, o2, o3, l1, l2, l3, sk):
        keep = jax.nn.sigmoid(la_t - sk)
        dla = _head_sum(da * oa_t) * keep * (1.0 - keep)
        m = jnp.maximum(jnp.maximum(l1, l2), l3)
        e1, e2, e3 = jnp.exp(l1 - m), jnp.exp(l2 - m), jnp.exp(l3 - m)
        z = e1 + e2 + e3
        w1, w2, w3 = e1 / z, e2 / z, e3 / z
        g1, g2, g3 = _head_sum(db * o1), _head_sum(db * o2), _head_sum(db * o3)
        gm = w1 * g1 + w2 * g2 + w3 * g3
        return (da * keep, dla, w1 * db, w2 * db, w3 * db,
                w1 * (g1 - gm), w2 * (g2 - gm), w3 * (g3 - gm), -_colsum(dla))

    rows = [(dcat, 0, 512, "cur"), (dcat, 512, 512, "cur")] + [(a, 0, 512, "cur") for a in (oa, la, *obs, *lbs)]
    return rowwise(fn, name=name, L=L, tm=_pick(L, 256, SUBLANE), rows=rows, consts=[sinkb],
                   outs=[(512, f32)] * 8, sums=[(1, 512)])


def attn_assemble(dqa, dka, dva, dqs, dks, dvs, name):
    L = dqa.shape[0]

    def fn(i, nb, qa, ka, va, q1, q2, q3, k1, k2, k3, v1, v2, v3):
        return jnp.concatenate([qa, ka, va, q1 + q2 + q3, k1 + k2 + k3, v1 + v2 + v3], axis=1)

    rows = [(dqa, 0, 512, "cur"), (dka, 0, 128, "cur"), (dva, 0, 128, "cur")]
    rows += [(a, 0, 512, "cur") for a in (*dqs, *dks, *dvs)]
    return rowwise(fn, name=name, L=L, tm=_pick(L, 256, SUBLANE), rows=rows, outs=[(ATTN_IN, bf16)])


def attention_fwd(hin, wqa, wka, wqb, wkb, sinkb, rides=None):
    rides = rides or {}
    oa, la = attn2_fwd(hin, _Plan(1, True, 4), wqa, wka, "attn_a_fwd")
    obs, lbs, gots = [], [], {}
    for _, d in B_BRANCHES:
        res = attn2_fwd(hin, _Plan(d, False, 4 if d < 16 else 2), wqb, wkb, f"attn_b{d}_fwd", ride=rides.get(d))
        obs.append(res[0])
        lbs.append(res[1])
        if d in rides:
            gots[d] = res[2]
    ocat = attn_merge_fwd(oa, la, obs, lbs, sinkb, "attn_merge_fwd")
    return ocat, (oa, la, obs, lbs), gots


def attention_bwd(hin, wqa, wka, wqb, wkb, sinkb, saved, dcat, rides=None):
    rides = dict(rides or {})
    gots = {}

    def ride_of(key):
        r = rides.get(key)
        return r(gots) if callable(r) else r

    oa, la, obs, lbs = saved
    res = attn_merge_bwd(dcat, oa, la, obs, lbs, sinkb, "attn_merge_bwd")
    doa, dla, dos, dls, dsink = res[0], res[1], res[2:5], res[5:8], res[8]
    zero = jnp.zeros((1, 64), f32)
    res = attn2_bwd(hin, _Plan(1, True, 4), wqa, wka, oa, la, doa, dla, (zero, zero), "attn_a_bwd", ride=ride_of("a"))
    dqa, dka, dva, dwqa, dwka = res[:5]
    if "a" in rides:
        gots["a"] = res[5]
    dqs, dks, dvs = [], [], []
    dwqb = dwkb = zero
    for g, (_, d) in enumerate(B_BRANCHES):
        res = attn2_bwd(hin, _Plan(d, False, 4 if d < 16 else 1), wqb, wkb, obs[g], lbs[g],
                        dos[g], dls[g], (dwqb, dwkb), f"attn_b{d}_bwd", ride=ride_of(d))
        dq, dk, dv, dwqb, dwkb = res[:5]
        if d in rides:
            gots[d] = res[5]
        dqs.append(dq)
        dks.append(dk)
        dvs.append(dv)
    dhin = attn_assemble(dqa, dka, dva, dqs, dks, dvs, "attn_assemble")
    return dhin, dwqa, dwka, dwqb, dwkb, dsink, gots


NS = S5_GROUPS * S5_STATE


def _s5_param_fn(lr, li, ldt):
    dt = jnp.exp(ldt)
    mag, ang = jnp.exp(lr * dt), li * dt
    ab_re, ab_im = mag * jnp.cos(ang), mag * jnp.sin(ang)
    nr, ni = ab_re - 1.0, ab_im
    den = lr * lr + li * li
    return ab_re, ab_im, (nr * lr + ni * li) / den, (ni * lr - nr * li) / den


def s5_params_fwd(lr, li, ldt):
    def body(lr_ref, li_ref, ldt_ref, *outs):
        for o_ref, o in zip(outs, _s5_param_fn(lr_ref[...], li_ref[...], ldt_ref[...])):
            o_ref[...] = o

    return pl.pallas_call(body, out_shape=[jax.ShapeDtypeStruct(lr.shape, f32)] * 4, name="s5_params_fwd")(lr, li, ldt)


def s5_params_bwd(lr, li, ldt, cts):
    def body(lr_ref, li_ref, ldt_ref, c0, c1, c2, c3, dlr, dli, dldt):
        _, vjp = jax.vjp(_s5_param_fn, lr_ref[...], li_ref[...], ldt_ref[...])
        a, b, c = vjp((c0[...], c1[...], c2[...], c3[...]))
        dlr[...] = a
        dli[...] = b
        dldt[...] = c

    return pl.pallas_call(
        body, out_shape=[jax.ShapeDtypeStruct(lr.shape, f32), jax.ShapeDtypeStruct(li.shape, f32),
                         jax.ShapeDtypeStruct(ldt.shape, f32)], name="s5_params_bwd")(lr, li, ldt, *cts)


def _cmul(ar, ai, br, bi):
    return ar * br - ai * bi, ar * bi + ai * br


def s5_scan(z, ab_re, ab_im, f_re, f_im, *, reverse, name):
    L = z.shape[0]
    tm = _pick(L, 256, SUBLANE)
    nb = L // tm
    ng = tm // SUBLANE
    use_f = f_re is not None
    consts = [ab_re, ab_im] + ([f_re, f_im] if use_f else [])

    def body(*refs):
        z_ref = refs[0]
        c_refs = refs[1:1 + len(consts)]
        x_ref, car = refs[1 + len(consts)], refs[2 + len(consts)]
        i = pl.program_id(0)

        @pl.when(i == 0)
        def _():
            car[...] = jnp.zeros_like(car)

        a1 = (c_refs[0][...], c_refs[1][...])
        a2 = _cmul(*a1, *a1)
        a3 = _cmul(*a2, *a1)
        a4 = _cmul(*a2, *a2)
        pw = [a1, a2, a3, a4, _cmul(*a4, *a1), _cmul(*a4, *a2), _cmul(*a4, *a3), _cmul(*a4, *a4)]
        if reverse:
            pw = pw[::-1]
        pw_re = _stack_rows([p[0] for p in pw])
        pw_im = _stack_rows([p[1] for p in pw])
        ridx = lax.broadcasted_iota(jnp.int32, (SUBLANE, NS), 0)
        if use_f:
            fr, fi = c_refs[2][...], c_refs[3][...]

        def group(s, carry):
            cr, ci = carry
            g = (ng - 1 - s) if reverse else s
            r0 = pl.multiple_of(g * SUBLANE, SUBLANE)
            xr = z_ref[pl.ds(r0, SUBLANE), 0:NS]
            xi = z_ref[pl.ds(r0, SUBLANE), NS:2 * NS]
            if use_f:
                xr, xi = _cmul(fr, fi, xr, xi)
            for sft, (pr, pi) in ((1, a1), (2, a2), (4, a4)):
                if reverse:
                    keep = ridx < SUBLANE - sft
                    sr = jnp.where(keep, pltpu.roll(xr, SUBLANE - sft, 0), 0.0)
                    si = jnp.where(keep, pltpu.roll(xi, SUBLANE - sft, 0), 0.0)
                else:
                    keep = ridx >= sft
                    sr = jnp.where(keep, pltpu.roll(xr, sft, 0), 0.0)
                    si = jnp.where(keep, pltpu.roll(xi, sft, 0), 0.0)
                tr, ti = _cmul(pr, pi, sr, si)
                xr, xi = xr + tr, xi + ti
            tr, ti = _cmul(pw_re, pw_im, cr, ci)
            xr, xi = xr + tr, xi + ti
            x_ref[pl.ds(r0, SUBLANE), 0:NS] = xr
            x_ref[pl.ds(r0, SUBLANE), NS:2 * NS] = xi
            row = 0 if reverse else SUBLANE - 1
            return xr[row:row + 1, :], xi[row:row + 1, :]

        cr, ci = lax.fori_loop(0, ng, group, (car[0:1, 0:NS], car[0:1, NS:2 * NS]))
        car[0:1, 0:NS] = cr
        car[0:1, NS:2 * NS] = ci

    blk = (lambda i: (nb - 1 - i, 0)) if reverse else (lambda i: (i, 0))
    return pl.pallas_call(
        body, grid=(nb,),
        in_specs=[pl.BlockSpec((tm, 2 * NS), blk)] + [pl.BlockSpec((1, NS), lambda i: (0, 0))] * len(consts),
        out_specs=pl.BlockSpec((tm, 2 * NS), blk),
        out_shape=jax.ShapeDtypeStruct((L, 2 * NS), f32),
        scratch_shapes=[pltpu.VMEM((SUBLANE, 2 * NS), f32)],
        compiler_params=_cp("arbitrary"), name=name,
    )(z, *consts)


def _s5_post_fn(ypre, u, dvec, gw, gb):
    y = ypre + dvec * u
    g = jax.nn.gelu(y)
    z = _dot(g.astype(bf16), gw.astype(bf16)) + gb
    return g * jax.nn.sigmoid(z)


def s5_post_fwd(ypre, hin, dvec, gw, gb):
    L = ypre.shape[0]

    def fn(i, nb, yt, ut, dv, gwv, gbv):
        return _s5_post_fn(yt, ut, dv, gwv, gbv)

    return rowwise(fn, name="s5_post_fwd", L=L, tm=_pick(L, 512, SUBLANE),
                   rows=[(ypre, 0, S5_WIDTH, "cur"), (hin, 3072, S5_WIDTH, "cur")],
                   consts=[dvec, gw, gb], outs=[(S5_WIDTH, f32)])


def s5_post_bwd(ypre, hin, dvec, gw, gb, dycat):
    L = ypre.shape[0]

    def fn(i, nb, yt, ut, dyt, dv, gwv, gbv):
        _, vjp = jax.vjp(_s5_post_fn, yt, ut, dv, gwv, gbv)
        return vjp(dyt)

    return rowwise(fn, name="s5_post_bwd", L=L, tm=_pick(L, 512, SUBLANE),
                   rows=[(ypre, 0, S5_WIDTH, "cur"), (hin, 3072, S5_WIDTH, "cur"), (dycat, 0, S5_WIDTH, "cur")],
                   consts=[dvec, gw, gb], outs=[(S5_WIDTH, f32)] * 2,
                   sums=[(1, S5_WIDTH), (S5_WIDTH, S5_WIDTH), (1, S5_WIDTH)])


def s5_acc(G, X, bu, f_re, f_im):
    L = G.shape[0]

    def fn(i, nb, g, x, b, xp8, fr, fi):
        gr, gi = g[:, :NS], g[:, NS:]
        xp = _shift_down(x, xp8 * (i > 0).astype(f32), 1)
        xr, xi = xp[:, :NS], xp[:, NS:]
        br, bi = b[:, :NS], b[:, NS:]
        dbu = jnp.concatenate([fr * gr + fi * gi, fr * gi - fi * gr], axis=1)
        return (dbu, _colsum(xr * gr + xi * gi), _colsum(xr * gi - xi * gr),
                _colsum(br * gr + bi * gi), _colsum(br * gi - bi * gr))

    return rowwise(fn, name="s5_acc", L=L, tm=_pick(L, 256, SUBLANE),
                   rows=[(G, 0, 2 * NS, "cur"), (X, 0, 2 * NS, "cur"), (bu, 0, 2 * NS, "cur"), (X, 0, 2 * NS, "prev")],
                   consts=[f_re, f_im], outs=[(2 * NS, bf16)], sums=[(1, NS)] * 4)


def _s5_blockdiag(b_re, b_im, c_re, c_im):
    eye = jnp.eye(S5_GROUPS, dtype=f32)
    bb = lambda b: jnp.einsum("gpi,gh->gihp", b, eye).reshape(S5_WIDTH, NS)
    cc = lambda c: jnp.einsum("gip,gh->gphi", c, eye).reshape(NS, S5_WIDTH)
    return jnp.concatenate([bb(b_re), bb(b_im)], axis=1), jnp.concatenate([cc(c_re), -cc(c_im)], axis=0)


def _s5_blockdiag_grads(dB, dC):
    gb = lambda m: jnp.einsum("gigp->gpi", m.reshape(S5_GROUPS, S5_GROUP, S5_GROUPS, S5_STATE))
    gc = lambda m: jnp.einsum("gpgi->gip", m.reshape(S5_GROUPS, S5_STATE, S5_GROUPS, S5_GROUP))
    return gb(dB[:, :NS]), gb(dB[:, NS:]), gc(dC[:NS]), -gc(dC[NS:])


def s5_fwd(hin, prm):
    ab_re, ab_im, f_re, f_im = s5_params_fwd(prm["lr"], prm["li"], prm["ldt"])
    flat = lambda a: a.reshape(1, NS)
    ab_re, ab_im, f_re, f_im = flat(ab_re), flat(ab_im), flat(f_re), flat(f_im)
    Bblk, Cblk = _s5_blockdiag(prm["b_re"], prm["b_im"], prm["c_re"], prm["c_im"])
    bu = mm(hin, Bblk, name="s5_bu", a_win=(3072, S5_WIDTH))
    X = s5_scan(bu, ab_re, ab_im, f_re, f_im, reverse=False, name="s5_scan_fwd")
    ypre = mm(X, Cblk, name="s5_y")
    yc = s5_post_fwd(ypre, hin, prm["d"], prm["gw"], prm["gb"])
    return yc, (ab_re, ab_im, f_re, f_im, Bblk, Cblk, bu, X, ypre)


def s5_bwd(hin, prm, saved, dycat):
    ab_re, ab_im, f_re, f_im, Bblk, Cblk, bu, X, ypre = saved
    dypre, du_skip, dd, dgw, dgb = s5_post_bwd(ypre, hin, prm["d"], prm["gw"], prm["gb"], dycat)
    dX = mm(dypre, Cblk, tb=True, name="s5_dx")
    dC = mm(X, dypre, ta=True, name="s5_dc")
    G = s5_scan(dX, ab_re, -ab_im, None, None, reverse=True, name="s5_scan_bwd")
    dbu, dar, dai, dfr, dfi = s5_acc(G, X, bu, f_re, f_im)
    dB = mm(hin, dbu, ta=True, a_win=(3072, S5_WIDTH), name="s5_db")
    du_b = mm(dbu, Bblk, tb=True, name="s5_du")
    sh = prm["lr"].shape
    dlr, dli, dldt = s5_params_bwd(prm["lr"], prm["li"], prm["ldt"],
                                   [a.reshape(sh) for a in (dar, dai, dfr, dfi)])
    db_re, db_im, dc_re, dc_im = _s5_blockdiag_grads(dB, dC)
    grads = dict(lr=dlr, li=dli, ldt=dldt, b_re=db_re, b_im=db_im, c_re=dc_re, c_im=dc_im, d=dd, gw=dgw, gb=dgb)
    return du_skip, du_b, grads


DN_W = DN_HEADS * DN_DK
QKV_W = 3 * DN_W


def _softplus(x):
    return jnp.maximum(x, 0.0) + jnp.log(1.0 + jnp.exp(-jnp.abs(x)))


def _dn_pre(c, ab, alog, dtb):
    s = _silu(c)
    parts = []
    for h in range(2 * DN_HEADS):
        sh = s[:, h * 128:(h + 1) * 128]
        scale = DN_DK ** -0.5 if h < DN_HEADS else 1.0
        parts.append(sh * (lax.rsqrt(jnp.sum(sh * sh, axis=-1, keepdims=True) + EPS) * scale))
    parts.append(s[:, 2 * DN_W:])
    g = -jnp.exp(alog) * _softplus(ab[:, :128] + dtb)
    beta = jax.nn.sigmoid(ab[:, 128:])
    return jnp.concatenate(parts, axis=1), jnp.concatenate([g, beta], axis=1)


def _dn_pre_bwd(c, ab, alog, dtb, dqkv, dgb):
    sg = jax.nn.sigmoid(c)
    s = c * sg
    parts = []
    for h in range(2 * DN_HEADS):
        sh = s[:, h * 128:(h + 1) * 128]
        dy = dqkv[:, h * 128:(h + 1) * 128]
        scale = DN_DK ** -0.5 if h < DN_HEADS else 1.0
        r = lax.rsqrt(jnp.sum(sh * sh, axis=-1, keepdims=True) + EPS)
        parts.append(scale * r * (dy - sh * (r * r) * jnp.sum(dy * sh, axis=-1, keepdims=True)))
    parts.append(dqkv[:, 2 * DN_W:])
    dc = jnp.concatenate(parts, axis=1) * (sg * (1.0 + c * (1.0 - sg)))
    pre = ab[:, :128] + dtb
    ea = jnp.exp(alog)
    dg = dgb[:, :128]
    da = dg * (-ea) * jax.nn.sigmoid(pre)
    dalog = _colsum(dg * (-ea) * _softplus(pre))
    beta = jax.nn.sigmoid(ab[:, 128:])
    db = dgb[:, 128:] * beta * (1.0 - beta)
    return dc, jnp.concatenate([da, db], axis=1), dalog, _colsum(da)


def dn_pre_fwd(hin, conv_w, alog, dtb):
    L = hin.shape[0]

    def fn(i, nb, x, ab, p8, w, al, db):
        c = _conv_causal(x, p8 * (i > 0).astype(f32), w)
        return _dn_pre(c, ab, al, db)

    return rowwise(fn, name="dn_pre_fwd", L=L, tm=_pick(L, 256, SUBLANE),
                   rows=[(hin, 0, QKV_W, "cur"), (hin, 3328, 256, "cur"), (hin, 0, QKV_W, "prev")],
                   consts=[conv_w, alog, dtb], outs=[(QKV_W, f32), (256, f32)])


def dn_pre_bwd(hin, conv_w, alog, dtb, dqkv3, dg, dbeta):
    L = hin.shape[0]
    W = conv_w.shape[0]

    def fn(i, nb, x, ab, dq, dk, dv, dgt, dbt, p8, xn, dqn, dkn, dvn, w, al, db):
        tm, ext = x.shape[0], xn.shape[0]
        more = (i < nb - 1).astype(f32)
        p8 = p8 * (i > 0).astype(f32)
        c = _conv_causal(jnp.concatenate([x, xn * more], axis=0), p8, w)
        dqkv = jnp.concatenate([jnp.concatenate([dq, dk, dv], axis=1),
                                jnp.concatenate([dqn, dkn, dvn], axis=1) * more], axis=0)
        zpad = lambda t: jnp.concatenate([t, jnp.zeros((ext, t.shape[1]), f32)], axis=0)
        dc, dab, dalog, ddtb = _dn_pre_bwd(c, zpad(ab), al, db, dqkv, zpad(jnp.concatenate([dgt, dbt], axis=1)))
        return _conv_causal_bwd_x(dc, tm, w), dab[:tm], _conv_causal_bwd_w(dc[:tm], x, p8, W), dalog, ddtb

    rows = [(hin, 0, QKV_W, "cur"), (hin, 3328, 256, "cur")] + [(a, 0, DN_W, "cur") for a in dqkv3]
    rows += [(dg, 0, 128, "cur"), (dbeta, 0, 128, "cur"), (hin, 0, QKV_W, "prev"), (hin, 0, QKV_W, "next")]
    rows += [(a, 0, DN_W, "next") for a in dqkv3]
    return rowwise(fn, name="dn_pre_bwd", L=L, tm=_pick(L, 128, SUBLANE), rows=rows,
                   consts=[conv_w, alog, dtb], outs=[(QKV_W, bf16), (256, f32)],
                   sums=[(SUBLANE, QKV_W), (1, 128), (1, 128)])


def _split(a):
    hi = a.astype(bf16)
    return hi, (a - hi.astype(f32)).astype(bf16)


def _dot3_raw(a, b, dims):
    ah, al = _split(a)
    bh, bl = _split(b)
    return _dot(ah, bh, dims) + (_dot(ah, bl, dims) + _dot(al, bh, dims))


@functools.partial(jax.custom_vjp, nondiff_argnums=(2,))
def _dot3(a, b, dims=NN):
    return _dot3_raw(a, b, dims)


def _dot3_fwd(a, b, dims):
    return _dot3_raw(a, b, dims), (a, b)


BNN = (((2,), (1,)), ((0,), (0,)))
BNT = (((2,), (2,)), ((0,), (0,)))
BTN = (((1,), (1,)), ((0,), (0,)))


def _dot_bwd(raw, dims, res, g):
    a, b = res
    nn, nt, tn = (BNN, BNT, BTN) if dims[1][0] else (NN, NT, TN)
    if dims == nn:
        return raw(g, b, nt), raw(a, g, tn)
    if dims == nt:
        return raw(g, b, nn), raw(g, a, tn)
    assert dims == tn
    return raw(b, g, nt), raw(a, g, nn)


_dot3.defvjp(_dot3_fwd, functools.partial(_dot_bwd, _dot3_raw))


def _dot1_raw(a, b, dims):
    return _dot(a.astype(bf16), b.astype(bf16), dims)


@functools.partial(jax.custom_vjp, nondiff_argnums=(2,))
def _dot1(a, b, dims=NN):
    return _dot1_raw(a, b, dims)


_dot1.defvjp(lambda a, b, dims: (_dot1_raw(a, b, dims), (a, b)), functools.partial(_dot_bwd, _dot1_raw))


def _unit_lower_inverse(nmat):
    C = nmat.shape[-1]
    eye = (lax.broadcasted_iota(jnp.int32, (C, C), 0) == lax.broadcasted_iota(jnp.int32, (C, C), 1)).astype(f32)
    T = eye - nmat
    Pw = _dot3(nmat, nmat, BNN)
    for step in range(5):
        T = T + _dot3(T, Pw, BNN)
        if step < 4:
            Pw = _dot3(Pw, Pw, BNN)
    return T


@jax.custom_vjp
def _inverse_known(nmat, T):
    return T


def _inverse_known_bwd(T, g):
    return -_dot3(_dot3(T, g, BTN), T, BNT), jnp.zeros_like(T)


_inverse_known.defvjp(lambda nmat, T: (T, T), _inverse_known_bwd)


def _dn_chunk(q, k, v, gcol, bcol, S, T_known=None):
    C = q.shape[1]
    r = lax.broadcasted_iota(jnp.int32, (C, C), 0)
    c = lax.broadcasted_iota(jnp.int32, (C, C), 1)
    tril = (r >= c).astype(f32)
    strict = (r > c).astype(f32)
    eye = (r == c).astype(f32)
    hd = _dot3
    grow = jnp.sum(eye * gcol, axis=1, keepdims=True)
    Gcol = jnp.sum(tril * grow, axis=2, keepdims=True)
    Grow = jnp.sum(eye * Gcol, axis=1, keepdims=True)
    gamma = jnp.exp((Gcol - Grow) * tril) * tril
    ld = _dot1
    nmat = strict * bcol * ld(k, k, BNT) * gamma
    T = _unit_lower_inverse(nmat) if T_known is None else _inverse_known(nmat, T_known)
    eG = jnp.exp(Gcol)
    u = hd(T, bcol * v, BNN)
    w = hd(T, (bcol * eG) * k, BNN)
    qk = ld(q, k, BNT) * gamma
    vnew = u - ld(w, S, BNN)
    o = ld(q * eG, S, BNN) + ld(qk, vnew, BNN)
    Glast = jnp.sum(gcol, axis=1, keepdims=True)
    S2 = S * jnp.exp(Glast) + ld(k * jnp.exp(Glast - Gcol), vnew, BTN)
    return o, S2, T


def _heads(x_ref):
    return jnp.stack([x_ref[:, h * 128:(h + 1) * 128] for h in range(DN_HEADS)])


def _head_cols(g_ref):
    return jnp.stack([g_ref[:, h:h + 1] for h in range(DN_HEADS)])


def dn_chunks_fwd(qkvn, gb):
    L = qkvn.shape[0]
    C = DN_CHUNK
    nc = L // C

    def body(q_ref, k_ref, v_ref, g_ref, b_ref, o_ref, sin_ref, t_ref, S):
        n = pl.program_id(0)

        @pl.when(n == 0)
        def _():
            S[...] = jnp.zeros_like(S)

        s_in = S[...]
        sin_ref[...] = s_in
        o, s2, t = _dn_chunk(_heads(q_ref), _heads(k_ref), _heads(v_ref), _head_cols(g_ref), _head_cols(b_ref), s_in)
        for h in range(DN_HEADS):
            o_ref[:, h * 128:(h + 1) * 128] = o[h]
        t_ref[...] = t
        S[...] = s2

    blk = lambda j: pl.BlockSpec((C, DN_W), lambda n, j=j: (n, j))
    gblk = lambda j: pl.BlockSpec((C, 128), lambda n, j=j: (n, j))
    return pl.pallas_call(
        body, grid=(nc,),
        in_specs=[blk(0), blk(1), blk(2), gblk(0), gblk(1)],
        out_specs=[pl.BlockSpec((C, DN_W), lambda n: (n, 0)),
                   pl.BlockSpec((DN_HEADS, None, 128, 128), lambda n: (0, n, 0, 0)),
                   pl.BlockSpec((DN_HEADS, None, C, C), lambda n: (0, n, 0, 0))],
        out_shape=[jax.ShapeDtypeStruct((L, DN_W), f32), jax.ShapeDtypeStruct((DN_HEADS, nc, 128, 128), f32),
                   jax.ShapeDtypeStruct((DN_HEADS, nc, C, C), f32)],
        scratch_shapes=[pltpu.VMEM((DN_HEADS, 128, 128), f32)],
        compiler_params=_cp("arbitrary"), name="dn_chunks_fwd",
    )(qkvn, qkvn, qkvn, gb, gb)


def dn_chunks_bwd(qkvn, gb, s_in, t_inv, do):
    L = qkvn.shape[0]
    C = DN_CHUNK
    nc = L // C

    def body(q_ref, k_ref, v_ref, g_ref, b_ref, sin_ref, t_ref, do_ref, dq_ref, dk_ref, dv_ref, dg_ref, db_ref, dS):
        n = pl.program_id(0)

        @pl.when(n == 0)
        def _():
            dS[...] = jnp.zeros_like(dS)

        args = (_heads(q_ref), _heads(k_ref), _heads(v_ref), _head_cols(g_ref), _head_cols(b_ref), sin_ref[...])
        t_known = t_ref[...]
        _, vjp = jax.vjp(lambda *a: _dn_chunk(*a, T_known=t_known)[:2], *args)
        dq, dk, dv, dg, db, ds = vjp((_heads(do_ref), dS[...]))
        lane = lax.broadcasted_iota(jnp.int32, (C, 128), 1)
        dg_all = jnp.zeros((C, 128), f32)
        db_all = jnp.zeros((C, 128), f32)
        for h in range(DN_HEADS):
            sl = slice(h * 128, (h + 1) * 128)
            dq_ref[:, sl] = dq[h]
            dk_ref[:, sl] = dk[h]
            dv_ref[:, sl] = dv[h]
            dg_all = dg_all + jnp.where(lane == h, dg[h], 0.0)
            db_all = db_all + jnp.where(lane == h, db[h], 0.0)
        dS[...] = ds
        dg_ref[...] = dg_all
        db_ref[...] = db_all

    rv = lambda n: nc - 1 - n
    blk = lambda j: pl.BlockSpec((C, DN_W), lambda n, j=j: (rv(n), j))
    gblk = lambda j: pl.BlockSpec((C, 128), lambda n, j=j: (rv(n), j))
    oblk = pl.BlockSpec((C, DN_W), lambda n: (rv(n), 0))
    gout = pl.BlockSpec((C, 128), lambda n: (rv(n), 0))
    return pl.pallas_call(
        body, grid=(nc,),
        in_specs=[blk(0), blk(1), blk(2), gblk(0), gblk(1),
                  pl.BlockSpec((DN_HEADS, None, 128, 128), lambda n: (0, rv(n), 0, 0)),
                  pl.BlockSpec((DN_HEADS, None, C, C), lambda n: (0, rv(n), 0, 0)), oblk],
        out_specs=[oblk] * 3 + [gout] * 2,
        out_shape=[jax.ShapeDtypeStruct((L, DN_W), f32)] * 3 + [jax.ShapeDtypeStruct((L, 128), f32)] * 2,
        scratch_shapes=[pltpu.VMEM((DN_HEADS, 128, 128), f32)],
        compiler_params=_cp("arbitrary"), name="dn_chunks_bwd",
    )(qkvn, qkvn, qkvn, gb, gb, s_in, t_inv, do)


def _dn_post(o, z, w):
    parts = []
    for h in range(DN_HEADS):
        oh = o[:, h * 128:(h + 1) * 128]
        r = lax.rsqrt(jnp.mean(oh * oh, axis=-1, keepdims=True) + EPS)
        parts.append(oh * r * w)
    return jnp.concatenate(parts, axis=1) * _silu(z)


def dn_post_fwd(o, hin, yc, onorm):
    L = o.shape[0]

    def fn(i, nb, ot, zt, yct, w):
        return jnp.concatenate([yct, _dn_post(ot, zt, w)], axis=1)

    return rowwise(fn, name="dn_post_fwd", L=L, tm=_pick(L, 256, SUBLANE),
                   rows=[(o, 0, DN_W, "cur"), (hin, 2304, DN_W, "cur"), (yc, 0, S5_WIDTH, "cur")],
                   consts=[onorm], outs=[(1024, bf16)])


def dn_post_bwd(o, hin, onorm, dycat):
    L = o.shape[0]

    def fn(i, nb, ot, zt, d0, d1, d2, w):
        dy = jnp.concatenate([d0, d1, d2], axis=1)
        sg = jax.nn.sigmoid(zt)
        sz = zt * sg
        dos, dw = [], jnp.zeros((1, 128), f32)
        nrm = []
        for h in range(DN_HEADS):
            sl = slice(h * 128, (h + 1) * 128)
            oh = ot[:, sl]
            r = lax.rsqrt(jnp.mean(oh * oh, axis=-1, keepdims=True) + EPS)
            ohat = oh * r
            t = dy[:, sl] * sz[:, sl]
            dw = dw + _colsum(t * ohat)
            t = t * w
            dos.append(r * (t - ohat * jnp.mean(t * ohat, axis=-1, keepdims=True)))
            nrm.append(ohat * w)
        dz = dy * jnp.concatenate(nrm, axis=1) * (sg * (1.0 + zt * (1.0 - sg)))
        return jnp.concatenate(dos, axis=1), dz, dw

    rows = [(o, 0, DN_W, "cur"), (hin, 2304, DN_W, "cur")] + [(dycat, 256 * (1 + j), 256, "cur") for j in range(3)]
    return rowwise(fn, name="dn_post_bwd", L=L, tm=_pick(L, 256, SUBLANE), rows=rows,
                   consts=[onorm], outs=[(DN_W, f32), (DN_W, f32)], sums=[(1, 128)])


def rec_assemble(dx_qkv, dz, du1, du2, dab):
    L = dz.shape[0]

    def fn(i, nb, a, b, c, d, e):
        return jnp.concatenate([a.astype(f32), b, c + d, e], axis=1)

    return rowwise(fn, name="rec_assemble", L=L, tm=_pick(L, 256, SUBLANE),
                   rows=[(dx_qkv, 0, QKV_W, "cur"), (dz, 0, DN_W, "cur"), (du1, 0, 256, "cur"),
                         (du2, 0, 256, "cur"), (dab, 0, 256, "cur")], outs=[(REC_PAD, bf16)])


def deltanet_fwd(hin, prm, yc):
    qkvn, gb = dn_pre_fwd(hin, prm["conv"], prm["alog"], prm["dtb"])
    o, s_in, t_inv = dn_chunks_fwd(qkvn, gb)
    ycat = dn_post_fwd(o, hin, yc, prm["onorm"])
    return ycat, (qkvn, gb, o, s_in, t_inv)


def deltanet_bwd(hin, prm, saved, dycat):
    qkvn, gb, o, s_in, t_inv = saved
    do, dz, donorm = dn_post_bwd(o, hin, prm["onorm"], dycat)
    dq, dk, dv, dgH, dbH = dn_chunks_bwd(qkvn, gb, s_in, t_inv, do)
    dx_qkv, dab, dconv, dalog, ddtb = dn_pre_bwd(hin, prm["conv"], prm["alog"], prm["dtb"], (dq, dk, dv), dgH, dbH)
    return dx_qkv, dz, dab, dict(conv=dconv[:DN_CONV], alog=dalog, dtb=ddtb, onorm=donorm)


AXES = ("x", "y", "c")


class _Coll:
    def __init__(self, x, axes, mode):
        self.axes, self.mode = axes, mode
        self.P = 2 ** len(axes)
        shape = x.shape if mode == "gather" else x.shape[1:]
        self.out_shape = jax.ShapeDtypeStruct((self.P,) + tuple(shape), x.dtype)
        self.scratch = [pltpu.SemaphoreType.DMA((self.P - 1,)), pltpu.SemaphoreType.DMA((self.P - 1,)),
                        pltpu.SemaphoreType.DMA]

    def _copies(self, x_ref, out_ref, send_sems, recv_sems, local_sem, with_recvs):
        axes, k = self.axes, len(self.axes)
        co = {a: lax.axis_index(a) for a in AXES}
        me = 0
        for a in axes:
            me = me * 2 + co[a]
        src = (lambda j: x_ref) if self.mode == "gather" else (lambda j: x_ref.at[j])
        local = pltpu.make_async_copy(src(me), out_ref.at[me], local_sem)
        sends, recvs = [], []
        for m in range(1, self.P):
            tco = dict(co)
            t = 0
            for i, a in enumerate(axes):
                if (m >> (k - 1 - i)) & 1:
                    tco[a] = 1 - co[a]
                t = t * 2 + tco[a]
            dev = tuple(tco[a] for a in AXES)
            mk = functools.partial(pltpu.make_async_remote_copy, src_ref=src(t), send_sem=send_sems.at[m - 1],
                                   recv_sem=recv_sems.at[m - 1], device_id=dev, device_id_type=MESH)
            sends.append(mk(dst_ref=out_ref.at[me]))
            if with_recvs:
                recvs.append(mk(dst_ref=out_ref.at[t]))
        return local, sends, recvs

    def start(self, *refs):
        local, sends, _ = self._copies(*refs, with_recvs=False)
        local.start()
        for cp in sends:
            cp.start()

    def wait(self, *refs):
        local, sends, recvs = self._copies(*refs, with_recvs=True)
        for cp in recvs:
            cp.wait_recv()
        for cp in sends:
            cp.wait_send()
        local.wait()


def _collective(x, axes, mode, name):
    coll = _Coll(x, axes, mode)

    def body(*refs):
        coll.start(*refs)
        coll.wait(*refs)

    return pl.pallas_call(
        body, in_specs=[pl.BlockSpec(memory_space=pl.ANY)], out_specs=pl.BlockSpec(memory_space=pl.ANY),
        out_shape=coll.out_shape, scratch_shapes=coll.scratch, name=name,
    )(x)


def all_gather(x, axes, name):
    return _collective(x, axes, "gather", name)


def exchange(x, axes, name):
    return _collective(x, axes, "exchange", name)


def sum_slots(x, name, out_dtype=f32):
    P, R, C = x.shape
    tr = _pick(R, 256, 2 * SUBLANE)

    def body(x_ref, o_ref):
        acc = x_ref[0].astype(f32)
        for j in range(1, P):
            acc = acc + x_ref[j].astype(f32)
        o_ref[...] = acc.astype(o_ref.dtype)

    return pl.pallas_call(
        body, grid=(R // tr,), in_specs=[pl.BlockSpec((P, tr, C), lambda i: (0, i, 0))],
        out_specs=pl.BlockSpec((tr, C), lambda i: (i, 0)), out_shape=jax.ShapeDtypeStruct((R, C), out_dtype),
        compiler_params=_cp("parallel"), name=name,
    )(x)


def _pack(arrs, width, row_mult, dtype):
    flat = jnp.concatenate([a.astype(dtype).reshape(-1) for a in arrs])
    unit = width * row_mult
    n = -(-flat.shape[0] // unit) * unit
    return jnp.pad(flat, (0, n - flat.shape[0])).reshape(n // width, width)


def _unpack(flat, shapes):
    flat = flat.reshape(-1)
    out, off = [], 0
    for s in shapes:
        n = int(np.prod(s))
        out.append(flat[off:off + n].reshape(s))
        off += n
    return out


def ada_fwd(c_all, ada_w):
    def body(c_ref, w_ref, o_ref):
        cond = _silu(c_ref[...])
        for l in range(ada_w.shape[0]):
            o_ref[l] = _dot(cond, w_ref[l], precision=HI)

    return pl.pallas_call(body, out_shape=jax.ShapeDtypeStruct((ada_w.shape[0], c_all.shape[0], ada_w.shape[2]), f32),
                          compiler_params=pltpu.CompilerParams(vmem_limit_bytes=VMEM_LIMIT), name="ada_fwd")(c_all, ada_w)


def ada_bwd(c_all, dmod):
    def body(c_ref, d_ref, o_ref):
        cond = _silu(c_ref[...])
        for l in range(dmod.shape[0]):
            o_ref[l] = _dot(cond, d_ref[l], TN, precision=HI)

    return pl.pallas_call(body, out_shape=jax.ShapeDtypeStruct((dmod.shape[0], c_all.shape[1], dmod.shape[2]), f32),
                          compiler_params=pltpu.CompilerParams(vmem_limit_bytes=VMEM_LIMIT), name="ada_bwd")(c_all, dmod)


def loss_fwd_bwd(y, target):
    L, D = y.shape

    def fn(i, nb, yt, tt):
        e = yt - tt
        return e * (1.0 / D), jnp.sum(jnp.sum(e * e, axis=1, keepdims=True), axis=0, keepdims=True)

    return rowwise(fn, name="loss", L=L, tm=_pick(L, 512, SUBLANE), rows=[(y, 0, D, "cur"), (target, 0, D, "cur")],
                   outs=[(D, f32)], sums=[(1, 1)])


def adamw(w, g, m, v, name):
    R, C = w.shape

    def fn(i, nb, wt, gt, mt, vt):
        m2 = ADAM_B1 * mt + (1.0 - ADAM_B1) * gt
        v2 = ADAM_B2 * vt + (1.0 - ADAM_B2) * (gt * gt)
        m_hat = m2 / (1.0 - ADAM_B1 ** ADAM_STEP)
        v_hat = v2 / (1.0 - ADAM_B2 ** ADAM_STEP)
        delta = -ADAM_LR * (m_hat / (jnp.sqrt(v_hat) + ADAM_EPS) + ADAM_WD * wt)
        return delta, m2, v2

    return rowwise(fn, name=name, L=R, tm=_pick(R, 256, SUBLANE), rows=[(a, 0, C, "cur") for a in (w, g, m, v)],
                   outs=[(C, f32)] * 3)


W_NAMES = ["ada_w", "ada_b", "norm_mix", "norm_ffn", "attn_w_in", "attn_q_norm_a", "attn_k_norm_a", "attn_q_norm_b",
           "attn_k_norm_b", "attn_sinks", "attn_w_out", "rec_w_in", "s5_lambda_re", "s5_lambda_im", "s5_log_dt",
           "s5_b_re", "s5_b_im", "s5_c_re", "s5_c_im", "s5_d", "s5_glu_w", "s5_glu_b", "dn_conv", "dn_a_log",
           "dn_dt_bias", "dn_out_norm", "rec_w_out", "ffn_w_up", "ffn_conv", "ffn_w_down"]
BIG = ["attn_w_in", "attn_w_out", "rec_w_in", "rec_w_out", "ffn_w_up", "ffn_w_down"]
SMALL_SHARDED = ["s5_d", "s5_glu_w", "s5_glu_b", "dn_conv", "ffn_conv"]
SMALL_REPL = [n for n in W_NAMES if n not in BIG and n not in SMALL_SHARDED and n != "ada_w"]
NSH = 4
GRAD_WIRE = (bf16,)


SHARD_AXIS = {"attn_w_in": 2, "attn_w_out": 1, "rec_w_in": 2, "rec_w_out": 1, "ffn_w_up": 2, "ffn_w_down": 1,
              "s5_d": 1, "s5_glu_w": 1, "s5_glu_b": 1, "dn_conv": 2, "ffn_conv": 2}


def _unshard(g, name):
    ax = SHARD_AXIS[name.rstrip("01")]
    g = jnp.moveaxis(g, 0, ax)
    s = g.shape
    return g.reshape(s[:ax] + (s[ax] * s[ax + 1],) + s[ax + 2:])


def _to_shards(full, name):
    ax = SHARD_AXIS[name.rstrip("01")]
    s = full.shape
    g = full.reshape(s[:ax] + (NSH, s[ax] // NSH) + s[ax + 1:])
    return jnp.moveaxis(g, ax, 0)


def _rec_pad_cols(w):
    z6 = jnp.zeros(w.shape[:-1] + (122,), w.dtype)
    return jnp.concatenate([w[..., 256:3328], w[..., 0:256], w[..., 3328:3334], z6, w[..., 3334:3340], z6], axis=-1)


def _rec_unpad_cols(g):
    return jnp.concatenate([g[..., 3072:3328], g[..., 0:3072], g[..., 3328:3334], g[..., 3456:3462]], axis=-1)


def _ffn_fwd(x1, nf, sc, sh, gate, w_up, conv, w_dn, tag, rides=()):
    rides = list(rides) + [None, None]
    h2 = modulate_fwd(x1, nf, sc, sh, f"{tag}_mod2_fwd")
    up = mm(h2, w_up, name=f"{tag}_ffn_up", out_dtypes=(bf16,), ride=rides[0])
    up, got0 = up if rides[0] else (up, None)
    act = ffn_act_fwd(up, conv, f"{tag}_ffn_act_fwd")
    res = mm(act, w_dn, name=f"{tag}_ffn_down", out_dtypes=(f32, f32), epi=_resid_epi, epi_mn=[x1], epi_n=[gate],
             ride=rides[1])
    return res[1], (h2, up, act, res[0]), (got0, res[2] if rides[1] else None)


def _ffn_bwd(dx, x1, nf, sc, sh, gate, w_up, conv, w_dn, saved, tag, rides=()):
    rides = list(rides) + [None, None, None]
    take = lambda res, r: res if r else (res, None)
    h2, up, act, f = saved
    df, dgate = resid_bwd(dx, f, gate, f"{tag}_res2_bwd")
    dact = mm(df, w_dn, tb=True, name=f"{tag}_ffn_dact", out_dtypes=(bf16,))
    dw_dn, got0 = take(mm(act, df, ta=True, name=f"{tag}_ffn_dwdown", out_dtypes=GRAD_WIRE, ride=rides[0]), rides[0])
    dup, dconv = ffn_act_conv_bwd(up, conv, dact, f"{tag}_ffn_act_conv_bwd")
    dw_up, got1 = take(mm(h2, dup, ta=True, name=f"{tag}_ffn_dwup", out_dtypes=GRAD_WIRE, ride=rides[1]), rides[1])
    if callable(rides[2]):
        rides[2] = rides[2](dw_dn)
    dh2, got2 = take(mm(dup, w_up, tb=True, name=f"{tag}_ffn_dh", ride=rides[2]), rides[2])
    dx, dnf, dsc, dsh = modulate_bwd(x1, nf, sc, sh, dh2, dx, f"{tag}_mod2_bwd")
    grads = dict(nf=dnf, sc=dsc, sh=dsh, gate=dgate, w_up=dw_up, conv=dconv[:FFN_CONV], w_dn=dw_dn)
    return dx, grads, (got0, got1, got2)


def kernel(x, c, ada_w, ada_b, norm_mix, norm_ffn, attn_w_in, attn_q_norm_a, attn_k_norm_a, attn_q_norm_b, attn_k_norm_b, attn_sinks, attn_w_out, rec_w_in, s5_lambda_re, s5_lambda_im, s5_log_dt, s5_b_re, s5_b_im, s5_c_re, s5_c_im, s5_d, s5_glu_w, s5_glu_b, dn_conv, dn_a_log, dn_dt_bias, dn_out_norm, rec_w_out, ffn_w_up, ffn_conv, ffn_w_down, loss_target, m_ada_w, m_ada_b, m_norm_mix, m_norm_ffn, m_attn_w_in, m_attn_q_norm_a, m_attn_k_norm_a, m_attn_q_norm_b, m_attn_k_norm_b, m_attn_sinks, m_attn_w_out, m_rec_w_in, m_s5_lambda_re, m_s5_lambda_im, m_s5_log_dt, m_s5_b_re, m_s5_b_im, m_s5_c_re, m_s5_c_im, m_s5_d, m_s5_glu_w, m_s5_glu_b, m_dn_conv, m_dn_a_log, m_dn_dt_bias, m_dn_out_norm, m_rec_w_out, m_ffn_w_up, m_ffn_conv, m_ffn_w_down, v_ada_w, v_ada_b, v_norm_mix, v_norm_ffn, v_attn_w_in, v_attn_q_norm_a, v_attn_k_norm_a, v_attn_q_norm_b, v_attn_k_norm_b, v_attn_sinks, v_attn_w_out, v_rec_w_in, v_s5_lambda_re, v_s5_lambda_im, v_s5_log_dt, v_s5_b_re, v_s5_b_im, v_s5_c_re, v_s5_c_im, v_s5_d, v_s5_glu_w, v_s5_glu_b, v_dn_conv, v_dn_a_log, v_dn_dt_bias, v_dn_out_norm, v_rec_w_out, v_ffn_w_up, v_ffn_conv, v_ffn_w_down):
    args = (ada_w, ada_b, norm_mix, norm_ffn, attn_w_in, attn_q_norm_a, attn_k_norm_a, attn_q_norm_b, attn_k_norm_b, attn_sinks, attn_w_out, rec_w_in, s5_lambda_re, s5_lambda_im, s5_log_dt, s5_b_re, s5_b_im, s5_c_re, s5_c_im, s5_d, s5_glu_w, s5_glu_b, dn_conv, dn_a_log, dn_dt_bias, dn_out_norm, rec_w_out, ffn_w_up, ffn_conv, ffn_w_down)
    ms = (m_ada_w, m_ada_b, m_norm_mix, m_norm_ffn, m_attn_w_in, m_attn_q_norm_a, m_attn_k_norm_a, m_attn_q_norm_b, m_attn_k_norm_b, m_attn_sinks, m_attn_w_out, m_rec_w_in, m_s5_lambda_re, m_s5_lambda_im, m_s5_log_dt, m_s5_b_re, m_s5_b_im, m_s5_c_re, m_s5_c_im, m_s5_d, m_s5_glu_w, m_s5_glu_b, m_dn_conv, m_dn_a_log, m_dn_dt_bias, m_dn_out_norm, m_rec_w_out, m_ffn_w_up, m_ffn_conv, m_ffn_w_down)
    vs = (v_ada_w, v_ada_b, v_norm_mix, v_norm_ffn, v_attn_w_in, v_attn_q_norm_a, v_attn_k_norm_a, v_attn_q_norm_b, v_attn_k_norm_b, v_attn_sinks, v_attn_w_out, v_rec_w_in, v_s5_lambda_re, v_s5_lambda_im, v_s5_log_dt, v_s5_b_re, v_s5_b_im, v_s5_c_re, v_s5_c_im, v_s5_d, v_s5_glu_w, v_s5_glu_b, v_dn_conv, v_dn_a_log, v_dn_dt_bias, v_dn_out_norm, v_rec_w_out, v_ffn_w_up, v_ffn_conv, v_ffn_w_down)
    W = dict(zip(W_NAMES, args))
    Mo = dict(zip(W_NAMES, ms))
    Vo = dict(zip(W_NAMES, vs))
    xi, yi, ci = lax.axis_index("x"), lax.axis_index("y"), lax.axis_index("c")
    shard = 2 * xi + yi
    me8 = 4 * xi + 2 * yi + ci
    xs = x[0]
    target = loss_target[0]
    L, D = xs.shape

    XY = ("x", "y")
    wparts = [
        [("attn_w_in", attn_w_in), ("attn_w_out", attn_w_out)],
        [("rec_w_in", rec_w_in), ("rec_w_out", rec_w_out)],
        [("ffn_w_up1", ffn_w_up[1:2]), ("ffn_w_down1", ffn_w_down[1:2])],
        [("ffn_w_up0", ffn_w_up[0:1]), ("ffn_w_down0", ffn_w_down[0:1])],
    ]
    wpack = [_pack([a for _, a in p], 1024, 16, bf16) for p in wparts]
    Wf = {}

    def unpack_weights(gathered, part):
        flat = gathered.reshape(NSH, -1)
        off = 0
        for n, a in part:
            sz = int(np.prod(a.shape))
            Wf[n] = _unshard(flat[:, off:off + sz].reshape((NSH,) + a.shape), n)[0]
            off += sz

    unpack_weights(all_gather(wpack[0], XY, "gather_w0"), wparts[0])

    sflat = _pack([c] + [W[n] for n in SMALL_SHARDED], 1024, 8, f32)
    s8 = all_gather(sflat, AXES, "gather_small")
    s8f = s8.reshape(8, -1)
    c_all = s8f[:, :D]
    Ws = {}
    off = D
    for n in SMALL_SHARDED:
        sz = int(np.prod(W[n].shape))
        Ws[n] = _unshard(s8f[0::2, off:off + sz].reshape((NSH,) + W[n].shape), n)
        off += sz

    modp = ada_fwd(c_all, ada_w)
    modg = all_gather(modp, ("x", "y"), "gather_mod")
    mod_all = jnp.moveaxis(modg, 0, 2).reshape(2, 8, -1) + ada_b[:, None, :]
    mod = lax.dynamic_slice(mod_all, (0, me8, 0), (2, 1, mod_all.shape[2]))[:, 0, :]
    mods = [[mod[l:l + 1, j * D:(j + 1) * D] for j in range(6)] for l in range(2)]

    sh1, sc1, g1, sh2, sc2, g2_ = mods[0]
    nm0, nf0 = norm_mix[0:1], norm_ffn[0:1]
    sinkb = jnp.repeat(attn_sinks[0], HEAD_DIM)[None]
    h0 = modulate_fwd(xs, nm0, sc1, sh1, "l0_mod1_fwd")
    hin0 = mm(h0, Wf["attn_w_in"], name="l0_in_proj")
    ocat, att_saved, got = attention_fwd(hin0, attn_q_norm_a, attn_k_norm_a, attn_q_norm_b, attn_k_norm_b, sinkb,
                                         rides={1: (wpack[3], XY, "gather"), 4: (wpack[2], XY, "gather")})
    unpack_weights(got[1], wparts[3])
    unpack_weights(got[4], wparts[2])
    y0, x1 = mm(ocat, Wf["attn_w_out"], name="l0_out_proj", out_dtypes=(f32, f32), epi=_resid_epi,
                epi_mn=[xs], epi_n=[g1])
    x2, ffn0_saved, got = _ffn_fwd(x1, nf0, sc2, sh2, g2_, Wf["ffn_w_up0"], Ws["ffn_conv"][0], Wf["ffn_w_down0"], "l0",
                                   rides=[(wpack[1], XY, "gather")])
    unpack_weights(got[0], wparts[1])
    rec_w_in_p = _rec_pad_cols(Wf["rec_w_in"])

    th1, tc1, t1, th2, tc2, t2 = mods[1]
    nm1, nf1 = norm_mix[1:2], norm_ffn[1:2]
    pad128 = lambda a: jnp.pad(a, ((0, 0), (0, 128 - a.shape[1])))
    s5p = dict(lr=s5_lambda_re[0], li=s5_lambda_im[0], ldt=s5_log_dt[0][:, None], b_re=s5_b_re[0], b_im=s5_b_im[0],
               c_re=s5_c_re[0], c_im=s5_c_im[0], d=Ws["s5_d"], gw=Ws["s5_glu_w"][0], gb=Ws["s5_glu_b"])
    dnp = dict(conv=Ws["dn_conv"][0], alog=pad128(dn_a_log), dtb=pad128(dn_dt_bias), onorm=dn_out_norm)
    h1 = modulate_fwd(x2, nm1, tc1, th1, "l1_mod1_fwd")
    hin1 = mm(h1, rec_w_in_p, name="l1_in_proj")
    yc, s5_saved = s5_fwd(hin1, s5p)
    ycat, dn_saved = deltanet_fwd(hin1, dnp, yc)
    y1, x3 = mm(ycat, Wf["rec_w_out"], name="l1_out_proj", out_dtypes=(f32, f32), epi=_resid_epi,
                epi_mn=[x2], epi_n=[t1])
    x4, ffn1_saved, _ = _ffn_fwd(x3, nf1, tc2, th2, t2, Wf["ffn_w_up1"], Ws["ffn_conv"][1], Wf["ffn_w_down1"], "l1")

    dx, sse = loss_fwd_bwd(x4, target)
    loss = lax.psum(0.5 * sse[0, 0] / D, AXES)

    dx, gf1, _ = _ffn_bwd(dx, x3, nf1, tc2, th2, t2, Wf["ffn_w_up1"], Ws["ffn_conv"][1], Wf["ffn_w_down1"], ffn1_saved, "l1")
    dy1, dt1 = resid_bwd(dx, y1, t1, "l1_res1_bwd")
    dycat = mm(dy1, Wf["rec_w_out"], tb=True, name="l1_dycat")
    dw_rec_out = mm(ycat, dy1, ta=True, name="l1_dwout", out_dtypes=GRAD_WIRE)
    du_skip, du_b, s5g = s5_bwd(hin1, s5p, s5_saved, dycat)
    dx_qkv, dz, dab, dng = deltanet_bwd(hin1, dnp, dn_saved, dycat)
    dhin1 = rec_assemble(dx_qkv, dz, du_skip, du_b, dab)
    dw_rec_in = _rec_unpad_cols(mm(h1, dhin1, ta=True, name="l1_dwin", out_dtypes=GRAD_WIRE))
    dh1 = mm(dhin1, rec_w_in_p, tb=True, name="l1_dh")
    dx, dnm1, dtc1, dth1 = modulate_bwd(x2, nm1, tc1, th1, dh1, dx, "l1_mod1_bwd")

    def grad_part(items):
        flat = jnp.concatenate([_to_shards(g, n).reshape(NSH, -1) for n, g in items], axis=1)
        unit = 256 * 1024
        npad = -(-flat.shape[1] // unit) * unit
        return jnp.pad(flat, ((0, 0), (0, npad - flat.shape[1]))).reshape(NSH, npad // 1024, 1024)

    w_dn1 = gf1["w_dn"][None]
    part2 = lambda dw_dn0: (grad_part([("ffn_w_down1", w_dn1), ("ffn_w_down0", dw_dn0[None])]), XY, "exchange")
    gparts = [[("rec_w_in", dw_rec_in[None]), ("rec_w_out", dw_rec_out[None])], [("ffn_w_up1", gf1["w_up"][None])]]
    dx, gf0, gq = _ffn_bwd(dx, x1, nf0, sc2, sh2, g2_, Wf["ffn_w_up0"], Ws["ffn_conv"][0], Wf["ffn_w_down0"], ffn0_saved,
                           "l0", rides=[(grad_part(gparts[0]), XY, "exchange"), (grad_part(gparts[1]), XY, "exchange"), part2])
    gparts.append([("ffn_w_down1", w_dn1), ("ffn_w_down0", gf0["w_dn"][None])])
    dy0, dg1 = resid_bwd(dx, y0, g1, "l0_res1_bwd")
    dcat = mm(dy0, Wf["attn_w_out"], tb=True, name="l0_dcat")
    dw_attn_out = mm(ocat, dy0, ta=True, name="l0_dwout", out_dtypes=GRAD_WIRE)
    gparts += [[("ffn_w_up0", gf0["w_up"][None])], [("attn_w_out", dw_attn_out[None])]]
    chip_sum = lambda qs, i0: jnp.concatenate([sum_slots(q, f"sum_chips{i0 + i}", bf16) for i, q in enumerate(qs)], axis=0)
    dhin0, dwqa, dwka, dwqb, dwkb, dsinkb, gots = attention_bwd(
        hin0, attn_q_norm_a, attn_k_norm_a, attn_q_norm_b, attn_k_norm_b, sinkb, att_saved, dcat,
        rides={"a": (grad_part(gparts[3]), XY, "exchange"), 1: (chip_sum(gq, 0), ("c",), "gather"),
               4: (grad_part(gparts[4]), XY, "exchange"), 16: lambda g: (chip_sum([g["a"]], 3), ("c",), "gather")})
    dw_attn_in = mm(h0, dhin0, ta=True, name="l0_dwin", out_dtypes=GRAD_WIRE)
    gparts.append([("attn_w_in", dw_attn_in[None])])
    dh0, gq5 = mm(dhin0, Wf["attn_w_in"], tb=True, name="l0_dh", ride=(grad_part(gparts[5]), XY, "exchange"))
    grad_x, dnm0, dsc1, dsh1 = modulate_bwd(xs, nm0, sc1, sh1, dh0, dx, "l0_mod1_bwd")

    dmod = jnp.concatenate([
        jnp.concatenate([dsh1, dsc1, dg1, gf0["sh"], gf0["sc"], gf0["gate"]], axis=1),
        jnp.concatenate([dth1, dtc1, dt1, gf1["sh"], gf1["sc"], gf1["gate"]], axis=1)], axis=0)
    gl = {
        "ada_b": dmod,
        "norm_mix": jnp.concatenate([dnm0, dnm1], axis=0),
        "norm_ffn": jnp.concatenate([gf0["nf"], gf1["nf"]], axis=0),
        "attn_q_norm_a": dwqa, "attn_k_norm_a": dwka, "attn_q_norm_b": dwqb, "attn_k_norm_b": dwkb,
        "attn_sinks": dsinkb[:, ::HEAD_DIM],
        "s5_lambda_re": s5g["lr"][None], "s5_lambda_im": s5g["li"][None], "s5_log_dt": s5g["ldt"][:, 0][None],
        "s5_b_re": s5g["b_re"][None], "s5_b_im": s5g["b_im"][None], "s5_c_re": s5g["c_re"][None],
        "s5_c_im": s5g["c_im"][None],
        "dn_a_log": dng["alog"][:, :DN_HEADS], "dn_dt_bias": dng["dtb"][:, :DN_HEADS], "dn_out_norm": dng["onorm"],
        "s5_d": s5g["d"], "s5_glu_w": s5g["gw"][None], "s5_glu_b": s5g["gb"], "dn_conv": dng["conv"][None],
        "ffn_conv": jnp.stack([gf0["conv"], gf1["conv"]]),
    }

    small_names = SMALL_REPL + SMALL_SHARDED
    gs = _pack([gl[n] for n in small_names], 128, 256, f32)
    gs8 = all_gather(gs, AXES, "gather_small_grads")
    gsum = sum_slots(gs8, "sum_small_grads")
    full_shapes = [gl[n].shape for n in small_names]
    gfull = dict(zip(small_names, _unpack(gsum, full_shapes)))
    dmod_all = gs8.reshape(8, -1)[:, :2 * 6 * D].reshape(8, 2, 6 * D)
    ncol = ada_w.shape[2]
    dmod_sh = jnp.moveaxis(lax.dynamic_slice(dmod_all, (0, 0, shard * ncol), (8, 2, ncol)), 0, 1)
    grads = {"ada_w": ada_bwd(c_all, dmod_sh)}
    for n in SMALL_REPL:
        grads[n] = gfull[n]
    for n in SMALL_SHARDED:
        sh_all = _to_shards(gfull[n], n)
        grads[n] = lax.dynamic_slice(sh_all, (shard,) + (0,) * (sh_all.ndim - 1), (1,) + sh_all.shape[1:])[0]

    gq = list(gq) + [gots["a"], gots[4], gq5]
    gc45 = all_gather(chip_sum(gq[4:], 4), ("c",), "gather_grad_c")
    gsh = jnp.concatenate([sum_slots(gots[1], "sum_pair012"), sum_slots(gots[16], "sum_pair3"),
                           sum_slots(gc45, "sum_pair45")], axis=0)
    row, got = 0, {}
    for part, q in zip(gparts, gq):
        flat = gsh[row:row + q.shape[1]].reshape(-1)
        row += q.shape[1]
        off = 0
        for n, g in part:
            sz = g.size // NSH
            got[n] = flat[off:off + sz].reshape((1,) + g.shape[1:-2] + _to_shards(g, n).shape[-2:])
            off += sz
    for n in ("attn_w_in", "attn_w_out", "rec_w_in", "rec_w_out"):
        grads[n] = got[n]
    grads["ffn_w_up"] = jnp.concatenate([got["ffn_w_up0"], got["ffn_w_up1"]], axis=0)
    grads["ffn_w_down"] = jnp.concatenate([got["ffn_w_down0"], got["ffn_w_down1"]], axis=0)

    delta, new_m, new_v = {}, {}, {}

    def as2d(a):
        return a.reshape(-1, a.shape[-1])

    for n in ["ada_w"] + BIG:
        d_, m_, v_ = adamw(as2d(W[n]), as2d(grads[n]), as2d(Mo[n]), as2d(Vo[n]), f"adamw_{n}")
        delta[n], new_m[n], new_v[n] = d_.reshape(W[n].shape), m_.reshape(W[n].shape), v_.reshape(W[n].shape)
    pk = lambda dd: _pack([dd[n] for n in small_names], 128, 256, f32)
    d_, m_, v_ = adamw(pk(W), pk(grads), pk(Mo), pk(Vo), "adamw_small")
    shp = [W[n].shape for n in small_names]
    for dst, src in ((delta, d_), (new_m, m_), (new_v, v_)):
        dst.update(zip(small_names, _unpack(src, shp)))

    return (loss, grad_x[None], *[grads[n] for n in W_NAMES], *[delta[n] for n in W_NAMES],
            *[new_m[n] for n in W_NAMES], *[new_v[n] for n in W_NAMES])
```
